```python
import math
import jax, jax.numpy as jnp
from jax import lax
import numpy as np

D_MODEL = 1024
BATCH = 8
SEQ = 4096
DEPTH = 2

RMS_EPS = 1e-6
ROPE_THETA = 10000.0
ATTN_Q_BLOCK = 128

MLA_HEADS = 4
MLA_NOPE_DIM = 128
MLA_ROPE_DIM = 64
MLA_V_DIM = 128
MLA_QK_DIM = MLA_NOPE_DIM + MLA_ROPE_DIM
MLA_Q_RANK = 384
MLA_KV_RANK = 256
MLA_WIDTH = MLA_HEADS * MLA_V_DIM

DN_HEADS = 4
DN_HEAD_DIM = 128
DN_WIDTH = DN_HEADS * DN_HEAD_DIM
DN_CONV = 4
DN_CHUNK = 64

DIL_WINDOWS = (128, 512, 2048)
DIL_DILATIONS = (1, 4, 16)
DIL_GROUPS = 3
DIL_HEADS_PER_GROUP = 4
DIL_HEAD_DIM = 128
DIL_QKV_WIDTH = DIL_GROUPS * DIL_HEADS_PER_GROUP * DIL_HEAD_DIM
DIL_WIDTH = DIL_HEADS_PER_GROUP * DIL_HEAD_DIM
DIL_BLOCK = 128

N_BRANCHES = 3
BRANCH_WIDTH = 512
IN_SPLITS = (MLA_Q_RANK, MLA_KV_RANK + MLA_ROPE_DIM, MLA_WIDTH,
             3 * DN_WIDTH, DN_HEADS, DN_HEADS, DN_WIDTH,
             3 * DIL_QKV_WIDTH, DIL_WIDTH,
             N_BRANCHES * D_MODEL)
IN_WIDTH = (MLA_Q_RANK + MLA_KV_RANK + MLA_ROPE_DIM + MLA_WIDTH
            + 3 * DN_WIDTH + 2 * DN_HEADS + DN_WIDTH
            + 3 * DIL_QKV_WIDTH + DIL_WIDTH + N_BRANCHES * D_MODEL)

kernel_name = "hybrid_mla_gdn_dilated_gated_merge"


def rms_norm(x, g):
    xf = x.astype(jnp.float32)
    y = xf * lax.rsqrt(jnp.mean(xf * xf, axis=-1, keepdims=True) + RMS_EPS)
    return (y * g.astype(jnp.float32)).astype(x.dtype)


def l2_normalize(x):
    return x * lax.rsqrt(jnp.sum(x * x, axis=-1, keepdims=True) + 1e-6)


def rope_tables(positions, dim):
    inv_freq = 1.0 / (ROPE_THETA ** (jnp.arange(0, dim, 2, dtype=jnp.float32) / dim))
    ang = positions.astype(jnp.float32)[..., None] * inv_freq
    return jnp.cos(ang), jnp.sin(ang)


def apply_rope(x, cos, sin):
    half = x.shape[-1] // 2
    xf = x.astype(jnp.float32)
    x1, x2 = xf[..., :half], xf[..., half:]
    c, s = cos[:, :, None, :], sin[:, :, None, :]
    return jnp.concatenate([x1 * c - x2 * s, x2 * c + x1 * s], axis=-1).astype(x.dtype)


def split_last(t, sizes):
    cuts, acc = [], 0
    for size in sizes[:-1]:
        acc += size
        cuts.append(acc)
    return jnp.split(t, cuts, axis=-1)


def causal_depthwise_conv(x, w):
    k = w.shape[0]
    return lax.conv_general_dilated(
        x, w[:, None, :].astype(x.dtype), window_strides=(1,), padding=[(k - 1, 0)],
        dimension_numbers=('NWC', 'WIO', 'NWC'), feature_group_count=x.shape[-1])


def causal_block_attention(q, k, v, scale):
    B, S, H, dk = q.shape
    dv = v.shape[-1]
    nb = S // ATTN_Q_BLOCK
    q_blocks = jnp.moveaxis(q.reshape(B, nb, ATTN_Q_BLOCK, H, dk), 1, 0)
    k_pos = jnp.arange(S)

    def one_block(args):
        q_blk, blk = args
        s = jnp.einsum('bqhd,bkhd->bhqk', q_blk, k, preferred_element_type=jnp.float32) * scale
        q_pos = blk * ATTN_Q_BLOCK + jnp.arange(ATTN_Q_BLOCK)
        s = jnp.where(k_pos[None, :] <= q_pos[:, None], s, -jnp.inf)
        p = jax.nn.softmax(s, axis=-1)
        return jnp.einsum('bhqk,bkhd->bqhd', p.astype(v.dtype), v)

    o = lax.map(one_block, (q_blocks, jnp.arange(nb)))
    return jnp.moveaxis(o, 0, 1).reshape(B, S, H, dv)


def mla_branch(q_lat, kv_lat, cos, sin, q_a_norm_g, w_q_b, kv_a_norm_g, w_kv_b, q_norm_g, k_norm_g):
    B, S, _ = q_lat.shape
    q = jnp.einsum('bsr,re->bse', rms_norm(q_lat, q_a_norm_g), w_q_b).reshape(B, S, MLA_HEADS, MLA_QK_DIM)
    c_kv, k_pe = kv_lat[..., :MLA_KV_RANK], kv_lat[..., MLA_KV_RANK:]
    kv = jnp.einsum('bsr,re->bse', rms_norm(c_kv, kv_a_norm_g), w_kv_b).reshape(B, S, MLA_HEADS, MLA_NOPE_DIM + MLA_V_DIM)
    k_nope, v = kv[..., :MLA_NOPE_DIM], kv[..., MLA_NOPE_DIM:]
    q_nope = rms_norm(q[..., :MLA_NOPE_DIM], q_norm_g[:MLA_NOPE_DIM])
    q_pe = apply_rope(rms_norm(q[..., MLA_NOPE_DIM:], q_norm_g[MLA_NOPE_DIM:]), cos, sin)
    k_nope = rms_norm(k_nope, k_norm_g[:MLA_NOPE_DIM])
    k_pe = apply_rope(rms_norm(k_pe, k_norm_g[MLA_NOPE_DIM:])[:, :, None, :], cos, sin)
    q_full = jnp.concatenate([q_nope, q_pe], axis=-1)
    k_full = jnp.concatenate([k_nope, jnp.broadcast_to(k_pe, (B, S, MLA_HEADS, MLA_ROPE_DIM))], axis=-1)
    o = causal_block_attention(q_full, k_full, v, MLA_QK_DIM ** -0.5)
    return o.reshape(B, S, MLA_WIDTH)


def chunk_gated_delta_rule(q, k, v, g, beta):
    B, S, H, dk = q.shape
    dv = v.shape[-1]
    C = DN_CHUNK
    N = S // C

    def chunks(t):
        t = jnp.moveaxis(t, 2, 1)
        return t.reshape(B, H, N, C, *t.shape[3:])

    q, k, v, g, beta = (chunks(t) for t in (q, k, v, g, beta))
    gc = jnp.cumsum(g, axis=-1)
    idx = jnp.arange(C)
    incl = idx[:, None] >= idx[None, :]
    strict = idx[:, None] > idx[None, :]
    decay = jnp.exp(jnp.where(incl, gc[..., :, None] - gc[..., None, :], -jnp.inf))
    k_beta = k * beta[..., None]
    lower = jnp.where(strict, jnp.einsum('bhncd,bhnmd->bhncm', k_beta, k) * decay, 0.0)
    rhs = jnp.concatenate([v * beta[..., None], k_beta * jnp.exp(gc)[..., None]], axis=-1)
    sol = lax.linalg.triangular_solve(lower + jnp.eye(C, dtype=lower.dtype), rhs, left_side=True, lower=True)
    u, w = sol[..., :dv], sol[..., dv:]
    qk = jnp.einsum('bhncd,bhnmd->bhncm', q, k) * decay

    def step(state, inp):
        q_i, k_i, u_i, w_i, gc_i, qk_i = inp
        v_new = u_i - jnp.einsum('bhcd,bhde->bhce', w_i, state)
        o_i = (jnp.einsum('bhcd,bhde->bhce', q_i * jnp.exp(gc_i)[..., None], state)
               + jnp.einsum('bhcm,bhme->bhce', qk_i, v_new))
        g_last = gc_i[..., -1:]
        state = (state * jnp.exp(g_last)[..., None]
                 + jnp.einsum('bhcd,bhce->bhde', k_i * jnp.exp(g_last - gc_i)[..., None], v_new))
        return state, o_i

    xs = tuple(jnp.moveaxis(t, 2, 0) for t in (q, k, u, w, gc, qk))
    state0 = jnp.zeros((B, H, dk, dv), jnp.float32)
    _, o = lax.scan(step, state0, xs)
    o = jnp.moveaxis(o, 0, 2).reshape(B, H, S, dv)
    return jnp.moveaxis(o, 1, 2)


def gated_deltanet_branch(qkv, a, b, conv_w, a_log, dt_bias, out_norm_g):
    B, S, _ = qkv.shape
    mixed = jax.nn.silu(causal_depthwise_conv(qkv, conv_w)).astype(jnp.float32)
    q, k, v = jnp.split(mixed, 3, axis=-1)
    q = l2_normalize(q.reshape(B, S, DN_HEADS, DN_HEAD_DIM)) * (DN_HEAD_DIM ** -0.5)
    k = l2_normalize(k.reshape(B, S, DN_HEADS, DN_HEAD_DIM))
    v = v.reshape(B, S, DN_HEADS, DN_HEAD_DIM)
    beta = jax.nn.sigmoid(b.astype(jnp.float32))
    g = -jnp.exp(a_log.astype(jnp.float32)) * jax.nn.softplus(a.astype(jnp.float32) + dt_bias.astype(jnp.float32))
    o = chunk_gated_delta_rule(q, k, v, g, beta)
    return rms_norm(o, out_norm_g).astype(qkv.dtype).reshape(B, S, DN_WIDTH)


def dilated_window_attention(q, k, v, window, dilation):
    B, S, H, hd = q.shape
    reach = window // dilation
    L = S // dilation
    nb = -(-L // DIL_BLOCK)
    Lp = nb * DIL_BLOCK

    def to_blocks(t):
        t = jnp.swapaxes(t.reshape(B, L, dilation, H, hd), 1, 2)
        t = jnp.pad(t, ((0, 0), (0, 0), (0, Lp - L), (0, 0), (0, 0)))
        return t.reshape(B, dilation, nb, DIL_BLOCK, H, hd)

    def with_previous(t):
        prev = jnp.pad(t[:, :, :-1], ((0, 0), (0, 0), (1, 0), (0, 0), (0, 0), (0, 0)))
        return jnp.concatenate([prev, t], axis=3)

    qb = to_blocks(q)
    kb = with_previous(to_blocks(k))
    vb = with_previous(to_blocks(v))
    s = jnp.einsum('bgnqhd,bgnkhd->bgnhqk', qb, kb, preferred_element_type=jnp.float32) * (hd ** -0.5)
    qi = jnp.arange(DIL_BLOCK)[:, None]
    kc = jnp.arange(2 * DIL_BLOCK)[None, :]
    dist = DIL_BLOCK + qi - kc
    band = (dist >= 0) & (dist <= reach)
    has_prev = (jnp.arange(nb) > 0)[:, None, None] | (kc >= DIL_BLOCK)[None]
    valid = band[None] & has_prev
    s = jnp.where(valid[:, None], s, -jnp.inf)
    m = jnp.max(s, axis=-1, keepdims=True)
    e = jnp.exp(s - m)
    den = jnp.sum(e, axis=-1)
    o = jnp.einsum('bgnhqk,bgnkhd->bgnqhd', e, vb.astype(jnp.float32)) / jnp.swapaxes(den, -1, -2)[..., None]
    lse = jnp.swapaxes(m[..., 0] + jnp.log(den), -1, -2)

    def from_blocks(t):
        t = t.reshape(B, dilation, Lp, *t.shape[4:])[:, :, :L]
        return jnp.swapaxes(t, 1, 2).reshape(B, S, *t.shape[3:])

    return from_blocks(o), from_blocks(lse)


def dilated_branch(qkv, cos, sin, q_norm_g, k_norm_g):
    B, S, _ = qkv.shape
    n_heads = DIL_GROUPS * DIL_HEADS_PER_GROUP
    q, k, v = (t.reshape(B, S, n_heads, DIL_HEAD_DIM) for t in jnp.split(qkv, 3, axis=-1))
    q = apply_rope(rms_norm(q, q_norm_g), cos, sin)
    k = apply_rope(rms_norm(k, k_norm_g), cos, sin)
    grp = (B, S, DIL_GROUPS, DIL_HEADS_PER_GROUP, DIL_HEAD_DIM)
    q, k, v = q.reshape(grp), k.reshape(grp), v.reshape(grp)
    outs, lses = [], []
    for gi in range(DIL_GROUPS):
        o_g, lse_g = dilated_window_attention(q[:, :, gi], k[:, :, gi], v[:, :, gi], DIL_WINDOWS[gi], DIL_DILATIONS[gi])
        outs.append(o_g)
        lses.append(lse_g)
    wts = jax.nn.softmax(jnp.stack(lses, axis=0), axis=0)
    o = jnp.sum(wts[..., None] * jnp.stack(outs, axis=0), axis=0)
    return o.reshape(B, S, DIL_WIDTH).astype(qkv.dtype)


def hybrid_layer(x, cos_r, sin_r, cos_h, sin_h, norm_g, w_in, mla_q_a_norm_g, mla_w_q_b,
                 mla_kv_a_norm_g, mla_w_kv_b, mla_q_norm_g, mla_k_norm_g, dn_conv_w, dn_a_log,
                 dn_dt_bias, dn_out_norm_g, dil_q_norm_g, dil_k_norm_g, w_branch, w_out):
    B, S, _ = x.shape
    h = rms_norm(x, norm_g)
    proj = jnp.einsum('bsd,de->bse', h, w_in)
    (q_lat, kv_lat, z_a, dn_qkv, dn_a, dn_b, z_b, dil_qkv, z_c, gate_logits) = split_last(proj, IN_SPLITS)
    y_a = mla_branch(q_lat, kv_lat, cos_r, sin_r, mla_q_a_norm_g, mla_w_q_b, mla_kv_a_norm_g,
                     mla_w_kv_b, mla_q_norm_g, mla_k_norm_g)
    y_b = gated_deltanet_branch(dn_qkv, dn_a, dn_b, dn_conv_w, dn_a_log, dn_dt_bias, dn_out_norm_g)
    y_c = dilated_branch(dil_qkv, cos_h, sin_h, dil_q_norm_g, dil_k_norm_g)
    ys = jnp.stack([y_a * jax.nn.silu(z_a), y_b * jax.nn.silu(z_b), y_c * jax.nn.silu(z_c)], axis=2)
    branch_out = jnp.einsum('bsnc,ncd->bsnd', ys, w_branch)
    gates = jax.nn.sigmoid(gate_logits.reshape(B, S, N_BRANCHES, D_MODEL))
    mixed = jnp.sum(gates * branch_out, axis=2)
    return x + jnp.einsum('bsd,de->bse', mixed, w_out)


def _fwd_setup_inputs(seed: int = 0) -> dict:
    key = jax.random.key(seed)
    ks = jax.random.split(key, 24)
    f32 = jnp.float32

    def normal(k, shape, fan_in):
        return jax.random.normal(k, shape, f32) * (fan_in ** -0.5)

    def gain(k, shape):
        return 1.0 + 0.02 * jax.random.normal(k, shape, f32)

    x = jax.random.normal(ks[0], (BATCH, SEQ, D_MODEL), f32)
    offsets = jax.random.randint(ks[1], (BATCH, 1), 0, 1024, dtype=jnp.int32)
    positions = jnp.arange(SEQ, dtype=jnp.int32)[None, :] + offsets
    dt = jnp.exp(jax.random.uniform(ks[14], (DEPTH, DN_HEADS), f32, math.log(1e-3), math.log(1e-1)))
    return {
        "x": x,
        "positions": positions,
        "norm_g": gain(ks[2], (DEPTH, D_MODEL)),
        "w_in": normal(ks[3], (DEPTH, D_MODEL, IN_WIDTH), D_MODEL),
        "mla_q_a_norm_g": gain(ks[4], (DEPTH, MLA_Q_RANK)),
        "mla_w_q_b": normal(ks[5], (DEPTH, MLA_Q_RANK, MLA_HEADS * MLA_QK_DIM), MLA_Q_RANK),
        "mla_kv_a_norm_g": gain(ks[6], (DEPTH, MLA_KV_RANK)),
        "mla_w_kv_b": normal(ks[7], (DEPTH, MLA_KV_RANK, MLA_HEADS * (MLA_NOPE_DIM + MLA_V_DIM)), MLA_KV_RANK),
        "mla_q_norm_g": gain(ks[8], (DEPTH, MLA_QK_DIM)),
        "mla_k_norm_g": gain(ks[9], (DEPTH, MLA_QK_DIM)),
        "dn_conv_w": normal(ks[10], (DEPTH, DN_CONV, 3 * DN_WIDTH), DN_CONV),
        "dn_a_log": jnp.log(jax.random.uniform(ks[11], (DEPTH, DN_HEADS), f32, 1.0, 16.0)),
        "dn_dt_bias": dt + jnp.log(-jnp.expm1(-dt)),
        "dn_out_norm_g": gain(ks[12], (DEPTH, DN_HEAD_DIM)),
        "dil_q_norm_g": gain(ks[13], (DEPTH, DIL_HEAD_DIM)),
        "dil_k_norm_g": gain(ks[15], (DEPTH, DIL_HEAD_DIM)),
        "w_branch": normal(ks[16], (DEPTH, N_BRANCHES, BRANCH_WIDTH, D_MODEL), BRANCH_WIDTH),
        "w_out": normal(ks[17], (DEPTH, D_MODEL, D_MODEL), D_MODEL),
    }


def _fwd_reference(x, positions, norm_g, w_in, mla_q_a_norm_g, mla_w_q_b, mla_kv_a_norm_g, mla_w_kv_b,
              mla_q_norm_g, mla_k_norm_g, dn_conv_w, dn_a_log, dn_dt_bias, dn_out_norm_g,
              dil_q_norm_g, dil_k_norm_g, w_branch, w_out):
    cos_r, sin_r = rope_tables(positions, MLA_ROPE_DIM)
    cos_h, sin_h = rope_tables(positions, DIL_HEAD_DIM)
    for layer in range(DEPTH):
        x = hybrid_layer(x, cos_r, sin_r, cos_h, sin_h, norm_g[layer], w_in[layer],
                         mla_q_a_norm_g[layer], mla_w_q_b[layer], mla_kv_a_norm_g[layer],
                         mla_w_kv_b[layer], mla_q_norm_g[layer], mla_k_norm_g[layer],
                         dn_conv_w[layer], dn_a_log[layer], dn_dt_bias[layer], dn_out_norm_g[layer],
                         dil_q_norm_g[layer], dil_k_norm_g[layer], w_branch[layer], w_out[layer])
    return x


import jax as _jax
import jax.numpy as _jnp

TWIN_FORMAT = 'train_step'
FWD_PARAMS = ['x', 'positions', 'norm_g', 'w_in', 'mla_q_a_norm_g', 'mla_w_q_b', 'mla_kv_a_norm_g', 'mla_w_kv_b', 'mla_q_norm_g', 'mla_k_norm_g', 'dn_conv_w', 'dn_a_log', 'dn_dt_bias', 'dn_out_norm_g', 'dil_q_norm_g', 'dil_k_norm_g', 'w_branch', 'w_out']
TWIN_WEIGHTS = ['norm_g', 'w_in', 'mla_q_a_norm_g', 'mla_w_q_b', 'mla_kv_a_norm_g', 'mla_w_kv_b', 'mla_q_norm_g', 'mla_k_norm_g', 'dn_conv_w', 'dn_a_log', 'dn_dt_bias', 'dn_out_norm_g', 'dil_q_norm_g', 'dil_k_norm_g', 'w_branch', 'w_out']
TWIN_DIFF_INPUT = 'x'
TWIN_INPUTS = ['x', 'positions', 'norm_g', 'w_in', 'mla_q_a_norm_g', 'mla_w_q_b', 'mla_kv_a_norm_g', 'mla_w_kv_b', 'mla_q_norm_g', 'mla_k_norm_g', 'dn_conv_w', 'dn_a_log', 'dn_dt_bias', 'dn_out_norm_g', 'dil_q_norm_g', 'dil_k_norm_g', 'w_branch', 'w_out', 'loss_target', 'm_norm_g', 'm_w_in', 'm_mla_q_a_norm_g', 'm_mla_w_q_b', 'm_mla_kv_a_norm_g', 'm_mla_w_kv_b', 'm_mla_q_norm_g', 'm_mla_k_norm_g', 'm_dn_conv_w', 'm_dn_a_log', 'm_dn_dt_bias', 'm_dn_out_norm_g', 'm_dil_q_norm_g', 'm_dil_k_norm_g', 'm_w_branch', 'm_w_out', 'v_norm_g', 'v_w_in', 'v_mla_q_a_norm_g', 'v_mla_w_q_b', 'v_mla_kv_a_norm_g', 'v_mla_w_kv_b', 'v_mla_q_norm_g', 'v_mla_k_norm_g', 'v_dn_conv_w', 'v_dn_a_log', 'v_dn_dt_bias', 'v_dn_out_norm_g', 'v_dil_q_norm_g', 'v_dil_k_norm_g', 'v_w_branch', 'v_w_out']
TWIN_OUTPUTS = ['loss', 'grad_x', 'grad_norm_g', 'grad_w_in', 'grad_mla_q_a_norm_g', 'grad_mla_w_q_b', 'grad_mla_kv_a_norm_g', 'grad_mla_w_kv_b', 'grad_mla_q_norm_g', 'grad_mla_k_norm_g', 'grad_dn_conv_w', 'grad_dn_a_log', 'grad_dn_dt_bias', 'grad_dn_out_norm_g', 'grad_dil_q_norm_g', 'grad_dil_k_norm_g', 'grad_w_branch', 'grad_w_out', 'delta_norm_g', 'delta_w_in', 'delta_mla_q_a_norm_g', 'delta_mla_w_q_b', 'delta_mla_kv_a_norm_g', 'delta_mla_w_kv_b', 'delta_mla_q_norm_g', 'delta_mla_k_norm_g', 'delta_dn_conv_w', 'delta_dn_a_log', 'delta_dn_dt_bias', 'delta_dn_out_norm_g', 'delta_dil_q_norm_g', 'delta_dil_k_norm_g', 'delta_w_branch', 'delta_w_out', 'new_m_norm_g', 'new_m_w_in', 'new_m_mla_q_a_norm_g', 'new_m_mla_w_q_b', 'new_m_mla_kv_a_norm_g', 'new_m_mla_w_kv_b', 'new_m_mla_q_norm_g', 'new_m_mla_k_norm_g', 'new_m_dn_conv_w', 'new_m_dn_a_log', 'new_m_dn_dt_bias', 'new_m_dn_out_norm_g', 'new_m_dil_q_norm_g', 'new_m_dil_k_norm_g', 'new_m_w_branch', 'new_m_w_out', 'new_v_norm_g', 'new_v_w_in', 'new_v_mla_q_a_norm_g', 'new_v_mla_w_q_b', 'new_v_mla_kv_a_norm_g', 'new_v_mla_w_kv_b', 'new_v_mla_q_norm_g', 'new_v_mla_k_norm_g', 'new_v_dn_conv_w', 'new_v_dn_a_log', 'new_v_dn_dt_bias', 'new_v_dn_out_norm_g', 'new_v_dil_q_norm_g', 'new_v_dil_k_norm_g', 'new_v_w_branch', 'new_v_w_out']
TWIN_LEAF_KINDS = {'loss': 'loss', 'grad_x': 'grad_x', 'grad_norm_g': 'grad_w', 'grad_w_in': 'grad_w', 'grad_mla_q_a_norm_g': 'grad_w', 'grad_mla_w_q_b': 'grad_w', 'grad_mla_kv_a_norm_g': 'grad_w', 'grad_mla_w_kv_b': 'grad_w', 'grad_mla_q_norm_g': 'grad_w', 'grad_mla_k_norm_g': 'grad_w', 'grad_dn_conv_w': 'grad_w', 'grad_dn_a_log': 'grad_w', 'grad_dn_dt_bias': 'grad_w', 'grad_dn_out_norm_g': 'grad_w', 'grad_dil_q_norm_g': 'grad_w', 'grad_dil_k_norm_g': 'grad_w', 'grad_w_branch': 'grad_w', 'grad_w_out': 'grad_w', 'delta_norm_g': 'delta_w', 'delta_w_in': 'delta_w', 'delta_mla_q_a_norm_g': 'delta_w', 'delta_mla_w_q_b': 'delta_w', 'delta_mla_kv_a_norm_g': 'delta_w', 'delta_mla_w_kv_b': 'delta_w', 'delta_mla_q_norm_g': 'delta_w', 'delta_mla_k_norm_g': 'delta_w', 'delta_dn_conv_w': 'delta_w', 'delta_dn_a_log': 'delta_w', 'delta_dn_dt_bias': 'delta_w', 'delta_dn_out_norm_g': 'delta_w', 'delta_dil_q_norm_g': 'delta_w', 'delta_dil_k_norm_g': 'delta_w', 'delta_w_branch': 'delta_w', 'delta_w_out': 'delta_w', 'new_m_norm_g': 'new_m', 'new_m_w_in': 'new_m', 'new_m_mla_q_a_norm_g': 'new_m', 'new_m_mla_w_q_b': 'new_m', 'new_m_mla_kv_a_norm_g': 'new_m', 'new_m_mla_w_kv_b': 'new_m', 'new_m_mla_q_norm_g': 'new_m', 'new_m_mla_k_norm_g': 'new_m', 'new_m_dn_conv_w': 'new_m', 'new_m_dn_a_log': 'new_m', 'new_m_dn_dt_bias': 'new_m', 'new_m_dn_out_norm_g': 'new_m', 'new_m_dil_q_norm_g': 'new_m', 'new_m_dil_k_norm_g': 'new_m', 'new_m_w_branch': 'new_m', 'new_m_w_out': 'new_m', 'new_v_norm_g': 'new_v', 'new_v_w_in': 'new_v', 'new_v_mla_q_a_norm_g': 'new_v', 'new_v_mla_w_q_b': 'new_v', 'new_v_mla_kv_a_norm_g': 'new_v', 'new_v_mla_w_kv_b': 'new_v', 'new_v_mla_q_norm_g': 'new_v', 'new_v_mla_k_norm_g': 'new_v', 'new_v_dn_conv_w': 'new_v', 'new_v_dn_a_log': 'new_v', 'new_v_dn_dt_bias': 'new_v', 'new_v_dn_out_norm_g': 'new_v', 'new_v_dil_q_norm_g': 'new_v', 'new_v_dil_k_norm_g': 'new_v', 'new_v_w_branch': 'new_v', 'new_v_w_out': 'new_v'}


def _forward(args):
    return _fwd_reference(*[args[k] for k in FWD_PARAMS])


def _output_shape():
    out = _jax.eval_shape(lambda: _forward(_fwd_setup_inputs(0)))
    return out.shape, out.dtype

N_MICROBATCH = 1
ADAM_LR = 0.001
ADAM_B1 = 0.9
ADAM_B2 = 0.999
ADAM_EPS = 1e-08
ADAM_WD = 0.01
ADAM_STEP = 10
PER_EXAMPLE_BATCH_AXIS = {'x': 0, 'positions': 0, 'loss_target': 0}
SHARED_INPUTS = []
_WEIGHT_DTYPES = {'norm_g': _jnp.float32, 'w_in': _jnp.float32, 'mla_q_a_norm_g': _jnp.float32, 'mla_w_q_b': _jnp.float32, 'mla_kv_a_norm_g': _jnp.float32, 'mla_w_kv_b': _jnp.float32, 'mla_q_norm_g': _jnp.float32, 'mla_k_norm_g': _jnp.float32, 'dn_conv_w': _jnp.float32, 'dn_a_log': _jnp.float32, 'dn_dt_bias': _jnp.float32, 'dn_out_norm_g': _jnp.float32, 'dil_q_norm_g': _jnp.float32, 'dil_k_norm_g': _jnp.float32, 'w_branch': _jnp.float32, 'w_out': _jnp.float32}
MOMENT_SCALE = {'norm_g': 4.157259e+00, 'w_in': 1.002754e-01, 'mla_q_a_norm_g': 3.252106e-02, 'mla_w_q_b': 2.372906e-02, 'mla_kv_a_norm_g': 1.421321e-01, 'mla_w_kv_b': 3.231512e-02, 'mla_q_norm_g': 1.019675e-01, 'mla_k_norm_g': 1.026950e-01, 'dn_conv_w': 2.751556e-01, 'dn_a_log': 9.696857e+00, 'dn_dt_bias': 8.987505e+00, 'dn_out_norm_g': 2.999822e+01, 'dil_q_norm_g': 1.210883e-01, 'dil_k_norm_g': 1.222933e-01, 'w_branch': 2.390206e-01, 'w_out': 3.457909e-01}


def _to_microbatches(a, axis):
    t = _jnp.moveaxis(a, axis, 0)
    t = t.reshape((N_MICROBATCH, t.shape[0] // N_MICROBATCH) + t.shape[1:])
    return _jnp.moveaxis(t, 1, axis + 1)


def setup_inputs(seed: int = 0) -> dict:
    inp = _fwd_setup_inputs(seed)
    key = _jax.random.fold_in(_jax.random.key(seed), 7919)
    shape, _ = _output_shape()
    out = dict(inp)
    out["loss_target"] = _jax.random.normal(_jax.random.fold_in(key, 0), shape, _jnp.float32)
    for i, name in enumerate(TWIN_WEIGHTS):
        w = inp[name].astype(_jnp.float32)
        if MOMENT_SCALE is None:
            s = _jnp.sqrt(_jnp.mean(_jnp.square(w)) + 1e-30)
        else:
            s = MOMENT_SCALE[name]
        km, kv = _jax.random.split(_jax.random.fold_in(key, i + 1))
        out[name] = w
        out["m_" + name] = s * _jax.random.normal(km, w.shape, _jnp.float32)
        out["v_" + name] = (s * s) * _jax.random.uniform(kv, w.shape, _jnp.float32, 0.5, 1.5)
    if N_MICROBATCH > 1:
        for name, axis in PER_EXAMPLE_BATCH_AXIS.items():
            out[name] = _to_microbatches(out[name], axis)
    return {'x': out['x'], 'positions': out['positions'], 'norm_g': out['norm_g'], 'w_in': out['w_in'], 'mla_q_a_norm_g': out['mla_q_a_norm_g'], 'mla_w_q_b': out['mla_w_q_b'], 'mla_kv_a_norm_g': out['mla_kv_a_norm_g'], 'mla_w_kv_b': out['mla_w_kv_b'], 'mla_q_norm_g': out['mla_q_norm_g'], 'mla_k_norm_g': out['mla_k_norm_g'], 'dn_conv_w': out['dn_conv_w'], 'dn_a_log': out['dn_a_log'], 'dn_dt_bias': out['dn_dt_bias'], 'dn_out_norm_g': out['dn_out_norm_g'], 'dil_q_norm_g': out['dil_q_norm_g'], 'dil_k_norm_g': out['dil_k_norm_g'], 'w_branch': out['w_branch'], 'w_out': out['w_out'], 'loss_target': out['loss_target'], 'm_norm_g': out['m_norm_g'], 'm_w_in': out['m_w_in'], 'm_mla_q_a_norm_g': out['m_mla_q_a_norm_g'], 'm_mla_w_q_b': out['m_mla_w_q_b'], 'm_mla_kv_a_norm_g': out['m_mla_kv_a_norm_g'], 'm_mla_w_kv_b': out['m_mla_w_kv_b'], 'm_mla_q_norm_g': out['m_mla_q_norm_g'], 'm_mla_k_norm_g': out['m_mla_k_norm_g'], 'm_dn_conv_w': out['m_dn_conv_w'], 'm_dn_a_log': out['m_dn_a_log'], 'm_dn_dt_bias': out['m_dn_dt_bias'], 'm_dn_out_norm_g': out['m_dn_out_norm_g'], 'm_dil_q_norm_g': out['m_dil_q_norm_g'], 'm_dil_k_norm_g': out['m_dil_k_norm_g'], 'm_w_branch': out['m_w_branch'], 'm_w_out': out['m_w_out'], 'v_norm_g': out['v_norm_g'], 'v_w_in': out['v_w_in'], 'v_mla_q_a_norm_g': out['v_mla_q_a_norm_g'], 'v_mla_w_q_b': out['v_mla_w_q_b'], 'v_mla_kv_a_norm_g': out['v_mla_kv_a_norm_g'], 'v_mla_w_kv_b': out['v_mla_w_kv_b'], 'v_mla_q_norm_g': out['v_mla_q_norm_g'], 'v_mla_k_norm_g': out['v_mla_k_norm_g'], 'v_dn_conv_w': out['v_dn_conv_w'], 'v_dn_a_log': out['v_dn_a_log'], 'v_dn_dt_bias': out['v_dn_dt_bias'], 'v_dn_out_norm_g': out['v_dn_out_norm_g'], 'v_dil_q_norm_g': out['v_dil_q_norm_g'], 'v_dil_k_norm_g': out['v_dil_k_norm_g'], 'v_w_branch': out['v_w_branch'], 'v_w_out': out['v_w_out']}


def _loss(weights, diff, rest, loss_target):
    with _jax.named_scope("forward"):
        args = {**rest, TWIN_DIFF_INPUT: diff, **{k: w.astype(_WEIGHT_DTYPES[k]) for k, w in weights.items()}}
        y = _forward(args)
    with _jax.named_scope("loss_head"):
        err = _jnp.square(y.astype(_jnp.float32) - loss_target)
        return 0.5 * _jnp.sum(_jnp.mean(err, axis=-1)) if err.ndim else 0.5 * err


def _adamw(w, g, m, v):
    m = ADAM_B1 * m + (1.0 - ADAM_B1) * g
    v = ADAM_B2 * v + (1.0 - ADAM_B2) * _jnp.square(g)
    m_hat = m / (1.0 - ADAM_B1 ** ADAM_STEP)
    v_hat = v / (1.0 - ADAM_B2 ** ADAM_STEP)
    delta = -ADAM_LR * (m_hat / (_jnp.sqrt(v_hat) + ADAM_EPS) + ADAM_WD * w)
    return delta, m, v


def reference(x, positions, norm_g, w_in, mla_q_a_norm_g, mla_w_q_b, mla_kv_a_norm_g, mla_w_kv_b, mla_q_norm_g, mla_k_norm_g, dn_conv_w, dn_a_log, dn_dt_bias, dn_out_norm_g, dil_q_norm_g, dil_k_norm_g, w_branch, w_out, loss_target, m_norm_g, m_w_in, m_mla_q_a_norm_g, m_mla_w_q_b, m_mla_kv_a_norm_g, m_mla_w_kv_b, m_mla_q_norm_g, m_mla_k_norm_g, m_dn_conv_w, m_dn_a_log, m_dn_dt_bias, m_dn_out_norm_g, m_dil_q_norm_g, m_dil_k_norm_g, m_w_branch, m_w_out, v_norm_g, v_w_in, v_mla_q_a_norm_g, v_mla_w_q_b, v_mla_kv_a_norm_g, v_mla_w_kv_b, v_mla_q_norm_g, v_mla_k_norm_g, v_dn_conv_w, v_dn_a_log, v_dn_dt_bias, v_dn_out_norm_g, v_dil_q_norm_g, v_dil_k_norm_g, v_w_branch, v_w_out):
    given = dict(x=x, positions=positions, norm_g=norm_g, w_in=w_in, mla_q_a_norm_g=mla_q_a_norm_g, mla_w_q_b=mla_w_q_b, mla_kv_a_norm_g=mla_kv_a_norm_g, mla_w_kv_b=mla_w_kv_b, mla_q_norm_g=mla_q_norm_g, mla_k_norm_g=mla_k_norm_g, dn_conv_w=dn_conv_w, dn_a_log=dn_a_log, dn_dt_bias=dn_dt_bias, dn_out_norm_g=dn_out_norm_g, dil_q_norm_g=dil_q_norm_g, dil_k_norm_g=dil_k_norm_g, w_branch=w_branch, w_out=w_out, loss_target=loss_target, m_norm_g=m_norm_g, m_w_in=m_w_in, m_mla_q_a_norm_g=m_mla_q_a_norm_g, m_mla_w_q_b=m_mla_w_q_b, m_mla_kv_a_norm_g=m_mla_kv_a_norm_g, m_mla_w_kv_b=m_mla_w_kv_b, m_mla_q_norm_g=m_mla_q_norm_g, m_mla_k_norm_g=m_mla_k_norm_g, m_dn_conv_w=m_dn_conv_w, m_dn_a_log=m_dn_a_log, m_dn_dt_bias=m_dn_dt_bias, m_dn_out_norm_g=m_dn_out_norm_g, m_dil_q_norm_g=m_dil_q_norm_g, m_dil_k_norm_g=m_dil_k_norm_g, m_w_branch=m_w_branch, m_w_out=m_w_out, v_norm_g=v_norm_g, v_w_in=v_w_in, v_mla_q_a_norm_g=v_mla_q_a_norm_g, v_mla_w_q_b=v_mla_w_q_b, v_mla_kv_a_norm_g=v_mla_kv_a_norm_g, v_mla_w_kv_b=v_mla_w_kv_b, v_mla_q_norm_g=v_mla_q_norm_g, v_mla_k_norm_g=v_mla_k_norm_g, v_dn_conv_w=v_dn_conv_w, v_dn_a_log=v_dn_a_log, v_dn_dt_bias=v_dn_dt_bias, v_dn_out_norm_g=v_dn_out_norm_g, v_dil_q_norm_g=v_dil_q_norm_g, v_dil_k_norm_g=v_dil_k_norm_g, v_w_branch=v_w_branch, v_w_out=v_w_out)
    weights = {n: given[n] for n in TWIN_WEIGHTS}
    shared = {n: given[n] for n in SHARED_INPUTS}
    per_example = {n: given[n] for n in ['x', 'positions']}
    grad_fn = _jax.value_and_grad(_loss, argnums=(0, 1))

    def one_microbatch(ex, loss_target):
        ex = dict(ex)
        diff = ex.pop(TWIN_DIFF_INPUT)
        return grad_fn(weights, diff, {**shared, **ex}, loss_target)

    if N_MICROBATCH == 1:
        loss, (grad_w, grad_x) = one_microbatch(per_example, given["loss_target"])
    else:
        def body(carry, xs):
            loss_sum, grad_sum = carry
            l_k, (gw_k, gx_k) = one_microbatch(xs[0], xs[1])
            with _jax.named_scope("update"):
                return (loss_sum + l_k, _jax.tree.map(_jnp.add, grad_sum, gw_k)), gx_k

        init = (_jnp.zeros((), _jnp.float32), _jax.tree.map(_jnp.zeros_like, weights))
        (loss, grad_w), grad_x = _jax.lax.scan(body, init, (per_example, given["loss_target"]))
    with _jax.named_scope("update"):
        delta_w, new_m, new_v = {}, {}, {}
        for n in TWIN_WEIGHTS:
            delta_w[n], new_m[n], new_v[n] = _adamw(weights[n], grad_w[n], given["m_" + n], given["v_" + n])
    return (loss, grad_x, *[grad_w[n] for n in TWIN_WEIGHTS], *[delta_w[n] for n in TWIN_WEIGHTS],
            *[new_m[n] for n in TWIN_WEIGHTS], *[new_v[n] for n in TWIN_WEIGHTS])
```

```python
import functools
import math

import jax
import jax.numpy as jnp
from jax import lax
from jax.experimental import pallas as pl
from jax.experimental.pallas import tpu as pltpu

F32 = jnp.float32
BF16 = jnp.bfloat16
HI = lax.Precision.HIGHEST
MESH = pl.DeviceIdType.MESH

LANE = 128
VMEM_LIMIT = 48 * 1024 * 1024
ROW_BLOCK = 256
PACK_ROWS = 2048

RMS_EPS = 1e-6
ROPE_THETA = 10000.0
D_MODEL = 1024
DEPTH = 2
MLA_HEADS = 4
MLA_NOPE = 128
MLA_ROPE = 64
MLA_QK = MLA_NOPE + MLA_ROPE
MLA_Q_RANK = 384
MLA_KV_RANK = 256
DN_HEADS = 4
DN_DIM = 128
DN_CONV = 4
GDN_CHUNK = 128
DIL_WINDOWS = (128, 512, 2048)
DIL_DILATIONS = (1, 4, 16)
DIL_GROUPS = 3
DIL_BLOCK = 128
HEAD = 128
BRANCH_W = 512

ADAM_LR = 0.001
ADAM_B1 = 0.9
ADAM_B2 = 0.999
ADAM_EPS = 1e-08
ADAM_WD = 0.01
ADAM_STEP = 10

WEIGHTS = ['norm_g', 'w_in', 'mla_q_a_norm_g', 'mla_w_q_b', 'mla_kv_a_norm_g', 'mla_w_kv_b', 'mla_q_norm_g',
           'mla_k_norm_g', 'dn_conv_w', 'dn_a_log', 'dn_dt_bias', 'dn_out_norm_g', 'dil_q_norm_g', 'dil_k_norm_g',
           'w_branch', 'w_out']
BIG = (('w_in', 2), ('mla_w_q_b', 2), ('mla_w_kv_b', 2), ('w_branch', 3), ('w_out', 1))
SMALL = ('norm_g', 'mla_q_a_norm_g', 'mla_kv_a_norm_g', 'mla_q_norm_g', 'mla_k_norm_g', 'dn_a_log', 'dn_dt_bias',
         'dn_out_norm_g', 'dil_q_norm_g', 'dil_k_norm_g')


def _cparams(*sem):
    return pltpu.CompilerParams(dimension_semantics=sem or None, vmem_limit_bytes=VMEM_LIMIT)


def _tile(dim, target):
    best = 0
    for t in range(LANE, min(dim, target) + 1, LANE):
        if dim % t == 0:
            best = t
    assert best, (dim, target)
    return best


def _mm(a, b, mode, out_dtype, res=None):
    if mode == 'nn':
        (m, k), (k2, n) = a.shape, b.shape
    elif mode == 'nt':
        (m, k), (n, k2) = a.shape, b.shape
    else:
        (k, m), (k2, n) = a.shape, b.shape
    assert k == k2, (a.shape, b.shape, mode)
    tm, tn, tk = _tile(m, 512), _tile(n, 1024), _tile(k, 1024)
    nk = k // tk
    dims = {'nn': (((1,), (0,)), ((), ())), 'nt': (((1,), (1,)), ((), ())), 'tn': (((0,), (0,)), ((), ()))}[mode]

    def body(*refs):
        if res is None:
            a_ref, b_ref, o_ref, acc = refs
        else:
            a_ref, b_ref, r_ref, o_ref, acc = refs
        kk = pl.program_id(2)

        @pl.when(kk == 0)
        def _():
            acc[...] = jnp.zeros_like(acc)

        acc[...] += lax.dot_general(a_ref[...].astype(BF16), b_ref[...].astype(BF16), dims,
                                    preferred_element_type=F32)

        @pl.when(kk == nk - 1)
        def _():
            r = acc[...]
            if res is not None:
                r = r + r_ref[...].astype(F32)
            o_ref[...] = r.astype(o_ref.dtype)

    a_spec = pl.BlockSpec((tk, tm), lambda i, j, kk: (kk, i)) if mode == 'tn' else pl.BlockSpec((tm, tk), lambda i, j, kk: (i, kk))
    b_spec = pl.BlockSpec((tn, tk), lambda i, j, kk: (j, kk)) if mode == 'nt' else pl.BlockSpec((tk, tn), lambda i, j, kk: (kk, j))
    o_spec = pl.BlockSpec((tm, tn), lambda i, j, kk: (i, j))
    in_specs = [a_spec, b_spec] + ([o_spec] if res is not None else [])
    args = (a, b) + ((res,) if res is not None else ())
    return pl.pallas_call(
        body, out_shape=jax.ShapeDtypeStruct((m, n), out_dtype), grid=(m // tm, n // tn, nk),
        in_specs=in_specs, out_specs=o_spec, scratch_shapes=[pltpu.VMEM((tm, tn), F32)],
        compiler_params=_cparams("parallel", "parallel", "arbitrary"),
        name=f"mm_{mode}_{m}x{k}x{n}" + ("_res" if res is not None else ""))(*args)


def _make_matmul(out_dtype, with_res):
    @jax.custom_vjp
    def mm(a, b, *r):
        return _mm(a, b, 'nn', out_dtype, *r)

    def fwd(a, b, *r):
        return mm(a, b, *r), (a, b)

    def bwd(saved, g):
        a, b = saved
        da = _mm(g, b, 'nt', a.dtype)
        db = _mm(a, g, 'tn', b.dtype)
        return (da, db) + ((g,) if with_res else ())

    mm.defvjp(fwd, bwd)
    return mm


def matmul(a, b, out_dtype=F32, res=None):
    if res is None:
        return _make_matmul(out_dtype, False)(a, b)
    return _make_matmul(out_dtype, True)(a, b, res)


def _chunks(ref):
    return [ref[:, c * LANE:(c + 1) * LANE].astype(F32) for c in range(ref.shape[-1] // LANE)]


def _store(ref, chunks):
    for c, ch in enumerate(chunks):
        ref[:, c * LANE:(c + 1) * LANE] = ch.astype(ref.dtype)


def _row_spec(width, br):
    return pl.BlockSpec((br, width), lambda i: (i, 0))


def _par_spec(width):
    return pl.BlockSpec((1, width), lambda i: (0, 0))


def _rw_fwd(f, name, rows, params, outs, br):
    s = rows[0].shape[0]
    br = min(br, s)
    nr, npar = len(rows), len(params)

    def body(*refs):
        rc = [_chunks(r) for r in refs[:nr]]
        pc = [_chunks(p) for p in refs[nr:nr + npar]]
        res = f(rc, pc)
        for o_ref, chs in zip(refs[nr + npar:], res):
            _store(o_ref, chs)

    return pl.pallas_call(
        body, out_shape=[jax.ShapeDtypeStruct((s, w), dt) for w, dt in outs], grid=(s // br,),
        in_specs=[_row_spec(r.shape[1], br) for r in rows] + [_par_spec(p.shape[1]) for p in params],
        out_specs=[_row_spec(w, br) for w, _ in outs],
        compiler_params=_cparams("parallel"), name=name + "_fwd")(*rows, *params)


def _rw_bwd(f, name, rows, params, cts, nograd, br):
    s = rows[0].shape[0]
    br = min(br, s)
    nr, npar, nct = len(rows), len(params), len(cts)
    grad_rows = [i for i in range(nr) if i not in nograd]

    def body(*refs):
        rc = [_chunks(r) for r in refs[:nr]]
        pc = [_chunks(p) for p in refs[nr:nr + npar]]
        ct = [_chunks(c) for c in refs[nr + npar:nr + npar + nct]]
        out_refs = refs[nr + npar + nct:]
        _, vjp = jax.vjp(f, rc, pc)
        drc, dpc = vjp(ct)
        for o_ref, i in zip(out_refs[:len(grad_rows)], grad_rows):
            _store(o_ref, drc[i])
        i0 = pl.program_id(0)
        for o_ref, chs in zip(out_refs[len(grad_rows):], dpc):
            @pl.when(i0 == 0)
            def _(o_ref=o_ref):
                o_ref[...] = jnp.zeros_like(o_ref)
            for c, ch in enumerate(chs):
                o_ref[:, c * LANE:(c + 1) * LANE] += ch

    out_shape = ([jax.ShapeDtypeStruct(rows[i].shape, rows[i].dtype) for i in grad_rows]
                 + [jax.ShapeDtypeStruct(p.shape, F32) for p in params])
    out_specs = [_row_spec(rows[i].shape[1], br) for i in grad_rows] + [_par_spec(p.shape[1]) for p in params]
    res = pl.pallas_call(
        body, out_shape=out_shape, grid=(s // br,),
        in_specs=([_row_spec(r.shape[1], br) for r in rows] + [_par_spec(p.shape[1]) for p in params]
                  + [_row_spec(c.shape[1], br) for c in cts]),
        out_specs=out_specs, compiler_params=_cparams("arbitrary"), name=name + "_bwd")(*rows, *params, *cts)
    drows = [None] * nr
    for o, i in zip(res[:len(grad_rows)], grad_rows):
        drows[i] = o
    for i in nograd:
        drows[i] = jnp.zeros_like(rows[i])
    return tuple(drows), tuple(res[len(grad_rows):])


def rowwise(f, name, rows, params, outs, nograd=(), br=ROW_BLOCK):
    @jax.custom_vjp
    def op(rows, params):
        return tuple(_rw_fwd(f, name, rows, params, outs, br))

    def fwd(rows, params):
        return op(rows, params), (rows, params)

    def bwd(saved, cts):
        rows, params = saved
        return _rw_bwd(f, name, rows, params, list(cts), nograd, br)

    op.defvjp(fwd, bwd)
    return op(tuple(rows), tuple(params))


def _lane_roll(x, s):
    w = x.shape[-1]

    @jax.custom_vjp
    def r(v):
        return pltpu.roll(v, s, 1)

    r.defvjp(lambda v: (r(v), None), lambda _, g: (pltpu.roll(g, (w - s) % w, 1),))
    return r(x)


def _lanes(x):
    return lax.broadcasted_iota(jnp.int32, x.shape, 1)


def _rms(x, g, n=LANE):
    return x * lax.rsqrt(jnp.sum(x * x, axis=-1, keepdims=True) / n + RMS_EPS) * g


def _rope128(x, cos, sin):
    lane = _lanes(x)
    rot = _lane_roll(x, 64) * jnp.where(lane < 64, -1.0, 1.0)
    return x * cos + rot * sin


def _rope64(x, cos, sin):
    lane = _lanes(x)
    rot = jnp.where(lane < 32, -_lane_roll(x, 96), jnp.where(lane < 64, _lane_roll(x, 32), 0.0))
    return x * cos + rot * sin


def _silu(z):
    return z * jax.nn.sigmoid(z)


def _softplus(x):
    return jnp.maximum(x, 0.0) + jnp.log1p(jnp.exp(-jnp.abs(x)))


def f_rms_full(rc, pc):
    x, g = rc[0], pc[0]
    width = len(x) * LANE
    ms = sum(jnp.sum(c * c, axis=-1, keepdims=True) for c in x) / width
    r = lax.rsqrt(ms + RMS_EPS)
    return [[c * r * gc for c, gc in zip(x, g)]]


def f_mla1(rc, pc):
    x = rc[0]
    qn = f_rms_full([x[0:3]], [pc[0]])[0]
    kvn = f_rms_full([x[3:5]], [pc[1]])[0]
    return [qn, kvn, [x[5]]]


def f_mla2(rc, pc):
    q, kv, kpe, cos, sin = rc[0], rc[1], rc[2][0], rc[3][0], rc[4][0]
    gqn, gqp, gkn, gkp = pc[0][0], pc[1][0], pc[2][0], pc[3][0]
    k_pe = _rope64(_rms(kpe, gkp, MLA_ROPE), cos, sin)
    q_att, k_att, v = [], [], []
    for h in range(MLA_HEADS):
        q_att += [_rms(q[h], gqn), _rope64(_rms(q[MLA_HEADS + h], gqp, MLA_ROPE), cos, sin)]
        k_att += [_rms(kv[2 * h], gkn), k_pe]
        v.append(kv[2 * h + 1])
    return [q_att, k_att, v]


def f_gates(rc, pc):
    x, a_log, dt_bias = rc[0][0], pc[0][0], pc[1][0]
    lane = _lanes(x)
    g = -jnp.exp(a_log) * _softplus(x + dt_bias)
    return [[jnp.where(lane < DN_HEADS, g, jnp.where(lane < 2 * DN_HEADS, jax.nn.sigmoid(x), 0.0))]]


def f_headnorm(rc, pc):
    return [[_rms(c, pc[0][0]) for c in rc[0]]]


def f_dil(rc, pc):
    x, cos, sin = rc[0], rc[1][0], rc[2][0]
    gq, gk = pc[0][0], pc[1][0]
    n = len(x) // 3
    q = [_rope128(_rms(c, gq), cos, sin) for c in x[:n]]
    k = [_rope128(_rms(c, gk), cos, sin) for c in x[n:2 * n]]
    return [q, k, list(x[2 * n:])]


def f_comb(rc, pc):
    o, l = rc[:DIL_GROUPS], rc[DIL_GROUPS:]
    out = []
    for c in range(len(o[0])):
        m = functools.reduce(jnp.maximum, [lg[c] for lg in l])
        e = [jnp.exp(lg[c] - m) for lg in l]
        den = sum(e)
        out.append(sum(eg * og[c] for eg, og in zip(e, o)) / den)
    return [out]


def f_merge1(rc, pc):
    z = rc[3]
    n = len(rc[0])
    return [[y * _silu(z[b * n + c]) for c, y in enumerate(rc[b])] for b in range(3)]


def f_merge2(rc, pc):
    gate = rc[0]
    n = len(rc[1])
    return [[sum(jax.nn.sigmoid(gate[b * n + c]) * rc[1 + b][c] for b in range(3)) for c in range(n)]]


MLA_DQK = 2 * LANE
MLA_SCALE = MLA_QK ** -0.5


def _causal_mask(qi, kj, tq, tk):
    qpos = qi * tq + lax.broadcasted_iota(jnp.int32, (tq, tk), 0)
    kpos = kj * tk + lax.broadcasted_iota(jnp.int32, (tq, tk), 1)
    return kpos <= qpos


def _mla_attn_fwd(q, k, v):
    s = q.shape[0]
    h = q.shape[1] // MLA_DQK
    t = min(512, s)
    n = s // t

    def body(q_ref, k_ref, v_ref, o_ref, lse_ref, m_sc, l_sc, acc_sc):
        qi, kj = pl.program_id(1), pl.program_id(2)

        @pl.when(kj == 0)
        def _():
            m_sc[...] = jnp.full_like(m_sc, -jnp.inf)
            l_sc[...] = jnp.zeros_like(l_sc)
            acc_sc[...] = jnp.zeros_like(acc_sc)

        @pl.when(kj <= qi)
        def _():
            sc = lax.dot_general(q_ref[...], k_ref[...], (((1,), (1,)), ((), ())), preferred_element_type=F32) * MLA_SCALE
            sc = jnp.where(_causal_mask(qi, kj, t, t), sc, -jnp.inf)
            m_new = jnp.maximum(m_sc[...], jnp.max(sc, axis=-1, keepdims=True))
            alpha = jnp.exp(m_sc[...] - m_new)
            p = jnp.exp(sc - m_new)
            l_sc[...] = alpha * l_sc[...] + jnp.sum(p, axis=-1, keepdims=True)
            acc_sc[...] = alpha * acc_sc[...] + jnp.dot(p.astype(BF16), v_ref[...], preferred_element_type=F32)
            m_sc[...] = m_new

        @pl.when(kj == n - 1)
        def _():
            o_ref[...] = acc_sc[...] / l_sc[...]
            lse_ref[...] = jnp.broadcast_to(m_sc[...] + jnp.log(l_sc[...]), lse_ref.shape)

    return pl.pallas_call(
        body, out_shape=[jax.ShapeDtypeStruct((s, h * HEAD), F32)] * 2, grid=(h, n, n),
        in_specs=[pl.BlockSpec((t, MLA_DQK), lambda hh, i, j: (i, hh)),
                  pl.BlockSpec((t, MLA_DQK), lambda hh, i, j: (jnp.minimum(j, i), hh)),
                  pl.BlockSpec((t, HEAD), lambda hh, i, j: (jnp.minimum(j, i), hh))],
        out_specs=[pl.BlockSpec((t, HEAD), lambda hh, i, j: (i, hh))] * 2,
        scratch_shapes=[pltpu.VMEM((t, 1), F32), pltpu.VMEM((t, 1), F32), pltpu.VMEM((t, HEAD), F32)],
        compiler_params=_cparams("parallel", "parallel", "arbitrary"), name="mla_attn_fwd")(q, k, v)


def _mla_p_ds(q_ref, k_ref, v_ref, o_ref, lse_ref, do_ref, qi, kj, t):
    sc = lax.dot_general(q_ref[...], k_ref[...], (((1,), (1,)), ((), ())), preferred_element_type=F32) * MLA_SCALE
    p = jnp.where(_causal_mask(qi, kj, t, t), jnp.exp(sc - lse_ref[:, 0:1]), 0.0)
    do = do_ref[...]
    dp = lax.dot_general(do.astype(BF16), v_ref[...], (((1,), (1,)), ((), ())), preferred_element_type=F32)
    delta = jnp.sum(do * o_ref[...], axis=-1, keepdims=True)
    ds = p * (dp - delta) * MLA_SCALE
    return p, ds, do


def _mla_attn_bwd(q, k, v, o, lse, do):
    s = q.shape[0]
    h = q.shape[1] // MLA_DQK
    t = min(512, s)
    n = s // t

    def body_kv(q_ref, k_ref, v_ref, o_ref, lse_ref, do_ref, dk_ref, dv_ref, dk_sc, dv_sc):
        kj, qi = pl.program_id(1), pl.program_id(2)

        @pl.when(qi == 0)
        def _():
            dk_sc[...] = jnp.zeros_like(dk_sc)
            dv_sc[...] = jnp.zeros_like(dv_sc)

        @pl.when(qi >= kj)
        def _():
            p, ds, do = _mla_p_ds(q_ref, k_ref, v_ref, o_ref, lse_ref, do_ref, qi, kj, t)
            dv_sc[...] += lax.dot_general(p.astype(BF16), do.astype(BF16), (((0,), (0,)), ((), ())), preferred_element_type=F32)
            dk_sc[...] += lax.dot_general(ds.astype(BF16), q_ref[...], (((0,), (0,)), ((), ())), preferred_element_type=F32)

        @pl.when(qi == n - 1)
        def _():
            dk_ref[...] = dk_sc[...].astype(dk_ref.dtype)
            dv_ref[...] = dv_sc[...].astype(dv_ref.dtype)

    qmap = lambda hh, j, i: (jnp.maximum(i, j), hh)
    kmap = lambda hh, j, i: (j, hh)
    dk, dv = pl.pallas_call(
        body_kv, out_shape=[jax.ShapeDtypeStruct(k.shape, BF16), jax.ShapeDtypeStruct(v.shape, BF16)], grid=(h, n, n),
        in_specs=[pl.BlockSpec((t, MLA_DQK), qmap), pl.BlockSpec((t, MLA_DQK), kmap), pl.BlockSpec((t, HEAD), kmap),
                  pl.BlockSpec((t, HEAD), qmap), pl.BlockSpec((t, HEAD), qmap), pl.BlockSpec((t, HEAD), qmap)],
        out_specs=[pl.BlockSpec((t, MLA_DQK), kmap), pl.BlockSpec((t, HEAD), kmap)],
        scratch_shapes=[pltpu.VMEM((t, MLA_DQK), F32), pltpu.VMEM((t, HEAD), F32)],
        compiler_params=_cparams("parallel", "parallel", "arbitrary"), name="mla_attn_bwd_kv")(q, k, v, o, lse, do)

    def body_q(q_ref, k_ref, v_ref, o_ref, lse_ref, do_ref, dq_ref, dq_sc):
        qi, kj = pl.program_id(1), pl.program_id(2)

        @pl.when(kj == 0)
        def _():
            dq_sc[...] = jnp.zeros_like(dq_sc)

        @pl.when(kj <= qi)
        def _():
            _, ds, _ = _mla_p_ds(q_ref, k_ref, v_ref, o_ref, lse_ref, do_ref, qi, kj, t)
            dq_sc[...] += jnp.dot(ds.astype(BF16), k_ref[...], preferred_element_type=F32)

        @pl.when(kj == n - 1)
        def _():
            dq_ref[...] = dq_sc[...].astype(dq_ref.dtype)

    qmap2 = lambda hh, i, j: (i, hh)
    kmap2 = lambda hh, i, j: (jnp.minimum(j, i), hh)
    dq = pl.pallas_call(
        body_q, out_shape=jax.ShapeDtypeStruct(q.shape, BF16), grid=(h, n, n),
        in_specs=[pl.BlockSpec((t, MLA_DQK), qmap2), pl.BlockSpec((t, MLA_DQK), kmap2), pl.BlockSpec((t, HEAD), kmap2),
                  pl.BlockSpec((t, HEAD), qmap2), pl.BlockSpec((t, HEAD), qmap2), pl.BlockSpec((t, HEAD), qmap2)],
        out_specs=pl.BlockSpec((t, MLA_DQK), qmap2), scratch_shapes=[pltpu.VMEM((t, MLA_DQK), F32)],
        compiler_params=_cparams("parallel", "parallel", "arbitrary"), name="mla_attn_bwd_q")(q, k, v, o, lse, do)
    return dq, dk, dv


@jax.custom_vjp
def mla_attention(q, k, v):
    return _mla_attn_fwd(q, k, v)[0]


def _mla_attention_fwd(q, k, v):
    o, lse = _mla_attn_fwd(q, k, v)
    return o, (q, k, v, o, lse)


def _mla_attention_bwd(saved, do):
    return _mla_attn_bwd(*saved, do)


mla_attention.defvjp(_mla_attention_fwd, _mla_attention_bwd)


DIL_SCALE = HEAD ** -0.5
GROUP_W = 4 * HEAD


def _dil_scores(q, kp, kc, n):
    dn = (((1,), (1,)), ((), ()))
    sp = lax.dot_general(q, kp, dn, preferred_element_type=F32) * DIL_SCALE
    sc = lax.dot_general(q, kc, dn, preferred_element_type=F32) * DIL_SCALE
    qi = lax.broadcasted_iota(jnp.int32, sp.shape, 0)
    kc_i = lax.broadcasted_iota(jnp.int32, sp.shape, 1)
    vp = jnp.logical_and(kc_i >= qi, n > 0)
    vc = kc_i <= qi
    return sp, sc, vp, vc


def _dil_specs(d):
    cur = pl.BlockSpec((DIL_BLOCK, GROUP_W), lambda r, n: (n, r))
    prev = pl.BlockSpec((DIL_BLOCK, GROUP_W), lambda r, n: (jnp.maximum(n - 1, 0), r))
    return cur, prev


def _dil_fwd(q, k, v, d):
    l = q.shape[0]
    nb = l // DIL_BLOCK
    cur, prev = _dil_specs(d)

    def body(q_ref, kp_ref, kc_ref, vp_ref, vc_ref, o_ref, lse_ref):
        n = pl.program_id(1)
        for h in range(4):
            sl = slice(h * HEAD, (h + 1) * HEAD)
            sp, sc, vp, vc = _dil_scores(q_ref[:, sl], kp_ref[:, sl], kc_ref[:, sl], n)
            sp = jnp.where(vp, sp, -jnp.inf)
            sc = jnp.where(vc, sc, -jnp.inf)
            m = jnp.maximum(jnp.max(sp, axis=-1, keepdims=True), jnp.max(sc, axis=-1, keepdims=True))
            ep, ec = jnp.exp(sp - m), jnp.exp(sc - m)
            den = jnp.sum(ep, axis=-1, keepdims=True) + jnp.sum(ec, axis=-1, keepdims=True)
            acc = (jnp.dot(ep.astype(BF16), vp_ref[:, sl], preferred_element_type=F32)
                   + jnp.dot(ec.astype(BF16), vc_ref[:, sl], preferred_element_type=F32))
            o_ref[:, sl] = acc / den
            lse_ref[:, sl] = jnp.broadcast_to(m + jnp.log(den), (DIL_BLOCK, HEAD))

    return pl.pallas_call(
        body, out_shape=[jax.ShapeDtypeStruct(q.shape, F32)] * 2, grid=(d, nb),
        in_specs=[cur, prev, cur, prev, cur], out_specs=[cur, cur],
        compiler_params=_cparams("parallel", "parallel"), name=f"dil_fwd_d{d}")(q, k, k, v, v)


def _dil_bwd(q, k, v, o, lse, do, dlse, d):
    l = q.shape[0]
    nb = l // DIL_BLOCK
    cur, prev = _dil_specs(d)
    tn = (((0,), (0,)), ((), ()))
    nt = (((1,), (1,)), ((), ()))

    def body(q_ref, kp_ref, kc_ref, vp_ref, vc_ref, o_ref, lse_ref, do_ref, dl_ref,
             dq_ref, dkp_ref, dkc_ref, dvp_ref, dvc_ref):
        n = pl.program_id(1)
        for h in range(4):
            sl = slice(h * HEAD, (h + 1) * HEAD)
            qh = q_ref[:, sl]
            sp, sc, vp, vc = _dil_scores(qh, kp_ref[:, sl], kc_ref[:, sl], n)
            lse_h = lse_ref[:, h * HEAD:h * HEAD + 1]
            pp = jnp.where(vp, jnp.exp(sp - lse_h), 0.0)
            pc = jnp.where(vc, jnp.exp(sc - lse_h), 0.0)
            do_h = do_ref[:, sl]
            do_b = do_h.astype(BF16)
            corr = (jnp.sum(dl_ref[:, sl], axis=-1, keepdims=True)
                    - jnp.sum(do_h * o_ref[:, sl], axis=-1, keepdims=True))
            dsp = pp * (lax.dot_general(do_b, vp_ref[:, sl], nt, preferred_element_type=F32) + corr) * DIL_SCALE
            dsc = pc * (lax.dot_general(do_b, vc_ref[:, sl], nt, preferred_element_type=F32) + corr) * DIL_SCALE
            dsp_b, dsc_b = dsp.astype(BF16), dsc.astype(BF16)
            dq_ref[:, sl] = (jnp.dot(dsp_b, kp_ref[:, sl], preferred_element_type=F32)
                             + jnp.dot(dsc_b, kc_ref[:, sl], preferred_element_type=F32)).astype(dq_ref.dtype)
            dkp_ref[:, sl] = lax.dot_general(dsp_b, qh, tn, preferred_element_type=F32)
            dkc_ref[:, sl] = lax.dot_general(dsc_b, qh, tn, preferred_element_type=F32)
            dvp_ref[:, sl] = lax.dot_general(pp.astype(BF16), do_b, tn, preferred_element_type=F32)
            dvc_ref[:, sl] = lax.dot_general(pc.astype(BF16), do_b, tn, preferred_element_type=F32)

    dq, dkp, dkc, dvp, dvc = pl.pallas_call(
        body, out_shape=[jax.ShapeDtypeStruct(q.shape, BF16)] + [jax.ShapeDtypeStruct(q.shape, F32)] * 4, grid=(d, nb),
        in_specs=[cur, prev, cur, prev, cur, cur, cur, cur, cur], out_specs=[cur] * 5,
        compiler_params=_cparams("parallel", "parallel"), name=f"dil_bwd_d{d}")(q, k, k, v, v, o, lse, do, dlse)

    def fold(cur_part, prev_part):
        shifted = jnp.concatenate([prev_part[DIL_BLOCK:], jnp.zeros((DIL_BLOCK, prev_part.shape[1]), F32)], axis=0)
        return (cur_part + shifted).astype(BF16)

    return dq, fold(dkc, dkp), fold(dvc, dvp)


def _make_dil(d):
    @jax.custom_vjp
    def att(q, k, v):
        return tuple(_dil_fwd(q, k, v, d))

    def fwd(q, k, v):
        o, lse = _dil_fwd(q, k, v, d)
        return (o, lse), (q, k, v, o, lse)

    def bwd(saved, cts):
        return _dil_bwd(*saved, cts[0], cts[1], d)

    att.defvjp(fwd, bwd)
    return att


def dilated_group(q, k, v, d):
    s = q.shape[0]
    view = lambda t: t.reshape(s // d, d * GROUP_W)
    o, lse = _make_dil(d)(view(q), view(k), view(v))
    return o.reshape(s, GROUP_W), lse.reshape(s, GROUP_W)


def _dn_post(c, kind, scale):
    m = _silu(c)
    nrm = m * lax.rsqrt(jnp.sum(m * m, axis=-1, keepdims=True) + 1e-6) * scale
    return kind * nrm + (1.0 - kind) * m


def _dn_kind_scale(j):
    kind = jnp.where(j < 2 * DN_HEADS, 1.0, 0.0).astype(F32)
    scale = jnp.where(j < DN_HEADS, DN_DIM ** -0.5, 1.0).astype(F32)
    return kind, scale


_CONV_RB = 512
_CONV_PAD = 8


def _conv_windows(pad_ref, w_ref, r0, rb, sign):
    acc = None
    for sh in range(DN_CONV):
        win = pad_ref[pl.ds(r0 + _CONV_PAD * (sign < 0) + sign * sh, rb), :]
        term = w_ref[DN_CONV - 1 - sh:DN_CONV - sh, :] * win
        acc = term if acc is None else acc + term
    return acc


def _dn_conv_fwd(x, w):
    s, width = x.shape
    rb = min(_CONV_RB, s)

    def body(x_ref, w_ref, o_ref, pad_ref):
        kind, scale = _dn_kind_scale(pl.program_id(0))
        pad_ref[0:_CONV_PAD, :] = jnp.zeros((_CONV_PAD, LANE), F32)
        pad_ref[_CONV_PAD:, :] = x_ref[...]
        for r0 in range(0, s, rb):
            c = _conv_windows(pad_ref, w_ref, r0, rb, -1)
            o_ref[r0:r0 + rb, :] = _dn_post(c, kind, scale)

    return pl.pallas_call(
        body, out_shape=jax.ShapeDtypeStruct(x.shape, F32), grid=(width // LANE,),
        in_specs=[pl.BlockSpec((s, LANE), lambda j: (0, j)), pl.BlockSpec((DN_CONV, LANE), lambda j: (0, j))],
        out_specs=pl.BlockSpec((s, LANE), lambda j: (0, j)), scratch_shapes=[pltpu.VMEM((s + _CONV_PAD, LANE), F32)],
        compiler_params=_cparams("parallel"), name="dn_conv_fwd")(x, w)


def _dn_conv_bwd(x, w, dy):
    s, width = x.shape
    rb = min(_CONV_RB, s)

    def body(x_ref, w_ref, dy_ref, dx_ref, dw_ref, pad_ref, dpad_ref):
        kind, scale = _dn_kind_scale(pl.program_id(0))
        pad_ref[0:_CONV_PAD, :] = jnp.zeros((_CONV_PAD, LANE), F32)
        pad_ref[_CONV_PAD:, :] = x_ref[...]
        dpad_ref[s:, :] = jnp.zeros((_CONV_PAD, LANE), F32)
        dws = [jnp.zeros((1, LANE), F32) for _ in range(DN_CONV)]
        for r0 in range(0, s, rb):
            c = _conv_windows(pad_ref, w_ref, r0, rb, -1)
            _, vjp = jax.vjp(lambda cc: _dn_post(cc, kind, scale), c)
            dc = vjp(dy_ref[r0:r0 + rb, :])[0]
            dpad_ref[r0:r0 + rb, :] = dc
            for sh in range(DN_CONV):
                win = pad_ref[pl.ds(r0 + _CONV_PAD - sh, rb), :]
                dws[DN_CONV - 1 - sh] = dws[DN_CONV - 1 - sh] + jnp.sum(dc * win, axis=0, keepdims=True)
        for j in range(DN_CONV):
            dw_ref[j:j + 1, :] = dws[j]
        for r0 in range(0, s, rb):
            dx_ref[r0:r0 + rb, :] = _conv_windows(dpad_ref, w_ref, r0, rb, 1)

    return pl.pallas_call(
        body, out_shape=[jax.ShapeDtypeStruct(x.shape, F32), jax.ShapeDtypeStruct(w.shape, F32)], grid=(width // LANE,),
        in_specs=[pl.BlockSpec((s, LANE), lambda j: (0, j)), pl.BlockSpec((DN_CONV, LANE), lambda j: (0, j)),
                  pl.BlockSpec((s, LANE), lambda j: (0, j))],
        out_specs=[pl.BlockSpec((s, LANE), lambda j: (0, j)), pl.BlockSpec((DN_CONV, LANE), lambda j: (0, j))],
        scratch_shapes=[pltpu.VMEM((s + _CONV_PAD, LANE), F32), pltpu.VMEM((s + _CONV_PAD, LANE), F32)],
        compiler_params=_cparams("parallel"), name="dn_conv_bwd")(x, w, dy)


@jax.custom_vjp
def dn_conv(x, w):
    return _dn_conv_fwd(x, w)


dn_conv.defvjp(lambda x, w: (_dn_conv_fwd(x, w), (x, w)), lambda saved, dy: tuple(_dn_conv_bwd(*saved, dy)))


def _hdot(a, b, dims=(((1,), (0,)), ((), ()))):
    return lax.dot_general(a, b, dims, precision=HI, preferred_element_type=F32)


_NT = (((1,), (1,)), ((), ()))
_TN = (((0,), (0,)), ((), ()))


def _gdn_prep_fn(qkv, gb):
    c = GDN_CHUNK
    row = lax.broadcasted_iota(jnp.int32, (c, c), 0)
    col = lax.broadcasted_iota(jnp.int32, (c, c), 1)
    incl, strict = row >= col, row > col
    ltri = incl.astype(F32)
    eye = (row == col).astype(F32)
    lane = _lanes(gb)
    outs = [[] for _ in range(6)]
    for h in range(DN_HEADS):
        q, k, v = qkv[h], qkv[DN_HEADS + h], qkv[2 * DN_HEADS + h]
        g = jnp.sum(jnp.where(lane == h, gb, 0.0), axis=-1, keepdims=True)
        beta = jnp.sum(jnp.where(lane == DN_HEADS + h, gb, 0.0), axis=-1, keepdims=True)
        gcb = _hdot(ltri, jnp.broadcast_to(g, (c, c)))
        gct = gcb.T
        decay = jnp.where(incl, jnp.exp(jnp.where(incl, gcb - gct, 0.0)), 0.0)
        kb = k * beta
        a = jnp.where(strict, _hdot(kb, k, _NT) * decay, 0.0)
        pw = -a
        t = eye + pw
        for _ in range(int(math.log2(c)) - 1):
            pw = _hdot(pw, pw)
            t = t + _hdot(t, pw)
        eg = jnp.exp(gcb)
        u = _hdot(t, v * beta)
        w = _hdot(t, kb * eg)
        qk = jnp.where(incl, _hdot(q, k, _NT) * decay, 0.0)
        g_last = jnp.sum(jnp.where(row == c - 1, gcb, 0.0), axis=0, keepdims=True)
        kg = k * jnp.exp(g_last - gcb)
        for lst, val in zip(outs, (u, w, q * eg, kg, qk, jnp.broadcast_to(jnp.exp(g_last), (8, LANE)))):
            lst.append(val)
    return outs


def _gdn_prep_specs(nq):
    c = GDN_CHUNK
    big = pl.BlockSpec((c, nq * LANE), lambda n: (n, 0))
    return big


def _gdn_prep_fwd(qkv, gb):
    s = qkv.shape[0]
    c = GDN_CHUNK
    n = s // c
    hw = DN_HEADS * LANE

    def body(qkv_ref, gb_ref, u_ref, w_ref, qg_ref, kg_ref, qk_ref, el_ref):
        res = _gdn_prep_fn(_chunks(qkv_ref), gb_ref[...])
        for ref, chs in zip((u_ref, w_ref, qg_ref, kg_ref, qk_ref, el_ref), res):
            _store(ref, chs)

    row = pl.BlockSpec((c, hw), lambda i: (i, 0))
    return pl.pallas_call(
        body, out_shape=[jax.ShapeDtypeStruct((s, hw), F32)] * 5 + [jax.ShapeDtypeStruct((n * 8, hw), F32)], grid=(n,),
        in_specs=[pl.BlockSpec((c, 3 * hw), lambda i: (i, 0)), pl.BlockSpec((c, LANE), lambda i: (i, 0))],
        out_specs=[row] * 5 + [pl.BlockSpec((8, hw), lambda i: (i, 0))],
        compiler_params=_cparams("parallel"), name="gdn_prep_fwd")(qkv, gb)


def _gdn_prep_bwd(qkv, gb, cts):
    s = qkv.shape[0]
    c = GDN_CHUNK
    n = s // c
    hw = DN_HEADS * LANE

    def body(qkv_ref, gb_ref, du_ref, dw_ref, dqg_ref, dkg_ref, dqk_ref, del_ref, dqkv_ref, dgb_ref):
        _, vjp = jax.vjp(_gdn_prep_fn, _chunks(qkv_ref), gb_ref[...])
        ct = [_chunks(r) for r in (du_ref, dw_ref, dqg_ref, dkg_ref, dqk_ref, del_ref)]
        dqkv, dgb = vjp(ct)
        _store(dqkv_ref, dqkv)
        dgb_ref[...] = dgb

    row = pl.BlockSpec((c, hw), lambda i: (i, 0))
    return pl.pallas_call(
        body, out_shape=[jax.ShapeDtypeStruct(qkv.shape, F32), jax.ShapeDtypeStruct(gb.shape, F32)], grid=(n,),
        in_specs=[pl.BlockSpec((c, 3 * hw), lambda i: (i, 0)), pl.BlockSpec((c, LANE), lambda i: (i, 0))]
        + [row] * 5 + [pl.BlockSpec((8, hw), lambda i: (i, 0))],
        out_specs=[pl.BlockSpec((c, 3 * hw), lambda i: (i, 0)), pl.BlockSpec((c, LANE), lambda i: (i, 0))],
        compiler_params=_cparams("parallel"), name="gdn_prep_bwd")(qkv, gb, *cts)


@jax.custom_vjp
def gdn_prep(qkv, gb):
    return tuple(_gdn_prep_fwd(qkv, gb))


gdn_prep.defvjp(lambda qkv, gb: (tuple(_gdn_prep_fwd(qkv, gb)), (qkv, gb)),
                lambda saved, cts: tuple(_gdn_prep_bwd(*saved, cts)))


def _gdn_scan_fwd(u, w, qg, kg, qk, el):
    s = u.shape[0]
    c = GDN_CHUNK
    n = s // c
    hw = DN_HEADS * LANE

    def body(u_ref, w_ref, qg_ref, kg_ref, qk_ref, el_ref, o_ref, st_ref, s_sc):
        @pl.when(pl.program_id(0) == 0)
        def _():
            s_sc[...] = jnp.zeros_like(s_sc)

        for h in range(DN_HEADS):
            sl = slice(h * LANE, (h + 1) * LANE)
            st = s_sc[h]
            st_ref[sl, :] = st
            v_new = u_ref[:, sl] - _hdot(w_ref[:, sl], st)
            o_ref[:, sl] = _hdot(qg_ref[:, sl], st) + _hdot(qk_ref[:, sl], v_new)
            s_sc[h] = st * el_ref[0:1, sl] + _hdot(kg_ref[:, sl], v_new, _TN)

    row = pl.BlockSpec((c, hw), lambda i: (i, 0))
    return pl.pallas_call(
        body, out_shape=[jax.ShapeDtypeStruct((s, hw), F32), jax.ShapeDtypeStruct((n, hw, LANE), F32)], grid=(n,),
        in_specs=[row] * 5 + [pl.BlockSpec((8, hw), lambda i: (i, 0))],
        out_specs=[row, pl.BlockSpec((None, hw, LANE), lambda i: (i, 0, 0))],
        scratch_shapes=[pltpu.VMEM((DN_HEADS, LANE, LANE), F32)],
        compiler_params=_cparams("arbitrary"), name="gdn_scan_fwd")(u, w, qg, kg, qk, el)


def _gdn_scan_bwd(u, w, qg, kg, qk, el, states, do):
    s = u.shape[0]
    c = GDN_CHUNK
    n = s // c
    hw = DN_HEADS * LANE

    def body(u_ref, w_ref, qg_ref, kg_ref, qk_ref, el_ref, st_ref, do_ref,
             du_ref, dw_ref, dqg_ref, dkg_ref, dqk_ref, del_ref, ds_sc):
        @pl.when(pl.program_id(0) == 0)
        def _():
            ds_sc[...] = jnp.zeros_like(ds_sc)

        for h in range(DN_HEADS):
            sl = slice(h * LANE, (h + 1) * LANE)
            st, ds, do_h = st_ref[sl, :], ds_sc[h], do_ref[:, sl]
            wv, qgv, kgv, qkv_ = w_ref[:, sl], qg_ref[:, sl], kg_ref[:, sl], qk_ref[:, sl]
            v_new = u_ref[:, sl] - _hdot(wv, st)
            dv_new = _hdot(qkv_, do_h, _TN) + _hdot(kgv, ds)
            du_ref[:, sl] = dv_new
            dw_ref[:, sl] = -_hdot(dv_new, st, _NT)
            dqg_ref[:, sl] = _hdot(do_h, st, _NT)
            dqk_ref[:, sl] = _hdot(do_h, v_new, _NT)
            dkg_ref[:, sl] = _hdot(v_new, ds, _NT)
            del_ref[:, sl] = jnp.where(lax.broadcasted_iota(jnp.int32, (8, LANE), 0) == 0,
                                       jnp.sum(st * ds, axis=0, keepdims=True), 0.0)
            ds_sc[h] = _hdot(qgv, do_h, _TN) + ds * el_ref[0:1, sl] - _hdot(wv, dv_new, _TN)

    row = pl.BlockSpec((c, hw), lambda i: (n - 1 - i, 0))
    small = pl.BlockSpec((8, hw), lambda i: (n - 1 - i, 0))
    return pl.pallas_call(
        body, out_shape=[jax.ShapeDtypeStruct((s, hw), F32)] * 5 + [jax.ShapeDtypeStruct((n * 8, hw), F32)], grid=(n,),
        in_specs=[row] * 5 + [small, pl.BlockSpec((None, hw, LANE), lambda i: (n - 1 - i, 0, 0)), row],
        out_specs=[row] * 5 + [small], scratch_shapes=[pltpu.VMEM((DN_HEADS, LANE, LANE), F32)],
        compiler_params=_cparams("arbitrary"), name="gdn_scan_bwd")(u, w, qg, kg, qk, el, states, do)


@jax.custom_vjp
def gdn_scan(u, w, qg, kg, qk, el):
    return _gdn_scan_fwd(u, w, qg, kg, qk, el)[0]


def _gdn_scan_vfwd(*args):
    o, states = _gdn_scan_fwd(*args)
    return o, args + (states,)


gdn_scan.defvjp(_gdn_scan_vfwd, lambda saved, do: tuple(_gdn_scan_bwd(*saved, do)))


def _loss_call(y, t):
    s, d = y.shape
    br = min(ROW_BLOCK, s)
    n = s // br

    def body(y_ref, t_ref, loss_ref, dy_ref, acc):
        i = pl.program_id(0)

        @pl.when(i == 0)
        def _():
            acc[...] = jnp.zeros_like(acc)

        e = y_ref[...] - t_ref[...]
        dy_ref[...] = e / d
        acc[...] += jnp.sum(e * e, axis=0, keepdims=True)

        @pl.when(i == n - 1)
        def _():
            loss_ref[...] = jnp.broadcast_to(jnp.sum(acc[...], axis=1, keepdims=True) * (0.5 / d), loss_ref.shape)

    row = pl.BlockSpec((br, d), lambda i: (i, 0))
    return pl.pallas_call(
        body, out_shape=[jax.ShapeDtypeStruct((1, LANE), F32), jax.ShapeDtypeStruct((s, d), F32)], grid=(n,),
        in_specs=[row, row], out_specs=[pl.BlockSpec((1, LANE), lambda i: (0, 0)), row],
        scratch_shapes=[pltpu.VMEM((1, d), F32)], compiler_params=_cparams("arbitrary"), name="loss_head")(y, t)


@jax.custom_vjp
def loss_head(y, t):
    return _loss_call(y, t)[0][0, 0]


def _loss_head_fwd(y, t):
    loss, dy = _loss_call(y, t)
    return loss[0, 0], (dy,)


loss_head.defvjp(_loss_head_fwd, lambda saved, g: (saved[0] * g, -saved[0] * g))


IN_OFF = {}
_o = 0
for _name, _size in (('q_lat', 384), ('kv_lat', 320), ('z_a', 512), ('dn_qkv', 1536), ('dn_ab', 8), ('z_b', 512),
                     ('dil_qkv', 4608), ('z_c', 512), ('gate', 3072)):
    IN_OFF[_name] = (_o, _o + _size)
    _o += _size
IN_WIDTH = _o


def _cols(w, name):
    a, b = IN_OFF[name]
    return w[:, a:b]


def _pad_cols(w, to):
    return jnp.concatenate([w, jnp.zeros((w.shape[0], to - w.shape[1]), w.dtype)], axis=1)


def _pad_row(v, to=None):
    v = v.reshape(1, -1)
    return v if to is None or v.shape[1] == to else _pad_cols(v, to)


def _layer(x, p, tabs):
    w_in = p['w_in']
    h, = rowwise(f_rms_full, "rms_in", [x], [_pad_row(p['norm_g'])], [(D_MODEL, BF16)])
    w_mla = _pad_cols(jnp.concatenate([_cols(w_in, 'q_lat'), _cols(w_in, 'kv_lat')], axis=1), 768)
    w_z = jnp.concatenate([_cols(w_in, 'z_a'), _cols(w_in, 'z_b'), _cols(w_in, 'z_c')], axis=1)
    w_ab = _pad_cols(_cols(w_in, 'dn_ab'), LANE)
    mla_in = matmul(h, w_mla)
    z = matmul(h, w_z)
    dn_qkv = matmul(h, _cols(w_in, 'dn_qkv'))
    dn_ab = matmul(h, w_ab)
    dil_qkv = matmul(h, _cols(w_in, 'dil_qkv'))
    gate = matmul(h, _cols(w_in, 'gate'))

    qn, kvn, kpe = rowwise(f_mla1, "mla_norm", [mla_in], [_pad_row(p['mla_q_a_norm_g']), _pad_row(p['mla_kv_a_norm_g'])],
                           [(MLA_Q_RANK, BF16), (MLA_KV_RANK, BF16), (LANE, F32)])
    wq = p['mla_w_q_b']
    wq_pad = jnp.concatenate(
        [wq[:, hh * MLA_QK:hh * MLA_QK + MLA_NOPE] for hh in range(MLA_HEADS)]
        + [_pad_cols(wq[:, hh * MLA_QK + MLA_NOPE:(hh + 1) * MLA_QK], LANE) for hh in range(MLA_HEADS)], axis=1)
    q = matmul(qn, wq_pad)
    kv = matmul(kvn, p['mla_w_kv_b'])
    gq, gk = p['mla_q_norm_g'], p['mla_k_norm_g']
    q_att, k_att, v_att = rowwise(
        f_mla2, "mla_qk", [q, kv, kpe, tabs['cos_r'], tabs['sin_r']],
        [_pad_row(gq[:MLA_NOPE]), _pad_row(gq[MLA_NOPE:], LANE), _pad_row(gk[:MLA_NOPE]), _pad_row(gk[MLA_NOPE:], LANE)],
        [(MLA_HEADS * MLA_DQK, BF16), (MLA_HEADS * MLA_DQK, BF16), (MLA_HEADS * HEAD, BF16)], nograd=(3, 4))
    y_a = mla_attention(q_att, k_att, v_att)

    qkv_n = dn_conv(dn_qkv, p['dn_conv_w'])
    gb, = rowwise(f_gates, "dn_gates", [dn_ab], [_pad_row(p['dn_a_log'], LANE), _pad_row(p['dn_dt_bias'], LANE)], [(LANE, F32)])
    o_b = gdn_scan(*gdn_prep(qkv_n, gb))
    y_b, = rowwise(f_headnorm, "dn_out_norm", [o_b], [_pad_row(p['dn_out_norm_g'])], [(DN_HEADS * DN_DIM, F32)])

    qd, kd, vd = rowwise(f_dil, "dil_qk", [dil_qkv, tabs['cos_h'], tabs['sin_h']],
                         [_pad_row(p['dil_q_norm_g']), _pad_row(p['dil_k_norm_g'])],
                         [(3 * GROUP_W, BF16)] * 3, nograd=(1, 2))
    outs, lses = [], []
    for gi, d in enumerate(DIL_DILATIONS):
        sl = slice(gi * GROUP_W, (gi + 1) * GROUP_W)
        o_g, l_g = dilated_group(qd[:, sl], kd[:, sl], vd[:, sl], d)
        outs.append(o_g)
        lses.append(l_g)
    y_c, = rowwise(f_comb, "dil_comb", outs + lses, [], [(GROUP_W, F32)])

    ys = rowwise(f_merge1, "merge_silu", [y_a, y_b, y_c, z], [], [(BRANCH_W, BF16)] * 3)
    bo = [matmul(ys[b], p['w_branch'][b]) for b in range(3)]
    mixed, = rowwise(f_merge2, "merge_gate", [gate] + bo, [], [(D_MODEL, BF16)])
    return matmul(mixed, p['w_out'], res=x)


def _rope_tables(pos, dim):
    inv_freq = 1.0 / (ROPE_THETA ** (jnp.arange(0, dim, 2, dtype=F32) / dim))
    ang = pos.astype(F32)[:, None] * inv_freq
    return jnp.cos(ang), jnp.sin(ang)


def _tables(pos):
    cr, sr = _rope_tables(pos, MLA_ROPE)
    ch, sh = _rope_tables(pos, HEAD)
    zero = jnp.zeros((pos.shape[0], LANE - MLA_ROPE), F32)
    return {'cos_r': jnp.concatenate([cr, cr, zero], axis=1), 'sin_r': jnp.concatenate([sr, sr, zero], axis=1),
            'cos_h': jnp.concatenate([ch, ch], axis=1), 'sin_h': jnp.concatenate([sh, sh], axis=1)}


def _local_loss(big, conv_w, small, x, target, tabs):
    for l in range(DEPTH):
        p = {k: v[l] for k, v in big.items()}
        p.update({k: v[l] for k, v in small.items()})
        p['dn_conv_w'] = conv_w[l]
        x = _layer(x, p, tabs)
    return loss_head(x, target)


def _pack(arrays, dtype, row_tile):
    flat = jnp.concatenate([a.astype(dtype).reshape(-1) for a in arrays])
    rows = -(-flat.shape[0] // (LANE * row_tile)) * row_tile
    flat = jnp.concatenate([flat, jnp.zeros((rows * LANE - flat.shape[0],), dtype)])
    return flat.reshape(rows, LANE)


def _unpack(buf, shapes):
    flat = buf.reshape(-1)
    out, off = [], 0
    for shp in shapes:
        n = math.prod(shp)
        out.append(flat[off:off + n].reshape(shp))
        off += n
    return out


def _full_from_chips(buf4, shard_shapes):
    per_chip = [_unpack(buf4[j], shard_shapes) for j in range(4)]
    return {name: jnp.concatenate([per_chip[j][i] for j in range(4)], axis=axis) for i, (name, axis) in enumerate(BIG)}


_HBM = pl.BlockSpec(memory_space=pltpu.HBM)
_VMEM = pl.BlockSpec(memory_space=pltpu.VMEM)


def _chip_peers(x, y):
    return [(1 - x, y), (x, 1 - y), (1 - x, 1 - y)]


def chip_all_gather(shard):
    def body(x_ref, out_ref, send_sems, recv_sems, local_sem):
        x, y, c = lax.axis_index("x"), lax.axis_index("y"), lax.axis_index("c")
        me = 2 * x + y
        mine = pltpu.make_async_copy(x_ref, out_ref.at[me], local_sem)
        mine.start()
        sends = [pltpu.make_async_remote_copy(src_ref=x_ref, dst_ref=out_ref.at[me], send_sem=send_sems.at[k],
                                              recv_sem=recv_sems.at[k], device_id=(px, py, c), device_id_type=MESH)
                 for k, (px, py) in enumerate(_chip_peers(x, y))]
        for cp in sends:
            cp.start()
        for k, (px, py) in enumerate(_chip_peers(x, y)):
            pltpu.make_async_remote_copy(src_ref=x_ref, dst_ref=out_ref.at[2 * px + py], send_sem=send_sems.at[k],
                                         recv_sem=recv_sems.at[k], device_id=(px, py, c), device_id_type=MESH).wait_recv()
        for cp in sends:
            cp.wait_send()
        mine.wait()

    return pl.pallas_call(
        body, out_shape=jax.ShapeDtypeStruct((4,) + shard.shape, shard.dtype), in_specs=[_HBM], out_specs=_HBM,
        scratch_shapes=[pltpu.SemaphoreType.DMA((3,)), pltpu.SemaphoreType.DMA((3,)), pltpu.SemaphoreType.DMA],
        name="chip_all_gather")(shard)


def chip_all_to_all(pieces):
    def body(g_ref, out_ref, send_sems, recv_sems, local_sem):
        x, y, c = lax.axis_index("x"), lax.axis_index("y"), lax.axis_index("c")
        me = 2 * x + y
        mine = pltpu.make_async_copy(g_ref.at[me], out_ref.at[me], local_sem)
        mine.start()
        sends = [pltpu.make_async_remote_copy(src_ref=g_ref.at[2 * px + py], dst_ref=out_ref.at[me], send_sem=send_sems.at[k],
                                              recv_sem=recv_sems.at[k], device_id=(px, py, c), device_id_type=MESH)
                 for k, (px, py) in enumerate(_chip_peers(x, y))]
        for cp in sends:
            cp.start()
        for k, (px, py) in enumerate(_chip_peers(x, y)):
            pltpu.make_async_remote_copy(src_ref=g_ref.at[me], dst_ref=out_ref.at[2 * px + py], send_sem=send_sems.at[k],
                                         recv_sem=recv_sems.at[k], device_id=(px, py, c), device_id_type=MESH).wait_recv()
        for cp in sends:
            cp.wait_send()
        mine.wait()

    return pl.pallas_call(
        body, out_shape=jax.ShapeDtypeStruct(pieces.shape, pieces.dtype), in_specs=[_HBM], out_specs=_HBM,
        scratch_shapes=[pltpu.SemaphoreType.DMA((3,)), pltpu.SemaphoreType.DMA((3,)), pltpu.SemaphoreType.DMA],
        name="chip_all_to_all")(pieces)


def sibling_swap(v):
    def body(v_ref, out_ref, send_sem, recv_sem):
        x, y, c = lax.axis_index("x"), lax.axis_index("y"), lax.axis_index("c")
        cp = pltpu.make_async_remote_copy(src_ref=v_ref, dst_ref=out_ref, send_sem=send_sem, recv_sem=recv_sem,
                                          device_id=(x, y, 1 - c), device_id_type=MESH)
        cp.start()
        cp.wait()

    return pl.pallas_call(
        body, out_shape=jax.ShapeDtypeStruct(v.shape, v.dtype), in_specs=[_HBM], out_specs=_HBM,
        scratch_shapes=[pltpu.SemaphoreType.DMA, pltpu.SemaphoreType.DMA], name="sibling_swap")(v)


def all_gather8(v, name):
    def body(v_ref, out_ref, send_sems, recv_sems):
        x, y, c = lax.axis_index("x"), lax.axis_index("y"), lax.axis_index("c")
        out_ref[4 * x + 2 * y + c] = v_ref[...]

        def peer(k):
            return (x ^ (k >> 2), y ^ ((k >> 1) & 1), c ^ (k & 1))

        sends = [pltpu.make_async_remote_copy(src_ref=v_ref, dst_ref=out_ref.at[4 * x + 2 * y + c], send_sem=send_sems.at[k - 1],
                                              recv_sem=recv_sems.at[k - 1], device_id=peer(k), device_id_type=MESH)
                 for k in range(1, 8)]
        for cp in sends:
            cp.start()
        for k in range(1, 8):
            px, py, pc = peer(k)
            pltpu.make_async_remote_copy(src_ref=v_ref, dst_ref=out_ref.at[4 * px + 2 * py + pc], send_sem=send_sems.at[k - 1],
                                         recv_sem=recv_sems.at[k - 1], device_id=peer(k), device_id_type=MESH).wait_recv()
        for cp in sends:
            cp.wait_send()

    return pl.pallas_call(
        body, out_shape=jax.ShapeDtypeStruct((8,) + v.shape, v.dtype), in_specs=[_VMEM], out_specs=_VMEM,
        scratch_shapes=[pltpu.SemaphoreType.DMA((7,)), pltpu.SemaphoreType.DMA((7,))], name=name)(v)


def sum_blocks(blocks, row_tile, name):
    n, rows, _ = blocks.shape

    def body(b_ref, o_ref):
        acc = b_ref[0].astype(F32)
        for j in range(1, n):
            acc = acc + b_ref[j].astype(F32)
        o_ref[...] = acc

    return pl.pallas_call(
        body, out_shape=jax.ShapeDtypeStruct((rows, LANE), F32), grid=(rows // row_tile,),
        in_specs=[pl.BlockSpec((n, row_tile, LANE), lambda i: (0, i, 0))],
        out_specs=pl.BlockSpec((row_tile, LANE), lambda i: (i, 0)), compiler_params=_cparams("parallel"), name=name)(blocks)


def adamw(g_parts, w, m, v, row_tile, name):
    npart = len(g_parts)

    def body(*refs):
        g = refs[0][...]
        for r in refs[1:npart]:
            g = g + r[...]
        w_ref, m_ref, v_ref, g_out, d_out, m_out, v_out = refs[npart:]
        m_new = ADAM_B1 * m_ref[...] + (1.0 - ADAM_B1) * g
        v_new = ADAM_B2 * v_ref[...] + (1.0 - ADAM_B2) * (g * g)
        m_hat = m_new / (1.0 - ADAM_B1 ** ADAM_STEP)
        v_hat = v_new / (1.0 - ADAM_B2 ** ADAM_STEP)
        g_out[...] = g
        d_out[...] = -ADAM_LR * (m_hat / (jnp.sqrt(v_hat) + ADAM_EPS) + ADAM_WD * w_ref[...])
        m_out[...] = m_new
        v_out[...] = v_new

    rows = w.shape[0]
    spec = pl.BlockSpec((row_tile, LANE), lambda i: (i, 0))
    return pl.pallas_call(
        body, out_shape=[jax.ShapeDtypeStruct(w.shape, F32)] * 4, grid=(rows // row_tile,),
        in_specs=[spec] * (npart + 3), out_specs=[spec] * 4, compiler_params=_cparams("parallel"), name=name)(*g_parts, w, m, v)


def kernel(x, positions, norm_g, w_in, mla_q_a_norm_g, mla_w_q_b, mla_kv_a_norm_g, mla_w_kv_b, mla_q_norm_g, mla_k_norm_g, dn_conv_w, dn_a_log, dn_dt_bias, dn_out_norm_g, dil_q_norm_g, dil_k_norm_g, w_branch, w_out, loss_target, m_norm_g, m_w_in, m_mla_q_a_norm_g, m_mla_w_q_b, m_mla_kv_a_norm_g, m_mla_w_kv_b, m_mla_q_norm_g, m_mla_k_norm_g, m_dn_conv_w, m_dn_a_log, m_dn_dt_bias, m_dn_out_norm_g, m_dil_q_norm_g, m_dil_k_norm_g, m_w_branch, m_w_out, v_norm_g, v_w_in, v_mla_q_a_norm_g, v_mla_w_q_b, v_mla_kv_a_norm_g, v_mla_w_kv_b, v_mla_q_norm_g, v_mla_k_norm_g, v_dn_conv_w, v_dn_a_log, v_dn_dt_bias, v_dn_out_norm_g, v_dil_q_norm_g, v_dil_k_norm_g, v_w_branch, v_w_out):
    w = dict(norm_g=norm_g, w_in=w_in, mla_q_a_norm_g=mla_q_a_norm_g, mla_w_q_b=mla_w_q_b, mla_kv_a_norm_g=mla_kv_a_norm_g,
             mla_w_kv_b=mla_w_kv_b, mla_q_norm_g=mla_q_norm_g, mla_k_norm_g=mla_k_norm_g, dn_conv_w=dn_conv_w, dn_a_log=dn_a_log,
             dn_dt_bias=dn_dt_bias, dn_out_norm_g=dn_out_norm_g, dil_q_norm_g=dil_q_norm_g, dil_k_norm_g=dil_k_norm_g,
             w_branch=w_branch, w_out=w_out)
    m = dict(norm_g=m_norm_g, w_in=m_w_in, mla_q_a_norm_g=m_mla_q_a_norm_g, mla_w_q_b=m_mla_w_q_b, mla_kv_a_norm_g=m_mla_kv_a_norm_g,
             mla_w_kv_b=m_mla_w_kv_b, mla_q_norm_g=m_mla_q_norm_g, mla_k_norm_g=m_mla_k_norm_g, dn_conv_w=m_dn_conv_w,
             dn_a_log=m_dn_a_log, dn_dt_bias=m_dn_dt_bias, dn_out_norm_g=m_dn_out_norm_g, dil_q_norm_g=m_dil_q_norm_g,
             dil_k_norm_g=m_dil_k_norm_g, w_branch=m_w_branch, w_out=m_w_out)
    v = dict(norm_g=v_norm_g, w_in=v_w_in, mla_q_a_norm_g=v_mla_q_a_norm_g, mla_w_q_b=v_mla_w_q_b, mla_kv_a_norm_g=v_mla_kv_a_norm_g,
             mla_w_kv_b=v_mla_w_kv_b, mla_q_norm_g=v_mla_q_norm_g, mla_k_norm_g=v_mla_k_norm_g, dn_conv_w=v_dn_conv_w,
             dn_a_log=v_dn_a_log, dn_dt_bias=v_dn_dt_bias, dn_out_norm_g=v_dn_out_norm_g, dil_q_norm_g=v_dil_q_norm_g,
             dil_k_norm_g=v_dil_k_norm_g, w_branch=v_w_branch, w_out=v_w_out)
    chip = 2 * lax.axis_index("x") + lax.axis_index("y")
    big_names = [n for n, _ in BIG]
    shard_shapes = [w[n].shape for n in big_names]
    conv_shard = dn_conv_w.shape

    big4 = chip_all_gather(_pack([w[n] for n in big_names], BF16, PACK_ROWS))
    conv8 = all_gather8(_pack([dn_conv_w], F32, 8), "gather_conv_w")
    conv_full = jnp.concatenate([_unpack(conv8[2 * j], [conv_shard])[0] for j in range(4)], axis=2)
    small = {n: w[n] for n in SMALL}
    tabs = _tables(positions[0])

    def loss_fn(big4, conv_full, small, xs):
        return _local_loss(_full_from_chips(big4, shard_shapes), conv_full, small, xs, loss_target[0], tabs)

    loss, (g_big4, g_conv, g_small, g_x) = jax.value_and_grad(loss_fn, argnums=(0, 1, 2, 3))(big4, conv_full, small, x[0])
    loss = lax.psum(loss, ("x", "y", "c"))

    part = sum_blocks(chip_all_to_all(g_big4), PACK_ROWS, "sum_chip_pieces")
    other = sibling_swap(part)
    packed = [_pack([d[n] for n in big_names], F32, PACK_ROWS) for d in (w, m, v)]
    res_big = [dict(zip(big_names, _unpack(r, shard_shapes))) for r in adamw([part, other], *packed, PACK_ROWS, "adamw_big")]

    small_shapes = [w[n].shape for n in SMALL] + [g_conv.shape]
    g_all = sum_blocks(all_gather8(_pack([g_small[n] for n in SMALL] + [g_conv], F32, 8), "gather_small_grads"), 8, "sum_small")
    g_list = _unpack(g_all, small_shapes)
    g_list[-1] = lax.dynamic_slice_in_dim(g_list[-1], chip * conv_shard[2], conv_shard[2], axis=2)
    small_names = list(SMALL) + ['dn_conv_w']
    packed_s = [_pack([d[n] for n in small_names], F32, 8) for d in (w, m, v)]
    shapes_s = [w[n].shape for n in small_names]
    res_small = [dict(zip(small_names, _unpack(r, shapes_s))) for r in adamw([_pack(g_list, F32, 8)], *packed_s, 8, "adamw_small")]

    outs = [loss, g_x[None]]
    for k in range(4):
        outs += [res_big[k][n] if n in res_big[k] else res_small[k][n] for n in WEIGHTS]
    return tuple(outs)
```

```python
import functools
import math

import jax
import jax.numpy as jnp
from jax import lax
from jax.experimental import pallas as pl
from jax.experimental.pallas import tpu as pltpu

F32 = jnp.float32
BF16 = jnp.bfloat16
HI = lax.Precision.HIGHEST
MESH = pl.DeviceIdType.MESH

LANE = 128
VMEM_LIMIT = 48 * 1024 * 1024
ROW_BLOCK = 256
MAT_ROWS = 1024
WIN_ROWS = 128

RMS_EPS = 1e-6
ROPE_THETA = 10000.0
D_MODEL = 1024
DEPTH = 2
MLA_HEADS = 4
MLA_NOPE = 128
MLA_ROPE = 64
MLA_QK = MLA_NOPE + MLA_ROPE
MLA_Q_RANK = 384
MLA_KV_RANK = 256
DN_HEADS = 4
DN_DIM = 128
DN_CONV = 4
GDN_CHUNK = 128
DIL_WINDOWS = (128, 512, 2048)
DIL_DILATIONS = (1, 4, 16)
DIL_GROUPS = 3
DIL_BLOCK = 128
HEAD = 128
BRANCH_W = 512

ADAM_LR = 0.001
ADAM_B1 = 0.9
ADAM_B2 = 0.999
ADAM_EPS = 1e-08
ADAM_WD = 0.01
ADAM_STEP = 10

WEIGHTS = ['norm_g', 'w_in', 'mla_q_a_norm_g', 'mla_w_q_b', 'mla_kv_a_norm_g', 'mla_w_kv_b', 'mla_q_norm_g',
           'mla_k_norm_g', 'dn_conv_w', 'dn_a_log', 'dn_dt_bias', 'dn_out_norm_g', 'dil_q_norm_g', 'dil_k_norm_g',
           'w_branch', 'w_out']
MATS = (('mla_w_q_b', 2), ('mla_w_kv_b', 2), ('w_branch', 3), ('w_out', 1))
SMALL = ('norm_g', 'mla_q_a_norm_g', 'mla_kv_a_norm_g', 'mla_q_norm_g', 'mla_k_norm_g', 'dn_a_log', 'dn_dt_bias',
         'dn_out_norm_g', 'dil_q_norm_g', 'dil_k_norm_g')


def _cparams(*sem):
    return pltpu.CompilerParams(dimension_semantics=sem or None, vmem_limit_bytes=VMEM_LIMIT)


def _tile(dim, target):
    best = 0
    for t in range(LANE, min(dim, target) + 1, LANE):
        if dim % t == 0:
            best = t
    assert best, (dim, target)
    return best


def _mm(a, b, mode, out_dtype, res=None):
    if mode == 'nn':
        (m, k), (k2, n) = a.shape, b.shape
    elif mode == 'nt':
        (m, k), (n, k2) = a.shape, b.shape
    else:
        (k, m), (k2, n) = a.shape, b.shape
    assert k == k2, (a.shape, b.shape, mode)
    tm, tn, tk = _tile(m, 512), _tile(n, 1024), _tile(k, 1024)
    nk = k // tk
    dims = {'nn': (((1,), (0,)), ((), ())), 'nt': (((1,), (1,)), ((), ())), 'tn': (((0,), (0,)), ((), ()))}[mode]

    def body(*refs):
        if res is None:
            a_ref, b_ref, o_ref, acc = refs
        else:
            a_ref, b_ref, r_ref, o_ref, acc = refs
        kk = pl.program_id(2)

        @pl.when(kk == 0)
        def _():
            acc[...] = jnp.zeros_like(acc)

        acc[...] += lax.dot_general(a_ref[...].astype(BF16), b_ref[...].astype(BF16), dims,
                                    preferred_element_type=F32)

        @pl.when(kk == nk - 1)
        def _():
            r = acc[...]
            if res is not None:
                r = r + r_ref[...].astype(F32)
            o_ref[...] = r.astype(o_ref.dtype)

    a_spec = pl.BlockSpec((tk, tm), lambda i, j, kk: (kk, i)) if mode == 'tn' else pl.BlockSpec((tm, tk), lambda i, j, kk: (i, kk))
    b_spec = pl.BlockSpec((tn, tk), lambda i, j, kk: (j, kk)) if mode == 'nt' else pl.BlockSpec((tk, tn), lambda i, j, kk: (kk, j))
    o_spec = pl.BlockSpec((tm, tn), lambda i, j, kk: (i, j))
    in_specs = [a_spec, b_spec] + ([o_spec] if res is not None else [])
    args = (a, b) + ((res,) if res is not None else ())
    return pl.pallas_call(
        body, out_shape=jax.ShapeDtypeStruct((m, n), out_dtype), grid=(m // tm, n // tn, nk),
        in_specs=in_specs, out_specs=o_spec, scratch_shapes=[pltpu.VMEM((tm, tn), F32)],
        compiler_params=_cparams("parallel", "parallel", "arbitrary"),
        name=f"mm_{mode}_{m}x{k}x{n}" + ("_res" if res is not None else ""))(*args)


def _make_matmul(out_dtype, with_res):
    @jax.custom_vjp
    def mm(a, b, *r):
        return _mm(a, b, 'nn', out_dtype, *r)

    def fwd(a, b, *r):
        return mm(a, b, *r), (a, b)

    def bwd(saved, g):
        a, b = saved
        da = _mm(g, b, 'nt', a.dtype)
        db = _mm(a, g, 'tn', b.dtype)
        return (da, db) + ((g,) if with_res else ())

    mm.defvjp(fwd, bwd)
    return mm


def matmul(a, b, out_dtype=F32, res=None):
    if res is None:
        return _make_matmul(out_dtype, False)(a, b)
    return _make_matmul(out_dtype, True)(a, b, res)


def _chunks(ref):
    return [ref[:, c * LANE:(c + 1) * LANE].astype(F32) for c in range(ref.shape[-1] // LANE)]


def _store(ref, chunks):
    for c, ch in enumerate(chunks):
        ref[:, c * LANE:(c + 1) * LANE] = ch.astype(ref.dtype)


def _row_spec(width, br):
    return pl.BlockSpec((br, width), lambda i: (i, 0))


def _par_spec(width):
    return pl.BlockSpec((1, width), lambda i: (0, 0))


def _rw_fwd(f, name, rows, params, outs, br):
    s = rows[0].shape[0]
    br = min(br, s)
    nr, npar = len(rows), len(params)

    def body(*refs):
        rc = [_chunks(r) for r in refs[:nr]]
        pc = [_chunks(p) for p in refs[nr:nr + npar]]
        res = f(rc, pc)
        for o_ref, chs in zip(refs[nr + npar:], res):
            _store(o_ref, chs)

    return pl.pallas_call(
        body, out_shape=[jax.ShapeDtypeStruct((s, w), dt) for w, dt in outs], grid=(s // br,),
        in_specs=[_row_spec(r.shape[1], br) for r in rows] + [_par_spec(p.shape[1]) for p in params],
        out_specs=[_row_spec(w, br) for w, _ in outs],
        compiler_params=_cparams("parallel"), name=name + "_fwd")(*rows, *params)


def _rw_bwd(f, name, rows, params, cts, nograd, br):
    s = rows[0].shape[0]
    br = min(br, s)
    nr, npar, nct = len(rows), len(params), len(cts)
    grad_rows = [i for i in range(nr) if i not in nograd]

    def body(*refs):
        rc = [_chunks(r) for r in refs[:nr]]
        pc = [_chunks(p) for p in refs[nr:nr + npar]]
        ct = [_chunks(c) for c in refs[nr + npar:nr + npar + nct]]
        out_refs = refs[nr + npar + nct:]
        _, vjp = jax.vjp(f, rc, pc)
        drc, dpc = vjp(ct)
        for o_ref, i in zip(out_refs[:len(grad_rows)], grad_rows):
            _store(o_ref, drc[i])
        i0 = pl.program_id(0)
        for o_ref, chs in zip(out_refs[len(grad_rows):], dpc):
            @pl.when(i0 == 0)
            def _(o_ref=o_ref):
                o_ref[...] = jnp.zeros_like(o_ref)
            for c, ch in enumerate(chs):
                o_ref[:, c * LANE:(c + 1) * LANE] += ch

    out_shape = ([jax.ShapeDtypeStruct(rows[i].shape, rows[i].dtype) for i in grad_rows]
                 + [jax.ShapeDtypeStruct(p.shape, F32) for p in params])
    out_specs = [_row_spec(rows[i].shape[1], br) for i in grad_rows] + [_par_spec(p.shape[1]) for p in params]
    res = pl.pallas_call(
        body, out_shape=out_shape, grid=(s // br,),
        in_specs=([_row_spec(r.shape[1], br) for r in rows] + [_par_spec(p.shape[1]) for p in params]
                  + [_row_spec(c.shape[1], br) for c in cts]),
        out_specs=out_specs, compiler_params=_cparams("arbitrary"), name=name + "_bwd")(*rows, *params, *cts)
    drows = [None] * nr
    for o, i in zip(res[:len(grad_rows)], grad_rows):
        drows[i] = o
    for i in nograd:
        drows[i] = jnp.zeros_like(rows[i])
    return tuple(drows), tuple(res[len(grad_rows):])


def rowwise(f, name, rows, params, outs, nograd=(), br=ROW_BLOCK):
    @jax.custom_vjp
    def op(rows, params):
        return tuple(_rw_fwd(f, name, rows, params, outs, br))

    def fwd(rows, params):
        return op(rows, params), (rows, params)

    def bwd(saved, cts):
        rows, params = saved
        return _rw_bwd(f, name, rows, params, list(cts), nograd, br)

    op.defvjp(fwd, bwd)
    return op(tuple(rows), tuple(params))


def _lane_roll(x, s):
    w = x.shape[-1]

    @jax.custom_vjp
    def r(v):
        return pltpu.roll(v, s, 1)

    r.defvjp(lambda v: (r(v), None), lambda _, g: (pltpu.roll(g, (w - s) % w, 1),))
    return r(x)


def _lanes(x):
    return lax.broadcasted_iota(jnp.int32, x.shape, 1)


def _rms(x, g, n=LANE):
    return x * lax.rsqrt(jnp.sum(x * x, axis=-1, keepdims=True) / n + RMS_EPS) * g


def _rope128(x, cos, sin):
    lane = _lanes(x)
    rot = _lane_roll(x, 64) * jnp.where(lane < 64, -1.0, 1.0)
    return x * cos + rot * sin


def _rope64(x, cos, sin):
    lane = _lanes(x)
    rot = jnp.where(lane < 32, -_lane_roll(x, 96), jnp.where(lane < 64, _lane_roll(x, 32), 0.0))
    return x * cos + rot * sin


def _silu(z):
    return z * jax.nn.sigmoid(z)


def _softplus(x):
    return jnp.maximum(x, 0.0) + jnp.log1p(jnp.exp(-jnp.abs(x)))


def f_rms_full(rc, pc):
    x, g = rc[0], pc[0]
    width = len(x) * LANE
    ms = sum(jnp.sum(c * c, axis=-1, keepdims=True) for c in x) / width
    r = lax.rsqrt(ms + RMS_EPS)
    return [[c * r * gc for c, gc in zip(x, g)]]


def f_mla1(rc, pc):
    x = rc[0]
    qn = f_rms_full([x[0:3]], [pc[0]])[0]
    kvn = f_rms_full([x[3:5]], [pc[1]])[0]
    return [qn, kvn, [x[5]]]


def f_mla2(rc, pc):
    q, kv, kpe, cos, sin = rc[0], rc[1], rc[2][0], rc[3][0], rc[4][0]
    gqn, gqp, gkn, gkp = pc[0][0], pc[1][0], pc[2][0], pc[3][0]
    k_pe = _rope64(_rms(kpe, gkp, MLA_ROPE), cos, sin)
    q_att, k_att, v = [], [], []
    for h in range(MLA_HEADS):
        q_att += [_rms(q[h], gqn), _rope64(_rms(q[MLA_HEADS + h], gqp, MLA_ROPE), cos, sin)]
        k_att += [_rms(kv[2 * h], gkn), k_pe]
        v.append(kv[2 * h + 1])
    return [q_att, k_att, v]


def f_gates(rc, pc):
    x, a_log, dt_bias = rc[0][0], pc[0][0], pc[1][0]
    lane = _lanes(x)
    g = -jnp.exp(a_log) * _softplus(x + dt_bias)
    return [[jnp.where(lane < DN_HEADS, g, jnp.where(lane < 2 * DN_HEADS, jax.nn.sigmoid(x), 0.0))]]


def f_headnorm(rc, pc):
    return [[_rms(c, pc[0][0]) for c in rc[0]]]


def f_dil(rc, pc):
    x, cos, sin = rc[0], rc[1][0], rc[2][0]
    gq, gk = pc[0][0], pc[1][0]
    n = len(x) // 3
    q = [_rope128(_rms(c, gq), cos, sin) for c in x[:n]]
    k = [_rope128(_rms(c, gk), cos, sin) for c in x[n:2 * n]]
    return [q, k, list(x[2 * n:])]


def f_comb(rc, pc):
    o, l = rc[:DIL_GROUPS], rc[DIL_GROUPS:]
    out = []
    for c in range(len(o[0])):
        m = functools.reduce(jnp.maximum, [lg[c] for lg in l])
        e = [jnp.exp(lg[c] - m) for lg in l]
        den = sum(e)
        out.append(sum(eg * og[c] for eg, og in zip(e, o)) / den)
    return [out]


def f_merge1(rc, pc):
    z = rc[3]
    n = len(rc[0])
    return [[y * _silu(z[b * n + c]) for c, y in enumerate(rc[b])] for b in range(3)]


def f_merge2(rc, pc):
    gate = rc[0]
    n = len(rc[1])
    return [[sum(jax.nn.sigmoid(gate[b * n + c]) * rc[1 + b][c] for b in range(3)) for c in range(n)]]


MLA_DQK = 2 * LANE
MLA_SCALE = MLA_QK ** -0.5


def _causal_mask(qi, kj, tq, tk):
    qpos = qi * tq + lax.broadcasted_iota(jnp.int32, (tq, tk), 0)
    kpos = kj * tk + lax.broadcasted_iota(jnp.int32, (tq, tk), 1)
    return kpos <= qpos


def _mla_attn_fwd(q, k, v):
    s = q.shape[0]
    h = q.shape[1] // MLA_DQK
    t = min(512, s)
    n = s // t

    def body(q_ref, k_ref, v_ref, o_ref, lse_ref, m_sc, l_sc, acc_sc):
        qi, kj = pl.program_id(1), pl.program_id(2)

        @pl.when(kj == 0)
        def _():
            m_sc[...] = jnp.full_like(m_sc, -jnp.inf)
            l_sc[...] = jnp.zeros_like(l_sc)
            acc_sc[...] = jnp.zeros_like(acc_sc)

        @pl.when(kj <= qi)
        def _():
            sc = lax.dot_general(q_ref[...], k_ref[...], (((1,), (1,)), ((), ())), preferred_element_type=F32) * MLA_SCALE
            sc = jnp.where(_causal_mask(qi, kj, t, t), sc, -jnp.inf)
            m_new = jnp.maximum(m_sc[...], jnp.max(sc, axis=-1, keepdims=True))
            alpha = jnp.exp(m_sc[...] - m_new)
            p = jnp.exp(sc - m_new)
            l_sc[...] = alpha * l_sc[...] + jnp.sum(p, axis=-1, keepdims=True)
            acc_sc[...] = alpha * acc_sc[...] + jnp.dot(p.astype(BF16), v_ref[...], preferred_element_type=F32)
            m_sc[...] = m_new

        @pl.when(kj == n - 1)
        def _():
            o_ref[...] = acc_sc[...] / l_sc[...]
            lse_ref[...] = jnp.broadcast_to(m_sc[...] + jnp.log(l_sc[...]), lse_ref.shape)

    return pl.pallas_call(
        body, out_shape=[jax.ShapeDtypeStruct((s, h * HEAD), F32)] * 2, grid=(h, n, n),
        in_specs=[pl.BlockSpec((t, MLA_DQK), lambda hh, i, j: (i, hh)),
                  pl.BlockSpec((t, MLA_DQK), lambda hh, i, j: (jnp.minimum(j, i), hh)),
                  pl.BlockSpec((t, HEAD), lambda hh, i, j: (jnp.minimum(j, i), hh))],
        out_specs=[pl.BlockSpec((t, HEAD), lambda hh, i, j: (i, hh))] * 2,
        scratch_shapes=[pltpu.VMEM((t, 1), F32), pltpu.VMEM((t, 1), F32), pltpu.VMEM((t, HEAD), F32)],
        compiler_params=_cparams("parallel", "parallel", "arbitrary"), name="mla_attn_fwd")(q, k, v)


def _mla_p_ds(q_ref, k_ref, v_ref, o_ref, lse_ref, do_ref, qi, kj, t):
    sc = lax.dot_general(q_ref[...], k_ref[...], (((1,), (1,)), ((), ())), preferred_element_type=F32) * MLA_SCALE
    p = jnp.where(_causal_mask(qi, kj, t, t), jnp.exp(sc - lse_ref[:, 0:1]), 0.0)
    do = do_ref[...]
    dp = lax.dot_general(do.astype(BF16), v_ref[...], (((1,), (1,)), ((), ())), preferred_element_type=F32)
    delta = jnp.sum(do * o_ref[...], axis=-1, keepdims=True)
    ds = p * (dp - delta) * MLA_SCALE
    return p, ds, do


def _mla_attn_bwd(q, k, v, o, lse, do):
    s = q.shape[0]
    h = q.shape[1] // MLA_DQK
    t = min(512, s)
    n = s // t

    def body_kv(q_ref, k_ref, v_ref, o_ref, lse_ref, do_ref, dk_ref, dv_ref, dk_sc, dv_sc):
        kj, qi = pl.program_id(1), pl.program_id(2)

        @pl.when(qi == 0)
        def _():
            dk_sc[...] = jnp.zeros_like(dk_sc)
            dv_sc[...] = jnp.zeros_like(dv_sc)

        @pl.when(qi >= kj)
        def _():
            p, ds, do = _mla_p_ds(q_ref, k_ref, v_ref, o_ref, lse_ref, do_ref, qi, kj, t)
            dv_sc[...] += lax.dot_general(p.astype(BF16), do.astype(BF16), (((0,), (0,)), ((), ())), preferred_element_type=F32)
            dk_sc[...] += lax.dot_general(ds.astype(BF16), q_ref[...], (((0,), (0,)), ((), ())), preferred_element_type=F32)

        @pl.when(qi == n - 1)
        def _():
            dk_ref[...] = dk_sc[...].astype(dk_ref.dtype)
            dv_ref[...] = dv_sc[...].astype(dv_ref.dtype)

    qmap = lambda hh, j, i: (jnp.maximum(i, j), hh)
    kmap = lambda hh, j, i: (j, hh)
    dk, dv = pl.pallas_call(
        body_kv, out_shape=[jax.ShapeDtypeStruct(k.shape, BF16), jax.ShapeDtypeStruct(v.shape, BF16)], grid=(h, n, n),
        in_specs=[pl.BlockSpec((t, MLA_DQK), qmap), pl.BlockSpec((t, MLA_DQK), kmap), pl.BlockSpec((t, HEAD), kmap),
                  pl.BlockSpec((t, HEAD), qmap), pl.BlockSpec((t, HEAD), qmap), pl.BlockSpec((t, HEAD), qmap)],
        out_specs=[pl.BlockSpec((t, MLA_DQK), kmap), pl.BlockSpec((t, HEAD), kmap)],
        scratch_shapes=[pltpu.VMEM((t, MLA_DQK), F32), pltpu.VMEM((t, HEAD), F32)],
        compiler_params=_cparams("parallel", "parallel", "arbitrary"), name="mla_attn_bwd_kv")(q, k, v, o, lse, do)

    def body_q(q_ref, k_ref, v_ref, o_ref, lse_ref, do_ref, dq_ref, dq_sc):
        qi, kj = pl.program_id(1), pl.program_id(2)

        @pl.when(kj == 0)
        def _():
            dq_sc[...] = jnp.zeros_like(dq_sc)

        @pl.when(kj <= qi)
        def _():
            _, ds, _ = _mla_p_ds(q_ref, k_ref, v_ref, o_ref, lse_ref, do_ref, qi, kj, t)
            dq_sc[...] += jnp.dot(ds.astype(BF16), k_ref[...], preferred_element_type=F32)

        @pl.when(kj == n - 1)
        def _():
            dq_ref[...] = dq_sc[...].astype(dq_ref.dtype)

    qmap2 = lambda hh, i, j: (i, hh)
    kmap2 = lambda hh, i, j: (jnp.minimum(j, i), hh)
    dq = pl.pallas_call(
        body_q, out_shape=jax.ShapeDtypeStruct(q.shape, BF16), grid=(h, n, n),
        in_specs=[pl.BlockSpec((t, MLA_DQK), qmap2), pl.BlockSpec((t, MLA_DQK), kmap2), pl.BlockSpec((t, HEAD), kmap2),
                  pl.BlockSpec((t, HEAD), qmap2), pl.BlockSpec((t, HEAD), qmap2), pl.BlockSpec((t, HEAD), qmap2)],
        out_specs=pl.BlockSpec((t, MLA_DQK), qmap2), scratch_shapes=[pltpu.VMEM((t, MLA_DQK), F32)],
        compiler_params=_cparams("parallel", "parallel", "arbitrary"), name="mla_attn_bwd_q")(q, k, v, o, lse, do)
    return dq, dk, dv


@jax.custom_vjp
def mla_attention(q, k, v):
    return _mla_attn_fwd(q, k, v)[0]


def _mla_attention_fwd(q, k, v):
    o, lse = _mla_attn_fwd(q, k, v)
    return o, (q, k, v, o, lse)


def _mla_attention_bwd(saved, do):
    return _mla_attn_bwd(*saved, do)


mla_attention.defvjp(_mla_attention_fwd, _mla_attention_bwd)


DIL_SCALE = HEAD ** -0.5
GROUP_W = 4 * HEAD


def _dil_scores(q, kp, kc, n):
    dn = (((1,), (1,)), ((), ()))
    sp = lax.dot_general(q, kp, dn, preferred_element_type=F32) * DIL_SCALE
    sc = lax.dot_general(q, kc, dn, preferred_element_type=F32) * DIL_SCALE
    qi = lax.broadcasted_iota(jnp.int32, sp.shape, 0)
    kc_i = lax.broadcasted_iota(jnp.int32, sp.shape, 1)
    vp = jnp.logical_and(kc_i >= qi, n > 0)
    vc = kc_i <= qi
    return sp, sc, vp, vc


def _dil_specs(d):
    cur = pl.BlockSpec((DIL_BLOCK, GROUP_W), lambda r, n: (n, r))
    prev = pl.BlockSpec((DIL_BLOCK, GROUP_W), lambda r, n: (jnp.maximum(n - 1, 0), r))
    return cur, prev


def _dil_fwd(q, k, v, d):
    l = q.shape[0]
    nb = l // DIL_BLOCK
    cur, prev = _dil_specs(d)

    def body(q_ref, kp_ref, kc_ref, vp_ref, vc_ref, o_ref, lse_ref):
        n = pl.program_id(1)
        for h in range(4):
            sl = slice(h * HEAD, (h + 1) * HEAD)
            sp, sc, vp, vc = _dil_scores(q_ref[:, sl], kp_ref[:, sl], kc_ref[:, sl], n)
            sp = jnp.where(vp, sp, -jnp.inf)
            sc = jnp.where(vc, sc, -jnp.inf)
            m = jnp.maximum(jnp.max(sp, axis=-1, keepdims=True), jnp.max(sc, axis=-1, keepdims=True))
            ep, ec = jnp.exp(sp - m), jnp.exp(sc - m)
            den = jnp.sum(ep, axis=-1, keepdims=True) + jnp.sum(ec, axis=-1, keepdims=True)
            acc = (jnp.dot(ep.astype(BF16), vp_ref[:, sl], preferred_element_type=F32)
                   + jnp.dot(ec.astype(BF16), vc_ref[:, sl], preferred_element_type=F32))
            o_ref[:, sl] = acc / den
            lse_ref[:, sl] = jnp.broadcast_to(m + jnp.log(den), (DIL_BLOCK, HEAD))

    return pl.pallas_call(
        body, out_shape=[jax.ShapeDtypeStruct(q.shape, F32)] * 2, grid=(d, nb),
        in_specs=[cur, prev, cur, prev, cur], out_specs=[cur, cur],
        compiler_params=_cparams("parallel", "parallel"), name=f"dil_fwd_d{d}")(q, k, k, v, v)


def _dil_bwd(q, k, v, o, lse, do, dlse, d):
    l = q.shape[0]
    nb = l // DIL_BLOCK
    cur, prev = _dil_specs(d)
    tn = (((0,), (0,)), ((), ()))
    nt = (((1,), (1,)), ((), ()))

    def body(q_ref, kp_ref, kc_ref, vp_ref, vc_ref, o_ref, lse_ref, do_ref, dl_ref,
             dq_ref, dkp_ref, dkc_ref, dvp_ref, dvc_ref):
        n = pl.program_id(1)
        for h in range(4):
            sl = slice(h * HEAD, (h + 1) * HEAD)
            qh = q_ref[:, sl]
            sp, sc, vp, vc = _dil_scores(qh, kp_ref[:, sl], kc_ref[:, sl], n)
            lse_h = lse_ref[:, h * HEAD:h * HEAD + 1]
            pp = jnp.where(vp, jnp.exp(sp - lse_h), 0.0)
            pc = jnp.where(vc, jnp.exp(sc - lse_h), 0.0)
            do_h = do_ref[:, sl]
            do_b = do_h.astype(BF16)
            corr = (jnp.sum(dl_ref[:, sl], axis=-1, keepdims=True)
                    - jnp.sum(do_h * o_ref[:, sl], axis=-1, keepdims=True))
            dsp = pp * (lax.dot_general(do_b, vp_ref[:, sl], nt, preferred_element_type=F32) + corr) * DIL_SCALE
            dsc = pc * (lax.dot_general(do_b, vc_ref[:, sl], nt, preferred_element_type=F32) + corr) * DIL_SCALE
            dsp_b, dsc_b = dsp.astype(BF16), dsc.astype(BF16)
            dq_ref[:, sl] = (jnp.dot(dsp_b, kp_ref[:, sl], preferred_element_type=F32)
                             + jnp.dot(dsc_b, kc_ref[:, sl], preferred_element_type=F32)).astype(dq_ref.dtype)
            dkp_ref[:, sl] = lax.dot_general(dsp_b, qh, tn, preferred_element_type=F32)
            dkc_ref[:, sl] = lax.dot_general(dsc_b, qh, tn, preferred_element_type=F32)
            dvp_ref[:, sl] = lax.dot_general(pp.astype(BF16), do_b, tn, preferred_element_type=F32)
            dvc_ref[:, sl] = lax.dot_general(pc.astype(BF16), do_b, tn, preferred_element_type=F32)

    dq, dkp, dkc, dvp, dvc = pl.pallas_call(
        body, out_shape=[jax.ShapeDtypeStruct(q.shape, BF16)] + [jax.ShapeDtypeStruct(q.shape, F32)] * 4, grid=(d, nb),
        in_specs=[cur, prev, cur, prev, cur, cur, cur, cur, cur], out_specs=[cur] * 5,
        compiler_params=_cparams("parallel", "parallel"), name=f"dil_bwd_d{d}")(q, k, k, v, v, o, lse, do, dlse)

    def fold(cur_part, prev_part):
        shifted = jnp.concatenate([prev_part[DIL_BLOCK:], jnp.zeros((DIL_BLOCK, prev_part.shape[1]), F32)], axis=0)
        return (cur_part + shifted).astype(BF16)

    return dq, fold(dkc, dkp), fold(dvc, dvp)


def _make_dil(d):
    @jax.custom_vjp
    def att(q, k, v):
        return tuple(_dil_fwd(q, k, v, d))

    def fwd(q, k, v):
        o, lse = _dil_fwd(q, k, v, d)
        return (o, lse), (q, k, v, o, lse)

    def bwd(saved, cts):
        return _dil_bwd(*saved, cts[0], cts[1], d)

    att.defvjp(fwd, bwd)
    return att


def dilated_group(q, k, v, d):
    s = q.shape[0]
    view = lambda t: t.reshape(s // d, d * GROUP_W)
    o, lse = _make_dil(d)(view(q), view(k), view(v))
    return o.reshape(s, GROUP_W), lse.reshape(s, GROUP_W)


def _dn_post(c, kind, scale):
    m = _silu(c)
    nrm = m * lax.rsqrt(jnp.sum(m * m, axis=-1, keepdims=True) + 1e-6) * scale
    return kind * nrm + (1.0 - kind) * m


def _dn_kind_scale(j):
    kind = jnp.where(j < 2 * DN_HEADS, 1.0, 0.0).astype(F32)
    scale = jnp.where(j < DN_HEADS, DN_DIM ** -0.5, 1.0).astype(F32)
    return kind, scale


_CONV_RB = 512
_CONV_PAD = 8


def _conv_windows(pad_ref, w_ref, r0, rb, sign):
    acc = None
    for sh in range(DN_CONV):
        win = pad_ref[pl.ds(r0 + _CONV_PAD * (sign < 0) + sign * sh, rb), :]
        term = w_ref[DN_CONV - 1 - sh:DN_CONV - sh, :] * win
        acc = term if acc is None else acc + term
    return acc


def _dn_conv_fwd(x, w):
    s, width = x.shape
    rb = min(_CONV_RB, s)

    def body(x_ref, w_ref, o_ref, pad_ref):
        kind, scale = _dn_kind_scale(pl.program_id(0))
        pad_ref[0:_CONV_PAD, :] = jnp.zeros((_CONV_PAD, LANE), F32)
        pad_ref[_CONV_PAD:, :] = x_ref[...]
        for r0 in range(0, s, rb):
            c = _conv_windows(pad_ref, w_ref, r0, rb, -1)
            o_ref[r0:r0 + rb, :] = _dn_post(c, kind, scale)

    return pl.pallas_call(
        body, out_shape=jax.ShapeDtypeStruct(x.shape, F32), grid=(width // LANE,),
        in_specs=[pl.BlockSpec((s, LANE), lambda j: (0, j)), pl.BlockSpec((DN_CONV, LANE), lambda j: (0, j))],
        out_specs=pl.BlockSpec((s, LANE), lambda j: (0, j)), scratch_shapes=[pltpu.VMEM((s + _CONV_PAD, LANE), F32)],
        compiler_params=_cparams("parallel"), name="dn_conv_fwd")(x, w)


def _dn_conv_bwd(x, w, dy):
    s, width = x.shape
    rb = min(_CONV_RB, s)

    def body(x_ref, w_ref, dy_ref, dx_ref, dw_ref, pad_ref, dpad_ref):
        kind, scale = _dn_kind_scale(pl.program_id(0))
        pad_ref[0:_CONV_PAD, :] = jnp.zeros((_CONV_PAD, LANE), F32)
        pad_ref[_CONV_PAD:, :] = x_ref[...]
        dpad_ref[s:, :] = jnp.zeros((_CONV_PAD, LANE), F32)
        dws = [jnp.zeros((1, LANE), F32) for _ in range(DN_CONV)]
        for r0 in range(0, s, rb):
            c = _conv_windows(pad_ref, w_ref, r0, rb, -1)
            _, vjp = jax.vjp(lambda cc: _dn_post(cc, kind, scale), c)
            dc = vjp(dy_ref[r0:r0 + rb, :])[0]
            dpad_ref[r0:r0 + rb, :] = dc
            for sh in range(DN_CONV):
                win = pad_ref[pl.ds(r0 + _CONV_PAD - sh, rb), :]
                dws[DN_CONV - 1 - sh] = dws[DN_CONV - 1 - sh] + jnp.sum(dc * win, axis=0, keepdims=True)
        for j in range(DN_CONV):
            dw_ref[j:j + 1, :] = dws[j]
        for r0 in range(0, s, rb):
            dx_ref[r0:r0 + rb, :] = _conv_windows(dpad_ref, w_ref, r0, rb, 1)

    return pl.pallas_call(
        body, out_shape=[jax.ShapeDtypeStruct(x.shape, F32), jax.ShapeDtypeStruct(w.shape, F32)], grid=(width // LANE,),
        in_specs=[pl.BlockSpec((s, LANE), lambda j: (0, j)), pl.BlockSpec((DN_CONV, LANE), lambda j: (0, j)),
                  pl.BlockSpec((s, LANE), lambda j: (0, j))],
        out_specs=[pl.BlockSpec((s, LANE), lambda j: (0, j)), pl.BlockSpec((DN_CONV, LANE), lambda j: (0, j))],
        scratch_shapes=[pltpu.VMEM((s + _CONV_PAD, LANE), F32), pltpu.VMEM((s + _CONV_PAD, LANE), F32)],
        compiler_params=_cparams("parallel"), name="dn_conv_bwd")(x, w, dy)


@jax.custom_vjp
def dn_conv(x, w):
    return _dn_conv_fwd(x, w)


dn_conv.defvjp(lambda x, w: (_dn_conv_fwd(x, w), (x, w)), lambda saved, dy: tuple(_dn_conv_bwd(*saved, dy)))


_NN = (((1,), (0,)), ((), ()))
_NT = (((1,), (1,)), ((), ()))
_TN = (((0,), (0,)), ((), ()))


def _bd(a, b, dims, passes=1):
    d = lambda x, y: lax.dot_general(x, y, dims, preferred_element_type=F32)
    if passes == 0:
        return lax.dot_general(a, b, dims, precision=lax.Precision.HIGHEST, preferred_element_type=F32)
    ah, bh = a.astype(BF16), b.astype(BF16)
    if passes == 1:
        return d(ah, bh)
    al, bl = (a - ah.astype(F32)).astype(BF16), (b - bh.astype(F32)).astype(BF16)
    return d(ah, bh) + d(ah, bl) + d(al, bh)


@functools.partial(jax.custom_vjp, nondiff_argnums=(2, 3))
def _pdot(a, b, dims, passes):
    return _bd(a, b, dims, passes)


def _pdot_bwd(dims, passes, saved, g):
    a, b = saved
    if dims == _NN:
        return _bd(g, b, _NT, passes), _bd(a, g, _TN, passes)
    if dims == _NT:
        return _bd(g, b, _NN, passes), _bd(g, a, _TN, passes)
    return _bd(b, g, _NT, passes), _bd(a, g, _NN, passes)


_pdot.defvjp(lambda a, b, dims, passes: (_bd(a, b, dims, passes), (a, b)), _pdot_bwd)


GDN_DOT_PASSES = 0
GDN_SOLVE_PASSES = 0


def _hdot(a, b, dims=_NN):
    return _pdot(a, b, dims, GDN_DOT_PASSES)


def _xdot(a, b, dims=_NN):
    return _pdot(a, b, dims, GDN_SOLVE_PASSES)


def _unit_lower_inverse(a, row, col):
    c = a.shape[0]
    base = 16
    same = lambda b: (row >> int(math.log2(b))) == (col >> int(math.log2(b)))
    pw = jnp.where(same(base), -a, 0.0)
    t = (row == col).astype(F32) + pw
    for _ in range(int(math.log2(base)) - 1):
        pw = _xdot(pw, pw)
        t = t + _xdot(t, pw)
    b = base
    while b < c:
        e = jnp.where(jnp.logical_and(same(2 * b), jnp.logical_not(same(b))), a, 0.0)
        t = t - _xdot(_xdot(t, e), t)
        b *= 2
    return t


def _split3(x):
    hi = x.astype(BF16)
    r1 = x - hi.astype(F32)
    mid = r1.astype(BF16)
    lo = (r1 - mid.astype(F32)).astype(BF16)
    return hi, mid, lo


def _tri_dot(tri, x, dims):
    t = tri.astype(BF16)
    return sum(lax.dot_general(t, p, dims, preferred_element_type=F32) for p in _split3(x))


@jax.custom_vjp
def _cumsum_rows(x):
    c = x.shape[0]
    tri = lax.broadcasted_iota(jnp.int32, (c, c), 0) >= lax.broadcasted_iota(jnp.int32, (c, c), 1)
    return _tri_dot(tri, x, _NN)


def _cumsum_rows_bwd(_, g):
    c = g.shape[0]
    tri = lax.broadcasted_iota(jnp.int32, (c, c), 0) >= lax.broadcasted_iota(jnp.int32, (c, c), 1)
    return (_tri_dot(tri, g, _TN),)


_cumsum_rows.defvjp(lambda x: (_cumsum_rows(x), None), _cumsum_rows_bwd)


def _gdn_prep_fn(qkv, gb):
    c = GDN_CHUNK
    row = lax.broadcasted_iota(jnp.int32, (c, c), 0)
    col = lax.broadcasted_iota(jnp.int32, (c, c), 1)
    incl, strict = row >= col, row > col
    lane = _lanes(gb)
    outs = [[] for _ in range(6)]
    for h in range(DN_HEADS):
        q, k, v = qkv[h], qkv[DN_HEADS + h], qkv[2 * DN_HEADS + h]
        g = jnp.sum(jnp.where(lane == h, gb, 0.0), axis=-1, keepdims=True)
        beta = jnp.sum(jnp.where(lane == DN_HEADS + h, gb, 0.0), axis=-1, keepdims=True)
        gcb = _cumsum_rows(jnp.broadcast_to(g, (c, c)))
        gct = gcb.T
        decay = jnp.where(incl, jnp.exp(jnp.where(incl, gcb - gct, 0.0)), 0.0)
        kb = k * beta
        a = jnp.where(strict, _hdot(kb, k, _NT) * decay, 0.0)
        pw = -a
        t = (row == col).astype(F32) + pw
        for _ in range(int(math.log2(c)) - 1):
            pw = _xdot(pw, pw)
            t = t + _xdot(t, pw)
        eg = jnp.exp(gcb)
        u = _xdot(t, v * beta)
        w = _xdot(t, kb * eg)
        qk = jnp.where(incl, _hdot(q, k, _NT) * decay, 0.0)
        g_last = jnp.sum(jnp.where(row == c - 1, gcb, 0.0), axis=0, keepdims=True)
        kg = k * jnp.exp(g_last - gcb)
        for lst, val in zip(outs, (u, w, q * eg, kg, qk, jnp.broadcast_to(jnp.exp(g_last), (8, LANE)))):
            lst.append(val)
    return outs


def _gdn_prep_specs(nq):
    c = GDN_CHUNK
    big = pl.BlockSpec((c, nq * LANE), lambda n: (n, 0))
    return big


def _gdn_prep_fwd(qkv, gb):
    s = qkv.shape[0]
    c = GDN_CHUNK
    n = s // c
    hw = DN_HEADS * LANE

    def body(qkv_ref, gb_ref, u_ref, w_ref, qg_ref, kg_ref, qk_ref, el_ref):
        res = _gdn_prep_fn(_chunks(qkv_ref), gb_ref[...])
        for ref, chs in zip((u_ref, w_ref, qg_ref, kg_ref, qk_ref, el_ref), res):
            _store(ref, chs)

    row = pl.BlockSpec((c, hw), lambda i: (i, 0))
    return pl.pallas_call(
        body, out_shape=[jax.ShapeDtypeStruct((s, hw), F32)] * 5 + [jax.ShapeDtypeStruct((n * 8, hw), F32)], grid=(n,),
        in_specs=[pl.BlockSpec((c, 3 * hw), lambda i: (i, 0)), pl.BlockSpec((c, LANE), lambda i: (i, 0))],
        out_specs=[row] * 5 + [pl.BlockSpec((8, hw), lambda i: (i, 0))],
        compiler_params=_cparams("parallel"), name="gdn_prep_fwd")(qkv, gb)


def _gdn_prep_bwd(qkv, gb, cts):
    s = qkv.shape[0]
    c = GDN_CHUNK
    n = s // c
    hw = DN_HEADS * LANE

    def body(qkv_ref, gb_ref, du_ref, dw_ref, dqg_ref, dkg_ref, dqk_ref, del_ref, dqkv_ref, dgb_ref):
        _, vjp = jax.vjp(_gdn_prep_fn, _chunks(qkv_ref), gb_ref[...])
        ct = [_chunks(r) for r in (du_ref, dw_ref, dqg_ref, dkg_ref, dqk_ref, del_ref)]
        dqkv, dgb = vjp(ct)
        _store(dqkv_ref, dqkv)
        dgb_ref[...] = dgb

    row = pl.BlockSpec((c, hw), lambda i: (i, 0))
    return pl.pallas_call(
        body, out_shape=[jax.ShapeDtypeStruct(qkv.shape, F32), jax.ShapeDtypeStruct(gb.shape, F32)], grid=(n,),
        in_specs=[pl.BlockSpec((c, 3 * hw), lambda i: (i, 0)), pl.BlockSpec((c, LANE), lambda i: (i, 0))]
        + [row] * 5 + [pl.BlockSpec((8, hw), lambda i: (i, 0))],
        out_specs=[pl.BlockSpec((c, 3 * hw), lambda i: (i, 0)), pl.BlockSpec((c, LANE), lambda i: (i, 0))],
        compiler_params=_cparams("parallel"), name="gdn_prep_bwd")(qkv, gb, *cts)


@jax.custom_vjp
def gdn_prep(qkv, gb):
    return tuple(_gdn_prep_fwd(qkv, gb))


gdn_prep.defvjp(lambda qkv, gb: (tuple(_gdn_prep_fwd(qkv, gb)), (qkv, gb)),
                lambda saved, cts: tuple(_gdn_prep_bwd(*saved, cts)))


def _gdn_scan_fwd(u, w, qg, kg, qk, el):
    s = u.shape[0]
    c = GDN_CHUNK
    n = s // c
    hw = DN_HEADS * LANE

    def body(u_ref, w_ref, qg_ref, kg_ref, qk_ref, el_ref, o_ref, st_ref, s_sc):
        @pl.when(pl.program_id(0) == 0)
        def _():
            s_sc[...] = jnp.zeros_like(s_sc)

        for h in range(DN_HEADS):
            sl = slice(h * LANE, (h + 1) * LANE)
            st = s_sc[h]
            st_ref[sl, :] = st
            v_new = u_ref[:, sl] - _hdot(w_ref[:, sl], st)
            o_ref[:, sl] = _hdot(qg_ref[:, sl], st) + _hdot(qk_ref[:, sl], v_new)
            s_sc[h] = st * el_ref[0:1, sl] + _hdot(kg_ref[:, sl], v_new, _TN)

    row = pl.BlockSpec((c, hw), lambda i: (i, 0))
    return pl.pallas_call(
        body, out_shape=[jax.ShapeDtypeStruct((s, hw), F32), jax.ShapeDtypeStruct((n, hw, LANE), F32)], grid=(n,),
        in_specs=[row] * 5 + [pl.BlockSpec((8, hw), lambda i: (i, 0))],
        out_specs=[row, pl.BlockSpec((None, hw, LANE), lambda i: (i, 0, 0))],
        scratch_shapes=[pltpu.VMEM((DN_HEADS, LANE, LANE), F32)],
        compiler_params=_cparams("arbitrary"), name="gdn_scan_fwd")(u, w, qg, kg, qk, el)


def _gdn_scan_bwd(u, w, qg, kg, qk, el, states, do):
    s = u.shape[0]
    c = GDN_CHUNK
    n = s // c
    hw = DN_HEADS * LANE

    def body(u_ref, w_ref, qg_ref, kg_ref, qk_ref, el_ref, st_ref, do_ref,
             du_ref, dw_ref, dqg_ref, dkg_ref, dqk_ref, del_ref, ds_sc):
        @pl.when(pl.program_id(0) == 0)
        def _():
            ds_sc[...] = jnp.zeros_like(ds_sc)

        for h in range(DN_HEADS):
            sl = slice(h * LANE, (h + 1) * LANE)
            st, ds, do_h = st_ref[sl, :], ds_sc[h], do_ref[:, sl]
            wv, qgv, kgv, qkv_ = w_ref[:, sl], qg_ref[:, sl], kg_ref[:, sl], qk_ref[:, sl]
            v_new = u_ref[:, sl] - _hdot(wv, st)
            dv_new = _hdot(qkv_, do_h, _TN) + _hdot(kgv, ds)
            du_ref[:, sl] = dv_new
            dw_ref[:, sl] = -_hdot(dv_new, st, _NT)
            dqg_ref[:, sl] = _hdot(do_h, st, _NT)
            dqk_ref[:, sl] = _hdot(do_h, v_new, _NT)
            dkg_ref[:, sl] = _hdot(v_new, ds, _NT)
            del_ref[:, sl] = jnp.where(lax.broadcasted_iota(jnp.int32, (8, LANE), 0) == 0,
                                       jnp.sum(st * ds, axis=0, keepdims=True), 0.0)
            ds_sc[h] = _hdot(qgv, do_h, _TN) + ds * el_ref[0:1, sl] - _hdot(wv, dv_new, _TN)

    row = pl.BlockSpec((c, hw), lambda i: (n - 1 - i, 0))
    small = pl.BlockSpec((8, hw), lambda i: (n - 1 - i, 0))
    return pl.pallas_call(
        body, out_shape=[jax.ShapeDtypeStruct((s, hw), F32)] * 5 + [jax.ShapeDtypeStruct((n * 8, hw), F32)], grid=(n,),
        in_specs=[row] * 5 + [small, pl.BlockSpec((None, hw, LANE), lambda i: (n - 1 - i, 0, 0)), row],
        out_specs=[row] * 5 + [small], scratch_shapes=[pltpu.VMEM((DN_HEADS, LANE, LANE), F32)],
        compiler_params=_cparams("arbitrary"), name="gdn_scan_bwd")(u, w, qg, kg, qk, el, states, do)


@jax.custom_vjp
def gdn_scan(u, w, qg, kg, qk, el):
    return _gdn_scan_fwd(u, w, qg, kg, qk, el)[0]


def _gdn_scan_vfwd(*args):
    o, states = _gdn_scan_fwd(*args)
    return o, args + (states,)


gdn_scan.defvjp(_gdn_scan_vfwd, lambda saved, do: tuple(_gdn_scan_bwd(*saved, do)))


def _loss_call(y, t):
    s, d = y.shape
    br = min(ROW_BLOCK, s)
    n = s // br

    def body(y_ref, t_ref, loss_ref, dy_ref, acc):
        i = pl.program_id(0)

        @pl.when(i == 0)
        def _():
            acc[...] = jnp.zeros_like(acc)

        e = y_ref[...] - t_ref[...]
        dy_ref[...] = e / d
        acc[...] += jnp.sum(e * e, axis=0, keepdims=True)

        @pl.when(i == n - 1)
        def _():
            loss_ref[...] = jnp.broadcast_to(jnp.sum(acc[...], axis=1, keepdims=True) * (0.5 / d), loss_ref.shape)

    row = pl.BlockSpec((br, d), lambda i: (i, 0))
    return pl.pallas_call(
        body, out_shape=[jax.ShapeDtypeStruct((1, LANE), F32), jax.ShapeDtypeStruct((s, d), F32)], grid=(n,),
        in_specs=[row, row], out_specs=[pl.BlockSpec((1, LANE), lambda i: (0, 0)), row],
        scratch_shapes=[pltpu.VMEM((1, d), F32)], compiler_params=_cparams("arbitrary"), name="loss_head")(y, t)


@jax.custom_vjp
def loss_head(y, t):
    return _loss_call(y, t)[0][0, 0]


def _loss_head_fwd(y, t):
    loss, dy = _loss_call(y, t)
    return loss[0, 0], (dy,)


loss_head.defvjp(_loss_head_fwd, lambda saved, g: (saved[0] * g, -saved[0] * g))


IN_OFF = {}
_o = 0
for _name, _size in (('q_lat', 384), ('kv_lat', 320), ('z_a', 512), ('dn_qkv', 1536), ('dn_ab', 8), ('z_b', 512),
                     ('dil_qkv', 4608), ('z_c', 512), ('gate', 3072)):
    IN_OFF[_name] = (_o, _o + _size)
    _o += _size
IN_WIDTH = _o
N_CHIPS = 4


def _cols(w, name):
    a, b = IN_OFF[name]
    return w[:, a:b]


def _pad_cols(w, to):
    return jnp.concatenate([w, jnp.zeros((w.shape[0], to - w.shape[1]), w.dtype)], axis=1)


def _pad_row(v, to=None):
    v = v.reshape(1, -1)
    return v if to is None or v.shape[1] == to else _pad_cols(v, to)


def _shard_cols(pieces, a, b):
    wsh = pieces[0].shape[1]
    parts = [pieces[j][:, max(a, j * wsh) - j * wsh:min(b, (j + 1) * wsh) - j * wsh]
             for j in range(len(pieces)) if max(a, j * wsh) < min(b, (j + 1) * wsh)]
    return parts[0] if len(parts) == 1 else jnp.concatenate(parts, axis=1)


def _win_groups_impl(w_in4):
    out = []
    for l in range(w_in4.shape[1]):
        pieces = [w_in4[j, l] for j in range(w_in4.shape[0])]
        cols = lambda name: _shard_cols(pieces, *IN_OFF[name])
        out.append((_pad_cols(jnp.concatenate([cols('q_lat'), cols('kv_lat')], axis=1), 768),
                    jnp.concatenate([cols('z_a'), cols('z_b'), cols('z_c')], axis=1),
                    cols('dn_qkv'), _pad_cols(cols('dn_ab'), LANE), cols('dil_qkv'), cols('gate')))
    return tuple(out)


@jax.custom_vjp
def win_groups(w_in4):
    return _win_groups_impl(w_in4)


def _win_groups_bwd(_, cts):
    n_chip, depth = N_CHIPS, len(cts)
    wsh = IN_WIDTH // n_chip
    per_layer = []
    for l in range(depth):
        dmla, dz, dgdn, dab, ddil, dgate = cts[l]
        full = jnp.concatenate([dmla[:, :704], dz[:, :512], dgdn, dab[:, :8], dz[:, 512:1024], ddil, dz[:, 1024:], dgate], axis=1)
        per_layer.append([full[:, j * wsh:(j + 1) * wsh] for j in range(n_chip)])
    return (jnp.stack([jnp.stack([per_layer[l][j] for l in range(depth)]) for j in range(n_chip)]),)


win_groups.defvjp(lambda w: (_win_groups_impl(w), None), _win_groups_bwd)


def _layer(x, p, tabs):
    w_mla, w_z, w_gdn, w_ab, w_dil, w_gate = p['w_in_groups']
    h, = rowwise(f_rms_full, "rms_in", [x], [_pad_row(p['norm_g'])], [(D_MODEL, BF16)])
    mla_in = matmul(h, w_mla)
    z = matmul(h, w_z)
    dn_qkv = matmul(h, w_gdn)
    dn_ab = matmul(h, w_ab)
    dil_qkv = matmul(h, w_dil)
    gate = matmul(h, w_gate)

    qn, kvn, kpe = rowwise(f_mla1, "mla_norm", [mla_in], [_pad_row(p['mla_q_a_norm_g']), _pad_row(p['mla_kv_a_norm_g'])],
                           [(MLA_Q_RANK, BF16), (MLA_KV_RANK, BF16), (LANE, F32)])
    wq = p['mla_w_q_b']
    wq_pad = jnp.concatenate(
        [wq[:, hh * MLA_QK:hh * MLA_QK + MLA_NOPE] for hh in range(MLA_HEADS)]
        + [_pad_cols(wq[:, hh * MLA_QK + MLA_NOPE:(hh + 1) * MLA_QK], LANE) for hh in range(MLA_HEADS)], axis=1)
    q = matmul(qn, wq_pad)
    kv = matmul(kvn, p['mla_w_kv_b'])
    gq, gk = p['mla_q_norm_g'], p['mla_k_norm_g']
    q_att, k_att, v_att = rowwise(
        f_mla2, "mla_qk", [q, kv, kpe, tabs['cos_r'], tabs['sin_r']],
        [_pad_row(gq[:MLA_NOPE]), _pad_row(gq[MLA_NOPE:], LANE), _pad_row(gk[:MLA_NOPE]), _pad_row(gk[MLA_NOPE:], LANE)],
        [(MLA_HEADS * MLA_DQK, BF16), (MLA_HEADS * MLA_DQK, BF16), (MLA_HEADS * HEAD, BF16)], nograd=(3, 4))
    y_a = mla_attention(q_att, k_att, v_att)

    qkv_n = dn_conv(dn_qkv, p['dn_conv_w'])
    gb, = rowwise(f_gates, "dn_gates", [dn_ab], [_pad_row(p['dn_a_log'], LANE), _pad_row(p['dn_dt_bias'], LANE)], [(LANE, F32)])
    o_b = gdn_scan(*gdn_prep(qkv_n, gb))
    y_b, = rowwise(f_headnorm, "dn_out_norm", [o_b], [_pad_row(p['dn_out_norm_g'])], [(DN_HEADS * DN_DIM, F32)])

    qd, kd, vd = rowwise(f_dil, "dil_qk", [dil_qkv, tabs['cos_h'], tabs['sin_h']],
                         [_pad_row(p['dil_q_norm_g']), _pad_row(p['dil_k_norm_g'])],
                         [(3 * GROUP_W, BF16)] * 3, nograd=(1, 2))
    outs, lses = [], []
    for gi, d in enumerate(DIL_DILATIONS):
        sl = slice(gi * GROUP_W, (gi + 1) * GROUP_W)
        o_g, l_g = dilated_group(qd[:, sl], kd[:, sl], vd[:, sl], d)
        outs.append(o_g)
        lses.append(l_g)
    y_c, = rowwise(f_comb, "dil_comb", outs + lses, [], [(GROUP_W, F32)])

    ys = rowwise(f_merge1, "merge_silu", [y_a, y_b, y_c, z], [], [(BRANCH_W, BF16)] * 3)
    bo = [matmul(ys[b], p['w_branch'][b]) for b in range(3)]
    mixed, = rowwise(f_merge2, "merge_gate", [gate] + bo, [], [(D_MODEL, BF16)])
    return matmul(mixed, p['w_out'], res=x)


def _rope_tables(pos, dim):
    inv_freq = 1.0 / (ROPE_THETA ** (jnp.arange(0, dim, 2, dtype=F32) / dim))
    ang = pos.astype(F32)[:, None] * inv_freq
    return jnp.cos(ang), jnp.sin(ang)


def _tables(pos):
    cr, sr = _rope_tables(pos, MLA_ROPE)
    ch, sh = _rope_tables(pos, HEAD)
    zero = jnp.zeros((pos.shape[0], LANE - MLA_ROPE), F32)
    return {'cos_r': jnp.concatenate([cr, cr, zero], axis=1), 'sin_r': jnp.concatenate([sr, sr, zero], axis=1),
            'cos_h': jnp.concatenate([ch, ch], axis=1), 'sin_h': jnp.concatenate([sh, sh], axis=1)}


def _local_loss(w_in4, mats, conv_w, small, x, target, tabs):
    groups = win_groups(w_in4)
    for l in range(DEPTH):
        p = {k: v[l] for k, v in mats.items()}
        p.update({k: v[l] for k, v in small.items()})
        p['dn_conv_w'] = conv_w[l]
        p['w_in_groups'] = groups[l]
        x = _layer(x, p, tabs)
    return loss_head(x, target)


def _pack(arrays, dtype, row_tile):
    flat = jnp.concatenate([a.astype(dtype).reshape(-1) for a in arrays])
    rows = -(-flat.shape[0] // (LANE * row_tile)) * row_tile
    flat = jnp.concatenate([flat, jnp.zeros((rows * LANE - flat.shape[0],), dtype)])
    return flat.reshape(rows, LANE)


def _unpack_impl(buf, shapes):
    flat = buf.reshape(-1)
    out, off = [], 0
    for shp in shapes:
        n = math.prod(shp)
        out.append(flat[off:off + n].reshape(shp))
        off += n
    return tuple(out)


@functools.partial(jax.custom_vjp, nondiff_argnums=(1, 2, 3))
def _unpack_p(buf, shapes, dtype_name, rows):
    return _unpack_impl(buf, shapes)


_unpack_p.defvjp(lambda buf, shapes, dtype_name, rows: (_unpack_impl(buf, shapes), None),
                 lambda shapes, dtype_name, rows, _, cts: (_pack(cts, jnp.dtype(dtype_name), rows),))


def _unpack(buf, shapes):
    return _unpack_p(buf, tuple(shapes), jnp.dtype(buf.dtype).name, buf.shape[0])


def _full_from_chips(buf4, shard_shapes):
    per_chip = [_unpack(buf4[j], tuple(shard_shapes)) for j in range(4)]
    return {name: jnp.concatenate([per_chip[j][i] for j in range(4)], axis=axis) for i, (name, axis) in enumerate(MATS)}


_HBM = pl.BlockSpec(memory_space=pltpu.HBM)
_VMEM = pl.BlockSpec(memory_space=pltpu.VMEM)


def _chip_peers(x, y):
    return [(1 - x, y), (x, 1 - y), (1 - x, 1 - y)]


def _chip_exchange(arrays, gather, name):
    n = len(arrays)

    def body(*refs):
        in_refs, out_refs = refs[:n], refs[n:2 * n]
        send_sems, recv_sems, local_sems = refs[2 * n:]
        x, y, c = lax.axis_index("x"), lax.axis_index("y"), lax.axis_index("c")
        me = 2 * x + y
        peers = _chip_peers(x, y)
        local, sends = [], []
        for a, (i_ref, o_ref) in enumerate(zip(in_refs, out_refs)):
            local.append(pltpu.make_async_copy(i_ref if gather else i_ref.at[me], o_ref.at[me], local_sems.at[a]))
            local[-1].start()
            for k, (px, py) in enumerate(peers):
                sends.append(pltpu.make_async_remote_copy(
                    src_ref=i_ref if gather else i_ref.at[2 * px + py], dst_ref=o_ref.at[me], send_sem=send_sems.at[3 * a + k],
                    recv_sem=recv_sems.at[3 * a + k], device_id=(px, py, c), device_id_type=MESH))
                sends[-1].start()
        for a, (i_ref, o_ref) in enumerate(zip(in_refs, out_refs)):
            for k, (px, py) in enumerate(peers):
                pltpu.make_async_remote_copy(
                    src_ref=i_ref if gather else i_ref.at[me], dst_ref=o_ref.at[2 * px + py], send_sem=send_sems.at[3 * a + k],
                    recv_sem=recv_sems.at[3 * a + k], device_id=(px, py, c), device_id_type=MESH).wait_recv()
        for cp in sends:
            cp.wait_send()
        for cp in local:
            cp.wait()

    out_shape = [jax.ShapeDtypeStruct(((4,) + a.shape) if gather else a.shape, a.dtype) for a in arrays]
    return pl.pallas_call(
        body, out_shape=out_shape, in_specs=[_HBM] * n, out_specs=[_HBM] * n,
        scratch_shapes=[pltpu.SemaphoreType.DMA((3 * n,)), pltpu.SemaphoreType.DMA((3 * n,)), pltpu.SemaphoreType.DMA((n,))],
        name=name)(*arrays)


def chip_all_gather(shards):
    return _chip_exchange(shards, True, "chip_all_gather")


def chip_all_to_all(pieces):
    return _chip_exchange(pieces, False, "chip_all_to_all")


def sibling_swap(arrays):
    n = len(arrays)

    def body(*refs):
        in_refs, out_refs = refs[:n], refs[n:2 * n]
        send_sems, recv_sems = refs[2 * n:]
        x, y, c = lax.axis_index("x"), lax.axis_index("y"), lax.axis_index("c")
        cps = [pltpu.make_async_remote_copy(src_ref=i_ref, dst_ref=o_ref, send_sem=send_sems.at[a], recv_sem=recv_sems.at[a],
                                            device_id=(x, y, 1 - c), device_id_type=MESH)
               for a, (i_ref, o_ref) in enumerate(zip(in_refs, out_refs))]
        for cp in cps:
            cp.start()
        for cp in cps:
            cp.wait()

    return pl.pallas_call(
        body, out_shape=[jax.ShapeDtypeStruct(a.shape, a.dtype) for a in arrays], in_specs=[_HBM] * n, out_specs=[_HBM] * n,
        scratch_shapes=[pltpu.SemaphoreType.DMA((n,)), pltpu.SemaphoreType.DMA((n,))], name="sibling_swap")(*arrays)


def all_gather8(v, name):
    def body(v_ref, out_ref, send_sems, recv_sems):
        x, y, c = lax.axis_index("x"), lax.axis_index("y"), lax.axis_index("c")
        out_ref[4 * x + 2 * y + c] = v_ref[...]

        def peer(k):
            return (x ^ (k >> 2), y ^ ((k >> 1) & 1), c ^ (k & 1))

        sends = [pltpu.make_async_remote_copy(src_ref=v_ref, dst_ref=out_ref.at[4 * x + 2 * y + c], send_sem=send_sems.at[k - 1],
                                              recv_sem=recv_sems.at[k - 1], device_id=peer(k), device_id_type=MESH)
                 for k in range(1, 8)]
        for cp in sends:
            cp.start()
        for k in range(1, 8):
            px, py, pc = peer(k)
            pltpu.make_async_remote_copy(src_ref=v_ref, dst_ref=out_ref.at[4 * px + 2 * py + pc], send_sem=send_sems.at[k - 1],
                                         recv_sem=recv_sems.at[k - 1], device_id=peer(k), device_id_type=MESH).wait_recv()
        for cp in sends:
            cp.wait_send()

    return pl.pallas_call(
        body, out_shape=jax.ShapeDtypeStruct((8,) + v.shape, v.dtype), in_specs=[_VMEM], out_specs=_VMEM,
        scratch_shapes=[pltpu.SemaphoreType.DMA((7,)), pltpu.SemaphoreType.DMA((7,))], name=name)(v)


def sum_blocks(blocks, row_tile, name):
    n, rows, width = blocks.shape

    def body(b_ref, o_ref):
        acc = b_ref[0].astype(F32)
        for j in range(1, n):
            acc = acc + b_ref[j].astype(F32)
        o_ref[...] = acc

    return pl.pallas_call(
        body, out_shape=jax.ShapeDtypeStruct((rows, width), F32), grid=(rows // row_tile,),
        in_specs=[pl.BlockSpec((n, row_tile, width), lambda i: (0, i, 0))],
        out_specs=pl.BlockSpec((row_tile, width), lambda i: (i, 0)), compiler_params=_cparams("parallel"), name=name)(blocks)


def adamw(g_parts, w, m, v, row_tile, name):
    npart = len(g_parts)

    def body(*refs):
        g = refs[0][...]
        for r in refs[1:npart]:
            g = g + r[...]
        w_ref, m_ref, v_ref, g_out, d_out, m_out, v_out = refs[npart:]
        m_new = ADAM_B1 * m_ref[...] + (1.0 - ADAM_B1) * g
        v_new = ADAM_B2 * v_ref[...] + (1.0 - ADAM_B2) * (g * g)
        m_hat = m_new / (1.0 - ADAM_B1 ** ADAM_STEP)
        v_hat = v_new / (1.0 - ADAM_B2 ** ADAM_STEP)
        g_out[...] = g
        d_out[...] = -ADAM_LR * (m_hat / (jnp.sqrt(v_hat) + ADAM_EPS) + ADAM_WD * w_ref[...])
        m_out[...] = m_new
        v_out[...] = v_new

    rows, width = w.shape
    spec = pl.BlockSpec((row_tile, width), lambda i: (i, 0))
    return pl.pallas_call(
        body, out_shape=[jax.ShapeDtypeStruct(w.shape, F32)] * 4, grid=(rows // row_tile,),
        in_specs=[spec] * (npart + 3), out_specs=[spec] * 4, compiler_params=_cparams("parallel"), name=name)(*g_parts, w, m, v)


def kernel(x, positions, norm_g, w_in, mla_q_a_norm_g, mla_w_q_b, mla_kv_a_norm_g, mla_w_kv_b, mla_q_norm_g, mla_k_norm_g, dn_conv_w, dn_a_log, dn_dt_bias, dn_out_norm_g, dil_q_norm_g, dil_k_norm_g, w_branch, w_out, loss_target, m_norm_g, m_w_in, m_mla_q_a_norm_g, m_mla_w_q_b, m_mla_kv_a_norm_g, m_mla_w_kv_b, m_mla_q_norm_g, m_mla_k_norm_g, m_dn_conv_w, m_dn_a_log, m_dn_dt_bias, m_dn_out_norm_g, m_dil_q_norm_g, m_dil_k_norm_g, m_w_branch, m_w_out, v_norm_g, v_w_in, v_mla_q_a_norm_g, v_mla_w_q_b, v_mla_kv_a_norm_g, v_mla_w_kv_b, v_mla_q_norm_g, v_mla_k_norm_g, v_dn_conv_w, v_dn_a_log, v_dn_dt_bias, v_dn_out_norm_g, v_dil_q_norm_g, v_dil_k_norm_g, v_w_branch, v_w_out):
    w = dict(norm_g=norm_g, w_in=w_in, mla_q_a_norm_g=mla_q_a_norm_g, mla_w_q_b=mla_w_q_b, mla_kv_a_norm_g=mla_kv_a_norm_g,
             mla_w_kv_b=mla_w_kv_b, mla_q_norm_g=mla_q_norm_g, mla_k_norm_g=mla_k_norm_g, dn_conv_w=dn_conv_w, dn_a_log=dn_a_log,
             dn_dt_bias=dn_dt_bias, dn_out_norm_g=dn_out_norm_g, dil_q_norm_g=dil_q_norm_g, dil_k_norm_g=dil_k_norm_g,
             w_branch=w_branch, w_out=w_out)
    m = dict(norm_g=m_norm_g, w_in=m_w_in, mla_q_a_norm_g=m_mla_q_a_norm_g, mla_w_q_b=m_mla_w_q_b, mla_kv_a_norm_g=m_mla_kv_a_norm_g,
             mla_w_kv_b=m_mla_w_kv_b, mla_q_norm_g=m_mla_q_norm_g, mla_k_norm_g=m_mla_k_norm_g, dn_conv_w=m_dn_conv_w,
             dn_a_log=m_dn_a_log, dn_dt_bias=m_dn_dt_bias, dn_out_norm_g=m_dn_out_norm_g, dil_q_norm_g=m_dil_q_norm_g,
             dil_k_norm_g=m_dil_k_norm_g, w_branch=m_w_branch, w_out=m_w_out)
    v = dict(norm_g=v_norm_g, w_in=v_w_in, mla_q_a_norm_g=v_mla_q_a_norm_g, mla_w_q_b=v_mla_w_q_b, mla_kv_a_norm_g=v_mla_kv_a_norm_g,
             mla_w_kv_b=v_mla_w_kv_b, mla_q_norm_g=v_mla_q_norm_g, mla_k_norm_g=v_mla_k_norm_g, dn_conv_w=v_dn_conv_w,
             dn_a_log=v_dn_a_log, dn_dt_bias=v_dn_dt_bias, dn_out_norm_g=v_dn_out_norm_g, dil_q_norm_g=v_dil_q_norm_g,
             dil_k_norm_g=v_dil_k_norm_g, w_branch=v_w_branch, w_out=v_w_out)
    chip = 2 * lax.axis_index("x") + lax.axis_index("y")
    mat_names = [n for n, _ in MATS]
    mat_shapes = tuple(w[n].shape for n in mat_names)
    conv_shard = dn_conv_w.shape
    win_shape = w_in.shape
    win_rows = (win_shape[0] * win_shape[1], win_shape[2])

    w_in4, mats4 = chip_all_gather([w_in.astype(BF16), _pack([w[n] for n in mat_names], BF16, MAT_ROWS)])
    conv8 = all_gather8(_pack([dn_conv_w], F32, 8), "gather_conv_w")
    conv_full = jnp.concatenate([_unpack(conv8[2 * j], (conv_shard,))[0] for j in range(4)], axis=2)
    small = {n: w[n] for n in SMALL}
    tabs = _tables(positions[0])

    def loss_fn(w_in4, mats4, conv_full, small, xs):
        return _local_loss(w_in4, _full_from_chips(mats4, mat_shapes), conv_full, small, xs, loss_target[0], tabs)

    loss, (g_win4, g_mats4, g_conv, g_small, g_x) = jax.value_and_grad(loss_fn, argnums=(0, 1, 2, 3, 4))(
        w_in4, mats4, conv_full, small, x[0])
    loss = lax.psum(loss, ("x", "y", "c"))

    r_win4, r_mats4 = chip_all_to_all([g_win4, g_mats4])
    part_win = sum_blocks(r_win4.reshape((4,) + win_rows), WIN_ROWS, "sum_chip_pieces_w_in")
    part_mats = sum_blocks(r_mats4, MAT_ROWS, "sum_chip_pieces_mats")
    other_win, other_mats = sibling_swap([part_win, part_mats])
    res_win = adamw([part_win, other_win], *[d['w_in'].reshape(win_rows) for d in (w, m, v)], WIN_ROWS, "adamw_w_in")
    packed = [_pack([d[n] for n in mat_names], F32, MAT_ROWS) for d in (w, m, v)]
    res_big = [dict(zip(mat_names, _unpack(r, mat_shapes))) for r in adamw([part_mats, other_mats], *packed, MAT_ROWS, "adamw_mats")]
    for k in range(4):
        res_big[k]['w_in'] = res_win[k].reshape(win_shape)

    small_shapes = tuple(w[n].shape for n in SMALL) + (g_conv.shape,)
    g_all = sum_blocks(all_gather8(_pack([g_small[n] for n in SMALL] + [g_conv], F32, 8), "gather_small_grads"), 8, "sum_small")
    g_list = list(_unpack(g_all, small_shapes))
    g_list[-1] = lax.dynamic_slice_in_dim(g_list[-1], chip * conv_shard[2], conv_shard[2], axis=2)
    small_names = list(SMALL) + ['dn_conv_w']
    packed_s = [_pack([d[n] for n in small_names], F32, 8) for d in (w, m, v)]
    shapes_s = tuple(w[n].shape for n in small_names)
    res_small = [dict(zip(small_names, _unpack(r, shapes_s))) for r in adamw([_pack(g_list, F32, 8)], *packed_s, 8, "adamw_small")]

    outs = [loss, g_x[None]]
    for k in range(4):
        outs += [res_big[k][n] if n in res_big[k] else res_small[k][n] for n in WEIGHTS]
    return tuple(outs)
```

```python
import functools
import math

import jax
import jax.numpy as jnp
from jax import lax
from jax.experimental import pallas as pl
from jax.experimental.pallas import tpu as pltpu

F32 = jnp.float32
BF16 = jnp.bfloat16
HI = lax.Precision.HIGHEST
MESH = pl.DeviceIdType.MESH

LANE = 128
VMEM_LIMIT = 48 * 1024 * 1024
ROW_BLOCK = 256
MAT_ROWS = 1024
WIN_ROWS = 128

RMS_EPS = 1e-6
ROPE_THETA = 10000.0
D_MODEL = 1024
DEPTH = 2
MLA_HEADS = 4
MLA_NOPE = 128
MLA_ROPE = 64
MLA_QK = MLA_NOPE + MLA_ROPE
MLA_Q_RANK = 384
MLA_KV_RANK = 256
DN_HEADS = 4
DN_DIM = 128
DN_CONV = 4
GDN_CHUNK = 128
DIL_WINDOWS = (128, 512, 2048)
DIL_DILATIONS = (1, 4, 16)
DIL_GROUPS = 3
DIL_BLOCK = 128
HEAD = 128
BRANCH_W = 512

ADAM_LR = 0.001
ADAM_B1 = 0.9
ADAM_B2 = 0.999
ADAM_EPS = 1e-08
ADAM_WD = 0.01
ADAM_STEP = 10

WEIGHTS = ['norm_g', 'w_in', 'mla_q_a_norm_g', 'mla_w_q_b', 'mla_kv_a_norm_g', 'mla_w_kv_b', 'mla_q_norm_g',
           'mla_k_norm_g', 'dn_conv_w', 'dn_a_log', 'dn_dt_bias', 'dn_out_norm_g', 'dil_q_norm_g', 'dil_k_norm_g',
           'w_branch', 'w_out']
MATS = (('mla_w_q_b', 2), ('mla_w_kv_b', 2), ('w_branch', 3), ('w_out', 1))
SMALL = ('norm_g', 'mla_q_a_norm_g', 'mla_kv_a_norm_g', 'mla_q_norm_g', 'mla_k_norm_g', 'dn_a_log', 'dn_dt_bias',
         'dn_out_norm_g', 'dil_q_norm_g', 'dil_k_norm_g')


def _cparams(*sem):
    return pltpu.CompilerParams(dimension_semantics=sem or None, vmem_limit_bytes=VMEM_LIMIT)


def _tile(dim, target):
    best = 0
    for t in range(LANE, min(dim, target) + 1, LANE):
        if dim % t == 0:
            best = t
    assert best, (dim, target)
    return best


def _mm(a, b, mode, out_dtype, res=None):
    if mode == 'nn':
        (m, k), (k2, n) = a.shape, b.shape
    elif mode == 'nt':
        (m, k), (n, k2) = a.shape, b.shape
    else:
        (k, m), (k2, n) = a.shape, b.shape
    assert k == k2, (a.shape, b.shape, mode)
    tm, tn, tk = _tile(m, 512), _tile(n, 1024), _tile(k, 1024)
    nk = k // tk
    dims = {'nn': (((1,), (0,)), ((), ())), 'nt': (((1,), (1,)), ((), ())), 'tn': (((0,), (0,)), ((), ()))}[mode]

    def body(*refs):
        if res is None:
            a_ref, b_ref, o_ref, acc = refs
        else:
            a_ref, b_ref, r_ref, o_ref, acc = refs
        kk = pl.program_id(2)

        @pl.when(kk == 0)
        def _():
            acc[...] = jnp.zeros_like(acc)

        acc[...] += lax.dot_general(a_ref[...].astype(BF16), b_ref[...].astype(BF16), dims,
                                    preferred_element_type=F32)

        @pl.when(kk == nk - 1)
        def _():
            r = acc[...]
            if res is not None:
                r = r + r_ref[...].astype(F32)
            o_ref[...] = r.astype(o_ref.dtype)

    a_spec = pl.BlockSpec((tk, tm), lambda i, j, kk: (kk, i)) if mode == 'tn' else pl.BlockSpec((tm, tk), lambda i, j, kk: (i, kk))
    b_spec = pl.BlockSpec((tn, tk), lambda i, j, kk: (j, kk)) if mode == 'nt' else pl.BlockSpec((tk, tn), lambda i, j, kk: (kk, j))
    o_spec = pl.BlockSpec((tm, tn), lambda i, j, kk: (i, j))
    in_specs = [a_spec, b_spec] + ([o_spec] if res is not None else [])
    args = (a, b) + ((res,) if res is not None else ())
    return pl.pallas_call(
        body, out_shape=jax.ShapeDtypeStruct((m, n), out_dtype), grid=(m // tm, n // tn, nk),
        in_specs=in_specs, out_specs=o_spec, scratch_shapes=[pltpu.VMEM((tm, tn), F32)],
        compiler_params=_cparams("parallel", "parallel", "arbitrary"),
        name=f"mm_{mode}_{m}x{k}x{n}" + ("_res" if res is not None else ""))(*args)


def _make_matmul(out_dtype, with_res):
    @jax.custom_vjp
    def mm(a, b, *r):
        return _mm(a, b, 'nn', out_dtype, *r)

    def fwd(a, b, *r):
        return mm(a, b, *r), (a, b)

    def bwd(saved, g):
        a, b = saved
        da = _mm(g, b, 'nt', a.dtype)
        db = _mm(a, g, 'tn', b.dtype)
        return (da, db) + ((g,) if with_res else ())

    mm.defvjp(fwd, bwd)
    return mm


def matmul(a, b, out_dtype=F32, res=None):
    if res is None:
        return _make_matmul(out_dtype, False)(a, b)
    return _make_matmul(out_dtype, True)(a, b, res)


def _chunks(ref):
    return [ref[:, c * LANE:(c + 1) * LANE].astype(F32) for c in range(ref.shape[-1] // LANE)]


def _store(ref, chunks):
    for c, ch in enumerate(chunks):
        ref[:, c * LANE:(c + 1) * LANE] = ch.astype(ref.dtype)


def _row_spec(width, br):
    return pl.BlockSpec((br, width), lambda i: (i, 0))


def _par_spec(width):
    return pl.BlockSpec((1, width), lambda i: (0, 0))


def _rw_fwd(f, name, rows, params, outs, br):
    s = rows[0].shape[0]
    br = min(br, s)
    nr, npar = len(rows), len(params)

    def body(*refs):
        rc = [_chunks(r) for r in refs[:nr]]
        pc = [_chunks(p) for p in refs[nr:nr + npar]]
        res = f(rc, pc)
        for o_ref, chs in zip(refs[nr + npar:], res):
            _store(o_ref, chs)

    return pl.pallas_call(
        body, out_shape=[jax.ShapeDtypeStruct((s, w), dt) for w, dt in outs], grid=(s // br,),
        in_specs=[_row_spec(r.shape[1], br) for r in rows] + [_par_spec(p.shape[1]) for p in params],
        out_specs=[_row_spec(w, br) for w, _ in outs],
        compiler_params=_cparams("parallel"), name=name + "_fwd")(*rows, *params)


def _rw_bwd(f, name, rows, params, cts, nograd, br):
    s = rows[0].shape[0]
    br = min(br, s)
    nr, npar, nct = len(rows), len(params), len(cts)
    grad_rows = [i for i in range(nr) if i not in nograd]

    def body(*refs):
        rc = [_chunks(r) for r in refs[:nr]]
        pc = [_chunks(p) for p in refs[nr:nr + npar]]
        ct = [_chunks(c) for c in refs[nr + npar:nr + npar + nct]]
        out_refs = refs[nr + npar + nct:]
        _, vjp = jax.vjp(f, rc, pc)
        drc, dpc = vjp(ct)
        for o_ref, i in zip(out_refs[:len(grad_rows)], grad_rows):
            _store(o_ref, drc[i])
        i0 = pl.program_id(0)
        for o_ref, chs in zip(out_refs[len(grad_rows):], dpc):
            @pl.when(i0 == 0)
            def _(o_ref=o_ref):
                o_ref[...] = jnp.zeros_like(o_ref)
            for c, ch in enumerate(chs):
                o_ref[:, c * LANE:(c + 1) * LANE] += ch

    out_shape = ([jax.ShapeDtypeStruct(rows[i].shape, rows[i].dtype) for i in grad_rows]
                 + [jax.ShapeDtypeStruct(p.shape, F32) for p in params])
    out_specs = [_row_spec(rows[i].shape[1], br) for i in grad_rows] + [_par_spec(p.shape[1]) for p in params]
    res = pl.pallas_call(
        body, out_shape=out_shape, grid=(s // br,),
        in_specs=([_row_spec(r.shape[1], br) for r in rows] + [_par_spec(p.shape[1]) for p in params]
                  + [_row_spec(c.shape[1], br) for c in cts]),
        out_specs=out_specs, compiler_params=_cparams("arbitrary"), name=name + "_bwd")(*rows, *params, *cts)
    drows = [None] * nr
    for o, i in zip(res[:len(grad_rows)], grad_rows):
        drows[i] = o
    for i in nograd:
        drows[i] = jnp.zeros_like(rows[i])
    return tuple(drows), tuple(res[len(grad_rows):])


def rowwise(f, name, rows, params, outs, nograd=(), br=ROW_BLOCK):
    @jax.custom_vjp
    def op(rows, params):
        return tuple(_rw_fwd(f, name, rows, params, outs, br))

    def fwd(rows, params):
        return op(rows, params), (rows, params)

    def bwd(saved, cts):
        rows, params = saved
        return _rw_bwd(f, name, rows, params, list(cts), nograd, br)

    op.defvjp(fwd, bwd)
    return op(tuple(rows), tuple(params))


def _lane_roll(x, s):
    w = x.shape[-1]

    @jax.custom_vjp
    def r(v):
        return pltpu.roll(v, s, 1)

    r.defvjp(lambda v: (r(v), None), lambda _, g: (pltpu.roll(g, (w - s) % w, 1),))
    return r(x)


def _lanes(x):
    return lax.broadcasted_iota(jnp.int32, x.shape, 1)


def _rms(x, g, n=LANE):
    return x * lax.rsqrt(jnp.sum(x * x, axis=-1, keepdims=True) / n + RMS_EPS) * g


def _rope128(x, cos, sin):
    lane = _lanes(x)
    rot = _lane_roll(x, 64) * jnp.where(lane < 64, -1.0, 1.0)
    return x * cos + rot * sin


def _rope64(x, cos, sin):
    lane = _lanes(x)
    rot = jnp.where(lane < 32, -_lane_roll(x, 96), jnp.where(lane < 64, _lane_roll(x, 32), 0.0))
    return x * cos + rot * sin


def _silu(z):
    return z * jax.nn.sigmoid(z)


def _softplus(x):
    return jnp.maximum(x, 0.0) + jnp.log1p(jnp.exp(-jnp.abs(x)))


def f_rms_full(rc, pc):
    x, g = rc[0], pc[0]
    width = len(x) * LANE
    ms = sum(jnp.sum(c * c, axis=-1, keepdims=True) for c in x) / width
    r = lax.rsqrt(ms + RMS_EPS)
    return [[c * r * gc for c, gc in zip(x, g)]]


def f_mla1(rc, pc):
    x = rc[0]
    qn = f_rms_full([x[0:3]], [pc[0]])[0]
    kvn = f_rms_full([x[3:5]], [pc[1]])[0]
    return [qn, kvn, [x[5]]]


def f_mla2(rc, pc):
    q, kv, kpe, cos, sin = rc[0], rc[1], rc[2][0], rc[3][0], rc[4][0]
    gqn, gqp, gkn, gkp = pc[0][0], pc[1][0], pc[2][0], pc[3][0]
    k_pe = _rope64(_rms(kpe, gkp, MLA_ROPE), cos, sin)
    q_att, k_att, v = [], [], []
    for h in range(MLA_HEADS):
        q_att += [_rms(q[h], gqn), _rope64(_rms(q[MLA_HEADS + h], gqp, MLA_ROPE), cos, sin)]
        k_att += [_rms(kv[2 * h], gkn), k_pe]
        v.append(kv[2 * h + 1])
    return [q_att, k_att, v]


def f_gates(rc, pc):
    x, a_log, dt_bias = rc[0][0], pc[0][0], pc[1][0]
    lane = _lanes(x)
    g = -jnp.exp(a_log) * _softplus(x + dt_bias)
    return [[jnp.where(lane < DN_HEADS, g, jnp.where(lane < 2 * DN_HEADS, jax.nn.sigmoid(x), 0.0))]]


def f_headnorm(rc, pc):
    return [[_rms(c, pc[0][0]) for c in rc[0]]]


def f_dil(rc, pc):
    x, cos, sin = rc[0], rc[1][0], rc[2][0]
    gq, gk = pc[0][0], pc[1][0]
    n = len(x) // 3
    q = [_rope128(_rms(c, gq), cos, sin) for c in x[:n]]
    k = [_rope128(_rms(c, gk), cos, sin) for c in x[n:2 * n]]
    return [q, k, list(x[2 * n:])]


def f_comb(rc, pc):
    o, l = rc[:DIL_GROUPS], rc[DIL_GROUPS:]
    out = []
    for c in range(len(o[0])):
        m = functools.reduce(jnp.maximum, [lg[c] for lg in l])
        e = [jnp.exp(lg[c] - m) for lg in l]
        den = sum(e)
        out.append(sum(eg * og[c] for eg, og in zip(e, o)) / den)
    return [out]


def f_merge1(rc, pc):
    z = rc[3]
    n = len(rc[0])
    return [[y * _silu(z[b * n + c]) for c, y in enumerate(rc[b])] for b in range(3)]


def f_merge2(rc, pc):
    gate = rc[0]
    n = len(rc[1])
    return [[sum(jax.nn.sigmoid(gate[b * n + c]) * rc[1 + b][c] for b in range(3)) for c in range(n)]]


MLA_DQK = 2 * LANE
MLA_SCALE = MLA_QK ** -0.5


def _causal_mask(qi, kj, tq, tk):
    qpos = qi * tq + lax.broadcasted_iota(jnp.int32, (tq, tk), 0)
    kpos = kj * tk + lax.broadcasted_iota(jnp.int32, (tq, tk), 1)
    return kpos <= qpos


def _mla_attn_fwd(q, k, v):
    s = q.shape[0]
    h = q.shape[1] // MLA_DQK
    t = min(512, s)
    n = s // t

    def body(q_ref, k_ref, v_ref, o_ref, lse_ref, m_sc, l_sc, acc_sc):
        qi, kj = pl.program_id(1), pl.program_id(2)

        @pl.when(kj == 0)
        def _():
            m_sc[...] = jnp.full_like(m_sc, -jnp.inf)
            l_sc[...] = jnp.zeros_like(l_sc)
            acc_sc[...] = jnp.zeros_like(acc_sc)

        @pl.when(kj <= qi)
        def _():
            sc = lax.dot_general(q_ref[...], k_ref[...], (((1,), (1,)), ((), ())), preferred_element_type=F32) * MLA_SCALE
            sc = jnp.where(_causal_mask(qi, kj, t, t), sc, -jnp.inf)
            m_new = jnp.maximum(m_sc[...], jnp.max(sc, axis=-1, keepdims=True))
            alpha = jnp.exp(m_sc[...] - m_new)
            p = jnp.exp(sc - m_new)
            l_sc[...] = alpha * l_sc[...] + jnp.sum(p, axis=-1, keepdims=True)
            acc_sc[...] = alpha * acc_sc[...] + jnp.dot(p.astype(BF16), v_ref[...], preferred_element_type=F32)
            m_sc[...] = m_new

        @pl.when(kj == n - 1)
        def _():
            o_ref[...] = acc_sc[...] / l_sc[...]
            lse_ref[...] = jnp.broadcast_to(m_sc[...] + jnp.log(l_sc[...]), lse_ref.shape)

    return pl.pallas_call(
        body, out_shape=[jax.ShapeDtypeStruct((s, h * HEAD), F32)] * 2, grid=(h, n, n),
        in_specs=[pl.BlockSpec((t, MLA_DQK), lambda hh, i, j: (i, hh)),
                  pl.BlockSpec((t, MLA_DQK), lambda hh, i, j: (jnp.minimum(j, i), hh)),
                  pl.BlockSpec((t, HEAD), lambda hh, i, j: (jnp.minimum(j, i), hh))],
        out_specs=[pl.BlockSpec((t, HEAD), lambda hh, i, j: (i, hh))] * 2,
        scratch_shapes=[pltpu.VMEM((t, 1), F32), pltpu.VMEM((t, 1), F32), pltpu.VMEM((t, HEAD), F32)],
        compiler_params=_cparams("parallel", "parallel", "arbitrary"), name="mla_attn_fwd")(q, k, v)


def _mla_p_ds(q_ref, k_ref, v_ref, o_ref, lse_ref, do_ref, qi, kj, t):
    sc = lax.dot_general(q_ref[...], k_ref[...], (((1,), (1,)), ((), ())), preferred_element_type=F32) * MLA_SCALE
    p = jnp.where(_causal_mask(qi, kj, t, t), jnp.exp(sc - lse_ref[:, 0:1]), 0.0)
    do = do_ref[...]
    dp = lax.dot_general(do.astype(BF16), v_ref[...], (((1,), (1,)), ((), ())), preferred_element_type=F32)
    delta = jnp.sum(do * o_ref[...], axis=-1, keepdims=True)
    ds = p * (dp - delta) * MLA_SCALE
    return p, ds, do


def _mla_attn_bwd(q, k, v, o, lse, do):
    s = q.shape[0]
    h = q.shape[1] // MLA_DQK
    t = min(512, s)
    n = s // t

    def body_kv(q_ref, k_ref, v_ref, o_ref, lse_ref, do_ref, dk_ref, dv_ref, dk_sc, dv_sc):
        kj, qi = pl.program_id(1), pl.program_id(2)

        @pl.when(qi == 0)
        def _():
            dk_sc[...] = jnp.zeros_like(dk_sc)
            dv_sc[...] = jnp.zeros_like(dv_sc)

        @pl.when(qi >= kj)
        def _():
            p, ds, do = _mla_p_ds(q_ref, k_ref, v_ref, o_ref, lse_ref, do_ref, qi, kj, t)
            dv_sc[...] += lax.dot_general(p.astype(BF16), do.astype(BF16), (((0,), (0,)), ((), ())), preferred_element_type=F32)
            dk_sc[...] += lax.dot_general(ds.astype(BF16), q_ref[...], (((0,), (0,)), ((), ())), preferred_element_type=F32)

        @pl.when(qi == n - 1)
        def _():
            dk_ref[...] = dk_sc[...].astype(dk_ref.dtype)
            dv_ref[...] = dv_sc[...].astype(dv_ref.dtype)

    qmap = lambda hh, j, i: (jnp.maximum(i, j), hh)
    kmap = lambda hh, j, i: (j, hh)
    dk, dv = pl.pallas_call(
        body_kv, out_shape=[jax.ShapeDtypeStruct(k.shape, BF16), jax.ShapeDtypeStruct(v.shape, BF16)], grid=(h, n, n),
        in_specs=[pl.BlockSpec((t, MLA_DQK), qmap), pl.BlockSpec((t, MLA_DQK), kmap), pl.BlockSpec((t, HEAD), kmap),
                  pl.BlockSpec((t, HEAD), qmap), pl.BlockSpec((t, HEAD), qmap), pl.BlockSpec((t, HEAD), qmap)],
        out_specs=[pl.BlockSpec((t, MLA_DQK), kmap), pl.BlockSpec((t, HEAD), kmap)],
        scratch_shapes=[pltpu.VMEM((t, MLA_DQK), F32), pltpu.VMEM((t, HEAD), F32)],
        compiler_params=_cparams("parallel", "parallel", "arbitrary"), name="mla_attn_bwd_kv")(q, k, v, o, lse, do)

    def body_q(q_ref, k_ref, v_ref, o_ref, lse_ref, do_ref, dq_ref, dq_sc):
        qi, kj = pl.program_id(1), pl.program_id(2)

        @pl.when(kj == 0)
        def _():
            dq_sc[...] = jnp.zeros_like(dq_sc)

        @pl.when(kj <= qi)
        def _():
            _, ds, _ = _mla_p_ds(q_ref, k_ref, v_ref, o_ref, lse_ref, do_ref, qi, kj, t)
            dq_sc[...] += jnp.dot(ds.astype(BF16), k_ref[...], preferred_element_type=F32)

        @pl.when(kj == n - 1)
        def _():
            dq_ref[...] = dq_sc[...].astype(dq_ref.dtype)

    qmap2 = lambda hh, i, j: (i, hh)
    kmap2 = lambda hh, i, j: (jnp.minimum(j, i), hh)
    dq = pl.pallas_call(
        body_q, out_shape=jax.ShapeDtypeStruct(q.shape, BF16), grid=(h, n, n),
        in_specs=[pl.BlockSpec((t, MLA_DQK), qmap2), pl.BlockSpec((t, MLA_DQK), kmap2), pl.BlockSpec((t, HEAD), kmap2),
                  pl.BlockSpec((t, HEAD), qmap2), pl.BlockSpec((t, HEAD), qmap2), pl.BlockSpec((t, HEAD), qmap2)],
        out_specs=pl.BlockSpec((t, MLA_DQK), qmap2), scratch_shapes=[pltpu.VMEM((t, MLA_DQK), F32)],
        compiler_params=_cparams("parallel", "parallel", "arbitrary"), name="mla_attn_bwd_q")(q, k, v, o, lse, do)
    return dq, dk, dv


@jax.custom_vjp
def mla_attention(q, k, v):
    return _mla_attn_fwd(q, k, v)[0]


def _mla_attention_fwd(q, k, v):
    o, lse = _mla_attn_fwd(q, k, v)
    return o, (q, k, v, o, lse)


def _mla_attention_bwd(saved, do):
    return _mla_attn_bwd(*saved, do)


mla_attention.defvjp(_mla_attention_fwd, _mla_attention_bwd)


DIL_SCALE = HEAD ** -0.5
GROUP_W = 4 * HEAD


def _dil_scores(q, kp, kc, n):
    dn = (((1,), (1,)), ((), ()))
    sp = lax.dot_general(q, kp, dn, preferred_element_type=F32) * DIL_SCALE
    sc = lax.dot_general(q, kc, dn, preferred_element_type=F32) * DIL_SCALE
    qi = lax.broadcasted_iota(jnp.int32, sp.shape, 0)
    kc_i = lax.broadcasted_iota(jnp.int32, sp.shape, 1)
    vp = jnp.logical_and(kc_i >= qi, n > 0)
    vc = kc_i <= qi
    return sp, sc, vp, vc


def _dil_specs(d):
    cur = pl.BlockSpec((DIL_BLOCK, GROUP_W), lambda r, n: (n, r))
    prev = pl.BlockSpec((DIL_BLOCK, GROUP_W), lambda r, n: (jnp.maximum(n - 1, 0), r))
    return cur, prev


def _dil_fwd(q, k, v, d):
    l = q.shape[0]
    nb = l // DIL_BLOCK
    cur, prev = _dil_specs(d)

    def body(q_ref, kp_ref, kc_ref, vp_ref, vc_ref, o_ref, lse_ref):
        n = pl.program_id(1)
        for h in range(4):
            sl = slice(h * HEAD, (h + 1) * HEAD)
            sp, sc, vp, vc = _dil_scores(q_ref[:, sl], kp_ref[:, sl], kc_ref[:, sl], n)
            sp = jnp.where(vp, sp, -jnp.inf)
            sc = jnp.where(vc, sc, -jnp.inf)
            m = jnp.maximum(jnp.max(sp, axis=-1, keepdims=True), jnp.max(sc, axis=-1, keepdims=True))
            ep, ec = jnp.exp(sp - m), jnp.exp(sc - m)
            den = jnp.sum(ep, axis=-1, keepdims=True) + jnp.sum(ec, axis=-1, keepdims=True)
            acc = (jnp.dot(ep.astype(BF16), vp_ref[:, sl], preferred_element_type=F32)
                   + jnp.dot(ec.astype(BF16), vc_ref[:, sl], preferred_element_type=F32))
            o_ref[:, sl] = acc / den
            lse_ref[:, sl] = jnp.broadcast_to(m + jnp.log(den), (DIL_BLOCK, HEAD))

    return pl.pallas_call(
        body, out_shape=[jax.ShapeDtypeStruct(q.shape, F32)] * 2, grid=(d, nb),
        in_specs=[cur, prev, cur, prev, cur], out_specs=[cur, cur],
        compiler_params=_cparams("parallel", "parallel"), name=f"dil_fwd_d{d}")(q, k, k, v, v)


def _dil_bwd(q, k, v, o, lse, do, dlse, d):
    l = q.shape[0]
    nb = l // DIL_BLOCK
    cur, prev = _dil_specs(d)
    tn = (((0,), (0,)), ((), ()))
    nt = (((1,), (1,)), ((), ()))

    def body(q_ref, kp_ref, kc_ref, vp_ref, vc_ref, o_ref, lse_ref, do_ref, dl_ref,
             dq_ref, dkp_ref, dkc_ref, dvp_ref, dvc_ref):
        n = pl.program_id(1)
        for h in range(4):
            sl = slice(h * HEAD, (h + 1) * HEAD)
            qh = q_ref[:, sl]
            sp, sc, vp, vc = _dil_scores(qh, kp_ref[:, sl], kc_ref[:, sl], n)
            lse_h = lse_ref[:, h * HEAD:h * HEAD + 1]
            pp = jnp.where(vp, jnp.exp(sp - lse_h), 0.0)
            pc = jnp.where(vc, jnp.exp(sc - lse_h), 0.0)
            do_h = do_ref[:, sl]
            do_b = do_h.astype(BF16)
            corr = (jnp.sum(dl_ref[:, sl], axis=-1, keepdims=True)
                    - jnp.sum(do_h * o_ref[:, sl], axis=-1, keepdims=True))
            dsp = pp * (lax.dot_general(do_b, vp_ref[:, sl], nt, preferred_element_type=F32) + corr) * DIL_SCALE
            dsc = pc * (lax.dot_general(do_b, vc_ref[:, sl], nt, preferred_element_type=F32) + corr) * DIL_SCALE
            dsp_b, dsc_b = dsp.astype(BF16), dsc.astype(BF16)
            dq_ref[:, sl] = (jnp.dot(dsp_b, kp_ref[:, sl], preferred_element_type=F32)
                             + jnp.dot(dsc_b, kc_ref[:, sl], preferred_element_type=F32)).astype(dq_ref.dtype)
            dkp_ref[:, sl] = lax.dot_general(dsp_b, qh, tn, preferred_element_type=F32)
            dkc_ref[:, sl] = lax.dot_general(dsc_b, qh, tn, preferred_element_type=F32)
            dvp_ref[:, sl] = lax.dot_general(pp.astype(BF16), do_b, tn, preferred_element_type=F32)
            dvc_ref[:, sl] = lax.dot_general(pc.astype(BF16), do_b, tn, preferred_element_type=F32)

    dq, dkp, dkc, dvp, dvc = pl.pallas_call(
        body, out_shape=[jax.ShapeDtypeStruct(q.shape, BF16)] + [jax.ShapeDtypeStruct(q.shape, F32)] * 4, grid=(d, nb),
        in_specs=[cur, prev, cur, prev, cur, cur, cur, cur, cur], out_specs=[cur] * 5,
        compiler_params=_cparams("parallel", "parallel"), name=f"dil_bwd_d{d}")(q, k, k, v, v, o, lse, do, dlse)

    def fold(cur_part, prev_part):
        shifted = jnp.concatenate([prev_part[DIL_BLOCK:], jnp.zeros((DIL_BLOCK, prev_part.shape[1]), F32)], axis=0)
        return (cur_part + shifted).astype(BF16)

    return dq, fold(dkc, dkp), fold(dvc, dvp)


def _make_dil(d):
    @jax.custom_vjp
    def att(q, k, v):
        return tuple(_dil_fwd(q, k, v, d))

    def fwd(q, k, v):
        o, lse = _dil_fwd(q, k, v, d)
        return (o, lse), (q, k, v, o, lse)

    def bwd(saved, cts):
        return _dil_bwd(*saved, cts[0], cts[1], d)

    att.defvjp(fwd, bwd)
    return att


def dilated_group(q, k, v, d):
    s = q.shape[0]
    view = lambda t: t.reshape(s // d, d * GROUP_W)
    o, lse = _make_dil(d)(view(q), view(k), view(v))
    return o.reshape(s, GROUP_W), lse.reshape(s, GROUP_W)


def _dn_post(c, kind, scale):
    m = _silu(c)
    nrm = m * lax.rsqrt(jnp.sum(m * m, axis=-1, keepdims=True) + 1e-6) * scale
    return kind * nrm + (1.0 - kind) * m


def _dn_kind_scale(j):
    kind = jnp.where(j < 2 * DN_HEADS, 1.0, 0.0).astype(F32)
    scale = jnp.where(j < DN_HEADS, DN_DIM ** -0.5, 1.0).astype(F32)
    return kind, scale


_CONV_RB = 512
_CONV_PAD = 8


def _conv_windows(pad_ref, w_ref, r0, rb, sign):
    acc = None
    for sh in range(DN_CONV):
        win = pad_ref[pl.ds(r0 + _CONV_PAD * (sign < 0) + sign * sh, rb), :]
        term = w_ref[DN_CONV - 1 - sh:DN_CONV - sh, :] * win
        acc = term if acc is None else acc + term
    return acc


def _dn_conv_fwd(x, w):
    s, width = x.shape
    rb = min(_CONV_RB, s)

    def body(x_ref, w_ref, o_ref, pad_ref):
        kind, scale = _dn_kind_scale(pl.program_id(0))
        pad_ref[0:_CONV_PAD, :] = jnp.zeros((_CONV_PAD, LANE), F32)
        pad_ref[_CONV_PAD:, :] = x_ref[...]
        for r0 in range(0, s, rb):
            c = _conv_windows(pad_ref, w_ref, r0, rb, -1)
            o_ref[r0:r0 + rb, :] = _dn_post(c, kind, scale)

    return pl.pallas_call(
        body, out_shape=jax.ShapeDtypeStruct(x.shape, F32), grid=(width // LANE,),
        in_specs=[pl.BlockSpec((s, LANE), lambda j: (0, j)), pl.BlockSpec((DN_CONV, LANE), lambda j: (0, j))],
        out_specs=pl.BlockSpec((s, LANE), lambda j: (0, j)), scratch_shapes=[pltpu.VMEM((s + _CONV_PAD, LANE), F32)],
        compiler_params=_cparams("parallel"), name="dn_conv_fwd")(x, w)


def _dn_conv_bwd(x, w, dy):
    s, width = x.shape
    rb = min(_CONV_RB, s)

    def body(x_ref, w_ref, dy_ref, dx_ref, dw_ref, pad_ref, dpad_ref):
        kind, scale = _dn_kind_scale(pl.program_id(0))
        pad_ref[0:_CONV_PAD, :] = jnp.zeros((_CONV_PAD, LANE), F32)
        pad_ref[_CONV_PAD:, :] = x_ref[...]
        dpad_ref[s:, :] = jnp.zeros((_CONV_PAD, LANE), F32)
        dws = [jnp.zeros((1, LANE), F32) for _ in range(DN_CONV)]
        for r0 in range(0, s, rb):
            c = _conv_windows(pad_ref, w_ref, r0, rb, -1)
            _, vjp = jax.vjp(lambda cc: _dn_post(cc, kind, scale), c)
            dc = vjp(dy_ref[r0:r0 + rb, :])[0]
            dpad_ref[r0:r0 + rb, :] = dc
            for sh in range(DN_CONV):
                win = pad_ref[pl.ds(r0 + _CONV_PAD - sh, rb), :]
                dws[DN_CONV - 1 - sh] = dws[DN_CONV - 1 - sh] + jnp.sum(dc * win, axis=0, keepdims=True)
        for j in range(DN_CONV):
            dw_ref[j:j + 1, :] = dws[j]
        for r0 in range(0, s, rb):
            dx_ref[r0:r0 + rb, :] = _conv_windows(dpad_ref, w_ref, r0, rb, 1)

    return pl.pallas_call(
        body, out_shape=[jax.ShapeDtypeStruct(x.shape, F32), jax.ShapeDtypeStruct(w.shape, F32)], grid=(width // LANE,),
        in_specs=[pl.BlockSpec((s, LANE), lambda j: (0, j)), pl.BlockSpec((DN_CONV, LANE), lambda j: (0, j)),
                  pl.BlockSpec((s, LANE), lambda j: (0, j))],
        out_specs=[pl.BlockSpec((s, LANE), lambda j: (0, j)), pl.BlockSpec((DN_CONV, LANE), lambda j: (0, j))],
        scratch_shapes=[pltpu.VMEM((s + _CONV_PAD, LANE), F32), pltpu.VMEM((s + _CONV_PAD, LANE), F32)],
        compiler_params=_cparams("parallel"), name="dn_conv_bwd")(x, w, dy)


@jax.custom_vjp
def dn_conv(x, w):
    return _dn_conv_fwd(x, w)


dn_conv.defvjp(lambda x, w: (_dn_conv_fwd(x, w), (x, w)), lambda saved, dy: tuple(_dn_conv_bwd(*saved, dy)))


_NN = (((1,), (0,)), ((), ()))
_NT = (((1,), (1,)), ((), ()))
_TN = (((0,), (0,)), ((), ()))


def _bd(a, b, dims, passes=1):
    d = lambda x, y: lax.dot_general(x, y, dims, preferred_element_type=F32)
    if passes == 0:
        return lax.dot_general(a, b, dims, precision=lax.Precision.HIGHEST, preferred_element_type=F32)
    ah, bh = a.astype(BF16), b.astype(BF16)
    if passes == 1:
        return d(ah, bh)
    al, bl = (a - ah.astype(F32)).astype(BF16), (b - bh.astype(F32)).astype(BF16)
    return d(ah, bh) + d(ah, bl) + d(al, bh)


@functools.partial(jax.custom_vjp, nondiff_argnums=(2, 3))
def _pdot(a, b, dims, passes):
    return _bd(a, b, dims, passes)


def _pdot_bwd(dims, passes, saved, g):
    a, b = saved
    if dims == _NN:
        return _bd(g, b, _NT, passes), _bd(a, g, _TN, passes)
    if dims == _NT:
        return _bd(g, b, _NN, passes), _bd(g, a, _TN, passes)
    return _bd(b, g, _NT, passes), _bd(a, g, _NN, passes)


_pdot.defvjp(lambda a, b, dims, passes: (_bd(a, b, dims, passes), (a, b)), _pdot_bwd)


GDN_DOT_PASSES = 1
GDN_SOLVE_PASSES = 3


def _hdot(a, b, dims=_NN):
    return _pdot(a, b, dims, GDN_DOT_PASSES)


def _xdot(a, b, dims=_NN):
    return _pdot(a, b, dims, GDN_SOLVE_PASSES)


def _split3(x):
    hi = x.astype(BF16)
    r1 = x - hi.astype(F32)
    mid = r1.astype(BF16)
    lo = (r1 - mid.astype(F32)).astype(BF16)
    return hi, mid, lo


def _tri_dot(tri, x, dims):
    t = tri.astype(BF16)
    return sum(lax.dot_general(t, p, dims, preferred_element_type=F32) for p in _split3(x))


@jax.custom_vjp
def _cumsum_rows(x):
    c = x.shape[0]
    tri = lax.broadcasted_iota(jnp.int32, (c, c), 0) >= lax.broadcasted_iota(jnp.int32, (c, c), 1)
    return _tri_dot(tri, x, _NN)


def _cumsum_rows_bwd(_, g):
    c = g.shape[0]
    tri = lax.broadcasted_iota(jnp.int32, (c, c), 0) >= lax.broadcasted_iota(jnp.int32, (c, c), 1)
    return (_tri_dot(tri, g, _TN),)


_cumsum_rows.defvjp(lambda x: (_cumsum_rows(x), None), _cumsum_rows_bwd)


def _gdn_prep_fn(qkv, gb):
    c = GDN_CHUNK
    row = lax.broadcasted_iota(jnp.int32, (c, c), 0)
    col = lax.broadcasted_iota(jnp.int32, (c, c), 1)
    incl, strict = row >= col, row > col
    lane = _lanes(gb)
    outs = [[] for _ in range(6)]
    for h in range(DN_HEADS):
        q, k, v = qkv[h], qkv[DN_HEADS + h], qkv[2 * DN_HEADS + h]
        g = jnp.sum(jnp.where(lane == h, gb, 0.0), axis=-1, keepdims=True)
        beta = jnp.sum(jnp.where(lane == DN_HEADS + h, gb, 0.0), axis=-1, keepdims=True)
        gcb = _cumsum_rows(jnp.broadcast_to(g, (c, c)))
        gct = gcb.T
        decay = jnp.where(incl, jnp.exp(jnp.where(incl, gcb - gct, 0.0)), 0.0)
        kb = k * beta
        a = jnp.where(strict, _hdot(kb, k, _NT) * decay, 0.0)
        pw = -a
        t = (row == col).astype(F32) + pw
        for _ in range(int(math.log2(c)) - 1):
            pw = _xdot(pw, pw)
            t = t + _xdot(t, pw)
        eg = jnp.exp(gcb)
        u = _xdot(t, v * beta)
        w = _xdot(t, kb * eg)
        qk = jnp.where(incl, _hdot(q, k, _NT) * decay, 0.0)
        g_last = jnp.sum(jnp.where(row == c - 1, gcb, 0.0), axis=0, keepdims=True)
        kg = k * jnp.exp(g_last - gcb)
        for lst, val in zip(outs, (u, w, q * eg, kg, qk, jnp.broadcast_to(jnp.exp(g_last), (8, LANE)))):
            lst.append(val)
    return outs


def _gdn_prep_specs(nq):
    c = GDN_CHUNK
    big = pl.BlockSpec((c, nq * LANE), lambda n: (n, 0))
    return big


def _gdn_prep_fwd(qkv, gb):
    s = qkv.shape[0]
    c = GDN_CHUNK
    n = s // c
    hw = DN_HEADS * LANE

    def body(qkv_ref, gb_ref, u_ref, w_ref, qg_ref, kg_ref, qk_ref, el_ref):
        res = _gdn_prep_fn(_chunks(qkv_ref), gb_ref[...])
        for ref, chs in zip((u_ref, w_ref, qg_ref, kg_ref, qk_ref, el_ref), res):
            _store(ref, chs)

    row = pl.BlockSpec((c, hw), lambda i: (i, 0))
    return pl.pallas_call(
        body, out_shape=[jax.ShapeDtypeStruct((s, hw), F32)] * 5 + [jax.ShapeDtypeStruct((n * 8, hw), F32)], grid=(n,),
        in_specs=[pl.BlockSpec((c, 3 * hw), lambda i: (i, 0)), pl.BlockSpec((c, LANE), lambda i: (i, 0))],
        out_specs=[row] * 5 + [pl.BlockSpec((8, hw), lambda i: (i, 0))],
        compiler_params=_cparams("parallel"), name="gdn_prep_fwd")(qkv, gb)


def _gdn_prep_bwd(qkv, gb, cts):
    s = qkv.shape[0]
    c = GDN_CHUNK
    n = s // c
    hw = DN_HEADS * LANE

    def body(qkv_ref, gb_ref, du_ref, dw_ref, dqg_ref, dkg_ref, dqk_ref, del_ref, dqkv_ref, dgb_ref):
        _, vjp = jax.vjp(_gdn_prep_fn, _chunks(qkv_ref), gb_ref[...])
        ct = [_chunks(r) for r in (du_ref, dw_ref, dqg_ref, dkg_ref, dqk_ref, del_ref)]
        dqkv, dgb = vjp(ct)
        _store(dqkv_ref, dqkv)
        dgb_ref[...] = dgb

    row = pl.BlockSpec((c, hw), lambda i: (i, 0))
    return pl.pallas_call(
        body, out_shape=[jax.ShapeDtypeStruct(qkv.shape, F32), jax.ShapeDtypeStruct(gb.shape, F32)], grid=(n,),
        in_specs=[pl.BlockSpec((c, 3 * hw), lambda i: (i, 0)), pl.BlockSpec((c, LANE), lambda i: (i, 0))]
        + [row] * 5 + [pl.BlockSpec((8, hw), lambda i: (i, 0))],
        out_specs=[pl.BlockSpec((c, 3 * hw), lambda i: (i, 0)), pl.BlockSpec((c, LANE), lambda i: (i, 0))],
        compiler_params=_cparams("parallel"), name="gdn_prep_bwd")(qkv, gb, *cts)


@jax.custom_vjp
def gdn_prep(qkv, gb):
    return tuple(_gdn_prep_fwd(qkv, gb))


gdn_prep.defvjp(lambda qkv, gb: (tuple(_gdn_prep_fwd(qkv, gb)), (qkv, gb)),
                lambda saved, cts: tuple(_gdn_prep_bwd(*saved, cts)))


def _gdn_scan_fwd(u, w, qg, kg, qk, el):
    s = u.shape[0]
    c = GDN_CHUNK
    n = s // c
    hw = DN_HEADS * LANE

    def body(u_ref, w_ref, qg_ref, kg_ref, qk_ref, el_ref, o_ref, st_ref, s_sc):
        @pl.when(pl.program_id(0) == 0)
        def _():
            s_sc[...] = jnp.zeros_like(s_sc)

        for h in range(DN_HEADS):
            sl = slice(h * LANE, (h + 1) * LANE)
            st = s_sc[h]
            st_ref[sl, :] = st
            v_new = u_ref[:, sl] - _hdot(w_ref[:, sl], st)
            o_ref[:, sl] = _hdot(qg_ref[:, sl], st) + _hdot(qk_ref[:, sl], v_new)
            s_sc[h] = st * el_ref[0:1, sl] + _hdot(kg_ref[:, sl], v_new, _TN)

    row = pl.BlockSpec((c, hw), lambda i: (i, 0))
    return pl.pallas_call(
        body, out_shape=[jax.ShapeDtypeStruct((s, hw), F32), jax.ShapeDtypeStruct((n, hw, LANE), F32)], grid=(n,),
        in_specs=[row] * 5 + [pl.BlockSpec((8, hw), lambda i: (i, 0))],
        out_specs=[row, pl.BlockSpec((None, hw, LANE), lambda i: (i, 0, 0))],
        scratch_shapes=[pltpu.VMEM((DN_HEADS, LANE, LANE), F32)],
        compiler_params=_cparams("arbitrary"), name="gdn_scan_fwd")(u, w, qg, kg, qk, el)


def _gdn_scan_bwd(u, w, qg, kg, qk, el, states, do):
    s = u.shape[0]
    c = GDN_CHUNK
    n = s // c
    hw = DN_HEADS * LANE

    def body(u_ref, w_ref, qg_ref, kg_ref, qk_ref, el_ref, st_ref, do_ref,
             du_ref, dw_ref, dqg_ref, dkg_ref, dqk_ref, del_ref, ds_sc):
        @pl.when(pl.program_id(0) == 0)
        def _():
            ds_sc[...] = jnp.zeros_like(ds_sc)

        for h in range(DN_HEADS):
            sl = slice(h * LANE, (h + 1) * LANE)
            st, ds, do_h = st_ref[sl, :], ds_sc[h], do_ref[:, sl]
            wv, qgv, kgv, qkv_ = w_ref[:, sl], qg_ref[:, sl], kg_ref[:, sl], qk_ref[:, sl]
            v_new = u_ref[:, sl] - _hdot(wv, st)
            dv_new = _hdot(qkv_, do_h, _TN) + _hdot(kgv, ds)
            du_ref[:, sl] = dv_new
            dw_ref[:, sl] = -_hdot(dv_new, st, _NT)
            dqg_ref[:, sl] = _hdot(do_h, st, _NT)
            dqk_ref[:, sl] = _hdot(do_h, v_new, _NT)
            dkg_ref[:, sl] = _hdot(v_new, ds, _NT)
            del_ref[:, sl] = jnp.where(lax.broadcasted_iota(jnp.int32, (8, LANE), 0) == 0,
                                       jnp.sum(st * ds, axis=0, keepdims=True), 0.0)
            ds_sc[h] = _hdot(qgv, do_h, _TN) + ds * el_ref[0:1, sl] - _hdot(wv, dv_new, _TN)

    row = pl.BlockSpec((c, hw), lambda i: (n - 1 - i, 0))
    small = pl.BlockSpec((8, hw), lambda i: (n - 1 - i, 0))
    return pl.pallas_call(
        body, out_shape=[jax.ShapeDtypeStruct((s, hw), F32)] * 5 + [jax.ShapeDtypeStruct((n * 8, hw), F32)], grid=(n,),
        in_specs=[row] * 5 + [small, pl.BlockSpec((None, hw, LANE), lambda i: (n - 1 - i, 0, 0)), row],
        out_specs=[row] * 5 + [small], scratch_shapes=[pltpu.VMEM((DN_HEADS, LANE, LANE), F32)],
        compiler_params=_cparams("arbitrary"), name="gdn_scan_bwd")(u, w, qg, kg, qk, el, states, do)


@jax.custom_vjp
def gdn_scan(u, w, qg, kg, qk, el):
    return _gdn_scan_fwd(u, w, qg, kg, qk, el)[0]


def _gdn_scan_vfwd(*args):
    o, states = _gdn_scan_fwd(*args)
    return o, args + (states,)


gdn_scan.defvjp(_gdn_scan_vfwd, lambda saved, do: tuple(_gdn_scan_bwd(*saved, do)))


def _loss_call(y, t):
    s, d = y.shape
    br = min(ROW_BLOCK, s)
    n = s // br

    def body(y_ref, t_ref, loss_ref, dy_ref, acc):
        i = pl.program_id(0)

        @pl.when(i == 0)
        def _():
            acc[...] = jnp.zeros_like(acc)

        e = y_ref[...] - t_ref[...]
        dy_ref[...] = e / d
        acc[...] += jnp.sum(e * e, axis=0, keepdims=True)

        @pl.when(i == n - 1)
        def _():
            loss_ref[...] = jnp.broadcast_to(jnp.sum(acc[...], axis=1, keepdims=True) * (0.5 / d), loss_ref.shape)

    row = pl.BlockSpec((br, d), lambda i: (i, 0))
    return pl.pallas_call(
        body, out_shape=[jax.ShapeDtypeStruct((1, LANE), F32), jax.ShapeDtypeStruct((s, d), F32)], grid=(n,),
        in_specs=[row, row], out_specs=[pl.BlockSpec((1, LANE), lambda i: (0, 0)), row],
        scratch_shapes=[pltpu.VMEM((1, d), F32)], compiler_params=_cparams("arbitrary"), name="loss_head")(y, t)


@jax.custom_vjp
def loss_head(y, t):
    return _loss_call(y, t)[0][0, 0]


def _loss_head_fwd(y, t):
    loss, dy = _loss_call(y, t)
    return loss[0, 0], (dy,)


loss_head.defvjp(_loss_head_fwd, lambda saved, g: (saved[0] * g, -saved[0] * g))


IN_OFF = {}
_o = 0
for _name, _size in (('q_lat', 384), ('kv_lat', 320), ('z_a', 512), ('dn_qkv', 1536), ('dn_ab', 8), ('z_b', 512),
                     ('dil_qkv', 4608), ('z_c', 512), ('gate', 3072)):
    IN_OFF[_name] = (_o, _o + _size)
    _o += _size
IN_WIDTH = _o
N_CHIPS = 4


def _cols(w, name):
    a, b = IN_OFF[name]
    return w[:, a:b]


def _pad_cols(w, to):
    return jnp.concatenate([w, jnp.zeros((w.shape[0], to - w.shape[1]), w.dtype)], axis=1)


def _pad_row(v, to=None):
    v = v.reshape(1, -1)
    return v if to is None or v.shape[1] == to else _pad_cols(v, to)


def _shard_cols(pieces, a, b):
    wsh = pieces[0].shape[1]
    parts = [pieces[j][:, max(a, j * wsh) - j * wsh:min(b, (j + 1) * wsh) - j * wsh]
             for j in range(len(pieces)) if max(a, j * wsh) < min(b, (j + 1) * wsh)]
    return parts[0] if len(parts) == 1 else jnp.concatenate(parts, axis=1)


def _win_groups_impl(w_in4):
    out = []
    for l in range(w_in4.shape[1]):
        pieces = [w_in4[j, l] for j in range(w_in4.shape[0])]
        cols = lambda name: _shard_cols(pieces, *IN_OFF[name])
        out.append((_pad_cols(jnp.concatenate([cols('q_lat'), cols('kv_lat')], axis=1), 768),
                    jnp.concatenate([cols('z_a'), cols('z_b'), cols('z_c')], axis=1),
                    cols('dn_qkv'), _pad_cols(cols('dn_ab'), LANE), cols('dil_qkv'), cols('gate')))
    return tuple(out)


@jax.custom_vjp
def win_groups(w_in4):
    return _win_groups_impl(w_in4)


def _win_groups_bwd(_, cts):
    n_chip, depth = N_CHIPS, len(cts)
    wsh = IN_WIDTH // n_chip
    per_layer = []
    for l in range(depth):
        dmla, dz, dgdn, dab, ddil, dgate = cts[l]
        full = jnp.concatenate([dmla[:, :704], dz[:, :512], dgdn, dab[:, :8], dz[:, 512:1024], ddil, dz[:, 1024:], dgate], axis=1)
        per_layer.append([full[:, j * wsh:(j + 1) * wsh] for j in range(n_chip)])
    return (jnp.stack([jnp.stack([per_layer[l][j] for l in range(depth)]) for j in range(n_chip)]),)


win_groups.defvjp(lambda w: (_win_groups_impl(w), None), _win_groups_bwd)


def _layer(x, p, tabs):
    w_mla, w_z, w_gdn, w_ab, w_dil, w_gate = p['w_in_groups']
    h, = rowwise(f_rms_full, "rms_in", [x], [_pad_row(p['norm_g'])], [(D_MODEL, BF16)])
    mla_in = matmul(h, w_mla)
    z = matmul(h, w_z)
    dn_qkv = matmul(h, w_gdn)
    dn_ab = matmul(h, w_ab)
    dil_qkv = matmul(h, w_dil)
    gate = matmul(h, w_gate)

    qn, kvn, kpe = rowwise(f_mla1, "mla_norm", [mla_in], [_pad_row(p['mla_q_a_norm_g']), _pad_row(p['mla_kv_a_norm_g'])],
                           [(MLA_Q_RANK, BF16), (MLA_KV_RANK, BF16), (LANE, F32)])
    wq = p['mla_w_q_b']
    wq_pad = jnp.concatenate(
        [wq[:, hh * MLA_QK:hh * MLA_QK + MLA_NOPE] for hh in range(MLA_HEADS)]
        + [_pad_cols(wq[:, hh * MLA_QK + MLA_NOPE:(hh + 1) * MLA_QK], LANE) for hh in range(MLA_HEADS)], axis=1)
    q = matmul(qn, wq_pad)
    kv = matmul(kvn, p['mla_w_kv_b'])
    gq, gk = p['mla_q_norm_g'], p['mla_k_norm_g']
    q_att, k_att, v_att = rowwise(
        f_mla2, "mla_qk", [q, kv, kpe, tabs['cos_r'], tabs['sin_r']],
        [_pad_row(gq[:MLA_NOPE]), _pad_row(gq[MLA_NOPE:], LANE), _pad_row(gk[:MLA_NOPE]), _pad_row(gk[MLA_NOPE:], LANE)],
        [(MLA_HEADS * MLA_DQK, BF16), (MLA_HEADS * MLA_DQK, BF16), (MLA_HEADS * HEAD, BF16)], nograd=(3, 4))
    y_a = mla_attention(q_att, k_att, v_att)

    qkv_n = dn_conv(dn_qkv, p['dn_conv_w'])
    gb, = rowwise(f_gates, "dn_gates", [dn_ab], [_pad_row(p['dn_a_log'], LANE), _pad_row(p['dn_dt_bias'], LANE)], [(LANE, F32)])
    o_b = gdn_scan(*gdn_prep(qkv_n, gb))
    y_b, = rowwise(f_headnorm, "dn_out_norm", [o_b], [_pad_row(p['dn_out_norm_g'])], [(DN_HEADS * DN_DIM, F32)])

    qd, kd, vd = rowwise(f_dil, "dil_qk", [dil_qkv, tabs['cos_h'], tabs['sin_h']],
                         [_pad_row(p['dil_q_norm_g']), _pad_row(p['dil_k_norm_g'])],
                         [(3 * GROUP_W, BF16)] * 3, nograd=(1, 2))
    outs, lses = [], []
    for gi, d in enumerate(DIL_DILATIONS):
        sl = slice(gi * GROUP_W, (gi + 1) * GROUP_W)
        o_g, l_g = dilated_group(qd[:, sl], kd[:, sl], vd[:, sl], d)
        outs.append(o_g)
        lses.append(l_g)
    y_c, = rowwise(f_comb, "dil_comb", outs + lses, [], [(GROUP_W, F32)])

    ys = rowwise(f_merge1, "merge_silu", [y_a, y_b, y_c, z], [], [(BRANCH_W, BF16)] * 3)
    bo = [matmul(ys[b], p['w_branch'][b]) for b in range(3)]
    mixed, = rowwise(f_merge2, "merge_gate", [gate] + bo, [], [(D_MODEL, BF16)])
    return matmul(mixed, p['w_out'], res=x)


def _rope_tables(pos, dim):
    inv_freq = 1.0 / (ROPE_THETA ** (jnp.arange(0, dim, 2, dtype=F32) / dim))
    ang = pos.astype(F32)[:, None] * inv_freq
    return jnp.cos(ang), jnp.sin(ang)


def _tables(pos):
    cr, sr = _rope_tables(pos, MLA_ROPE)
    ch, sh = _rope_tables(pos, HEAD)
    zero = jnp.zeros((pos.shape[0], LANE - MLA_ROPE), F32)
    return {'cos_r': jnp.concatenate([cr, cr, zero], axis=1), 'sin_r': jnp.concatenate([sr, sr, zero], axis=1),
            'cos_h': jnp.concatenate([ch, ch], axis=1), 'sin_h': jnp.concatenate([sh, sh], axis=1)}


def _local_loss(w_in4, mats, conv_w, small, x, target, tabs):
    groups = win_groups(w_in4)
    for l in range(DEPTH):
        p = {k: v[l] for k, v in mats.items()}
        p.update({k: v[l] for k, v in small.items()})
        p['dn_conv_w'] = conv_w[l]
        p['w_in_groups'] = groups[l]
        x = _layer(x, p, tabs)
    return loss_head(x, target)


def _pack(arrays, dtype, row_tile):
    flat = jnp.concatenate([a.astype(dtype).reshape(-1) for a in arrays])
    rows = -(-flat.shape[0] // (LANE * row_tile)) * row_tile
    flat = jnp.concatenate([flat, jnp.zeros((rows * LANE - flat.shape[0],), dtype)])
    return flat.reshape(rows, LANE)


def _unpack_impl(buf, shapes):
    flat = buf.reshape(-1)
    out, off = [], 0
    for shp in shapes:
        n = math.prod(shp)
        out.append(flat[off:off + n].reshape(shp))
        off += n
    return tuple(out)


@functools.partial(jax.custom_vjp, nondiff_argnums=(1, 2, 3))
def _unpack_p(buf, shapes, dtype_name, rows):
    return _unpack_impl(buf, shapes)


_unpack_p.defvjp(lambda buf, shapes, dtype_name, rows: (_unpack_impl(buf, shapes), None),
                 lambda shapes, dtype_name, rows, _, cts: (_pack(cts, jnp.dtype(dtype_name), rows),))


def _unpack(buf, shapes):
    return _unpack_p(buf, tuple(shapes), jnp.dtype(buf.dtype).name, buf.shape[0])


def _full_from_chips(buf4, shard_shapes):
    per_chip = [_unpack(buf4[j], tuple(shard_shapes)) for j in range(4)]
    return {name: jnp.concatenate([per_chip[j][i] for j in range(4)], axis=axis) for i, (name, axis) in enumerate(MATS)}


_HBM = pl.BlockSpec(memory_space=pltpu.HBM)
_VMEM = pl.BlockSpec(memory_space=pltpu.VMEM)


def _chip_peers(x, y):
    return [(1 - x, y), (x, 1 - y), (1 - x, 1 - y)]


def _chip_exchange(arrays, gather, name):
    n = len(arrays)

    def body(*refs):
        in_refs, out_refs = refs[:n], refs[n:2 * n]
        send_sems, recv_sems, local_sems = refs[2 * n:]
        x, y, c = lax.axis_index("x"), lax.axis_index("y"), lax.axis_index("c")
        me = 2 * x + y
        peers = _chip_peers(x, y)
        local, sends = [], []
        for a, (i_ref, o_ref) in enumerate(zip(in_refs, out_refs)):
            local.append(pltpu.make_async_copy(i_ref if gather else i_ref.at[me], o_ref.at[me], local_sems.at[a]))
            local[-1].start()
            for k, (px, py) in enumerate(peers):
                sends.append(pltpu.make_async_remote_copy(
                    src_ref=i_ref if gather else i_ref.at[2 * px + py], dst_ref=o_ref.at[me], send_sem=send_sems.at[3 * a + k],
                    recv_sem=recv_sems.at[3 * a + k], device_id=(px, py, c), device_id_type=MESH))
                sends[-1].start()
        for a, (i_ref, o_ref) in enumerate(zip(in_refs, out_refs)):
            for k, (px, py) in enumerate(peers):
                pltpu.make_async_remote_copy(
                    src_ref=i_ref if gather else i_ref.at[me], dst_ref=o_ref.at[2 * px + py], send_sem=send_sems.at[3 * a + k],
                    recv_sem=recv_sems.at[3 * a + k], device_id=(px, py, c), device_id_type=MESH).wait_recv()
        for cp in sends:
            cp.wait_send()
        for cp in local:
            cp.wait()

    out_shape = [jax.ShapeDtypeStruct(((4,) + a.shape) if gather else a.shape, a.dtype) for a in arrays]
    return pl.pallas_call(
        body, out_shape=out_shape, in_specs=[_HBM] * n, out_specs=[_HBM] * n,
        scratch_shapes=[pltpu.SemaphoreType.DMA((3 * n,)), pltpu.SemaphoreType.DMA((3 * n,)), pltpu.SemaphoreType.DMA((n,))],
        name=name)(*arrays)


def chip_all_gather(shards):
    return _chip_exchange(shards, True, "chip_all_gather")


def chip_all_to_all(pieces):
    return _chip_exchange(pieces, False, "chip_all_to_all")


def sibling_swap(arrays):
    n = len(arrays)

    def body(*refs):
        in_refs, out_refs = refs[:n], refs[n:2 * n]
        send_sems, recv_sems = refs[2 * n:]
        x, y, c = lax.axis_index("x"), lax.axis_index("y"), lax.axis_index("c")
        cps = [pltpu.make_async_remote_copy(src_ref=i_ref, dst_ref=o_ref, send_sem=send_sems.at[a], recv_sem=recv_sems.at[a],
                                            device_id=(x, y, 1 - c), device_id_type=MESH)
               for a, (i_ref, o_ref) in enumerate(zip(in_refs, out_refs))]
        for cp in cps:
            cp.start()
        for cp in cps:
            cp.wait()

    return pl.pallas_call(
        body, out_shape=[jax.ShapeDtypeStruct(a.shape, a.dtype) for a in arrays], in_specs=[_HBM] * n, out_specs=[_HBM] * n,
        scratch_shapes=[pltpu.SemaphoreType.DMA((n,)), pltpu.SemaphoreType.DMA((n,))], name="sibling_swap")(*arrays)


def all_gather8(v, name):
    def body(v_ref, out_ref, send_sems, recv_sems):
        x, y, c = lax.axis_index("x"), lax.axis_index("y"), lax.axis_index("c")
        out_ref[4 * x + 2 * y + c] = v_ref[...]

        def peer(k):
            return (x ^ (k >> 2), y ^ ((k >> 1) & 1), c ^ (k & 1))

        sends = [pltpu.make_async_remote_copy(src_ref=v_ref, dst_ref=out_ref.at[4 * x + 2 * y + c], send_sem=send_sems.at[k - 1],
                                              recv_sem=recv_sems.at[k - 1], device_id=peer(k), device_id_type=MESH)
                 for k in range(1, 8)]
        for cp in sends:
            cp.start()
        for k in range(1, 8):
            px, py, pc = peer(k)
            pltpu.make_async_remote_copy(src_ref=v_ref, dst_ref=out_ref.at[4 * px + 2 * py + pc], send_sem=send_sems.at[k - 1],
                                         recv_sem=recv_sems.at[k - 1], device_id=peer(k), device_id_type=MESH).wait_recv()
        for cp in sends:
            cp.wait_send()

    return pl.pallas_call(
        body, out_shape=jax.ShapeDtypeStruct((8,) + v.shape, v.dtype), in_specs=[_VMEM], out_specs=_VMEM,
        scratch_shapes=[pltpu.SemaphoreType.DMA((7,)), pltpu.SemaphoreType.DMA((7,))], name=name)(v)


def sum_blocks(blocks, row_tile, name):
    n, rows, width = blocks.shape

    def body(b_ref, o_ref):
        acc = b_ref[0].astype(F32)
        for j in range(1, n):
            acc = acc + b_ref[j].astype(F32)
        o_ref[...] = acc

    return pl.pallas_call(
        body, out_shape=jax.ShapeDtypeStruct((rows, width), F32), grid=(rows // row_tile,),
        in_specs=[pl.BlockSpec((n, row_tile, width), lambda i: (0, i, 0))],
        out_specs=pl.BlockSpec((row_tile, width), lambda i: (i, 0)), compiler_params=_cparams("parallel"), name=name)(blocks)


def adamw(g_parts, w, m, v, row_tile, name):
    npart = len(g_parts)

    def body(*refs):
        g = refs[0][...]
        for r in refs[1:npart]:
            g = g + r[...]
        w_ref, m_ref, v_ref, g_out, d_out, m_out, v_out = refs[npart:]
        m_new = ADAM_B1 * m_ref[...] + (1.0 - ADAM_B1) * g
        v_new = ADAM_B2 * v_ref[...] + (1.0 - ADAM_B2) * (g * g)
        m_hat = m_new / (1.0 - ADAM_B1 ** ADAM_STEP)
        v_hat = v_new / (1.0 - ADAM_B2 ** ADAM_STEP)
        g_out[...] = g
        d_out[...] = -ADAM_LR * (m_hat / (jnp.sqrt(v_hat) + ADAM_EPS) + ADAM_WD * w_ref[...])
        m_out[...] = m_new
        v_out[...] = v_new

    rows, width = w.shape
    spec = pl.BlockSpec((row_tile, width), lambda i: (i, 0))
    return pl.pallas_call(
        body, out_shape=[jax.ShapeDtypeStruct(w.shape, F32)] * 4, grid=(rows // row_tile,),
        in_specs=[spec] * (npart + 3), out_specs=[spec] * 4, compiler_params=_cparams("parallel"), name=name)(*g_parts, w, m, v)


def kernel(x, positions, norm_g, w_in, mla_q_a_norm_g, mla_w_q_b, mla_kv_a_norm_g, mla_w_kv_b, mla_q_norm_g, mla_k_norm_g, dn_conv_w, dn_a_log, dn_dt_bias, dn_out_norm_g, dil_q_norm_g, dil_k_norm_g, w_branch, w_out, loss_target, m_norm_g, m_w_in, m_mla_q_a_norm_g, m_mla_w_q_b, m_mla_kv_a_norm_g, m_mla_w_kv_b, m_mla_q_norm_g, m_mla_k_norm_g, m_dn_conv_w, m_dn_a_log, m_dn_dt_bias, m_dn_out_norm_g, m_dil_q_norm_g, m_dil_k_norm_g, m_w_branch, m_w_out, v_norm_g, v_w_in, v_mla_q_a_norm_g, v_mla_w_q_b, v_mla_kv_a_norm_g, v_mla_w_kv_b, v_mla_q_norm_g, v_mla_k_norm_g, v_dn_conv_w, v_dn_a_log, v_dn_dt_bias, v_dn_out_norm_g, v_dil_q_norm_g, v_dil_k_norm_g, v_w_branch, v_w_out):
    w = dict(norm_g=norm_g, w_in=w_in, mla_q_a_norm_g=mla_q_a_norm_g, mla_w_q_b=mla_w_q_b, mla_kv_a_norm_g=mla_kv_a_norm_g,
             mla_w_kv_b=mla_w_kv_b, mla_q_norm_g=mla_q_norm_g, mla_k_norm_g=mla_k_norm_g, dn_conv_w=dn_conv_w, dn_a_log=dn_a_log,
             dn_dt_bias=dn_dt_bias, dn_out_norm_g=dn_out_norm_g, dil_q_norm_g=dil_q_norm_g, dil_k_norm_g=dil_k_norm_g,
             w_branch=w_branch, w_out=w_out)
    m = dict(norm_g=m_norm_g, w_in=m_w_in, mla_q_a_norm_g=m_mla_q_a_norm_g, mla_w_q_b=m_mla_w_q_b, mla_kv_a_norm_g=m_mla_kv_a_norm_g,
             mla_w_kv_b=m_mla_w_kv_b, mla_q_norm_g=m_mla_q_norm_g, mla_k_norm_g=m_mla_k_norm_g, dn_conv_w=m_dn_conv_w,
             dn_a_log=m_dn_a_log, dn_dt_bias=m_dn_dt_bias, dn_out_norm_g=m_dn_out_norm_g, dil_q_norm_g=m_dil_q_norm_g,
             dil_k_norm_g=m_dil_k_norm_g, w_branch=m_w_branch, w_out=m_w_out)
    v = dict(norm_g=v_norm_g, w_in=v_w_in, mla_q_a_norm_g=v_mla_q_a_norm_g, mla_w_q_b=v_mla_w_q_b, mla_kv_a_norm_g=v_mla_kv_a_norm_g,
             mla_w_kv_b=v_mla_w_kv_b, mla_q_norm_g=v_mla_q_norm_g, mla_k_norm_g=v_mla_k_norm_g, dn_conv_w=v_dn_conv_w,
             dn_a_log=v_dn_a_log, dn_dt_bias=v_dn_dt_bias, dn_out_norm_g=v_dn_out_norm_g, dil_q_norm_g=v_dil_q_norm_g,
             dil_k_norm_g=v_dil_k_norm_g, w_branch=v_w_branch, w_out=v_w_out)
    chip = 2 * lax.axis_index("x") + lax.axis_index("y")
    mat_names = [n for n, _ in MATS]
    mat_shapes = tuple(w[n].shape for n in mat_names)
    conv_shard = dn_conv_w.shape
    win_shape = w_in.shape
    win_rows = (win_shape[0] * win_shape[1], win_shape[2])

    w_in4, mats4 = chip_all_gather([w_in.astype(BF16), _pack([w[n] for n in mat_names], BF16, MAT_ROWS)])
    conv8 = all_gather8(_pack([dn_conv_w], F32, 8), "gather_conv_w")
    conv_full = jnp.concatenate([_unpack(conv8[2 * j], (conv_shard,))[0] for j in range(4)], axis=2)
    small = {n: w[n] for n in SMALL}
    tabs = _tables(positions[0])

    def loss_fn(w_in4, mats4, conv_full, small, xs):
        return _local_loss(w_in4, _full_from_chips(mats4, mat_shapes), conv_full, small, xs, loss_target[0], tabs)

    loss, (g_win4, g_mats4, g_conv, g_small, g_x) = jax.value_and_grad(loss_fn, argnums=(0, 1, 2, 3, 4))(
        w_in4, mats4, conv_full, small, x[0])
    loss = lax.psum(loss, ("x", "y", "c"))

    r_win4, r_mats4 = chip_all_to_all([g_win4, g_mats4])
    part_win = sum_blocks(r_win4.reshape((4,) + win_rows), WIN_ROWS, "sum_chip_pieces_w_in")
    part_mats = sum_blocks(r_mats4, MAT_ROWS, "sum_chip_pieces_mats")
    other_win, other_mats = sibling_swap([part_win, part_mats])
    res_win = adamw([part_win, other_win], *[d['w_in'].reshape(win_rows) for d in (w, m, v)], WIN_ROWS, "adamw_w_in")
    packed = [_pack([d[n] for n in mat_names], F32, MAT_ROWS) for d in (w, m, v)]
    res_big = [dict(zip(mat_names, _unpack(r, mat_shapes))) for r in adamw([part_mats, other_mats], *packed, MAT_ROWS, "adamw_mats")]
    for k in range(4):
        res_big[k]['w_in'] = res_win[k].reshape(win_shape)

    small_shapes = tuple(w[n].shape for n in SMALL) + (g_conv.shape,)
    g_all = sum_blocks(all_gather8(_pack([g_small[n] for n in SMALL] + [g_conv], F32, 8), "gather_small_grads"), 8, "sum_small")
    g_list = list(_unpack(g_all, small_shapes))
    g_list[-1] = lax.dynamic_slice_in_dim(g_list[-1], chip * conv_shard[2], conv_shard[2], axis=2)
    small_names = list(SMALL) + ['dn_conv_w']
    packed_s = [_pack([d[n] for n in small_names], F32, 8) for d in (w, m, v)]
    shapes_s = tuple(w[n].shape for n in small_names)
    res_small = [dict(zip(small_names, _unpack(r, shapes_s))) for r in adamw([_pack(g_list, F32, 8)], *packed_s, 8, "adamw_small")]

    outs = [loss, g_x[None]]
    for k in range(4):
        outs += [res_big[k][n] if n in res_big[k] else res_small[k][n] for n in WEIGHTS]
    return tuple(outs)
```

```python
import functools
import math

import jax
import jax.numpy as jnp
from jax import lax
from jax.experimental import pallas as pl
from jax.experimental.pallas import tpu as pltpu

F32 = jnp.float32
BF16 = jnp.bfloat16
HI = lax.Precision.HIGHEST
MESH = pl.DeviceIdType.MESH

LANE = 128
VMEM_LIMIT = 48 * 1024 * 1024
ROW_BLOCK = 256
MAT_ROWS = 1024
WIN_ROWS = 128

RMS_EPS = 1e-6
ROPE_THETA = 10000.0
D_MODEL = 1024
DEPTH = 2
MLA_HEADS = 4
MLA_NOPE = 128
MLA_ROPE = 64
MLA_QK = MLA_NOPE + MLA_ROPE
MLA_Q_RANK = 384
MLA_KV_RANK = 256
DN_HEADS = 4
DN_DIM = 128
DN_CONV = 4
GDN_CHUNK = 128
DIL_WINDOWS = (128, 512, 2048)
DIL_DILATIONS = (1, 4, 16)
DIL_GROUPS = 3
DIL_BLOCK = 128
HEAD = 128
BRANCH_W = 512

ADAM_LR = 0.001
ADAM_B1 = 0.9
ADAM_B2 = 0.999
ADAM_EPS = 1e-08
ADAM_WD = 0.01
ADAM_STEP = 10

WEIGHTS = ['norm_g', 'w_in', 'mla_q_a_norm_g', 'mla_w_q_b', 'mla_kv_a_norm_g', 'mla_w_kv_b', 'mla_q_norm_g',
           'mla_k_norm_g', 'dn_conv_w', 'dn_a_log', 'dn_dt_bias', 'dn_out_norm_g', 'dil_q_norm_g', 'dil_k_norm_g',
           'w_branch', 'w_out']
MATS = (('mla_w_q_b', 2), ('mla_w_kv_b', 2), ('w_branch', 3), ('w_out', 1))
SMALL = ('norm_g', 'mla_q_a_norm_g', 'mla_kv_a_norm_g', 'mla_q_norm_g', 'mla_k_norm_g', 'dn_a_log', 'dn_dt_bias',
         'dn_out_norm_g', 'dil_q_norm_g', 'dil_k_norm_g')


def _cparams(*sem):
    return pltpu.CompilerParams(dimension_semantics=sem or None, vmem_limit_bytes=VMEM_LIMIT)


def _tile(dim, target):
    best = 0
    for t in range(LANE, min(dim, target) + 1, LANE):
        if dim % t == 0:
            best = t
    assert best, (dim, target)
    return best


def _mm(a, b, mode, out_dtype, res=None):
    if mode == 'nn':
        (m, k), (k2, n) = a.shape, b.shape
    elif mode == 'nt':
        (m, k), (n, k2) = a.shape, b.shape
    else:
        (k, m), (k2, n) = a.shape, b.shape
    assert k == k2, (a.shape, b.shape, mode)
    tm, tn, tk = _tile(m, 512), _tile(n, 1024), _tile(k, 1024)
    nk = k // tk
    dims = {'nn': (((1,), (0,)), ((), ())), 'nt': (((1,), (1,)), ((), ())), 'tn': (((0,), (0,)), ((), ()))}[mode]

    def body(*refs):
        if res is None:
            a_ref, b_ref, o_ref, acc = refs
        else:
            a_ref, b_ref, r_ref, o_ref, acc = refs
        kk = pl.program_id(2)

        @pl.when(kk == 0)
        def _():
            acc[...] = jnp.zeros_like(acc)

        acc[...] += lax.dot_general(a_ref[...].astype(BF16), b_ref[...].astype(BF16), dims,
                                    preferred_element_type=F32)

        @pl.when(kk == nk - 1)
        def _():
            r = acc[...]
            if res is not None:
                r = r + r_ref[...].astype(F32)
            o_ref[...] = r.astype(o_ref.dtype)

    a_spec = pl.BlockSpec((tk, tm), lambda i, j, kk: (kk, i)) if mode == 'tn' else pl.BlockSpec((tm, tk), lambda i, j, kk: (i, kk))
    b_spec = pl.BlockSpec((tn, tk), lambda i, j, kk: (j, kk)) if mode == 'nt' else pl.BlockSpec((tk, tn), lambda i, j, kk: (kk, j))
    o_spec = pl.BlockSpec((tm, tn), lambda i, j, kk: (i, j))
    in_specs = [a_spec, b_spec] + ([o_spec] if res is not None else [])
    args = (a, b) + ((res,) if res is not None else ())
    return pl.pallas_call(
        body, out_shape=jax.ShapeDtypeStruct((m, n), out_dtype), grid=(m // tm, n // tn, nk),
        in_specs=in_specs, out_specs=o_spec, scratch_shapes=[pltpu.VMEM((tm, tn), F32)],
        compiler_params=_cparams("parallel", "parallel", "arbitrary"),
        name=f"mm_{mode}_{m}x{k}x{n}" + ("_res" if res is not None else ""))(*args)


def _make_matmul(out_dtype, with_res):
    @jax.custom_vjp
    def mm(a, b, *r):
        return _mm(a, b, 'nn', out_dtype, *r)

    def fwd(a, b, *r):
        return mm(a, b, *r), (a, b)

    def bwd(saved, g):
        a, b = saved
        da = _mm(g, b, 'nt', a.dtype)
        db = _mm(a, g, 'tn', b.dtype)
        return (da, db) + ((g,) if with_res else ())

    mm.defvjp(fwd, bwd)
    return mm


def matmul(a, b, out_dtype=F32, res=None):
    if res is None:
        return _make_matmul(out_dtype, False)(a, b)
    return _make_matmul(out_dtype, True)(a, b, res)


def _chunks(ref):
    return [ref[:, c * LANE:(c + 1) * LANE].astype(F32) for c in range(ref.shape[-1] // LANE)]


def _store(ref, chunks):
    for c, ch in enumerate(chunks):
        ref[:, c * LANE:(c + 1) * LANE] = ch.astype(ref.dtype)


def _row_spec(width, br):
    return pl.BlockSpec((br, width), lambda i: (i, 0))


def _par_spec(width):
    return pl.BlockSpec((1, width), lambda i: (0, 0))


def _rw_fwd(f, name, rows, params, outs, br):
    s = rows[0].shape[0]
    br = min(br, s)
    nr, npar = len(rows), len(params)

    def body(*refs):
        rc = [_chunks(r) for r in refs[:nr]]
        pc = [_chunks(p) for p in refs[nr:nr + npar]]
        res = f(rc, pc)
        for o_ref, chs in zip(refs[nr + npar:], res):
            _store(o_ref, chs)

    return pl.pallas_call(
        body, out_shape=[jax.ShapeDtypeStruct((s, w), dt) for w, dt in outs], grid=(s // br,),
        in_specs=[_row_spec(r.shape[1], br) for r in rows] + [_par_spec(p.shape[1]) for p in params],
        out_specs=[_row_spec(w, br) for w, _ in outs],
        compiler_params=_cparams("parallel"), name=name + "_fwd")(*rows, *params)


def _rw_bwd(f, name, rows, params, cts, nograd, br):
    s = rows[0].shape[0]
    br = min(br, s)
    nr, npar, nct = len(rows), len(params), len(cts)
    grad_rows = [i for i in range(nr) if i not in nograd]

    def body(*refs):
        rc = [_chunks(r) for r in refs[:nr]]
        pc = [_chunks(p) for p in refs[nr:nr + npar]]
        ct = [_chunks(c) for c in refs[nr + npar:nr + npar + nct]]
        out_refs = refs[nr + npar + nct:]
        _, vjp = jax.vjp(f, rc, pc)
        drc, dpc = vjp(ct)
        for o_ref, i in zip(out_refs[:len(grad_rows)], grad_rows):
            _store(o_ref, drc[i])
        i0 = pl.program_id(0)
        for o_ref, chs in zip(out_refs[len(grad_rows):], dpc):
            @pl.when(i0 == 0)
            def _(o_ref=o_ref):
                o_ref[...] = jnp.zeros_like(o_ref)
            for c, ch in enumerate(chs):
                o_ref[:, c * LANE:(c + 1) * LANE] += ch

    out_shape = ([jax.ShapeDtypeStruct(rows[i].shape, rows[i].dtype) for i in grad_rows]
                 + [jax.ShapeDtypeStruct(p.shape, F32) for p in params])
    out_specs = [_row_spec(rows[i].shape[1], br) for i in grad_rows] + [_par_spec(p.shape[1]) for p in params]
    res = pl.pallas_call(
        body, out_shape=out_shape, grid=(s // br,),
        in_specs=([_row_spec(r.shape[1], br) for r in rows] + [_par_spec(p.shape[1]) for p in params]
                  + [_row_spec(c.shape[1], br) for c in cts]),
        out_specs=out_specs, compiler_params=_cparams("arbitrary"), name=name + "_bwd")(*rows, *params, *cts)
    drows = [None] * nr
    for o, i in zip(res[:len(grad_rows)], grad_rows):
        drows[i] = o
    for i in nograd:
        drows[i] = jnp.zeros_like(rows[i])
    return tuple(drows), tuple(res[len(grad_rows):])


def rowwise(f, name, rows, params, outs, nograd=(), br=ROW_BLOCK):
    @jax.custom_vjp
    def op(rows, params):
        return tuple(_rw_fwd(f, name, rows, params, outs, br))

    def fwd(rows, params):
        return op(rows, params), (rows, params)

    def bwd(saved, cts):
        rows, params = saved
        return _rw_bwd(f, name, rows, params, list(cts), nograd, br)

    op.defvjp(fwd, bwd)
    return op(tuple(rows), tuple(params))


def _lane_roll(x, s):
    w = x.shape[-1]

    @jax.custom_vjp
    def r(v):
        return pltpu.roll(v, s, 1)

    r.defvjp(lambda v: (r(v), None), lambda _, g: (pltpu.roll(g, (w - s) % w, 1),))
    return r(x)


def _lanes(x):
    return lax.broadcasted_iota(jnp.int32, x.shape, 1)


def _rms(x, g, n=LANE):
    return x * lax.rsqrt(jnp.sum(x * x, axis=-1, keepdims=True) / n + RMS_EPS) * g


def _rope128(x, cos, sin):
    lane = _lanes(x)
    rot = _lane_roll(x, 64) * jnp.where(lane < 64, -1.0, 1.0)
    return x * cos + rot * sin


def _rope64(x, cos, sin):
    lane = _lanes(x)
    rot = jnp.where(lane < 32, -_lane_roll(x, 96), jnp.where(lane < 64, _lane_roll(x, 32), 0.0))
    return x * cos + rot * sin


def _silu(z):
    return z * jax.nn.sigmoid(z)


def _softplus(x):
    return jnp.maximum(x, 0.0) + jnp.log1p(jnp.exp(-jnp.abs(x)))


def f_rms_full(rc, pc):
    x, g = rc[0], pc[0]
    width = len(x) * LANE
    ms = sum(jnp.sum(c * c, axis=-1, keepdims=True) for c in x) / width
    r = lax.rsqrt(ms + RMS_EPS)
    return [[c * r * gc for c, gc in zip(x, g)]]


def f_mla1(rc, pc):
    x = rc[0]
    qn = f_rms_full([x[0:3]], [pc[0]])[0]
    kvn = f_rms_full([x[3:5]], [pc[1]])[0]
    return [qn, kvn, [x[5]]]


def f_mla2(rc, pc):
    q, kv, kpe, cos, sin = rc[0], rc[1], rc[2][0], rc[3][0], rc[4][0]
    gqn, gqp, gkn, gkp = pc[0][0], pc[1][0], pc[2][0], pc[3][0]
    k_pe = _rope64(_rms(kpe, gkp, MLA_ROPE), cos, sin)
    q_att, k_att, v = [], [], []
    for h in range(MLA_HEADS):
        q_att += [_rms(q[h], gqn), _rope64(_rms(q[MLA_HEADS + h], gqp, MLA_ROPE), cos, sin)]
        k_att += [_rms(kv[2 * h], gkn), k_pe]
        v.append(kv[2 * h + 1])
    return [q_att, k_att, v]


def f_gates(rc, pc):
    x, a_log, dt_bias = rc[0][0], pc[0][0], pc[1][0]
    lane = _lanes(x)
    g = -jnp.exp(a_log) * _softplus(x + dt_bias)
    return [[jnp.where(lane < DN_HEADS, g, jnp.where(lane < 2 * DN_HEADS, jax.nn.sigmoid(x), 0.0))]]


def f_headnorm(rc, pc):
    return [[_rms(c, pc[0][0]) for c in rc[0]]]


def f_dil(rc, pc):
    x, cos, sin = rc[0], rc[1][0], rc[2][0]
    gq, gk = pc[0][0], pc[1][0]
    n = len(x) // 3
    q = [_rope128(_rms(c, gq), cos, sin) for c in x[:n]]
    k = [_rope128(_rms(c, gk), cos, sin) for c in x[n:2 * n]]
    return [q, k, list(x[2 * n:])]


def f_comb(rc, pc):
    o, l = rc[:DIL_GROUPS], rc[DIL_GROUPS:]
    out = []
    for c in range(len(o[0])):
        m = functools.reduce(jnp.maximum, [lg[c] for lg in l])
        e = [jnp.exp(lg[c] - m) for lg in l]
        den = sum(e)
        out.append(sum(eg * og[c] for eg, og in zip(e, o)) / den)
    return [out]


def f_merge1(rc, pc):
    z = rc[3]
    n = len(rc[0])
    return [[y * _silu(z[b * n + c]) for c, y in enumerate(rc[b])] for b in range(3)]


def f_merge2(rc, pc):
    gate = rc[0]
    n = len(rc[1])
    return [[sum(jax.nn.sigmoid(gate[b * n + c]) * rc[1 + b][c] for b in range(3)) for c in range(n)]]


MLA_DQK = 2 * LANE
MLA_SCALE = MLA_QK ** -0.5


def _mla_attn_fwd(q, k, v):
    s = q.shape[0]
    h = q.shape[1] // MLA_DQK
    t = min(512, s)
    n = s // t

    def body(q_ref, k_ref, v_ref, o_ref, lse_ref, m_sc, l_sc, acc_sc):
        qi, kj = pl.program_id(1), pl.program_id(2)

        @pl.when(kj == 0)
        def _():
            m_sc[...] = jnp.full_like(m_sc, -jnp.inf)
            l_sc[...] = jnp.zeros_like(l_sc)
            acc_sc[...] = jnp.zeros_like(acc_sc)

        nsub = 2 if t % 256 == 0 else 1
        ts = t // nsub
        rows = [slice(r * ts, (r + 1) * ts) for r in range(nsub)]

        def step(on_diagonal):
            sc = [lax.dot_general(q_ref[rw, :], k_ref[...], (((1,), (1,)), ((), ())), preferred_element_type=F32) * MLA_SCALE
                  for rw in rows]
            if on_diagonal:
                keep = [(lax.broadcasted_iota(jnp.int32, (ts, t), 1)
                         <= r * ts + lax.broadcasted_iota(jnp.int32, (ts, t), 0)) for r in range(nsub)]
                sc = [jnp.where(kp, x, -jnp.inf) for kp, x in zip(keep, sc)]
            m_old = [m_sc[rw, :] for rw in rows]
            m_new = [jnp.maximum(mo, jnp.max(x, axis=-1, keepdims=True)) for mo, x in zip(m_old, sc)]
            alpha = [jnp.exp(mo - mn) for mo, mn in zip(m_old, m_new)]
            p = [jnp.exp(x - mn) for x, mn in zip(sc, m_new)]
            pv = [jnp.dot(x.astype(BF16), v_ref[...], preferred_element_type=F32) for x in p]
            for r, rw in enumerate(rows):
                l_sc[rw, :] = alpha[r] * l_sc[rw, :] + jnp.sum(p[r], axis=-1, keepdims=True)
                acc_sc[rw, :] = alpha[r] * acc_sc[rw, :] + pv[r]
                m_sc[rw, :] = m_new[r]

        @pl.when(kj < qi)
        def _():
            step(False)

        @pl.when(kj == qi)
        def _():
            step(True)

        @pl.when(kj == n - 1)
        def _():
            o_ref[...] = acc_sc[...] / l_sc[...]
            lse_ref[...] = jnp.broadcast_to(m_sc[...] + jnp.log(l_sc[...]), lse_ref.shape)

    return pl.pallas_call(
        body, out_shape=[jax.ShapeDtypeStruct((s, h * HEAD), F32)] * 2, grid=(h, n, n),
        in_specs=[pl.BlockSpec((t, MLA_DQK), lambda hh, i, j: (i, hh)),
                  pl.BlockSpec((t, MLA_DQK), lambda hh, i, j: (jnp.minimum(j, i), hh)),
                  pl.BlockSpec((t, HEAD), lambda hh, i, j: (jnp.minimum(j, i), hh))],
        out_specs=[pl.BlockSpec((t, HEAD), lambda hh, i, j: (i, hh))] * 2,
        scratch_shapes=[pltpu.VMEM((t, 1), F32), pltpu.VMEM((t, 1), F32), pltpu.VMEM((t, HEAD), F32)],
        compiler_params=_cparams("parallel", "parallel", "arbitrary"), name="mla_attn_fwd")(q, k, v)


def _mla_attn_bwd(q, k, v, o, lse, do):
    s = q.shape[0]
    h = q.shape[1] // MLA_DQK
    t = min(512, s)
    n = s // t
    nt, tn = (((1,), (1,)), ((), ())), (((0,), (0,)), ((), ()))

    def body(q_ref, k_ref, v_ref, o_ref, lse_ref, do_ref, dq_ref, dk_ref, dv_ref, dq_sc, dk_sc, dv_sc):
        kj, qi = pl.program_id(1), pl.program_id(2)

        @pl.when(jnp.logical_and(kj == 0, qi == 0))
        def _():
            dq_sc[...] = jnp.zeros_like(dq_sc)

        @pl.when(qi == 0)
        def _():
            dk_sc[...] = jnp.zeros_like(dk_sc)
            dv_sc[...] = jnp.zeros_like(dv_sc)

        def pair(on_diagonal):
            sc = lax.dot_general(q_ref[...], k_ref[...], nt, preferred_element_type=F32) * MLA_SCALE
            p = jnp.exp(sc - lse_ref[:, 0:1])
            if on_diagonal:
                p = jnp.where(lax.broadcasted_iota(jnp.int32, (t, t), 1) <= lax.broadcasted_iota(jnp.int32, (t, t), 0), p, 0.0)
            do_v = do_ref[...]
            do_b = do_v.astype(BF16)
            dp = lax.dot_general(do_b, v_ref[...], nt, preferred_element_type=F32)
            delta = jnp.sum(do_v * o_ref[...], axis=-1, keepdims=True)
            ds = (p * (dp - delta) * MLA_SCALE).astype(BF16)
            dv_sc[...] += lax.dot_general(p.astype(BF16), do_b, tn, preferred_element_type=F32)
            dk_sc[...] += lax.dot_general(ds, q_ref[...], tn, preferred_element_type=F32)
            rows = pl.ds(pl.multiple_of(qi * t, t), t)
            dq_sc[rows, :] += jnp.dot(ds, k_ref[...], preferred_element_type=F32)

        @pl.when(qi > kj)
        def _():
            pair(False)

        @pl.when(qi == kj)
        def _():
            pair(True)

        @pl.when(qi == n - 1)
        def _():
            dk_ref[...] = dk_sc[...].astype(dk_ref.dtype)
            dv_ref[...] = dv_sc[...].astype(dv_ref.dtype)

        @pl.when(jnp.logical_and(kj == n - 1, qi == n - 1))
        def _():
            dq_ref[...] = dq_sc[...].astype(dq_ref.dtype)

    qmap = lambda hh, j, i: (jnp.maximum(i, j), hh)
    kmap = lambda hh, j, i: (j, hh)
    return pl.pallas_call(
        body, out_shape=[jax.ShapeDtypeStruct(q.shape, BF16), jax.ShapeDtypeStruct(k.shape, BF16), jax.ShapeDtypeStruct(v.shape, BF16)],
        grid=(h, n, n),
        in_specs=[pl.BlockSpec((t, MLA_DQK), qmap), pl.BlockSpec((t, MLA_DQK), kmap), pl.BlockSpec((t, HEAD), kmap),
                  pl.BlockSpec((t, HEAD), qmap), pl.BlockSpec((t, HEAD), qmap), pl.BlockSpec((t, HEAD), qmap)],
        out_specs=[pl.BlockSpec((s, MLA_DQK), lambda hh, j, i: (0, hh)), pl.BlockSpec((t, MLA_DQK), kmap),
                   pl.BlockSpec((t, HEAD), kmap)],
        scratch_shapes=[pltpu.VMEM((s, MLA_DQK), F32), pltpu.VMEM((t, MLA_DQK), F32), pltpu.VMEM((t, HEAD), F32)],
        compiler_params=_cparams("parallel", "arbitrary", "arbitrary"), name="mla_attn_bwd")(q, k, v, o, lse, do)


@jax.custom_vjp
def mla_attention(q, k, v):
    return _mla_attn_fwd(q, k, v)[0]


def _mla_attention_fwd(q, k, v):
    o, lse = _mla_attn_fwd(q, k, v)
    return o, (q, k, v, o, lse)


def _mla_attention_bwd(saved, do):
    return tuple(_mla_attn_bwd(*saved, do))


mla_attention.defvjp(_mla_attention_fwd, _mla_attention_bwd)


DIL_SCALE = HEAD ** -0.5
GROUP_W = 4 * HEAD


def _dil_scores(q, kp, kc, n):
    dn = (((1,), (1,)), ((), ()))
    sp = lax.dot_general(q, kp, dn, preferred_element_type=F32) * DIL_SCALE
    sc = lax.dot_general(q, kc, dn, preferred_element_type=F32) * DIL_SCALE
    qi = lax.broadcasted_iota(jnp.int32, sp.shape, 0)
    kc_i = lax.broadcasted_iota(jnp.int32, sp.shape, 1)
    vp = jnp.logical_and(kc_i >= qi, n > 0)
    vc = kc_i <= qi
    return sp, sc, vp, vc


def _dil_specs(d):
    cur = pl.BlockSpec((DIL_BLOCK, GROUP_W), lambda r, n: (n, r))
    prev = pl.BlockSpec((DIL_BLOCK, GROUP_W), lambda r, n: (jnp.maximum(n - 1, 0), r))
    return cur, prev


def _dil_fwd(q, k, v, d):
    l = q.shape[0]
    nb = l // DIL_BLOCK
    cur, prev = _dil_specs(d)

    def body(q_ref, kp_ref, kc_ref, vp_ref, vc_ref, o_ref, lse_ref):
        n = pl.program_id(1)
        heads = range(4)
        sl = [slice(h * HEAD, (h + 1) * HEAD) for h in heads]
        scores = [_dil_scores(q_ref[:, sl[h]], kp_ref[:, sl[h]], kc_ref[:, sl[h]], n) for h in heads]
        sp = [jnp.where(vp, s_p, -jnp.inf) for s_p, _, vp, _ in scores]
        sc = [jnp.where(vc, s_c, -jnp.inf) for _, s_c, _, vc in scores]
        m = [jnp.maximum(jnp.max(sp[h], axis=-1, keepdims=True), jnp.max(sc[h], axis=-1, keepdims=True)) for h in heads]
        ep = [jnp.exp(sp[h] - m[h]) for h in heads]
        ec = [jnp.exp(sc[h] - m[h]) for h in heads]
        den = [jnp.sum(ep[h], axis=-1, keepdims=True) + jnp.sum(ec[h], axis=-1, keepdims=True) for h in heads]
        acc = [jnp.dot(ep[h].astype(BF16), vp_ref[:, sl[h]], preferred_element_type=F32)
               + jnp.dot(ec[h].astype(BF16), vc_ref[:, sl[h]], preferred_element_type=F32) for h in heads]
        for h in heads:
            o_ref[:, sl[h]] = acc[h] / den[h]
            lse_ref[:, sl[h]] = jnp.broadcast_to(m[h] + jnp.log(den[h]), (DIL_BLOCK, HEAD))

    return pl.pallas_call(
        body, out_shape=[jax.ShapeDtypeStruct(q.shape, F32)] * 2, grid=(d, nb),
        in_specs=[cur, prev, cur, prev, cur], out_specs=[cur, cur],
        compiler_params=_cparams("parallel", "parallel"), name=f"dil_fwd_d{d}")(q, k, k, v, v)


def _dil_bwd(q, k, v, o, lse, do, dlse, d):
    l = q.shape[0]
    nb = l // DIL_BLOCK
    cur, prev = _dil_specs(d)
    tn = (((0,), (0,)), ((), ()))
    nt = (((1,), (1,)), ((), ()))

    def body(q_ref, kp_ref, kc_ref, vp_ref, vc_ref, o_ref, lse_ref, do_ref, dl_ref,
             dq_ref, dkp_ref, dkc_ref, dvp_ref, dvc_ref):
        n = pl.program_id(1)
        heads = range(4)
        sl = [slice(h * HEAD, (h + 1) * HEAD) for h in heads]
        scores = [_dil_scores(q_ref[:, sl[h]], kp_ref[:, sl[h]], kc_ref[:, sl[h]], n) for h in heads]
        lse = [lse_ref[:, h * HEAD:h * HEAD + 1] for h in heads]
        pp = [jnp.where(scores[h][2], jnp.exp(scores[h][0] - lse[h]), 0.0) for h in heads]
        pc = [jnp.where(scores[h][3], jnp.exp(scores[h][1] - lse[h]), 0.0) for h in heads]
        do_b = [do_ref[:, sl[h]].astype(BF16) for h in heads]
        corr = [jnp.sum(dl_ref[:, sl[h]], axis=-1, keepdims=True)
                - jnp.sum(do_ref[:, sl[h]] * o_ref[:, sl[h]], axis=-1, keepdims=True) for h in heads]
        dsp = [(pp[h] * (lax.dot_general(do_b[h], vp_ref[:, sl[h]], nt, preferred_element_type=F32) + corr[h])
                * DIL_SCALE).astype(BF16) for h in heads]
        dsc = [(pc[h] * (lax.dot_general(do_b[h], vc_ref[:, sl[h]], nt, preferred_element_type=F32) + corr[h])
                * DIL_SCALE).astype(BF16) for h in heads]
        for h in heads:
            dq_ref[:, sl[h]] = (jnp.dot(dsp[h], kp_ref[:, sl[h]], preferred_element_type=F32)
                                + jnp.dot(dsc[h], kc_ref[:, sl[h]], preferred_element_type=F32)).astype(dq_ref.dtype)
            dkp_ref[:, sl[h]] = lax.dot_general(dsp[h], q_ref[:, sl[h]], tn, preferred_element_type=F32)
            dkc_ref[:, sl[h]] = lax.dot_general(dsc[h], q_ref[:, sl[h]], tn, preferred_element_type=F32)
            dvp_ref[:, sl[h]] = lax.dot_general(pp[h].astype(BF16), do_b[h], tn, preferred_element_type=F32)
            dvc_ref[:, sl[h]] = lax.dot_general(pc[h].astype(BF16), do_b[h], tn, preferred_element_type=F32)

    dq, dkp, dkc, dvp, dvc = pl.pallas_call(
        body, out_shape=[jax.ShapeDtypeStruct(q.shape, BF16)] + [jax.ShapeDtypeStruct(q.shape, F32)] * 4, grid=(d, nb),
        in_specs=[cur, prev, cur, prev, cur, cur, cur, cur, cur], out_specs=[cur] * 5,
        compiler_params=_cparams("parallel", "parallel"), name=f"dil_bwd_d{d}")(q, k, k, v, v, o, lse, do, dlse)

    def fold(cur_part, prev_part):
        shifted = jnp.concatenate([prev_part[DIL_BLOCK:], jnp.zeros((DIL_BLOCK, prev_part.shape[1]), F32)], axis=0)
        return (cur_part + shifted).astype(BF16)

    return dq, fold(dkc, dkp), fold(dvc, dvp)


def _make_dil(d):
    @jax.custom_vjp
    def att(q, k, v):
        return tuple(_dil_fwd(q, k, v, d))

    def fwd(q, k, v):
        o, lse = _dil_fwd(q, k, v, d)
        return (o, lse), (q, k, v, o, lse)

    def bwd(saved, cts):
        return _dil_bwd(*saved, cts[0], cts[1], d)

    att.defvjp(fwd, bwd)
    return att


def dilated_group(q, k, v, d):
    s = q.shape[0]
    view = lambda t: t.reshape(s // d, d * GROUP_W)
    o, lse = _make_dil(d)(view(q), view(k), view(v))
    return o.reshape(s, GROUP_W), lse.reshape(s, GROUP_W)


def _dn_post(c, kind, scale):
    m = _silu(c)
    nrm = m * lax.rsqrt(jnp.sum(m * m, axis=-1, keepdims=True) + 1e-6) * scale
    return kind * nrm + (1.0 - kind) * m


def _dn_kind_scale(j):
    kind = jnp.where(j < 2 * DN_HEADS, 1.0, 0.0).astype(F32)
    scale = jnp.where(j < DN_HEADS, DN_DIM ** -0.5, 1.0).astype(F32)
    return kind, scale


_CONV_RB = 512
_CONV_PAD = 8


def _conv_windows(pad_ref, w_ref, r0, rb, sign):
    acc = None
    for sh in range(DN_CONV):
        win = pad_ref[pl.ds(r0 + _CONV_PAD * (sign < 0) + sign * sh, rb), :]
        term = w_ref[DN_CONV - 1 - sh:DN_CONV - sh, :] * win
        acc = term if acc is None else acc + term
    return acc


def _dn_conv_fwd(x, w):
    s, width = x.shape
    rb = min(_CONV_RB, s)

    def body(x_ref, w_ref, o_ref, pad_ref):
        kind, scale = _dn_kind_scale(pl.program_id(0))
        pad_ref[0:_CONV_PAD, :] = jnp.zeros((_CONV_PAD, LANE), F32)
        pad_ref[_CONV_PAD:, :] = x_ref[...]
        for r0 in range(0, s, rb):
            c = _conv_windows(pad_ref, w_ref, r0, rb, -1)
            o_ref[r0:r0 + rb, :] = _dn_post(c, kind, scale)

    return pl.pallas_call(
        body, out_shape=jax.ShapeDtypeStruct(x.shape, F32), grid=(width // LANE,),
        in_specs=[pl.BlockSpec((s, LANE), lambda j: (0, j)), pl.BlockSpec((DN_CONV, LANE), lambda j: (0, j))],
        out_specs=pl.BlockSpec((s, LANE), lambda j: (0, j)), scratch_shapes=[pltpu.VMEM((s + _CONV_PAD, LANE), F32)],
        compiler_params=_cparams("parallel"), name="dn_conv_fwd")(x, w)


def _dn_conv_bwd(x, w, dy):
    s, width = x.shape
    rb = min(_CONV_RB, s)

    def body(x_ref, w_ref, dy_ref, dx_ref, dw_ref, pad_ref, dpad_ref):
        kind, scale = _dn_kind_scale(pl.program_id(0))
        pad_ref[0:_CONV_PAD, :] = jnp.zeros((_CONV_PAD, LANE), F32)
        pad_ref[_CONV_PAD:, :] = x_ref[...]
        dpad_ref[s:, :] = jnp.zeros((_CONV_PAD, LANE), F32)
        dws = [jnp.zeros((1, LANE), F32) for _ in range(DN_CONV)]
        for r0 in range(0, s, rb):
            c = _conv_windows(pad_ref, w_ref, r0, rb, -1)
            _, vjp = jax.vjp(lambda cc: _dn_post(cc, kind, scale), c)
            dc = vjp(dy_ref[r0:r0 + rb, :])[0]
            dpad_ref[r0:r0 + rb, :] = dc
            for sh in range(DN_CONV):
                win = pad_ref[pl.ds(r0 + _CONV_PAD - sh, rb), :]
                dws[DN_CONV - 1 - sh] = dws[DN_CONV - 1 - sh] + jnp.sum(dc * win, axis=0, keepdims=True)
        for j in range(DN_CONV):
            dw_ref[j:j + 1, :] = dws[j]
        for r0 in range(0, s, rb):
            dx_ref[r0:r0 + rb, :] = _conv_windows(dpad_ref, w_ref, r0, rb, 1)

    return pl.pallas_call(
        body, out_shape=[jax.ShapeDtypeStruct(x.shape, F32), jax.ShapeDtypeStruct(w.shape, F32)], grid=(width // LANE,),
        in_specs=[pl.BlockSpec((s, LANE), lambda j: (0, j)), pl.BlockSpec((DN_CONV, LANE), lambda j: (0, j)),
                  pl.BlockSpec((s, LANE), lambda j: (0, j))],
        out_specs=[pl.BlockSpec((s, LANE), lambda j: (0, j)), pl.BlockSpec((DN_CONV, LANE), lambda j: (0, j))],
        scratch_shapes=[pltpu.VMEM((s + _CONV_PAD, LANE), F32), pltpu.VMEM((s + _CONV_PAD, LANE), F32)],
        compiler_params=_cparams("parallel"), name="dn_conv_bwd")(x, w, dy)


@jax.custom_vjp
def dn_conv(x, w):
    return _dn_conv_fwd(x, w)


dn_conv.defvjp(lambda x, w: (_dn_conv_fwd(x, w), (x, w)), lambda saved, dy: tuple(_dn_conv_bwd(*saved, dy)))


_NN = (((1,), (0,)), ((), ()))
_NT = (((1,), (1,)), ((), ()))
_TN = (((0,), (0,)), ((), ()))


def _bd(a, b, dims, passes=1):
    d = lambda x, y: lax.dot_general(x, y, dims, preferred_element_type=F32)
    if passes == 0:
        return lax.dot_general(a, b, dims, precision=lax.Precision.HIGHEST, preferred_element_type=F32)
    ah, bh = a.astype(BF16), b.astype(BF16)
    if passes == 1:
        return d(ah, bh)
    al, bl = (a - ah.astype(F32)).astype(BF16), (b - bh.astype(F32)).astype(BF16)
    return d(ah, bh) + d(ah, bl) + d(al, bh)


@functools.partial(jax.custom_vjp, nondiff_argnums=(2, 3))
def _pdot(a, b, dims, passes):
    return _bd(a, b, dims, passes)


def _pdot_bwd(dims, passes, saved, g):
    a, b = saved
    if dims == _NN:
        return _bd(g, b, _NT, passes), _bd(a, g, _TN, passes)
    if dims == _NT:
        return _bd(g, b, _NN, passes), _bd(g, a, _TN, passes)
    return _bd(b, g, _NT, passes), _bd(a, g, _NN, passes)


_pdot.defvjp(lambda a, b, dims, passes: (_bd(a, b, dims, passes), (a, b)), _pdot_bwd)


GDN_DOT_PASSES = 1
GDN_SOLVE_PASSES = 3


def _hdot(a, b, dims=_NN):
    return _pdot(a, b, dims, GDN_DOT_PASSES)


def _xdot(a, b, dims=_NN):
    return _pdot(a, b, dims, GDN_SOLVE_PASSES)


def _split3(x):
    hi = x.astype(BF16)
    r1 = x - hi.astype(F32)
    mid = r1.astype(BF16)
    lo = (r1 - mid.astype(F32)).astype(BF16)
    return hi, mid, lo


def _tri_dot(tri, x, dims):
    t = tri.astype(BF16)
    return sum(lax.dot_general(t, p, dims, preferred_element_type=F32) for p in _split3(x))


@jax.custom_vjp
def _cumsum_rows(x):
    c = x.shape[0]
    tri = lax.broadcasted_iota(jnp.int32, (c, c), 0) >= lax.broadcasted_iota(jnp.int32, (c, c), 1)
    return _tri_dot(tri, x, _NN)


def _cumsum_rows_bwd(_, g):
    c = g.shape[0]
    tri = lax.broadcasted_iota(jnp.int32, (c, c), 0) >= lax.broadcasted_iota(jnp.int32, (c, c), 1)
    return (_tri_dot(tri, g, _TN),)


_cumsum_rows.defvjp(lambda x: (_cumsum_rows(x), None), _cumsum_rows_bwd)


def _gdn_prep_fn(qkv, gb):
    c = GDN_CHUNK
    row = lax.broadcasted_iota(jnp.int32, (c, c), 0)
    col = lax.broadcasted_iota(jnp.int32, (c, c), 1)
    incl, strict = row >= col, row > col
    lane = _lanes(gb)
    heads = range(DN_HEADS)
    q, k, v = qkv[:DN_HEADS], qkv[DN_HEADS:2 * DN_HEADS], qkv[2 * DN_HEADS:]
    g = [jnp.sum(jnp.where(lane == h, gb, 0.0), axis=-1, keepdims=True) for h in heads]
    beta = [jnp.sum(jnp.where(lane == DN_HEADS + h, gb, 0.0), axis=-1, keepdims=True) for h in heads]
    gcb = [_cumsum_rows(jnp.broadcast_to(g[h], (c, c))) for h in heads]
    decay = [jnp.where(incl, jnp.exp(jnp.where(incl, gcb[h] - gcb[h].T, 0.0)), 0.0) for h in heads]
    kb = [k[h] * beta[h] for h in heads]
    a = [jnp.where(strict, _hdot(kb[h], k[h], _NT) * decay[h], 0.0) for h in heads]
    pw = [-a[h] for h in heads]
    t = [(row == col).astype(F32) + pw[h] for h in heads]
    for _ in range(int(math.log2(c)) - 1):
        pw = [_xdot(pw[h], pw[h]) for h in heads]
        t = [t[h] + _xdot(t[h], pw[h]) for h in heads]
    eg = [jnp.exp(gcb[h]) for h in heads]
    u = [_xdot(t[h], v[h] * beta[h]) for h in heads]
    w = [_xdot(t[h], kb[h] * eg[h]) for h in heads]
    qk = [jnp.where(incl, _hdot(q[h], k[h], _NT) * decay[h], 0.0) for h in heads]
    g_last = [jnp.sum(jnp.where(row == c - 1, gcb[h], 0.0), axis=0, keepdims=True) for h in heads]
    kg = [k[h] * jnp.exp(g_last[h] - gcb[h]) for h in heads]
    qg = [q[h] * eg[h] for h in heads]
    el = [jnp.broadcast_to(jnp.exp(g_last[h]), (8, LANE)) for h in heads]
    return [u, w, qg, kg, qk, el]


def _gdn_prep_specs(nq):
    c = GDN_CHUNK
    big = pl.BlockSpec((c, nq * LANE), lambda n: (n, 0))
    return big


def _gdn_prep_fwd(qkv, gb):
    s = qkv.shape[0]
    c = GDN_CHUNK
    n = s // c
    hw = DN_HEADS * LANE

    def body(qkv_ref, gb_ref, u_ref, w_ref, qg_ref, kg_ref, qk_ref, el_ref):
        res = _gdn_prep_fn(_chunks(qkv_ref), gb_ref[...])
        for ref, chs in zip((u_ref, w_ref, qg_ref, kg_ref, qk_ref, el_ref), res):
            _store(ref, chs)

    row = pl.BlockSpec((c, hw), lambda i: (i, 0))
    return pl.pallas_call(
        body, out_shape=[jax.ShapeDtypeStruct((s, hw), F32)] * 5 + [jax.ShapeDtypeStruct((n * 8, hw), F32)], grid=(n,),
        in_specs=[pl.BlockSpec((c, 3 * hw), lambda i: (i, 0)), pl.BlockSpec((c, LANE), lambda i: (i, 0))],
        out_specs=[row] * 5 + [pl.BlockSpec((8, hw), lambda i: (i, 0))],
        compiler_params=_cparams("parallel"), name="gdn_prep_fwd")(qkv, gb)


def _gdn_prep_bwd(qkv, gb, cts):
    s = qkv.shape[0]
    c = GDN_CHUNK
    n = s // c
    hw = DN_HEADS * LANE

    def body(qkv_ref, gb_ref, du_ref, dw_ref, dqg_ref, dkg_ref, dqk_ref, del_ref, dqkv_ref, dgb_ref):
        _, vjp = jax.vjp(_gdn_prep_fn, _chunks(qkv_ref), gb_ref[...])
        ct = [_chunks(r) for r in (du_ref, dw_ref, dqg_ref, dkg_ref, dqk_ref, del_ref)]
        dqkv, dgb = vjp(ct)
        _store(dqkv_ref, dqkv)
        dgb_ref[...] = dgb

    row = pl.BlockSpec((c, hw), lambda i: (i, 0))
    return pl.pallas_call(
        body, out_shape=[jax.ShapeDtypeStruct(qkv.shape, F32), jax.ShapeDtypeStruct(gb.shape, F32)], grid=(n,),
        in_specs=[pl.BlockSpec((c, 3 * hw), lambda i: (i, 0)), pl.BlockSpec((c, LANE), lambda i: (i, 0))]
        + [row] * 5 + [pl.BlockSpec((8, hw), lambda i: (i, 0))],
        out_specs=[pl.BlockSpec((c, 3 * hw), lambda i: (i, 0)), pl.BlockSpec((c, LANE), lambda i: (i, 0))],
        compiler_params=_cparams("parallel"), name="gdn_prep_bwd")(qkv, gb, *cts)


@jax.custom_vjp
def gdn_prep(qkv, gb):
    return tuple(_gdn_prep_fwd(qkv, gb))


gdn_prep.defvjp(lambda qkv, gb: (tuple(_gdn_prep_fwd(qkv, gb)), (qkv, gb)),
                lambda saved, cts: tuple(_gdn_prep_bwd(*saved, cts)))


def _gdn_scan_fwd(u, w, qg, kg, qk, el):
    s = u.shape[0]
    c = GDN_CHUNK
    n = s // c
    hw = DN_HEADS * LANE

    def body(u_ref, w_ref, qg_ref, kg_ref, qk_ref, el_ref, o_ref, st_ref, s_sc):
        @pl.when(pl.program_id(0) == 0)
        def _():
            s_sc[...] = jnp.zeros_like(s_sc)

        heads = range(DN_HEADS)
        sl = [slice(h * LANE, (h + 1) * LANE) for h in heads]
        st = [s_sc[h] for h in heads]
        for h in heads:
            st_ref[sl[h], :] = st[h]
        v_new = [u_ref[:, sl[h]] - _hdot(w_ref[:, sl[h]], st[h]) for h in heads]
        o_st = [_hdot(qg_ref[:, sl[h]], st[h]) for h in heads]
        o_in = [_hdot(qk_ref[:, sl[h]], v_new[h]) for h in heads]
        s_up = [_hdot(kg_ref[:, sl[h]], v_new[h], _TN) for h in heads]
        for h in heads:
            o_ref[:, sl[h]] = o_st[h] + o_in[h]
            s_sc[h] = st[h] * el_ref[0:1, sl[h]] + s_up[h]

    row = pl.BlockSpec((c, hw), lambda i: (i, 0))
    return pl.pallas_call(
        body, out_shape=[jax.ShapeDtypeStruct((s, hw), F32), jax.ShapeDtypeStruct((n, hw, LANE), F32)], grid=(n,),
        in_specs=[row] * 5 + [pl.BlockSpec((8, hw), lambda i: (i, 0))],
        out_specs=[row, pl.BlockSpec((None, hw, LANE), lambda i: (i, 0, 0))],
        scratch_shapes=[pltpu.VMEM((DN_HEADS, LANE, LANE), F32)],
        compiler_params=_cparams("arbitrary"), name="gdn_scan_fwd")(u, w, qg, kg, qk, el)


def _gdn_scan_bwd(u, w, qg, kg, qk, el, states, do):
    s = u.shape[0]
    c = GDN_CHUNK
    n = s // c
    hw = DN_HEADS * LANE

    def body(u_ref, w_ref, qg_ref, kg_ref, qk_ref, el_ref, st_ref, do_ref,
             du_ref, dw_ref, dqg_ref, dkg_ref, dqk_ref, del_ref, ds_sc):
        @pl.when(pl.program_id(0) == 0)
        def _():
            ds_sc[...] = jnp.zeros_like(ds_sc)

        heads = range(DN_HEADS)
        sl = [slice(h * LANE, (h + 1) * LANE) for h in heads]
        st = [st_ref[sl[h], :] for h in heads]
        ds = [ds_sc[h] for h in heads]
        do = [do_ref[:, sl[h]] for h in heads]
        v_new = [u_ref[:, sl[h]] - _hdot(w_ref[:, sl[h]], st[h]) for h in heads]
        dv_new = [_hdot(qk_ref[:, sl[h]], do[h], _TN) + _hdot(kg_ref[:, sl[h]], ds[h]) for h in heads]
        first_row = lax.broadcasted_iota(jnp.int32, (8, LANE), 0) == 0
        for h in heads:
            du_ref[:, sl[h]] = dv_new[h]
            dw_ref[:, sl[h]] = -_hdot(dv_new[h], st[h], _NT)
            dqg_ref[:, sl[h]] = _hdot(do[h], st[h], _NT)
            dqk_ref[:, sl[h]] = _hdot(do[h], v_new[h], _NT)
            dkg_ref[:, sl[h]] = _hdot(v_new[h], ds[h], _NT)
            del_ref[:, sl[h]] = jnp.where(first_row, jnp.sum(st[h] * ds[h], axis=0, keepdims=True), 0.0)
        ds_new = [_hdot(qg_ref[:, sl[h]], do[h], _TN) + ds[h] * el_ref[0:1, sl[h]] - _hdot(w_ref[:, sl[h]], dv_new[h], _TN)
                  for h in heads]
        for h in heads:
            ds_sc[h] = ds_new[h]

    row = pl.BlockSpec((c, hw), lambda i: (n - 1 - i, 0))
    small = pl.BlockSpec((8, hw), lambda i: (n - 1 - i, 0))
    return pl.pallas_call(
        body, out_shape=[jax.ShapeDtypeStruct((s, hw), F32)] * 5 + [jax.ShapeDtypeStruct((n * 8, hw), F32)], grid=(n,),
        in_specs=[row] * 5 + [small, pl.BlockSpec((None, hw, LANE), lambda i: (n - 1 - i, 0, 0)), row],
        out_specs=[row] * 5 + [small], scratch_shapes=[pltpu.VMEM((DN_HEADS, LANE, LANE), F32)],
        compiler_params=_cparams("arbitrary"), name="gdn_scan_bwd")(u, w, qg, kg, qk, el, states, do)


@jax.custom_vjp
def gdn_scan(u, w, qg, kg, qk, el):
    return _gdn_scan_fwd(u, w, qg, kg, qk, el)[0]


def _gdn_scan_vfwd(*args):
    o, states = _gdn_scan_fwd(*args)
    return o, args + (states,)


gdn_scan.defvjp(_gdn_scan_vfwd, lambda saved, do: tuple(_gdn_scan_bwd(*saved, do)))


def _loss_call(y, t):
    s, d = y.shape
    br = min(ROW_BLOCK, s)
    n = s // br

    def body(y_ref, t_ref, loss_ref, dy_ref, acc):
        i = pl.program_id(0)

        @pl.when(i == 0)
        def _():
            acc[...] = jnp.zeros_like(acc)

        e = y_ref[...] - t_ref[...]
        dy_ref[...] = e / d
        acc[...] += jnp.sum(e * e, axis=0, keepdims=True)

        @pl.when(i == n - 1)
        def _():
            loss_ref[...] = jnp.broadcast_to(jnp.sum(acc[...], axis=1, keepdims=True) * (0.5 / d), loss_ref.shape)

    row = pl.BlockSpec((br, d), lambda i: (i, 0))
    return pl.pallas_call(
        body, out_shape=[jax.ShapeDtypeStruct((1, LANE), F32), jax.ShapeDtypeStruct((s, d), F32)], grid=(n,),
        in_specs=[row, row], out_specs=[pl.BlockSpec((1, LANE), lambda i: (0, 0)), row],
        scratch_shapes=[pltpu.VMEM((1, d), F32)], compiler_params=_cparams("arbitrary"), name="loss_head")(y, t)


@jax.custom_vjp
def loss_head(y, t):
    return _loss_call(y, t)[0][0, 0]


def _loss_head_fwd(y, t):
    loss, dy = _loss_call(y, t)
    return loss[0, 0], (dy,)


loss_head.defvjp(_loss_head_fwd, lambda saved, g: (saved[0] * g, -saved[0] * g))


IN_OFF = {}
_o = 0
for _name, _size in (('q_lat', 384), ('kv_lat', 320), ('z_a', 512), ('dn_qkv', 1536), ('dn_ab', 8), ('z_b', 512),
                     ('dil_qkv', 4608), ('z_c', 512), ('gate', 3072)):
    IN_OFF[_name] = (_o, _o + _size)
    _o += _size
IN_WIDTH = _o
N_CHIPS = 4


def _cols(w, name):
    a, b = IN_OFF[name]
    return w[:, a:b]


def _pad_cols(w, to):
    return jnp.concatenate([w, jnp.zeros((w.shape[0], to - w.shape[1]), w.dtype)], axis=1)


def _pad_row(v, to=None):
    v = v.reshape(1, -1)
    return v if to is None or v.shape[1] == to else _pad_cols(v, to)


def _shard_cols(pieces, a, b):
    wsh = pieces[0].shape[1]
    parts = [pieces[j][:, max(a, j * wsh) - j * wsh:min(b, (j + 1) * wsh) - j * wsh]
             for j in range(len(pieces)) if max(a, j * wsh) < min(b, (j + 1) * wsh)]
    return parts[0] if len(parts) == 1 else jnp.concatenate(parts, axis=1)


def _win_groups_impl(w_in4):
    out = []
    for l in range(w_in4.shape[1]):
        pieces = [w_in4[j, l] for j in range(w_in4.shape[0])]
        cols = lambda name: _shard_cols(pieces, *IN_OFF[name])
        out.append((_pad_cols(jnp.concatenate([cols('q_lat'), cols('kv_lat')], axis=1), 768),
                    jnp.concatenate([cols('z_a'), cols('z_b'), cols('z_c')], axis=1),
                    cols('dn_qkv'), _pad_cols(cols('dn_ab'), LANE), cols('dil_qkv'), cols('gate')))
    return tuple(out)


@jax.custom_vjp
def win_groups(w_in4):
    return _win_groups_impl(w_in4)


def _win_groups_bwd(_, cts):
    n_chip, depth = N_CHIPS, len(cts)
    wsh = IN_WIDTH // n_chip
    per_layer = []
    for l in range(depth):
        dmla, dz, dgdn, dab, ddil, dgate = cts[l]
        full = jnp.concatenate([dmla[:, :704], dz[:, :512], dgdn, dab[:, :8], dz[:, 512:1024], ddil, dz[:, 1024:], dgate], axis=1)
        per_layer.append([full[:, j * wsh:(j + 1) * wsh] for j in range(n_chip)])
    return (jnp.stack([jnp.stack([per_layer[l][j] for l in range(depth)]) for j in range(n_chip)]),)


win_groups.defvjp(lambda w: (_win_groups_impl(w), None), _win_groups_bwd)


def _layer(x, p, tabs):
    w_mla, w_z, w_gdn, w_ab, w_dil, w_gate = p['w_in_groups']
    h, = rowwise(f_rms_full, "rms_in", [x], [_pad_row(p['norm_g'])], [(D_MODEL, BF16)])
    mla_in = matmul(h, w_mla)
    z = matmul(h, w_z)
    dn_qkv = matmul(h, w_gdn)
    dn_ab = matmul(h, w_ab)
    dil_qkv = matmul(h, w_dil)
    gate = matmul(h, w_gate)

    qn, kvn, kpe = rowwise(f_mla1, "mla_norm", [mla_in], [_pad_row(p['mla_q_a_norm_g']), _pad_row(p['mla_kv_a_norm_g'])],
                           [(MLA_Q_RANK, BF16), (MLA_KV_RANK, BF16), (LANE, F32)])
    wq = p['mla_w_q_b']
    wq_pad = jnp.concatenate(
        [wq[:, hh * MLA_QK:hh * MLA_QK + MLA_NOPE] for hh in range(MLA_HEADS)]
        + [_pad_cols(wq[:, hh * MLA_QK + MLA_NOPE:(hh + 1) * MLA_QK], LANE) for hh in range(MLA_HEADS)], axis=1)
    q = matmul(qn, wq_pad)
    kv = matmul(kvn, p['mla_w_kv_b'])
    gq, gk = p['mla_q_norm_g'], p['mla_k_norm_g']
    q_att, k_att, v_att = rowwise(
        f_mla2, "mla_qk", [q, kv, kpe, tabs['cos_r'], tabs['sin_r']],
        [_pad_row(gq[:MLA_NOPE]), _pad_row(gq[MLA_NOPE:], LANE), _pad_row(gk[:MLA_NOPE]), _pad_row(gk[MLA_NOPE:], LANE)],
        [(MLA_HEADS * MLA_DQK, BF16), (MLA_HEADS * MLA_DQK, BF16), (MLA_HEADS * HEAD, BF16)], nograd=(3, 4))
    y_a = mla_attention(q_att, k_att, v_att)

    qkv_n = dn_conv(dn_qkv, p['dn_conv_w'])
    gb, = rowwise(f_gates, "dn_gates", [dn_ab], [_pad_row(p['dn_a_log'], LANE), _pad_row(p['dn_dt_bias'], LANE)], [(LANE, F32)])
    o_b = gdn_scan(*gdn_prep(qkv_n, gb))
    y_b, = rowwise(f_headnorm, "dn_out_norm", [o_b], [_pad_row(p['dn_out_norm_g'])], [(DN_HEADS * DN_DIM, F32)])

    qd, kd, vd = rowwise(f_dil, "dil_qk", [dil_qkv, tabs['cos_h'], tabs['sin_h']],
                         [_pad_row(p['dil_q_norm_g']), _pad_row(p['dil_k_norm_g'])],
                         [(3 * GROUP_W, BF16)] * 3, nograd=(1, 2))
    outs, lses = [], []
    for gi, d in enumerate(DIL_DILATIONS):
        sl = slice(gi * GROUP_W, (gi + 1) * GROUP_W)
        o_g, l_g = dilated_group(qd[:, sl], kd[:, sl], vd[:, sl], d)
        outs.append(o_g)
        lses.append(l_g)
    y_c, = rowwise(f_comb, "dil_comb", outs + lses, [], [(GROUP_W, F32)])

    ys = rowwise(f_merge1, "merge_silu", [y_a, y_b, y_c, z], [], [(BRANCH_W, BF16)] * 3)
    bo = [matmul(ys[b], p['w_branch'][b]) for b in range(3)]
    mixed, = rowwise(f_merge2, "merge_gate", [gate] + bo, [], [(D_MODEL, BF16)])
    return matmul(mixed, p['w_out'], res=x)


def _rope_tables(pos, dim):
    inv_freq = 1.0 / (ROPE_THETA ** (jnp.arange(0, dim, 2, dtype=F32) / dim))
    ang = pos.astype(F32)[:, None] * inv_freq
    return jnp.cos(ang), jnp.sin(ang)


def _tables(pos):
    cr, sr = _rope_tables(pos, MLA_ROPE)
    ch, sh = _rope_tables(pos, HEAD)
    zero = jnp.zeros((pos.shape[0], LANE - MLA_ROPE), F32)
    return {'cos_r': jnp.concatenate([cr, cr, zero], axis=1), 'sin_r': jnp.concatenate([sr, sr, zero], axis=1),
            'cos_h': jnp.concatenate([ch, ch], axis=1), 'sin_h': jnp.concatenate([sh, sh], axis=1)}


def _local_loss(w_in4, mats, conv_w, small, x, target, tabs):
    groups = win_groups(w_in4)
    for l in range(DEPTH):
        p = {k: v[l] for k, v in mats.items()}
        p.update({k: v[l] for k, v in small.items()})
        p['dn_conv_w'] = conv_w[l]
        p['w_in_groups'] = groups[l]
        x = _layer(x, p, tabs)
    return loss_head(x, target)


def _pack(arrays, dtype, row_tile):
    flat = jnp.concatenate([a.astype(dtype).reshape(-1) for a in arrays])
    rows = -(-flat.shape[0] // (LANE * row_tile)) * row_tile
    flat = jnp.concatenate([flat, jnp.zeros((rows * LANE - flat.shape[0],), dtype)])
    return flat.reshape(rows, LANE)


def _unpack_impl(buf, shapes):
    flat = buf.reshape(-1)
    out, off = [], 0
    for shp in shapes:
        n = math.prod(shp)
        out.append(flat[off:off + n].reshape(shp))
        off += n
    return tuple(out)


@functools.partial(jax.custom_vjp, nondiff_argnums=(1, 2, 3))
def _unpack_p(buf, shapes, dtype_name, rows):
    return _unpack_impl(buf, shapes)


_unpack_p.defvjp(lambda buf, shapes, dtype_name, rows: (_unpack_impl(buf, shapes), None),
                 lambda shapes, dtype_name, rows, _, cts: (_pack(cts, jnp.dtype(dtype_name), rows),))


def _unpack(buf, shapes):
    return _unpack_p(buf, tuple(shapes), jnp.dtype(buf.dtype).name, buf.shape[0])


def _full_from_chips(buf4, shard_shapes):
    per_chip = [_unpack(buf4[j], tuple(shard_shapes)) for j in range(4)]
    return {name: jnp.concatenate([per_chip[j][i] for j in range(4)], axis=axis) for i, (name, axis) in enumerate(MATS)}


_HBM = pl.BlockSpec(memory_space=pltpu.HBM)
_VMEM = pl.BlockSpec(memory_space=pltpu.VMEM)


def _chip_peers(x, y):
    return [(1 - x, y), (x, 1 - y), (1 - x, 1 - y)]


def _chip_exchange(arrays, gather, name):
    n = len(arrays)

    def body(*refs):
        in_refs, out_refs = refs[:n], refs[n:2 * n]
        send_sems, recv_sems, local_sems = refs[2 * n:]
        x, y, c = lax.axis_index("x"), lax.axis_index("y"), lax.axis_index("c")
        me = 2 * x + y
        peers = _chip_peers(x, y)
        local, sends = [], []
        for a, (i_ref, o_ref) in enumerate(zip(in_refs, out_refs)):
            local.append(pltpu.make_async_copy(i_ref if gather else i_ref.at[me], o_ref.at[me], local_sems.at[a]))
            local[-1].start()
            for k, (px, py) in enumerate(peers):
                sends.append(pltpu.make_async_remote_copy(
                    src_ref=i_ref if gather else i_ref.at[2 * px + py], dst_ref=o_ref.at[me], send_sem=send_sems.at[3 * a + k],
                    recv_sem=recv_sems.at[3 * a + k], device_id=(px, py, c), device_id_type=MESH))
                sends[-1].start()
        for a, (i_ref, o_ref) in enumerate(zip(in_refs, out_refs)):
            for k, (px, py) in enumerate(peers):
                pltpu.make_async_remote_copy(
                    src_ref=i_ref if gather else i_ref.at[me], dst_ref=o_ref.at[2 * px + py], send_sem=send_sems.at[3 * a + k],
                    recv_sem=recv_sems.at[3 * a + k], device_id=(px, py, c), device_id_type=MESH).wait_recv()
        for cp in sends:
            cp.wait_send()
        for cp in local:
            cp.wait()

    out_shape = [jax.ShapeDtypeStruct(((4,) + a.shape) if gather else a.shape, a.dtype) for a in arrays]
    return pl.pallas_call(
        body, out_shape=out_shape, in_specs=[_HBM] * n, out_specs=[_HBM] * n,
        scratch_shapes=[pltpu.SemaphoreType.DMA((3 * n,)), pltpu.SemaphoreType.DMA((3 * n,)), pltpu.SemaphoreType.DMA((n,))],
        name=name)(*arrays)


def chip_all_gather(shards):
    return _chip_exchange(shards, True, "chip_all_gather")


def chip_all_to_all(pieces):
    return _chip_exchange(pieces, False, "chip_all_to_all")


def sibling_swap(arrays):
    n = len(arrays)

    def body(*refs):
        in_refs, out_refs = refs[:n], refs[n:2 * n]
        send_sems, recv_sems = refs[2 * n:]
        x, y, c = lax.axis_index("x"), lax.axis_index("y"), lax.axis_index("c")
        cps = [pltpu.make_async_remote_copy(src_ref=i_ref, dst_ref=o_ref, send_sem=send_sems.at[a], recv_sem=recv_sems.at[a],
                                            device_id=(x, y, 1 - c), device_id_type=MESH)
               for a, (i_ref, o_ref) in enumerate(zip(in_refs, out_refs))]
        for cp in cps:
            cp.start()
        for cp in cps:
            cp.wait()

    return pl.pallas_call(
        body, out_shape=[jax.ShapeDtypeStruct(a.shape, a.dtype) for a in arrays], in_specs=[_HBM] * n, out_specs=[_HBM] * n,
        scratch_shapes=[pltpu.SemaphoreType.DMA((n,)), pltpu.SemaphoreType.DMA((n,))], name="sibling_swap")(*arrays)


def all_gather8(v, name):
    def body(v_ref, out_ref, send_sems, recv_sems):
        x, y, c = lax.axis_index("x"), lax.axis_index("y"), lax.axis_index("c")
        out_ref[4 * x + 2 * y + c] = v_ref[...]

        def peer(k):
            return (x ^ (k >> 2), y ^ ((k >> 1) & 1), c ^ (k & 1))

        sends = [pltpu.make_async_remote_copy(src_ref=v_ref, dst_ref=out_ref.at[4 * x + 2 * y + c], send_sem=send_sems.at[k - 1],
                                              recv_sem=recv_sems.at[k - 1], device_id=peer(k), device_id_type=MESH)
                 for k in range(1, 8)]
        for cp in sends:
            cp.start()
        for k in range(1, 8):
            px, py, pc = peer(k)
            pltpu.make_async_remote_copy(src_ref=v_ref, dst_ref=out_ref.at[4 * px + 2 * py + pc], send_sem=send_sems.at[k - 1],
                                         recv_sem=recv_sems.at[k - 1], device_id=peer(k), device_id_type=MESH).wait_recv()
        for cp in sends:
            cp.wait_send()

    return pl.pallas_call(
        body, out_shape=jax.ShapeDtypeStruct((8,) + v.shape, v.dtype), in_specs=[_VMEM], out_specs=_VMEM,
        scratch_shapes=[pltpu.SemaphoreType.DMA((7,)), pltpu.SemaphoreType.DMA((7,))], name=name)(v)


def sum_blocks(blocks, row_tile, name):
    n, rows, width = blocks.shape

    def body(b_ref, o_ref):
        acc = b_ref[0].astype(F32)
        for j in range(1, n):
            acc = acc + b_ref[j].astype(F32)
        o_ref[...] = acc

    return pl.pallas_call(
        body, out_shape=jax.ShapeDtypeStruct((rows, width), F32), grid=(rows // row_tile,),
        in_specs=[pl.BlockSpec((n, row_tile, width), lambda i: (0, i, 0))],
        out_specs=pl.BlockSpec((row_tile, width), lambda i: (i, 0)), compiler_params=_cparams("parallel"), name=name)(blocks)


def adamw(g_parts, w, m, v, row_tile, name):
    npart = len(g_parts)

    def body(*refs):
        g = refs[0][...]
        for r in refs[1:npart]:
            g = g + r[...]
        w_ref, m_ref, v_ref, g_out, d_out, m_out, v_out = refs[npart:]
        m_new = ADAM_B1 * m_ref[...] + (1.0 - ADAM_B1) * g
        v_new = ADAM_B2 * v_ref[...] + (1.0 - ADAM_B2) * (g * g)
        m_hat = m_new / (1.0 - ADAM_B1 ** ADAM_STEP)
        v_hat = v_new / (1.0 - ADAM_B2 ** ADAM_STEP)
        g_out[...] = g
        d_out[...] = -ADAM_LR * (m_hat / (jnp.sqrt(v_hat) + ADAM_EPS) + ADAM_WD * w_ref[...])
        m_out[...] = m_new
        v_out[...] = v_new

    rows, width = w.shape
    spec = pl.BlockSpec((row_tile, width), lambda i: (i, 0))
    return pl.pallas_call(
        body, out_shape=[jax.ShapeDtypeStruct(w.shape, F32)] * 4, grid=(rows // row_tile,),
        in_specs=[spec] * (npart + 3), out_specs=[spec] * 4, compiler_params=_cparams("parallel"), name=name)(*g_parts, w, m, v)


def kernel(x, positions, norm_g, w_in, mla_q_a_norm_g, mla_w_q_b, mla_kv_a_norm_g, mla_w_kv_b, mla_q_norm_g, mla_k_norm_g, dn_conv_w, dn_a_log, dn_dt_bias, dn_out_norm_g, dil_q_norm_g, dil_k_norm_g, w_branch, w_out, loss_target, m_norm_g, m_w_in, m_mla_q_a_norm_g, m_mla_w_q_b, m_mla_kv_a_norm_g, m_mla_w_kv_b, m_mla_q_norm_g, m_mla_k_norm_g, m_dn_conv_w, m_dn_a_log, m_dn_dt_bias, m_dn_out_norm_g, m_dil_q_norm_g, m_dil_k_norm_g, m_w_branch, m_w_out, v_norm_g, v_w_in, v_mla_q_a_norm_g, v_mla_w_q_b, v_mla_kv_a_norm_g, v_mla_w_kv_b, v_mla_q_norm_g, v_mla_k_norm_g, v_dn_conv_w, v_dn_a_log, v_dn_dt_bias, v_dn_out_norm_g, v_dil_q_norm_g, v_dil_k_norm_g, v_w_branch, v_w_out):
    w = dict(norm_g=norm_g, w_in=w_in, mla_q_a_norm_g=mla_q_a_norm_g, mla_w_q_b=mla_w_q_b, mla_kv_a_norm_g=mla_kv_a_norm_g,
             mla_w_kv_b=mla_w_kv_b, mla_q_norm_g=mla_q_norm_g, mla_k_norm_g=mla_k_norm_g, dn_conv_w=dn_conv_w, dn_a_log=dn_a_log,
             dn_dt_bias=dn_dt_bias, dn_out_norm_g=dn_out_norm_g, dil_q_norm_g=dil_q_norm_g, dil_k_norm_g=dil_k_norm_g,
             w_branch=w_branch, w_out=w_out)
    m = dict(norm_g=m_norm_g, w_in=m_w_in, mla_q_a_norm_g=m_mla_q_a_norm_g, mla_w_q_b=m_mla_w_q_b, mla_kv_a_norm_g=m_mla_kv_a_norm_g,
             mla_w_kv_b=m_mla_w_kv_b, mla_q_norm_g=m_mla_q_norm_g, mla_k_norm_g=m_mla_k_norm_g, dn_conv_w=m_dn_conv_w,
             dn_a_log=m_dn_a_log, dn_dt_bias=m_dn_dt_bias, dn_out_norm_g=m_dn_out_norm_g, dil_q_norm_g=m_dil_q_norm_g,
             dil_k_norm_g=m_dil_k_norm_g, w_branch=m_w_branch, w_out=m_w_out)
    v = dict(norm_g=v_norm_g, w_in=v_w_in, mla_q_a_norm_g=v_mla_q_a_norm_g, mla_w_q_b=v_mla_w_q_b, mla_kv_a_norm_g=v_mla_kv_a_norm_g,
             mla_w_kv_b=v_mla_w_kv_b, mla_q_norm_g=v_mla_q_norm_g, mla_k_norm_g=v_mla_k_norm_g, dn_conv_w=v_dn_conv_w,
             dn_a_log=v_dn_a_log, dn_dt_bias=v_dn_dt_bias, dn_out_norm_g=v_dn_out_norm_g, dil_q_norm_g=v_dil_q_norm_g,
             dil_k_norm_g=v_dil_k_norm_g, w_branch=v_w_branch, w_out=v_w_out)
    chip = 2 * lax.axis_index("x") + lax.axis_index("y")
    mat_names = [n for n, _ in MATS]
    mat_shapes = tuple(w[n].shape for n in mat_names)
    conv_shard = dn_conv_w.shape
    win_shape = w_in.shape
    win_rows = (win_shape[0] * win_shape[1], win_shape[2])

    w_in4, mats4 = chip_all_gather([w_in.astype(BF16), _pack([w[n] for n in mat_names], BF16, MAT_ROWS)])
    conv8 = all_gather8(_pack([dn_conv_w], F32, 8), "gather_conv_w")
    conv_full = jnp.concatenate([_unpack(conv8[2 * j], (conv_shard,))[0] for j in range(4)], axis=2)
    small = {n: w[n] for n in SMALL}
    tabs = _tables(positions[0])

    def loss_fn(w_in4, mats4, conv_full, small, xs):
        return _local_loss(w_in4, _full_from_chips(mats4, mat_shapes), conv_full, small, xs, loss_target[0], tabs)

    loss, (g_win4, g_mats4, g_conv, g_small, g_x) = jax.value_and_grad(loss_fn, argnums=(0, 1, 2, 3, 4))(
        w_in4, mats4, conv_full, small, x[0])
    loss = lax.psum(loss, ("x", "y", "c"))

    r_win4, r_mats4 = chip_all_to_all([g_win4, g_mats4])
    part_win = sum_blocks(r_win4.reshape((4,) + win_rows), WIN_ROWS, "sum_chip_pieces_w_in")
    part_mats = sum_blocks(r_mats4, MAT_ROWS, "sum_chip_pieces_mats")
    other_win, other_mats = sibling_swap([part_win, part_mats])
    res_win = adamw([part_win, other_win], *[d['w_in'].reshape(win_rows) for d in (w, m, v)], WIN_ROWS, "adamw_w_in")
    packed = [_pack([d[n] for n in mat_names], F32, MAT_ROWS) for d in (w, m, v)]
    res_big = [dict(zip(mat_names, _unpack(r, mat_shapes))) for r in adamw([part_mats, other_mats], *packed, MAT_ROWS, "adamw_mats")]
    for k in range(4):
        res_big[k]['w_in'] = res_win[k].reshape(win_shape)

    small_shapes = tuple(w[n].shape for n in SMALL) + (g_conv.shape,)
    g_all = sum_blocks(all_gather8(_pack([g_small[n] for n in SMALL] + [g_conv], F32, 8), "gather_small_grads"), 8, "sum_small")
    g_list = list(_unpack(g_all, small_shapes))
    g_list[-1] = lax.dynamic_slice_in_dim(g_list[-1], chip * conv_shard[2], conv_shard[2], axis=2)
    small_names = list(SMALL) + ['dn_conv_w']
    packed_s = [_pack([d[n] for n in small_names], F32, 8) for d in (w, m, v)]
    shapes_s = tuple(w[n].shape for n in small_names)
    res_small = [dict(zip(small_names, _unpack(r, shapes_s))) for r in adamw([_pack(g_list, F32, 8)], *packed_s, 8, "adamw_small")]

    outs = [loss, g_x[None]]
    for k in range(4):
        outs += [res_big[k][n] if n in res_big[k] else res_small[k][n] for n in WEIGHTS]
    return tuple(outs)
```

```python
import functools
import math

import jax
import jax.numpy as jnp
from jax import lax
from jax.experimental import pallas as pl
from jax.experimental.pallas import tpu as pltpu

F32 = jnp.float32
BF16 = jnp.bfloat16
HI = lax.Precision.HIGHEST
MESH = pl.DeviceIdType.MESH

LANE = 128
VMEM_LIMIT = 48 * 1024 * 1024
ROW_BLOCK = 256
MAT_ROWS = 512
WIN_ROWS = 128

RMS_EPS = 1e-6
ROPE_THETA = 10000.0
D_MODEL = 1024
DEPTH = 2
MLA_HEADS = 4
MLA_NOPE = 128
MLA_ROPE = 64
MLA_QK = MLA_NOPE + MLA_ROPE
MLA_Q_RANK = 384
MLA_KV_RANK = 256
DN_HEADS = 4
DN_DIM = 128
DN_CONV = 4
GDN_CHUNK = 128
DIL_WINDOWS = (128, 512, 2048)
DIL_DILATIONS = (1, 4, 16)
DIL_GROUPS = 3
DIL_BLOCK = 128
HEAD = 128
BRANCH_W = 512

ADAM_LR = 0.001
ADAM_B1 = 0.9
ADAM_B2 = 0.999
ADAM_EPS = 1e-08
ADAM_WD = 0.01
ADAM_STEP = 10

WEIGHTS = ['norm_g', 'w_in', 'mla_q_a_norm_g', 'mla_w_q_b', 'mla_kv_a_norm_g', 'mla_w_kv_b', 'mla_q_norm_g',
           'mla_k_norm_g', 'dn_conv_w', 'dn_a_log', 'dn_dt_bias', 'dn_out_norm_g', 'dil_q_norm_g', 'dil_k_norm_g',
           'w_branch', 'w_out']
MATS = (('mla_w_q_b', 2), ('mla_w_kv_b', 2), ('w_branch', 3), ('w_out', 1))
SMALL = ('norm_g', 'mla_q_a_norm_g', 'mla_kv_a_norm_g', 'mla_q_norm_g', 'mla_k_norm_g', 'dn_a_log', 'dn_dt_bias',
         'dn_out_norm_g', 'dil_q_norm_g', 'dil_k_norm_g')


def _cparams(*sem):
    return pltpu.CompilerParams(dimension_semantics=sem or None, vmem_limit_bytes=VMEM_LIMIT)


def _tile(dim, target):
    best = 0
    for t in range(LANE, min(dim, target) + 1, LANE):
        if dim % t == 0:
            best = t
    assert best, (dim, target)
    return best


def _mm(a, b, mode, out_dtype, res=None):
    if mode == 'nn':
        (m, k), (k2, n) = a.shape, b.shape
    elif mode == 'nt':
        (m, k), (n, k2) = a.shape, b.shape
    else:
        (k, m), (k2, n) = a.shape, b.shape
    assert k == k2, (a.shape, b.shape, mode)
    tm, tn, tk = _tile(m, 512), _tile(n, 1024), _tile(k, 1024)
    nk = k // tk
    dims = {'nn': (((1,), (0,)), ((), ())), 'nt': (((1,), (1,)), ((), ())), 'tn': (((0,), (0,)), ((), ()))}[mode]

    def body(*refs):
        if res is None:
            a_ref, b_ref, o_ref, acc = refs
        else:
            a_ref, b_ref, r_ref, o_ref, acc = refs
        kk = pl.program_id(2)

        @pl.when(kk == 0)
        def _():
            acc[...] = jnp.zeros_like(acc)

        acc[...] += lax.dot_general(a_ref[...].astype(BF16), b_ref[...].astype(BF16), dims,
                                    preferred_element_type=F32)

        @pl.when(kk == nk - 1)
        def _():
            r = acc[...]
            if res is not None:
                r = r + r_ref[...].astype(F32)
            o_ref[...] = r.astype(o_ref.dtype)

    a_spec = pl.BlockSpec((tk, tm), lambda i, j, kk: (kk, i)) if mode == 'tn' else pl.BlockSpec((tm, tk), lambda i, j, kk: (i, kk))
    b_spec = pl.BlockSpec((tn, tk), lambda i, j, kk: (j, kk)) if mode == 'nt' else pl.BlockSpec((tk, tn), lambda i, j, kk: (kk, j))
    o_spec = pl.BlockSpec((tm, tn), lambda i, j, kk: (i, j))
    in_specs = [a_spec, b_spec] + ([o_spec] if res is not None else [])
    args = (a, b) + ((res,) if res is not None else ())
    return pl.pallas_call(
        body, out_shape=jax.ShapeDtypeStruct((m, n), out_dtype), grid=(m // tm, n // tn, nk),
        in_specs=in_specs, out_specs=o_spec, scratch_shapes=[pltpu.VMEM((tm, tn), F32)],
        compiler_params=_cparams("parallel", "parallel", "arbitrary"),
        name=f"mm_{mode}_{m}x{k}x{n}" + ("_res" if res is not None else ""))(*args)


def _make_matmul(out_dtype, with_res):
    @jax.custom_vjp
    def mm(a, b, *r):
        return _mm(a, b, 'nn', out_dtype, *r)

    def fwd(a, b, *r):
        return mm(a, b, *r), (a, b)

    def bwd(saved, g):
        a, b = saved
        da = _mm(g, b, 'nt', a.dtype)
        db = _mm(a, g, 'tn', b.dtype)
        return (da, db) + ((g,) if with_res else ())

    mm.defvjp(fwd, bwd)
    return mm


def matmul(a, b, out_dtype=F32, res=None):
    if res is None:
        return _make_matmul(out_dtype, False)(a, b)
    return _make_matmul(out_dtype, True)(a, b, res)


def _chunks(ref):
    return [ref[:, c * LANE:(c + 1) * LANE].astype(F32) for c in range(ref.shape[-1] // LANE)]


def _store(ref, chunks):
    for c, ch in enumerate(chunks):
        ref[:, c * LANE:(c + 1) * LANE] = ch.astype(ref.dtype)


def _row_spec(width, br):
    return pl.BlockSpec((br, width), lambda i: (i, 0))


def _par_spec(width):
    return pl.BlockSpec((1, width), lambda i: (0, 0))


def _rw_fwd(f, name, rows, params, outs, br):
    s = rows[0].shape[0]
    br = min(br, s)
    nr, npar = len(rows), len(params)

    def body(*refs):
        rc = [_chunks(r) for r in refs[:nr]]
        pc = [_chunks(p) for p in refs[nr:nr + npar]]
        res = f(rc, pc)
        for o_ref, chs in zip(refs[nr + npar:], res):
            _store(o_ref, chs)

    return pl.pallas_call(
        body, out_shape=[jax.ShapeDtypeStruct((s, w), dt) for w, dt in outs], grid=(s // br,),
        in_specs=[_row_spec(r.shape[1], br) for r in rows] + [_par_spec(p.shape[1]) for p in params],
        out_specs=[_row_spec(w, br) for w, _ in outs],
        compiler_params=_cparams("parallel"), name=name + "_fwd")(*rows, *params)


def _rw_bwd(f, name, rows, params, cts, nograd, br):
    s = rows[0].shape[0]
    br = min(br, s)
    nr, npar, nct = len(rows), len(params), len(cts)
    grad_rows = [i for i in range(nr) if i not in nograd]

    def body(*refs):
        rc = [_chunks(r) for r in refs[:nr]]
        pc = [_chunks(p) for p in refs[nr:nr + npar]]
        ct = [_chunks(c) for c in refs[nr + npar:nr + npar + nct]]
        out_refs = refs[nr + npar + nct:]
        _, vjp = jax.vjp(f, rc, pc)
        drc, dpc = vjp(ct)
        for o_ref, i in zip(out_refs[:len(grad_rows)], grad_rows):
            _store(o_ref, drc[i])
        i0 = pl.program_id(0)
        for o_ref, chs in zip(out_refs[len(grad_rows):], dpc):
            @pl.when(i0 == 0)
            def _(o_ref=o_ref):
                o_ref[...] = jnp.zeros_like(o_ref)
            for c, ch in enumerate(chs):
                o_ref[:, c * LANE:(c + 1) * LANE] += ch

    out_shape = ([jax.ShapeDtypeStruct(rows[i].shape, rows[i].dtype) for i in grad_rows]
                 + [jax.ShapeDtypeStruct(p.shape, F32) for p in params])
    out_specs = [_row_spec(rows[i].shape[1], br) for i in grad_rows] + [_par_spec(p.shape[1]) for p in params]
    res = pl.pallas_call(
        body, out_shape=out_shape, grid=(s // br,),
        in_specs=([_row_spec(r.shape[1], br) for r in rows] + [_par_spec(p.shape[1]) for p in params]
                  + [_row_spec(c.shape[1], br) for c in cts]),
        out_specs=out_specs, compiler_params=_cparams("arbitrary"), name=name + "_bwd")(*rows, *params, *cts)
    drows = [None] * nr
    for o, i in zip(res[:len(grad_rows)], grad_rows):
        drows[i] = o
    for i in nograd:
        drows[i] = jnp.zeros_like(rows[i])
    return tuple(drows), tuple(res[len(grad_rows):])


def rowwise(f, name, rows, params, outs, nograd=(), br=ROW_BLOCK):
    @jax.custom_vjp
    def op(rows, params):
        return tuple(_rw_fwd(f, name, rows, params, outs, br))

    def fwd(rows, params):
        return op(rows, params), (rows, params)

    def bwd(saved, cts):
        rows, params = saved
        return _rw_bwd(f, name, rows, params, list(cts), nograd, br)

    op.defvjp(fwd, bwd)
    return op(tuple(rows), tuple(params))


def _lane_roll(x, s):
    w = x.shape[-1]

    @jax.custom_vjp
    def r(v):
        return pltpu.roll(v, s, 1)

    r.defvjp(lambda v: (r(v), None), lambda _, g: (pltpu.roll(g, (w - s) % w, 1),))
    return r(x)


def _lanes(x):
    return lax.broadcasted_iota(jnp.int32, x.shape, 1)


def _rms(x, g, n=LANE):
    return x * lax.rsqrt(jnp.sum(x * x, axis=-1, keepdims=True) / n + RMS_EPS) * g


def _rope128(x, cos, sin):
    lane = _lanes(x)
    rot = _lane_roll(x, 64) * jnp.where(lane < 64, -1.0, 1.0)
    return x * cos + rot * sin


def _rope64(x, cos, sin):
    lane = _lanes(x)
    rot = jnp.where(lane < 32, -_lane_roll(x, 96), jnp.where(lane < 64, _lane_roll(x, 32), 0.0))
    return x * cos + rot * sin


def _silu(z):
    return z * jax.nn.sigmoid(z)


def _softplus(x):
    return jnp.maximum(x, 0.0) + jnp.log1p(jnp.exp(-jnp.abs(x)))


def f_rms_full(rc, pc):
    x, g = rc[0], pc[0]
    width = len(x) * LANE
    ms = sum(jnp.sum(c * c, axis=-1, keepdims=True) for c in x) / width
    r = lax.rsqrt(ms + RMS_EPS)
    return [[c * r * gc for c, gc in zip(x, g)]]


def f_mla1(rc, pc):
    x = rc[0]
    qn = f_rms_full([x[0:3]], [pc[0]])[0]
    kvn = f_rms_full([x[3:5]], [pc[1]])[0]
    return [qn, kvn, [x[5]]]


def f_mla2(rc, pc):
    q, kv, kpe, cos, sin = rc[0], rc[1], rc[2][0], rc[3][0], rc[4][0]
    gqn, gqp, gkn, gkp = pc[0][0], pc[1][0], pc[2][0], pc[3][0]
    k_pe = _rope64(_rms(kpe, gkp, MLA_ROPE), cos, sin)
    q_att, k_att, v = [], [], []
    for h in range(MLA_HEADS):
        q_att += [_rms(q[h], gqn), _rope64(_rms(q[MLA_HEADS + h], gqp, MLA_ROPE), cos, sin)]
        k_att += [_rms(kv[2 * h], gkn), k_pe]
        v.append(kv[2 * h + 1])
    return [q_att, k_att, v]


def f_gates(rc, pc):
    x, a_log, dt_bias = rc[0][0], pc[0][0], pc[1][0]
    lane = _lanes(x)
    g = -jnp.exp(a_log) * _softplus(x + dt_bias)
    return [[jnp.where(lane < DN_HEADS, g, jnp.where(lane < 2 * DN_HEADS, jax.nn.sigmoid(x), 0.0))]]


def f_headnorm(rc, pc):
    return [[_rms(c, pc[0][0]) for c in rc[0]]]


def f_dil(rc, pc):
    x, cos, sin = rc[0], rc[1][0], rc[2][0]
    gq, gk = pc[0][0], pc[1][0]
    n = len(x) // 3
    q = [_rope128(_rms(c, gq), cos, sin) for c in x[:n]]
    k = [_rope128(_rms(c, gk), cos, sin) for c in x[n:2 * n]]
    return [q, k, list(x[2 * n:])]


def f_comb(rc, pc):
    o, l = rc[:DIL_GROUPS], rc[DIL_GROUPS:]
    out = []
    for c in range(len(o[0])):
        m = functools.reduce(jnp.maximum, [lg[c] for lg in l])
        e = [jnp.exp(lg[c] - m) for lg in l]
        den = sum(e)
        out.append(sum(eg * og[c] for eg, og in zip(e, o)) / den)
    return [out]


def f_merge1(rc, pc):
    z = rc[3]
    n = len(rc[0])
    return [[y * _silu(z[b * n + c]) for c, y in enumerate(rc[b])] for b in range(3)]


def f_merge2(rc, pc):
    gate = rc[0]
    n = len(rc[1])
    return [[sum(jax.nn.sigmoid(gate[b * n + c]) * rc[1 + b][c] for b in range(3)) for c in range(n)]]


MLA_DQK = 2 * LANE
MLA_SCALE = MLA_QK ** -0.5


def _mla_attn_fwd(q, k, v):
    s = q.shape[0]
    h = q.shape[1] // MLA_DQK
    t = min(512, s)
    n = s // t

    def body(q_ref, k_ref, v_ref, o_ref, lse_ref, m_sc, l_sc, acc_sc):
        qi, kj = pl.program_id(1), pl.program_id(2)

        @pl.when(kj == 0)
        def _():
            m_sc[...] = jnp.full_like(m_sc, -jnp.inf)
            l_sc[...] = jnp.zeros_like(l_sc)
            acc_sc[...] = jnp.zeros_like(acc_sc)

        nsub = 2 if t % 256 == 0 else 1
        ts = t // nsub
        rows = [slice(r * ts, (r + 1) * ts) for r in range(nsub)]

        def step(on_diagonal):
            sc = [lax.dot_general(q_ref[rw, :], k_ref[...], (((1,), (1,)), ((), ())), preferred_element_type=F32) * MLA_SCALE
                  for rw in rows]
            if on_diagonal:
                keep = [(lax.broadcasted_iota(jnp.int32, (ts, t), 1)
                         <= r * ts + lax.broadcasted_iota(jnp.int32, (ts, t), 0)) for r in range(nsub)]
                sc = [jnp.where(kp, x, -jnp.inf) for kp, x in zip(keep, sc)]
            m_old = [m_sc[rw, :] for rw in rows]
            m_new = [jnp.maximum(mo, jnp.max(x, axis=-1, keepdims=True)) for mo, x in zip(m_old, sc)]
            alpha = [jnp.exp(mo - mn) for mo, mn in zip(m_old, m_new)]
            p = [jnp.exp(x - mn) for x, mn in zip(sc, m_new)]
            pv = [jnp.dot(x.astype(BF16), v_ref[...], preferred_element_type=F32) for x in p]
            for r, rw in enumerate(rows):
                l_sc[rw, :] = alpha[r] * l_sc[rw, :] + jnp.sum(p[r], axis=-1, keepdims=True)
                acc_sc[rw, :] = alpha[r] * acc_sc[rw, :] + pv[r]
                m_sc[rw, :] = m_new[r]

        @pl.when(kj < qi)
        def _():
            step(False)

        @pl.when(kj == qi)
        def _():
            step(True)

        @pl.when(kj == n - 1)
        def _():
            o_ref[...] = acc_sc[...] / l_sc[...]
            lse_ref[...] = jnp.broadcast_to(m_sc[...] + jnp.log(l_sc[...]), lse_ref.shape)

    return pl.pallas_call(
        body, out_shape=[jax.ShapeDtypeStruct((s, h * HEAD), F32)] * 2, grid=(h, n, n),
        in_specs=[pl.BlockSpec((t, MLA_DQK), lambda hh, i, j: (i, hh)),
                  pl.BlockSpec((t, MLA_DQK), lambda hh, i, j: (jnp.minimum(j, i), hh)),
                  pl.BlockSpec((t, HEAD), lambda hh, i, j: (jnp.minimum(j, i), hh))],
        out_specs=[pl.BlockSpec((t, HEAD), lambda hh, i, j: (i, hh))] * 2,
        scratch_shapes=[pltpu.VMEM((t, 1), F32), pltpu.VMEM((t, 1), F32), pltpu.VMEM((t, HEAD), F32)],
        compiler_params=_cparams("parallel", "parallel", "arbitrary"), name="mla_attn_fwd")(q, k, v)


def _mla_attn_bwd(q, k, v, o, lse, do):
    s = q.shape[0]
    h = q.shape[1] // MLA_DQK
    t = min(512, s)
    n = s // t
    nt, tn = (((1,), (1,)), ((), ())), (((0,), (0,)), ((), ()))

    def body(q_ref, k_ref, v_ref, o_ref, lse_ref, do_ref, dq_ref, dk_ref, dv_ref, dq_sc, dk_sc, dv_sc):
        kj, qi = pl.program_id(1), pl.program_id(2)

        @pl.when(jnp.logical_and(kj == 0, qi == 0))
        def _():
            dq_sc[...] = jnp.zeros_like(dq_sc)

        @pl.when(qi == 0)
        def _():
            dk_sc[...] = jnp.zeros_like(dk_sc)
            dv_sc[...] = jnp.zeros_like(dv_sc)

        def pair(on_diagonal):
            sc = lax.dot_general(q_ref[...], k_ref[...], nt, preferred_element_type=F32) * MLA_SCALE
            p = jnp.exp(sc - lse_ref[:, 0:1])
            if on_diagonal:
                p = jnp.where(lax.broadcasted_iota(jnp.int32, (t, t), 1) <= lax.broadcasted_iota(jnp.int32, (t, t), 0), p, 0.0)
            do_v = do_ref[...]
            do_b = do_v.astype(BF16)
            dp = lax.dot_general(do_b, v_ref[...], nt, preferred_element_type=F32)
            delta = jnp.sum(do_v * o_ref[...], axis=-1, keepdims=True)
            ds = (p * (dp - delta) * MLA_SCALE).astype(BF16)
            dv_sc[...] += lax.dot_general(p.astype(BF16), do_b, tn, preferred_element_type=F32)
            dk_sc[...] += lax.dot_general(ds, q_ref[...], tn, preferred_element_type=F32)
            rows = pl.ds(pl.multiple_of(qi * t, t), t)
            dq_sc[rows, :] += jnp.dot(ds, k_ref[...], preferred_element_type=F32)

        @pl.when(qi > kj)
        def _():
            pair(False)

        @pl.when(qi == kj)
        def _():
            pair(True)

        @pl.when(qi == n - 1)
        def _():
            dk_ref[...] = dk_sc[...].astype(dk_ref.dtype)
            dv_ref[...] = dv_sc[...].astype(dv_ref.dtype)

        @pl.when(jnp.logical_and(kj == n - 1, qi == n - 1))
        def _():
            dq_ref[...] = dq_sc[...].astype(dq_ref.dtype)

    qmap = lambda hh, j, i: (jnp.maximum(i, j), hh)
    kmap = lambda hh, j, i: (j, hh)
    return pl.pallas_call(
        body, out_shape=[jax.ShapeDtypeStruct(q.shape, BF16), jax.ShapeDtypeStruct(k.shape, BF16), jax.ShapeDtypeStruct(v.shape, BF16)],
        grid=(h, n, n),
        in_specs=[pl.BlockSpec((t, MLA_DQK), qmap), pl.BlockSpec((t, MLA_DQK), kmap), pl.BlockSpec((t, HEAD), kmap),
                  pl.BlockSpec((t, HEAD), qmap), pl.BlockSpec((t, HEAD), qmap), pl.BlockSpec((t, HEAD), qmap)],
        out_specs=[pl.BlockSpec((s, MLA_DQK), lambda hh, j, i: (0, hh)), pl.BlockSpec((t, MLA_DQK), kmap),
                   pl.BlockSpec((t, HEAD), kmap)],
        scratch_shapes=[pltpu.VMEM((s, MLA_DQK), F32), pltpu.VMEM((t, MLA_DQK), F32), pltpu.VMEM((t, HEAD), F32)],
        compiler_params=_cparams("parallel", "arbitrary", "arbitrary"), name="mla_attn_bwd")(q, k, v, o, lse, do)


@jax.custom_vjp
def mla_attention(q, k, v):
    return _mla_attn_fwd(q, k, v)[0]


def _mla_attention_fwd(q, k, v):
    o, lse = _mla_attn_fwd(q, k, v)
    return o, (q, k, v, o, lse)


def _mla_attention_bwd(saved, do):
    return tuple(_mla_attn_bwd(*saved, do))


mla_attention.defvjp(_mla_attention_fwd, _mla_attention_bwd)


DIL_SCALE = HEAD ** -0.5
GROUP_W = 4 * HEAD


def _dil_scores(q, kp, kc, n):
    dn = (((1,), (1,)), ((), ()))
    sp = lax.dot_general(q, kp, dn, preferred_element_type=F32) * DIL_SCALE
    sc = lax.dot_general(q, kc, dn, preferred_element_type=F32) * DIL_SCALE
    qi = lax.broadcasted_iota(jnp.int32, sp.shape, 0)
    kc_i = lax.broadcasted_iota(jnp.int32, sp.shape, 1)
    vp = jnp.logical_and(kc_i >= qi, n > 0)
    vc = kc_i <= qi
    return sp, sc, vp, vc


def _dil_specs(d):
    cur = pl.BlockSpec((DIL_BLOCK, GROUP_W), lambda r, n: (n, r))
    prev = pl.BlockSpec((DIL_BLOCK, GROUP_W), lambda r, n: (jnp.maximum(n - 1, 0), r))
    return cur, prev


def _dil_fwd(q, k, v, d):
    l = q.shape[0]
    nb = l // DIL_BLOCK
    cur, prev = _dil_specs(d)

    def body(q_ref, kp_ref, kc_ref, vp_ref, vc_ref, o_ref, lse_ref):
        n = pl.program_id(1)
        heads = range(4)
        sl = [slice(h * HEAD, (h + 1) * HEAD) for h in heads]
        scores = [_dil_scores(q_ref[:, sl[h]], kp_ref[:, sl[h]], kc_ref[:, sl[h]], n) for h in heads]
        sp = [jnp.where(vp, s_p, -jnp.inf) for s_p, _, vp, _ in scores]
        sc = [jnp.where(vc, s_c, -jnp.inf) for _, s_c, _, vc in scores]
        m = [jnp.maximum(jnp.max(sp[h], axis=-1, keepdims=True), jnp.max(sc[h], axis=-1, keepdims=True)) for h in heads]
        ep = [jnp.exp(sp[h] - m[h]) for h in heads]
        ec = [jnp.exp(sc[h] - m[h]) for h in heads]
        den = [jnp.sum(ep[h], axis=-1, keepdims=True) + jnp.sum(ec[h], axis=-1, keepdims=True) for h in heads]
        acc = [jnp.dot(ep[h].astype(BF16), vp_ref[:, sl[h]], preferred_element_type=F32)
               + jnp.dot(ec[h].astype(BF16), vc_ref[:, sl[h]], preferred_element_type=F32) for h in heads]
        for h in heads:
            o_ref[:, sl[h]] = acc[h] / den[h]
            lse_ref[:, sl[h]] = jnp.broadcast_to(m[h] + jnp.log(den[h]), (DIL_BLOCK, HEAD))

    return pl.pallas_call(
        body, out_shape=[jax.ShapeDtypeStruct(q.shape, F32)] * 2, grid=(d, nb),
        in_specs=[cur, prev, cur, prev, cur], out_specs=[cur, cur],
        compiler_params=_cparams("parallel", "parallel"), name=f"dil_fwd_d{d}")(q, k, k, v, v)


def _dil_bwd(q, k, v, o, lse, do, dlse, d):
    l = q.shape[0]
    nb = l // DIL_BLOCK
    cur, prev = _dil_specs(d)
    tn = (((0,), (0,)), ((), ()))
    nt = (((1,), (1,)), ((), ()))

    def body(q_ref, kp_ref, kc_ref, vp_ref, vc_ref, o_ref, lse_ref, do_ref, dl_ref,
             dq_ref, dkp_ref, dkc_ref, dvp_ref, dvc_ref):
        n = pl.program_id(1)
        heads = range(4)
        sl = [slice(h * HEAD, (h + 1) * HEAD) for h in heads]
        scores = [_dil_scores(q_ref[:, sl[h]], kp_ref[:, sl[h]], kc_ref[:, sl[h]], n) for h in heads]
        lse = [lse_ref[:, h * HEAD:h * HEAD + 1] for h in heads]
        pp = [jnp.where(scores[h][2], jnp.exp(scores[h][0] - lse[h]), 0.0) for h in heads]
        pc = [jnp.where(scores[h][3], jnp.exp(scores[h][1] - lse[h]), 0.0) for h in heads]
        do_b = [do_ref[:, sl[h]].astype(BF16) for h in heads]
        corr = [jnp.sum(dl_ref[:, sl[h]], axis=-1, keepdims=True)
                - jnp.sum(do_ref[:, sl[h]] * o_ref[:, sl[h]], axis=-1, keepdims=True) for h in heads]
        dsp = [(pp[h] * (lax.dot_general(do_b[h], vp_ref[:, sl[h]], nt, preferred_element_type=F32) + corr[h])
                * DIL_SCALE).astype(BF16) for h in heads]
        dsc = [(pc[h] * (lax.dot_general(do_b[h], vc_ref[:, sl[h]], nt, preferred_element_type=F32) + corr[h])
                * DIL_SCALE).astype(BF16) for h in heads]
        for h in heads:
            dq_ref[:, sl[h]] = (jnp.dot(dsp[h], kp_ref[:, sl[h]], preferred_element_type=F32)
                                + jnp.dot(dsc[h], kc_ref[:, sl[h]], preferred_element_type=F32)).astype(dq_ref.dtype)
            dkp_ref[:, sl[h]] = lax.dot_general(dsp[h], q_ref[:, sl[h]], tn, preferred_element_type=F32)
            dkc_ref[:, sl[h]] = lax.dot_general(dsc[h], q_ref[:, sl[h]], tn, preferred_element_type=F32)
            dvp_ref[:, sl[h]] = lax.dot_general(pp[h].astype(BF16), do_b[h], tn, preferred_element_type=F32)
            dvc_ref[:, sl[h]] = lax.dot_general(pc[h].astype(BF16), do_b[h], tn, preferred_element_type=F32)

    dq, dkp, dkc, dvp, dvc = pl.pallas_call(
        body, out_shape=[jax.ShapeDtypeStruct(q.shape, BF16)] + [jax.ShapeDtypeStruct(q.shape, F32)] * 4, grid=(d, nb),
        in_specs=[cur, prev, cur, prev, cur, cur, cur, cur, cur], out_specs=[cur] * 5,
        compiler_params=_cparams("parallel", "parallel"), name=f"dil_bwd_d{d}")(q, k, k, v, v, o, lse, do, dlse)

    def fold(cur_part, prev_part):
        shifted = jnp.concatenate([prev_part[DIL_BLOCK:], jnp.zeros((DIL_BLOCK, prev_part.shape[1]), F32)], axis=0)
        return (cur_part + shifted).astype(BF16)

    return dq, fold(dkc, dkp), fold(dvc, dvp)


def _make_dil(d):
    @jax.custom_vjp
    def att(q, k, v):
        return tuple(_dil_fwd(q, k, v, d))

    def fwd(q, k, v):
        o, lse = _dil_fwd(q, k, v, d)
        return (o, lse), (q, k, v, o, lse)

    def bwd(saved, cts):
        return _dil_bwd(*saved, cts[0], cts[1], d)

    att.defvjp(fwd, bwd)
    return att


def dilated_group(q, k, v, d):
    s = q.shape[0]
    view = lambda t: t.reshape(s // d, d * GROUP_W)
    o, lse = _make_dil(d)(view(q), view(k), view(v))
    return o.reshape(s, GROUP_W), lse.reshape(s, GROUP_W)


def _dn_post(c, kind, scale):
    m = _silu(c)
    nrm = m * lax.rsqrt(jnp.sum(m * m, axis=-1, keepdims=True) + 1e-6) * scale
    return kind * nrm + (1.0 - kind) * m


def _dn_kind_scale(j):
    kind = jnp.where(j < 2 * DN_HEADS, 1.0, 0.0).astype(F32)
    scale = jnp.where(j < DN_HEADS, DN_DIM ** -0.5, 1.0).astype(F32)
    return kind, scale


_CONV_RB = 512
_CONV_PAD = 8


def _conv_windows(pad_ref, w_ref, r0, rb, sign):
    acc = None
    for sh in range(DN_CONV):
        win = pad_ref[pl.ds(r0 + _CONV_PAD * (sign < 0) + sign * sh, rb), :]
        term = w_ref[DN_CONV - 1 - sh:DN_CONV - sh, :] * win
        acc = term if acc is None else acc + term
    return acc


def _dn_conv_fwd(x, w):
    s, width = x.shape
    rb = min(_CONV_RB, s)

    def body(x_ref, w_ref, o_ref, pad_ref):
        kind, scale = _dn_kind_scale(pl.program_id(0))
        pad_ref[0:_CONV_PAD, :] = jnp.zeros((_CONV_PAD, LANE), F32)
        pad_ref[_CONV_PAD:, :] = x_ref[...]
        for r0 in range(0, s, rb):
            c = _conv_windows(pad_ref, w_ref, r0, rb, -1)
            o_ref[r0:r0 + rb, :] = _dn_post(c, kind, scale)

    return pl.pallas_call(
        body, out_shape=jax.ShapeDtypeStruct(x.shape, F32), grid=(width // LANE,),
        in_specs=[pl.BlockSpec((s, LANE), lambda j: (0, j)), pl.BlockSpec((DN_CONV, LANE), lambda j: (0, j))],
        out_specs=pl.BlockSpec((s, LANE), lambda j: (0, j)), scratch_shapes=[pltpu.VMEM((s + _CONV_PAD, LANE), F32)],
        compiler_params=_cparams("parallel"), name="dn_conv_fwd")(x, w)


def _dn_conv_bwd(x, w, dy):
    s, width = x.shape
    rb = min(_CONV_RB, s)

    def body(x_ref, w_ref, dy_ref, dx_ref, dw_ref, pad_ref, dpad_ref):
        kind, scale = _dn_kind_scale(pl.program_id(0))
        pad_ref[0:_CONV_PAD, :] = jnp.zeros((_CONV_PAD, LANE), F32)
        pad_ref[_CONV_PAD:, :] = x_ref[...]
        dpad_ref[s:, :] = jnp.zeros((_CONV_PAD, LANE), F32)
        dws = [jnp.zeros((1, LANE), F32) for _ in range(DN_CONV)]
        for r0 in range(0, s, rb):
            c = _conv_windows(pad_ref, w_ref, r0, rb, -1)
            _, vjp = jax.vjp(lambda cc: _dn_post(cc, kind, scale), c)
            dc = vjp(dy_ref[r0:r0 + rb, :])[0]
            dpad_ref[r0:r0 + rb, :] = dc
            for sh in range(DN_CONV):
                win = pad_ref[pl.ds(r0 + _CONV_PAD - sh, rb), :]
                dws[DN_CONV - 1 - sh] = dws[DN_CONV - 1 - sh] + jnp.sum(dc * win, axis=0, keepdims=True)
        for j in range(DN_CONV):
            dw_ref[j:j + 1, :] = dws[j]
        for r0 in range(0, s, rb):
            dx_ref[r0:r0 + rb, :] = _conv_windows(dpad_ref, w_ref, r0, rb, 1)

    return pl.pallas_call(
        body, out_shape=[jax.ShapeDtypeStruct(x.shape, F32), jax.ShapeDtypeStruct(w.shape, F32)], grid=(width // LANE,),
        in_specs=[pl.BlockSpec((s, LANE), lambda j: (0, j)), pl.BlockSpec((DN_CONV, LANE), lambda j: (0, j)),
                  pl.BlockSpec((s, LANE), lambda j: (0, j))],
        out_specs=[pl.BlockSpec((s, LANE), lambda j: (0, j)), pl.BlockSpec((DN_CONV, LANE), lambda j: (0, j))],
        scratch_shapes=[pltpu.VMEM((s + _CONV_PAD, LANE), F32), pltpu.VMEM((s + _CONV_PAD, LANE), F32)],
        compiler_params=_cparams("parallel"), name="dn_conv_bwd")(x, w, dy)


@jax.custom_vjp
def dn_conv(x, w):
    return _dn_conv_fwd(x, w)


dn_conv.defvjp(lambda x, w: (_dn_conv_fwd(x, w), (x, w)), lambda saved, dy: tuple(_dn_conv_bwd(*saved, dy)))


_NN = (((1,), (0,)), ((), ()))
_NT = (((1,), (1,)), ((), ()))
_TN = (((0,), (0,)), ((), ()))


def _bd(a, b, dims, passes=1):
    d = lambda x, y: lax.dot_general(x, y, dims, preferred_element_type=F32)
    if passes == 0:
        return lax.dot_general(a, b, dims, precision=lax.Precision.HIGHEST, preferred_element_type=F32)
    ah, bh = a.astype(BF16), b.astype(BF16)
    if passes == 1:
        return d(ah, bh)
    al, bl = (a - ah.astype(F32)).astype(BF16), (b - bh.astype(F32)).astype(BF16)
    return d(ah, bh) + d(ah, bl) + d(al, bh)


@functools.partial(jax.custom_vjp, nondiff_argnums=(2, 3))
def _pdot(a, b, dims, passes):
    return _bd(a, b, dims, passes)


def _pdot_bwd(dims, passes, saved, g):
    a, b = saved
    if dims == _NN:
        return _bd(g, b, _NT, passes), _bd(a, g, _TN, passes)
    if dims == _NT:
        return _bd(g, b, _NN, passes), _bd(g, a, _TN, passes)
    return _bd(b, g, _NT, passes), _bd(a, g, _NN, passes)


_pdot.defvjp(lambda a, b, dims, passes: (_bd(a, b, dims, passes), (a, b)), _pdot_bwd)


GDN_DOT_PASSES = 1
GDN_SOLVE_PASSES = 3


def _hdot(a, b, dims=_NN):
    return _pdot(a, b, dims, GDN_DOT_PASSES)


def _xdot(a, b, dims=_NN):
    return _pdot(a, b, dims, GDN_SOLVE_PASSES)


def _split3(x):
    hi = x.astype(BF16)
    r1 = x - hi.astype(F32)
    mid = r1.astype(BF16)
    lo = (r1 - mid.astype(F32)).astype(BF16)
    return hi, mid, lo


def _tri_dot(tri, x, dims):
    t = tri.astype(BF16)
    return sum(lax.dot_general(t, p, dims, preferred_element_type=F32) for p in _split3(x))


@jax.custom_vjp
def _cumsum_rows(x):
    c = x.shape[0]
    tri = lax.broadcasted_iota(jnp.int32, (c, c), 0) >= lax.broadcasted_iota(jnp.int32, (c, c), 1)
    return _tri_dot(tri, x, _NN)


def _cumsum_rows_bwd(_, g):
    c = g.shape[0]
    tri = lax.broadcasted_iota(jnp.int32, (c, c), 0) >= lax.broadcasted_iota(jnp.int32, (c, c), 1)
    return (_tri_dot(tri, g, _TN),)


_cumsum_rows.defvjp(lambda x: (_cumsum_rows(x), None), _cumsum_rows_bwd)


def _gdn_prep_fn(qkv, gb):
    c = GDN_CHUNK
    row = lax.broadcasted_iota(jnp.int32, (c, c), 0)
    col = lax.broadcasted_iota(jnp.int32, (c, c), 1)
    incl, strict = row >= col, row > col
    lane = _lanes(gb)
    heads = range(DN_HEADS)
    q, k, v = qkv[:DN_HEADS], qkv[DN_HEADS:2 * DN_HEADS], qkv[2 * DN_HEADS:]
    g = [jnp.sum(jnp.where(lane == h, gb, 0.0), axis=-1, keepdims=True) for h in heads]
    beta = [jnp.sum(jnp.where(lane == DN_HEADS + h, gb, 0.0), axis=-1, keepdims=True) for h in heads]
    gcb = [_cumsum_rows(jnp.broadcast_to(g[h], (c, c))) for h in heads]
    decay = [jnp.where(incl, jnp.exp(jnp.where(incl, gcb[h] - gcb[h].T, 0.0)), 0.0) for h in heads]
    kb = [k[h] * beta[h] for h in heads]
    a = [jnp.where(strict, _hdot(kb[h], k[h], _NT) * decay[h], 0.0) for h in heads]
    pw = [-a[h] for h in heads]
    t = [(row == col).astype(F32) + pw[h] for h in heads]
    for _ in range(int(math.log2(c)) - 1):
        pw = [_xdot(pw[h], pw[h]) for h in heads]
        t = [t[h] + _xdot(t[h], pw[h]) for h in heads]
    eg = [jnp.exp(gcb[h]) for h in heads]
    u = [_xdot(t[h], v[h] * beta[h]) for h in heads]
    w = [_xdot(t[h], kb[h] * eg[h]) for h in heads]
    qk = [jnp.where(incl, _hdot(q[h], k[h], _NT) * decay[h], 0.0) for h in heads]
    g_last = [jnp.sum(jnp.where(row == c - 1, gcb[h], 0.0), axis=0, keepdims=True) for h in heads]
    kg = [k[h] * jnp.exp(g_last[h] - gcb[h]) for h in heads]
    qg = [q[h] * eg[h] for h in heads]
    el = [jnp.broadcast_to(jnp.exp(g_last[h]), (8, LANE)) for h in heads]
    return [u, w, qg, kg, qk, el]


def _gdn_prep_specs(nq):
    c = GDN_CHUNK
    big = pl.BlockSpec((c, nq * LANE), lambda n: (n, 0))
    return big


def _gdn_prep_fwd(qkv, gb):
    s = qkv.shape[0]
    c = GDN_CHUNK
    n = s // c
    hw = DN_HEADS * LANE

    def body(qkv_ref, gb_ref, u_ref, w_ref, qg_ref, kg_ref, qk_ref, el_ref):
        res = _gdn_prep_fn(_chunks(qkv_ref), gb_ref[...])
        for ref, chs in zip((u_ref, w_ref, qg_ref, kg_ref, qk_ref, el_ref), res):
            _store(ref, chs)

    row = pl.BlockSpec((c, hw), lambda i: (i, 0))
    return pl.pallas_call(
        body, out_shape=[jax.ShapeDtypeStruct((s, hw), F32)] * 5 + [jax.ShapeDtypeStruct((n * 8, hw), F32)], grid=(n,),
        in_specs=[pl.BlockSpec((c, 3 * hw), lambda i: (i, 0)), pl.BlockSpec((c, LANE), lambda i: (i, 0))],
        out_specs=[row] * 5 + [pl.BlockSpec((8, hw), lambda i: (i, 0))],
        compiler_params=_cparams("parallel"), name="gdn_prep_fwd")(qkv, gb)


def _gdn_prep_bwd(qkv, gb, cts):
    s = qkv.shape[0]
    c = GDN_CHUNK
    n = s // c
    hw = DN_HEADS * LANE

    def body(qkv_ref, gb_ref, du_ref, dw_ref, dqg_ref, dkg_ref, dqk_ref, del_ref, dqkv_ref, dgb_ref):
        _, vjp = jax.vjp(_gdn_prep_fn, _chunks(qkv_ref), gb_ref[...])
        ct = [_chunks(r) for r in (du_ref, dw_ref, dqg_ref, dkg_ref, dqk_ref, del_ref)]
        dqkv, dgb = vjp(ct)
        _store(dqkv_ref, dqkv)
        dgb_ref[...] = dgb

    row = pl.BlockSpec((c, hw), lambda i: (i, 0))
    return pl.pallas_call(
        body, out_shape=[jax.ShapeDtypeStruct(qkv.shape, F32), jax.ShapeDtypeStruct(gb.shape, F32)], grid=(n,),
        in_specs=[pl.BlockSpec((c, 3 * hw), lambda i: (i, 0)), pl.BlockSpec((c, LANE), lambda i: (i, 0))]
        + [row] * 5 + [pl.BlockSpec((8, hw), lambda i: (i, 0))],
        out_specs=[pl.BlockSpec((c, 3 * hw), lambda i: (i, 0)), pl.BlockSpec((c, LANE), lambda i: (i, 0))],
        compiler_params=_cparams("parallel"), name="gdn_prep_bwd")(qkv, gb, *cts)


@jax.custom_vjp
def gdn_prep(qkv, gb):
    return tuple(_gdn_prep_fwd(qkv, gb))


gdn_prep.defvjp(lambda qkv, gb: (tuple(_gdn_prep_fwd(qkv, gb)), (qkv, gb)),
                lambda saved, cts: tuple(_gdn_prep_bwd(*saved, cts)))


def _gdn_scan_fwd(u, w, qg, kg, qk, el):
    s = u.shape[0]
    c = GDN_CHUNK
    n = s // c
    hw = DN_HEADS * LANE

    def body(u_ref, w_ref, qg_ref, kg_ref, qk_ref, el_ref, o_ref, st_ref, s_sc):
        @pl.when(pl.program_id(0) == 0)
        def _():
            s_sc[...] = jnp.zeros_like(s_sc)

        heads = range(DN_HEADS)
        sl = [slice(h * LANE, (h + 1) * LANE) for h in heads]
        st = [s_sc[h] for h in heads]
        for h in heads:
            st_ref[sl[h], :] = st[h]
        v_new = [u_ref[:, sl[h]] - _hdot(w_ref[:, sl[h]], st[h]) for h in heads]
        o_st = [_hdot(qg_ref[:, sl[h]], st[h]) for h in heads]
        o_in = [_hdot(qk_ref[:, sl[h]], v_new[h]) for h in heads]
        s_up = [_hdot(kg_ref[:, sl[h]], v_new[h], _TN) for h in heads]
        for h in heads:
            o_ref[:, sl[h]] = o_st[h] + o_in[h]
            s_sc[h] = st[h] * el_ref[0:1, sl[h]] + s_up[h]

    row = pl.BlockSpec((c, hw), lambda i: (i, 0))
    return pl.pallas_call(
        body, out_shape=[jax.ShapeDtypeStruct((s, hw), F32), jax.ShapeDtypeStruct((n, hw, LANE), F32)], grid=(n,),
        in_specs=[row] * 5 + [pl.BlockSpec((8, hw), lambda i: (i, 0))],
        out_specs=[row, pl.BlockSpec((None, hw, LANE), lambda i: (i, 0, 0))],
        scratch_shapes=[pltpu.VMEM((DN_HEADS, LANE, LANE), F32)],
        compiler_params=_cparams("arbitrary"), name="gdn_scan_fwd")(u, w, qg, kg, qk, el)


def _gdn_scan_bwd(u, w, qg, kg, qk, el, states, do):
    s = u.shape[0]
    c = GDN_CHUNK
    n = s // c
    hw = DN_HEADS * LANE

    def body(u_ref, w_ref, qg_ref, kg_ref, qk_ref, el_ref, st_ref, do_ref,
             du_ref, dw_ref, dqg_ref, dkg_ref, dqk_ref, del_ref, ds_sc):
        @pl.when(pl.program_id(0) == 0)
        def _():
            ds_sc[...] = jnp.zeros_like(ds_sc)

        heads = range(DN_HEADS)
        sl = [slice(h * LANE, (h + 1) * LANE) for h in heads]
        st = [st_ref[sl[h], :] for h in heads]
        ds = [ds_sc[h] for h in heads]
        do = [do_ref[:, sl[h]] for h in heads]
        v_new = [u_ref[:, sl[h]] - _hdot(w_ref[:, sl[h]], st[h]) for h in heads]
        dv_new = [_hdot(qk_ref[:, sl[h]], do[h], _TN) + _hdot(kg_ref[:, sl[h]], ds[h]) for h in heads]
        first_row = lax.broadcasted_iota(jnp.int32, (8, LANE), 0) == 0
        for h in heads:
            du_ref[:, sl[h]] = dv_new[h]
            dw_ref[:, sl[h]] = -_hdot(dv_new[h], st[h], _NT)
            dqg_ref[:, sl[h]] = _hdot(do[h], st[h], _NT)
            dqk_ref[:, sl[h]] = _hdot(do[h], v_new[h], _NT)
            dkg_ref[:, sl[h]] = _hdot(v_new[h], ds[h], _NT)
            del_ref[:, sl[h]] = jnp.where(first_row, jnp.sum(st[h] * ds[h], axis=0, keepdims=True), 0.0)
        ds_new = [_hdot(qg_ref[:, sl[h]], do[h], _TN) + ds[h] * el_ref[0:1, sl[h]] - _hdot(w_ref[:, sl[h]], dv_new[h], _TN)
                  for h in heads]
        for h in heads:
            ds_sc[h] = ds_new[h]

    row = pl.BlockSpec((c, hw), lambda i: (n - 1 - i, 0))
    small = pl.BlockSpec((8, hw), lambda i: (n - 1 - i, 0))
    return pl.pallas_call(
        body, out_shape=[jax.ShapeDtypeStruct((s, hw), F32)] * 5 + [jax.ShapeDtypeStruct((n * 8, hw), F32)], grid=(n,),
        in_specs=[row] * 5 + [small, pl.BlockSpec((None, hw, LANE), lambda i: (n - 1 - i, 0, 0)), row],
        out_specs=[row] * 5 + [small], scratch_shapes=[pltpu.VMEM((DN_HEADS, LANE, LANE), F32)],
        compiler_params=_cparams("arbitrary"), name="gdn_scan_bwd")(u, w, qg, kg, qk, el, states, do)


@jax.custom_vjp
def gdn_scan(u, w, qg, kg, qk, el):
    return _gdn_scan_fwd(u, w, qg, kg, qk, el)[0]


def _gdn_scan_vfwd(*args):
    o, states = _gdn_scan_fwd(*args)
    return o, args + (states,)


gdn_scan.defvjp(_gdn_scan_vfwd, lambda saved, do: tuple(_gdn_scan_bwd(*saved, do)))


def _loss_call(y, t):
    s, d = y.shape
    br = min(ROW_BLOCK, s)
    n = s // br

    def body(y_ref, t_ref, loss_ref, dy_ref, acc):
        i = pl.program_id(0)

        @pl.when(i == 0)
        def _():
            acc[...] = jnp.zeros_like(acc)

        e = y_ref[...] - t_ref[...]
        dy_ref[...] = e / d
        acc[...] += jnp.sum(e * e, axis=0, keepdims=True)

        @pl.when(i == n - 1)
        def _():
            loss_ref[...] = jnp.broadcast_to(jnp.sum(acc[...], axis=1, keepdims=True) * (0.5 / d), loss_ref.shape)

    row = pl.BlockSpec((br, d), lambda i: (i, 0))
    return pl.pallas_call(
        body, out_shape=[jax.ShapeDtypeStruct((1, LANE), F32), jax.ShapeDtypeStruct((s, d), F32)], grid=(n,),
        in_specs=[row, row], out_specs=[pl.BlockSpec((1, LANE), lambda i: (0, 0)), row],
        scratch_shapes=[pltpu.VMEM((1, d), F32)], compiler_params=_cparams("arbitrary"), name="loss_head")(y, t)


@jax.custom_vjp
def loss_head(y, t):
    return _loss_call(y, t)[0][0, 0]


def _loss_head_fwd(y, t):
    loss, dy = _loss_call(y, t)
    return loss[0, 0], (dy,)


loss_head.defvjp(_loss_head_fwd, lambda saved, g: (saved[0] * g, -saved[0] * g))


IN_OFF = {}
_o = 0
for _name, _size in (('q_lat', 384), ('kv_lat', 320), ('z_a', 512), ('dn_qkv', 1536), ('dn_ab', 8), ('z_b', 512),
                     ('dil_qkv', 4608), ('z_c', 512), ('gate', 3072)):
    IN_OFF[_name] = (_o, _o + _size)
    _o += _size
IN_WIDTH = _o
N_CHIPS = 4


def _cols(w, name):
    a, b = IN_OFF[name]
    return w[:, a:b]


def _pad_cols(w, to):
    return jnp.concatenate([w, jnp.zeros((w.shape[0], to - w.shape[1]), w.dtype)], axis=1)


def _pad_row(v, to=None):
    v = v.reshape(1, -1)
    return v if to is None or v.shape[1] == to else _pad_cols(v, to)


def _shard_cols(pieces, a, b):
    wsh = pieces[0].shape[1]
    parts = [pieces[j][:, max(a, j * wsh) - j * wsh:min(b, (j + 1) * wsh) - j * wsh]
             for j in range(len(pieces)) if max(a, j * wsh) < min(b, (j + 1) * wsh)]
    return parts[0] if len(parts) == 1 else jnp.concatenate(parts, axis=1)


def _win_groups_impl(w_in4):
    out = []
    for l in range(w_in4.shape[1]):
        pieces = [w_in4[j, l] for j in range(w_in4.shape[0])]
        cols = lambda name: _shard_cols(pieces, *IN_OFF[name])
        out.append((_pad_cols(jnp.concatenate([cols('q_lat'), cols('kv_lat')], axis=1), 768),
                    jnp.concatenate([cols('z_a'), cols('z_b'), cols('z_c')], axis=1),
                    cols('dn_qkv'), _pad_cols(cols('dn_ab'), LANE), cols('dil_qkv'), cols('gate')))
    return tuple(out)


@jax.custom_vjp
def win_groups(w_in4):
    return _win_groups_impl(w_in4)


def _win_groups_bwd(_, cts):
    n_chip, depth = N_CHIPS, len(cts)
    wsh = IN_WIDTH // n_chip
    per_layer = []
    for l in range(depth):
        dmla, dz, dgdn, dab, ddil, dgate = cts[l]
        full = jnp.concatenate([dmla[:, :704], dz[:, :512], dgdn, dab[:, :8], dz[:, 512:1024], ddil, dz[:, 1024:], dgate], axis=1)
        per_layer.append([full[:, j * wsh:(j + 1) * wsh] for j in range(n_chip)])
    return (jnp.stack([jnp.stack([per_layer[l][j] for l in range(depth)]) for j in range(n_chip)]),)


win_groups.defvjp(lambda w: (_win_groups_impl(w), None), _win_groups_bwd)


def _layer(x, p, tabs):
    w_mla, w_z, w_gdn, w_ab, w_dil, w_gate = p['w_in_groups']
    h, = rowwise(f_rms_full, "rms_in", [x], [_pad_row(p['norm_g'])], [(D_MODEL, BF16)])
    mla_in = matmul(h, w_mla)
    z = matmul(h, w_z)
    dn_qkv = matmul(h, w_gdn)
    dn_ab = matmul(h, w_ab)
    dil_qkv = matmul(h, w_dil)
    gate = matmul(h, w_gate)

    qn, kvn, kpe = rowwise(f_mla1, "mla_norm", [mla_in], [_pad_row(p['mla_q_a_norm_g']), _pad_row(p['mla_kv_a_norm_g'])],
                           [(MLA_Q_RANK, BF16), (MLA_KV_RANK, BF16), (LANE, F32)])
    wq = p['mla_w_q_b']
    wq_pad = jnp.concatenate(
        [wq[:, hh * MLA_QK:hh * MLA_QK + MLA_NOPE] for hh in range(MLA_HEADS)]
        + [_pad_cols(wq[:, hh * MLA_QK + MLA_NOPE:(hh + 1) * MLA_QK], LANE) for hh in range(MLA_HEADS)], axis=1)
    q = matmul(qn, wq_pad)
    kv = matmul(kvn, p['mla_w_kv_b'])
    gq, gk = p['mla_q_norm_g'], p['mla_k_norm_g']
    q_att, k_att, v_att = rowwise(
        f_mla2, "mla_qk", [q, kv, kpe, tabs['cos_r'], tabs['sin_r']],
        [_pad_row(gq[:MLA_NOPE]), _pad_row(gq[MLA_NOPE:], LANE), _pad_row(gk[:MLA_NOPE]), _pad_row(gk[MLA_NOPE:], LANE)],
        [(MLA_HEADS * MLA_DQK, BF16), (MLA_HEADS * MLA_DQK, BF16), (MLA_HEADS * HEAD, BF16)], nograd=(3, 4))
    y_a = mla_attention(q_att, k_att, v_att)

    qkv_n = dn_conv(dn_qkv, p['dn_conv_w'])
    gb, = rowwise(f_gates, "dn_gates", [dn_ab], [_pad_row(p['dn_a_log'], LANE), _pad_row(p['dn_dt_bias'], LANE)], [(LANE, F32)])
    o_b = gdn_scan(*gdn_prep(qkv_n, gb))
    y_b, = rowwise(f_headnorm, "dn_out_norm", [o_b], [_pad_row(p['dn_out_norm_g'])], [(DN_HEADS * DN_DIM, F32)])

    qd, kd, vd = rowwise(f_dil, "dil_qk", [dil_qkv, tabs['cos_h'], tabs['sin_h']],
                         [_pad_row(p['dil_q_norm_g']), _pad_row(p['dil_k_norm_g'])],
                         [(3 * GROUP_W, BF16)] * 3, nograd=(1, 2))
    outs, lses = [], []
    for gi, d in enumerate(DIL_DILATIONS):
        sl = slice(gi * GROUP_W, (gi + 1) * GROUP_W)
        o_g, l_g = dilated_group(qd[:, sl], kd[:, sl], vd[:, sl], d)
        outs.append(o_g)
        lses.append(l_g)
    y_c, = rowwise(f_comb, "dil_comb", outs + lses, [], [(GROUP_W, F32)])

    ys = rowwise(f_merge1, "merge_silu", [y_a, y_b, y_c, z], [], [(BRANCH_W, BF16)] * 3)
    bo = [matmul(ys[b], p['w_branch'][b]) for b in range(3)]
    mixed, = rowwise(f_merge2, "merge_gate", [gate] + bo, [], [(D_MODEL, BF16)])
    return matmul(mixed, p['w_out'], res=x)


def _rope_tables(pos, dim):
    inv_freq = 1.0 / (ROPE_THETA ** (jnp.arange(0, dim, 2, dtype=F32) / dim))
    ang = pos.astype(F32)[:, None] * inv_freq
    return jnp.cos(ang), jnp.sin(ang)


def _tables(pos):
    cr, sr = _rope_tables(pos, MLA_ROPE)
    ch, sh = _rope_tables(pos, HEAD)
    zero = jnp.zeros((pos.shape[0], LANE - MLA_ROPE), F32)
    return {'cos_r': jnp.concatenate([cr, cr, zero], axis=1), 'sin_r': jnp.concatenate([sr, sr, zero], axis=1),
            'cos_h': jnp.concatenate([ch, ch], axis=1), 'sin_h': jnp.concatenate([sh, sh], axis=1)}


def _local_loss(w_in4, mats, conv_w, small, x, target, tabs):
    groups = win_groups(w_in4)
    for l in range(DEPTH):
        p = {k: v[l] for k, v in mats.items()}
        p.update({k: v[l] for k, v in small.items()})
        p['dn_conv_w'] = conv_w[l]
        p['w_in_groups'] = groups[l]
        x = _layer(x, p, tabs)
    return loss_head(x, target)


def _pack(arrays, dtype, row_tile):
    flat = jnp.concatenate([a.astype(dtype).reshape(-1) for a in arrays])
    rows = -(-flat.shape[0] // (LANE * row_tile)) * row_tile
    flat = jnp.concatenate([flat, jnp.zeros((rows * LANE - flat.shape[0],), dtype)])
    return flat.reshape(rows, LANE)


def _unpack_impl(buf, shapes):
    flat = buf.reshape(-1)
    out, off = [], 0
    for shp in shapes:
        n = math.prod(shp)
        out.append(flat[off:off + n].reshape(shp))
        off += n
    return tuple(out)


@functools.partial(jax.custom_vjp, nondiff_argnums=(1, 2, 3))
def _unpack_p(buf, shapes, dtype_name, rows):
    return _unpack_impl(buf, shapes)


_unpack_p.defvjp(lambda buf, shapes, dtype_name, rows: (_unpack_impl(buf, shapes), None),
                 lambda shapes, dtype_name, rows, _, cts: (_pack(cts, jnp.dtype(dtype_name), rows),))


def _unpack(buf, shapes):
    return _unpack_p(buf, tuple(shapes), jnp.dtype(buf.dtype).name, buf.shape[0])


def _full_from_chips(buf4, shard_shapes):
    per_chip = [_unpack(buf4[j], tuple(shard_shapes)) for j in range(4)]
    return {name: jnp.concatenate([per_chip[j][i] for j in range(4)], axis=axis) for i, (name, axis) in enumerate(MATS)}


_HBM = pl.BlockSpec(memory_space=pltpu.HBM)
_VMEM = pl.BlockSpec(memory_space=pltpu.VMEM)


def _chip_peers(x, y):
    return [(1 - x, y), (x, 1 - y), (1 - x, 1 - y)]


def chip_all_to_all(arrays):
    n = len(arrays)

    def body(*refs):
        in_refs, out_refs = refs[:n], refs[n:2 * n]
        send_sems, recv_sems, local_sems = refs[2 * n:]
        x, y, c = lax.axis_index("x"), lax.axis_index("y"), lax.axis_index("c")
        me = 2 * x + y
        peers = _chip_peers(x, y)
        local, sends = [], []
        for a, (i_ref, o_ref) in enumerate(zip(in_refs, out_refs)):
            local.append(pltpu.make_async_copy(i_ref.at[me], o_ref.at[me], local_sems.at[a]))
            local[-1].start()
            for k, (px, py) in enumerate(peers):
                sends.append(pltpu.make_async_remote_copy(
                    src_ref=i_ref.at[2 * px + py], dst_ref=o_ref.at[me], send_sem=send_sems.at[3 * a + k],
                    recv_sem=recv_sems.at[3 * a + k], device_id=(px, py, c), device_id_type=MESH))
                sends[-1].start()
        for a, (i_ref, o_ref) in enumerate(zip(in_refs, out_refs)):
            for k, (px, py) in enumerate(peers):
                pltpu.make_async_remote_copy(
                    src_ref=i_ref.at[me], dst_ref=o_ref.at[2 * px + py], send_sem=send_sems.at[3 * a + k],
                    recv_sem=recv_sems.at[3 * a + k], device_id=(px, py, c), device_id_type=MESH).wait_recv()
        for cp in sends:
            cp.wait_send()
        for cp in local:
            cp.wait()

    return pl.pallas_call(
        body, out_shape=[jax.ShapeDtypeStruct(a.shape, a.dtype) for a in arrays], in_specs=[_HBM] * n, out_specs=[_HBM] * n,
        scratch_shapes=[pltpu.SemaphoreType.DMA((3 * n,)), pltpu.SemaphoreType.DMA((3 * n,)), pltpu.SemaphoreType.DMA((n,))],
        name="chip_all_to_all")(*arrays)


def chip_all_gather(shards):
    n = len(shards)

    def body(*refs):
        in_refs, out_refs = refs[:n], refs[n:2 * n]
        send_sems, recv_sems, local_sems = refs[2 * n:]
        x, y, c = lax.axis_index("x"), lax.axis_index("y"), lax.axis_index("c")
        me = 2 * x + y
        peers = _chip_peers(x, y)
        local, sends = [], []
        for a, (i_ref, o_ref) in enumerate(zip(in_refs, out_refs)):
            local.append(pltpu.make_async_copy(i_ref, o_ref.at[me], local_sems.at[a]))
            local[-1].start()
            for k, (px, py) in enumerate(peers):
                sends.append(pltpu.make_async_remote_copy(
                    src_ref=i_ref.at[c], dst_ref=o_ref.at[me, c], send_sem=send_sems.at[6 * a + k],
                    recv_sem=recv_sems.at[6 * a + k], device_id=(px, py, c), device_id_type=MESH))
                sends[-1].start()
        for a, (i_ref, o_ref) in enumerate(zip(in_refs, out_refs)):
            for k, (px, py) in enumerate(peers):
                landed = o_ref.at[2 * px + py, c]
                pltpu.make_async_remote_copy(src_ref=i_ref.at[c], dst_ref=landed, send_sem=send_sems.at[6 * a + k],
                                             recv_sem=recv_sems.at[6 * a + k], device_id=(px, py, c), device_id_type=MESH).wait_recv()
                sends.append(pltpu.make_async_remote_copy(
                    src_ref=landed, dst_ref=landed, send_sem=send_sems.at[6 * a + 3 + k], recv_sem=recv_sems.at[6 * a + 3 + k],
                    device_id=(x, y, 1 - c), device_id_type=MESH))
                sends[-1].start()
        for a, (i_ref, o_ref) in enumerate(zip(in_refs, out_refs)):
            for k, (px, py) in enumerate(peers):
                other = o_ref.at[2 * px + py, 1 - c]
                pltpu.make_async_remote_copy(src_ref=other, dst_ref=other, send_sem=send_sems.at[6 * a + 3 + k],
                                             recv_sem=recv_sems.at[6 * a + 3 + k], device_id=(x, y, 1 - c),
                                             device_id_type=MESH).wait_recv()
        for cp in sends:
            cp.wait_send()
        for cp in local:
            cp.wait()

    return pl.pallas_call(
        body, out_shape=[jax.ShapeDtypeStruct((4,) + a.shape, a.dtype) for a in shards], in_specs=[_HBM] * n, out_specs=[_HBM] * n,
        scratch_shapes=[pltpu.SemaphoreType.DMA((6 * n,)), pltpu.SemaphoreType.DMA((6 * n,)), pltpu.SemaphoreType.DMA((n,))],
        name="chip_all_gather")(*shards)


def sibling_swap(arrays):
    n = len(arrays)

    def body(*refs):
        in_refs, out_refs = refs[:n], refs[n:2 * n]
        send_sems, recv_sems = refs[2 * n:]
        x, y, c = lax.axis_index("x"), lax.axis_index("y"), lax.axis_index("c")
        cps = [pltpu.make_async_remote_copy(src_ref=i_ref, dst_ref=o_ref, send_sem=send_sems.at[a], recv_sem=recv_sems.at[a],
                                            device_id=(x, y, 1 - c), device_id_type=MESH)
               for a, (i_ref, o_ref) in enumerate(zip(in_refs, out_refs))]
        for cp in cps:
            cp.start()
        for cp in cps:
            cp.wait()

    return pl.pallas_call(
        body, out_shape=[jax.ShapeDtypeStruct(a.shape, a.dtype) for a in arrays], in_specs=[_HBM] * n, out_specs=[_HBM] * n,
        scratch_shapes=[pltpu.SemaphoreType.DMA((n,)), pltpu.SemaphoreType.DMA((n,))], name="sibling_swap")(*arrays)


def sibling_all_gather(halves):
    n = len(halves)

    def body(*refs):
        in_refs, out_refs = refs[:n], refs[n:2 * n]
        send_sems, recv_sems, local_sems = refs[2 * n:]
        x, y, c = lax.axis_index("x"), lax.axis_index("y"), lax.axis_index("c")
        local = [pltpu.make_async_copy(i_ref, o_ref.at[c], local_sems.at[a])
                 for a, (i_ref, o_ref) in enumerate(zip(in_refs, out_refs))]
        sends = [pltpu.make_async_remote_copy(src_ref=i_ref, dst_ref=o_ref.at[c], send_sem=send_sems.at[a], recv_sem=recv_sems.at[a],
                                              device_id=(x, y, 1 - c), device_id_type=MESH)
                 for a, (i_ref, o_ref) in enumerate(zip(in_refs, out_refs))]
        for cp in local + sends:
            cp.start()
        for a, (i_ref, o_ref) in enumerate(zip(in_refs, out_refs)):
            pltpu.make_async_remote_copy(src_ref=i_ref, dst_ref=o_ref.at[1 - c], send_sem=send_sems.at[a], recv_sem=recv_sems.at[a],
                                         device_id=(x, y, 1 - c), device_id_type=MESH).wait_recv()
        for cp in sends:
            cp.wait_send()
        for cp in local:
            cp.wait()

    return pl.pallas_call(
        body, out_shape=[jax.ShapeDtypeStruct((2,) + a.shape, a.dtype) for a in halves], in_specs=[_HBM] * n, out_specs=[_HBM] * n,
        scratch_shapes=[pltpu.SemaphoreType.DMA((n,)), pltpu.SemaphoreType.DMA((n,)), pltpu.SemaphoreType.DMA((n,))],
        name="sibling_all_gather")(*halves)


def all_gather8(v, name):
    def body(v_ref, out_ref, send_sems, recv_sems):
        x, y, c = lax.axis_index("x"), lax.axis_index("y"), lax.axis_index("c")
        out_ref[4 * x + 2 * y + c] = v_ref[...]

        def peer(k):
            return (x ^ (k >> 2), y ^ ((k >> 1) & 1), c ^ (k & 1))

        sends = [pltpu.make_async_remote_copy(src_ref=v_ref, dst_ref=out_ref.at[4 * x + 2 * y + c], send_sem=send_sems.at[k - 1],
                                              recv_sem=recv_sems.at[k - 1], device_id=peer(k), device_id_type=MESH)
                 for k in range(1, 8)]
        for cp in sends:
            cp.start()
        for k in range(1, 8):
            px, py, pc = peer(k)
            pltpu.make_async_remote_copy(src_ref=v_ref, dst_ref=out_ref.at[4 * px + 2 * py + pc], send_sem=send_sems.at[k - 1],
                                         recv_sem=recv_sems.at[k - 1], device_id=peer(k), device_id_type=MESH).wait_recv()
        for cp in sends:
            cp.wait_send()

    return pl.pallas_call(
        body, out_shape=jax.ShapeDtypeStruct((8,) + v.shape, v.dtype), in_specs=[_VMEM], out_specs=_VMEM,
        scratch_shapes=[pltpu.SemaphoreType.DMA((7,)), pltpu.SemaphoreType.DMA((7,))], name=name)(v)


def pair_add(a, b, row_tile, name):
    rows, width = a.shape

    def body(a_ref, b_ref, o_ref):
        o_ref[...] = (a_ref[...].astype(F32) + b_ref[...].astype(F32)).astype(o_ref.dtype)

    spec = pl.BlockSpec((row_tile, width), lambda i: (i, 0))
    return pl.pallas_call(body, out_shape=jax.ShapeDtypeStruct(a.shape, a.dtype), grid=(rows // row_tile,),
                          in_specs=[spec, spec], out_specs=spec, compiler_params=_cparams("parallel"), name=name)(a, b)


def sum_blocks(blocks, row_tile, name):
    n, rows, width = blocks.shape

    def body(b_ref, o_ref):
        acc = b_ref[0].astype(F32)
        for j in range(1, n):
            acc = acc + b_ref[j].astype(F32)
        o_ref[...] = acc

    return pl.pallas_call(
        body, out_shape=jax.ShapeDtypeStruct((rows, width), F32), grid=(rows // row_tile,),
        in_specs=[pl.BlockSpec((n, row_tile, width), lambda i: (0, i, 0))],
        out_specs=pl.BlockSpec((row_tile, width), lambda i: (i, 0)), compiler_params=_cparams("parallel"), name=name)(blocks)


def adamw(g_parts, w, m, v, row_tile, name):
    npart = len(g_parts)

    def body(*refs):
        g = refs[0][...]
        for r in refs[1:npart]:
            g = g + r[...]
        w_ref, m_ref, v_ref, g_out, d_out, m_out, v_out = refs[npart:]
        m_new = ADAM_B1 * m_ref[...] + (1.0 - ADAM_B1) * g
        v_new = ADAM_B2 * v_ref[...] + (1.0 - ADAM_B2) * (g * g)
        m_hat = m_new / (1.0 - ADAM_B1 ** ADAM_STEP)
        v_hat = v_new / (1.0 - ADAM_B2 ** ADAM_STEP)
        g_out[...] = g
        d_out[...] = -ADAM_LR * (m_hat / (jnp.sqrt(v_hat) + ADAM_EPS) + ADAM_WD * w_ref[...])
        m_out[...] = m_new
        v_out[...] = v_new

    rows, width = w.shape
    spec = pl.BlockSpec((row_tile, width), lambda i: (i, 0))
    return pl.pallas_call(
        body, out_shape=[jax.ShapeDtypeStruct(w.shape, F32)] * 4, grid=(rows // row_tile,),
        in_specs=[spec] * (npart + 3), out_specs=[spec] * 4, compiler_params=_cparams("parallel"), name=name)(*g_parts, w, m, v)


def kernel(x, positions, norm_g, w_in, mla_q_a_norm_g, mla_w_q_b, mla_kv_a_norm_g, mla_w_kv_b, mla_q_norm_g, mla_k_norm_g, dn_conv_w, dn_a_log, dn_dt_bias, dn_out_norm_g, dil_q_norm_g, dil_k_norm_g, w_branch, w_out, loss_target, m_norm_g, m_w_in, m_mla_q_a_norm_g, m_mla_w_q_b, m_mla_kv_a_norm_g, m_mla_w_kv_b, m_mla_q_norm_g, m_mla_k_norm_g, m_dn_conv_w, m_dn_a_log, m_dn_dt_bias, m_dn_out_norm_g, m_dil_q_norm_g, m_dil_k_norm_g, m_w_branch, m_w_out, v_norm_g, v_w_in, v_mla_q_a_norm_g, v_mla_w_q_b, v_mla_kv_a_norm_g, v_mla_w_kv_b, v_mla_q_norm_g, v_mla_k_norm_g, v_dn_conv_w, v_dn_a_log, v_dn_dt_bias, v_dn_out_norm_g, v_dil_q_norm_g, v_dil_k_norm_g, v_w_branch, v_w_out):
    w = dict(norm_g=norm_g, w_in=w_in, mla_q_a_norm_g=mla_q_a_norm_g, mla_w_q_b=mla_w_q_b, mla_kv_a_norm_g=mla_kv_a_norm_g,
             mla_w_kv_b=mla_w_kv_b, mla_q_norm_g=mla_q_norm_g, mla_k_norm_g=mla_k_norm_g, dn_conv_w=dn_conv_w, dn_a_log=dn_a_log,
             dn_dt_bias=dn_dt_bias, dn_out_norm_g=dn_out_norm_g, dil_q_norm_g=dil_q_norm_g, dil_k_norm_g=dil_k_norm_g,
             w_branch=w_branch, w_out=w_out)
    m = dict(norm_g=m_norm_g, w_in=m_w_in, mla_q_a_norm_g=m_mla_q_a_norm_g, mla_w_q_b=m_mla_w_q_b, mla_kv_a_norm_g=m_mla_kv_a_norm_g,
             mla_w_kv_b=m_mla_w_kv_b, mla_q_norm_g=m_mla_q_norm_g, mla_k_norm_g=m_mla_k_norm_g, dn_conv_w=m_dn_conv_w,
             dn_a_log=m_dn_a_log, dn_dt_bias=m_dn_dt_bias, dn_out_norm_g=m_dn_out_norm_g, dil_q_norm_g=m_dil_q_norm_g,
             dil_k_norm_g=m_dil_k_norm_g, w_branch=m_w_branch, w_out=m_w_out)
    v = dict(norm_g=v_norm_g, w_in=v_w_in, mla_q_a_norm_g=v_mla_q_a_norm_g, mla_w_q_b=v_mla_w_q_b, mla_kv_a_norm_g=v_mla_kv_a_norm_g,
             mla_w_kv_b=v_mla_w_kv_b, mla_q_norm_g=v_mla_q_norm_g, mla_k_norm_g=v_mla_k_norm_g, dn_conv_w=v_dn_conv_w,
             dn_a_log=v_dn_a_log, dn_dt_bias=v_dn_dt_bias, dn_out_norm_g=v_dn_out_norm_g, dil_q_norm_g=v_dil_q_norm_g,
             dil_k_norm_g=v_dil_k_norm_g, w_branch=v_w_branch, w_out=v_w_out)
    chip = 2 * lax.axis_index("x") + lax.axis_index("y")
    mat_names = [n for n, _ in MATS]
    mat_shapes = tuple(w[n].shape for n in mat_names)
    conv_shard = dn_conv_w.shape
    win_shape = w_in.shape
    win_rows = (win_shape[0] * win_shape[1], win_shape[2])

    mats_sh = _pack([w[n] for n in mat_names], BF16, 2 * MAT_ROWS)
    mat_rows = mats_sh.shape[0]
    w_in4, mats4 = chip_all_gather([w_in.astype(BF16), mats_sh.reshape(2, mat_rows // 2, LANE)])
    mats4 = mats4.reshape(N_CHIPS, mat_rows, LANE)
    conv8 = all_gather8(_pack([dn_conv_w], F32, 8), "gather_conv_w")
    conv_full = jnp.concatenate([_unpack(conv8[2 * j], (conv_shard,))[0] for j in range(4)], axis=2)
    small = {n: w[n] for n in SMALL}
    tabs = _tables(positions[0])

    def loss_fn(w_in4, mats4, conv_full, small, xs):
        return _local_loss(w_in4, _full_from_chips(mats4, mat_shapes), conv_full, small, xs, loss_target[0], tabs)

    loss, (g_win4, g_mats4, g_conv, g_small, g_x) = jax.value_and_grad(loss_fn, argnums=(0, 1, 2, 3, 4))(
        w_in4, mats4, conv_full, small, x[0])
    loss = lax.psum(loss, ("x", "y", "c"))

    c_idx = lax.axis_index("c")
    half_win = (N_CHIPS * win_shape[1], win_shape[2])
    half_mats = (N_CHIPS * (mat_rows // 2), LANE)
    g_mats4 = g_mats4.reshape(N_CHIPS, 2, mat_rows // 2, LANE)
    pick = lambda g, i, shape: lax.dynamic_index_in_dim(g, i, axis=1, keepdims=False).reshape(shape)
    from_sib = sibling_swap([pick(g_win4, 1 - c_idx, half_win), pick(g_mats4, 1 - c_idx, half_mats)])
    s_win = pair_add(pick(g_win4, c_idx, half_win), from_sib[0], WIN_ROWS, "pair_add_w_in")
    s_mats = pair_add(pick(g_mats4, c_idx, half_mats), from_sib[1], MAT_ROWS, "pair_add_mats")
    r_win, r_mats = chip_all_to_all([s_win.reshape(N_CHIPS, win_shape[1], win_shape[2]),
                                     s_mats.reshape(N_CHIPS, mat_rows // 2, LANE)])
    g_win, g_mats = sibling_all_gather([sum_blocks(r_win, WIN_ROWS, "sum_chip_pieces_w_in"),
                                        sum_blocks(r_mats, MAT_ROWS, "sum_chip_pieces_mats")])
    res_win = adamw([g_win.reshape(win_rows)], *[d['w_in'].reshape(win_rows) for d in (w, m, v)], WIN_ROWS, "adamw_w_in")
    packed = [_pack([d[n] for n in mat_names], F32, 2 * MAT_ROWS) for d in (w, m, v)]
    res_big = [dict(zip(mat_names, _unpack(r, mat_shapes)))
               for r in adamw([g_mats.reshape(mat_rows, LANE)], *packed, MAT_ROWS, "adamw_mats")]
    for k in range(4):
        res_big[k]['w_in'] = res_win[k].reshape(win_shape)

    small_shapes = tuple(w[n].shape for n in SMALL) + (g_conv.shape,)
    g_all = sum_blocks(all_gather8(_pack([g_small[n] for n in SMALL] + [g_conv], F32, 8), "gather_small_grads"), 8, "sum_small")
    g_list = list(_unpack(g_all, small_shapes))
    g_list[-1] = lax.dynamic_slice_in_dim(g_list[-1], chip * conv_shard[2], conv_shard[2], axis=2)
    small_names = list(SMALL) + ['dn_conv_w']
    packed_s = [_pack([d[n] for n in small_names], F32, 8) for d in (w, m, v)]
    shapes_s = tuple(w[n].shape for n in small_names)
    res_small = [dict(zip(small_names, _unpack(r, shapes_s))) for r in adamw([_pack(g_list, F32, 8)], *packed_s, 8, "adamw_small")]

    outs = [loss, g_x[None]]
    for k in range(4):
        outs += [res_big[k][n] if n in res_big[k] else res_small[k][n] for n in WEIGHTS]
    return tuple(outs)
```

```python
import functools
import math

import jax
import jax.numpy as jnp
from jax import lax
from jax.experimental import pallas as pl
from jax.experimental.pallas import tpu as pltpu

F32 = jnp.float32
BF16 = jnp.bfloat16
HI = lax.Precision.HIGHEST
MESH = pl.DeviceIdType.MESH

LANE = 128
VMEM_LIMIT = 48 * 1024 * 1024
ROW_BLOCK = 256
MAT_ROWS = 512
WIN_ROWS = 128

RMS_EPS = 1e-6
ROPE_THETA = 10000.0
D_MODEL = 1024
DEPTH = 2
MLA_HEADS = 4
MLA_NOPE = 128
MLA_ROPE = 64
MLA_QK = MLA_NOPE + MLA_ROPE
MLA_Q_RANK = 384
MLA_KV_RANK = 256
DN_HEADS = 4
DN_DIM = 128
DN_CONV = 4
GDN_CHUNK = 128
DIL_WINDOWS = (128, 512, 2048)
DIL_DILATIONS = (1, 4, 16)
DIL_GROUPS = 3
DIL_BLOCK = 128
HEAD = 128
BRANCH_W = 512

ADAM_LR = 0.001
ADAM_B1 = 0.9
ADAM_B2 = 0.999
ADAM_EPS = 1e-08
ADAM_WD = 0.01
ADAM_STEP = 10

WEIGHTS = ['norm_g', 'w_in', 'mla_q_a_norm_g', 'mla_w_q_b', 'mla_kv_a_norm_g', 'mla_w_kv_b', 'mla_q_norm_g',
           'mla_k_norm_g', 'dn_conv_w', 'dn_a_log', 'dn_dt_bias', 'dn_out_norm_g', 'dil_q_norm_g', 'dil_k_norm_g',
           'w_branch', 'w_out']
MATS = (('mla_w_q_b', 2), ('mla_w_kv_b', 2), ('w_branch', 3), ('w_out', 1))
SMALL = ('norm_g', 'mla_q_a_norm_g', 'mla_kv_a_norm_g', 'mla_q_norm_g', 'mla_k_norm_g', 'dn_a_log', 'dn_dt_bias',
         'dn_out_norm_g', 'dil_q_norm_g', 'dil_k_norm_g')


def _cparams(*sem):
    return pltpu.CompilerParams(dimension_semantics=sem or None, vmem_limit_bytes=VMEM_LIMIT)


def _tile(dim, target):
    best = 0
    for t in range(LANE, min(dim, target) + 1, LANE):
        if dim % t == 0:
            best = t
    assert best, (dim, target)
    return best


def _mm(a, b, mode, out_dtype, res=None):
    if mode == 'nn':
        (m, k), (k2, n) = a.shape, b.shape
    elif mode == 'nt':
        (m, k), (n, k2) = a.shape, b.shape
    else:
        (k, m), (k2, n) = a.shape, b.shape
    assert k == k2, (a.shape, b.shape, mode)
    tm, tn, tk = _tile(m, 512), _tile(n, 1024), _tile(k, 1024)
    nk = k // tk
    dims = {'nn': (((1,), (0,)), ((), ())), 'nt': (((1,), (1,)), ((), ())), 'tn': (((0,), (0,)), ((), ()))}[mode]

    def body(*refs):
        if res is None:
            a_ref, b_ref, o_ref, acc = refs
        else:
            a_ref, b_ref, r_ref, o_ref, acc = refs
        kk = pl.program_id(2)

        @pl.when(kk == 0)
        def _():
            acc[...] = jnp.zeros_like(acc)

        acc[...] += lax.dot_general(a_ref[...].astype(BF16), b_ref[...].astype(BF16), dims,
                                    preferred_element_type=F32)

        @pl.when(kk == nk - 1)
        def _():
            r = acc[...]
            if res is not None:
                r = r + r_ref[...].astype(F32)
            o_ref[...] = r.astype(o_ref.dtype)

    a_spec = pl.BlockSpec((tk, tm), lambda i, j, kk: (kk, i)) if mode == 'tn' else pl.BlockSpec((tm, tk), lambda i, j, kk: (i, kk))
    b_spec = pl.BlockSpec((tn, tk), lambda i, j, kk: (j, kk)) if mode == 'nt' else pl.BlockSpec((tk, tn), lambda i, j, kk: (kk, j))
    o_spec = pl.BlockSpec((tm, tn), lambda i, j, kk: (i, j))
    in_specs = [a_spec, b_spec] + ([o_spec] if res is not None else [])
    args = (a, b) + ((res,) if res is not None else ())
    return pl.pallas_call(
        body, out_shape=jax.ShapeDtypeStruct((m, n), out_dtype), grid=(m // tm, n // tn, nk),
        in_specs=in_specs, out_specs=o_spec, scratch_shapes=[pltpu.VMEM((tm, tn), F32)],
        compiler_params=_cparams("parallel", "parallel", "arbitrary"),
        name=f"mm_{mode}_{m}x{k}x{n}" + ("_res" if res is not None else ""))(*args)


def _make_matmul(out_dtype, with_res):
    @jax.custom_vjp
    def mm(a, b, *r):
        return _mm(a, b, 'nn', out_dtype, *r)

    def fwd(a, b, *r):
        return mm(a, b, *r), (a, b)

    def bwd(saved, g):
        a, b = saved
        da = _mm(g, b, 'nt', a.dtype)
        db = _mm(a, g, 'tn', b.dtype)
        return (da, db) + ((g,) if with_res else ())

    mm.defvjp(fwd, bwd)
    return mm


def matmul(a, b, out_dtype=F32, res=None):
    if res is None:
        return _make_matmul(out_dtype, False)(a, b)
    return _make_matmul(out_dtype, True)(a, b, res)


def _chunks(ref):
    return [ref[:, c * LANE:(c + 1) * LANE].astype(F32) for c in range(ref.shape[-1] // LANE)]


def _store(ref, chunks):
    for c, ch in enumerate(chunks):
        ref[:, c * LANE:(c + 1) * LANE] = ch.astype(ref.dtype)


def _row_spec(width, br):
    return pl.BlockSpec((br, width), lambda i: (i, 0))


def _par_spec(width):
    return pl.BlockSpec((1, width), lambda i: (0, 0))


def _rw_fwd(f, name, rows, params, outs, br):
    s = rows[0].shape[0]
    br = min(br, s)
    nr, npar = len(rows), len(params)

    def body(*refs):
        rc = [_chunks(r) for r in refs[:nr]]
        pc = [_chunks(p) for p in refs[nr:nr + npar]]
        res = f(rc, pc)
        for o_ref, chs in zip(refs[nr + npar:], res):
            _store(o_ref, chs)

    return pl.pallas_call(
        body, out_shape=[jax.ShapeDtypeStruct((s, w), dt) for w, dt in outs], grid=(s // br,),
        in_specs=[_row_spec(r.shape[1], br) for r in rows] + [_par_spec(p.shape[1]) for p in params],
        out_specs=[_row_spec(w, br) for w, _ in outs],
        compiler_params=_cparams("parallel"), name=name + "_fwd")(*rows, *params)


def _rw_bwd(f, name, rows, params, cts, nograd, br):
    s = rows[0].shape[0]
    br = min(br, s)
    nr, npar, nct = len(rows), len(params), len(cts)
    grad_rows = [i for i in range(nr) if i not in nograd]

    def body(*refs):
        rc = [_chunks(r) for r in refs[:nr]]
        pc = [_chunks(p) for p in refs[nr:nr + npar]]
        ct = [_chunks(c) for c in refs[nr + npar:nr + npar + nct]]
        out_refs = refs[nr + npar + nct:]
        _, vjp = jax.vjp(f, rc, pc)
        drc, dpc = vjp(ct)
        for o_ref, i in zip(out_refs[:len(grad_rows)], grad_rows):
            _store(o_ref, drc[i])
        i0 = pl.program_id(0)
        for o_ref, chs in zip(out_refs[len(grad_rows):], dpc):
            @pl.when(i0 == 0)
            def _(o_ref=o_ref):
                o_ref[...] = jnp.zeros_like(o_ref)
            for c, ch in enumerate(chs):
                o_ref[:, c * LANE:(c + 1) * LANE] += ch

    out_shape = ([jax.ShapeDtypeStruct(rows[i].shape, rows[i].dtype) for i in grad_rows]
                 + [jax.ShapeDtypeStruct(p.shape, F32) for p in params])
    out_specs = [_row_spec(rows[i].shape[1], br) for i in grad_rows] + [_par_spec(p.shape[1]) for p in params]
    res = pl.pallas_call(
        body, out_shape=out_shape, grid=(s // br,),
        in_specs=([_row_spec(r.shape[1], br) for r in rows] + [_par_spec(p.shape[1]) for p in params]
                  + [_row_spec(c.shape[1], br) for c in cts]),
        out_specs=out_specs, compiler_params=_cparams("arbitrary"), name=name + "_bwd")(*rows, *params, *cts)
    drows = [None] * nr
    for o, i in zip(res[:len(grad_rows)], grad_rows):
        drows[i] = o
    for i in nograd:
        drows[i] = jnp.zeros_like(rows[i])
    return tuple(drows), tuple(res[len(grad_rows):])


def rowwise(f, name, rows, params, outs, nograd=(), br=ROW_BLOCK):
    @jax.custom_vjp
    def op(rows, params):
        return tuple(_rw_fwd(f, name, rows, params, outs, br))

    def fwd(rows, params):
        return op(rows, params), (rows, params)

    def bwd(saved, cts):
        rows, params = saved
        return _rw_bwd(f, name, rows, params, list(cts), nograd, br)

    op.defvjp(fwd, bwd)
    return op(tuple(rows), tuple(params))


def _lane_roll(x, s):
    w = x.shape[-1]

    @jax.custom_vjp
    def r(v):
        return pltpu.roll(v, s, 1)

    r.defvjp(lambda v: (r(v), None), lambda _, g: (pltpu.roll(g, (w - s) % w, 1),))
    return r(x)


def _lanes(x):
    return lax.broadcasted_iota(jnp.int32, x.shape, 1)


def _rms(x, g, n=LANE):
    return x * lax.rsqrt(jnp.sum(x * x, axis=-1, keepdims=True) / n + RMS_EPS) * g


def _rope128(x, cos, sin):
    lane = _lanes(x)
    rot = _lane_roll(x, 64) * jnp.where(lane < 64, -1.0, 1.0)
    return x * cos + rot * sin


def _rope64(x, cos, sin):
    lane = _lanes(x)
    rot = jnp.where(lane < 32, -_lane_roll(x, 96), jnp.where(lane < 64, _lane_roll(x, 32), 0.0))
    return x * cos + rot * sin


def _silu(z):
    return z * jax.nn.sigmoid(z)


def _softplus(x):
    return jnp.maximum(x, 0.0) + jnp.log1p(jnp.exp(-jnp.abs(x)))


def f_rms_full(rc, pc):
    x, g = rc[0], pc[0]
    width = len(x) * LANE
    ms = sum(jnp.sum(c * c, axis=-1, keepdims=True) for c in x) / width
    r = lax.rsqrt(ms + RMS_EPS)
    return [[c * r * gc for c, gc in zip(x, g)]]


def f_mla1(rc, pc):
    x = rc[0]
    qn = f_rms_full([x[0:3]], [pc[0]])[0]
    kvn = f_rms_full([x[3:5]], [pc[1]])[0]
    return [qn, kvn, [x[5]]]


def f_mla2(rc, pc):
    q, kv, kpe, cos, sin = rc[0], rc[1], rc[2][0], rc[3][0], rc[4][0]
    gqn, gqp, gkn, gkp = pc[0][0], pc[1][0], pc[2][0], pc[3][0]
    k_pe = _rope64(_rms(kpe, gkp, MLA_ROPE), cos, sin)
    q_att, k_att, v = [], [], []
    for h in range(MLA_HEADS):
        q_att += [_rms(q[h], gqn), _rope64(_rms(q[MLA_HEADS + h], gqp, MLA_ROPE), cos, sin)]
        k_att += [_rms(kv[2 * h], gkn), k_pe]
        v.append(kv[2 * h + 1])
    return [q_att, k_att, v]


def f_gates(rc, pc):
    x, a_log, dt_bias = rc[0][0], pc[0][0], pc[1][0]
    lane = _lanes(x)
    g = -jnp.exp(a_log) * _softplus(x + dt_bias)
    return [[jnp.where(lane < DN_HEADS, g, jnp.where(lane < 2 * DN_HEADS, jax.nn.sigmoid(x), 0.0))]]


def f_headnorm(rc, pc):
    return [[_rms(c, pc[0][0]) for c in rc[0]]]


def f_dil(rc, pc):
    x, cos, sin = rc[0], rc[1][0], rc[2][0]
    gq, gk = pc[0][0], pc[1][0]
    n = len(x) // 3
    q = [_rope128(_rms(c, gq), cos, sin) for c in x[:n]]
    k = [_rope128(_rms(c, gk), cos, sin) for c in x[n:2 * n]]
    return [q, k, list(x[2 * n:])]


def f_comb(rc, pc):
    o, l = rc[:DIL_GROUPS], rc[DIL_GROUPS:]
    out = []
    for c in range(len(o[0])):
        m = functools.reduce(jnp.maximum, [lg[c] for lg in l])
        e = [jnp.exp(lg[c] - m) for lg in l]
        den = sum(e)
        out.append(sum(eg * og[c] for eg, og in zip(e, o)) / den)
    return [out]


def f_merge1(rc, pc):
    z = rc[3]
    n = len(rc[0])
    return [[y * _silu(z[b * n + c]) for c, y in enumerate(rc[b])] for b in range(3)]


def f_merge2(rc, pc):
    gate = rc[0]
    n = len(rc[1])
    return [[sum(jax.nn.sigmoid(gate[b * n + c]) * rc[1 + b][c] for b in range(3)) for c in range(n)]]


MLA_DQK = 2 * LANE
MLA_SCALE = MLA_QK ** -0.5


def _mla_attn_fwd(q, k, v):
    s = q.shape[0]
    h = q.shape[1] // MLA_DQK
    t = min(512, s)
    n = s // t

    def body(q_ref, k_ref, v_ref, o_ref, lse_ref, m_sc, l_sc, acc_sc):
        qi, kj = pl.program_id(1), pl.program_id(2)

        @pl.when(kj == 0)
        def _():
            m_sc[...] = jnp.full_like(m_sc, -jnp.inf)
            l_sc[...] = jnp.zeros_like(l_sc)
            acc_sc[...] = jnp.zeros_like(acc_sc)

        nsub = 2 if t % 256 == 0 else 1
        ts = t // nsub
        rows = [slice(r * ts, (r + 1) * ts) for r in range(nsub)]

        def step(on_diagonal):
            sc = [lax.dot_general(q_ref[rw, :], k_ref[...], (((1,), (1,)), ((), ())), preferred_element_type=F32) * MLA_SCALE
                  for rw in rows]
            if on_diagonal:
                keep = [(lax.broadcasted_iota(jnp.int32, (ts, t), 1)
                         <= r * ts + lax.broadcasted_iota(jnp.int32, (ts, t), 0)) for r in range(nsub)]
                sc = [jnp.where(kp, x, -jnp.inf) for kp, x in zip(keep, sc)]
            m_old = [m_sc[rw, :] for rw in rows]
            m_new = [jnp.maximum(mo, jnp.max(x, axis=-1, keepdims=True)) for mo, x in zip(m_old, sc)]
            alpha = [jnp.exp(mo - mn) for mo, mn in zip(m_old, m_new)]
            p = [jnp.exp(x - mn) for x, mn in zip(sc, m_new)]
            pv = [jnp.dot(x.astype(BF16), v_ref[...], preferred_element_type=F32) for x in p]
            for r, rw in enumerate(rows):
                l_sc[rw, :] = alpha[r] * l_sc[rw, :] + jnp.sum(p[r], axis=-1, keepdims=True)
                acc_sc[rw, :] = alpha[r] * acc_sc[rw, :] + pv[r]
                m_sc[rw, :] = m_new[r]

        @pl.when(kj < qi)
        def _():
            step(False)

        @pl.when(kj == qi)
        def _():
            step(True)

        @pl.when(kj == n - 1)
        def _():
            o_ref[...] = acc_sc[...] / l_sc[...]
            lse_ref[...] = jnp.broadcast_to(m_sc[...] + jnp.log(l_sc[...]), lse_ref.shape)

    return pl.pallas_call(
        body, out_shape=[jax.ShapeDtypeStruct((s, h * HEAD), F32)] * 2, grid=(h, n, n),
        in_specs=[pl.BlockSpec((t, MLA_DQK), lambda hh, i, j: (i, hh)),
                  pl.BlockSpec((t, MLA_DQK), lambda hh, i, j: (jnp.minimum(j, i), hh)),
                  pl.BlockSpec((t, HEAD), lambda hh, i, j: (jnp.minimum(j, i), hh))],
        out_specs=[pl.BlockSpec((t, HEAD), lambda hh, i, j: (i, hh))] * 2,
        scratch_shapes=[pltpu.VMEM((t, 1), F32), pltpu.VMEM((t, 1), F32), pltpu.VMEM((t, HEAD), F32)],
        compiler_params=_cparams("parallel", "parallel", "arbitrary"), name="mla_attn_fwd")(q, k, v)


def _mla_attn_bwd(q, k, v, o, lse, do):
    s = q.shape[0]
    h = q.shape[1] // MLA_DQK
    t = min(512, s)
    n = s // t
    nt, tn = (((1,), (1,)), ((), ())), (((0,), (0,)), ((), ()))

    def body(q_ref, k_ref, v_ref, o_ref, lse_ref, do_ref, dq_ref, dk_ref, dv_ref, dq_sc, dk_sc, dv_sc):
        kj, qi = pl.program_id(1), pl.program_id(2)

        @pl.when(jnp.logical_and(kj == 0, qi == 0))
        def _():
            dq_sc[...] = jnp.zeros_like(dq_sc)

        @pl.when(qi == 0)
        def _():
            dk_sc[...] = jnp.zeros_like(dk_sc)
            dv_sc[...] = jnp.zeros_like(dv_sc)

        def pair(on_diagonal):
            sc = lax.dot_general(q_ref[...], k_ref[...], nt, preferred_element_type=F32) * MLA_SCALE
            p = jnp.exp(sc - lse_ref[:, 0:1])
            if on_diagonal:
                p = jnp.where(lax.broadcasted_iota(jnp.int32, (t, t), 1) <= lax.broadcasted_iota(jnp.int32, (t, t), 0), p, 0.0)
            do_v = do_ref[...]
            do_b = do_v.astype(BF16)
            dp = lax.dot_general(do_b, v_ref[...], nt, preferred_element_type=F32)
            delta = jnp.sum(do_v * o_ref[...], axis=-1, keepdims=True)
            ds = (p * (dp - delta) * MLA_SCALE).astype(BF16)
            dv_sc[...] += lax.dot_general(p.astype(BF16), do_b, tn, preferred_element_type=F32)
            dk_sc[...] += lax.dot_general(ds, q_ref[...], tn, preferred_element_type=F32)
            rows = pl.ds(pl.multiple_of(qi * t, t), t)
            dq_sc[rows, :] += jnp.dot(ds, k_ref[...], preferred_element_type=F32)

        @pl.when(qi > kj)
        def _():
            pair(False)

        @pl.when(qi == kj)
        def _():
            pair(True)

        @pl.when(qi == n - 1)
        def _():
            dk_ref[...] = dk_sc[...].astype(dk_ref.dtype)
            dv_ref[...] = dv_sc[...].astype(dv_ref.dtype)

        @pl.when(jnp.logical_and(kj == n - 1, qi == n - 1))
        def _():
            dq_ref[...] = dq_sc[...].astype(dq_ref.dtype)

    qmap = lambda hh, j, i: (jnp.maximum(i, j), hh)
    kmap = lambda hh, j, i: (j, hh)
    return pl.pallas_call(
        body, out_shape=[jax.ShapeDtypeStruct(q.shape, BF16), jax.ShapeDtypeStruct(k.shape, BF16), jax.ShapeDtypeStruct(v.shape, BF16)],
        grid=(h, n, n),
        in_specs=[pl.BlockSpec((t, MLA_DQK), qmap), pl.BlockSpec((t, MLA_DQK), kmap), pl.BlockSpec((t, HEAD), kmap),
                  pl.BlockSpec((t, HEAD), qmap), pl.BlockSpec((t, HEAD), qmap), pl.BlockSpec((t, HEAD), qmap)],
        out_specs=[pl.BlockSpec((s, MLA_DQK), lambda hh, j, i: (0, hh)), pl.BlockSpec((t, MLA_DQK), kmap),
                   pl.BlockSpec((t, HEAD), kmap)],
        scratch_shapes=[pltpu.VMEM((s, MLA_DQK), F32), pltpu.VMEM((t, MLA_DQK), F32), pltpu.VMEM((t, HEAD), F32)],
        compiler_params=_cparams("parallel", "arbitrary", "arbitrary"), name="mla_attn_bwd")(q, k, v, o, lse, do)


@jax.custom_vjp
def mla_attention(q, k, v):
    return _mla_attn_fwd(q, k, v)[0]


def _mla_attention_fwd(q, k, v):
    o, lse = _mla_attn_fwd(q, k, v)
    return o, (q, k, v, o, lse)


def _mla_attention_bwd(saved, do):
    return tuple(_mla_attn_bwd(*saved, do))


mla_attention.defvjp(_mla_attention_fwd, _mla_attention_bwd)


DIL_SCALE = HEAD ** -0.5
GROUP_W = 4 * HEAD


def _dil_scores(q, kp, kc, n):
    dn = (((1,), (1,)), ((), ()))
    sp = lax.dot_general(q, kp, dn, preferred_element_type=F32) * DIL_SCALE
    sc = lax.dot_general(q, kc, dn, preferred_element_type=F32) * DIL_SCALE
    qi = lax.broadcasted_iota(jnp.int32, sp.shape, 0)
    kc_i = lax.broadcasted_iota(jnp.int32, sp.shape, 1)
    vp = jnp.logical_and(kc_i >= qi, n > 0)
    vc = kc_i <= qi
    return sp, sc, vp, vc


def _dil_specs(d):
    cur = pl.BlockSpec((DIL_BLOCK, GROUP_W), lambda r, n: (n, r))
    prev = pl.BlockSpec((DIL_BLOCK, GROUP_W), lambda r, n: (jnp.maximum(n - 1, 0), r))
    return cur, prev


def _dil_fwd(q, k, v, d):
    l = q.shape[0]
    nb = l // DIL_BLOCK
    cur, prev = _dil_specs(d)

    def body(q_ref, kp_ref, kc_ref, vp_ref, vc_ref, o_ref, lse_ref):
        n = pl.program_id(1)
        heads = range(4)
        sl = [slice(h * HEAD, (h + 1) * HEAD) for h in heads]
        scores = [_dil_scores(q_ref[:, sl[h]], kp_ref[:, sl[h]], kc_ref[:, sl[h]], n) for h in heads]
        sp = [jnp.where(vp, s_p, -jnp.inf) for s_p, _, vp, _ in scores]
        sc = [jnp.where(vc, s_c, -jnp.inf) for _, s_c, _, vc in scores]
        m = [jnp.maximum(jnp.max(sp[h], axis=-1, keepdims=True), jnp.max(sc[h], axis=-1, keepdims=True)) for h in heads]
        ep = [jnp.exp(sp[h] - m[h]) for h in heads]
        ec = [jnp.exp(sc[h] - m[h]) for h in heads]
        den = [jnp.sum(ep[h], axis=-1, keepdims=True) + jnp.sum(ec[h], axis=-1, keepdims=True) for h in heads]
        acc = [jnp.dot(ep[h].astype(BF16), vp_ref[:, sl[h]], preferred_element_type=F32)
               + jnp.dot(ec[h].astype(BF16), vc_ref[:, sl[h]], preferred_element_type=F32) for h in heads]
        for h in heads:
            o_ref[:, sl[h]] = acc[h] / den[h]
            lse_ref[:, sl[h]] = jnp.broadcast_to(m[h] + jnp.log(den[h]), (DIL_BLOCK, HEAD))

    return pl.pallas_call(
        body, out_shape=[jax.ShapeDtypeStruct(q.shape, F32)] * 2, grid=(d, nb),
        in_specs=[cur, prev, cur, prev, cur], out_specs=[cur, cur],
        compiler_params=_cparams("parallel", "parallel"), name=f"dil_fwd_d{d}")(q, k, k, v, v)


def _dil_bwd(q, k, v, o, lse, do, dlse, d):
    l = q.shape[0]
    nb = l // DIL_BLOCK
    cur, prev = _dil_specs(d)
    tn = (((0,), (0,)), ((), ()))
    nt = (((1,), (1,)), ((), ()))

    def body(q_ref, kp_ref, kc_ref, vp_ref, vc_ref, o_ref, lse_ref, do_ref, dl_ref,
             dq_ref, dkp_ref, dkc_ref, dvp_ref, dvc_ref):
        n = pl.program_id(1)
        heads = range(4)
        sl = [slice(h * HEAD, (h + 1) * HEAD) for h in heads]
        scores = [_dil_scores(q_ref[:, sl[h]], kp_ref[:, sl[h]], kc_ref[:, sl[h]], n) for h in heads]
        lse = [lse_ref[:, h * HEAD:h * HEAD + 1] for h in heads]
        pp = [jnp.where(scores[h][2], jnp.exp(scores[h][0] - lse[h]), 0.0) for h in heads]
        pc = [jnp.where(scores[h][3], jnp.exp(scores[h][1] - lse[h]), 0.0) for h in heads]
        do_b = [do_ref[:, sl[h]].astype(BF16) for h in heads]
        corr = [jnp.sum(dl_ref[:, sl[h]], axis=-1, keepdims=True)
                - jnp.sum(do_ref[:, sl[h]] * o_ref[:, sl[h]], axis=-1, keepdims=True) for h in heads]
        dsp = [(pp[h] * (lax.dot_general(do_b[h], vp_ref[:, sl[h]], nt, preferred_element_type=F32) + corr[h])
                * DIL_SCALE).astype(BF16) for h in heads]
        dsc = [(pc[h] * (lax.dot_general(do_b[h], vc_ref[:, sl[h]], nt, preferred_element_type=F32) + corr[h])
                * DIL_SCALE).astype(BF16) for h in heads]
        for h in heads:
            dq_ref[:, sl[h]] = (jnp.dot(dsp[h], kp_ref[:, sl[h]], preferred_element_type=F32)
                                + jnp.dot(dsc[h], kc_ref[:, sl[h]], preferred_element_type=F32)).astype(dq_ref.dtype)
            dkp_ref[:, sl[h]] = lax.dot_general(dsp[h], q_ref[:, sl[h]], tn, preferred_element_type=F32)
            dkc_ref[:, sl[h]] = lax.dot_general(dsc[h], q_ref[:, sl[h]], tn, preferred_element_type=F32)
            dvp_ref[:, sl[h]] = lax.dot_general(pp[h].astype(BF16), do_b[h], tn, preferred_element_type=F32)
            dvc_ref[:, sl[h]] = lax.dot_general(pc[h].astype(BF16), do_b[h], tn, preferred_element_type=F32)

    dq, dkp, dkc, dvp, dvc = pl.pallas_call(
        body, out_shape=[jax.ShapeDtypeStruct(q.shape, BF16)] + [jax.ShapeDtypeStruct(q.shape, F32)] * 4, grid=(d, nb),
        in_specs=[cur, prev, cur, prev, cur, cur, cur, cur, cur], out_specs=[cur] * 5,
        compiler_params=_cparams("parallel", "parallel"), name=f"dil_bwd_d{d}")(q, k, k, v, v, o, lse, do, dlse)

    def fold(cur_part, prev_part):
        shifted = jnp.concatenate([prev_part[DIL_BLOCK:], jnp.zeros((DIL_BLOCK, prev_part.shape[1]), F32)], axis=0)
        return (cur_part + shifted).astype(BF16)

    return dq, fold(dkc, dkp), fold(dvc, dvp)


def _make_dil(d):
    @jax.custom_vjp
    def att(q, k, v):
        return tuple(_dil_fwd(q, k, v, d))

    def fwd(q, k, v):
        o, lse = _dil_fwd(q, k, v, d)
        return (o, lse), (q, k, v, o, lse)

    def bwd(saved, cts):
        return _dil_bwd(*saved, cts[0], cts[1], d)

    att.defvjp(fwd, bwd)
    return att


def dilated_group(q, k, v, d):
    s = q.shape[0]
    view = lambda t: t.reshape(s // d, d * GROUP_W)
    o, lse = _make_dil(d)(view(q), view(k), view(v))
    return o.reshape(s, GROUP_W), lse.reshape(s, GROUP_W)


def _dn_post(c, kind, scale):
    m = _silu(c)
    nrm = m * lax.rsqrt(jnp.sum(m * m, axis=-1, keepdims=True) + 1e-6) * scale
    return kind * nrm + (1.0 - kind) * m


def _dn_kind_scale(j):
    kind = jnp.where(j < 2 * DN_HEADS, 1.0, 0.0).astype(F32)
    scale = jnp.where(j < DN_HEADS, DN_DIM ** -0.5, 1.0).astype(F32)
    return kind, scale


_CONV_RB = 512
_CONV_PAD = 8


def _conv_windows(pad_ref, w_ref, r0, rb, sign):
    acc = None
    for sh in range(DN_CONV):
        win = pad_ref[pl.ds(r0 + _CONV_PAD * (sign < 0) + sign * sh, rb), :]
        term = w_ref[DN_CONV - 1 - sh:DN_CONV - sh, :] * win
        acc = term if acc is None else acc + term
    return acc


def _dn_conv_fwd(x, w):
    s, width = x.shape
    rb = min(_CONV_RB, s)

    def body(x_ref, w_ref, o_ref, pad_ref):
        kind, scale = _dn_kind_scale(pl.program_id(0))
        pad_ref[0:_CONV_PAD, :] = jnp.zeros((_CONV_PAD, LANE), F32)
        pad_ref[_CONV_PAD:, :] = x_ref[...]
        for r0 in range(0, s, rb):
            c = _conv_windows(pad_ref, w_ref, r0, rb, -1)
            o_ref[r0:r0 + rb, :] = _dn_post(c, kind, scale)

    return pl.pallas_call(
        body, out_shape=jax.ShapeDtypeStruct(x.shape, F32), grid=(width // LANE,),
        in_specs=[pl.BlockSpec((s, LANE), lambda j: (0, j)), pl.BlockSpec((DN_CONV, LANE), lambda j: (0, j))],
        out_specs=pl.BlockSpec((s, LANE), lambda j: (0, j)), scratch_shapes=[pltpu.VMEM((s + _CONV_PAD, LANE), F32)],
        compiler_params=_cparams("parallel"), name="dn_conv_fwd")(x, w)


def _dn_conv_bwd(x, w, dy):
    s, width = x.shape
    rb = min(_CONV_RB, s)

    def body(x_ref, w_ref, dy_ref, dx_ref, dw_ref, pad_ref, dpad_ref):
        kind, scale = _dn_kind_scale(pl.program_id(0))
        pad_ref[0:_CONV_PAD, :] = jnp.zeros((_CONV_PAD, LANE), F32)
        pad_ref[_CONV_PAD:, :] = x_ref[...]
        dpad_ref[s:, :] = jnp.zeros((_CONV_PAD, LANE), F32)
        dws = [jnp.zeros((1, LANE), F32) for _ in range(DN_CONV)]
        for r0 in range(0, s, rb):
            c = _conv_windows(pad_ref, w_ref, r0, rb, -1)
            _, vjp = jax.vjp(lambda cc: _dn_post(cc, kind, scale), c)
            dc = vjp(dy_ref[r0:r0 + rb, :])[0]
            dpad_ref[r0:r0 + rb, :] = dc
            for sh in range(DN_CONV):
                win = pad_ref[pl.ds(r0 + _CONV_PAD - sh, rb), :]
                dws[DN_CONV - 1 - sh] = dws[DN_CONV - 1 - sh] + jnp.sum(dc * win, axis=0, keepdims=True)
        for j in range(DN_CONV):
            dw_ref[j:j + 1, :] = dws[j]
        for r0 in range(0, s, rb):
            dx_ref[r0:r0 + rb, :] = _conv_windows(dpad_ref, w_ref, r0, rb, 1)

    return pl.pallas_call(
        body, out_shape=[jax.ShapeDtypeStruct(x.shape, F32), jax.ShapeDtypeStruct(w.shape, F32)], grid=(width // LANE,),
        in_specs=[pl.BlockSpec((s, LANE), lambda j: (0, j)), pl.BlockSpec((DN_CONV, LANE), lambda j: (0, j)),
                  pl.BlockSpec((s, LANE), lambda j: (0, j))],
        out_specs=[pl.BlockSpec((s, LANE), lambda j: (0, j)), pl.BlockSpec((DN_CONV, LANE), lambda j: (0, j))],
        scratch_shapes=[pltpu.VMEM((s + _CONV_PAD, LANE), F32), pltpu.VMEM((s + _CONV_PAD, LANE), F32)],
        compiler_params=_cparams("parallel"), name="dn_conv_bwd")(x, w, dy)


@jax.custom_vjp
def dn_conv(x, w):
    return _dn_conv_fwd(x, w)


dn_conv.defvjp(lambda x, w: (_dn_conv_fwd(x, w), (x, w)), lambda saved, dy: tuple(_dn_conv_bwd(*saved, dy)))


_NN = (((1,), (0,)), ((), ()))
_NT = (((1,), (1,)), ((), ()))
_TN = (((0,), (0,)), ((), ()))


def _bd(a, b, dims, passes=1):
    d = lambda x, y: lax.dot_general(x, y, dims, preferred_element_type=F32)
    if passes == 0:
        return lax.dot_general(a, b, dims, precision=lax.Precision.HIGHEST, preferred_element_type=F32)
    ah, bh = a.astype(BF16), b.astype(BF16)
    if passes == 1:
        return d(ah, bh)
    al, bl = (a - ah.astype(F32)).astype(BF16), (b - bh.astype(F32)).astype(BF16)
    return d(ah, bh) + d(ah, bl) + d(al, bh)


@functools.partial(jax.custom_vjp, nondiff_argnums=(2, 3))
def _pdot(a, b, dims, passes):
    return _bd(a, b, dims, passes)


def _pdot_bwd(dims, passes, saved, g):
    a, b = saved
    if dims == _NN:
        return _bd(g, b, _NT, passes), _bd(a, g, _TN, passes)
    if dims == _NT:
        return _bd(g, b, _NN, passes), _bd(g, a, _TN, passes)
    return _bd(b, g, _NT, passes), _bd(a, g, _NN, passes)


_pdot.defvjp(lambda a, b, dims, passes: (_bd(a, b, dims, passes), (a, b)), _pdot_bwd)


GDN_DOT_PASSES = 1
GDN_SOLVE_PASSES = 3


def _hdot(a, b, dims=_NN):
    return _pdot(a, b, dims, GDN_DOT_PASSES)


def _xdot(a, b, dims=_NN):
    return _pdot(a, b, dims, GDN_SOLVE_PASSES)


def _split3(x):
    hi = x.astype(BF16)
    r1 = x - hi.astype(F32)
    mid = r1.astype(BF16)
    lo = (r1 - mid.astype(F32)).astype(BF16)
    return hi, mid, lo


def _tri_dot(tri, x, dims):
    t = tri.astype(BF16)
    return sum(lax.dot_general(t, p, dims, preferred_element_type=F32) for p in _split3(x))


@jax.custom_vjp
def _cumsum_rows(x):
    c = x.shape[0]
    tri = lax.broadcasted_iota(jnp.int32, (c, c), 0) >= lax.broadcasted_iota(jnp.int32, (c, c), 1)
    return _tri_dot(tri, x, _NN)


def _cumsum_rows_bwd(_, g):
    c = g.shape[0]
    tri = lax.broadcasted_iota(jnp.int32, (c, c), 0) >= lax.broadcasted_iota(jnp.int32, (c, c), 1)
    return (_tri_dot(tri, g, _TN),)


_cumsum_rows.defvjp(lambda x: (_cumsum_rows(x), None), _cumsum_rows_bwd)


def _gdn_prep_fn(qkv, gb):
    c = GDN_CHUNK
    row = lax.broadcasted_iota(jnp.int32, (c, c), 0)
    col = lax.broadcasted_iota(jnp.int32, (c, c), 1)
    incl, strict = row >= col, row > col
    lane = _lanes(gb)
    heads = range(DN_HEADS)
    q, k, v = qkv[:DN_HEADS], qkv[DN_HEADS:2 * DN_HEADS], qkv[2 * DN_HEADS:]
    g = [jnp.sum(jnp.where(lane == h, gb, 0.0), axis=-1, keepdims=True) for h in heads]
    beta = [jnp.sum(jnp.where(lane == DN_HEADS + h, gb, 0.0), axis=-1, keepdims=True) for h in heads]
    gcb = [_cumsum_rows(jnp.broadcast_to(g[h], (c, c))) for h in heads]
    decay = [jnp.where(incl, jnp.exp(jnp.where(incl, gcb[h] - gcb[h].T, 0.0)), 0.0) for h in heads]
    kb = [k[h] * beta[h] for h in heads]
    a = [jnp.where(strict, _hdot(kb[h], k[h], _NT) * decay[h], 0.0) for h in heads]
    pw = [-a[h] for h in heads]
    t = [(row == col).astype(F32) + pw[h] for h in heads]
    for _ in range(int(math.log2(c)) - 1):
        pw = [_xdot(pw[h], pw[h]) for h in heads]
        t = [t[h] + _xdot(t[h], pw[h]) for h in heads]
    eg = [jnp.exp(gcb[h]) for h in heads]
    u = [_xdot(t[h], v[h] * beta[h]) for h in heads]
    w = [_xdot(t[h], kb[h] * eg[h]) for h in heads]
    qk = [jnp.where(incl, _hdot(q[h], k[h], _NT) * decay[h], 0.0) for h in heads]
    g_last = [jnp.sum(jnp.where(row == c - 1, gcb[h], 0.0), axis=0, keepdims=True) for h in heads]
    kg = [k[h] * jnp.exp(g_last[h] - gcb[h]) for h in heads]
    qg = [q[h] * eg[h] for h in heads]
    el = [jnp.broadcast_to(jnp.exp(g_last[h]), (8, LANE)) for h in heads]
    return [u, w, qg, kg, qk, el]


def _gdn_prep_specs(nq):
    c = GDN_CHUNK
    big = pl.BlockSpec((c, nq * LANE), lambda n: (n, 0))
    return big


def _gdn_prep_fwd(qkv, gb):
    s = qkv.shape[0]
    c = GDN_CHUNK
    n = s // c
    hw = DN_HEADS * LANE

    def body(qkv_ref, gb_ref, u_ref, w_ref, qg_ref, kg_ref, qk_ref, el_ref):
        res = _gdn_prep_fn(_chunks(qkv_ref), gb_ref[...])
        for ref, chs in zip((u_ref, w_ref, qg_ref, kg_ref, qk_ref, el_ref), res):
            _store(ref, chs)

    row = pl.BlockSpec((c, hw), lambda i: (i, 0))
    return pl.pallas_call(
        body, out_shape=[jax.ShapeDtypeStruct((s, hw), F32)] * 5 + [jax.ShapeDtypeStruct((n * 8, hw), F32)], grid=(n,),
        in_specs=[pl.BlockSpec((c, 3 * hw), lambda i: (i, 0)), pl.BlockSpec((c, LANE), lambda i: (i, 0))],
        out_specs=[row] * 5 + [pl.BlockSpec((8, hw), lambda i: (i, 0))],
        compiler_params=_cparams("parallel"), name="gdn_prep_fwd")(qkv, gb)


def _gdn_prep_bwd(qkv, gb, cts):
    s = qkv.shape[0]
    c = GDN_CHUNK
    n = s // c
    hw = DN_HEADS * LANE

    def body(qkv_ref, gb_ref, du_ref, dw_ref, dqg_ref, dkg_ref, dqk_ref, del_ref, dqkv_ref, dgb_ref):
        _, vjp = jax.vjp(_gdn_prep_fn, _chunks(qkv_ref), gb_ref[...])
        ct = [_chunks(r) for r in (du_ref, dw_ref, dqg_ref, dkg_ref, dqk_ref, del_ref)]
        dqkv, dgb = vjp(ct)
        _store(dqkv_ref, dqkv)
        dgb_ref[...] = dgb

    row = pl.BlockSpec((c, hw), lambda i: (i, 0))
    return pl.pallas_call(
        body, out_shape=[jax.ShapeDtypeStruct(qkv.shape, F32), jax.ShapeDtypeStruct(gb.shape, F32)], grid=(n,),
        in_specs=[pl.BlockSpec((c, 3 * hw), lambda i: (i, 0)), pl.BlockSpec((c, LANE), lambda i: (i, 0))]
        + [row] * 5 + [pl.BlockSpec((8, hw), lambda i: (i, 0))],
        out_specs=[pl.BlockSpec((c, 3 * hw), lambda i: (i, 0)), pl.BlockSpec((c, LANE), lambda i: (i, 0))],
        compiler_params=_cparams("parallel"), name="gdn_prep_bwd")(qkv, gb, *cts)


@jax.custom_vjp
def gdn_prep(qkv, gb):
    return tuple(_gdn_prep_fwd(qkv, gb))


gdn_prep.defvjp(lambda qkv, gb: (tuple(_gdn_prep_fwd(qkv, gb)), (qkv, gb)),
                lambda saved, cts: tuple(_gdn_prep_bwd(*saved, cts)))


def _gdn_scan_fwd(u, w, qg, kg, qk, el):
    s = u.shape[0]
    c = GDN_CHUNK
    n = s // c
    hw = DN_HEADS * LANE

    def body(u_ref, w_ref, qg_ref, kg_ref, qk_ref, el_ref, o_ref, st_ref, s_sc):
        @pl.when(pl.program_id(0) == 0)
        def _():
            s_sc[...] = jnp.zeros_like(s_sc)

        heads = range(DN_HEADS)
        sl = [slice(h * LANE, (h + 1) * LANE) for h in heads]
        st = [s_sc[h] for h in heads]
        for h in heads:
            st_ref[sl[h], :] = st[h]
        v_new = [u_ref[:, sl[h]] - _hdot(w_ref[:, sl[h]], st[h]) for h in heads]
        o_st = [_hdot(qg_ref[:, sl[h]], st[h]) for h in heads]
        o_in = [_hdot(qk_ref[:, sl[h]], v_new[h]) for h in heads]
        s_up = [_hdot(kg_ref[:, sl[h]], v_new[h], _TN) for h in heads]
        for h in heads:
            o_ref[:, sl[h]] = o_st[h] + o_in[h]
            s_sc[h] = st[h] * el_ref[0:1, sl[h]] + s_up[h]

    row = pl.BlockSpec((c, hw), lambda i: (i, 0))
    return pl.pallas_call(
        body, out_shape=[jax.ShapeDtypeStruct((s, hw), F32), jax.ShapeDtypeStruct((n, hw, LANE), F32)], grid=(n,),
        in_specs=[row] * 5 + [pl.BlockSpec((8, hw), lambda i: (i, 0))],
        out_specs=[row, pl.BlockSpec((None, hw, LANE), lambda i: (i, 0, 0))],
        scratch_shapes=[pltpu.VMEM((DN_HEADS, LANE, LANE), F32)],
        compiler_params=_cparams("arbitrary"), name="gdn_scan_fwd")(u, w, qg, kg, qk, el)


def _gdn_scan_bwd(u, w, qg, kg, qk, el, states, do):
    s = u.shape[0]
    c = GDN_CHUNK
    n = s // c
    hw = DN_HEADS * LANE

    def body(u_ref, w_ref, qg_ref, kg_ref, qk_ref, el_ref, st_ref, do_ref,
             du_ref, dw_ref, dqg_ref, dkg_ref, dqk_ref, del_ref, ds_sc):
        @pl.when(pl.program_id(0) == 0)
        def _():
            ds_sc[...] = jnp.zeros_like(ds_sc)

        heads = range(DN_HEADS)
        sl = [slice(h * LANE, (h + 1) * LANE) for h in heads]
        st = [st_ref[sl[h], :] for h in heads]
        ds = [ds_sc[h] for h in heads]
        do = [do_ref[:, sl[h]] for h in heads]
        v_new = [u_ref[:, sl[h]] - _hdot(w_ref[:, sl[h]], st[h]) for h in heads]
        dv_new = [_hdot(qk_ref[:, sl[h]], do[h], _TN) + _hdot(kg_ref[:, sl[h]], ds[h]) for h in heads]
        first_row = lax.broadcasted_iota(jnp.int32, (8, LANE), 0) == 0
        for h in heads:
            du_ref[:, sl[h]] = dv_new[h]
            dw_ref[:, sl[h]] = -_hdot(dv_new[h], st[h], _NT)
            dqg_ref[:, sl[h]] = _hdot(do[h], st[h], _NT)
            dqk_ref[:, sl[h]] = _hdot(do[h], v_new[h], _NT)
            dkg_ref[:, sl[h]] = _hdot(v_new[h], ds[h], _NT)
            del_ref[:, sl[h]] = jnp.where(first_row, jnp.sum(st[h] * ds[h], axis=0, keepdims=True), 0.0)
        ds_new = [_hdot(qg_ref[:, sl[h]], do[h], _TN) + ds[h] * el_ref[0:1, sl[h]] - _hdot(w_ref[:, sl[h]], dv_new[h], _TN)
                  for h in heads]
        for h in heads:
            ds_sc[h] = ds_new[h]

    row = pl.BlockSpec((c, hw), lambda i: (n - 1 - i, 0))
    small = pl.BlockSpec((8, hw), lambda i: (n - 1 - i, 0))
    return pl.pallas_call(
        body, out_shape=[jax.ShapeDtypeStruct((s, hw), F32)] * 5 + [jax.ShapeDtypeStruct((n * 8, hw), F32)], grid=(n,),
        in_specs=[row] * 5 + [small, pl.BlockSpec((None, hw, LANE), lambda i: (n - 1 - i, 0, 0)), row],
        out_specs=[row] * 5 + [small], scratch_shapes=[pltpu.VMEM((DN_HEADS, LANE, LANE), F32)],
        compiler_params=_cparams("arbitrary"), name="gdn_scan_bwd")(u, w, qg, kg, qk, el, states, do)


@jax.custom_vjp
def gdn_scan(u, w, qg, kg, qk, el):
    return _gdn_scan_fwd(u, w, qg, kg, qk, el)[0]


def _gdn_scan_vfwd(*args):
    o, states = _gdn_scan_fwd(*args)
    return o, args + (states,)


gdn_scan.defvjp(_gdn_scan_vfwd, lambda saved, do: tuple(_gdn_scan_bwd(*saved, do)))


def _loss_call(y, t):
    s, d = y.shape
    br = min(ROW_BLOCK, s)
    n = s // br

    def body(y_ref, t_ref, loss_ref, dy_ref, acc):
        i = pl.program_id(0)

        @pl.when(i == 0)
        def _():
            acc[...] = jnp.zeros_like(acc)

        e = y_ref[...] - t_ref[...]
        dy_ref[...] = e / d
        acc[...] += jnp.sum(e * e, axis=0, keepdims=True)

        @pl.when(i == n - 1)
        def _():
            loss_ref[...] = jnp.broadcast_to(jnp.sum(acc[...], axis=1, keepdims=True) * (0.5 / d), loss_ref.shape)

    row = pl.BlockSpec((br, d), lambda i: (i, 0))
    return pl.pallas_call(
        body, out_shape=[jax.ShapeDtypeStruct((1, LANE), F32), jax.ShapeDtypeStruct((s, d), F32)], grid=(n,),
        in_specs=[row, row], out_specs=[pl.BlockSpec((1, LANE), lambda i: (0, 0)), row],
        scratch_shapes=[pltpu.VMEM((1, d), F32)], compiler_params=_cparams("arbitrary"), name="loss_head")(y, t)


@jax.custom_vjp
def loss_head(y, t):
    return _loss_call(y, t)[0][0, 0]


def _loss_head_fwd(y, t):
    loss, dy = _loss_call(y, t)
    return loss[0, 0], (dy,)


loss_head.defvjp(_loss_head_fwd, lambda saved, g: (saved[0] * g, -saved[0] * g))


IN_OFF = {}
_o = 0
for _name, _size in (('q_lat', 384), ('kv_lat', 320), ('z_a', 512), ('dn_qkv', 1536), ('dn_ab', 8), ('z_b', 512),
                     ('dil_qkv', 4608), ('z_c', 512), ('gate', 3072)):
    IN_OFF[_name] = (_o, _o + _size)
    _o += _size
IN_WIDTH = _o
N_CHIPS = 4


def _cols(w, name):
    a, b = IN_OFF[name]
    return w[:, a:b]


def _pad_cols(w, to):
    return jnp.concatenate([w, jnp.zeros((w.shape[0], to - w.shape[1]), w.dtype)], axis=1)


def _pad_row(v, to=None):
    v = v.reshape(1, -1)
    return v if to is None or v.shape[1] == to else _pad_cols(v, to)


def _shard_cols(pieces, a, b):
    wsh = pieces[0].shape[1]
    parts = [pieces[j][:, max(a, j * wsh) - j * wsh:min(b, (j + 1) * wsh) - j * wsh]
             for j in range(len(pieces)) if max(a, j * wsh) < min(b, (j + 1) * wsh)]
    return parts[0] if len(parts) == 1 else jnp.concatenate(parts, axis=1)


def _win_groups_impl(w_in4):
    out = []
    for l in range(w_in4.shape[1]):
        pieces = [w_in4[j, l] for j in range(w_in4.shape[0])]
        cols = lambda name: _shard_cols(pieces, *IN_OFF[name])
        out.append((_pad_cols(jnp.concatenate([cols('q_lat'), cols('kv_lat')], axis=1), 768),
                    jnp.concatenate([cols('z_a'), cols('z_b'), cols('z_c')], axis=1),
                    cols('dn_qkv'), _pad_cols(cols('dn_ab'), LANE), cols('dil_qkv'), cols('gate')))
    return tuple(out)


@jax.custom_vjp
def win_groups(w_in4):
    return _win_groups_impl(w_in4)


def _win_groups_bwd(_, cts):
    n_chip, depth = N_CHIPS, len(cts)
    wsh = IN_WIDTH // n_chip
    per_layer = []
    for l in range(depth):
        dmla, dz, dgdn, dab, ddil, dgate = cts[l]
        full = jnp.concatenate([dmla[:, :704], dz[:, :512], dgdn, dab[:, :8], dz[:, 512:1024], ddil, dz[:, 1024:], dgate], axis=1)
        per_layer.append([full[:, j * wsh:(j + 1) * wsh] for j in range(n_chip)])
    return (jnp.stack([jnp.stack([per_layer[l][j] for l in range(depth)]) for j in range(n_chip)]),)


win_groups.defvjp(lambda w: (_win_groups_impl(w), None), _win_groups_bwd)


def _layer(x, p, tabs):
    w_mla, w_z, w_gdn, w_ab, w_dil, w_gate = p['w_in_groups']
    h, = rowwise(f_rms_full, "rms_in", [x], [_pad_row(p['norm_g'])], [(D_MODEL, BF16)])
    mla_in = matmul(h, w_mla)
    z = matmul(h, w_z)
    dn_qkv = matmul(h, w_gdn)
    dn_ab = matmul(h, w_ab)
    dil_qkv = matmul(h, w_dil)
    gate = matmul(h, w_gate)

    qn, kvn, kpe = rowwise(f_mla1, "mla_norm", [mla_in], [_pad_row(p['mla_q_a_norm_g']), _pad_row(p['mla_kv_a_norm_g'])],
                           [(MLA_Q_RANK, BF16), (MLA_KV_RANK, BF16), (LANE, F32)])
    wq = p['mla_w_q_b']
    wq_pad = jnp.concatenate(
        [wq[:, hh * MLA_QK:hh * MLA_QK + MLA_NOPE] for hh in range(MLA_HEADS)]
        + [_pad_cols(wq[:, hh * MLA_QK + MLA_NOPE:(hh + 1) * MLA_QK], LANE) for hh in range(MLA_HEADS)], axis=1)
    q = matmul(qn, wq_pad)
    kv = matmul(kvn, p['mla_w_kv_b'])
    gq, gk = p['mla_q_norm_g'], p['mla_k_norm_g']
    q_att, k_att, v_att = rowwise(
        f_mla2, "mla_qk", [q, kv, kpe, tabs['cos_r'], tabs['sin_r']],
        [_pad_row(gq[:MLA_NOPE]), _pad_row(gq[MLA_NOPE:], LANE), _pad_row(gk[:MLA_NOPE]), _pad_row(gk[MLA_NOPE:], LANE)],
        [(MLA_HEADS * MLA_DQK, BF16), (MLA_HEADS * MLA_DQK, BF16), (MLA_HEADS * HEAD, BF16)], nograd=(3, 4))
    y_a = mla_attention(q_att, k_att, v_att)

    qkv_n = dn_conv(dn_qkv, p['dn_conv_w'])
    gb, = rowwise(f_gates, "dn_gates", [dn_ab], [_pad_row(p['dn_a_log'], LANE), _pad_row(p['dn_dt_bias'], LANE)], [(LANE, F32)])
    o_b = gdn_scan(*gdn_prep(qkv_n, gb))
    y_b, = rowwise(f_headnorm, "dn_out_norm", [o_b], [_pad_row(p['dn_out_norm_g'])], [(DN_HEADS * DN_DIM, F32)])

    qd, kd, vd = rowwise(f_dil, "dil_qk", [dil_qkv, tabs['cos_h'], tabs['sin_h']],
                         [_pad_row(p['dil_q_norm_g']), _pad_row(p['dil_k_norm_g'])],
                         [(3 * GROUP_W, BF16)] * 3, nograd=(1, 2))
    outs, lses = [], []
    for gi, d in enumerate(DIL_DILATIONS):
        sl = slice(gi * GROUP_W, (gi + 1) * GROUP_W)
        o_g, l_g = dilated_group(qd[:, sl], kd[:, sl], vd[:, sl], d)
        outs.append(o_g)
        lses.append(l_g)
    y_c, = rowwise(f_comb, "dil_comb", outs + lses, [], [(GROUP_W, F32)])

    ys = rowwise(f_merge1, "merge_silu", [y_a, y_b, y_c, z], [], [(BRANCH_W, BF16)] * 3)
    bo = [matmul(ys[b], p['w_branch'][b]) for b in range(3)]
    mixed, = rowwise(f_merge2, "merge_gate", [gate] + bo, [], [(D_MODEL, BF16)])
    return matmul(mixed, p['w_out'], res=x)


def _rope_tables(pos, dim):
    inv_freq = 1.0 / (ROPE_THETA ** (jnp.arange(0, dim, 2, dtype=F32) / dim))
    ang = pos.astype(F32)[:, None] * inv_freq
    return jnp.cos(ang), jnp.sin(ang)


def _tables(pos):
    cr, sr = _rope_tables(pos, MLA_ROPE)
    ch, sh = _rope_tables(pos, HEAD)
    zero = jnp.zeros((pos.shape[0], LANE - MLA_ROPE), F32)
    return {'cos_r': jnp.concatenate([cr, cr, zero], axis=1), 'sin_r': jnp.concatenate([sr, sr, zero], axis=1),
            'cos_h': jnp.concatenate([ch, ch], axis=1), 'sin_h': jnp.concatenate([sh, sh], axis=1)}


def _local_loss(w_in4, mats, conv_w, small, x, target, tabs):
    groups = win_groups(w_in4)
    for l in range(DEPTH):
        p = {k: v[l] for k, v in mats.items()}
        p.update({k: v[l] for k, v in small.items()})
        p['dn_conv_w'] = conv_w[l]
        p['w_in_groups'] = groups[l]
        x = _layer(x, p, tabs)
    return loss_head(x, target)


def _pack(arrays, dtype, row_tile):
    flat = jnp.concatenate([a.astype(dtype).reshape(-1) for a in arrays])
    rows = -(-flat.shape[0] // (LANE * row_tile)) * row_tile
    flat = jnp.concatenate([flat, jnp.zeros((rows * LANE - flat.shape[0],), dtype)])
    return flat.reshape(rows, LANE)


def _unpack_impl(buf, shapes):
    flat = buf.reshape(-1)
    out, off = [], 0
    for shp in shapes:
        n = math.prod(shp)
        out.append(flat[off:off + n].reshape(shp))
        off += n
    return tuple(out)


@functools.partial(jax.custom_vjp, nondiff_argnums=(1, 2, 3))
def _unpack_p(buf, shapes, dtype_name, rows):
    return _unpack_impl(buf, shapes)


_unpack_p.defvjp(lambda buf, shapes, dtype_name, rows: (_unpack_impl(buf, shapes), None),
                 lambda shapes, dtype_name, rows, _, cts: (_pack(cts, jnp.dtype(dtype_name), rows),))


def _unpack(buf, shapes):
    return _unpack_p(buf, tuple(shapes), jnp.dtype(buf.dtype).name, buf.shape[0])


def _full_from_chips(buf4, shard_shapes):
    per_chip = [_unpack(buf4[j], tuple(shard_shapes)) for j in range(4)]
    return {name: jnp.concatenate([per_chip[j][i] for j in range(4)], axis=axis) for i, (name, axis) in enumerate(MATS)}


_HBM = pl.BlockSpec(memory_space=pltpu.HBM)
_VMEM = pl.BlockSpec(memory_space=pltpu.VMEM)


def _chip_peers(x, y):
    return [(1 - x, y), (x, 1 - y), (1 - x, 1 - y)]


def chip_all_to_all(arrays):
    n = len(arrays)

    def body(*refs):
        in_refs, out_refs = refs[:n], refs[n:2 * n]
        send_sems, recv_sems, local_sems = refs[2 * n:]
        x, y, c = lax.axis_index("x"), lax.axis_index("y"), lax.axis_index("c")
        me = 2 * x + y
        peers = _chip_peers(x, y)
        local, sends = [], []
        for a, (i_ref, o_ref) in enumerate(zip(in_refs, out_refs)):
            local.append(pltpu.make_async_copy(i_ref.at[me], o_ref.at[me], local_sems.at[a]))
            local[-1].start()
            for k, (px, py) in enumerate(peers):
                sends.append(pltpu.make_async_remote_copy(
                    src_ref=i_ref.at[2 * px + py], dst_ref=o_ref.at[me], send_sem=send_sems.at[3 * a + k],
                    recv_sem=recv_sems.at[3 * a + k], device_id=(px, py, c), device_id_type=MESH))
                sends[-1].start()
        for a, (i_ref, o_ref) in enumerate(zip(in_refs, out_refs)):
            for k, (px, py) in enumerate(peers):
                pltpu.make_async_remote_copy(
                    src_ref=i_ref.at[me], dst_ref=o_ref.at[2 * px + py], send_sem=send_sems.at[3 * a + k],
                    recv_sem=recv_sems.at[3 * a + k], device_id=(px, py, c), device_id_type=MESH).wait_recv()
        for cp in sends:
            cp.wait_send()
        for cp in local:
            cp.wait()

    return pl.pallas_call(
        body, out_shape=[jax.ShapeDtypeStruct(a.shape, a.dtype) for a in arrays], in_specs=[_HBM] * n, out_specs=[_HBM] * n,
        scratch_shapes=[pltpu.SemaphoreType.DMA((3 * n,)), pltpu.SemaphoreType.DMA((3 * n,)), pltpu.SemaphoreType.DMA((n,))],
        name="chip_all_to_all")(*arrays)


def chip_all_gather(shards):
    n = len(shards)

    def body(*refs):
        in_refs, out_refs = refs[:n], refs[n:2 * n]
        send_sems, recv_sems = refs[2 * n:]
        x, y, c = lax.axis_index("x"), lax.axis_index("y"), lax.axis_index("c")
        me = 2 * x + y
        peers = _chip_peers(x, y)
        sends = []
        for a, (i_ref, o_ref) in enumerate(zip(in_refs, out_refs)):
            for k, (px, py) in enumerate(peers):
                sends.append(pltpu.make_async_remote_copy(
                    src_ref=i_ref.at[c], dst_ref=o_ref.at[me, c], send_sem=send_sems.at[6 * a + k],
                    recv_sem=recv_sems.at[6 * a + k], device_id=(px, py, c), device_id_type=MESH))
                sends[-1].start()
        for a, (i_ref, o_ref) in enumerate(zip(in_refs, out_refs)):
            for k, (px, py) in enumerate(peers):
                landed = o_ref.at[2 * px + py, c]
                pltpu.make_async_remote_copy(src_ref=i_ref.at[c], dst_ref=landed, send_sem=send_sems.at[6 * a + k],
                                             recv_sem=recv_sems.at[6 * a + k], device_id=(px, py, c), device_id_type=MESH).wait_recv()
                sends.append(pltpu.make_async_remote_copy(
                    src_ref=landed, dst_ref=landed, send_sem=send_sems.at[6 * a + 3 + k], recv_sem=recv_sems.at[6 * a + 3 + k],
                    device_id=(x, y, 1 - c), device_id_type=MESH))
                sends[-1].start()
        for a, (i_ref, o_ref) in enumerate(zip(in_refs, out_refs)):
            for k, (px, py) in enumerate(peers):
                other = o_ref.at[2 * px + py, 1 - c]
                pltpu.make_async_remote_copy(src_ref=other, dst_ref=other, send_sem=send_sems.at[6 * a + 3 + k],
                                             recv_sem=recv_sems.at[6 * a + 3 + k], device_id=(x, y, 1 - c),
                                             device_id_type=MESH).wait_recv()
        for cp in sends:
            cp.wait_send()

    me = 2 * lax.axis_index("x") + lax.axis_index("y")
    outs = pl.pallas_call(
        body, out_shape=[jax.ShapeDtypeStruct((4,) + a.shape, a.dtype) for a in shards], in_specs=[_HBM] * n, out_specs=[_HBM] * n,
        scratch_shapes=[pltpu.SemaphoreType.DMA((6 * n,)), pltpu.SemaphoreType.DMA((6 * n,))],
        name="chip_all_gather")(*shards)
    return [lax.dynamic_update_index_in_dim(o, a, me, 0) for o, a in zip(outs, shards)]


def sibling_swap(arrays):
    n = len(arrays)

    def body(*refs):
        in_refs, out_refs = refs[:n], refs[n:2 * n]
        send_sems, recv_sems = refs[2 * n:]
        x, y, c = lax.axis_index("x"), lax.axis_index("y"), lax.axis_index("c")
        cps = [pltpu.make_async_remote_copy(src_ref=i_ref, dst_ref=o_ref, send_sem=send_sems.at[a], recv_sem=recv_sems.at[a],
                                            device_id=(x, y, 1 - c), device_id_type=MESH)
               for a, (i_ref, o_ref) in enumerate(zip(in_refs, out_refs))]
        for cp in cps:
            cp.start()
        for cp in cps:
            cp.wait()

    return pl.pallas_call(
        body, out_shape=[jax.ShapeDtypeStruct(a.shape, a.dtype) for a in arrays], in_specs=[_HBM] * n, out_specs=[_HBM] * n,
        scratch_shapes=[pltpu.SemaphoreType.DMA((n,)), pltpu.SemaphoreType.DMA((n,))], name="sibling_swap")(*arrays)


def sibling_all_gather(halves):
    n = len(halves)

    def body(*refs):
        in_refs, out_refs = refs[:n], refs[n:2 * n]
        send_sems, recv_sems = refs[2 * n:]
        x, y, c = lax.axis_index("x"), lax.axis_index("y"), lax.axis_index("c")
        sends = [pltpu.make_async_remote_copy(src_ref=i_ref, dst_ref=o_ref.at[c], send_sem=send_sems.at[a], recv_sem=recv_sems.at[a],
                                              device_id=(x, y, 1 - c), device_id_type=MESH)
                 for a, (i_ref, o_ref) in enumerate(zip(in_refs, out_refs))]
        for cp in sends:
            cp.start()
        for a, (i_ref, o_ref) in enumerate(zip(in_refs, out_refs)):
            pltpu.make_async_remote_copy(src_ref=i_ref, dst_ref=o_ref.at[1 - c], send_sem=send_sems.at[a], recv_sem=recv_sems.at[a],
                                         device_id=(x, y, 1 - c), device_id_type=MESH).wait_recv()
        for cp in sends:
            cp.wait_send()

    outs = pl.pallas_call(
        body, out_shape=[jax.ShapeDtypeStruct((2,) + a.shape, a.dtype) for a in halves], in_specs=[_HBM] * n, out_specs=[_HBM] * n,
        scratch_shapes=[pltpu.SemaphoreType.DMA((n,)), pltpu.SemaphoreType.DMA((n,))], name="sibling_all_gather")(*halves)
    return [lax.dynamic_update_index_in_dim(o, a, lax.axis_index("c"), 0) for o, a in zip(outs, halves)]


def all_gather8(v, name):
    def body(v_ref, out_ref, send_sems, recv_sems):
        x, y, c = lax.axis_index("x"), lax.axis_index("y"), lax.axis_index("c")
        out_ref[4 * x + 2 * y + c] = v_ref[...]

        def peer(k):
            return (x ^ (k >> 2), y ^ ((k >> 1) & 1), c ^ (k & 1))

        sends = [pltpu.make_async_remote_copy(src_ref=v_ref, dst_ref=out_ref.at[4 * x + 2 * y + c], send_sem=send_sems.at[k - 1],
                                              recv_sem=recv_sems.at[k - 1], device_id=peer(k), device_id_type=MESH)
                 for k in range(1, 8)]
        for cp in sends:
            cp.start()
        for k in range(1, 8):
            px, py, pc = peer(k)
            pltpu.make_async_remote_copy(src_ref=v_ref, dst_ref=out_ref.at[4 * px + 2 * py + pc], send_sem=send_sems.at[k - 1],
                                         recv_sem=recv_sems.at[k - 1], device_id=peer(k), device_id_type=MESH).wait_recv()
        for cp in sends:
            cp.wait_send()

    return pl.pallas_call(
        body, out_shape=jax.ShapeDtypeStruct((8,) + v.shape, v.dtype), in_specs=[_VMEM], out_specs=_VMEM,
        scratch_shapes=[pltpu.SemaphoreType.DMA((7,)), pltpu.SemaphoreType.DMA((7,))], name=name)(v)


def pair_add(a, b, row_tile, name):
    rows, width = a.shape

    def body(a_ref, b_ref, o_ref):
        o_ref[...] = (a_ref[...].astype(F32) + b_ref[...].astype(F32)).astype(o_ref.dtype)

    spec = pl.BlockSpec((row_tile, width), lambda i: (i, 0))
    return pl.pallas_call(body, out_shape=jax.ShapeDtypeStruct(a.shape, a.dtype), grid=(rows // row_tile,),
                          in_specs=[spec, spec], out_specs=spec, compiler_params=_cparams("parallel"), name=name)(a, b)


def sum_blocks(blocks, row_tile, name):
    n, rows, width = blocks.shape

    def body(b_ref, o_ref):
        acc = b_ref[0].astype(F32)
        for j in range(1, n):
            acc = acc + b_ref[j].astype(F32)
        o_ref[...] = acc

    return pl.pallas_call(
        body, out_shape=jax.ShapeDtypeStruct((rows, width), F32), grid=(rows // row_tile,),
        in_specs=[pl.BlockSpec((n, row_tile, width), lambda i: (0, i, 0))],
        out_specs=pl.BlockSpec((row_tile, width), lambda i: (i, 0)), compiler_params=_cparams("parallel"), name=name)(blocks)


def adamw(g_parts, w, m, v, row_tile, name):
    npart = len(g_parts)

    def body(*refs):
        g = refs[0][...]
        for r in refs[1:npart]:
            g = g + r[...]
        w_ref, m_ref, v_ref, g_out, d_out, m_out, v_out = refs[npart:]
        m_new = ADAM_B1 * m_ref[...] + (1.0 - ADAM_B1) * g
        v_new = ADAM_B2 * v_ref[...] + (1.0 - ADAM_B2) * (g * g)
        m_hat = m_new / (1.0 - ADAM_B1 ** ADAM_STEP)
        v_hat = v_new / (1.0 - ADAM_B2 ** ADAM_STEP)
        g_out[...] = g
        d_out[...] = -ADAM_LR * (m_hat / (jnp.sqrt(v_hat) + ADAM_EPS) + ADAM_WD * w_ref[...])
        m_out[...] = m_new
        v_out[...] = v_new

    rows, width = w.shape
    spec = pl.BlockSpec((row_tile, width), lambda i: (i, 0))
    return pl.pallas_call(
        body, out_shape=[jax.ShapeDtypeStruct(w.shape, F32)] * 4, grid=(rows // row_tile,),
        in_specs=[spec] * (npart + 3), out_specs=[spec] * 4, compiler_params=_cparams("parallel"), name=name)(*g_parts, w, m, v)


def kernel(x, positions, norm_g, w_in, mla_q_a_norm_g, mla_w_q_b, mla_kv_a_norm_g, mla_w_kv_b, mla_q_norm_g, mla_k_norm_g, dn_conv_w, dn_a_log, dn_dt_bias, dn_out_norm_g, dil_q_norm_g, dil_k_norm_g, w_branch, w_out, loss_target, m_norm_g, m_w_in, m_mla_q_a_norm_g, m_mla_w_q_b, m_mla_kv_a_norm_g, m_mla_w_kv_b, m_mla_q_norm_g, m_mla_k_norm_g, m_dn_conv_w, m_dn_a_log, m_dn_dt_bias, m_dn_out_norm_g, m_dil_q_norm_g, m_dil_k_norm_g, m_w_branch, m_w_out, v_norm_g, v_w_in, v_mla_q_a_norm_g, v_mla_w_q_b, v_mla_kv_a_norm_g, v_mla_w_kv_b, v_mla_q_norm_g, v_mla_k_norm_g, v_dn_conv_w, v_dn_a_log, v_dn_dt_bias, v_dn_out_norm_g, v_dil_q_norm_g, v_dil_k_norm_g, v_w_branch, v_w_out):
    w = dict(norm_g=norm_g, w_in=w_in, mla_q_a_norm_g=mla_q_a_norm_g, mla_w_q_b=mla_w_q_b, mla_kv_a_norm_g=mla_kv_a_norm_g,
             mla_w_kv_b=mla_w_kv_b, mla_q_norm_g=mla_q_norm_g, mla_k_norm_g=mla_k_norm_g, dn_conv_w=dn_conv_w, dn_a_log=dn_a_log,
             dn_dt_bias=dn_dt_bias, dn_out_norm_g=dn_out_norm_g, dil_q_norm_g=dil_q_norm_g, dil_k_norm_g=dil_k_norm_g,
             w_branch=w_branch, w_out=w_out)
    m = dict(norm_g=m_norm_g, w_in=m_w_in, mla_q_a_norm_g=m_mla_q_a_norm_g, mla_w_q_b=m_mla_w_q_b, mla_kv_a_norm_g=m_mla_kv_a_norm_g,
             mla_w_kv_b=m_mla_w_kv_b, mla_q_norm_g=m_mla_q_norm_g, mla_k_norm_g=m_mla_k_norm_g, dn_conv_w=m_dn_conv_w,
             dn_a_log=m_dn_a_log, dn_dt_bias=m_dn_dt_bias, dn_out_norm_g=m_dn_out_norm_g, dil_q_norm_g=m_dil_q_norm_g,
             dil_k_norm_g=m_dil_k_norm_g, w_branch=m_w_branch, w_out=m_w_out)
    v = dict(norm_g=v_norm_g, w_in=v_w_in, mla_q_a_norm_g=v_mla_q_a_norm_g, mla_w_q_b=v_mla_w_q_b, mla_kv_a_norm_g=v_mla_kv_a_norm_g,
             mla_w_kv_b=v_mla_w_kv_b, mla_q_norm_g=v_mla_q_norm_g, mla_k_norm_g=v_mla_k_norm_g, dn_conv_w=v_dn_conv_w,
             dn_a_log=v_dn_a_log, dn_dt_bias=v_dn_dt_bias, dn_out_norm_g=v_dn_out_norm_g, dil_q_norm_g=v_dil_q_norm_g,
             dil_k_norm_g=v_dil_k_norm_g, w_branch=v_w_branch, w_out=v_w_out)
    chip = 2 * lax.axis_index("x") + lax.axis_index("y")
    mat_names = [n for n, _ in MATS]
    mat_shapes = tuple(w[n].shape for n in mat_names)
    conv_shard = dn_conv_w.shape
    win_shape = w_in.shape
    win_rows = (win_shape[0] * win_shape[1], win_shape[2])

    mats_sh = _pack([w[n] for n in mat_names], BF16, 2 * MAT_ROWS)
    mat_rows = mats_sh.shape[0]
    w_in4, mats4 = chip_all_gather([w_in.astype(BF16), mats_sh.reshape(2, mat_rows // 2, LANE)])
    mats4 = mats4.reshape(N_CHIPS, mat_rows, LANE)
    conv8 = all_gather8(_pack([dn_conv_w], F32, 8), "gather_conv_w")
    conv_full = jnp.concatenate([_unpack(conv8[2 * j], (conv_shard,))[0] for j in range(4)], axis=2)
    small = {n: w[n] for n in SMALL}
    tabs = _tables(positions[0])

    def loss_fn(w_in4, mats4, conv_full, small, xs):
        return _local_loss(w_in4, _full_from_chips(mats4, mat_shapes), conv_full, small, xs, loss_target[0], tabs)

    loss, (g_win4, g_mats4, g_conv, g_small, g_x) = jax.value_and_grad(loss_fn, argnums=(0, 1, 2, 3, 4))(
        w_in4, mats4, conv_full, small, x[0])
    loss = lax.psum(loss, ("x", "y", "c"))

    c_idx = lax.axis_index("c")
    half_win = (N_CHIPS * win_shape[1], win_shape[2])
    half_mats = (N_CHIPS * (mat_rows // 2), LANE)
    g_mats4 = g_mats4.reshape(N_CHIPS, 2, mat_rows // 2, LANE)
    pick = lambda g, i, shape: lax.dynamic_index_in_dim(g, i, axis=1, keepdims=False).reshape(shape)
    from_sib = sibling_swap([pick(g_win4, 1 - c_idx, half_win), pick(g_mats4, 1 - c_idx, half_mats)])
    s_win = pair_add(pick(g_win4, c_idx, half_win), from_sib[0], WIN_ROWS, "pair_add_w_in")
    s_mats = pair_add(pick(g_mats4, c_idx, half_mats), from_sib[1], MAT_ROWS, "pair_add_mats")
    r_win, r_mats = chip_all_to_all([s_win.reshape(N_CHIPS, win_shape[1], win_shape[2]),
                                     s_mats.reshape(N_CHIPS, mat_rows // 2, LANE)])
    g_win, g_mats = sibling_all_gather([sum_blocks(r_win, WIN_ROWS, "sum_chip_pieces_w_in"),
                                        sum_blocks(r_mats, MAT_ROWS, "sum_chip_pieces_mats")])
    res_win = adamw([g_win.reshape(win_rows)], *[d['w_in'].reshape(win_rows) for d in (w, m, v)], WIN_ROWS, "adamw_w_in")
    packed = [_pack([d[n] for n in mat_names], F32, 2 * MAT_ROWS) for d in (w, m, v)]
    res_big = [dict(zip(mat_names, _unpack(r, mat_shapes)))
               for r in adamw([g_mats.reshape(mat_rows, LANE)], *packed, MAT_ROWS, "adamw_mats")]
    for k in range(4):
        res_big[k]['w_in'] = res_win[k].reshape(win_shape)

    small_shapes = tuple(w[n].shape for n in SMALL) + (g_conv.shape,)
    g_all = sum_blocks(all_gather8(_pack([g_small[n] for n in SMALL] + [g_conv], F32, 8), "gather_small_grads"), 8, "sum_small")
    g_list = list(_unpack(g_all, small_shapes))
    g_list[-1] = lax.dynamic_slice_in_dim(g_list[-1], chip * conv_shard[2], conv_shard[2], axis=2)
    small_names = list(SMALL) + ['dn_conv_w']
    packed_s = [_pack([d[n] for n in small_names], F32, 8) for d in (w, m, v)]
    shapes_s = tuple(w[n].shape for n in small_names)
    res_small = [dict(zip(small_names, _unpack(r, shapes_s))) for r in adamw([_pack(g_list, F32, 8)], *packed_s, 8, "adamw_small")]

    outs = [loss, g_x[None]]
    for k in range(4):
        outs += [res_big[k][n] if n in res_big[k] else res_small[k][n] for n in WEIGHTS]
    return tuple(outs)
```

```python
import functools
import math

import jax
import jax.numpy as jnp
from jax import lax
from jax.experimental import pallas as pl
from jax.experimental.pallas import tpu as pltpu

F32 = jnp.float32
BF16 = jnp.bfloat16
HI = lax.Precision.HIGHEST
MESH = pl.DeviceIdType.MESH

LANE = 128
VMEM_LIMIT = 48 * 1024 * 1024
ROW_BLOCK = 256
MM_TM, MM_TN, MM_TK = 1024, 1024, 1024
MAT_ROWS = 512
WIN_ROWS = 128

RMS_EPS = 1e-6
ROPE_THETA = 10000.0
D_MODEL = 1024
DEPTH = 2
MLA_HEADS = 4
MLA_NOPE = 128
MLA_ROPE = 64
MLA_QK = MLA_NOPE + MLA_ROPE
MLA_Q_RANK = 384
MLA_KV_RANK = 256
DN_HEADS = 4
DN_DIM = 128
DN_CONV = 4
GDN_CHUNK = 128
DIL_WINDOWS = (128, 512, 2048)
DIL_DILATIONS = (1, 4, 16)
DIL_GROUPS = 3
DIL_BLOCK = 128
HEAD = 128
BRANCH_W = 512

ADAM_LR = 0.001
ADAM_B1 = 0.9
ADAM_B2 = 0.999
ADAM_EPS = 1e-08
ADAM_WD = 0.01
ADAM_STEP = 10

WEIGHTS = ['norm_g', 'w_in', 'mla_q_a_norm_g', 'mla_w_q_b', 'mla_kv_a_norm_g', 'mla_w_kv_b', 'mla_q_norm_g',
           'mla_k_norm_g', 'dn_conv_w', 'dn_a_log', 'dn_dt_bias', 'dn_out_norm_g', 'dil_q_norm_g', 'dil_k_norm_g',
           'w_branch', 'w_out']
MATS = (('mla_w_q_b', 2), ('mla_w_kv_b', 2), ('w_branch', 3), ('w_out', 1))
SMALL = ('norm_g', 'mla_q_a_norm_g', 'mla_kv_a_norm_g', 'mla_q_norm_g', 'mla_k_norm_g', 'dn_a_log', 'dn_dt_bias',
         'dn_out_norm_g', 'dil_q_norm_g', 'dil_k_norm_g')


def _cparams(*sem):
    return pltpu.CompilerParams(dimension_semantics=sem or None, vmem_limit_bytes=VMEM_LIMIT)


def _tile(dim, target):
    best = 0
    for t in range(LANE, min(dim, target) + 1, LANE):
        if dim % t == 0:
            best = t
    assert best, (dim, target)
    return best


def _mm(a, b, mode, out_dtype, res=None):
    if mode == 'nn':
        (m, k), (k2, n) = a.shape, b.shape
    elif mode == 'nt':
        (m, k), (n, k2) = a.shape, b.shape
    else:
        (k, m), (k2, n) = a.shape, b.shape
    assert k == k2, (a.shape, b.shape, mode)
    tm, tn, tk = _tile(m, MM_TM), _tile(n, MM_TN), _tile(k, MM_TK)
    nk = k // tk
    dims = {'nn': (((1,), (0,)), ((), ())), 'nt': (((1,), (1,)), ((), ())), 'tn': (((0,), (0,)), ((), ()))}[mode]

    def body(*refs):
        if res is None:
            a_ref, b_ref, o_ref = refs[:3]
        else:
            a_ref, b_ref, r_ref, o_ref = refs[:4]
        part = lax.dot_general(a_ref[...].astype(BF16), b_ref[...].astype(BF16), dims, preferred_element_type=F32)

        def finish(r):
            if res is not None:
                r = r + r_ref[...].astype(F32)
            o_ref[...] = r.astype(o_ref.dtype)

        if nk == 1:
            finish(part)
            return
        acc = refs[-1]
        kk = pl.program_id(2)

        @pl.when(kk == 0)
        def _():
            acc[...] = part

        @pl.when(jnp.logical_and(kk > 0, kk < nk - 1))
        def _():
            acc[...] += part

        @pl.when(kk == nk - 1)
        def _():
            finish(acc[...] + part)

    a_spec = pl.BlockSpec((tk, tm), lambda i, j, kk: (kk, i)) if mode == 'tn' else pl.BlockSpec((tm, tk), lambda i, j, kk: (i, kk))
    b_spec = pl.BlockSpec((tn, tk), lambda i, j, kk: (j, kk)) if mode == 'nt' else pl.BlockSpec((tk, tn), lambda i, j, kk: (kk, j))
    o_spec = pl.BlockSpec((tm, tn), lambda i, j, kk: (i, j))
    in_specs = [a_spec, b_spec] + ([o_spec] if res is not None else [])
    args = (a, b) + ((res,) if res is not None else ())
    return pl.pallas_call(
        body, out_shape=jax.ShapeDtypeStruct((m, n), out_dtype), grid=(m // tm, n // tn, nk),
        in_specs=in_specs, out_specs=o_spec, scratch_shapes=[pltpu.VMEM((tm, tn), F32)] if nk > 1 else [],
        compiler_params=_cparams("parallel", "parallel", "arbitrary"),
        name=f"mm_{mode}_{m}x{k}x{n}" + ("_res" if res is not None else ""))(*args)


def _make_matmul(out_dtype, with_res):
    @jax.custom_vjp
    def mm(a, b, *r):
        return _mm(a, b, 'nn', out_dtype, *r)

    def fwd(a, b, *r):
        return mm(a, b, *r), (a, b)

    def bwd(saved, g):
        a, b = saved
        da = _mm(g, b, 'nt', a.dtype)
        db = _mm(a, g, 'tn', b.dtype)
        return (da, db) + ((g,) if with_res else ())

    mm.defvjp(fwd, bwd)
    return mm


def matmul(a, b, out_dtype=F32, res=None):
    if res is None:
        return _make_matmul(out_dtype, False)(a, b)
    return _make_matmul(out_dtype, True)(a, b, res)


def _chunks(ref):
    return [ref[:, c * LANE:(c + 1) * LANE].astype(F32) for c in range(ref.shape[-1] // LANE)]


def _store(ref, chunks):
    for c, ch in enumerate(chunks):
        ref[:, c * LANE:(c + 1) * LANE] = ch.astype(ref.dtype)


def _row_spec(width, br):
    return pl.BlockSpec((br, width), lambda i: (i, 0))


def _par_spec(width):
    return pl.BlockSpec((1, width), lambda i: (0, 0))


def _rw_fwd(f, name, rows, params, outs, br):
    s = rows[0].shape[0]
    br = min(br, s)
    nr, npar = len(rows), len(params)

    def body(*refs):
        rc = [_chunks(r) for r in refs[:nr]]
        pc = [_chunks(p) for p in refs[nr:nr + npar]]
        res = f(rc, pc)
        for o_ref, chs in zip(refs[nr + npar:], res):
            _store(o_ref, chs)

    return pl.pallas_call(
        body, out_shape=[jax.ShapeDtypeStruct((s, w), dt) for w, dt in outs], grid=(s // br,),
        in_specs=[_row_spec(r.shape[1], br) for r in rows] + [_par_spec(p.shape[1]) for p in params],
        out_specs=[_row_spec(w, br) for w, _ in outs],
        compiler_params=_cparams("parallel"), name=name + "_fwd")(*rows, *params)


def _rw_bwd(f, name, rows, params, cts, nograd, br):
    s = rows[0].shape[0]
    br = min(br, s)
    nr, npar, nct = len(rows), len(params), len(cts)
    grad_rows = [i for i in range(nr) if i not in nograd]

    def body(*refs):
        rc = [_chunks(r) for r in refs[:nr]]
        pc = [_chunks(p) for p in refs[nr:nr + npar]]
        ct = [_chunks(c) for c in refs[nr + npar:nr + npar + nct]]
        out_refs = refs[nr + npar + nct:]
        _, vjp = jax.vjp(f, rc, pc)
        drc, dpc = vjp(ct)
        for o_ref, i in zip(out_refs[:len(grad_rows)], grad_rows):
            _store(o_ref, drc[i])
        i0 = pl.program_id(0)
        for o_ref, chs in zip(out_refs[len(grad_rows):], dpc):
            @pl.when(i0 == 0)
            def _(o_ref=o_ref):
                o_ref[...] = jnp.zeros_like(o_ref)
            for c, ch in enumerate(chs):
                o_ref[:, c * LANE:(c + 1) * LANE] += ch

    out_shape = ([jax.ShapeDtypeStruct(rows[i].shape, rows[i].dtype) for i in grad_rows]
                 + [jax.ShapeDtypeStruct(p.shape, F32) for p in params])
    out_specs = [_row_spec(rows[i].shape[1], br) for i in grad_rows] + [_par_spec(p.shape[1]) for p in params]
    res = pl.pallas_call(
        body, out_shape=out_shape, grid=(s // br,),
        in_specs=([_row_spec(r.shape[1], br) for r in rows] + [_par_spec(p.shape[1]) for p in params]
                  + [_row_spec(c.shape[1], br) for c in cts]),
        out_specs=out_specs, compiler_params=_cparams("arbitrary"), name=name + "_bwd")(*rows, *params, *cts)
    drows = [None] * nr
    for o, i in zip(res[:len(grad_rows)], grad_rows):
        drows[i] = o
    for i in nograd:
        drows[i] = jnp.zeros_like(rows[i])
    return tuple(drows), tuple(res[len(grad_rows):])


def rowwise(f, name, rows, params, outs, nograd=(), br=ROW_BLOCK):
    @jax.custom_vjp
    def op(rows, params):
        return tuple(_rw_fwd(f, name, rows, params, outs, br))

    def fwd(rows, params):
        return op(rows, params), (rows, params)

    def bwd(saved, cts):
        rows, params = saved
        return _rw_bwd(f, name, rows, params, list(cts), nograd, br)

    op.defvjp(fwd, bwd)
    return op(tuple(rows), tuple(params))


def _lane_roll(x, s):
    w = x.shape[-1]

    @jax.custom_vjp
    def r(v):
        return pltpu.roll(v, s, 1)

    r.defvjp(lambda v: (r(v), None), lambda _, g: (pltpu.roll(g, (w - s) % w, 1),))
    return r(x)


def _lanes(x):
    return lax.broadcasted_iota(jnp.int32, x.shape, 1)


def _rms(x, g, n=LANE):
    return x * lax.rsqrt(jnp.sum(x * x, axis=-1, keepdims=True) / n + RMS_EPS) * g


def _rope128(x, cos, sin):
    lane = _lanes(x)
    rot = _lane_roll(x, 64) * jnp.where(lane < 64, -1.0, 1.0)
    return x * cos + rot * sin


def _rope64(x, cos, sin):
    lane = _lanes(x)
    rot = jnp.where(lane < 32, -_lane_roll(x, 96), jnp.where(lane < 64, _lane_roll(x, 32), 0.0))
    return x * cos + rot * sin


def _silu(z):
    return z * jax.nn.sigmoid(z)


def _softplus(x):
    return jnp.maximum(x, 0.0) + jnp.log1p(jnp.exp(-jnp.abs(x)))


def f_rms_full(rc, pc):
    x, g = rc[0], pc[0]
    width = len(x) * LANE
    ms = sum(jnp.sum(c * c, axis=-1, keepdims=True) for c in x) / width
    r = lax.rsqrt(ms + RMS_EPS)
    return [[c * r * gc for c, gc in zip(x, g)]]


def f_mla1(rc, pc):
    x = rc[0]
    qn = f_rms_full([x[0:3]], [pc[0]])[0]
    kvn = f_rms_full([x[3:5]], [pc[1]])[0]
    return [qn, kvn, [x[5]]]


def f_mla2(rc, pc):
    q, kv, kpe, cos, sin = rc[0], rc[1], rc[2][0], rc[3][0], rc[4][0]
    gqn, gqp, gkn, gkp = pc[0][0], pc[1][0], pc[2][0], pc[3][0]
    k_pe = _rope64(_rms(kpe, gkp, MLA_ROPE), cos, sin)
    q_att, k_att, v = [], [], []
    for h in range(MLA_HEADS):
        q_att += [_rms(q[h], gqn), _rope64(_rms(q[MLA_HEADS + h], gqp, MLA_ROPE), cos, sin)]
        k_att += [_rms(kv[2 * h], gkn), k_pe]
        v.append(kv[2 * h + 1])
    return [q_att, k_att, v]


def f_gates(rc, pc):
    x, a_log, dt_bias = rc[0][0], pc[0][0], pc[1][0]
    lane = _lanes(x)
    g = -jnp.exp(a_log) * _softplus(x + dt_bias)
    return [[jnp.where(lane < DN_HEADS, g, jnp.where(lane < 2 * DN_HEADS, jax.nn.sigmoid(x), 0.0))]]


def f_headnorm(rc, pc):
    return [[_rms(c, pc[0][0]) for c in rc[0]]]


def f_dil(rc, pc):
    x, cos, sin = rc[0], rc[1][0], rc[2][0]
    gq, gk = pc[0][0], pc[1][0]
    n = len(x) // 3
    q = [_rope128(_rms(c, gq), cos, sin) for c in x[:n]]
    k = [_rope128(_rms(c, gk), cos, sin) for c in x[n:2 * n]]
    return [q, k, list(x[2 * n:])]


def f_comb(rc, pc):
    o, l = rc[:DIL_GROUPS], rc[DIL_GROUPS:]
    out = []
    for c in range(len(o[0])):
        m = functools.reduce(jnp.maximum, [lg[c] for lg in l])
        e = [jnp.exp(lg[c] - m) for lg in l]
        den = sum(e)
        out.append(sum(eg * og[c] for eg, og in zip(e, o)) / den)
    return [out]


def f_merge1(rc, pc):
    z = rc[3]
    n = len(rc[0])
    return [[y * _silu(z[b * n + c]) for c, y in enumerate(rc[b])] for b in range(3)]


def f_merge2(rc, pc):
    gate = rc[0]
    n = len(rc[1])
    return [[sum(jax.nn.sigmoid(gate[b * n + c]) * rc[1 + b][c] for b in range(3)) for c in range(n)]]


MLA_DQK = 2 * LANE
MLA_SCALE = MLA_QK ** -0.5


def _mla_attn_fwd(q, k, v):
    s = q.shape[0]
    h = q.shape[1] // MLA_DQK
    t = min(512, s)
    n = s // t

    def body(q_ref, k_ref, v_ref, o_ref, lse_ref, m_sc, l_sc, acc_sc):
        qi, kj = pl.program_id(1), pl.program_id(2)

        @pl.when(kj == 0)
        def _():
            m_sc[...] = jnp.full_like(m_sc, -jnp.inf)
            l_sc[...] = jnp.zeros_like(l_sc)
            acc_sc[...] = jnp.zeros_like(acc_sc)

        nsub = 2 if t % 256 == 0 else 1
        ts = t // nsub
        rows = [slice(r * ts, (r + 1) * ts) for r in range(nsub)]

        def step(on_diagonal):
            sc = [lax.dot_general(q_ref[rw, :], k_ref[...], (((1,), (1,)), ((), ())), preferred_element_type=F32) * MLA_SCALE
                  for rw in rows]
            if on_diagonal:
                keep = [(lax.broadcasted_iota(jnp.int32, (ts, t), 1)
                         <= r * ts + lax.broadcasted_iota(jnp.int32, (ts, t), 0)) for r in range(nsub)]
                sc = [jnp.where(kp, x, -jnp.inf) for kp, x in zip(keep, sc)]
            m_old = [m_sc[rw, :] for rw in rows]
            m_new = [jnp.maximum(mo, jnp.max(x, axis=-1, keepdims=True)) for mo, x in zip(m_old, sc)]
            alpha = [jnp.exp(mo - mn) for mo, mn in zip(m_old, m_new)]
            p = [jnp.exp(x - mn) for x, mn in zip(sc, m_new)]
            pv = [jnp.dot(x.astype(BF16), v_ref[...], preferred_element_type=F32) for x in p]
            for r, rw in enumerate(rows):
                l_sc[rw, :] = alpha[r] * l_sc[rw, :] + jnp.sum(p[r], axis=-1, keepdims=True)
                acc_sc[rw, :] = alpha[r] * acc_sc[rw, :] + pv[r]
                m_sc[rw, :] = m_new[r]

        @pl.when(kj < qi)
        def _():
            step(False)

        @pl.when(kj == qi)
        def _():
            step(True)

        @pl.when(kj == n - 1)
        def _():
            o_ref[...] = acc_sc[...] / l_sc[...]
            lse_ref[...] = jnp.broadcast_to(m_sc[...] + jnp.log(l_sc[...]), lse_ref.shape)

    return pl.pallas_call(
        body, out_shape=[jax.ShapeDtypeStruct((s, h * HEAD), F32)] * 2, grid=(h, n, n),
        in_specs=[pl.BlockSpec((t, MLA_DQK), lambda hh, i, j: (i, hh)),
                  pl.BlockSpec((t, MLA_DQK), lambda hh, i, j: (jnp.minimum(j, i), hh)),
                  pl.BlockSpec((t, HEAD), lambda hh, i, j: (jnp.minimum(j, i), hh))],
        out_specs=[pl.BlockSpec((t, HEAD), lambda hh, i, j: (i, hh))] * 2,
        scratch_shapes=[pltpu.VMEM((t, 1), F32), pltpu.VMEM((t, 1), F32), pltpu.VMEM((t, HEAD), F32)],
        compiler_params=_cparams("parallel", "parallel", "arbitrary"), name="mla_attn_fwd")(q, k, v)


def _mla_attn_bwd(q, k, v, o, lse, do):
    s = q.shape[0]
    h = q.shape[1] // MLA_DQK
    t = min(512, s)
    n = s // t
    nt, tn = (((1,), (1,)), ((), ())), (((0,), (0,)), ((), ()))

    def body(q_ref, k_ref, v_ref, o_ref, lse_ref, do_ref, dq_ref, dk_ref, dv_ref, dq_sc, dk_sc, dv_sc):
        kj, qi = pl.program_id(1), pl.program_id(2)

        @pl.when(jnp.logical_and(kj == 0, qi == 0))
        def _():
            dq_sc[...] = jnp.zeros_like(dq_sc)

        @pl.when(qi == 0)
        def _():
            dk_sc[...] = jnp.zeros_like(dk_sc)
            dv_sc[...] = jnp.zeros_like(dv_sc)

        def pair(on_diagonal):
            sc = lax.dot_general(q_ref[...], k_ref[...], nt, preferred_element_type=F32) * MLA_SCALE
            p = jnp.exp(sc - lse_ref[:, 0:1])
            if on_diagonal:
                p = jnp.where(lax.broadcasted_iota(jnp.int32, (t, t), 1) <= lax.broadcasted_iota(jnp.int32, (t, t), 0), p, 0.0)
            do_v = do_ref[...]
            do_b = do_v.astype(BF16)
            dp = lax.dot_general(do_b, v_ref[...], nt, preferred_element_type=F32)
            delta = jnp.sum(do_v * o_ref[...], axis=-1, keepdims=True)
            ds = (p * (dp - delta) * MLA_SCALE).astype(BF16)
            dv_sc[...] += lax.dot_general(p.astype(BF16), do_b, tn, preferred_element_type=F32)
            dk_sc[...] += lax.dot_general(ds, q_ref[...], tn, preferred_element_type=F32)
            rows = pl.ds(pl.multiple_of(qi * t, t), t)
            dq_sc[rows, :] += jnp.dot(ds, k_ref[...], preferred_element_type=F32)

        @pl.when(qi > kj)
        def _():
            pair(False)

        @pl.when(qi == kj)
        def _():
            pair(True)

        @pl.when(qi == n - 1)
        def _():
            dk_ref[...] = dk_sc[...].astype(dk_ref.dtype)
            dv_ref[...] = dv_sc[...].astype(dv_ref.dtype)

        @pl.when(jnp.logical_and(kj == n - 1, qi == n - 1))
        def _():
            dq_ref[...] = dq_sc[...].astype(dq_ref.dtype)

    qmap = lambda hh, j, i: (jnp.maximum(i, j), hh)
    kmap = lambda hh, j, i: (j, hh)
    return pl.pallas_call(
        body, out_shape=[jax.ShapeDtypeStruct(q.shape, BF16), jax.ShapeDtypeStruct(k.shape, BF16), jax.ShapeDtypeStruct(v.shape, BF16)],
        grid=(h, n, n),
        in_specs=[pl.BlockSpec((t, MLA_DQK), qmap), pl.BlockSpec((t, MLA_DQK), kmap), pl.BlockSpec((t, HEAD), kmap),
                  pl.BlockSpec((t, HEAD), qmap), pl.BlockSpec((t, HEAD), qmap), pl.BlockSpec((t, HEAD), qmap)],
        out_specs=[pl.BlockSpec((s, MLA_DQK), lambda hh, j, i: (0, hh)), pl.BlockSpec((t, MLA_DQK), kmap),
                   pl.BlockSpec((t, HEAD), kmap)],
        scratch_shapes=[pltpu.VMEM((s, MLA_DQK), F32), pltpu.VMEM((t, MLA_DQK), F32), pltpu.VMEM((t, HEAD), F32)],
        compiler_params=_cparams("parallel", "arbitrary", "arbitrary"), name="mla_attn_bwd")(q, k, v, o, lse, do)


@jax.custom_vjp
def mla_attention(q, k, v):
    return _mla_attn_fwd(q, k, v)[0]


def _mla_attention_fwd(q, k, v):
    o, lse = _mla_attn_fwd(q, k, v)
    return o, (q, k, v, o, lse)


def _mla_attention_bwd(saved, do):
    return tuple(_mla_attn_bwd(*saved, do))


mla_attention.defvjp(_mla_attention_fwd, _mla_attention_bwd)


DIL_SCALE = HEAD ** -0.5
GROUP_W = 4 * HEAD


def _dil_scores(q, kp, kc, n):
    dn = (((1,), (1,)), ((), ()))
    sp = lax.dot_general(q, kp, dn, preferred_element_type=F32) * DIL_SCALE
    sc = lax.dot_general(q, kc, dn, preferred_element_type=F32) * DIL_SCALE
    qi = lax.broadcasted_iota(jnp.int32, sp.shape, 0)
    kc_i = lax.broadcasted_iota(jnp.int32, sp.shape, 1)
    vp = jnp.logical_and(kc_i >= qi, n > 0)
    vc = kc_i <= qi
    return sp, sc, vp, vc


def _dil_specs(d):
    cur = pl.BlockSpec((DIL_BLOCK, GROUP_W), lambda r, n: (n, r))
    prev = pl.BlockSpec((DIL_BLOCK, GROUP_W), lambda r, n: (jnp.maximum(n - 1, 0), r))
    return cur, prev


def _dil_fwd(q, k, v, d):
    l = q.shape[0]
    nb = l // DIL_BLOCK
    cur, prev = _dil_specs(d)

    def body(q_ref, kp_ref, kc_ref, vp_ref, vc_ref, o_ref, lse_ref):
        n = pl.program_id(1)
        heads = range(4)
        sl = [slice(h * HEAD, (h + 1) * HEAD) for h in heads]
        scores = [_dil_scores(q_ref[:, sl[h]], kp_ref[:, sl[h]], kc_ref[:, sl[h]], n) for h in heads]
        sp = [jnp.where(vp, s_p, -jnp.inf) for s_p, _, vp, _ in scores]
        sc = [jnp.where(vc, s_c, -jnp.inf) for _, s_c, _, vc in scores]
        m = [jnp.maximum(jnp.max(sp[h], axis=-1, keepdims=True), jnp.max(sc[h], axis=-1, keepdims=True)) for h in heads]
        ep = [jnp.exp(sp[h] - m[h]) for h in heads]
        ec = [jnp.exp(sc[h] - m[h]) for h in heads]
        den = [jnp.sum(ep[h], axis=-1, keepdims=True) + jnp.sum(ec[h], axis=-1, keepdims=True) for h in heads]
        acc = [jnp.dot(ep[h].astype(BF16), vp_ref[:, sl[h]], preferred_element_type=F32)
               + jnp.dot(ec[h].astype(BF16), vc_ref[:, sl[h]], preferred_element_type=F32) for h in heads]
        for h in heads:
            o_ref[:, sl[h]] = acc[h] / den[h]
            lse_ref[:, sl[h]] = jnp.broadcast_to(m[h] + jnp.log(den[h]), (DIL_BLOCK, HEAD))

    return pl.pallas_call(
        body, out_shape=[jax.ShapeDtypeStruct(q.shape, F32)] * 2, grid=(d, nb),
        in_specs=[cur, prev, cur, prev, cur], out_specs=[cur, cur],
        compiler_params=_cparams("parallel", "parallel"), name=f"dil_fwd_d{d}")(q, k, k, v, v)


def _dil_bwd(q, k, v, o, lse, do, dlse, d):
    l = q.shape[0]
    nb = l // DIL_BLOCK
    cur, prev = _dil_specs(d)
    tn = (((0,), (0,)), ((), ()))
    nt = (((1,), (1,)), ((), ()))

    def body(q_ref, kp_ref, kc_ref, vp_ref, vc_ref, o_ref, lse_ref, do_ref, dl_ref,
             dq_ref, dkp_ref, dkc_ref, dvp_ref, dvc_ref):
        n = pl.program_id(1)
        heads = range(4)
        sl = [slice(h * HEAD, (h + 1) * HEAD) for h in heads]
        scores = [_dil_scores(q_ref[:, sl[h]], kp_ref[:, sl[h]], kc_ref[:, sl[h]], n) for h in heads]
        lse = [lse_ref[:, h * HEAD:h * HEAD + 1] for h in heads]
        pp = [jnp.where(scores[h][2], jnp.exp(scores[h][0] - lse[h]), 0.0) for h in heads]
        pc = [jnp.where(scores[h][3], jnp.exp(scores[h][1] - lse[h]), 0.0) for h in heads]
        do_b = [do_ref[:, sl[h]].astype(BF16) for h in heads]
        corr = [jnp.sum(dl_ref[:, sl[h]], axis=-1, keepdims=True)
                - jnp.sum(do_ref[:, sl[h]] * o_ref[:, sl[h]], axis=-1, keepdims=True) for h in heads]
        dsp = [(pp[h] * (lax.dot_general(do_b[h], vp_ref[:, sl[h]], nt, preferred_element_type=F32) + corr[h])
                * DIL_SCALE).astype(BF16) for h in heads]
        dsc = [(pc[h] * (lax.dot_general(do_b[h], vc_ref[:, sl[h]], nt, preferred_element_type=F32) + corr[h])
                * DIL_SCALE).astype(BF16) for h in heads]
        for h in heads:
            dq_ref[:, sl[h]] = (jnp.dot(dsp[h], kp_ref[:, sl[h]], preferred_element_type=F32)
                                + jnp.dot(dsc[h], kc_ref[:, sl[h]], preferred_element_type=F32)).astype(dq_ref.dtype)
            dkp_ref[:, sl[h]] = lax.dot_general(dsp[h], q_ref[:, sl[h]], tn, preferred_element_type=F32)
            dkc_ref[:, sl[h]] = lax.dot_general(dsc[h], q_ref[:, sl[h]], tn, preferred_element_type=F32)
            dvp_ref[:, sl[h]] = lax.dot_general(pp[h].astype(BF16), do_b[h], tn, preferred_element_type=F32)
            dvc_ref[:, sl[h]] = lax.dot_general(pc[h].astype(BF16), do_b[h], tn, preferred_element_type=F32)

    dq, dkp, dkc, dvp, dvc = pl.pallas_call(
        body, out_shape=[jax.ShapeDtypeStruct(q.shape, BF16)] + [jax.ShapeDtypeStruct(q.shape, F32)] * 4, grid=(d, nb),
        in_specs=[cur, prev, cur, prev, cur, cur, cur, cur, cur], out_specs=[cur] * 5,
        compiler_params=_cparams("parallel", "parallel"), name=f"dil_bwd_d{d}")(q, k, k, v, v, o, lse, do, dlse)

    def fold(cur_part, prev_part):
        shifted = jnp.concatenate([prev_part[DIL_BLOCK:], jnp.zeros((DIL_BLOCK, prev_part.shape[1]), F32)], axis=0)
        return (cur_part + shifted).astype(BF16)

    return dq, fold(dkc, dkp), fold(dvc, dvp)


def _make_dil(d):
    @jax.custom_vjp
    def att(q, k, v):
        return tuple(_dil_fwd(q, k, v, d))

    def fwd(q, k, v):
        o, lse = _dil_fwd(q, k, v, d)
        return (o, lse), (q, k, v, o, lse)

    def bwd(saved, cts):
        return _dil_bwd(*saved, cts[0], cts[1], d)

    att.defvjp(fwd, bwd)
    return att


def dilated_group(q, k, v, d):
    s = q.shape[0]
    view = lambda t: t.reshape(s // d, d * GROUP_W)
    o, lse = _make_dil(d)(view(q), view(k), view(v))
    return o.reshape(s, GROUP_W), lse.reshape(s, GROUP_W)


def _dn_post(c, kind, scale):
    m = _silu(c)
    nrm = m * lax.rsqrt(jnp.sum(m * m, axis=-1, keepdims=True) + 1e-6) * scale
    return kind * nrm + (1.0 - kind) * m


def _dn_kind_scale(j):
    kind = jnp.where(j < 2 * DN_HEADS, 1.0, 0.0).astype(F32)
    scale = jnp.where(j < DN_HEADS, DN_DIM ** -0.5, 1.0).astype(F32)
    return kind, scale


_CONV_RB = 512
_CONV_PAD = 8


def _conv_windows(pad_ref, w_ref, r0, rb, sign):
    acc = None
    for sh in range(DN_CONV):
        win = pad_ref[pl.ds(r0 + _CONV_PAD * (sign < 0) + sign * sh, rb), :]
        term = w_ref[DN_CONV - 1 - sh:DN_CONV - sh, :] * win
        acc = term if acc is None else acc + term
    return acc


def _dn_conv_fwd(x, w):
    s, width = x.shape
    rb = min(_CONV_RB, s)

    def body(x_ref, w_ref, o_ref, pad_ref):
        kind, scale = _dn_kind_scale(pl.program_id(0))
        pad_ref[0:_CONV_PAD, :] = jnp.zeros((_CONV_PAD, LANE), F32)
        pad_ref[_CONV_PAD:, :] = x_ref[...]
        for r0 in range(0, s, rb):
            c = _conv_windows(pad_ref, w_ref, r0, rb, -1)
            o_ref[r0:r0 + rb, :] = _dn_post(c, kind, scale)

    return pl.pallas_call(
        body, out_shape=jax.ShapeDtypeStruct(x.shape, F32), grid=(width // LANE,),
        in_specs=[pl.BlockSpec((s, LANE), lambda j: (0, j)), pl.BlockSpec((DN_CONV, LANE), lambda j: (0, j))],
        out_specs=pl.BlockSpec((s, LANE), lambda j: (0, j)), scratch_shapes=[pltpu.VMEM((s + _CONV_PAD, LANE), F32)],
        compiler_params=_cparams("parallel"), name="dn_conv_fwd")(x, w)


def _dn_conv_bwd(x, w, dy):
    s, width = x.shape
    rb = min(_CONV_RB, s)

    def body(x_ref, w_ref, dy_ref, dx_ref, dw_ref, pad_ref, dpad_ref):
        kind, scale = _dn_kind_scale(pl.program_id(0))
        pad_ref[0:_CONV_PAD, :] = jnp.zeros((_CONV_PAD, LANE), F32)
        pad_ref[_CONV_PAD:, :] = x_ref[...]
        dpad_ref[s:, :] = jnp.zeros((_CONV_PAD, LANE), F32)
        dws = [jnp.zeros((1, LANE), F32) for _ in range(DN_CONV)]
        for r0 in range(0, s, rb):
            c = _conv_windows(pad_ref, w_ref, r0, rb, -1)
            _, vjp = jax.vjp(lambda cc: _dn_post(cc, kind, scale), c)
            dc = vjp(dy_ref[r0:r0 + rb, :])[0]
            dpad_ref[r0:r0 + rb, :] = dc
            for sh in range(DN_CONV):
                win = pad_ref[pl.ds(r0 + _CONV_PAD - sh, rb), :]
                dws[DN_CONV - 1 - sh] = dws[DN_CONV - 1 - sh] + jnp.sum(dc * win, axis=0, keepdims=True)
        for j in range(DN_CONV):
            dw_ref[j:j + 1, :] = dws[j]
        for r0 in range(0, s, rb):
            dx_ref[r0:r0 + rb, :] = _conv_windows(dpad_ref, w_ref, r0, rb, 1)

    return pl.pallas_call(
        body, out_shape=[jax.ShapeDtypeStruct(x.shape, F32), jax.ShapeDtypeStruct(w.shape, F32)], grid=(width // LANE,),
        in_specs=[pl.BlockSpec((s, LANE), lambda j: (0, j)), pl.BlockSpec((DN_CONV, LANE), lambda j: (0, j)),
                  pl.BlockSpec((s, LANE), lambda j: (0, j))],
        out_specs=[pl.BlockSpec((s, LANE), lambda j: (0, j)), pl.BlockSpec((DN_CONV, LANE), lambda j: (0, j))],
        scratch_shapes=[pltpu.VMEM((s + _CONV_PAD, LANE), F32), pltpu.VMEM((s + _CONV_PAD, LANE), F32)],
        compiler_params=_cparams("parallel"), name="dn_conv_bwd")(x, w, dy)


@jax.custom_vjp
def dn_conv(x, w):
    return _dn_conv_fwd(x, w)


dn_conv.defvjp(lambda x, w: (_dn_conv_fwd(x, w), (x, w)), lambda saved, dy: tuple(_dn_conv_bwd(*saved, dy)))


_NN = (((1,), (0,)), ((), ()))
_NT = (((1,), (1,)), ((), ()))
_TN = (((0,), (0,)), ((), ()))


def _bd(a, b, dims, passes=1):
    d = lambda x, y: lax.dot_general(x, y, dims, preferred_element_type=F32)
    if passes == 0:
        return lax.dot_general(a, b, dims, precision=lax.Precision.HIGHEST, preferred_element_type=F32)
    ah, bh = a.astype(BF16), b.astype(BF16)
    if passes == 1:
        return d(ah, bh)
    al, bl = (a - ah.astype(F32)).astype(BF16), (b - bh.astype(F32)).astype(BF16)
    return d(ah, bh) + d(ah, bl) + d(al, bh)


@functools.partial(jax.custom_vjp, nondiff_argnums=(2, 3))
def _pdot(a, b, dims, passes):
    return _bd(a, b, dims, passes)


def _pdot_bwd(dims, passes, saved, g):
    a, b = saved
    if dims == _NN:
        return _bd(g, b, _NT, passes), _bd(a, g, _TN, passes)
    if dims == _NT:
        return _bd(g, b, _NN, passes), _bd(g, a, _TN, passes)
    return _bd(b, g, _NT, passes), _bd(a, g, _NN, passes)


_pdot.defvjp(lambda a, b, dims, passes: (_bd(a, b, dims, passes), (a, b)), _pdot_bwd)


GDN_DOT_PASSES = 1
GDN_SOLVE_PASSES = 3


def _hdot(a, b, dims=_NN):
    return _pdot(a, b, dims, GDN_DOT_PASSES)


def _xdot(a, b, dims=_NN):
    return _pdot(a, b, dims, GDN_SOLVE_PASSES)


def _split3(x):
    hi = x.astype(BF16)
    r1 = x - hi.astype(F32)
    mid = r1.astype(BF16)
    lo = (r1 - mid.astype(F32)).astype(BF16)
    return hi, mid, lo


def _tri_dot(tri, x, dims):
    t = tri.astype(BF16)
    return sum(lax.dot_general(t, p, dims, preferred_element_type=F32) for p in _split3(x))


@jax.custom_vjp
def _cumsum_rows(x):
    c = x.shape[0]
    tri = lax.broadcasted_iota(jnp.int32, (c, c), 0) >= lax.broadcasted_iota(jnp.int32, (c, c), 1)
    return _tri_dot(tri, x, _NN)


def _cumsum_rows_bwd(_, g):
    c = g.shape[0]
    tri = lax.broadcasted_iota(jnp.int32, (c, c), 0) >= lax.broadcasted_iota(jnp.int32, (c, c), 1)
    return (_tri_dot(tri, g, _TN),)


_cumsum_rows.defvjp(lambda x: (_cumsum_rows(x), None), _cumsum_rows_bwd)


def _gdn_prep_fn(qkv, gb):
    c = GDN_CHUNK
    row = lax.broadcasted_iota(jnp.int32, (c, c), 0)
    col = lax.broadcasted_iota(jnp.int32, (c, c), 1)
    incl, strict = row >= col, row > col
    lane = _lanes(gb)
    heads = range(DN_HEADS)
    q, k, v = qkv[:DN_HEADS], qkv[DN_HEADS:2 * DN_HEADS], qkv[2 * DN_HEADS:]
    g = [jnp.sum(jnp.where(lane == h, gb, 0.0), axis=-1, keepdims=True) for h in heads]
    beta = [jnp.sum(jnp.where(lane == DN_HEADS + h, gb, 0.0), axis=-1, keepdims=True) for h in heads]
    gcb = [_cumsum_rows(jnp.broadcast_to(g[h], (c, c))) for h in heads]
    decay = [jnp.where(incl, jnp.exp(jnp.where(incl, gcb[h] - gcb[h].T, 0.0)), 0.0) for h in heads]
    kb = [k[h] * beta[h] for h in heads]
    a = [jnp.where(strict, _hdot(kb[h], k[h], _NT) * decay[h], 0.0) for h in heads]
    pw = [-a[h] for h in heads]
    t = [(row == col).astype(F32) + pw[h] for h in heads]
    for _ in range(int(math.log2(c)) - 1):
        pw = [_xdot(pw[h], pw[h]) for h in heads]
        t = [t[h] + _xdot(t[h], pw[h]) for h in heads]
    eg = [jnp.exp(gcb[h]) for h in heads]
    u = [_xdot(t[h], v[h] * beta[h]) for h in heads]
    w = [_xdot(t[h], kb[h] * eg[h]) for h in heads]
    qk = [jnp.where(incl, _hdot(q[h], k[h], _NT) * decay[h], 0.0) for h in heads]
    g_last = [jnp.sum(jnp.where(row == c - 1, gcb[h], 0.0), axis=0, keepdims=True) for h in heads]
    kg = [k[h] * jnp.exp(g_last[h] - gcb[h]) for h in heads]
    qg = [q[h] * eg[h] for h in heads]
    el = [jnp.broadcast_to(jnp.exp(g_last[h]), (8, LANE)) for h in heads]
    return [u, w, qg, kg, qk, el]


def _gdn_prep_specs(nq):
    c = GDN_CHUNK
    big = pl.BlockSpec((c, nq * LANE), lambda n: (n, 0))
    return big


def _gdn_prep_fwd(qkv, gb):
    s = qkv.shape[0]
    c = GDN_CHUNK
    n = s // c
    hw = DN_HEADS * LANE

    def body(qkv_ref, gb_ref, u_ref, w_ref, qg_ref, kg_ref, qk_ref, el_ref):
        res = _gdn_prep_fn(_chunks(qkv_ref), gb_ref[...])
        for ref, chs in zip((u_ref, w_ref, qg_ref, kg_ref, qk_ref, el_ref), res):
            _store(ref, chs)

    row = pl.BlockSpec((c, hw), lambda i: (i, 0))
    return pl.pallas_call(
        body, out_shape=[jax.ShapeDtypeStruct((s, hw), F32)] * 5 + [jax.ShapeDtypeStruct((n * 8, hw), F32)], grid=(n,),
        in_specs=[pl.BlockSpec((c, 3 * hw), lambda i: (i, 0)), pl.BlockSpec((c, LANE), lambda i: (i, 0))],
        out_specs=[row] * 5 + [pl.BlockSpec((8, hw), lambda i: (i, 0))],
        compiler_params=_cparams("parallel"), name="gdn_prep_fwd")(qkv, gb)


def _gdn_prep_bwd(qkv, gb, cts):
    s = qkv.shape[0]
    c = GDN_CHUNK
    n = s // c
    hw = DN_HEADS * LANE

    def body(qkv_ref, gb_ref, du_ref, dw_ref, dqg_ref, dkg_ref, dqk_ref, del_ref, dqkv_ref, dgb_ref):
        _, vjp = jax.vjp(_gdn_prep_fn, _chunks(qkv_ref), gb_ref[...])
        ct = [_chunks(r) for r in (du_ref, dw_ref, dqg_ref, dkg_ref, dqk_ref, del_ref)]
        dqkv, dgb = vjp(ct)
        _store(dqkv_ref, dqkv)
        dgb_ref[...] = dgb

    row = pl.BlockSpec((c, hw), lambda i: (i, 0))
    return pl.pallas_call(
        body, out_shape=[jax.ShapeDtypeStruct(qkv.shape, F32), jax.ShapeDtypeStruct(gb.shape, F32)], grid=(n,),
        in_specs=[pl.BlockSpec((c, 3 * hw), lambda i: (i, 0)), pl.BlockSpec((c, LANE), lambda i: (i, 0))]
        + [row] * 5 + [pl.BlockSpec((8, hw), lambda i: (i, 0))],
        out_specs=[pl.BlockSpec((c, 3 * hw), lambda i: (i, 0)), pl.BlockSpec((c, LANE), lambda i: (i, 0))],
        compiler_params=_cparams("parallel"), name="gdn_prep_bwd")(qkv, gb, *cts)


@jax.custom_vjp
def gdn_prep(qkv, gb):
    return tuple(_gdn_prep_fwd(qkv, gb))


gdn_prep.defvjp(lambda qkv, gb: (tuple(_gdn_prep_fwd(qkv, gb)), (qkv, gb)),
                lambda saved, cts: tuple(_gdn_prep_bwd(*saved, cts)))


def _gdn_scan_fwd(u, w, qg, kg, qk, el):
    s = u.shape[0]
    c = GDN_CHUNK
    n = s // c
    hw = DN_HEADS * LANE

    def body(u_ref, w_ref, qg_ref, kg_ref, qk_ref, el_ref, o_ref, st_ref, s_sc):
        @pl.when(pl.program_id(0) == 0)
        def _():
            s_sc[...] = jnp.zeros_like(s_sc)

        heads = range(DN_HEADS)
        sl = [slice(h * LANE, (h + 1) * LANE) for h in heads]
        st = [s_sc[h] for h in heads]
        for h in heads:
            st_ref[sl[h], :] = st[h]
        v_new = [u_ref[:, sl[h]] - _hdot(w_ref[:, sl[h]], st[h]) for h in heads]
        o_st = [_hdot(qg_ref[:, sl[h]], st[h]) for h in heads]
        o_in = [_hdot(qk_ref[:, sl[h]], v_new[h]) for h in heads]
        s_up = [_hdot(kg_ref[:, sl[h]], v_new[h], _TN) for h in heads]
        for h in heads:
            o_ref[:, sl[h]] = o_st[h] + o_in[h]
            s_sc[h] = st[h] * el_ref[0:1, sl[h]] + s_up[h]

    row = pl.BlockSpec((c, hw), lambda i: (i, 0))
    return pl.pallas_call(
        body, out_shape=[jax.ShapeDtypeStruct((s, hw), F32), jax.ShapeDtypeStruct((n, hw, LANE), F32)], grid=(n,),
        in_specs=[row] * 5 + [pl.BlockSpec((8, hw), lambda i: (i, 0))],
        out_specs=[row, pl.BlockSpec((None, hw, LANE), lambda i: (i, 0, 0))],
        scratch_shapes=[pltpu.VMEM((DN_HEADS, LANE, LANE), F32)],
        compiler_params=_cparams("arbitrary"), name="gdn_scan_fwd")(u, w, qg, kg, qk, el)


def _gdn_scan_bwd(u, w, qg, kg, qk, el, states, do):
    s = u.shape[0]
    c = GDN_CHUNK
    n = s // c
    hw = DN_HEADS * LANE

    def body(u_ref, w_ref, qg_ref, kg_ref, qk_ref, el_ref, st_ref, do_ref,
             du_ref, dw_ref, dqg_ref, dkg_ref, dqk_ref, del_ref, ds_sc):
        @pl.when(pl.program_id(0) == 0)
        def _():
            ds_sc[...] = jnp.zeros_like(ds_sc)

        heads = range(DN_HEADS)
        sl = [slice(h * LANE, (h + 1) * LANE) for h in heads]
        st = [st_ref[sl[h], :] for h in heads]
        ds = [ds_sc[h] for h in heads]
        do = [do_ref[:, sl[h]] for h in heads]
        v_new = [u_ref[:, sl[h]] - _hdot(w_ref[:, sl[h]], st[h]) for h in heads]
        dv_new = [_hdot(qk_ref[:, sl[h]], do[h], _TN) + _hdot(kg_ref[:, sl[h]], ds[h]) for h in heads]
        first_row = lax.broadcasted_iota(jnp.int32, (8, LANE), 0) == 0
        for h in heads:
            du_ref[:, sl[h]] = dv_new[h]
            dw_ref[:, sl[h]] = -_hdot(dv_new[h], st[h], _NT)
            dqg_ref[:, sl[h]] = _hdot(do[h], st[h], _NT)
            dqk_ref[:, sl[h]] = _hdot(do[h], v_new[h], _NT)
            dkg_ref[:, sl[h]] = _hdot(v_new[h], ds[h], _NT)
            del_ref[:, sl[h]] = jnp.where(first_row, jnp.sum(st[h] * ds[h], axis=0, keepdims=True), 0.0)
        ds_new = [_hdot(qg_ref[:, sl[h]], do[h], _TN) + ds[h] * el_ref[0:1, sl[h]] - _hdot(w_ref[:, sl[h]], dv_new[h], _TN)
                  for h in heads]
        for h in heads:
            ds_sc[h] = ds_new[h]

    row = pl.BlockSpec((c, hw), lambda i: (n - 1 - i, 0))
    small = pl.BlockSpec((8, hw), lambda i: (n - 1 - i, 0))
    return pl.pallas_call(
        body, out_shape=[jax.ShapeDtypeStruct((s, hw), F32)] * 5 + [jax.ShapeDtypeStruct((n * 8, hw), F32)], grid=(n,),
        in_specs=[row] * 5 + [small, pl.BlockSpec((None, hw, LANE), lambda i: (n - 1 - i, 0, 0)), row],
        out_specs=[row] * 5 + [small], scratch_shapes=[pltpu.VMEM((DN_HEADS, LANE, LANE), F32)],
        compiler_params=_cparams("arbitrary"), name="gdn_scan_bwd")(u, w, qg, kg, qk, el, states, do)


@jax.custom_vjp
def gdn_scan(u, w, qg, kg, qk, el):
    return _gdn_scan_fwd(u, w, qg, kg, qk, el)[0]


def _gdn_scan_vfwd(*args):
    o, states = _gdn_scan_fwd(*args)
    return o, args + (states,)


gdn_scan.defvjp(_gdn_scan_vfwd, lambda saved, do: tuple(_gdn_scan_bwd(*saved, do)))


def _loss_call(y, t):
    s, d = y.shape
    br = min(ROW_BLOCK, s)
    n = s // br

    def body(y_ref, t_ref, loss_ref, dy_ref, acc):
        i = pl.program_id(0)

        @pl.when(i == 0)
        def _():
            acc[...] = jnp.zeros_like(acc)

        e = y_ref[...] - t_ref[...]
        dy_ref[...] = e / d
        acc[...] += jnp.sum(e * e, axis=0, keepdims=True)

        @pl.when(i == n - 1)
        def _():
            loss_ref[...] = jnp.broadcast_to(jnp.sum(acc[...], axis=1, keepdims=True) * (0.5 / d), loss_ref.shape)

    row = pl.BlockSpec((br, d), lambda i: (i, 0))
    return pl.pallas_call(
        body, out_shape=[jax.ShapeDtypeStruct((1, LANE), F32), jax.ShapeDtypeStruct((s, d), F32)], grid=(n,),
        in_specs=[row, row], out_specs=[pl.BlockSpec((1, LANE), lambda i: (0, 0)), row],
        scratch_shapes=[pltpu.VMEM((1, d), F32)], compiler_params=_cparams("arbitrary"), name="loss_head")(y, t)


@jax.custom_vjp
def loss_head(y, t):
    return _loss_call(y, t)[0][0, 0]


def _loss_head_fwd(y, t):
    loss, dy = _loss_call(y, t)
    return loss[0, 0], (dy,)


loss_head.defvjp(_loss_head_fwd, lambda saved, g: (saved[0] * g, -saved[0] * g))


IN_OFF = {}
_o = 0
for _name, _size in (('q_lat', 384), ('kv_lat', 320), ('z_a', 512), ('dn_qkv', 1536), ('dn_ab', 8), ('z_b', 512),
                     ('dil_qkv', 4608), ('z_c', 512), ('gate', 3072)):
    IN_OFF[_name] = (_o, _o + _size)
    _o += _size
IN_WIDTH = _o
N_CHIPS = 4


def _cols(w, name):
    a, b = IN_OFF[name]
    return w[:, a:b]


def _pad_cols(w, to):
    return jnp.concatenate([w, jnp.zeros((w.shape[0], to - w.shape[1]), w.dtype)], axis=1)


def _pad_row(v, to=None):
    v = v.reshape(1, -1)
    return v if to is None or v.shape[1] == to else _pad_cols(v, to)


def _shard_cols(pieces, a, b):
    wsh = pieces[0].shape[1]
    parts = [pieces[j][:, max(a, j * wsh) - j * wsh:min(b, (j + 1) * wsh) - j * wsh]
             for j in range(len(pieces)) if max(a, j * wsh) < min(b, (j + 1) * wsh)]
    return parts[0] if len(parts) == 1 else jnp.concatenate(parts, axis=1)


def _win_groups_impl(w_in4):
    out = []
    for l in range(w_in4.shape[1]):
        pieces = [w_in4[j, l] for j in range(w_in4.shape[0])]
        cols = lambda name: _shard_cols(pieces, *IN_OFF[name])
        out.append((_pad_cols(jnp.concatenate([cols('q_lat'), cols('kv_lat')], axis=1), 768),
                    jnp.concatenate([cols('z_a'), cols('z_b'), cols('z_c')], axis=1),
                    cols('dn_qkv'), _pad_cols(cols('dn_ab'), LANE), cols('dil_qkv'), cols('gate')))
    return tuple(out)


@jax.custom_vjp
def win_groups(w_in4):
    return _win_groups_impl(w_in4)


def _win_groups_bwd(_, cts):
    n_chip, depth = N_CHIPS, len(cts)
    wsh = IN_WIDTH // n_chip
    per_layer = []
    for l in range(depth):
        dmla, dz, dgdn, dab, ddil, dgate = cts[l]
        full = jnp.concatenate([dmla[:, :704], dz[:, :512], dgdn, dab[:, :8], dz[:, 512:1024], ddil, dz[:, 1024:], dgate], axis=1)
        per_layer.append([full[:, j * wsh:(j + 1) * wsh] for j in range(n_chip)])
    return (jnp.stack([jnp.stack([per_layer[l][j] for l in range(depth)]) for j in range(n_chip)]),)


win_groups.defvjp(lambda w: (_win_groups_impl(w), None), _win_groups_bwd)


def _layer(x, p, tabs):
    w_mla, w_z, w_gdn, w_ab, w_dil, w_gate = p['w_in_groups']
    h, = rowwise(f_rms_full, "rms_in", [x], [_pad_row(p['norm_g'])], [(D_MODEL, BF16)])
    mla_in = matmul(h, w_mla)
    z = matmul(h, w_z)
    dn_qkv = matmul(h, w_gdn)
    dn_ab = matmul(h, w_ab)
    dil_qkv = matmul(h, w_dil)
    gate = matmul(h, w_gate)

    qn, kvn, kpe = rowwise(f_mla1, "mla_norm", [mla_in], [_pad_row(p['mla_q_a_norm_g']), _pad_row(p['mla_kv_a_norm_g'])],
                           [(MLA_Q_RANK, BF16), (MLA_KV_RANK, BF16), (LANE, F32)])
    wq = p['mla_w_q_b']
    wq_pad = jnp.concatenate(
        [wq[:, hh * MLA_QK:hh * MLA_QK + MLA_NOPE] for hh in range(MLA_HEADS)]
        + [_pad_cols(wq[:, hh * MLA_QK + MLA_NOPE:(hh + 1) * MLA_QK], LANE) for hh in range(MLA_HEADS)], axis=1)
    q = matmul(qn, wq_pad)
    kv = matmul(kvn, p['mla_w_kv_b'])
    gq, gk = p['mla_q_norm_g'], p['mla_k_norm_g']
    q_att, k_att, v_att = rowwise(
        f_mla2, "mla_qk", [q, kv, kpe, tabs['cos_r'], tabs['sin_r']],
        [_pad_row(gq[:MLA_NOPE]), _pad_row(gq[MLA_NOPE:], LANE), _pad_row(gk[:MLA_NOPE]), _pad_row(gk[MLA_NOPE:], LANE)],
        [(MLA_HEADS * MLA_DQK, BF16), (MLA_HEADS * MLA_DQK, BF16), (MLA_HEADS * HEAD, BF16)], nograd=(3, 4))
    y_a = mla_attention(q_att, k_att, v_att)

    qkv_n = dn_conv(dn_qkv, p['dn_conv_w'])
    gb, = rowwise(f_gates, "dn_gates", [dn_ab], [_pad_row(p['dn_a_log'], LANE), _pad_row(p['dn_dt_bias'], LANE)], [(LANE, F32)])
    o_b = gdn_scan(*gdn_prep(qkv_n, gb))
    y_b, = rowwise(f_headnorm, "dn_out_norm", [o_b], [_pad_row(p['dn_out_norm_g'])], [(DN_HEADS * DN_DIM, F32)])

    qd, kd, vd = rowwise(f_dil, "dil_qk", [dil_qkv, tabs['cos_h'], tabs['sin_h']],
                         [_pad_row(p['dil_q_norm_g']), _pad_row(p['dil_k_norm_g'])],
                         [(3 * GROUP_W, BF16)] * 3, nograd=(1, 2))
    outs, lses = [], []
    for gi, d in enumerate(DIL_DILATIONS):
        sl = slice(gi * GROUP_W, (gi + 1) * GROUP_W)
        o_g, l_g = dilated_group(qd[:, sl], kd[:, sl], vd[:, sl], d)
        outs.append(o_g)
        lses.append(l_g)
    y_c, = rowwise(f_comb, "dil_comb", outs + lses, [], [(GROUP_W, F32)])

    ys = rowwise(f_merge1, "merge_silu", [y_a, y_b, y_c, z], [], [(BRANCH_W, BF16)] * 3)
    bo = [matmul(ys[b], p['w_branch'][b]) for b in range(3)]
    mixed, = rowwise(f_merge2, "merge_gate", [gate] + bo, [], [(D_MODEL, BF16)])
    return matmul(mixed, p['w_out'], res=x)


def _rope_tables(pos, dim):
    inv_freq = 1.0 / (ROPE_THETA ** (jnp.arange(0, dim, 2, dtype=F32) / dim))
    ang = pos.astype(F32)[:, None] * inv_freq
    return jnp.cos(ang), jnp.sin(ang)


def _tables(pos):
    cr, sr = _rope_tables(pos, MLA_ROPE)
    ch, sh = _rope_tables(pos, HEAD)
    zero = jnp.zeros((pos.shape[0], LANE - MLA_ROPE), F32)
    return {'cos_r': jnp.concatenate([cr, cr, zero], axis=1), 'sin_r': jnp.concatenate([sr, sr, zero], axis=1),
            'cos_h': jnp.concatenate([ch, ch], axis=1), 'sin_h': jnp.concatenate([sh, sh], axis=1)}


def _local_loss(w_in4, mats, conv_w, small, x, target, tabs):
    groups = win_groups(w_in4)
    for l in range(DEPTH):
        p = {k: v[l] for k, v in mats.items()}
        p.update({k: v[l] for k, v in small.items()})
        p['dn_conv_w'] = conv_w[l]
        p['w_in_groups'] = groups[l]
        x = _layer(x, p, tabs)
    return loss_head(x, target)


def _pack(arrays, dtype, row_tile):
    flat = jnp.concatenate([a.astype(dtype).reshape(-1) for a in arrays])
    rows = -(-flat.shape[0] // (LANE * row_tile)) * row_tile
    flat = jnp.concatenate([flat, jnp.zeros((rows * LANE - flat.shape[0],), dtype)])
    return flat.reshape(rows, LANE)


def _unpack_impl(buf, shapes):
    flat = buf.reshape(-1)
    out, off = [], 0
    for shp in shapes:
        n = math.prod(shp)
        out.append(flat[off:off + n].reshape(shp))
        off += n
    return tuple(out)


@functools.partial(jax.custom_vjp, nondiff_argnums=(1, 2, 3))
def _unpack_p(buf, shapes, dtype_name, rows):
    return _unpack_impl(buf, shapes)


_unpack_p.defvjp(lambda buf, shapes, dtype_name, rows: (_unpack_impl(buf, shapes), None),
                 lambda shapes, dtype_name, rows, _, cts: (_pack(cts, jnp.dtype(dtype_name), rows),))


def _unpack(buf, shapes):
    return _unpack_p(buf, tuple(shapes), jnp.dtype(buf.dtype).name, buf.shape[0])


def _full_from_chips(buf4, shard_shapes):
    per_chip = [_unpack(buf4[j], tuple(shard_shapes)) for j in range(4)]
    return {name: jnp.concatenate([per_chip[j][i] for j in range(4)], axis=axis) for i, (name, axis) in enumerate(MATS)}


_HBM = pl.BlockSpec(memory_space=pltpu.HBM)
_VMEM = pl.BlockSpec(memory_space=pltpu.VMEM)


def _chip_peers(x, y):
    return [(1 - x, y), (x, 1 - y), (1 - x, 1 - y)]


def chip_all_to_all(arrays):
    n = len(arrays)

    def body(*refs):
        in_refs, out_refs = refs[:n], refs[n:2 * n]
        send_sems, recv_sems, local_sems = refs[2 * n:]
        x, y, c = lax.axis_index("x"), lax.axis_index("y"), lax.axis_index("c")
        me = 2 * x + y
        peers = _chip_peers(x, y)
        local, sends = [], []
        for a, (i_ref, o_ref) in enumerate(zip(in_refs, out_refs)):
            local.append(pltpu.make_async_copy(i_ref.at[me], o_ref.at[me], local_sems.at[a]))
            local[-1].start()
            for k, (px, py) in enumerate(peers):
                sends.append(pltpu.make_async_remote_copy(
                    src_ref=i_ref.at[2 * px + py], dst_ref=o_ref.at[me], send_sem=send_sems.at[3 * a + k],
                    recv_sem=recv_sems.at[3 * a + k], device_id=(px, py, c), device_id_type=MESH))
                sends[-1].start()
        for a, (i_ref, o_ref) in enumerate(zip(in_refs, out_refs)):
            for k, (px, py) in enumerate(peers):
                pltpu.make_async_remote_copy(
                    src_ref=i_ref.at[me], dst_ref=o_ref.at[2 * px + py], send_sem=send_sems.at[3 * a + k],
                    recv_sem=recv_sems.at[3 * a + k], device_id=(px, py, c), device_id_type=MESH).wait_recv()
        for cp in sends:
            cp.wait_send()
        for cp in local:
            cp.wait()

    return pl.pallas_call(
        body, out_shape=[jax.ShapeDtypeStruct(a.shape, a.dtype) for a in arrays], in_specs=[_HBM] * n, out_specs=[_HBM] * n,
        scratch_shapes=[pltpu.SemaphoreType.DMA((3 * n,)), pltpu.SemaphoreType.DMA((3 * n,)), pltpu.SemaphoreType.DMA((n,))],
        name="chip_all_to_all")(*arrays)


def chip_all_gather(shards):
    n = len(shards)

    def body(*refs):
        in_refs, out_refs = refs[:n], refs[n:2 * n]
        send_sems, recv_sems = refs[2 * n:]
        x, y, c = lax.axis_index("x"), lax.axis_index("y"), lax.axis_index("c")
        me = 2 * x + y
        peers = _chip_peers(x, y)
        sends = []
        for a, (i_ref, o_ref) in enumerate(zip(in_refs, out_refs)):
            for k, (px, py) in enumerate(peers):
                sends.append(pltpu.make_async_remote_copy(
                    src_ref=i_ref.at[c], dst_ref=o_ref.at[me, c], send_sem=send_sems.at[6 * a + k],
                    recv_sem=recv_sems.at[6 * a + k], device_id=(px, py, c), device_id_type=MESH))
                sends[-1].start()
        for a, (i_ref, o_ref) in enumerate(zip(in_refs, out_refs)):
            for k, (px, py) in enumerate(peers):
                landed = o_ref.at[2 * px + py, c]
                pltpu.make_async_remote_copy(src_ref=i_ref.at[c], dst_ref=landed, send_sem=send_sems.at[6 * a + k],
                                             recv_sem=recv_sems.at[6 * a + k], device_id=(px, py, c), device_id_type=MESH).wait_recv()
                sends.append(pltpu.make_async_remote_copy(
                    src_ref=landed, dst_ref=landed, send_sem=send_sems.at[6 * a + 3 + k], recv_sem=recv_sems.at[6 * a + 3 + k],
                    device_id=(x, y, 1 - c), device_id_type=MESH))
                sends[-1].start()
        for a, (i_ref, o_ref) in enumerate(zip(in_refs, out_refs)):
            for k, (px, py) in enumerate(peers):
                other = o_ref.at[2 * px + py, 1 - c]
                pltpu.make_async_remote_copy(src_ref=other, dst_ref=other, send_sem=send_sems.at[6 * a + 3 + k],
                                             recv_sem=recv_sems.at[6 * a + 3 + k], device_id=(x, y, 1 - c),
                                             device_id_type=MESH).wait_recv()
        for cp in sends:
            cp.wait_send()

    me = 2 * lax.axis_index("x") + lax.axis_index("y")
    outs = pl.pallas_call(
        body, out_shape=[jax.ShapeDtypeStruct((4,) + a.shape, a.dtype) for a in shards], in_specs=[_HBM] * n, out_specs=[_HBM] * n,
        scratch_shapes=[pltpu.SemaphoreType.DMA((6 * n,)), pltpu.SemaphoreType.DMA((6 * n,))],
        name="chip_all_gather")(*shards)
    return [lax.dynamic_update_index_in_dim(o, a, me, 0) for o, a in zip(outs, shards)]


def sibling_swap(arrays):
    n = len(arrays)

    def body(*refs):
        in_refs, out_refs = refs[:n], refs[n:2 * n]
        send_sems, recv_sems = refs[2 * n:]
        x, y, c = lax.axis_index("x"), lax.axis_index("y"), lax.axis_index("c")
        cps = [pltpu.make_async_remote_copy(src_ref=i_ref, dst_ref=o_ref, send_sem=send_sems.at[a], recv_sem=recv_sems.at[a],
                                            device_id=(x, y, 1 - c), device_id_type=MESH)
               for a, (i_ref, o_ref) in enumerate(zip(in_refs, out_refs))]
        for cp in cps:
            cp.start()
        for cp in cps:
            cp.wait()

    return pl.pallas_call(
        body, out_shape=[jax.ShapeDtypeStruct(a.shape, a.dtype) for a in arrays], in_specs=[_HBM] * n, out_specs=[_HBM] * n,
        scratch_shapes=[pltpu.SemaphoreType.DMA((n,)), pltpu.SemaphoreType.DMA((n,))], name="sibling_swap")(*arrays)


def sibling_all_gather(halves):
    n = len(halves)

    def body(*refs):
        in_refs, out_refs = refs[:n], refs[n:2 * n]
        send_sems, recv_sems = refs[2 * n:]
        x, y, c = lax.axis_index("x"), lax.axis_index("y"), lax.axis_index("c")
        sends = [pltpu.make_async_remote_copy(src_ref=i_ref, dst_ref=o_ref.at[c], send_sem=send_sems.at[a], recv_sem=recv_sems.at[a],
                                              device_id=(x, y, 1 - c), device_id_type=MESH)
                 for a, (i_ref, o_ref) in enumerate(zip(in_refs, out_refs))]
        for cp in sends:
            cp.start()
        for a, (i_ref, o_ref) in enumerate(zip(in_refs, out_refs)):
            pltpu.make_async_remote_copy(src_ref=i_ref, dst_ref=o_ref.at[1 - c], send_sem=send_sems.at[a], recv_sem=recv_sems.at[a],
                                         device_id=(x, y, 1 - c), device_id_type=MESH).wait_recv()
        for cp in sends:
            cp.wait_send()

    outs = pl.pallas_call(
        body, out_shape=[jax.ShapeDtypeStruct((2,) + a.shape, a.dtype) for a in halves], in_specs=[_HBM] * n, out_specs=[_HBM] * n,
        scratch_shapes=[pltpu.SemaphoreType.DMA((n,)), pltpu.SemaphoreType.DMA((n,))], name="sibling_all_gather")(*halves)
    return [lax.dynamic_update_index_in_dim(o, a, lax.axis_index("c"), 0) for o, a in zip(outs, halves)]


def all_gather8(v, name):
    def body(v_ref, out_ref, send_sems, recv_sems):
        x, y, c = lax.axis_index("x"), lax.axis_index("y"), lax.axis_index("c")
        out_ref[4 * x + 2 * y + c] = v_ref[...]

        def peer(k):
            return (x ^ (k >> 2), y ^ ((k >> 1) & 1), c ^ (k & 1))

        sends = [pltpu.make_async_remote_copy(src_ref=v_ref, dst_ref=out_ref.at[4 * x + 2 * y + c], send_sem=send_sems.at[k - 1],
                                              recv_sem=recv_sems.at[k - 1], device_id=peer(k), device_id_type=MESH)
                 for k in range(1, 8)]
        for cp in sends:
            cp.start()
        for k in range(1, 8):
            px, py, pc = peer(k)
            pltpu.make_async_remote_copy(src_ref=v_ref, dst_ref=out_ref.at[4 * px + 2 * py + pc], send_sem=send_sems.at[k - 1],
                                         recv_sem=recv_sems.at[k - 1], device_id=peer(k), device_id_type=MESH).wait_recv()
        for cp in sends:
            cp.wait_send()

    return pl.pallas_call(
        body, out_shape=jax.ShapeDtypeStruct((8,) + v.shape, v.dtype), in_specs=[_VMEM], out_specs=_VMEM,
        scratch_shapes=[pltpu.SemaphoreType.DMA((7,)), pltpu.SemaphoreType.DMA((7,))], name=name)(v)


def pair_add(a, b, row_tile, name):
    rows, width = a.shape

    def body(a_ref, b_ref, o_ref):
        o_ref[...] = (a_ref[...].astype(F32) + b_ref[...].astype(F32)).astype(o_ref.dtype)

    spec = pl.BlockSpec((row_tile, width), lambda i: (i, 0))
    return pl.pallas_call(body, out_shape=jax.ShapeDtypeStruct(a.shape, a.dtype), grid=(rows // row_tile,),
                          in_specs=[spec, spec], out_specs=spec, compiler_params=_cparams("parallel"), name=name)(a, b)


def sum_blocks(blocks, row_tile, name):
    n, rows, width = blocks.shape

    def body(b_ref, o_ref):
        acc = b_ref[0].astype(F32)
        for j in range(1, n):
            acc = acc + b_ref[j].astype(F32)
        o_ref[...] = acc

    return pl.pallas_call(
        body, out_shape=jax.ShapeDtypeStruct((rows, width), F32), grid=(rows // row_tile,),
        in_specs=[pl.BlockSpec((n, row_tile, width), lambda i: (0, i, 0))],
        out_specs=pl.BlockSpec((row_tile, width), lambda i: (i, 0)), compiler_params=_cparams("parallel"), name=name)(blocks)


def adamw(g_parts, w, m, v, row_tile, name):
    npart = len(g_parts)

    def body(*refs):
        g = refs[0][...]
        for r in refs[1:npart]:
            g = g + r[...]
        w_ref, m_ref, v_ref, g_out, d_out, m_out, v_out = refs[npart:]
        m_new = ADAM_B1 * m_ref[...] + (1.0 - ADAM_B1) * g
        v_new = ADAM_B2 * v_ref[...] + (1.0 - ADAM_B2) * (g * g)
        m_hat = m_new / (1.0 - ADAM_B1 ** ADAM_STEP)
        v_hat = v_new / (1.0 - ADAM_B2 ** ADAM_STEP)
        g_out[...] = g
        d_out[...] = -ADAM_LR * (m_hat / (jnp.sqrt(v_hat) + ADAM_EPS) + ADAM_WD * w_ref[...])
        m_out[...] = m_new
        v_out[...] = v_new

    rows, width = w.shape
    spec = pl.BlockSpec((row_tile, width), lambda i: (i, 0))
    return pl.pallas_call(
        body, out_shape=[jax.ShapeDtypeStruct(w.shape, F32)] * 4, grid=(rows // row_tile,),
        in_specs=[spec] * (npart + 3), out_specs=[spec] * 4, compiler_params=_cparams("parallel"), name=name)(*g_parts, w, m, v)


def kernel(x, positions, norm_g, w_in, mla_q_a_norm_g, mla_w_q_b, mla_kv_a_norm_g, mla_w_kv_b, mla_q_norm_g, mla_k_norm_g, dn_conv_w, dn_a_log, dn_dt_bias, dn_out_norm_g, dil_q_norm_g, dil_k_norm_g, w_branch, w_out, loss_target, m_norm_g, m_w_in, m_mla_q_a_norm_g, m_mla_w_q_b, m_mla_kv_a_norm_g, m_mla_w_kv_b, m_mla_q_norm_g, m_mla_k_norm_g, m_dn_conv_w, m_dn_a_log, m_dn_dt_bias, m_dn_out_norm_g, m_dil_q_norm_g, m_dil_k_norm_g, m_w_branch, m_w_out, v_norm_g, v_w_in, v_mla_q_a_norm_g, v_mla_w_q_b, v_mla_kv_a_norm_g, v_mla_w_kv_b, v_mla_q_norm_g, v_mla_k_norm_g, v_dn_conv_w, v_dn_a_log, v_dn_dt_bias, v_dn_out_norm_g, v_dil_q_norm_g, v_dil_k_norm_g, v_w_branch, v_w_out):
    w = dict(norm_g=norm_g, w_in=w_in, mla_q_a_norm_g=mla_q_a_norm_g, mla_w_q_b=mla_w_q_b, mla_kv_a_norm_g=mla_kv_a_norm_g,
             mla_w_kv_b=mla_w_kv_b, mla_q_norm_g=mla_q_norm_g, mla_k_norm_g=mla_k_norm_g, dn_conv_w=dn_conv_w, dn_a_log=dn_a_log,
             dn_dt_bias=dn_dt_bias, dn_out_norm_g=dn_out_norm_g, dil_q_norm_g=dil_q_norm_g, dil_k_norm_g=dil_k_norm_g,
             w_branch=w_branch, w_out=w_out)
    m = dict(norm_g=m_norm_g, w_in=m_w_in, mla_q_a_norm_g=m_mla_q_a_norm_g, mla_w_q_b=m_mla_w_q_b, mla_kv_a_norm_g=m_mla_kv_a_norm_g,
             mla_w_kv_b=m_mla_w_kv_b, mla_q_norm_g=m_mla_q_norm_g, mla_k_norm_g=m_mla_k_norm_g, dn_conv_w=m_dn_conv_w,
             dn_a_log=m_dn_a_log, dn_dt_bias=m_dn_dt_bias, dn_out_norm_g=m_dn_out_norm_g, dil_q_norm_g=m_dil_q_norm_g,
             dil_k_norm_g=m_dil_k_norm_g, w_branch=m_w_branch, w_out=m_w_out)
    v = dict(norm_g=v_norm_g, w_in=v_w_in, mla_q_a_norm_g=v_mla_q_a_norm_g, mla_w_q_b=v_mla_w_q_b, mla_kv_a_norm_g=v_mla_kv_a_norm_g,
             mla_w_kv_b=v_mla_w_kv_b, mla_q_norm_g=v_mla_q_norm_g, mla_k_norm_g=v_mla_k_norm_g, dn_conv_w=v_dn_conv_w,
             dn_a_log=v_dn_a_log, dn_dt_bias=v_dn_dt_bias, dn_out_norm_g=v_dn_out_norm_g, dil_q_norm_g=v_dil_q_norm_g,
             dil_k_norm_g=v_dil_k_norm_g, w_branch=v_w_branch, w_out=v_w_out)
    chip = 2 * lax.axis_index("x") + lax.axis_index("y")
    mat_names = [n for n, _ in MATS]
    mat_shapes = tuple(w[n].shape for n in mat_names)
    conv_shard = dn_conv_w.shape
    win_shape = w_in.shape
    win_rows = (win_shape[0] * win_shape[1], win_shape[2])

    mats_sh = _pack([w[n] for n in mat_names], BF16, 2 * MAT_ROWS)
    mat_rows = mats_sh.shape[0]
    w_in4, mats4 = chip_all_gather([w_in.astype(BF16), mats_sh.reshape(2, mat_rows // 2, LANE)])
    mats4 = mats4.reshape(N_CHIPS, mat_rows, LANE)
    conv8 = all_gather8(_pack([dn_conv_w], F32, 8), "gather_conv_w")
    conv_full = jnp.concatenate([_unpack(conv8[2 * j], (conv_shard,))[0] for j in range(4)], axis=2)
    small = {n: w[n] for n in SMALL}
    tabs = _tables(positions[0])

    def loss_fn(w_in4, mats4, conv_full, small, xs):
        return _local_loss(w_in4, _full_from_chips(mats4, mat_shapes), conv_full, small, xs, loss_target[0], tabs)

    loss, (g_win4, g_mats4, g_conv, g_small, g_x) = jax.value_and_grad(loss_fn, argnums=(0, 1, 2, 3, 4))(
        w_in4, mats4, conv_full, small, x[0])
    loss = lax.psum(loss, ("x", "y", "c"))

    c_idx = lax.axis_index("c")
    half_win = (N_CHIPS * win_shape[1], win_shape[2])
    half_mats = (N_CHIPS * (mat_rows // 2), LANE)
    g_mats4 = g_mats4.reshape(N_CHIPS, 2, mat_rows // 2, LANE)
    pick = lambda g, i, shape: lax.dynamic_index_in_dim(g, i, axis=1, keepdims=False).reshape(shape)
    from_sib = sibling_swap([pick(g_win4, 1 - c_idx, half_win), pick(g_mats4, 1 - c_idx, half_mats)])
    s_win = pair_add(pick(g_win4, c_idx, half_win), from_sib[0], WIN_ROWS, "pair_add_w_in")
    s_mats = pair_add(pick(g_mats4, c_idx, half_mats), from_sib[1], MAT_ROWS, "pair_add_mats")
    r_win, r_mats = chip_all_to_all([s_win.reshape(N_CHIPS, win_shape[1], win_shape[2]),
                                     s_mats.reshape(N_CHIPS, mat_rows // 2, LANE)])
    g_win, g_mats = sibling_all_gather([sum_blocks(r_win, WIN_ROWS, "sum_chip_pieces_w_in"),
                                        sum_blocks(r_mats, MAT_ROWS, "sum_chip_pieces_mats")])
    res_win = adamw([g_win.reshape(win_rows)], *[d['w_in'].reshape(win_rows) for d in (w, m, v)], WIN_ROWS, "adamw_w_in")
    packed = [_pack([d[n] for n in mat_names], F32, 2 * MAT_ROWS) for d in (w, m, v)]
    res_big = [dict(zip(mat_names, _unpack(r, mat_shapes)))
               for r in adamw([g_mats.reshape(mat_rows, LANE)], *packed, MAT_ROWS, "adamw_mats")]
    for k in range(4):
        res_big[k]['w_in'] = res_win[k].reshape(win_shape)

    small_shapes = tuple(w[n].shape for n in SMALL) + (g_conv.shape,)
    g_all = sum_blocks(all_gather8(_pack([g_small[n] for n in SMALL] + [g_conv], F32, 8), "gather_small_grads"), 8, "sum_small")
    g_list = list(_unpack(g_all, small_shapes))
    g_list[-1] = lax.dynamic_slice_in_dim(g_list[-1], chip * conv_shard[2], conv_shard[2], axis=2)
    small_names = list(SMALL) + ['dn_conv_w']
    packed_s = [_pack([d[n] for n in small_names], F32, 8) for d in (w, m, v)]
    shapes_s = tuple(w[n].shape for n in small_names)
    res_small = [dict(zip(small_names, _unpack(r, shapes_s))) for r in adamw([_pack(g_list, F32, 8)], *packed_s, 8, "adamw_small")]

    outs = [loss, g_x[None]]
    for k in range(4):
        outs += [res_big[k][n] if n in res_big[k] else res_small[k][n] for n in WEIGHTS]
    return tuple(outs)
```

```python
import functools
import math

import jax
import jax.numpy as jnp
from jax import lax
from jax.experimental import pallas as pl
from jax.experimental.pallas import tpu as pltpu

F32 = jnp.float32
BF16 = jnp.bfloat16
HI = lax.Precision.HIGHEST
MESH = pl.DeviceIdType.MESH

LANE = 128
VMEM_LIMIT = 48 * 1024 * 1024
ROW_BLOCK = 256
MM_TM, MM_TN, MM_TK = 1024, 1024, 1024
MAT_ROWS = 512
WIN_ROWS = 128

RMS_EPS = 1e-6
ROPE_THETA = 10000.0
D_MODEL = 1024
DEPTH = 2
MLA_HEADS = 4
MLA_NOPE = 128
MLA_ROPE = 64
MLA_QK = MLA_NOPE + MLA_ROPE
MLA_Q_RANK = 384
MLA_KV_RANK = 256
DN_HEADS = 4
DN_DIM = 128
DN_CONV = 4
GDN_CHUNK = 128
DIL_WINDOWS = (128, 512, 2048)
DIL_DILATIONS = (1, 4, 16)
DIL_GROUPS = 3
DIL_BLOCK = 128
HEAD = 128
BRANCH_W = 512

ADAM_LR = 0.001
ADAM_B1 = 0.9
ADAM_B2 = 0.999
ADAM_EPS = 1e-08
ADAM_WD = 0.01
ADAM_STEP = 10

WEIGHTS = ['norm_g', 'w_in', 'mla_q_a_norm_g', 'mla_w_q_b', 'mla_kv_a_norm_g', 'mla_w_kv_b', 'mla_q_norm_g',
           'mla_k_norm_g', 'dn_conv_w', 'dn_a_log', 'dn_dt_bias', 'dn_out_norm_g', 'dil_q_norm_g', 'dil_k_norm_g',
           'w_branch', 'w_out']
MATS = (('mla_w_q_b', 2), ('mla_w_kv_b', 2), ('w_branch', 3), ('w_out', 1))
SMALL = ('norm_g', 'mla_q_a_norm_g', 'mla_kv_a_norm_g', 'mla_q_norm_g', 'mla_k_norm_g', 'dn_a_log', 'dn_dt_bias',
         'dn_out_norm_g', 'dil_q_norm_g', 'dil_k_norm_g')


def _cparams(*sem):
    return pltpu.CompilerParams(dimension_semantics=sem or None, vmem_limit_bytes=VMEM_LIMIT)


def _tile(dim, target):
    best = 0
    for t in range(LANE, min(dim, target) + 1, LANE):
        if dim % t == 0:
            best = t
    assert best, (dim, target)
    return best


def _mm(a, b, mode, out_dtype, res=None):
    if mode == 'nn':
        (m, k), (k2, n) = a.shape, b.shape
    elif mode == 'nt':
        (m, k), (n, k2) = a.shape, b.shape
    else:
        (k, m), (k2, n) = a.shape, b.shape
    assert k == k2, (a.shape, b.shape, mode)
    tm, tn, tk = _tile(m, MM_TM), _tile(n, MM_TN), _tile(k, MM_TK)
    nk = k // tk
    dims = {'nn': (((1,), (0,)), ((), ())), 'nt': (((1,), (1,)), ((), ())), 'tn': (((0,), (0,)), ((), ()))}[mode]

    def body(*refs):
        if res is None:
            a_ref, b_ref, o_ref = refs[:3]
        else:
            a_ref, b_ref, r_ref, o_ref = refs[:4]
        part = lax.dot_general(a_ref[...].astype(BF16), b_ref[...].astype(BF16), dims, preferred_element_type=F32)

        def finish(r):
            if res is not None:
                r = r + r_ref[...].astype(F32)
            o_ref[...] = r.astype(o_ref.dtype)

        if nk == 1:
            finish(part)
            return
        acc = refs[-1]
        kk = pl.program_id(2)

        @pl.when(kk == 0)
        def _():
            acc[...] = part

        @pl.when(jnp.logical_and(kk > 0, kk < nk - 1))
        def _():
            acc[...] += part

        @pl.when(kk == nk - 1)
        def _():
            finish(acc[...] + part)

    a_spec = pl.BlockSpec((tk, tm), lambda i, j, kk: (kk, i)) if mode == 'tn' else pl.BlockSpec((tm, tk), lambda i, j, kk: (i, kk))
    b_spec = pl.BlockSpec((tn, tk), lambda i, j, kk: (j, kk)) if mode == 'nt' else pl.BlockSpec((tk, tn), lambda i, j, kk: (kk, j))
    o_spec = pl.BlockSpec((tm, tn), lambda i, j, kk: (i, j))
    in_specs = [a_spec, b_spec] + ([o_spec] if res is not None else [])
    args = (a, b) + ((res,) if res is not None else ())
    return pl.pallas_call(
        body, out_shape=jax.ShapeDtypeStruct((m, n), out_dtype), grid=(m // tm, n // tn, nk),
        in_specs=in_specs, out_specs=o_spec, scratch_shapes=[pltpu.VMEM((tm, tn), F32)] if nk > 1 else [],
        compiler_params=_cparams("parallel", "parallel", "arbitrary"),
        name=f"mm_{mode}_{m}x{k}x{n}" + ("_res" if res is not None else ""))(*args)


def _make_matmul(out_dtype, with_res):
    @jax.custom_vjp
    def mm(a, b, *r):
        return _mm(a, b, 'nn', out_dtype, *r)

    def fwd(a, b, *r):
        return mm(a, b, *r), (a, b)

    def bwd(saved, g):
        a, b = saved
        da = _mm(g, b, 'nt', a.dtype)
        db = _mm(a, g, 'tn', b.dtype)
        return (da, db) + ((g,) if with_res else ())

    mm.defvjp(fwd, bwd)
    return mm


def matmul(a, b, out_dtype=F32, res=None):
    if res is None:
        return _make_matmul(out_dtype, False)(a, b)
    return _make_matmul(out_dtype, True)(a, b, res)


def _chunks(ref):
    return [ref[:, c * LANE:(c + 1) * LANE].astype(F32) for c in range(ref.shape[-1] // LANE)]


def _store(ref, chunks):
    for c, ch in enumerate(chunks):
        ref[:, c * LANE:(c + 1) * LANE] = ch.astype(ref.dtype)


def _row_spec(width, br, d=1):
    return pl.BlockSpec((br // d, d * width), lambda i: (i, 0))


def _par_spec(width):
    return pl.BlockSpec((1, width), lambda i: (0, 0))


def _load_rows(ref, scratch, d):
    if d == 1:
        return _chunks(ref)
    n, width = ref.shape[0], ref.shape[1] // d
    for c in range(width // LANE):
        for r in range(d):
            lanes = slice(r * width + c * LANE, r * width + (c + 1) * LANE)
            scratch[c, pl.ds(r, n, stride=d), :] = ref[:, lanes].astype(F32)
    return [scratch[c] for c in range(width // LANE)]


def _store_rows(ref, scratch, d, chunks):
    if d == 1:
        return _store(ref, chunks)
    n, width = ref.shape[0], ref.shape[1] // d
    for c, ch in enumerate(chunks):
        scratch[c] = ch
        for r in range(d):
            lanes = slice(r * width + c * LANE, r * width + (c + 1) * LANE)
            ref[:, lanes] = scratch[c, pl.ds(r, n, stride=d), :].astype(ref.dtype)


def _view_scratch(widths_dils, br):
    return [pltpu.VMEM((w // LANE, br, LANE), F32) for w, d in widths_dils if d > 1]


def _with_scratch(dils, scratch_refs):
    it = iter(scratch_refs)
    return [next(it) if d > 1 else None for d in dils]


def _rw_fwd(f, name, rows, params, outs, br, row_dil, out_dil):
    s = rows[0].shape[0] * row_dil[0]
    br = min(br, s)
    nr, npar, nout = len(rows), len(params), len(outs)
    row_w = [r.shape[1] // d for r, d in zip(rows, row_dil)]

    def body(*refs):
        scr = refs[nr + npar + nout:]
        n_in = sum(d > 1 for d in row_dil)
        rc = [_load_rows(r, sc, d) for r, sc, d in zip(refs[:nr], _with_scratch(row_dil, scr[:n_in]), row_dil)]
        pc = [_chunks(p) for p in refs[nr:nr + npar]]
        res = f(rc, pc)
        for o_ref, sc, d, chs in zip(refs[nr + npar:nr + npar + nout], _with_scratch(out_dil, scr[n_in:]), out_dil, res):
            _store_rows(o_ref, sc, d, chs)

    return pl.pallas_call(
        body, out_shape=[jax.ShapeDtypeStruct((s // d, d * w), dt) for (w, dt), d in zip(outs, out_dil)], grid=(s // br,),
        in_specs=[_row_spec(w, br, d) for w, d in zip(row_w, row_dil)] + [_par_spec(p.shape[1]) for p in params],
        out_specs=[_row_spec(w, br, d) for (w, _), d in zip(outs, out_dil)],
        scratch_shapes=_view_scratch(zip(row_w, row_dil), br) + _view_scratch([(w, d) for (w, _), d in zip(outs, out_dil)], br),
        compiler_params=_cparams("parallel"), name=name + "_fwd")(*rows, *params)


def _rw_bwd(f, name, rows, params, cts, nograd, br, row_dil, out_dil):
    s = rows[0].shape[0] * row_dil[0]
    br = min(br, s)
    nr, npar, nct = len(rows), len(params), len(cts)
    grad_rows = [i for i in range(nr) if i not in nograd]
    row_w = [r.shape[1] // d for r, d in zip(rows, row_dil)]
    ct_w = [c.shape[1] // d for c, d in zip(cts, out_dil)]
    grad_dil = [row_dil[i] for i in grad_rows]

    def body(*refs):
        n_out = len(grad_rows) + npar
        out_refs = refs[nr + npar + nct:nr + npar + nct + n_out]
        scr = refs[nr + npar + nct + n_out:]
        n_in, n_ct = sum(d > 1 for d in row_dil), sum(d > 1 for d in out_dil)
        rc = [_load_rows(r, sc, d) for r, sc, d in zip(refs[:nr], _with_scratch(row_dil, scr[:n_in]), row_dil)]
        pc = [_chunks(p) for p in refs[nr:nr + npar]]
        ct = [_load_rows(c, sc, d) for c, sc, d in
              zip(refs[nr + npar:nr + npar + nct], _with_scratch(out_dil, scr[n_in:n_in + n_ct]), out_dil)]
        _, vjp = jax.vjp(f, rc, pc)
        drc, dpc = vjp(ct)
        for o_ref, sc, d, i in zip(out_refs[:len(grad_rows)], _with_scratch(grad_dil, scr[n_in + n_ct:]), grad_dil, grad_rows):
            _store_rows(o_ref, sc, d, drc[i])
        i0 = pl.program_id(0)
        for o_ref, chs in zip(out_refs[len(grad_rows):], dpc):
            @pl.when(i0 == 0)
            def _(o_ref=o_ref):
                o_ref[...] = jnp.zeros_like(o_ref)
            for c, ch in enumerate(chs):
                o_ref[:, c * LANE:(c + 1) * LANE] += ch

    out_shape = ([jax.ShapeDtypeStruct(rows[i].shape, rows[i].dtype) for i in grad_rows]
                 + [jax.ShapeDtypeStruct(p.shape, F32) for p in params])
    out_specs = [_row_spec(row_w[i], br, row_dil[i]) for i in grad_rows] + [_par_spec(p.shape[1]) for p in params]
    res = pl.pallas_call(
        body, out_shape=out_shape, grid=(s // br,),
        in_specs=([_row_spec(w, br, d) for w, d in zip(row_w, row_dil)] + [_par_spec(p.shape[1]) for p in params]
                  + [_row_spec(w, br, d) for w, d in zip(ct_w, out_dil)]),
        out_specs=out_specs,
        scratch_shapes=(_view_scratch(zip(row_w, row_dil), br) + _view_scratch(zip(ct_w, out_dil), br)
                        + _view_scratch([(row_w[i], row_dil[i]) for i in grad_rows], br)),
        compiler_params=_cparams("arbitrary"), name=name + "_bwd")(*rows, *params, *cts)
    drows = [None] * nr
    for o, i in zip(res[:len(grad_rows)], grad_rows):
        drows[i] = o
    for i in nograd:
        drows[i] = jnp.zeros_like(rows[i])
    return tuple(drows), tuple(res[len(grad_rows):])


def rowwise(f, name, rows, params, outs, nograd=(), br=ROW_BLOCK, row_dil=None, out_dil=None):
    row_dil = tuple(row_dil or [1] * len(rows))
    out_dil = tuple(out_dil or [1] * len(outs))

    @jax.custom_vjp
    def op(rows, params):
        return tuple(_rw_fwd(f, name, rows, params, outs, br, row_dil, out_dil))

    def fwd(rows, params):
        return op(rows, params), (rows, params)

    def bwd(saved, cts):
        rows, params = saved
        return _rw_bwd(f, name, rows, params, list(cts), nograd, br, row_dil, out_dil)

    op.defvjp(fwd, bwd)
    return op(tuple(rows), tuple(params))


def _lane_roll(x, s):
    w = x.shape[-1]

    @jax.custom_vjp
    def r(v):
        return pltpu.roll(v, s, 1)

    r.defvjp(lambda v: (r(v), None), lambda _, g: (pltpu.roll(g, (w - s) % w, 1),))
    return r(x)


def _lanes(x):
    return lax.broadcasted_iota(jnp.int32, x.shape, 1)


def _rms(x, g, n=LANE):
    return x * lax.rsqrt(jnp.sum(x * x, axis=-1, keepdims=True) / n + RMS_EPS) * g


def _rope128(x, cos, sin):
    lane = _lanes(x)
    rot = _lane_roll(x, 64) * jnp.where(lane < 64, -1.0, 1.0)
    return x * cos + rot * sin


def _rope64(x, cos, sin):
    lane = _lanes(x)
    rot = jnp.where(lane < 32, -_lane_roll(x, 96), jnp.where(lane < 64, _lane_roll(x, 32), 0.0))
    return x * cos + rot * sin


def _silu(z):
    return z * jax.nn.sigmoid(z)


def _softplus(x):
    return jnp.maximum(x, 0.0) + jnp.log1p(jnp.exp(-jnp.abs(x)))


def f_rms_full(rc, pc):
    x, g = rc[0], pc[0]
    width = len(x) * LANE
    ms = sum(jnp.sum(c * c, axis=-1, keepdims=True) for c in x) / width
    r = lax.rsqrt(ms + RMS_EPS)
    return [[c * r * gc for c, gc in zip(x, g)]]


def f_mla1(rc, pc):
    x = rc[0]
    qn = f_rms_full([x[0:3]], [pc[0]])[0]
    kvn = f_rms_full([x[3:5]], [pc[1]])[0]
    return [qn, kvn, [x[5]]]


def f_mla2(rc, pc):
    q, kv, kpe, cos, sin = rc[0], rc[1], rc[2][0], rc[3][0], rc[4][0]
    gqn, gqp, gkn, gkp = pc[0][0], pc[1][0], pc[2][0], pc[3][0]
    k_pe = _rope64(_rms(kpe, gkp, MLA_ROPE), cos, sin)
    q_att, k_att, v = [], [], []
    for h in range(MLA_HEADS):
        q_att += [_rms(q[h], gqn), _rope64(_rms(q[MLA_HEADS + h], gqp, MLA_ROPE), cos, sin)]
        k_att += [_rms(kv[2 * h], gkn), k_pe]
        v.append(kv[2 * h + 1])
    return [q_att, k_att, v]


def f_gates(rc, pc):
    x, a_log, dt_bias = rc[0][0], pc[0][0], pc[1][0]
    lane = _lanes(x)
    g = -jnp.exp(a_log) * _softplus(x + dt_bias)
    return [[jnp.where(lane < DN_HEADS, g, jnp.where(lane < 2 * DN_HEADS, jax.nn.sigmoid(x), 0.0))]]


def f_headnorm(rc, pc):
    return [[_rms(c, pc[0][0]) for c in rc[0]]]


def f_dil(rc, pc):
    x, cos, sin = rc[0], rc[1][0], rc[2][0]
    gq, gk = pc[0][0], pc[1][0]
    n = len(x) // 3
    q = [_rope128(_rms(c, gq), cos, sin) for c in x[:n]]
    k = [_rope128(_rms(c, gk), cos, sin) for c in x[n:2 * n]]
    v = list(x[2 * n:])
    per = n // DIL_GROUPS
    return [t[g * per:(g + 1) * per] for g in range(DIL_GROUPS) for t in (q, k, v)]


def f_comb(rc, pc):
    o, l = rc[:DIL_GROUPS], rc[DIL_GROUPS:]
    out = []
    for c in range(len(o[0])):
        m = functools.reduce(jnp.maximum, [lg[c] for lg in l])
        e = [jnp.exp(lg[c] - m) for lg in l]
        den = sum(e)
        out.append(sum(eg * og[c] for eg, og in zip(e, o)) / den)
    return [out]


def f_merge1(rc, pc):
    z = rc[3]
    n = len(rc[0])
    return [[y * _silu(z[b * n + c]) for c, y in enumerate(rc[b])] for b in range(3)]


def f_merge2(rc, pc):
    gate = rc[0]
    n = len(rc[1])
    return [[sum(jax.nn.sigmoid(gate[b * n + c]) * rc[1 + b][c] for b in range(3)) for c in range(n)]]


MLA_DQK = 2 * LANE
MLA_SCALE = MLA_QK ** -0.5


def _mla_attn_fwd(q, k, v):
    s = q.shape[0]
    h = q.shape[1] // MLA_DQK
    t = min(512, s)
    n = s // t

    def body(q_ref, k_ref, v_ref, o_ref, lse_ref, m_sc, l_sc, acc_sc):
        qi, kj = pl.program_id(1), pl.program_id(2)

        @pl.when(kj == 0)
        def _():
            m_sc[...] = jnp.full_like(m_sc, -jnp.inf)
            l_sc[...] = jnp.zeros_like(l_sc)
            acc_sc[...] = jnp.zeros_like(acc_sc)

        nsub = 2 if t % 256 == 0 else 1
        ts = t // nsub
        rows = [slice(r * ts, (r + 1) * ts) for r in range(nsub)]

        def step(on_diagonal):
            sc = [lax.dot_general(q_ref[rw, :], k_ref[...], (((1,), (1,)), ((), ())), preferred_element_type=F32) * MLA_SCALE
                  for rw in rows]
            if on_diagonal:
                keep = [(lax.broadcasted_iota(jnp.int32, (ts, t), 1)
                         <= r * ts + lax.broadcasted_iota(jnp.int32, (ts, t), 0)) for r in range(nsub)]
                sc = [jnp.where(kp, x, -jnp.inf) for kp, x in zip(keep, sc)]
            m_old = [m_sc[rw, :] for rw in rows]
            m_new = [jnp.maximum(mo, jnp.max(x, axis=-1, keepdims=True)) for mo, x in zip(m_old, sc)]
            alpha = [jnp.exp(mo - mn) for mo, mn in zip(m_old, m_new)]
            p = [jnp.exp(x - mn) for x, mn in zip(sc, m_new)]
            pv = [jnp.dot(x.astype(BF16), v_ref[...], preferred_element_type=F32) for x in p]
            for r, rw in enumerate(rows):
                l_sc[rw, :] = alpha[r] * l_sc[rw, :] + jnp.sum(p[r], axis=-1, keepdims=True)
                acc_sc[rw, :] = alpha[r] * acc_sc[rw, :] + pv[r]
                m_sc[rw, :] = m_new[r]

        @pl.when(kj < qi)
        def _():
            step(False)

        @pl.when(kj == qi)
        def _():
            step(True)

        @pl.when(kj == n - 1)
        def _():
            o_ref[...] = acc_sc[...] / l_sc[...]
            lse_ref[...] = jnp.broadcast_to(m_sc[...] + jnp.log(l_sc[...]), lse_ref.shape)

    return pl.pallas_call(
        body, out_shape=[jax.ShapeDtypeStruct((s, h * HEAD), F32)] * 2, grid=(h, n, n),
        in_specs=[pl.BlockSpec((t, MLA_DQK), lambda hh, i, j: (i, hh)),
                  pl.BlockSpec((t, MLA_DQK), lambda hh, i, j: (jnp.minimum(j, i), hh)),
                  pl.BlockSpec((t, HEAD), lambda hh, i, j: (jnp.minimum(j, i), hh))],
        out_specs=[pl.BlockSpec((t, HEAD), lambda hh, i, j: (i, hh))] * 2,
        scratch_shapes=[pltpu.VMEM((t, 1), F32), pltpu.VMEM((t, 1), F32), pltpu.VMEM((t, HEAD), F32)],
        compiler_params=_cparams("parallel", "parallel", "arbitrary"), name="mla_attn_fwd")(q, k, v)


def _mla_attn_bwd(q, k, v, o, lse, do):
    s = q.shape[0]
    h = q.shape[1] // MLA_DQK
    t = min(512, s)
    n = s // t
    nt, tn = (((1,), (1,)), ((), ())), (((0,), (0,)), ((), ()))

    def body(q_ref, k_ref, v_ref, o_ref, lse_ref, do_ref, dq_ref, dk_ref, dv_ref, dq_sc, dk_sc, dv_sc):
        kj, qi = pl.program_id(1), pl.program_id(2)

        @pl.when(jnp.logical_and(kj == 0, qi == 0))
        def _():
            dq_sc[...] = jnp.zeros_like(dq_sc)

        @pl.when(qi == 0)
        def _():
            dk_sc[...] = jnp.zeros_like(dk_sc)
            dv_sc[...] = jnp.zeros_like(dv_sc)

        def pair(on_diagonal):
            sc = lax.dot_general(q_ref[...], k_ref[...], nt, preferred_element_type=F32) * MLA_SCALE
            p = jnp.exp(sc - lse_ref[:, 0:1])
            if on_diagonal:
                p = jnp.where(lax.broadcasted_iota(jnp.int32, (t, t), 1) <= lax.broadcasted_iota(jnp.int32, (t, t), 0), p, 0.0)
            do_v = do_ref[...]
            do_b = do_v.astype(BF16)
            dp = lax.dot_general(do_b, v_ref[...], nt, preferred_element_type=F32)
            delta = jnp.sum(do_v * o_ref[...], axis=-1, keepdims=True)
            ds = (p * (dp - delta) * MLA_SCALE).astype(BF16)
            dv_sc[...] += lax.dot_general(p.astype(BF16), do_b, tn, preferred_element_type=F32)
            dk_sc[...] += lax.dot_general(ds, q_ref[...], tn, preferred_element_type=F32)
            rows = pl.ds(pl.multiple_of(qi * t, t), t)
            dq_sc[rows, :] += jnp.dot(ds, k_ref[...], preferred_element_type=F32)

        @pl.when(qi > kj)
        def _():
            pair(False)

        @pl.when(qi == kj)
        def _():
            pair(True)

        @pl.when(qi == n - 1)
        def _():
            dk_ref[...] = dk_sc[...].astype(dk_ref.dtype)
            dv_ref[...] = dv_sc[...].astype(dv_ref.dtype)

        @pl.when(jnp.logical_and(kj == n - 1, qi == n - 1))
        def _():
            dq_ref[...] = dq_sc[...].astype(dq_ref.dtype)

    qmap = lambda hh, j, i: (jnp.maximum(i, j), hh)
    kmap = lambda hh, j, i: (j, hh)
    return pl.pallas_call(
        body, out_shape=[jax.ShapeDtypeStruct(q.shape, BF16), jax.ShapeDtypeStruct(k.shape, BF16), jax.ShapeDtypeStruct(v.shape, BF16)],
        grid=(h, n, n),
        in_specs=[pl.BlockSpec((t, MLA_DQK), qmap), pl.BlockSpec((t, MLA_DQK), kmap), pl.BlockSpec((t, HEAD), kmap),
                  pl.BlockSpec((t, HEAD), qmap), pl.BlockSpec((t, HEAD), qmap), pl.BlockSpec((t, HEAD), qmap)],
        out_specs=[pl.BlockSpec((s, MLA_DQK), lambda hh, j, i: (0, hh)), pl.BlockSpec((t, MLA_DQK), kmap),
                   pl.BlockSpec((t, HEAD), kmap)],
        scratch_shapes=[pltpu.VMEM((s, MLA_DQK), F32), pltpu.VMEM((t, MLA_DQK), F32), pltpu.VMEM((t, HEAD), F32)],
        compiler_params=_cparams("parallel", "arbitrary", "arbitrary"), name="mla_attn_bwd")(q, k, v, o, lse, do)


@jax.custom_vjp
def mla_attention(q, k, v):
    return _mla_attn_fwd(q, k, v)[0]


def _mla_attention_fwd(q, k, v):
    o, lse = _mla_attn_fwd(q, k, v)
    return o, (q, k, v, o, lse)


def _mla_attention_bwd(saved, do):
    return tuple(_mla_attn_bwd(*saved, do))


mla_attention.defvjp(_mla_attention_fwd, _mla_attention_bwd)


DIL_SCALE = HEAD ** -0.5
GROUP_W = 4 * HEAD


def _dil_scores(q, kp, kc, n):
    dn = (((1,), (1,)), ((), ()))
    sp = lax.dot_general(q, kp, dn, preferred_element_type=F32) * DIL_SCALE
    sc = lax.dot_general(q, kc, dn, preferred_element_type=F32) * DIL_SCALE
    qi = lax.broadcasted_iota(jnp.int32, sp.shape, 0)
    kc_i = lax.broadcasted_iota(jnp.int32, sp.shape, 1)
    vp = jnp.logical_and(kc_i >= qi, n > 0)
    vc = kc_i <= qi
    return sp, sc, vp, vc


def _dil_specs(d):
    cur = pl.BlockSpec((DIL_BLOCK, GROUP_W), lambda r, n: (n, r))
    prev = pl.BlockSpec((DIL_BLOCK, GROUP_W), lambda r, n: (jnp.maximum(n - 1, 0), r))
    return cur, prev


def _dil_fwd(q, k, v, d):
    l = q.shape[0]
    nb = l // DIL_BLOCK
    cur, prev = _dil_specs(d)

    def body(q_ref, kp_ref, kc_ref, vp_ref, vc_ref, o_ref, lse_ref):
        n = pl.program_id(1)
        heads = range(4)
        sl = [slice(h * HEAD, (h + 1) * HEAD) for h in heads]
        scores = [_dil_scores(q_ref[:, sl[h]], kp_ref[:, sl[h]], kc_ref[:, sl[h]], n) for h in heads]
        sp = [jnp.where(vp, s_p, -jnp.inf) for s_p, _, vp, _ in scores]
        sc = [jnp.where(vc, s_c, -jnp.inf) for _, s_c, _, vc in scores]
        m = [jnp.maximum(jnp.max(sp[h], axis=-1, keepdims=True), jnp.max(sc[h], axis=-1, keepdims=True)) for h in heads]
        ep = [jnp.exp(sp[h] - m[h]) for h in heads]
        ec = [jnp.exp(sc[h] - m[h]) for h in heads]
        den = [jnp.sum(ep[h], axis=-1, keepdims=True) + jnp.sum(ec[h], axis=-1, keepdims=True) for h in heads]
        acc = [jnp.dot(ep[h].astype(BF16), vp_ref[:, sl[h]], preferred_element_type=F32)
               + jnp.dot(ec[h].astype(BF16), vc_ref[:, sl[h]], preferred_element_type=F32) for h in heads]
        for h in heads:
            o_ref[:, sl[h]] = acc[h] / den[h]
            lse_ref[:, sl[h]] = jnp.broadcast_to(m[h] + jnp.log(den[h]), (DIL_BLOCK, HEAD))

    return pl.pallas_call(
        body, out_shape=[jax.ShapeDtypeStruct(q.shape, F32)] * 2, grid=(d, nb),
        in_specs=[cur, prev, cur, prev, cur], out_specs=[cur, cur],
        compiler_params=_cparams("parallel", "parallel"), name=f"dil_fwd_d{d}")(q, k, k, v, v)


def _dil_bwd(q, k, v, o, lse, do, dlse, d):
    l = q.shape[0]
    nb = l // DIL_BLOCK
    cur, prev = _dil_specs(d)
    tn = (((0,), (0,)), ((), ()))
    nt = (((1,), (1,)), ((), ()))

    def body(q_ref, kp_ref, kc_ref, vp_ref, vc_ref, o_ref, lse_ref, do_ref, dl_ref,
             dq_ref, dkp_ref, dkc_ref, dvp_ref, dvc_ref):
        n = pl.program_id(1)
        heads = range(4)
        sl = [slice(h * HEAD, (h + 1) * HEAD) for h in heads]
        scores = [_dil_scores(q_ref[:, sl[h]], kp_ref[:, sl[h]], kc_ref[:, sl[h]], n) for h in heads]
        lse = [lse_ref[:, h * HEAD:h * HEAD + 1] for h in heads]
        pp = [jnp.where(scores[h][2], jnp.exp(scores[h][0] - lse[h]), 0.0) for h in heads]
        pc = [jnp.where(scores[h][3], jnp.exp(scores[h][1] - lse[h]), 0.0) for h in heads]
        do_b = [do_ref[:, sl[h]].astype(BF16) for h in heads]
        corr = [jnp.sum(dl_ref[:, sl[h]], axis=-1, keepdims=True)
                - jnp.sum(do_ref[:, sl[h]] * o_ref[:, sl[h]], axis=-1, keepdims=True) for h in heads]
        dsp = [(pp[h] * (lax.dot_general(do_b[h], vp_ref[:, sl[h]], nt, preferred_element_type=F32) + corr[h])
                * DIL_SCALE).astype(BF16) for h in heads]
        dsc = [(pc[h] * (lax.dot_general(do_b[h], vc_ref[:, sl[h]], nt, preferred_element_type=F32) + corr[h])
                * DIL_SCALE).astype(BF16) for h in heads]
        for h in heads:
            dq_ref[:, sl[h]] = (jnp.dot(dsp[h], kp_ref[:, sl[h]], preferred_element_type=F32)
                                + jnp.dot(dsc[h], kc_ref[:, sl[h]], preferred_element_type=F32)).astype(dq_ref.dtype)
            dkp_ref[:, sl[h]] = lax.dot_general(dsp[h], q_ref[:, sl[h]], tn, preferred_element_type=F32)
            dkc_ref[:, sl[h]] = lax.dot_general(dsc[h], q_ref[:, sl[h]], tn, preferred_element_type=F32)
            dvp_ref[:, sl[h]] = lax.dot_general(pp[h].astype(BF16), do_b[h], tn, preferred_element_type=F32)
            dvc_ref[:, sl[h]] = lax.dot_general(pc[h].astype(BF16), do_b[h], tn, preferred_element_type=F32)

    dq, dkp, dkc, dvp, dvc = pl.pallas_call(
        body, out_shape=[jax.ShapeDtypeStruct(q.shape, BF16)] + [jax.ShapeDtypeStruct(q.shape, F32)] * 4, grid=(d, nb),
        in_specs=[cur, prev, cur, prev, cur, cur, cur, cur, cur], out_specs=[cur] * 5,
        compiler_params=_cparams("parallel", "parallel"), name=f"dil_bwd_d{d}")(q, k, k, v, v, o, lse, do, dlse)

    def fold(cur_part, prev_part):
        shifted = jnp.concatenate([prev_part[DIL_BLOCK:], jnp.zeros((DIL_BLOCK, prev_part.shape[1]), F32)], axis=0)
        return (cur_part + shifted).astype(BF16)

    return dq, fold(dkc, dkp), fold(dvc, dvp)


def _make_dil(d):
    @jax.custom_vjp
    def att(q, k, v):
        return tuple(_dil_fwd(q, k, v, d))

    def fwd(q, k, v):
        o, lse = _dil_fwd(q, k, v, d)
        return (o, lse), (q, k, v, o, lse)

    def bwd(saved, cts):
        return _dil_bwd(*saved, cts[0], cts[1], d)

    att.defvjp(fwd, bwd)
    return att


def dilated_group(q, k, v, d):
    return _make_dil(d)(q, k, v)


def _dn_post(c, kind, scale):
    m = _silu(c)
    nrm = m * lax.rsqrt(jnp.sum(m * m, axis=-1, keepdims=True) + 1e-6) * scale
    return kind * nrm + (1.0 - kind) * m


def _dn_kind_scale(j):
    kind = jnp.where(j < 2 * DN_HEADS, 1.0, 0.0).astype(F32)
    scale = jnp.where(j < DN_HEADS, DN_DIM ** -0.5, 1.0).astype(F32)
    return kind, scale


_CONV_RB = 512
_CONV_PAD = 8


def _conv_windows(pad_ref, w_ref, r0, rb, sign):
    acc = None
    for sh in range(DN_CONV):
        win = pad_ref[pl.ds(r0 + _CONV_PAD * (sign < 0) + sign * sh, rb), :]
        term = w_ref[DN_CONV - 1 - sh:DN_CONV - sh, :] * win
        acc = term if acc is None else acc + term
    return acc


def _dn_conv_fwd(x, w):
    s, width = x.shape
    rb = min(_CONV_RB, s)

    def body(x_ref, w_ref, o_ref, pad_ref):
        kind, scale = _dn_kind_scale(pl.program_id(0))
        pad_ref[0:_CONV_PAD, :] = jnp.zeros((_CONV_PAD, LANE), F32)
        pad_ref[_CONV_PAD:, :] = x_ref[...]
        for r0 in range(0, s, rb):
            c = _conv_windows(pad_ref, w_ref, r0, rb, -1)
            o_ref[r0:r0 + rb, :] = _dn_post(c, kind, scale)

    return pl.pallas_call(
        body, out_shape=jax.ShapeDtypeStruct(x.shape, F32), grid=(width // LANE,),
        in_specs=[pl.BlockSpec((s, LANE), lambda j: (0, j)), pl.BlockSpec((DN_CONV, LANE), lambda j: (0, j))],
        out_specs=pl.BlockSpec((s, LANE), lambda j: (0, j)), scratch_shapes=[pltpu.VMEM((s + _CONV_PAD, LANE), F32)],
        compiler_params=_cparams("parallel"), name="dn_conv_fwd")(x, w)


def _dn_conv_bwd(x, w, dy):
    s, width = x.shape
    rb = min(_CONV_RB, s)

    def body(x_ref, w_ref, dy_ref, dx_ref, dw_ref, pad_ref, dpad_ref):
        kind, scale = _dn_kind_scale(pl.program_id(0))
        pad_ref[0:_CONV_PAD, :] = jnp.zeros((_CONV_PAD, LANE), F32)
        pad_ref[_CONV_PAD:, :] = x_ref[...]
        dpad_ref[s:, :] = jnp.zeros((_CONV_PAD, LANE), F32)
        dws = [jnp.zeros((1, LANE), F32) for _ in range(DN_CONV)]
        for r0 in range(0, s, rb):
            c = _conv_windows(pad_ref, w_ref, r0, rb, -1)
            _, vjp = jax.vjp(lambda cc: _dn_post(cc, kind, scale), c)
            dc = vjp(dy_ref[r0:r0 + rb, :])[0]
            dpad_ref[r0:r0 + rb, :] = dc
            for sh in range(DN_CONV):
                win = pad_ref[pl.ds(r0 + _CONV_PAD - sh, rb), :]
                dws[DN_CONV - 1 - sh] = dws[DN_CONV - 1 - sh] + jnp.sum(dc * win, axis=0, keepdims=True)
        for j in range(DN_CONV):
            dw_ref[j:j + 1, :] = dws[j]
        for r0 in range(0, s, rb):
            dx_ref[r0:r0 + rb, :] = _conv_windows(dpad_ref, w_ref, r0, rb, 1)

    return pl.pallas_call(
        body, out_shape=[jax.ShapeDtypeStruct(x.shape, F32), jax.ShapeDtypeStruct(w.shape, F32)], grid=(width // LANE,),
        in_specs=[pl.BlockSpec((s, LANE), lambda j: (0, j)), pl.BlockSpec((DN_CONV, LANE), lambda j: (0, j)),
                  pl.BlockSpec((s, LANE), lambda j: (0, j))],
        out_specs=[pl.BlockSpec((s, LANE), lambda j: (0, j)), pl.BlockSpec((DN_CONV, LANE), lambda j: (0, j))],
        scratch_shapes=[pltpu.VMEM((s + _CONV_PAD, LANE), F32), pltpu.VMEM((s + _CONV_PAD, LANE), F32)],
        compiler_params=_cparams("parallel"), name="dn_conv_bwd")(x, w, dy)


@jax.custom_vjp
def dn_conv(x, w):
    return _dn_conv_fwd(x, w)


dn_conv.defvjp(lambda x, w: (_dn_conv_fwd(x, w), (x, w)), lambda saved, dy: tuple(_dn_conv_bwd(*saved, dy)))


_NN = (((1,), (0,)), ((), ()))
_NT = (((1,), (1,)), ((), ()))
_TN = (((0,), (0,)), ((), ()))


def _bd(a, b, dims, passes=1):
    d = lambda x, y: lax.dot_general(x, y, dims, preferred_element_type=F32)
    if passes == 0:
        return lax.dot_general(a, b, dims, precision=lax.Precision.HIGHEST, preferred_element_type=F32)
    ah, bh = a.astype(BF16), b.astype(BF16)
    if passes == 1:
        return d(ah, bh)
    al, bl = (a - ah.astype(F32)).astype(BF16), (b - bh.astype(F32)).astype(BF16)
    return d(ah, bh) + d(ah, bl) + d(al, bh)


@functools.partial(jax.custom_vjp, nondiff_argnums=(2, 3))
def _pdot(a, b, dims, passes):
    return _bd(a, b, dims, passes)


def _pdot_bwd(dims, passes, saved, g):
    a, b = saved
    if dims == _NN:
        return _bd(g, b, _NT, passes), _bd(a, g, _TN, passes)
    if dims == _NT:
        return _bd(g, b, _NN, passes), _bd(g, a, _TN, passes)
    return _bd(b, g, _NT, passes), _bd(a, g, _NN, passes)


_pdot.defvjp(lambda a, b, dims, passes: (_bd(a, b, dims, passes), (a, b)), _pdot_bwd)


GDN_DOT_PASSES = 1
GDN_SOLVE_PASSES = 3


def _hdot(a, b, dims=_NN):
    return _pdot(a, b, dims, GDN_DOT_PASSES)


def _xdot(a, b, dims=_NN):
    return _pdot(a, b, dims, GDN_SOLVE_PASSES)


def _split3(x):
    hi = x.astype(BF16)
    r1 = x - hi.astype(F32)
    mid = r1.astype(BF16)
    lo = (r1 - mid.astype(F32)).astype(BF16)
    return hi, mid, lo


def _tri_dot(tri, x, dims):
    t = tri.astype(BF16)
    return sum(lax.dot_general(t, p, dims, preferred_element_type=F32) for p in _split3(x))


@jax.custom_vjp
def _cumsum_rows(x):
    c = x.shape[0]
    tri = lax.broadcasted_iota(jnp.int32, (c, c), 0) >= lax.broadcasted_iota(jnp.int32, (c, c), 1)
    return _tri_dot(tri, x, _NN)


def _cumsum_rows_bwd(_, g):
    c = g.shape[0]
    tri = lax.broadcasted_iota(jnp.int32, (c, c), 0) >= lax.broadcasted_iota(jnp.int32, (c, c), 1)
    return (_tri_dot(tri, g, _TN),)


_cumsum_rows.defvjp(lambda x: (_cumsum_rows(x), None), _cumsum_rows_bwd)


def _gdn_prep_fn(qkv, gb):
    c = GDN_CHUNK
    row = lax.broadcasted_iota(jnp.int32, (c, c), 0)
    col = lax.broadcasted_iota(jnp.int32, (c, c), 1)
    incl, strict = row >= col, row > col
    lane = _lanes(gb)
    heads = range(DN_HEADS)
    q, k, v = qkv[:DN_HEADS], qkv[DN_HEADS:2 * DN_HEADS], qkv[2 * DN_HEADS:]
    g = [jnp.sum(jnp.where(lane == h, gb, 0.0), axis=-1, keepdims=True) for h in heads]
    beta = [jnp.sum(jnp.where(lane == DN_HEADS + h, gb, 0.0), axis=-1, keepdims=True) for h in heads]
    gcb = [_cumsum_rows(jnp.broadcast_to(g[h], (c, c))) for h in heads]
    decay = [jnp.where(incl, jnp.exp(jnp.where(incl, gcb[h] - gcb[h].T, 0.0)), 0.0) for h in heads]
    kb = [k[h] * beta[h] for h in heads]
    a = [jnp.where(strict, _hdot(kb[h], k[h], _NT) * decay[h], 0.0) for h in heads]
    pw = [-a[h] for h in heads]
    t = [(row == col).astype(F32) + pw[h] for h in heads]
    for _ in range(int(math.log2(c)) - 1):
        pw = [_xdot(pw[h], pw[h]) for h in heads]
        t = [t[h] + _xdot(t[h], pw[h]) for h in heads]
    eg = [jnp.exp(gcb[h]) for h in heads]
    u = [_xdot(t[h], v[h] * beta[h]) for h in heads]
    w = [_xdot(t[h], kb[h] * eg[h]) for h in heads]
    qk = [jnp.where(incl, _hdot(q[h], k[h], _NT) * decay[h], 0.0) for h in heads]
    g_last = [jnp.sum(jnp.where(row == c - 1, gcb[h], 0.0), axis=0, keepdims=True) for h in heads]
    kg = [k[h] * jnp.exp(g_last[h] - gcb[h]) for h in heads]
    qg = [q[h] * eg[h] for h in heads]
    el = [jnp.broadcast_to(jnp.exp(g_last[h]), (8, LANE)) for h in heads]
    return [u, w, qg, kg, qk, el]


def _gdn_prep_specs(nq):
    c = GDN_CHUNK
    big = pl.BlockSpec((c, nq * LANE), lambda n: (n, 0))
    return big


def _gdn_prep_fwd(qkv, gb):
    s = qkv.shape[0]
    c = GDN_CHUNK
    n = s // c
    hw = DN_HEADS * LANE

    def body(qkv_ref, gb_ref, u_ref, w_ref, qg_ref, kg_ref, qk_ref, el_ref):
        res = _gdn_prep_fn(_chunks(qkv_ref), gb_ref[...])
        for ref, chs in zip((u_ref, w_ref, qg_ref, kg_ref, qk_ref, el_ref), res):
            _store(ref, chs)

    row = pl.BlockSpec((c, hw), lambda i: (i, 0))
    return pl.pallas_call(
        body, out_shape=[jax.ShapeDtypeStruct((s, hw), F32)] * 5 + [jax.ShapeDtypeStruct((n * 8, hw), F32)], grid=(n,),
        in_specs=[pl.BlockSpec((c, 3 * hw), lambda i: (i, 0)), pl.BlockSpec((c, LANE), lambda i: (i, 0))],
        out_specs=[row] * 5 + [pl.BlockSpec((8, hw), lambda i: (i, 0))],
        compiler_params=_cparams("parallel"), name="gdn_prep_fwd")(qkv, gb)


def _gdn_prep_bwd(qkv, gb, cts):
    s = qkv.shape[0]
    c = GDN_CHUNK
    n = s // c
    hw = DN_HEADS * LANE

    def body(qkv_ref, gb_ref, du_ref, dw_ref, dqg_ref, dkg_ref, dqk_ref, del_ref, dqkv_ref, dgb_ref):
        _, vjp = jax.vjp(_gdn_prep_fn, _chunks(qkv_ref), gb_ref[...])
        ct = [_chunks(r) for r in (du_ref, dw_ref, dqg_ref, dkg_ref, dqk_ref, del_ref)]
        dqkv, dgb = vjp(ct)
        _store(dqkv_ref, dqkv)
        dgb_ref[...] = dgb

    row = pl.BlockSpec((c, hw), lambda i: (i, 0))
    return pl.pallas_call(
        body, out_shape=[jax.ShapeDtypeStruct(qkv.shape, F32), jax.ShapeDtypeStruct(gb.shape, F32)], grid=(n,),
        in_specs=[pl.BlockSpec((c, 3 * hw), lambda i: (i, 0)), pl.BlockSpec((c, LANE), lambda i: (i, 0))]
        + [row] * 5 + [pl.BlockSpec((8, hw), lambda i: (i, 0))],
        out_specs=[pl.BlockSpec((c, 3 * hw), lambda i: (i, 0)), pl.BlockSpec((c, LANE), lambda i: (i, 0))],
        compiler_params=_cparams("parallel"), name="gdn_prep_bwd")(qkv, gb, *cts)


@jax.custom_vjp
def gdn_prep(qkv, gb):
    return tuple(_gdn_prep_fwd(qkv, gb))


gdn_prep.defvjp(lambda qkv, gb: (tuple(_gdn_prep_fwd(qkv, gb)), (qkv, gb)),
                lambda saved, cts: tuple(_gdn_prep_bwd(*saved, cts)))


def _gdn_scan_fwd(u, w, qg, kg, qk, el):
    s = u.shape[0]
    c = GDN_CHUNK
    n = s // c
    hw = DN_HEADS * LANE

    def body(u_ref, w_ref, qg_ref, kg_ref, qk_ref, el_ref, o_ref, st_ref, s_sc):
        @pl.when(pl.program_id(0) == 0)
        def _():
            s_sc[...] = jnp.zeros_like(s_sc)

        heads = range(DN_HEADS)
        sl = [slice(h * LANE, (h + 1) * LANE) for h in heads]
        st = [s_sc[h] for h in heads]
        for h in heads:
            st_ref[sl[h], :] = st[h]
        v_new = [u_ref[:, sl[h]] - _hdot(w_ref[:, sl[h]], st[h]) for h in heads]
        o_st = [_hdot(qg_ref[:, sl[h]], st[h]) for h in heads]
        o_in = [_hdot(qk_ref[:, sl[h]], v_new[h]) for h in heads]
        s_up = [_hdot(kg_ref[:, sl[h]], v_new[h], _TN) for h in heads]
        for h in heads:
            o_ref[:, sl[h]] = o_st[h] + o_in[h]
            s_sc[h] = st[h] * el_ref[0:1, sl[h]] + s_up[h]

    row = pl.BlockSpec((c, hw), lambda i: (i, 0))
    return pl.pallas_call(
        body, out_shape=[jax.ShapeDtypeStruct((s, hw), F32), jax.ShapeDtypeStruct((n, hw, LANE), F32)], grid=(n,),
        in_specs=[row] * 5 + [pl.BlockSpec((8, hw), lambda i: (i, 0))],
        out_specs=[row, pl.BlockSpec((None, hw, LANE), lambda i: (i, 0, 0))],
        scratch_shapes=[pltpu.VMEM((DN_HEADS, LANE, LANE), F32)],
        compiler_params=_cparams("arbitrary"), name="gdn_scan_fwd")(u, w, qg, kg, qk, el)


def _gdn_scan_bwd(u, w, qg, kg, qk, el, states, do):
    s = u.shape[0]
    c = GDN_CHUNK
    n = s // c
    hw = DN_HEADS * LANE

    def body(u_ref, w_ref, qg_ref, kg_ref, qk_ref, el_ref, st_ref, do_ref,
             du_ref, dw_ref, dqg_ref, dkg_ref, dqk_ref, del_ref, ds_sc):
        @pl.when(pl.program_id(0) == 0)
        def _():
            ds_sc[...] = jnp.zeros_like(ds_sc)

        heads = range(DN_HEADS)
        sl = [slice(h * LANE, (h + 1) * LANE) for h in heads]
        st = [st_ref[sl[h], :] for h in heads]
        ds = [ds_sc[h] for h in heads]
        do = [do_ref[:, sl[h]] for h in heads]
        v_new = [u_ref[:, sl[h]] - _hdot(w_ref[:, sl[h]], st[h]) for h in heads]
        dv_new = [_hdot(qk_ref[:, sl[h]], do[h], _TN) + _hdot(kg_ref[:, sl[h]], ds[h]) for h in heads]
        first_row = lax.broadcasted_iota(jnp.int32, (8, LANE), 0) == 0
        for h in heads:
            du_ref[:, sl[h]] = dv_new[h]
            dw_ref[:, sl[h]] = -_hdot(dv_new[h], st[h], _NT)
            dqg_ref[:, sl[h]] = _hdot(do[h], st[h], _NT)
            dqk_ref[:, sl[h]] = _hdot(do[h], v_new[h], _NT)
            dkg_ref[:, sl[h]] = _hdot(v_new[h], ds[h], _NT)
            del_ref[:, sl[h]] = jnp.where(first_row, jnp.sum(st[h] * ds[h], axis=0, keepdims=True), 0.0)
        ds_new = [_hdot(qg_ref[:, sl[h]], do[h], _TN) + ds[h] * el_ref[0:1, sl[h]] - _hdot(w_ref[:, sl[h]], dv_new[h], _TN)
                  for h in heads]
        for h in heads:
            ds_sc[h] = ds_new[h]

    row = pl.BlockSpec((c, hw), lambda i: (n - 1 - i, 0))
    small = pl.BlockSpec((8, hw), lambda i: (n - 1 - i, 0))
    return pl.pallas_call(
        body, out_shape=[jax.ShapeDtypeStruct((s, hw), F32)] * 5 + [jax.ShapeDtypeStruct((n * 8, hw), F32)], grid=(n,),
        in_specs=[row] * 5 + [small, pl.BlockSpec((None, hw, LANE), lambda i: (n - 1 - i, 0, 0)), row],
        out_specs=[row] * 5 + [small], scratch_shapes=[pltpu.VMEM((DN_HEADS, LANE, LANE), F32)],
        compiler_params=_cparams("arbitrary"), name="gdn_scan_bwd")(u, w, qg, kg, qk, el, states, do)


@jax.custom_vjp
def gdn_scan(u, w, qg, kg, qk, el):
    return _gdn_scan_fwd(u, w, qg, kg, qk, el)[0]


def _gdn_scan_vfwd(*args):
    o, states = _gdn_scan_fwd(*args)
    return o, args + (states,)


gdn_scan.defvjp(_gdn_scan_vfwd, lambda saved, do: tuple(_gdn_scan_bwd(*saved, do)))


def _loss_call(y, t):
    s, d = y.shape
    br = min(ROW_BLOCK, s)
    n = s // br

    def body(y_ref, t_ref, loss_ref, dy_ref, acc):
        i = pl.program_id(0)

        @pl.when(i == 0)
        def _():
            acc[...] = jnp.zeros_like(acc)

        e = y_ref[...] - t_ref[...]
        dy_ref[...] = e / d
        acc[...] += jnp.sum(e * e, axis=0, keepdims=True)

        @pl.when(i == n - 1)
        def _():
            loss_ref[...] = jnp.broadcast_to(jnp.sum(acc[...], axis=1, keepdims=True) * (0.5 / d), loss_ref.shape)

    row = pl.BlockSpec((br, d), lambda i: (i, 0))
    return pl.pallas_call(
        body, out_shape=[jax.ShapeDtypeStruct((1, LANE), F32), jax.ShapeDtypeStruct((s, d), F32)], grid=(n,),
        in_specs=[row, row], out_specs=[pl.BlockSpec((1, LANE), lambda i: (0, 0)), row],
        scratch_shapes=[pltpu.VMEM((1, d), F32)], compiler_params=_cparams("arbitrary"), name="loss_head")(y, t)


@jax.custom_vjp
def loss_head(y, t):
    return _loss_call(y, t)[0][0, 0]


def _loss_head_fwd(y, t):
    loss, dy = _loss_call(y, t)
    return loss[0, 0], (dy,)


loss_head.defvjp(_loss_head_fwd, lambda saved, g: (saved[0] * g, -saved[0] * g))


IN_OFF = {}
_o = 0
for _name, _size in (('q_lat', 384), ('kv_lat', 320), ('z_a', 512), ('dn_qkv', 1536), ('dn_ab', 8), ('z_b', 512),
                     ('dil_qkv', 4608), ('z_c', 512), ('gate', 3072)):
    IN_OFF[_name] = (_o, _o + _size)
    _o += _size
IN_WIDTH = _o
N_CHIPS = 4


def _cols(w, name):
    a, b = IN_OFF[name]
    return w[:, a:b]


def _pad_cols(w, to):
    return jnp.concatenate([w, jnp.zeros((w.shape[0], to - w.shape[1]), w.dtype)], axis=1)


def _pad_row(v, to=None):
    v = v.reshape(1, -1)
    return v if to is None or v.shape[1] == to else _pad_cols(v, to)


def _shard_cols(pieces, a, b):
    wsh = pieces[0].shape[1]
    parts = [pieces[j][:, max(a, j * wsh) - j * wsh:min(b, (j + 1) * wsh) - j * wsh]
             for j in range(len(pieces)) if max(a, j * wsh) < min(b, (j + 1) * wsh)]
    return parts[0] if len(parts) == 1 else jnp.concatenate(parts, axis=1)


def _win_groups_impl(w_in4):
    out = []
    for l in range(w_in4.shape[1]):
        pieces = [w_in4[j, l] for j in range(w_in4.shape[0])]
        cols = lambda name: _shard_cols(pieces, *IN_OFF[name])
        out.append((_pad_cols(jnp.concatenate([cols('q_lat'), cols('kv_lat')], axis=1), 768),
                    jnp.concatenate([cols('z_a'), cols('z_b'), cols('z_c')], axis=1),
                    cols('dn_qkv'), _pad_cols(cols('dn_ab'), LANE), cols('dil_qkv'), cols('gate')))
    return tuple(out)


@jax.custom_vjp
def win_groups(w_in4):
    return _win_groups_impl(w_in4)


def _win_groups_bwd(_, cts):
    n_chip, depth = N_CHIPS, len(cts)
    wsh = IN_WIDTH // n_chip
    per_layer = []
    for l in range(depth):
        dmla, dz, dgdn, dab, ddil, dgate = cts[l]
        full = jnp.concatenate([dmla[:, :704], dz[:, :512], dgdn, dab[:, :8], dz[:, 512:1024], ddil, dz[:, 1024:], dgate], axis=1)
        per_layer.append([full[:, j * wsh:(j + 1) * wsh] for j in range(n_chip)])
    return (jnp.stack([jnp.stack([per_layer[l][j] for l in range(depth)]) for j in range(n_chip)]),)


win_groups.defvjp(lambda w: (_win_groups_impl(w), None), _win_groups_bwd)


def _layer(x, p, tabs):
    w_mla, w_z, w_gdn, w_ab, w_dil, w_gate = p['w_in_groups']
    h, = rowwise(f_rms_full, "rms_in", [x], [_pad_row(p['norm_g'])], [(D_MODEL, BF16)])
    mla_in = matmul(h, w_mla)
    z = matmul(h, w_z)
    dn_qkv = matmul(h, w_gdn)
    dn_ab = matmul(h, w_ab)
    dil_qkv = matmul(h, w_dil)
    gate = matmul(h, w_gate)

    qn, kvn, kpe = rowwise(f_mla1, "mla_norm", [mla_in], [_pad_row(p['mla_q_a_norm_g']), _pad_row(p['mla_kv_a_norm_g'])],
                           [(MLA_Q_RANK, BF16), (MLA_KV_RANK, BF16), (LANE, F32)])
    wq = p['mla_w_q_b']
    wq_pad = jnp.concatenate(
        [wq[:, hh * MLA_QK:hh * MLA_QK + MLA_NOPE] for hh in range(MLA_HEADS)]
        + [_pad_cols(wq[:, hh * MLA_QK + MLA_NOPE:(hh + 1) * MLA_QK], LANE) for hh in range(MLA_HEADS)], axis=1)
    q = matmul(qn, wq_pad)
    kv = matmul(kvn, p['mla_w_kv_b'])
    gq, gk = p['mla_q_norm_g'], p['mla_k_norm_g']
    q_att, k_att, v_att = rowwise(
        f_mla2, "mla_qk", [q, kv, kpe, tabs['cos_r'], tabs['sin_r']],
        [_pad_row(gq[:MLA_NOPE]), _pad_row(gq[MLA_NOPE:], LANE), _pad_row(gk[:MLA_NOPE]), _pad_row(gk[MLA_NOPE:], LANE)],
        [(MLA_HEADS * MLA_DQK, BF16), (MLA_HEADS * MLA_DQK, BF16), (MLA_HEADS * HEAD, BF16)], nograd=(3, 4))
    y_a = mla_attention(q_att, k_att, v_att)

    qkv_n = dn_conv(dn_qkv, p['dn_conv_w'])
    gb, = rowwise(f_gates, "dn_gates", [dn_ab], [_pad_row(p['dn_a_log'], LANE), _pad_row(p['dn_dt_bias'], LANE)], [(LANE, F32)])
    o_b = gdn_scan(*gdn_prep(qkv_n, gb))
    y_b, = rowwise(f_headnorm, "dn_out_norm", [o_b], [_pad_row(p['dn_out_norm_g'])], [(DN_HEADS * DN_DIM, F32)])

    qkv_d = rowwise(f_dil, "dil_qk", [dil_qkv, tabs['cos_h'], tabs['sin_h']],
                    [_pad_row(p['dil_q_norm_g']), _pad_row(p['dil_k_norm_g'])],
                    [(GROUP_W, BF16)] * (3 * DIL_GROUPS), nograd=(1, 2), out_dil=[d for d in DIL_DILATIONS for _ in range(3)])
    outs, lses = [], []
    for gi, d in enumerate(DIL_DILATIONS):
        o_g, l_g = dilated_group(*qkv_d[3 * gi:3 * gi + 3], d)
        outs.append(o_g)
        lses.append(l_g)
    y_c, = rowwise(f_comb, "dil_comb", outs + lses, [], [(GROUP_W, F32)], row_dil=list(DIL_DILATIONS) * 2)

    ys = rowwise(f_merge1, "merge_silu", [y_a, y_b, y_c, z], [], [(BRANCH_W, BF16)] * 3)
    bo = [matmul(ys[b], p['w_branch'][b]) for b in range(3)]
    mixed, = rowwise(f_merge2, "merge_gate", [gate] + bo, [], [(D_MODEL, BF16)])
    return matmul(mixed, p['w_out'], res=x)


def _rope_tables(pos, dim):
    inv_freq = 1.0 / (ROPE_THETA ** (jnp.arange(0, dim, 2, dtype=F32) / dim))
    ang = pos.astype(F32)[:, None] * inv_freq
    return jnp.cos(ang), jnp.sin(ang)


def _tables(pos):
    cr, sr = _rope_tables(pos, MLA_ROPE)
    ch, sh = _rope_tables(pos, HEAD)
    zero = jnp.zeros((pos.shape[0], LANE - MLA_ROPE), F32)
    return {'cos_r': jnp.concatenate([cr, cr, zero], axis=1), 'sin_r': jnp.concatenate([sr, sr, zero], axis=1),
            'cos_h': jnp.concatenate([ch, ch], axis=1), 'sin_h': jnp.concatenate([sh, sh], axis=1)}


def _local_loss(w_in4, mats, conv_w, small, x, target, tabs):
    groups = win_groups(w_in4)
    for l in range(DEPTH):
        p = {k: v[l] for k, v in mats.items()}
        p.update({k: v[l] for k, v in small.items()})
        p['dn_conv_w'] = conv_w[l]
        p['w_in_groups'] = groups[l]
        x = _layer(x, p, tabs)
    return loss_head(x, target)


def _pack(arrays, dtype, row_tile):
    flat = jnp.concatenate([a.astype(dtype).reshape(-1) for a in arrays])
    rows = -(-flat.shape[0] // (LANE * row_tile)) * row_tile
    flat = jnp.concatenate([flat, jnp.zeros((rows * LANE - flat.shape[0],), dtype)])
    return flat.reshape(rows, LANE)


def _unpack_impl(buf, shapes):
    flat = buf.reshape(-1)
    out, off = [], 0
    for shp in shapes:
        n = math.prod(shp)
        out.append(flat[off:off + n].reshape(shp))
        off += n
    return tuple(out)


@functools.partial(jax.custom_vjp, nondiff_argnums=(1, 2, 3))
def _unpack_p(buf, shapes, dtype_name, rows):
    return _unpack_impl(buf, shapes)


_unpack_p.defvjp(lambda buf, shapes, dtype_name, rows: (_unpack_impl(buf, shapes), None),
                 lambda shapes, dtype_name, rows, _, cts: (_pack(cts, jnp.dtype(dtype_name), rows),))


def _unpack(buf, shapes):
    return _unpack_p(buf, tuple(shapes), jnp.dtype(buf.dtype).name, buf.shape[0])


def _full_from_chips(buf4, shard_shapes):
    per_chip = [_unpack(buf4[j], tuple(shard_shapes)) for j in range(4)]
    return {name: jnp.concatenate([per_chip[j][i] for j in range(4)], axis=axis) for i, (name, axis) in enumerate(MATS)}


_HBM = pl.BlockSpec(memory_space=pltpu.HBM)
_VMEM = pl.BlockSpec(memory_space=pltpu.VMEM)


def _chip_peers(x, y):
    return [(1 - x, y), (x, 1 - y), (1 - x, 1 - y)]


def chip_all_to_all(arrays):
    n = len(arrays)

    def body(*refs):
        in_refs, out_refs = refs[:n], refs[n:2 * n]
        send_sems, recv_sems, local_sems = refs[2 * n:]
        x, y, c = lax.axis_index("x"), lax.axis_index("y"), lax.axis_index("c")
        me = 2 * x + y
        peers = _chip_peers(x, y)
        local, sends = [], []
        for a, (i_ref, o_ref) in enumerate(zip(in_refs, out_refs)):
            local.append(pltpu.make_async_copy(i_ref.at[me], o_ref.at[me], local_sems.at[a]))
            local[-1].start()
            for k, (px, py) in enumerate(peers):
                sends.append(pltpu.make_async_remote_copy(
                    src_ref=i_ref.at[2 * px + py], dst_ref=o_ref.at[me], send_sem=send_sems.at[3 * a + k],
                    recv_sem=recv_sems.at[3 * a + k], device_id=(px, py, c), device_id_type=MESH))
                sends[-1].start()
        for a, (i_ref, o_ref) in enumerate(zip(in_refs, out_refs)):
            for k, (px, py) in enumerate(peers):
                pltpu.make_async_remote_copy(
                    src_ref=i_ref.at[me], dst_ref=o_ref.at[2 * px + py], send_sem=send_sems.at[3 * a + k],
                    recv_sem=recv_sems.at[3 * a + k], device_id=(px, py, c), device_id_type=MESH).wait_recv()
        for cp in sends:
            cp.wait_send()
        for cp in local:
            cp.wait()

    return pl.pallas_call(
        body, out_shape=[jax.ShapeDtypeStruct(a.shape, a.dtype) for a in arrays], in_specs=[_HBM] * n, out_specs=[_HBM] * n,
        scratch_shapes=[pltpu.SemaphoreType.DMA((3 * n,)), pltpu.SemaphoreType.DMA((3 * n,)), pltpu.SemaphoreType.DMA((n,))],
        name="chip_all_to_all")(*arrays)


def chip_all_gather(shards):
    n = len(shards)

    def body(*refs):
        in_refs, out_refs = refs[:n], refs[n:2 * n]
        send_sems, recv_sems = refs[2 * n:]
        x, y, c = lax.axis_index("x"), lax.axis_index("y"), lax.axis_index("c")
        me = 2 * x + y
        peers = _chip_peers(x, y)
        sends = []
        for a, (i_ref, o_ref) in enumerate(zip(in_refs, out_refs)):
            for k, (px, py) in enumerate(peers):
                sends.append(pltpu.make_async_remote_copy(
                    src_ref=i_ref.at[c], dst_ref=o_ref.at[me, c], send_sem=send_sems.at[6 * a + k],
                    recv_sem=recv_sems.at[6 * a + k], device_id=(px, py, c), device_id_type=MESH))
                sends[-1].start()
        for a, (i_ref, o_ref) in enumerate(zip(in_refs, out_refs)):
            for k, (px, py) in enumerate(peers):
                landed = o_ref.at[2 * px + py, c]
                pltpu.make_async_remote_copy(src_ref=i_ref.at[c], dst_ref=landed, send_sem=send_sems.at[6 * a + k],
                                             recv_sem=recv_sems.at[6 * a + k], device_id=(px, py, c), device_id_type=MESH).wait_recv()
                sends.append(pltpu.make_async_remote_copy(
                    src_ref=landed, dst_ref=landed, send_sem=send_sems.at[6 * a + 3 + k], recv_sem=recv_sems.at[6 * a + 3 + k],
                    device_id=(x, y, 1 - c), device_id_type=MESH))
                sends[-1].start()
        for a, (i_ref, o_ref) in enumerate(zip(in_refs, out_refs)):
            for k, (px, py) in enumerate(peers):
                other = o_ref.at[2 * px + py, 1 - c]
                pltpu.make_async_remote_copy(src_ref=other, dst_ref=other, send_sem=send_sems.at[6 * a + 3 + k],
                                             recv_sem=recv_sems.at[6 * a + 3 + k], device_id=(x, y, 1 - c),
                                             device_id_type=MESH).wait_recv()
        for cp in sends:
            cp.wait_send()

    me = 2 * lax.axis_index("x") + lax.axis_index("y")
    outs = pl.pallas_call(
        body, out_shape=[jax.ShapeDtypeStruct((4,) + a.shape, a.dtype) for a in shards], in_specs=[_HBM] * n, out_specs=[_HBM] * n,
        scratch_shapes=[pltpu.SemaphoreType.DMA((6 * n,)), pltpu.SemaphoreType.DMA((6 * n,))],
        name="chip_all_gather")(*shards)
    return [lax.dynamic_update_index_in_dim(o, a, me, 0) for o, a in zip(outs, shards)]


def sibling_swap(arrays):
    n = len(arrays)

    def body(*refs):
        in_refs, out_refs = refs[:n], refs[n:2 * n]
        send_sems, recv_sems = refs[2 * n:]
        x, y, c = lax.axis_index("x"), lax.axis_index("y"), lax.axis_index("c")
        cps = [pltpu.make_async_remote_copy(src_ref=i_ref, dst_ref=o_ref, send_sem=send_sems.at[a], recv_sem=recv_sems.at[a],
                                            device_id=(x, y, 1 - c), device_id_type=MESH)
               for a, (i_ref, o_ref) in enumerate(zip(in_refs, out_refs))]
        for cp in cps:
            cp.start()
        for cp in cps:
            cp.wait()

    return pl.pallas_call(
        body, out_shape=[jax.ShapeDtypeStruct(a.shape, a.dtype) for a in arrays], in_specs=[_HBM] * n, out_specs=[_HBM] * n,
        scratch_shapes=[pltpu.SemaphoreType.DMA((n,)), pltpu.SemaphoreType.DMA((n,))], name="sibling_swap")(*arrays)


def sibling_all_gather(halves):
    n = len(halves)

    def body(*refs):
        in_refs, out_refs = refs[:n], refs[n:2 * n]
        send_sems, recv_sems = refs[2 * n:]
        x, y, c = lax.axis_index("x"), lax.axis_index("y"), lax.axis_index("c")
        sends = [pltpu.make_async_remote_copy(src_ref=i_ref, dst_ref=o_ref.at[c], send_sem=send_sems.at[a], recv_sem=recv_sems.at[a],
                                              device_id=(x, y, 1 - c), device_id_type=MESH)
                 for a, (i_ref, o_ref) in enumerate(zip(in_refs, out_refs))]
        for cp in sends:
            cp.start()
        for a, (i_ref, o_ref) in enumerate(zip(in_refs, out_refs)):
            pltpu.make_async_remote_copy(src_ref=i_ref, dst_ref=o_ref.at[1 - c], send_sem=send_sems.at[a], recv_sem=recv_sems.at[a],
                                         device_id=(x, y, 1 - c), device_id_type=MESH).wait_recv()
        for cp in sends:
            cp.wait_send()

    outs = pl.pallas_call(
        body, out_shape=[jax.ShapeDtypeStruct((2,) + a.shape, a.dtype) for a in halves], in_specs=[_HBM] * n, out_specs=[_HBM] * n,
        scratch_shapes=[pltpu.SemaphoreType.DMA((n,)), pltpu.SemaphoreType.DMA((n,))], name="sibling_all_gather")(*halves)
    return [lax.dynamic_update_index_in_dim(o, a, lax.axis_index("c"), 0) for o, a in zip(outs, halves)]


def all_gather8(v, name):
    def body(v_ref, out_ref, send_sems, recv_sems):
        x, y, c = lax.axis_index("x"), lax.axis_index("y"), lax.axis_index("c")
        out_ref[4 * x + 2 * y + c] = v_ref[...]

        def peer(k):
            return (x ^ (k >> 2), y ^ ((k >> 1) & 1), c ^ (k & 1))

        sends = [pltpu.make_async_remote_copy(src_ref=v_ref, dst_ref=out_ref.at[4 * x + 2 * y + c], send_sem=send_sems.at[k - 1],
                                              recv_sem=recv_sems.at[k - 1], device_id=peer(k), device_id_type=MESH)
                 for k in range(1, 8)]
        for cp in sends:
            cp.start()
        for k in range(1, 8):
            px, py, pc = peer(k)
            pltpu.make_async_remote_copy(src_ref=v_ref, dst_ref=out_ref.at[4 * px + 2 * py + pc], send_sem=send_sems.at[k - 1],
                                         recv_sem=recv_sems.at[k - 1], device_id=peer(k), device_id_type=MESH).wait_recv()
        for cp in sends:
            cp.wait_send()

    return pl.pallas_call(
        body, out_shape=jax.ShapeDtypeStruct((8,) + v.shape, v.dtype), in_specs=[_VMEM], out_specs=_VMEM,
        scratch_shapes=[pltpu.SemaphoreType.DMA((7,)), pltpu.SemaphoreType.DMA((7,))], name=name)(v)


def pair_add(a, b, row_tile, name):
    rows, width = a.shape

    def body(a_ref, b_ref, o_ref):
        o_ref[...] = (a_ref[...].astype(F32) + b_ref[...].astype(F32)).astype(o_ref.dtype)

    spec = pl.BlockSpec((row_tile, width), lambda i: (i, 0))
    return pl.pallas_call(body, out_shape=jax.ShapeDtypeStruct(a.shape, a.dtype), grid=(rows // row_tile,),
                          in_specs=[spec, spec], out_specs=spec, compiler_params=_cparams("parallel"), name=name)(a, b)


def sum_blocks(blocks, row_tile, name):
    n, rows, width = blocks.shape

    def body(b_ref, o_ref):
        acc = b_ref[0].astype(F32)
        for j in range(1, n):
            acc = acc + b_ref[j].astype(F32)
        o_ref[...] = acc

    return pl.pallas_call(
        body, out_shape=jax.ShapeDtypeStruct((rows, width), F32), grid=(rows // row_tile,),
        in_specs=[pl.BlockSpec((n, row_tile, width), lambda i: (0, i, 0))],
        out_specs=pl.BlockSpec((row_tile, width), lambda i: (i, 0)), compiler_params=_cparams("parallel"), name=name)(blocks)


def adamw(g_parts, w, m, v, row_tile, name):
    npart = len(g_parts)

    def body(*refs):
        g = refs[0][...]
        for r in refs[1:npart]:
            g = g + r[...]
        w_ref, m_ref, v_ref, g_out, d_out, m_out, v_out = refs[npart:]
        m_new = ADAM_B1 * m_ref[...] + (1.0 - ADAM_B1) * g
        v_new = ADAM_B2 * v_ref[...] + (1.0 - ADAM_B2) * (g * g)
        m_hat = m_new / (1.0 - ADAM_B1 ** ADAM_STEP)
        v_hat = v_new / (1.0 - ADAM_B2 ** ADAM_STEP)
        g_out[...] = g
        d_out[...] = -ADAM_LR * (m_hat / (jnp.sqrt(v_hat) + ADAM_EPS) + ADAM_WD * w_ref[...])
        m_out[...] = m_new
        v_out[...] = v_new

    rows, width = w.shape
    spec = pl.BlockSpec((row_tile, width), lambda i: (i, 0))
    return pl.pallas_call(
        body, out_shape=[jax.ShapeDtypeStruct(w.shape, F32)] * 4, grid=(rows // row_tile,),
        in_specs=[spec] * (npart + 3), out_specs=[spec] * 4, compiler_params=_cparams("parallel"), name=name)(*g_parts, w, m, v)


def kernel(x, positions, norm_g, w_in, mla_q_a_norm_g, mla_w_q_b, mla_kv_a_norm_g, mla_w_kv_b, mla_q_norm_g, mla_k_norm_g, dn_conv_w, dn_a_log, dn_dt_bias, dn_out_norm_g, dil_q_norm_g, dil_k_norm_g, w_branch, w_out, loss_target, m_norm_g, m_w_in, m_mla_q_a_norm_g, m_mla_w_q_b, m_mla_kv_a_norm_g, m_mla_w_kv_b, m_mla_q_norm_g, m_mla_k_norm_g, m_dn_conv_w, m_dn_a_log, m_dn_dt_bias, m_dn_out_norm_g, m_dil_q_norm_g, m_dil_k_norm_g, m_w_branch, m_w_out, v_norm_g, v_w_in, v_mla_q_a_norm_g, v_mla_w_q_b, v_mla_kv_a_norm_g, v_mla_w_kv_b, v_mla_q_norm_g, v_mla_k_norm_g, v_dn_conv_w, v_dn_a_log, v_dn_dt_bias, v_dn_out_norm_g, v_dil_q_norm_g, v_dil_k_norm_g, v_w_branch, v_w_out):
    w = dict(norm_g=norm_g, w_in=w_in, mla_q_a_norm_g=mla_q_a_norm_g, mla_w_q_b=mla_w_q_b, mla_kv_a_norm_g=mla_kv_a_norm_g,
             mla_w_kv_b=mla_w_kv_b, mla_q_norm_g=mla_q_norm_g, mla_k_norm_g=mla_k_norm_g, dn_conv_w=dn_conv_w, dn_a_log=dn_a_log,
             dn_dt_bias=dn_dt_bias, dn_out_norm_g=dn_out_norm_g, dil_q_norm_g=dil_q_norm_g, dil_k_norm_g=dil_k_norm_g,
             w_branch=w_branch, w_out=w_out)
    m = dict(norm_g=m_norm_g, w_in=m_w_in, mla_q_a_norm_g=m_mla_q_a_norm_g, mla_w_q_b=m_mla_w_q_b, mla_kv_a_norm_g=m_mla_kv_a_norm_g,
             mla_w_kv_b=m_mla_w_kv_b, mla_q_norm_g=m_mla_q_norm_g, mla_k_norm_g=m_mla_k_norm_g, dn_conv_w=m_dn_conv_w,
             dn_a_log=m_dn_a_log, dn_dt_bias=m_dn_dt_bias, dn_out_norm_g=m_dn_out_norm_g, dil_q_norm_g=m_dil_q_norm_g,
             dil_k_norm_g=m_dil_k_norm_g, w_branch=m_w_branch, w_out=m_w_out)
    v = dict(norm_g=v_norm_g, w_in=v_w_in, mla_q_a_norm_g=v_mla_q_a_norm_g, mla_w_q_b=v_mla_w_q_b, mla_kv_a_norm_g=v_mla_kv_a_norm_g,
             mla_w_kv_b=v_mla_w_kv_b, mla_q_norm_g=v_mla_q_norm_g, mla_k_norm_g=v_mla_k_norm_g, dn_conv_w=v_dn_conv_w,
             dn_a_log=v_dn_a_log, dn_dt_bias=v_dn_dt_bias, dn_out_norm_g=v_dn_out_norm_g, dil_q_norm_g=v_dil_q_norm_g,
             dil_k_norm_g=v_dil_k_norm_g, w_branch=v_w_branch, w_out=v_w_out)
    chip = 2 * lax.axis_index("x") + lax.axis_index("y")
    mat_names = [n for n, _ in MATS]
    mat_shapes = tuple(w[n].shape for n in mat_names)
    conv_shard = dn_conv_w.shape
    win_shape = w_in.shape
    win_rows = (win_shape[0] * win_shape[1], win_shape[2])

    mats_sh = _pack([w[n] for n in mat_names], BF16, 2 * MAT_ROWS)
    mat_rows = mats_sh.shape[0]
    w_in4, mats4 = chip_all_gather([w_in.astype(BF16), mats_sh.reshape(2, mat_rows // 2, LANE)])
    mats4 = mats4.reshape(N_CHIPS, mat_rows, LANE)
    conv8 = all_gather8(_pack([dn_conv_w], F32, 8), "gather_conv_w")
    conv_full = jnp.concatenate([_unpack(conv8[2 * j], (conv_shard,))[0] for j in range(4)], axis=2)
    small = {n: w[n] for n in SMALL}
    tabs = _tables(positions[0])

    def loss_fn(w_in4, mats4, conv_full, small, xs):
        return _local_loss(w_in4, _full_from_chips(mats4, mat_shapes), conv_full, small, xs, loss_target[0], tabs)

    loss, (g_win4, g_mats4, g_conv, g_small, g_x) = jax.value_and_grad(loss_fn, argnums=(0, 1, 2, 3, 4))(
        w_in4, mats4, conv_full, small, x[0])
    loss = lax.psum(loss, ("x", "y", "c"))

    c_idx = lax.axis_index("c")
    half_win = (N_CHIPS * win_shape[1], win_shape[2])
    half_mats = (N_CHIPS * (mat_rows // 2), LANE)
    g_mats4 = g_mats4.reshape(N_CHIPS, 2, mat_rows // 2, LANE)
    pick = lambda g, i, shape: lax.dynamic_index_in_dim(g, i, axis=1, keepdims=False).reshape(shape)
    from_sib = sibling_swap([pick(g_win4, 1 - c_idx, half_win), pick(g_mats4, 1 - c_idx, half_mats)])
    s_win = pair_add(pick(g_win4, c_idx, half_win), from_sib[0], WIN_ROWS, "pair_add_w_in")
    s_mats = pair_add(pick(g_mats4, c_idx, half_mats), from_sib[1], MAT_ROWS, "pair_add_mats")
    r_win, r_mats = chip_all_to_all([s_win.reshape(N_CHIPS, win_shape[1], win_shape[2]),
                                     s_mats.reshape(N_CHIPS, mat_rows // 2, LANE)])
    g_win, g_mats = sibling_all_gather([sum_blocks(r_win, WIN_ROWS, "sum_chip_pieces_w_in"),
                                        sum_blocks(r_mats, MAT_ROWS, "sum_chip_pieces_mats")])
    res_win = adamw([g_win.reshape(win_rows)], *[d['w_in'].reshape(win_rows) for d in (w, m, v)], WIN_ROWS, "adamw_w_in")
    packed = [_pack([d[n] for n in mat_names], F32, 2 * MAT_ROWS) for d in (w, m, v)]
    res_big = [dict(zip(mat_names, _unpack(r, mat_shapes)))
               for r in adamw([g_mats.reshape(mat_rows, LANE)], *packed, MAT_ROWS, "adamw_mats")]
    for k in range(4):
        res_big[k]['w_in'] = res_win[k].reshape(win_shape)

    small_shapes = tuple(w[n].shape for n in SMALL) + (g_conv.shape,)
    g_all = sum_blocks(all_gather8(_pack([g_small[n] for n in SMALL] + [g_conv], F32, 8), "gather_small_grads"), 8, "sum_small")
    g_list = list(_unpack(g_all, small_shapes))
    g_list[-1] = lax.dynamic_slice_in_dim(g_list[-1], chip * conv_shard[2], conv_shard[2], axis=2)
    small_names = list(SMALL) + ['dn_conv_w']
    packed_s = [_pack([d[n] for n in small_names], F32, 8) for d in (w, m, v)]
    shapes_s = tuple(w[n].shape for n in small_names)
    res_small = [dict(zip(small_names, _unpack(r, shapes_s))) for r in adamw([_pack(g_list, F32, 8)], *packed_s, 8, "adamw_small")]

    outs = [loss, g_x[None]]
    for k in range(4):
        outs += [res_big[k][n] if n in res_big[k] else res_small[k][n] for n in WEIGHTS]
    return tuple(outs)
```

```python
import functools
import math

import jax
import jax.numpy as jnp
from jax import lax
from jax.experimental import pallas as pl
from jax.experimental.pallas import tpu as pltpu

F32 = jnp.float32
BF16 = jnp.bfloat16
HI = lax.Precision.HIGHEST
MESH = pl.DeviceIdType.MESH

LANE = 128
VMEM_LIMIT = 48 * 1024 * 1024
ROW_BLOCK = 256
MM_TM, MM_TN, MM_TK = 1024, 1024, 1024
MAT_ROWS = 512
WIN_ROWS = 128

RMS_EPS = 1e-6
ROPE_THETA = 10000.0
D_MODEL = 1024
DEPTH = 2
MLA_HEADS = 4
MLA_NOPE = 128
MLA_ROPE = 64
MLA_QK = MLA_NOPE + MLA_ROPE
MLA_Q_RANK = 384
MLA_KV_RANK = 256
DN_HEADS = 4
DN_DIM = 128
DN_CONV = 4
GDN_CHUNK = 128
DIL_WINDOWS = (128, 512, 2048)
DIL_DILATIONS = (1, 4, 16)
DIL_GROUPS = 3
DIL_BLOCK = 128
HEAD = 128
BRANCH_W = 512

ADAM_LR = 0.001
ADAM_B1 = 0.9
ADAM_B2 = 0.999
ADAM_EPS = 1e-08
ADAM_WD = 0.01
ADAM_STEP = 10

WEIGHTS = ['norm_g', 'w_in', 'mla_q_a_norm_g', 'mla_w_q_b', 'mla_kv_a_norm_g', 'mla_w_kv_b', 'mla_q_norm_g',
           'mla_k_norm_g', 'dn_conv_w', 'dn_a_log', 'dn_dt_bias', 'dn_out_norm_g', 'dil_q_norm_g', 'dil_k_norm_g',
           'w_branch', 'w_out']
MATS = (('mla_w_q_b', 2), ('mla_w_kv_b', 2), ('w_branch', 3), ('w_out', 1))
SMALL = ('norm_g', 'mla_q_a_norm_g', 'mla_kv_a_norm_g', 'mla_q_norm_g', 'mla_k_norm_g', 'dn_a_log', 'dn_dt_bias',
         'dn_out_norm_g', 'dil_q_norm_g', 'dil_k_norm_g')


def _cparams(*sem):
    return pltpu.CompilerParams(dimension_semantics=sem or None, vmem_limit_bytes=VMEM_LIMIT)


def _tile(dim, target):
    best = 0
    for t in range(LANE, min(dim, target) + 1, LANE):
        if dim % t == 0:
            best = t
    assert best, (dim, target)
    return best


def _mm(a, b, mode, out_dtype, res=None):
    if mode == 'nn':
        (m, k), (k2, n) = a.shape, b.shape
    elif mode == 'nt':
        (m, k), (n, k2) = a.shape, b.shape
    else:
        (k, m), (k2, n) = a.shape, b.shape
    assert k == k2, (a.shape, b.shape, mode)
    tm, tn, tk = _tile(m, MM_TM), _tile(n, MM_TN), _tile(k, MM_TK)
    nk = k // tk
    dims = {'nn': (((1,), (0,)), ((), ())), 'nt': (((1,), (1,)), ((), ())), 'tn': (((0,), (0,)), ((), ()))}[mode]

    def body(*refs):
        if res is None:
            a_ref, b_ref, o_ref = refs[:3]
        else:
            a_ref, b_ref, r_ref, o_ref = refs[:4]
        part = lax.dot_general(a_ref[...].astype(BF16), b_ref[...].astype(BF16), dims, preferred_element_type=F32)

        def finish(r):
            if res is not None:
                r = r + r_ref[...].astype(F32)
            o_ref[...] = r.astype(o_ref.dtype)

        if nk == 1:
            finish(part)
            return
        acc = refs[-1]
        kk = pl.program_id(2)

        @pl.when(kk == 0)
        def _():
            acc[...] = part

        @pl.when(jnp.logical_and(kk > 0, kk < nk - 1))
        def _():
            acc[...] += part

        @pl.when(kk == nk - 1)
        def _():
            finish(acc[...] + part)

    a_spec = pl.BlockSpec((tk, tm), lambda i, j, kk: (kk, i)) if mode == 'tn' else pl.BlockSpec((tm, tk), lambda i, j, kk: (i, kk))
    b_spec = pl.BlockSpec((tn, tk), lambda i, j, kk: (j, kk)) if mode == 'nt' else pl.BlockSpec((tk, tn), lambda i, j, kk: (kk, j))
    o_spec = pl.BlockSpec((tm, tn), lambda i, j, kk: (i, j))
    in_specs = [a_spec, b_spec] + ([o_spec] if res is not None else [])
    args = (a, b) + ((res,) if res is not None else ())
    return pl.pallas_call(
        body, out_shape=jax.ShapeDtypeStruct((m, n), out_dtype), grid=(m // tm, n // tn, nk),
        in_specs=in_specs, out_specs=o_spec, scratch_shapes=[pltpu.VMEM((tm, tn), F32)] if nk > 1 else [],
        compiler_params=_cparams("parallel", "parallel", "arbitrary"),
        name=f"mm_{mode}_{m}x{k}x{n}" + ("_res" if res is not None else ""))(*args)


def _make_matmul(out_dtype, with_res):
    @jax.custom_vjp
    def mm(a, b, *r):
        return _mm(a, b, 'nn', out_dtype, *r)

    def fwd(a, b, *r):
        return mm(a, b, *r), (a, b)

    def bwd(saved, g):
        a, b = saved
        da = _mm(g, b, 'nt', a.dtype)
        db = _mm(a, g, 'tn', b.dtype)
        return (da, db) + ((g,) if with_res else ())

    mm.defvjp(fwd, bwd)
    return mm


def matmul(a, b, out_dtype=F32, res=None):
    if res is None:
        return _make_matmul(out_dtype, False)(a, b)
    return _make_matmul(out_dtype, True)(a, b, res)


def _chunks(ref):
    return [ref[:, c * LANE:(c + 1) * LANE].astype(F32) for c in range(ref.shape[-1] // LANE)]


def _store(ref, chunks):
    for c, ch in enumerate(chunks):
        ref[:, c * LANE:(c + 1) * LANE] = ch.astype(ref.dtype)


def _row_spec(width, br, d=1):
    return pl.BlockSpec((br // d, d * width), lambda i: (i, 0))


def _par_spec(width):
    return pl.BlockSpec((1, width), lambda i: (0, 0))


def _load_rows(ref, scratch, d):
    if d == 1:
        return _chunks(ref)
    n, width = ref.shape[0], ref.shape[1] // d
    for c in range(width // LANE):
        for r in range(d):
            lanes = slice(r * width + c * LANE, r * width + (c + 1) * LANE)
            scratch[c, pl.ds(r, n, stride=d), :] = ref[:, lanes].astype(F32)
    return [scratch[c] for c in range(width // LANE)]


def _store_rows(ref, scratch, d, chunks):
    if d == 1:
        return _store(ref, chunks)
    n, width = ref.shape[0], ref.shape[1] // d
    for c, ch in enumerate(chunks):
        scratch[c] = ch
        for r in range(d):
            lanes = slice(r * width + c * LANE, r * width + (c + 1) * LANE)
            ref[:, lanes] = scratch[c, pl.ds(r, n, stride=d), :].astype(ref.dtype)


def _view_scratch(widths_dils, br):
    return [pltpu.VMEM((w // LANE, br, LANE), F32) for w, d in widths_dils if d > 1]


def _with_scratch(dils, scratch_refs):
    it = iter(scratch_refs)
    return [next(it) if d > 1 else None for d in dils]


def _rw_fwd(f, name, rows, params, outs, br, row_dil, out_dil):
    s = rows[0].shape[0] * row_dil[0]
    br = min(br, s)
    nr, npar, nout = len(rows), len(params), len(outs)
    row_w = [r.shape[1] // d for r, d in zip(rows, row_dil)]

    def body(*refs):
        scr = refs[nr + npar + nout:]
        n_in = sum(d > 1 for d in row_dil)
        rc = [_load_rows(r, sc, d) for r, sc, d in zip(refs[:nr], _with_scratch(row_dil, scr[:n_in]), row_dil)]
        pc = [_chunks(p) for p in refs[nr:nr + npar]]
        res = f(rc, pc)
        for o_ref, sc, d, chs in zip(refs[nr + npar:nr + npar + nout], _with_scratch(out_dil, scr[n_in:]), out_dil, res):
            _store_rows(o_ref, sc, d, chs)

    return pl.pallas_call(
        body, out_shape=[jax.ShapeDtypeStruct((s // d, d * w), dt) for (w, dt), d in zip(outs, out_dil)], grid=(s // br,),
        in_specs=[_row_spec(w, br, d) for w, d in zip(row_w, row_dil)] + [_par_spec(p.shape[1]) for p in params],
        out_specs=[_row_spec(w, br, d) for (w, _), d in zip(outs, out_dil)],
        scratch_shapes=_view_scratch(zip(row_w, row_dil), br) + _view_scratch([(w, d) for (w, _), d in zip(outs, out_dil)], br),
        compiler_params=_cparams("parallel"), name=name + "_fwd")(*rows, *params)


def _rw_bwd(f, name, rows, params, cts, nograd, br, row_dil, out_dil):
    s = rows[0].shape[0] * row_dil[0]
    br = min(br, s)
    nr, npar, nct = len(rows), len(params), len(cts)
    grad_rows = [i for i in range(nr) if i not in nograd]
    row_w = [r.shape[1] // d for r, d in zip(rows, row_dil)]
    ct_w = [c.shape[1] // d for c, d in zip(cts, out_dil)]
    grad_dil = [row_dil[i] for i in grad_rows]

    def body(*refs):
        n_out = len(grad_rows) + npar
        out_refs = refs[nr + npar + nct:nr + npar + nct + n_out]
        scr = refs[nr + npar + nct + n_out:]
        n_in, n_ct = sum(d > 1 for d in row_dil), sum(d > 1 for d in out_dil)
        rc = [_load_rows(r, sc, d) for r, sc, d in zip(refs[:nr], _with_scratch(row_dil, scr[:n_in]), row_dil)]
        pc = [_chunks(p) for p in refs[nr:nr + npar]]
        ct = [_load_rows(c, sc, d) for c, sc, d in
              zip(refs[nr + npar:nr + npar + nct], _with_scratch(out_dil, scr[n_in:n_in + n_ct]), out_dil)]
        _, vjp = jax.vjp(f, rc, pc)
        drc, dpc = vjp(ct)
        for o_ref, sc, d, i in zip(out_refs[:len(grad_rows)], _with_scratch(grad_dil, scr[n_in + n_ct:]), grad_dil, grad_rows):
            _store_rows(o_ref, sc, d, drc[i])
        i0 = pl.program_id(0)
        for o_ref, chs in zip(out_refs[len(grad_rows):], dpc):
            @pl.when(i0 == 0)
            def _(o_ref=o_ref):
                o_ref[...] = jnp.zeros_like(o_ref)
            for c, ch in enumerate(chs):
                o_ref[:, c * LANE:(c + 1) * LANE] += ch

    out_shape = ([jax.ShapeDtypeStruct(rows[i].shape, rows[i].dtype) for i in grad_rows]
                 + [jax.ShapeDtypeStruct(p.shape, F32) for p in params])
    out_specs = [_row_spec(row_w[i], br, row_dil[i]) for i in grad_rows] + [_par_spec(p.shape[1]) for p in params]
    res = pl.pallas_call(
        body, out_shape=out_shape, grid=(s // br,),
        in_specs=([_row_spec(w, br, d) for w, d in zip(row_w, row_dil)] + [_par_spec(p.shape[1]) for p in params]
                  + [_row_spec(w, br, d) for w, d in zip(ct_w, out_dil)]),
        out_specs=out_specs,
        scratch_shapes=(_view_scratch(zip(row_w, row_dil), br) + _view_scratch(zip(ct_w, out_dil), br)
                        + _view_scratch([(row_w[i], row_dil[i]) for i in grad_rows], br)),
        compiler_params=_cparams("arbitrary"), name=name + "_bwd")(*rows, *params, *cts)
    drows = [None] * nr
    for o, i in zip(res[:len(grad_rows)], grad_rows):
        drows[i] = o
    for i in nograd:
        drows[i] = jnp.zeros_like(rows[i])
    return tuple(drows), tuple(res[len(grad_rows):])


def rowwise(f, name, rows, params, outs, nograd=(), br=ROW_BLOCK, row_dil=None, out_dil=None):
    row_dil = tuple(row_dil or [1] * len(rows))
    out_dil = tuple(out_dil or [1] * len(outs))

    @jax.custom_vjp
    def op(rows, params):
        return tuple(_rw_fwd(f, name, rows, params, outs, br, row_dil, out_dil))

    def fwd(rows, params):
        return op(rows, params), (rows, params)

    def bwd(saved, cts):
        rows, params = saved
        return _rw_bwd(f, name, rows, params, list(cts), nograd, br, row_dil, out_dil)

    op.defvjp(fwd, bwd)
    return op(tuple(rows), tuple(params))


def _lane_roll(x, s):
    w = x.shape[-1]

    @jax.custom_vjp
    def r(v):
        return pltpu.roll(v, s, 1)

    r.defvjp(lambda v: (r(v), None), lambda _, g: (pltpu.roll(g, (w - s) % w, 1),))
    return r(x)


def _lanes(x):
    return lax.broadcasted_iota(jnp.int32, x.shape, 1)


def _rms(x, g, n=LANE):
    return x * lax.rsqrt(jnp.sum(x * x, axis=-1, keepdims=True) / n + RMS_EPS) * g


def _rope128(x, cos, sin):
    lane = _lanes(x)
    rot = _lane_roll(x, 64) * jnp.where(lane < 64, -1.0, 1.0)
    return x * cos + rot * sin


def _rope64(x, cos, sin):
    lane = _lanes(x)
    rot = jnp.where(lane < 32, -_lane_roll(x, 96), jnp.where(lane < 64, _lane_roll(x, 32), 0.0))
    return x * cos + rot * sin


def _silu(z):
    return z * jax.nn.sigmoid(z)


def _softplus(x):
    return jnp.maximum(x, 0.0) + jnp.log1p(jnp.exp(-jnp.abs(x)))


def f_rms_full(rc, pc):
    x, g = rc[0], pc[0]
    width = len(x) * LANE
    ms = sum(jnp.sum(c * c, axis=-1, keepdims=True) for c in x) / width
    r = lax.rsqrt(ms + RMS_EPS)
    return [[c * r * gc for c, gc in zip(x, g)]]


def f_mla1(rc, pc):
    x = rc[0]
    qn = f_rms_full([x[0:3]], [pc[0]])[0]
    kvn = f_rms_full([x[3:5]], [pc[1]])[0]
    return [qn, kvn, [x[5]]]


def f_mla2(rc, pc):
    q, kv, kpe, cos, sin = rc[0], rc[1], rc[2][0], rc[3][0], rc[4][0]
    gqn, gqp, gkn, gkp = pc[0][0], pc[1][0], pc[2][0], pc[3][0]
    k_pe = _rope64(_rms(kpe, gkp, MLA_ROPE), cos, sin)
    q_att, k_att, v = [], [], []
    for h in range(MLA_HEADS):
        q_att += [_rms(q[h], gqn), _rope64(_rms(q[MLA_HEADS + h], gqp, MLA_ROPE), cos, sin)]
        k_att += [_rms(kv[2 * h], gkn), k_pe]
        v.append(kv[2 * h + 1])
    return [q_att, k_att, v]


def f_gates(rc, pc):
    x, a_log, dt_bias = rc[0][0], pc[0][0], pc[1][0]
    lane = _lanes(x)
    g = -jnp.exp(a_log) * _softplus(x + dt_bias)
    return [[jnp.where(lane < DN_HEADS, g, jnp.where(lane < 2 * DN_HEADS, jax.nn.sigmoid(x), 0.0))]]


def f_headnorm(rc, pc):
    return [[_rms(c, pc[0][0]) for c in rc[0]]]


def f_dil(rc, pc):
    x, cos, sin = rc[0], rc[1][0], rc[2][0]
    gq, gk = pc[0][0], pc[1][0]
    n = len(x) // 3
    q = [_rope128(_rms(c, gq), cos, sin) for c in x[:n]]
    k = [_rope128(_rms(c, gk), cos, sin) for c in x[n:2 * n]]
    v = list(x[2 * n:])
    per = n // DIL_GROUPS
    return [t[g * per:(g + 1) * per] for g in range(DIL_GROUPS) for t in (q, k, v)]


def f_comb(rc, pc):
    o, l = rc[:DIL_GROUPS], rc[DIL_GROUPS:]
    out = []
    for c in range(len(o[0])):
        m = functools.reduce(jnp.maximum, [lg[c] for lg in l])
        e = [jnp.exp(lg[c] - m) for lg in l]
        den = sum(e)
        out.append(sum(eg * og[c] for eg, og in zip(e, o)) / den)
    return [out]


def f_merge1(rc, pc):
    z = rc[3]
    n = len(rc[0])
    return [[y * _silu(z[b * n + c]) for c, y in enumerate(rc[b])] for b in range(3)]


def f_merge2(rc, pc):
    gate = rc[0]
    n = len(rc[1])
    return [[sum(jax.nn.sigmoid(gate[b * n + c]) * rc[1 + b][c] for b in range(3)) for c in range(n)]]


MLA_DQK = 2 * LANE
MLA_SCALE = MLA_QK ** -0.5


def _mla_attn_fwd(q, k, v):
    s = q.shape[0]
    h = q.shape[1] // MLA_DQK
    t = min(512, s)
    n = s // t

    def body(q_ref, k_ref, v_ref, o_ref, lse_ref, m_sc, l_sc, acc_sc):
        qi, kj = pl.program_id(1), pl.program_id(2)

        @pl.when(kj == 0)
        def _():
            m_sc[...] = jnp.full_like(m_sc, -jnp.inf)
            l_sc[...] = jnp.zeros_like(l_sc)
            acc_sc[...] = jnp.zeros_like(acc_sc)

        nsub = 2 if t % 256 == 0 else 1
        ts = t // nsub
        rows = [slice(r * ts, (r + 1) * ts) for r in range(nsub)]

        def step(on_diagonal):
            sc = [lax.dot_general(q_ref[rw, :], k_ref[...], (((1,), (1,)), ((), ())), preferred_element_type=F32) * MLA_SCALE
                  for rw in rows]
            if on_diagonal:
                keep = [(lax.broadcasted_iota(jnp.int32, (ts, t), 1)
                         <= r * ts + lax.broadcasted_iota(jnp.int32, (ts, t), 0)) for r in range(nsub)]
                sc = [jnp.where(kp, x, -jnp.inf) for kp, x in zip(keep, sc)]
            m_old = [m_sc[rw, :] for rw in rows]
            m_new = [jnp.maximum(mo, jnp.max(x, axis=-1, keepdims=True)) for mo, x in zip(m_old, sc)]
            alpha = [jnp.exp(mo - mn) for mo, mn in zip(m_old, m_new)]
            p = [jnp.exp(x - mn) for x, mn in zip(sc, m_new)]
            pv = [jnp.dot(x.astype(BF16), v_ref[...], preferred_element_type=F32) for x in p]
            for r, rw in enumerate(rows):
                l_sc[rw, :] = alpha[r] * l_sc[rw, :] + jnp.sum(p[r], axis=-1, keepdims=True)
                acc_sc[rw, :] = alpha[r] * acc_sc[rw, :] + pv[r]
                m_sc[rw, :] = m_new[r]

        @pl.when(kj < qi)
        def _():
            step(False)

        @pl.when(kj == qi)
        def _():
            step(True)

        @pl.when(kj == n - 1)
        def _():
            o_ref[...] = acc_sc[...] / l_sc[...]
            lse_ref[...] = jnp.broadcast_to(m_sc[...] + jnp.log(l_sc[...]), lse_ref.shape)

    return pl.pallas_call(
        body, out_shape=[jax.ShapeDtypeStruct((s, h * HEAD), F32)] * 2, grid=(h, n, n),
        in_specs=[pl.BlockSpec((t, MLA_DQK), lambda hh, i, j: (i, hh)),
                  pl.BlockSpec((t, MLA_DQK), lambda hh, i, j: (jnp.minimum(j, i), hh)),
                  pl.BlockSpec((t, HEAD), lambda hh, i, j: (jnp.minimum(j, i), hh))],
        out_specs=[pl.BlockSpec((t, HEAD), lambda hh, i, j: (i, hh))] * 2,
        scratch_shapes=[pltpu.VMEM((t, 1), F32), pltpu.VMEM((t, 1), F32), pltpu.VMEM((t, HEAD), F32)],
        compiler_params=_cparams("parallel", "parallel", "arbitrary"), name="mla_attn_fwd")(q, k, v)


def _mla_attn_bwd(q, k, v, o, lse, do):
    s = q.shape[0]
    h = q.shape[1] // MLA_DQK
    t = min(512, s)
    n = s // t
    nt, tn = (((1,), (1,)), ((), ())), (((0,), (0,)), ((), ()))

    def body(q_ref, k_ref, v_ref, o_ref, lse_ref, do_ref, dq_ref, dk_ref, dv_ref, dq_sc, dk_sc, dv_sc):
        kj, qi = pl.program_id(1), pl.program_id(2)

        @pl.when(jnp.logical_and(kj == 0, qi == 0))
        def _():
            dq_sc[...] = jnp.zeros_like(dq_sc)

        @pl.when(qi == 0)
        def _():
            dk_sc[...] = jnp.zeros_like(dk_sc)
            dv_sc[...] = jnp.zeros_like(dv_sc)

        def pair(on_diagonal):
            sc = lax.dot_general(q_ref[...], k_ref[...], nt, preferred_element_type=F32) * MLA_SCALE
            p = jnp.exp(sc - lse_ref[:, 0:1])
            if on_diagonal:
                p = jnp.where(lax.broadcasted_iota(jnp.int32, (t, t), 1) <= lax.broadcasted_iota(jnp.int32, (t, t), 0), p, 0.0)
            do_v = do_ref[...]
            do_b = do_v.astype(BF16)
            dp = lax.dot_general(do_b, v_ref[...], nt, preferred_element_type=F32)
            delta = jnp.sum(do_v * o_ref[...], axis=-1, keepdims=True)
            ds = (p * (dp - delta) * MLA_SCALE).astype(BF16)
            dv_sc[...] += lax.dot_general(p.astype(BF16), do_b, tn, preferred_element_type=F32)
            dk_sc[...] += lax.dot_general(ds, q_ref[...], tn, preferred_element_type=F32)
            rows = pl.ds(pl.multiple_of(qi * t, t), t)
            dq_sc[rows, :] += jnp.dot(ds, k_ref[...], preferred_element_type=F32)

        @pl.when(qi > kj)
        def _():
            pair(False)

        @pl.when(qi == kj)
        def _():
            pair(True)

        @pl.when(qi == n - 1)
        def _():
            dk_ref[...] = dk_sc[...].astype(dk_ref.dtype)
            dv_ref[...] = dv_sc[...].astype(dv_ref.dtype)

        @pl.when(jnp.logical_and(kj == n - 1, qi == n - 1))
        def _():
            dq_ref[...] = dq_sc[...].astype(dq_ref.dtype)

    qmap = lambda hh, j, i: (jnp.maximum(i, j), hh)
    kmap = lambda hh, j, i: (j, hh)
    return pl.pallas_call(
        body, out_shape=[jax.ShapeDtypeStruct(q.shape, BF16), jax.ShapeDtypeStruct(k.shape, BF16), jax.ShapeDtypeStruct(v.shape, BF16)],
        grid=(h, n, n),
        in_specs=[pl.BlockSpec((t, MLA_DQK), qmap), pl.BlockSpec((t, MLA_DQK), kmap), pl.BlockSpec((t, HEAD), kmap),
                  pl.BlockSpec((t, HEAD), qmap), pl.BlockSpec((t, HEAD), qmap), pl.BlockSpec((t, HEAD), qmap)],
        out_specs=[pl.BlockSpec((s, MLA_DQK), lambda hh, j, i: (0, hh)), pl.BlockSpec((t, MLA_DQK), kmap),
                   pl.BlockSpec((t, HEAD), kmap)],
        scratch_shapes=[pltpu.VMEM((s, MLA_DQK), F32), pltpu.VMEM((t, MLA_DQK), F32), pltpu.VMEM((t, HEAD), F32)],
        compiler_params=_cparams("parallel", "arbitrary", "arbitrary"), name="mla_attn_bwd")(q, k, v, o, lse, do)


@jax.custom_vjp
def mla_attention(q, k, v):
    return _mla_attn_fwd(q, k, v)[0]


def _mla_attention_fwd(q, k, v):
    o, lse = _mla_attn_fwd(q, k, v)
    return o, (q, k, v, o, lse)


def _mla_attention_bwd(saved, do):
    return tuple(_mla_attn_bwd(*saved, do))


mla_attention.defvjp(_mla_attention_fwd, _mla_attention_bwd)


DIL_SCALE = HEAD ** -0.5
GROUP_W = 4 * HEAD


def _dil_scores(q, kp, kc, n):
    dn = (((1,), (1,)), ((), ()))
    sp = lax.dot_general(q, kp, dn, preferred_element_type=F32) * DIL_SCALE
    sc = lax.dot_general(q, kc, dn, preferred_element_type=F32) * DIL_SCALE
    qi = lax.broadcasted_iota(jnp.int32, sp.shape, 0)
    kc_i = lax.broadcasted_iota(jnp.int32, sp.shape, 1)
    vp = jnp.logical_and(kc_i >= qi, n > 0)
    vc = kc_i <= qi
    return sp, sc, vp, vc


def _dil_specs(d):
    cur = pl.BlockSpec((DIL_BLOCK, GROUP_W), lambda r, n: (n, r))
    prev = pl.BlockSpec((DIL_BLOCK, GROUP_W), lambda r, n: (jnp.maximum(n - 1, 0), r))
    return cur, prev


def _dil_fwd(q, k, v, d):
    l = q.shape[0]
    nb = l // DIL_BLOCK
    cur, prev = _dil_specs(d)

    def body(q_ref, kp_ref, kc_ref, vp_ref, vc_ref, o_ref, lse_ref):
        n = pl.program_id(1)
        heads = range(4)
        sl = [slice(h * HEAD, (h + 1) * HEAD) for h in heads]
        scores = [_dil_scores(q_ref[:, sl[h]], kp_ref[:, sl[h]], kc_ref[:, sl[h]], n) for h in heads]
        sp = [jnp.where(vp, s_p, -jnp.inf) for s_p, _, vp, _ in scores]
        sc = [jnp.where(vc, s_c, -jnp.inf) for _, s_c, _, vc in scores]
        m = [jnp.maximum(jnp.max(sp[h], axis=-1, keepdims=True), jnp.max(sc[h], axis=-1, keepdims=True)) for h in heads]
        ep = [jnp.exp(sp[h] - m[h]) for h in heads]
        ec = [jnp.exp(sc[h] - m[h]) for h in heads]
        den = [jnp.sum(ep[h], axis=-1, keepdims=True) + jnp.sum(ec[h], axis=-1, keepdims=True) for h in heads]
        acc = [jnp.dot(ep[h].astype(BF16), vp_ref[:, sl[h]], preferred_element_type=F32)
               + jnp.dot(ec[h].astype(BF16), vc_ref[:, sl[h]], preferred_element_type=F32) for h in heads]
        for h in heads:
            o_ref[:, sl[h]] = acc[h] / den[h]
            lse_ref[:, sl[h]] = jnp.broadcast_to(m[h] + jnp.log(den[h]), (DIL_BLOCK, HEAD))

    return pl.pallas_call(
        body, out_shape=[jax.ShapeDtypeStruct(q.shape, F32)] * 2, grid=(d, nb),
        in_specs=[cur, prev, cur, prev, cur], out_specs=[cur, cur],
        compiler_params=_cparams("parallel", "parallel"), name=f"dil_fwd_d{d}")(q, k, k, v, v)


def _dil_bwd(q, k, v, o, lse, do, dlse, d):
    l = q.shape[0]
    nb = l // DIL_BLOCK
    cur, prev = _dil_specs(d)
    tn = (((0,), (0,)), ((), ()))
    nt = (((1,), (1,)), ((), ()))

    def body(q_ref, kp_ref, kc_ref, vp_ref, vc_ref, o_ref, lse_ref, do_ref, dl_ref,
             dq_ref, dkp_ref, dkc_ref, dvp_ref, dvc_ref):
        n = pl.program_id(1)
        heads = range(4)
        sl = [slice(h * HEAD, (h + 1) * HEAD) for h in heads]
        scores = [_dil_scores(q_ref[:, sl[h]], kp_ref[:, sl[h]], kc_ref[:, sl[h]], n) for h in heads]
        lse = [lse_ref[:, h * HEAD:h * HEAD + 1] for h in heads]
        pp = [jnp.where(scores[h][2], jnp.exp(scores[h][0] - lse[h]), 0.0) for h in heads]
        pc = [jnp.where(scores[h][3], jnp.exp(scores[h][1] - lse[h]), 0.0) for h in heads]
        do_b = [do_ref[:, sl[h]].astype(BF16) for h in heads]
        corr = [jnp.sum(dl_ref[:, sl[h]], axis=-1, keepdims=True)
                - jnp.sum(do_ref[:, sl[h]] * o_ref[:, sl[h]], axis=-1, keepdims=True) for h in heads]
        dsp = [(pp[h] * (lax.dot_general(do_b[h], vp_ref[:, sl[h]], nt, preferred_element_type=F32) + corr[h])
                * DIL_SCALE).astype(BF16) for h in heads]
        dsc = [(pc[h] * (lax.dot_general(do_b[h], vc_ref[:, sl[h]], nt, preferred_element_type=F32) + corr[h])
                * DIL_SCALE).astype(BF16) for h in heads]
        for h in heads:
            dq_ref[:, sl[h]] = (jnp.dot(dsp[h], kp_ref[:, sl[h]], preferred_element_type=F32)
                                + jnp.dot(dsc[h], kc_ref[:, sl[h]], preferred_element_type=F32)).astype(dq_ref.dtype)
            dkp_ref[:, sl[h]] = lax.dot_general(dsp[h], q_ref[:, sl[h]], tn, preferred_element_type=F32)
            dkc_ref[:, sl[h]] = lax.dot_general(dsc[h], q_ref[:, sl[h]], tn, preferred_element_type=F32)
            dvp_ref[:, sl[h]] = lax.dot_general(pp[h].astype(BF16), do_b[h], tn, preferred_element_type=F32)
            dvc_ref[:, sl[h]] = lax.dot_general(pc[h].astype(BF16), do_b[h], tn, preferred_element_type=F32)

    dq, dkp, dkc, dvp, dvc = pl.pallas_call(
        body, out_shape=[jax.ShapeDtypeStruct(q.shape, BF16)] + [jax.ShapeDtypeStruct(q.shape, F32)] * 4, grid=(d, nb),
        in_specs=[cur, prev, cur, prev, cur, cur, cur, cur, cur], out_specs=[cur] * 5,
        compiler_params=_cparams("parallel", "parallel"), name=f"dil_bwd_d{d}")(q, k, k, v, v, o, lse, do, dlse)

    def fold(cur_part, prev_part):
        shifted = jnp.concatenate([prev_part[DIL_BLOCK:], jnp.zeros((DIL_BLOCK, prev_part.shape[1]), F32)], axis=0)
        return (cur_part + shifted).astype(BF16)

    return dq, fold(dkc, dkp), fold(dvc, dvp)


def _make_dil(d):
    @jax.custom_vjp
    def att(q, k, v):
        return tuple(_dil_fwd(q, k, v, d))

    def fwd(q, k, v):
        o, lse = _dil_fwd(q, k, v, d)
        return (o, lse), (q, k, v, o, lse)

    def bwd(saved, cts):
        return _dil_bwd(*saved, cts[0], cts[1], d)

    att.defvjp(fwd, bwd)
    return att


def dilated_group(q, k, v, d):
    return _make_dil(d)(q, k, v)


def _dn_post(c, kind, scale):
    m = _silu(c)
    nrm = m * lax.rsqrt(jnp.sum(m * m, axis=-1, keepdims=True) + 1e-6) * scale
    return kind * nrm + (1.0 - kind) * m


def _dn_kind_scale(j):
    kind = jnp.where(j < 2 * DN_HEADS, 1.0, 0.0).astype(F32)
    scale = jnp.where(j < DN_HEADS, DN_DIM ** -0.5, 1.0).astype(F32)
    return kind, scale


_CONV_RB = 512
_CONV_PAD = 8


def _conv_windows(pad_ref, w_ref, r0, rb, sign):
    acc = None
    for sh in range(DN_CONV):
        win = pad_ref[pl.ds(r0 + _CONV_PAD * (sign < 0) + sign * sh, rb), :]
        term = w_ref[DN_CONV - 1 - sh:DN_CONV - sh, :] * win
        acc = term if acc is None else acc + term
    return acc


def _dn_conv_fwd(x, w):
    s, width = x.shape
    rb = min(_CONV_RB, s)

    def body(x_ref, w_ref, o_ref, pad_ref):
        kind, scale = _dn_kind_scale(pl.program_id(0))
        pad_ref[0:_CONV_PAD, :] = jnp.zeros((_CONV_PAD, LANE), F32)
        pad_ref[_CONV_PAD:, :] = x_ref[...]
        for r0 in range(0, s, rb):
            c = _conv_windows(pad_ref, w_ref, r0, rb, -1)
            o_ref[r0:r0 + rb, :] = _dn_post(c, kind, scale)

    return pl.pallas_call(
        body, out_shape=jax.ShapeDtypeStruct(x.shape, F32), grid=(width // LANE,),
        in_specs=[pl.BlockSpec((s, LANE), lambda j: (0, j)), pl.BlockSpec((DN_CONV, LANE), lambda j: (0, j))],
        out_specs=pl.BlockSpec((s, LANE), lambda j: (0, j)), scratch_shapes=[pltpu.VMEM((s + _CONV_PAD, LANE), F32)],
        compiler_params=_cparams("parallel"), name="dn_conv_fwd")(x, w)


def _dn_conv_bwd(x, w, dy):
    s, width = x.shape
    rb = min(_CONV_RB, s)

    def body(x_ref, w_ref, dy_ref, dx_ref, dw_ref, pad_ref, dpad_ref):
        kind, scale = _dn_kind_scale(pl.program_id(0))
        pad_ref[0:_CONV_PAD, :] = jnp.zeros((_CONV_PAD, LANE), F32)
        pad_ref[_CONV_PAD:, :] = x_ref[...]
        dpad_ref[s:, :] = jnp.zeros((_CONV_PAD, LANE), F32)
        dws = [jnp.zeros((1, LANE), F32) for _ in range(DN_CONV)]
        for r0 in range(0, s, rb):
            c = _conv_windows(pad_ref, w_ref, r0, rb, -1)
            _, vjp = jax.vjp(lambda cc: _dn_post(cc, kind, scale), c)
            dc = vjp(dy_ref[r0:r0 + rb, :])[0]
            dpad_ref[r0:r0 + rb, :] = dc
            for sh in range(DN_CONV):
                win = pad_ref[pl.ds(r0 + _CONV_PAD - sh, rb), :]
                dws[DN_CONV - 1 - sh] = dws[DN_CONV - 1 - sh] + jnp.sum(dc * win, axis=0, keepdims=True)
        for j in range(DN_CONV):
            dw_ref[j:j + 1, :] = dws[j]
        for r0 in range(0, s, rb):
            dx_ref[r0:r0 + rb, :] = _conv_windows(dpad_ref, w_ref, r0, rb, 1)

    return pl.pallas_call(
        body, out_shape=[jax.ShapeDtypeStruct(x.shape, F32), jax.ShapeDtypeStruct(w.shape, F32)], grid=(width // LANE,),
        in_specs=[pl.BlockSpec((s, LANE), lambda j: (0, j)), pl.BlockSpec((DN_CONV, LANE), lambda j: (0, j)),
                  pl.BlockSpec((s, LANE), lambda j: (0, j))],
        out_specs=[pl.BlockSpec((s, LANE), lambda j: (0, j)), pl.BlockSpec((DN_CONV, LANE), lambda j: (0, j))],
        scratch_shapes=[pltpu.VMEM((s + _CONV_PAD, LANE), F32), pltpu.VMEM((s + _CONV_PAD, LANE), F32)],
        compiler_params=_cparams("parallel"), name="dn_conv_bwd")(x, w, dy)


@jax.custom_vjp
def dn_conv(x, w):
    return _dn_conv_fwd(x, w)


dn_conv.defvjp(lambda x, w: (_dn_conv_fwd(x, w), (x, w)), lambda saved, dy: tuple(_dn_conv_bwd(*saved, dy)))


_NN = (((1,), (0,)), ((), ()))
_NT = (((1,), (1,)), ((), ()))
_TN = (((0,), (0,)), ((), ()))


def _bd(a, b, dims, passes=1):
    d = lambda x, y: lax.dot_general(x, y, dims, preferred_element_type=F32)
    if passes == 0:
        return lax.dot_general(a, b, dims, precision=lax.Precision.HIGHEST, preferred_element_type=F32)
    ah, bh = a.astype(BF16), b.astype(BF16)
    if passes == 1:
        return d(ah, bh)
    al, bl = (a - ah.astype(F32)).astype(BF16), (b - bh.astype(F32)).astype(BF16)
    return d(ah, bh) + d(ah, bl) + d(al, bh)


@functools.partial(jax.custom_vjp, nondiff_argnums=(2, 3))
def _pdot(a, b, dims, passes):
    return _bd(a, b, dims, passes)


def _pdot_bwd(dims, passes, saved, g):
    a, b = saved
    if dims == _NN:
        return _bd(g, b, _NT, passes), _bd(a, g, _TN, passes)
    if dims == _NT:
        return _bd(g, b, _NN, passes), _bd(g, a, _TN, passes)
    return _bd(b, g, _NT, passes), _bd(a, g, _NN, passes)


_pdot.defvjp(lambda a, b, dims, passes: (_bd(a, b, dims, passes), (a, b)), _pdot_bwd)


GDN_DOT_PASSES = 1
GDN_SOLVE_PASSES = 3


def _hdot(a, b, dims=_NN):
    return _pdot(a, b, dims, GDN_DOT_PASSES)


def _xdot(a, b, dims=_NN):
    return _pdot(a, b, dims, GDN_SOLVE_PASSES)


def _split3(x):
    hi = x.astype(BF16)
    r1 = x - hi.astype(F32)
    mid = r1.astype(BF16)
    lo = (r1 - mid.astype(F32)).astype(BF16)
    return hi, mid, lo


def _tri_dot(tri, x, dims):
    t = tri.astype(BF16)
    return sum(lax.dot_general(t, p, dims, preferred_element_type=F32) for p in _split3(x))


@jax.custom_vjp
def _cumsum_rows(x):
    c = x.shape[0]
    tri = lax.broadcasted_iota(jnp.int32, (c, c), 0) >= lax.broadcasted_iota(jnp.int32, (c, c), 1)
    return _tri_dot(tri, x, _NN)


def _cumsum_rows_bwd(_, g):
    c = g.shape[0]
    tri = lax.broadcasted_iota(jnp.int32, (c, c), 0) >= lax.broadcasted_iota(jnp.int32, (c, c), 1)
    return (_tri_dot(tri, g, _TN),)


_cumsum_rows.defvjp(lambda x: (_cumsum_rows(x), None), _cumsum_rows_bwd)


@jax.custom_vjp
def _unit_lower_inverses(a):
    c = a[0].shape[0]
    eye = (lax.broadcasted_iota(jnp.int32, (c, c), 0) == lax.broadcasted_iota(jnp.int32, (c, c), 1)).astype(F32)
    pw = [-x for x in a]
    t = [eye + p for p in pw]
    for _ in range(int(math.log2(c)) - 1):
        pw = [_bd(p, p, _NN, GDN_SOLVE_PASSES) for p in pw]
        t = [x + _bd(x, p, _NN, GDN_SOLVE_PASSES) for x, p in zip(t, pw)]
    return t


def _unit_lower_inverses_bwd(t, g):
    left = [_bd(x, y, _TN, GDN_SOLVE_PASSES) for x, y in zip(t, g)]
    return ([-_bd(x, y, _NT, GDN_SOLVE_PASSES) for x, y in zip(left, t)],)


_unit_lower_inverses.defvjp(lambda a: (lambda t: (t, t))(_unit_lower_inverses(a)), _unit_lower_inverses_bwd)


@jax.custom_vjp
def _kept_inverses(a, t):
    return t


_kept_inverses.defvjp(lambda a, t: (t, t),
                      lambda t, g: (_unit_lower_inverses_bwd(t, g)[0], [jnp.zeros_like(x) for x in t]))


def _gdn_prep_fn(qkv, gb, kept=None, keep=None):
    c = GDN_CHUNK
    row = lax.broadcasted_iota(jnp.int32, (c, c), 0)
    col = lax.broadcasted_iota(jnp.int32, (c, c), 1)
    incl, strict = row >= col, row > col
    lane = _lanes(gb)
    heads = range(DN_HEADS)
    q, k, v = qkv[:DN_HEADS], qkv[DN_HEADS:2 * DN_HEADS], qkv[2 * DN_HEADS:]
    g = [jnp.sum(jnp.where(lane == h, gb, 0.0), axis=-1, keepdims=True) for h in heads]
    beta = [jnp.sum(jnp.where(lane == DN_HEADS + h, gb, 0.0), axis=-1, keepdims=True) for h in heads]
    gcb = [_cumsum_rows(jnp.broadcast_to(g[h], (c, c))) for h in heads]
    decay = [jnp.where(incl, jnp.exp(jnp.where(incl, gcb[h] - gcb[h].T, 0.0)), 0.0) for h in heads]
    kb = [k[h] * beta[h] for h in heads]
    a = [jnp.where(strict, _hdot(kb[h], k[h], _NT) * decay[h], 0.0) for h in heads]
    t = _unit_lower_inverses(a) if kept is None else _kept_inverses(a, kept)
    if keep is not None:
        keep.extend(t)
    eg = [jnp.exp(gcb[h]) for h in heads]
    u = [_xdot(t[h], v[h] * beta[h]) for h in heads]
    w = [_xdot(t[h], kb[h] * eg[h]) for h in heads]
    qk = [jnp.where(incl, _hdot(q[h], k[h], _NT) * decay[h], 0.0) for h in heads]
    g_last = [jnp.sum(jnp.where(row == c - 1, gcb[h], 0.0), axis=0, keepdims=True) for h in heads]
    kg = [k[h] * jnp.exp(g_last[h] - gcb[h]) for h in heads]
    qg = [q[h] * eg[h] for h in heads]
    el = [jnp.broadcast_to(jnp.exp(g_last[h]), (8, LANE)) for h in heads]
    return [u, w, qg, kg, qk, el]


def _gdn_prep_specs(nq):
    c = GDN_CHUNK
    big = pl.BlockSpec((c, nq * LANE), lambda n: (n, 0))
    return big


def _gdn_prep_fwd(qkv, gb):
    s = qkv.shape[0]
    c = GDN_CHUNK
    n = s // c
    hw = DN_HEADS * LANE

    def body(qkv_ref, gb_ref, u_ref, w_ref, qg_ref, kg_ref, qk_ref, el_ref, t_ref):
        kept = []
        res = _gdn_prep_fn(_chunks(qkv_ref), gb_ref[...], keep=kept)
        for ref, chs in zip((u_ref, w_ref, qg_ref, kg_ref, qk_ref, el_ref, t_ref), res + [kept]):
            _store(ref, chs)

    row = pl.BlockSpec((c, hw), lambda i: (i, 0))
    return pl.pallas_call(
        body, out_shape=[jax.ShapeDtypeStruct((s, hw), F32)] * 5 + [jax.ShapeDtypeStruct((n * 8, hw), F32),
                                                                   jax.ShapeDtypeStruct((s, hw), F32)], grid=(n,),
        in_specs=[pl.BlockSpec((c, 3 * hw), lambda i: (i, 0)), pl.BlockSpec((c, LANE), lambda i: (i, 0))],
        out_specs=[row] * 5 + [pl.BlockSpec((8, hw), lambda i: (i, 0)), row],
        compiler_params=_cparams("parallel"), name="gdn_prep_fwd")(qkv, gb)


def _gdn_prep_bwd(qkv, gb, t, cts):
    s = qkv.shape[0]
    c = GDN_CHUNK
    n = s // c
    hw = DN_HEADS * LANE

    def body(qkv_ref, gb_ref, t_ref, du_ref, dw_ref, dqg_ref, dkg_ref, dqk_ref, del_ref, dqkv_ref, dgb_ref):
        kept = _chunks(t_ref)
        _, vjp = jax.vjp(lambda x, y: _gdn_prep_fn(x, y, kept=kept), _chunks(qkv_ref), gb_ref[...])
        ct = [_chunks(r) for r in (du_ref, dw_ref, dqg_ref, dkg_ref, dqk_ref, del_ref)]
        dqkv, dgb = vjp(ct)
        _store(dqkv_ref, dqkv)
        dgb_ref[...] = dgb

    row = pl.BlockSpec((c, hw), lambda i: (i, 0))
    return pl.pallas_call(
        body, out_shape=[jax.ShapeDtypeStruct(qkv.shape, F32), jax.ShapeDtypeStruct(gb.shape, F32)], grid=(n,),
        in_specs=[pl.BlockSpec((c, 3 * hw), lambda i: (i, 0)), pl.BlockSpec((c, LANE), lambda i: (i, 0)), row]
        + [row] * 5 + [pl.BlockSpec((8, hw), lambda i: (i, 0))],
        out_specs=[pl.BlockSpec((c, 3 * hw), lambda i: (i, 0)), pl.BlockSpec((c, LANE), lambda i: (i, 0))],
        compiler_params=_cparams("parallel"), name="gdn_prep_bwd")(qkv, gb, t, *cts)


@jax.custom_vjp
def gdn_prep(qkv, gb):
    return tuple(_gdn_prep_fwd(qkv, gb)[:6])


def _gdn_prep_vfwd(qkv, gb):
    res = _gdn_prep_fwd(qkv, gb)
    return tuple(res[:6]), (qkv, gb, res[6])


gdn_prep.defvjp(_gdn_prep_vfwd, lambda saved, cts: tuple(_gdn_prep_bwd(*saved, cts)))


def _gdn_scan_fwd(u, w, qg, kg, qk, el):
    s = u.shape[0]
    c = GDN_CHUNK
    n = s // c
    hw = DN_HEADS * LANE

    def body(u_ref, w_ref, qg_ref, kg_ref, qk_ref, el_ref, o_ref, st_ref, s_sc):
        @pl.when(pl.program_id(0) == 0)
        def _():
            s_sc[...] = jnp.zeros_like(s_sc)

        heads = range(DN_HEADS)
        sl = [slice(h * LANE, (h + 1) * LANE) for h in heads]
        st = [s_sc[h] for h in heads]
        for h in heads:
            st_ref[sl[h], :] = st[h]
        v_new = [u_ref[:, sl[h]] - _hdot(w_ref[:, sl[h]], st[h]) for h in heads]
        o_st = [_hdot(qg_ref[:, sl[h]], st[h]) for h in heads]
        o_in = [_hdot(qk_ref[:, sl[h]], v_new[h]) for h in heads]
        s_up = [_hdot(kg_ref[:, sl[h]], v_new[h], _TN) for h in heads]
        for h in heads:
            o_ref[:, sl[h]] = o_st[h] + o_in[h]
            s_sc[h] = st[h] * el_ref[0:1, sl[h]] + s_up[h]

    row = pl.BlockSpec((c, hw), lambda i: (i, 0))
    return pl.pallas_call(
        body, out_shape=[jax.ShapeDtypeStruct((s, hw), F32), jax.ShapeDtypeStruct((n, hw, LANE), F32)], grid=(n,),
        in_specs=[row] * 5 + [pl.BlockSpec((8, hw), lambda i: (i, 0))],
        out_specs=[row, pl.BlockSpec((None, hw, LANE), lambda i: (i, 0, 0))],
        scratch_shapes=[pltpu.VMEM((DN_HEADS, LANE, LANE), F32)],
        compiler_params=_cparams("arbitrary"), name="gdn_scan_fwd")(u, w, qg, kg, qk, el)


def _gdn_scan_bwd(u, w, qg, kg, qk, el, states, do):
    s = u.shape[0]
    c = GDN_CHUNK
    n = s // c
    hw = DN_HEADS * LANE

    def body(u_ref, w_ref, qg_ref, kg_ref, qk_ref, el_ref, st_ref, do_ref,
             du_ref, dw_ref, dqg_ref, dkg_ref, dqk_ref, del_ref, ds_sc):
        @pl.when(pl.program_id(0) == 0)
        def _():
            ds_sc[...] = jnp.zeros_like(ds_sc)

        heads = range(DN_HEADS)
        sl = [slice(h * LANE, (h + 1) * LANE) for h in heads]
        st = [st_ref[sl[h], :] for h in heads]
        ds = [ds_sc[h] for h in heads]
        do = [do_ref[:, sl[h]] for h in heads]
        v_new = [u_ref[:, sl[h]] - _hdot(w_ref[:, sl[h]], st[h]) for h in heads]
        dv_new = [_hdot(qk_ref[:, sl[h]], do[h], _TN) + _hdot(kg_ref[:, sl[h]], ds[h]) for h in heads]
        first_row = lax.broadcasted_iota(jnp.int32, (8, LANE), 0) == 0
        for h in heads:
            du_ref[:, sl[h]] = dv_new[h]
            dw_ref[:, sl[h]] = -_hdot(dv_new[h], st[h], _NT)
            dqg_ref[:, sl[h]] = _hdot(do[h], st[h], _NT)
            dqk_ref[:, sl[h]] = _hdot(do[h], v_new[h], _NT)
            dkg_ref[:, sl[h]] = _hdot(v_new[h], ds[h], _NT)
            del_ref[:, sl[h]] = jnp.where(first_row, jnp.sum(st[h] * ds[h], axis=0, keepdims=True), 0.0)
        ds_new = [_hdot(qg_ref[:, sl[h]], do[h], _TN) + ds[h] * el_ref[0:1, sl[h]] - _hdot(w_ref[:, sl[h]], dv_new[h], _TN)
                  for h in heads]
        for h in heads:
            ds_sc[h] = ds_new[h]

    row = pl.BlockSpec((c, hw), lambda i: (n - 1 - i, 0))
    small = pl.BlockSpec((8, hw), lambda i: (n - 1 - i, 0))
    return pl.pallas_call(
        body, out_shape=[jax.ShapeDtypeStruct((s, hw), F32)] * 5 + [jax.ShapeDtypeStruct((n * 8, hw), F32)], grid=(n,),
        in_specs=[row] * 5 + [small, pl.BlockSpec((None, hw, LANE), lambda i: (n - 1 - i, 0, 0)), row],
        out_specs=[row] * 5 + [small], scratch_shapes=[pltpu.VMEM((DN_HEADS, LANE, LANE), F32)],
        compiler_params=_cparams("arbitrary"), name="gdn_scan_bwd")(u, w, qg, kg, qk, el, states, do)


@jax.custom_vjp
def gdn_scan(u, w, qg, kg, qk, el):
    return _gdn_scan_fwd(u, w, qg, kg, qk, el)[0]


def _gdn_scan_vfwd(*args):
    o, states = _gdn_scan_fwd(*args)
    return o, args + (states,)


gdn_scan.defvjp(_gdn_scan_vfwd, lambda saved, do: tuple(_gdn_scan_bwd(*saved, do)))


def _loss_call(y, t):
    s, d = y.shape
    br = min(ROW_BLOCK, s)
    n = s // br

    def body(y_ref, t_ref, loss_ref, dy_ref, acc):
        i = pl.program_id(0)

        @pl.when(i == 0)
        def _():
            acc[...] = jnp.zeros_like(acc)

        e = y_ref[...] - t_ref[...]
        dy_ref[...] = e / d
        acc[...] += jnp.sum(e * e, axis=0, keepdims=True)

        @pl.when(i == n - 1)
        def _():
            loss_ref[...] = jnp.broadcast_to(jnp.sum(acc[...], axis=1, keepdims=True) * (0.5 / d), loss_ref.shape)

    row = pl.BlockSpec((br, d), lambda i: (i, 0))
    return pl.pallas_call(
        body, out_shape=[jax.ShapeDtypeStruct((1, LANE), F32), jax.ShapeDtypeStruct((s, d), F32)], grid=(n,),
        in_specs=[row, row], out_specs=[pl.BlockSpec((1, LANE), lambda i: (0, 0)), row],
        scratch_shapes=[pltpu.VMEM((1, d), F32)], compiler_params=_cparams("arbitrary"), name="loss_head")(y, t)


@jax.custom_vjp
def loss_head(y, t):
    return _loss_call(y, t)[0][0, 0]


def _loss_head_fwd(y, t):
    loss, dy = _loss_call(y, t)
    return loss[0, 0], (dy,)


loss_head.defvjp(_loss_head_fwd, lambda saved, g: (saved[0] * g, -saved[0] * g))


IN_OFF = {}
_o = 0
for _name, _size in (('q_lat', 384), ('kv_lat', 320), ('z_a', 512), ('dn_qkv', 1536), ('dn_ab', 8), ('z_b', 512),
                     ('dil_qkv', 4608), ('z_c', 512), ('gate', 3072)):
    IN_OFF[_name] = (_o, _o + _size)
    _o += _size
IN_WIDTH = _o
N_CHIPS = 4


def _cols(w, name):
    a, b = IN_OFF[name]
    return w[:, a:b]


def _pad_cols(w, to):
    return jnp.concatenate([w, jnp.zeros((w.shape[0], to - w.shape[1]), w.dtype)], axis=1)


def _pad_row(v, to=None):
    v = v.reshape(1, -1)
    return v if to is None or v.shape[1] == to else _pad_cols(v, to)


def _shard_cols(pieces, a, b):
    wsh = pieces[0].shape[1]
    parts = [pieces[j][:, max(a, j * wsh) - j * wsh:min(b, (j + 1) * wsh) - j * wsh]
             for j in range(len(pieces)) if max(a, j * wsh) < min(b, (j + 1) * wsh)]
    return parts[0] if len(parts) == 1 else jnp.concatenate(parts, axis=1)


def _win_groups_impl(w_in4):
    out = []
    for l in range(w_in4.shape[1]):
        pieces = [w_in4[j, l] for j in range(w_in4.shape[0])]
        cols = lambda name: _shard_cols(pieces, *IN_OFF[name])
        out.append((_pad_cols(jnp.concatenate([cols('q_lat'), cols('kv_lat')], axis=1), 768),
                    jnp.concatenate([cols('z_a'), cols('z_b'), cols('z_c')], axis=1),
                    cols('dn_qkv'), _pad_cols(cols('dn_ab'), LANE), cols('dil_qkv'), cols('gate')))
    return tuple(out)


@jax.custom_vjp
def win_groups(w_in4):
    return _win_groups_impl(w_in4)


def _win_groups_bwd(_, cts):
    n_chip, depth = N_CHIPS, len(cts)
    wsh = IN_WIDTH // n_chip
    per_layer = []
    for l in range(depth):
        dmla, dz, dgdn, dab, ddil, dgate = cts[l]
        full = jnp.concatenate([dmla[:, :704], dz[:, :512], dgdn, dab[:, :8], dz[:, 512:1024], ddil, dz[:, 1024:], dgate], axis=1)
        per_layer.append([full[:, j * wsh:(j + 1) * wsh] for j in range(n_chip)])
    return (jnp.stack([jnp.stack([per_layer[l][j] for l in range(depth)]) for j in range(n_chip)]),)


win_groups.defvjp(lambda w: (_win_groups_impl(w), None), _win_groups_bwd)


def _layer(x, p, tabs):
    w_mla, w_z, w_gdn, w_ab, w_dil, w_gate = p['w_in_groups']
    h, = rowwise(f_rms_full, "rms_in", [x], [_pad_row(p['norm_g'])], [(D_MODEL, BF16)])
    mla_in = matmul(h, w_mla)
    z = matmul(h, w_z)
    dn_qkv = matmul(h, w_gdn)
    dn_ab = matmul(h, w_ab)
    dil_qkv = matmul(h, w_dil)
    gate = matmul(h, w_gate)

    qn, kvn, kpe = rowwise(f_mla1, "mla_norm", [mla_in], [_pad_row(p['mla_q_a_norm_g']), _pad_row(p['mla_kv_a_norm_g'])],
                           [(MLA_Q_RANK, BF16), (MLA_KV_RANK, BF16), (LANE, F32)])
    wq = p['mla_w_q_b']
    wq_pad = jnp.concatenate(
        [wq[:, hh * MLA_QK:hh * MLA_QK + MLA_NOPE] for hh in range(MLA_HEADS)]
        + [_pad_cols(wq[:, hh * MLA_QK + MLA_NOPE:(hh + 1) * MLA_QK], LANE) for hh in range(MLA_HEADS)], axis=1)
    q = matmul(qn, wq_pad)
    kv = matmul(kvn, p['mla_w_kv_b'])
    gq, gk = p['mla_q_norm_g'], p['mla_k_norm_g']
    q_att, k_att, v_att = rowwise(
        f_mla2, "mla_qk", [q, kv, kpe, tabs['cos_r'], tabs['sin_r']],
        [_pad_row(gq[:MLA_NOPE]), _pad_row(gq[MLA_NOPE:], LANE), _pad_row(gk[:MLA_NOPE]), _pad_row(gk[MLA_NOPE:], LANE)],
        [(MLA_HEADS * MLA_DQK, BF16), (MLA_HEADS * MLA_DQK, BF16), (MLA_HEADS * HEAD, BF16)], nograd=(3, 4))
    y_a = mla_attention(q_att, k_att, v_att)

    qkv_n = dn_conv(dn_qkv, p['dn_conv_w'])
    gb, = rowwise(f_gates, "dn_gates", [dn_ab], [_pad_row(p['dn_a_log'], LANE), _pad_row(p['dn_dt_bias'], LANE)], [(LANE, F32)])
    o_b = gdn_scan(*gdn_prep(qkv_n, gb))
    y_b, = rowwise(f_headnorm, "dn_out_norm", [o_b], [_pad_row(p['dn_out_norm_g'])], [(DN_HEADS * DN_DIM, F32)])

    qkv_d = rowwise(f_dil, "dil_qk", [dil_qkv, tabs['cos_h'], tabs['sin_h']],
                    [_pad_row(p['dil_q_norm_g']), _pad_row(p['dil_k_norm_g'])],
                    [(GROUP_W, BF16)] * (3 * DIL_GROUPS), nograd=(1, 2), out_dil=[d for d in DIL_DILATIONS for _ in range(3)])
    outs, lses = [], []
    for gi, d in enumerate(DIL_DILATIONS):
        o_g, l_g = dilated_group(*qkv_d[3 * gi:3 * gi + 3], d)
        outs.append(o_g)
        lses.append(l_g)
    y_c, = rowwise(f_comb, "dil_comb", outs + lses, [], [(GROUP_W, F32)], row_dil=list(DIL_DILATIONS) * 2)

    ys = rowwise(f_merge1, "merge_silu", [y_a, y_b, y_c, z], [], [(BRANCH_W, BF16)] * 3)
    bo = [matmul(ys[b], p['w_branch'][b]) for b in range(3)]
    mixed, = rowwise(f_merge2, "merge_gate", [gate] + bo, [], [(D_MODEL, BF16)])
    return matmul(mixed, p['w_out'], res=x)


def _rope_tables(pos, dim):
    inv_freq = 1.0 / (ROPE_THETA ** (jnp.arange(0, dim, 2, dtype=F32) / dim))
    ang = pos.astype(F32)[:, None] * inv_freq
    return jnp.cos(ang), jnp.sin(ang)


def _tables(pos):
    cr, sr = _rope_tables(pos, MLA_ROPE)
    ch, sh = _rope_tables(pos, HEAD)
    zero = jnp.zeros((pos.shape[0], LANE - MLA_ROPE), F32)
    return {'cos_r': jnp.concatenate([cr, cr, zero], axis=1), 'sin_r': jnp.concatenate([sr, sr, zero], axis=1),
            'cos_h': jnp.concatenate([ch, ch], axis=1), 'sin_h': jnp.concatenate([sh, sh], axis=1)}


def _local_loss(w_in4, mats, conv_w, small, x, target, tabs):
    groups = win_groups(w_in4)
    for l in range(DEPTH):
        p = {k: v[l] for k, v in mats.items()}
        p.update({k: v[l] for k, v in small.items()})
        p['dn_conv_w'] = conv_w[l]
        p['w_in_groups'] = groups[l]
        x = _layer(x, p, tabs)
    return loss_head(x, target)


def _pack(arrays, dtype, row_tile):
    flat = jnp.concatenate([a.astype(dtype).reshape(-1) for a in arrays])
    rows = -(-flat.shape[0] // (LANE * row_tile)) * row_tile
    flat = jnp.concatenate([flat, jnp.zeros((rows * LANE - flat.shape[0],), dtype)])
    return flat.reshape(rows, LANE)


def _unpack_impl(buf, shapes):
    flat = buf.reshape(-1)
    out, off = [], 0
    for shp in shapes:
        n = math.prod(shp)
        out.append(flat[off:off + n].reshape(shp))
        off += n
    return tuple(out)


@functools.partial(jax.custom_vjp, nondiff_argnums=(1, 2, 3))
def _unpack_p(buf, shapes, dtype_name, rows):
    return _unpack_impl(buf, shapes)


_unpack_p.defvjp(lambda buf, shapes, dtype_name, rows: (_unpack_impl(buf, shapes), None),
                 lambda shapes, dtype_name, rows, _, cts: (_pack(cts, jnp.dtype(dtype_name), rows),))


def _unpack(buf, shapes):
    return _unpack_p(buf, tuple(shapes), jnp.dtype(buf.dtype).name, buf.shape[0])


def _full_from_chips(buf4, shard_shapes):
    per_chip = [_unpack(buf4[j], tuple(shard_shapes)) for j in range(4)]
    return {name: jnp.concatenate([per_chip[j][i] for j in range(4)], axis=axis) for i, (name, axis) in enumerate(MATS)}


_HBM = pl.BlockSpec(memory_space=pltpu.HBM)
_VMEM = pl.BlockSpec(memory_space=pltpu.VMEM)


def _chip_peers(x, y):
    return [(1 - x, y), (x, 1 - y), (1 - x, 1 - y)]


def chip_all_to_all(arrays):
    n = len(arrays)

    def body(*refs):
        in_refs, out_refs = refs[:n], refs[n:2 * n]
        send_sems, recv_sems, local_sems = refs[2 * n:]
        x, y, c = lax.axis_index("x"), lax.axis_index("y"), lax.axis_index("c")
        me = 2 * x + y
        peers = _chip_peers(x, y)
        local, sends = [], []
        for a, (i_ref, o_ref) in enumerate(zip(in_refs, out_refs)):
            local.append(pltpu.make_async_copy(i_ref.at[me], o_ref.at[me], local_sems.at[a]))
            local[-1].start()
            for k, (px, py) in enumerate(peers):
                sends.append(pltpu.make_async_remote_copy(
                    src_ref=i_ref.at[2 * px + py], dst_ref=o_ref.at[me], send_sem=send_sems.at[3 * a + k],
                    recv_sem=recv_sems.at[3 * a + k], device_id=(px, py, c), device_id_type=MESH))
                sends[-1].start()
        for a, (i_ref, o_ref) in enumerate(zip(in_refs, out_refs)):
            for k, (px, py) in enumerate(peers):
                pltpu.make_async_remote_copy(
                    src_ref=i_ref.at[me], dst_ref=o_ref.at[2 * px + py], send_sem=send_sems.at[3 * a + k],
                    recv_sem=recv_sems.at[3 * a + k], device_id=(px, py, c), device_id_type=MESH).wait_recv()
        for cp in sends:
            cp.wait_send()
        for cp in local:
            cp.wait()

    return pl.pallas_call(
        body, out_shape=[jax.ShapeDtypeStruct(a.shape, a.dtype) for a in arrays], in_specs=[_HBM] * n, out_specs=[_HBM] * n,
        scratch_shapes=[pltpu.SemaphoreType.DMA((3 * n,)), pltpu.SemaphoreType.DMA((3 * n,)), pltpu.SemaphoreType.DMA((n,))],
        name="chip_all_to_all")(*arrays)


def chip_all_gather(shards):
    n = len(shards)

    def body(*refs):
        in_refs, out_refs = refs[:n], refs[n:2 * n]
        send_sems, recv_sems = refs[2 * n:]
        x, y, c = lax.axis_index("x"), lax.axis_index("y"), lax.axis_index("c")
        me = 2 * x + y
        peers = _chip_peers(x, y)
        sends = []
        for a, (i_ref, o_ref) in enumerate(zip(in_refs, out_refs)):
            for k, (px, py) in enumerate(peers):
                sends.append(pltpu.make_async_remote_copy(
                    src_ref=i_ref.at[c], dst_ref=o_ref.at[me, c], send_sem=send_sems.at[6 * a + k],
                    recv_sem=recv_sems.at[6 * a + k], device_id=(px, py, c), device_id_type=MESH))
                sends[-1].start()
        for a, (i_ref, o_ref) in enumerate(zip(in_refs, out_refs)):
            for k, (px, py) in enumerate(peers):
                landed = o_ref.at[2 * px + py, c]
                pltpu.make_async_remote_copy(src_ref=i_ref.at[c], dst_ref=landed, send_sem=send_sems.at[6 * a + k],
                                             recv_sem=recv_sems.at[6 * a + k], device_id=(px, py, c), device_id_type=MESH).wait_recv()
                sends.append(pltpu.make_async_remote_copy(
                    src_ref=landed, dst_ref=landed, send_sem=send_sems.at[6 * a + 3 + k], recv_sem=recv_sems.at[6 * a + 3 + k],
                    device_id=(x, y, 1 - c), device_id_type=MESH))
                sends[-1].start()
        for a, (i_ref, o_ref) in enumerate(zip(in_refs, out_refs)):
            for k, (px, py) in enumerate(peers):
                other = o_ref.at[2 * px + py, 1 - c]
                pltpu.make_async_remote_copy(src_ref=other, dst_ref=other, send_sem=send_sems.at[6 * a + 3 + k],
                                             recv_sem=recv_sems.at[6 * a + 3 + k], device_id=(x, y, 1 - c),
                                             device_id_type=MESH).wait_recv()
        for cp in sends:
            cp.wait_send()

    me = 2 * lax.axis_index("x") + lax.axis_index("y")
    outs = pl.pallas_call(
        body, out_shape=[jax.ShapeDtypeStruct((4,) + a.shape, a.dtype) for a in shards], in_specs=[_HBM] * n, out_specs=[_HBM] * n,
        scratch_shapes=[pltpu.SemaphoreType.DMA((6 * n,)), pltpu.SemaphoreType.DMA((6 * n,))],
        name="chip_all_gather")(*shards)
    return [lax.dynamic_update_index_in_dim(o, a, me, 0) for o, a in zip(outs, shards)]


def sibling_swap(arrays):
    n = len(arrays)

    def body(*refs):
        in_refs, out_refs = refs[:n], refs[n:2 * n]
        send_sems, recv_sems = refs[2 * n:]
        x, y, c = lax.axis_index("x"), lax.axis_index("y"), lax.axis_index("c")
        cps = [pltpu.make_async_remote_copy(src_ref=i_ref, dst_ref=o_ref, send_sem=send_sems.at[a], recv_sem=recv_sems.at[a],
                                            device_id=(x, y, 1 - c), device_id_type=MESH)
               for a, (i_ref, o_ref) in enumerate(zip(in_refs, out_refs))]
        for cp in cps:
            cp.start()
        for cp in cps:
            cp.wait()

    return pl.pallas_call(
        body, out_shape=[jax.ShapeDtypeStruct(a.shape, a.dtype) for a in arrays], in_specs=[_HBM] * n, out_specs=[_HBM] * n,
        scratch_shapes=[pltpu.SemaphoreType.DMA((n,)), pltpu.SemaphoreType.DMA((n,))], name="sibling_swap")(*arrays)


def sibling_all_gather(halves):
    n = len(halves)

    def body(*refs):
        in_refs, out_refs = refs[:n], refs[n:2 * n]
        send_sems, recv_sems = refs[2 * n:]
        x, y, c = lax.axis_index("x"), lax.axis_index("y"), lax.axis_index("c")
        sends = [pltpu.make_async_remote_copy(src_ref=i_ref, dst_ref=o_ref.at[c], send_sem=send_sems.at[a], recv_sem=recv_sems.at[a],
                                              device_id=(x, y, 1 - c), device_id_type=MESH)
                 for a, (i_ref, o_ref) in enumerate(zip(in_refs, out_refs))]
        for cp in sends:
            cp.start()
        for a, (i_ref, o_ref) in enumerate(zip(in_refs, out_refs)):
            pltpu.make_async_remote_copy(src_ref=i_ref, dst_ref=o_ref.at[1 - c], send_sem=send_sems.at[a], recv_sem=recv_sems.at[a],
                                         device_id=(x, y, 1 - c), device_id_type=MESH).wait_recv()
        for cp in sends:
            cp.wait_send()

    outs = pl.pallas_call(
        body, out_shape=[jax.ShapeDtypeStruct((2,) + a.shape, a.dtype) for a in halves], in_specs=[_HBM] * n, out_specs=[_HBM] * n,
        scratch_shapes=[pltpu.SemaphoreType.DMA((n,)), pltpu.SemaphoreType.DMA((n,))], name="sibling_all_gather")(*halves)
    return [lax.dynamic_update_index_in_dim(o, a, lax.axis_index("c"), 0) for o, a in zip(outs, halves)]


def all_gather8(v, name):
    def body(v_ref, out_ref, send_sems, recv_sems):
        x, y, c = lax.axis_index("x"), lax.axis_index("y"), lax.axis_index("c")
        out_ref[4 * x + 2 * y + c] = v_ref[...]

        def peer(k):
            return (x ^ (k >> 2), y ^ ((k >> 1) & 1), c ^ (k & 1))

        sends = [pltpu.make_async_remote_copy(src_ref=v_ref, dst_ref=out_ref.at[4 * x + 2 * y + c], send_sem=send_sems.at[k - 1],
                                              recv_sem=recv_sems.at[k - 1], device_id=peer(k), device_id_type=MESH)
                 for k in range(1, 8)]
        for cp in sends:
            cp.start()
        for k in range(1, 8):
            px, py, pc = peer(k)
            pltpu.make_async_remote_copy(src_ref=v_ref, dst_ref=out_ref.at[4 * px + 2 * py + pc], send_sem=send_sems.at[k - 1],
                                         recv_sem=recv_sems.at[k - 1], device_id=peer(k), device_id_type=MESH).wait_recv()
        for cp in sends:
            cp.wait_send()

    return pl.pallas_call(
        body, out_shape=jax.ShapeDtypeStruct((8,) + v.shape, v.dtype), in_specs=[_VMEM], out_specs=_VMEM,
        scratch_shapes=[pltpu.SemaphoreType.DMA((7,)), pltpu.SemaphoreType.DMA((7,))], name=name)(v)


def pair_add(a, b, row_tile, name):
    rows, width = a.shape

    def body(a_ref, b_ref, o_ref):
        o_ref[...] = (a_ref[...].astype(F32) + b_ref[...].astype(F32)).astype(o_ref.dtype)

    spec = pl.BlockSpec((row_tile, width), lambda i: (i, 0))
    return pl.pallas_call(body, out_shape=jax.ShapeDtypeStruct(a.shape, a.dtype), grid=(rows // row_tile,),
                          in_specs=[spec, spec], out_specs=spec, compiler_params=_cparams("parallel"), name=name)(a, b)


def sum_blocks(blocks, row_tile, name):
    n, rows, width = blocks.shape

    def body(b_ref, o_ref):
        acc = b_ref[0].astype(F32)
        for j in range(1, n):
            acc = acc + b_ref[j].astype(F32)
        o_ref[...] = acc

    return pl.pallas_call(
        body, out_shape=jax.ShapeDtypeStruct((rows, width), F32), grid=(rows // row_tile,),
        in_specs=[pl.BlockSpec((n, row_tile, width), lambda i: (0, i, 0))],
        out_specs=pl.BlockSpec((row_tile, width), lambda i: (i, 0)), compiler_params=_cparams("parallel"), name=name)(blocks)


def adamw(g_parts, w, m, v, row_tile, name):
    npart = len(g_parts)

    def body(*refs):
        g = refs[0][...]
        for r in refs[1:npart]:
            g = g + r[...]
        w_ref, m_ref, v_ref, g_out, d_out, m_out, v_out = refs[npart:]
        m_new = ADAM_B1 * m_ref[...] + (1.0 - ADAM_B1) * g
        v_new = ADAM_B2 * v_ref[...] + (1.0 - ADAM_B2) * (g * g)
        m_hat = m_new / (1.0 - ADAM_B1 ** ADAM_STEP)
        v_hat = v_new / (1.0 - ADAM_B2 ** ADAM_STEP)
        g_out[...] = g
        d_out[...] = -ADAM_LR * (m_hat / (jnp.sqrt(v_hat) + ADAM_EPS) + ADAM_WD * w_ref[...])
        m_out[...] = m_new
        v_out[...] = v_new

    rows, width = w.shape
    spec = pl.BlockSpec((row_tile, width), lambda i: (i, 0))
    return pl.pallas_call(
        body, out_shape=[jax.ShapeDtypeStruct(w.shape, F32)] * 4, grid=(rows // row_tile,),
        in_specs=[spec] * (npart + 3), out_specs=[spec] * 4, compiler_params=_cparams("parallel"), name=name)(*g_parts, w, m, v)


def kernel(x, positions, norm_g, w_in, mla_q_a_norm_g, mla_w_q_b, mla_kv_a_norm_g, mla_w_kv_b, mla_q_norm_g, mla_k_norm_g, dn_conv_w, dn_a_log, dn_dt_bias, dn_out_norm_g, dil_q_norm_g, dil_k_norm_g, w_branch, w_out, loss_target, m_norm_g, m_w_in, m_mla_q_a_norm_g, m_mla_w_q_b, m_mla_kv_a_norm_g, m_mla_w_kv_b, m_mla_q_norm_g, m_mla_k_norm_g, m_dn_conv_w, m_dn_a_log, m_dn_dt_bias, m_dn_out_norm_g, m_dil_q_norm_g, m_dil_k_norm_g, m_w_branch, m_w_out, v_norm_g, v_w_in, v_mla_q_a_norm_g, v_mla_w_q_b, v_mla_kv_a_norm_g, v_mla_w_kv_b, v_mla_q_norm_g, v_mla_k_norm_g, v_dn_conv_w, v_dn_a_log, v_dn_dt_bias, v_dn_out_norm_g, v_dil_q_norm_g, v_dil_k_norm_g, v_w_branch, v_w_out):
    w = dict(norm_g=norm_g, w_in=w_in, mla_q_a_norm_g=mla_q_a_norm_g, mla_w_q_b=mla_w_q_b, mla_kv_a_norm_g=mla_kv_a_norm_g,
             mla_w_kv_b=mla_w_kv_b, mla_q_norm_g=mla_q_norm_g, mla_k_norm_g=mla_k_norm_g, dn_conv_w=dn_conv_w, dn_a_log=dn_a_log,
             dn_dt_bias=dn_dt_bias, dn_out_norm_g=dn_out_norm_g, dil_q_norm_g=dil_q_norm_g, dil_k_norm_g=dil_k_norm_g,
             w_branch=w_branch, w_out=w_out)
    m = dict(norm_g=m_norm_g, w_in=m_w_in, mla_q_a_norm_g=m_mla_q_a_norm_g, mla_w_q_b=m_mla_w_q_b, mla_kv_a_norm_g=m_mla_kv_a_norm_g,
             mla_w_kv_b=m_mla_w_kv_b, mla_q_norm_g=m_mla_q_norm_g, mla_k_norm_g=m_mla_k_norm_g, dn_conv_w=m_dn_conv_w,
             dn_a_log=m_dn_a_log, dn_dt_bias=m_dn_dt_bias, dn_out_norm_g=m_dn_out_norm_g, dil_q_norm_g=m_dil_q_norm_g,
             dil_k_norm_g=m_dil_k_norm_g, w_branch=m_w_branch, w_out=m_w_out)
    v = dict(norm_g=v_norm_g, w_in=v_w_in, mla_q_a_norm_g=v_mla_q_a_norm_g, mla_w_q_b=v_mla_w_q_b, mla_kv_a_norm_g=v_mla_kv_a_norm_g,
             mla_w_kv_b=v_mla_w_kv_b, mla_q_norm_g=v_mla_q_norm_g, mla_k_norm_g=v_mla_k_norm_g, dn_conv_w=v_dn_conv_w,
             dn_a_log=v_dn_a_log, dn_dt_bias=v_dn_dt_bias, dn_out_norm_g=v_dn_out_norm_g, dil_q_norm_g=v_dil_q_norm_g,
             dil_k_norm_g=v_dil_k_norm_g, w_branch=v_w_branch, w_out=v_w_out)
    chip = 2 * lax.axis_index("x") + lax.axis_index("y")
    mat_names = [n for n, _ in MATS]
    mat_shapes = tuple(w[n].shape for n in mat_names)
    conv_shard = dn_conv_w.shape
    win_shape = w_in.shape
    win_rows = (win_shape[0] * win_shape[1], win_shape[2])

    mats_sh = _pack([w[n] for n in mat_names], BF16, 2 * MAT_ROWS)
    mat_rows = mats_sh.shape[0]
    w_in4, mats4 = chip_all_gather([w_in.astype(BF16), mats_sh.reshape(2, mat_rows // 2, LANE)])
    mats4 = mats4.reshape(N_CHIPS, mat_rows, LANE)
    conv8 = all_gather8(_pack([dn_conv_w], F32, 8), "gather_conv_w")
    conv_full = jnp.concatenate([_unpack(conv8[2 * j], (conv_shard,))[0] for j in range(4)], axis=2)
    small = {n: w[n] for n in SMALL}
    tabs = _tables(positions[0])

    def loss_fn(w_in4, mats4, conv_full, small, xs):
        return _local_loss(w_in4, _full_from_chips(mats4, mat_shapes), conv_full, small, xs, loss_target[0], tabs)

    loss, (g_win4, g_mats4, g_conv, g_small, g_x) = jax.value_and_grad(loss_fn, argnums=(0, 1, 2, 3, 4))(
        w_in4, mats4, conv_full, small, x[0])
    loss = lax.psum(loss, ("x", "y", "c"))

    c_idx = lax.axis_index("c")
    half_win = (N_CHIPS * win_shape[1], win_shape[2])
    half_mats = (N_CHIPS * (mat_rows // 2), LANE)
    g_mats4 = g_mats4.reshape(N_CHIPS, 2, mat_rows // 2, LANE)
    pick = lambda g, i, shape: lax.dynamic_index_in_dim(g, i, axis=1, keepdims=False).reshape(shape)
    from_sib = sibling_swap([pick(g_win4, 1 - c_idx, half_win), pick(g_mats4, 1 - c_idx, half_mats)])
    s_win = pair_add(pick(g_win4, c_idx, half_win), from_sib[0], WIN_ROWS, "pair_add_w_in")
    s_mats = pair_add(pick(g_mats4, c_idx, half_mats), from_sib[1], MAT_ROWS, "pair_add_mats")
    r_win, r_mats = chip_all_to_all([s_win.reshape(N_CHIPS, win_shape[1], win_shape[2]),
                                     s_mats.reshape(N_CHIPS, mat_rows // 2, LANE)])
    g_win, g_mats = sibling_all_gather([sum_blocks(r_win, WIN_ROWS, "sum_chip_pieces_w_in"),
                                        sum_blocks(r_mats, MAT_ROWS, "sum_chip_pieces_mats")])
    res_win = adamw([g_win.reshape(win_rows)], *[d['w_in'].reshape(win_rows) for d in (w, m, v)], WIN_ROWS, "adamw_w_in")
    packed = [_pack([d[n] for n in mat_names], F32, 2 * MAT_ROWS) for d in (w, m, v)]
    res_big = [dict(zip(mat_names, _unpack(r, mat_shapes)))
               for r in adamw([g_mats.reshape(mat_rows, LANE)], *packed, MAT_ROWS, "adamw_mats")]
    for k in range(4):
        res_big[k]['w_in'] = res_win[k].reshape(win_shape)

    small_shapes = tuple(w[n].shape for n in SMALL) + (g_conv.shape,)
    g_all = sum_blocks(all_gather8(_pack([g_small[n] for n in SMALL] + [g_conv], F32, 8), "gather_small_grads"), 8, "sum_small")
    g_list = list(_unpack(g_all, small_shapes))
    g_list[-1] = lax.dynamic_slice_in_dim(g_list[-1], chip * conv_shard[2], conv_shard[2], axis=2)
    small_names = list(SMALL) + ['dn_conv_w']
    packed_s = [_pack([d[n] for n in small_names], F32, 8) for d in (w, m, v)]
    shapes_s = tuple(w[n].shape for n in small_names)
    res_small = [dict(zip(small_names, _unpack(r, shapes_s))) for r in adamw([_pack(g_list, F32, 8)], *packed_s, 8, "adamw_small")]

    outs = [loss, g_x[None]]
    for k in range(4):
        outs += [res_big[k][n] if n in res_big[k] else res_small[k][n] for n in WEIGHTS]
    return tuple(outs)
```

```python
import functools
import math

import jax
import jax.numpy as jnp
from jax import lax
from jax.experimental import pallas as pl
from jax.experimental.pallas import tpu as pltpu

F32 = jnp.float32
BF16 = jnp.bfloat16
HI = lax.Precision.HIGHEST
MESH = pl.DeviceIdType.MESH

LANE = 128
VMEM_LIMIT = 48 * 1024 * 1024
ROW_BLOCK = 256
MM_TM, MM_TN, MM_TK = 1024, 1024, 1024
MAT_ROWS = 512
WIN_ROWS = 128

RMS_EPS = 1e-6
ROPE_THETA = 10000.0
D_MODEL = 1024
DEPTH = 2
MLA_HEADS = 4
MLA_NOPE = 128
MLA_ROPE = 64
MLA_QK = MLA_NOPE + MLA_ROPE
MLA_Q_RANK = 384
MLA_KV_RANK = 256
DN_HEADS = 4
DN_DIM = 128
DN_CONV = 4
GDN_CHUNK = 128
DIL_WINDOWS = (128, 512, 2048)
DIL_DILATIONS = (1, 4, 16)
DIL_GROUPS = 3
DIL_BLOCK = 128
HEAD = 128
BRANCH_W = 512

ADAM_LR = 0.001
ADAM_B1 = 0.9
ADAM_B2 = 0.999
ADAM_EPS = 1e-08
ADAM_WD = 0.01
ADAM_STEP = 10

WEIGHTS = ['norm_g', 'w_in', 'mla_q_a_norm_g', 'mla_w_q_b', 'mla_kv_a_norm_g', 'mla_w_kv_b', 'mla_q_norm_g',
           'mla_k_norm_g', 'dn_conv_w', 'dn_a_log', 'dn_dt_bias', 'dn_out_norm_g', 'dil_q_norm_g', 'dil_k_norm_g',
           'w_branch', 'w_out']
MATS = (('mla_w_q_b', 2), ('mla_w_kv_b', 2), ('w_branch', 3), ('w_out', 1))
SMALL = ('norm_g', 'mla_q_a_norm_g', 'mla_kv_a_norm_g', 'mla_q_norm_g', 'mla_k_norm_g', 'dn_a_log', 'dn_dt_bias',
         'dn_out_norm_g', 'dil_q_norm_g', 'dil_k_norm_g')


def _cparams(*sem):
    return pltpu.CompilerParams(dimension_semantics=sem or None, vmem_limit_bytes=VMEM_LIMIT)


def _tile(dim, target):
    best = 0
    for t in range(LANE, min(dim, target) + 1, LANE):
        if dim % t == 0:
            best = t
    assert best, (dim, target)
    return best


def _mm(a, b, mode, out_dtype, res=None):
    if mode == 'nn':
        (m, k), (k2, n) = a.shape, b.shape
    elif mode == 'nt':
        (m, k), (n, k2) = a.shape, b.shape
    else:
        (k, m), (k2, n) = a.shape, b.shape
    assert k == k2, (a.shape, b.shape, mode)
    tm, tn, tk = _tile(m, MM_TM), _tile(n, MM_TN), _tile(k, MM_TK)
    nk = k // tk
    dims = {'nn': (((1,), (0,)), ((), ())), 'nt': (((1,), (1,)), ((), ())), 'tn': (((0,), (0,)), ((), ()))}[mode]

    def body(*refs):
        if res is None:
            a_ref, b_ref, o_ref = refs[:3]
        else:
            a_ref, b_ref, r_ref, o_ref = refs[:4]
        part = lax.dot_general(a_ref[...].astype(BF16), b_ref[...].astype(BF16), dims, preferred_element_type=F32)

        def finish(r):
            if res is not None:
                r = r + r_ref[...].astype(F32)
            o_ref[...] = r.astype(o_ref.dtype)

        if nk == 1:
            finish(part)
            return
        acc = refs[-1]
        kk = pl.program_id(2)

        @pl.when(kk == 0)
        def _():
            acc[...] = part

        @pl.when(jnp.logical_and(kk > 0, kk < nk - 1))
        def _():
            acc[...] += part

        @pl.when(kk == nk - 1)
        def _():
            finish(acc[...] + part)

    a_spec = pl.BlockSpec((tk, tm), lambda i, j, kk: (kk, i)) if mode == 'tn' else pl.BlockSpec((tm, tk), lambda i, j, kk: (i, kk))
    b_spec = pl.BlockSpec((tn, tk), lambda i, j, kk: (j, kk)) if mode == 'nt' else pl.BlockSpec((tk, tn), lambda i, j, kk: (kk, j))
    o_spec = pl.BlockSpec((tm, tn), lambda i, j, kk: (i, j))
    in_specs = [a_spec, b_spec] + ([o_spec] if res is not None else [])
    args = (a, b) + ((res,) if res is not None else ())
    return pl.pallas_call(
        body, out_shape=jax.ShapeDtypeStruct((m, n), out_dtype), grid=(m // tm, n // tn, nk),
        in_specs=in_specs, out_specs=o_spec, scratch_shapes=[pltpu.VMEM((tm, tn), F32)] if nk > 1 else [],
        compiler_params=_cparams("parallel", "parallel", "arbitrary"),
        name=f"mm_{mode}_{m}x{k}x{n}" + ("_res" if res is not None else ""))(*args)


def _make_matmul(out_dtype, with_res):
    @jax.custom_vjp
    def mm(a, b, *r):
        return _mm(a, b, 'nn', out_dtype, *r)

    def fwd(a, b, *r):
        return mm(a, b, *r), (a, b)

    def bwd(saved, g):
        a, b = saved
        da = _mm(g, b, 'nt', a.dtype)
        db = _mm(a, g, 'tn', b.dtype)
        return (da, db) + ((g,) if with_res else ())

    mm.defvjp(fwd, bwd)
    return mm


@jax.custom_vjp
def matmul_shared(a, bs):
    return tuple(_mm(a, b, 'nn', F32) for b in bs)


def _matmul_shared_bwd(saved, gs):
    a, bs = saved
    da = None
    for i, (g, b) in enumerate(zip(gs, bs)):
        da = _mm(g, b, 'nt', a.dtype if i == len(bs) - 1 else F32, da)
    return da, tuple(_mm(a, g, 'tn', b.dtype) for g, b in zip(gs, bs))


matmul_shared.defvjp(lambda a, bs: (matmul_shared(a, bs), (a, bs)), _matmul_shared_bwd)


def matmul(a, b, out_dtype=F32, res=None):
    if res is None:
        return _make_matmul(out_dtype, False)(a, b)
    return _make_matmul(out_dtype, True)(a, b, res)


def _chunks(ref):
    return [ref[:, c * LANE:(c + 1) * LANE].astype(F32) for c in range(ref.shape[-1] // LANE)]


def _store(ref, chunks):
    for c, ch in enumerate(chunks):
        ref[:, c * LANE:(c + 1) * LANE] = ch.astype(ref.dtype)


def _row_spec(width, br, d=1):
    return pl.BlockSpec((br // d, d * width), lambda i: (i, 0))


def _par_spec(width):
    return pl.BlockSpec((1, width), lambda i: (0, 0))


def _load_rows(ref, scratch, d):
    if d == 1:
        return _chunks(ref)
    n, width = ref.shape[0], ref.shape[1] // d
    for c in range(width // LANE):
        for r in range(d):
            lanes = slice(r * width + c * LANE, r * width + (c + 1) * LANE)
            scratch[c, pl.ds(r, n, stride=d), :] = ref[:, lanes].astype(F32)
    return [scratch[c] for c in range(width // LANE)]


def _store_rows(ref, scratch, d, chunks):
    if d == 1:
        return _store(ref, chunks)
    n, width = ref.shape[0], ref.shape[1] // d
    for c, ch in enumerate(chunks):
        scratch[c] = ch
        for r in range(d):
            lanes = slice(r * width + c * LANE, r * width + (c + 1) * LANE)
            ref[:, lanes] = scratch[c, pl.ds(r, n, stride=d), :].astype(ref.dtype)


def _view_scratch(widths_dils, br):
    return [pltpu.VMEM((w // LANE, br, LANE), F32) for w, d in widths_dils if d > 1]


def _with_scratch(dils, scratch_refs):
    it = iter(scratch_refs)
    return [next(it) if d > 1 else None for d in dils]


def _rw_fwd(f, name, rows, params, outs, br, row_dil, out_dil):
    s = rows[0].shape[0] * row_dil[0]
    br = min(br, s)
    nr, npar, nout = len(rows), len(params), len(outs)
    row_w = [r.shape[1] // d for r, d in zip(rows, row_dil)]

    def body(*refs):
        scr = refs[nr + npar + nout:]
        n_in = sum(d > 1 for d in row_dil)
        rc = [_load_rows(r, sc, d) for r, sc, d in zip(refs[:nr], _with_scratch(row_dil, scr[:n_in]), row_dil)]
        pc = [_chunks(p) for p in refs[nr:nr + npar]]
        res = f(rc, pc)
        for o_ref, sc, d, chs in zip(refs[nr + npar:nr + npar + nout], _with_scratch(out_dil, scr[n_in:]), out_dil, res):
            _store_rows(o_ref, sc, d, chs)

    return pl.pallas_call(
        body, out_shape=[jax.ShapeDtypeStruct((s // d, d * w), dt) for (w, dt), d in zip(outs, out_dil)], grid=(s // br,),
        in_specs=[_row_spec(w, br, d) for w, d in zip(row_w, row_dil)] + [_par_spec(p.shape[1]) for p in params],
        out_specs=[_row_spec(w, br, d) for (w, _), d in zip(outs, out_dil)],
        scratch_shapes=_view_scratch(zip(row_w, row_dil), br) + _view_scratch([(w, d) for (w, _), d in zip(outs, out_dil)], br),
        compiler_params=_cparams("parallel"), name=name + "_fwd")(*rows, *params)


def _rw_bwd(f, name, rows, params, cts, nograd, br, row_dil, out_dil):
    s = rows[0].shape[0] * row_dil[0]
    br = min(br, s)
    nr, npar, nct = len(rows), len(params), len(cts)
    grad_rows = [i for i in range(nr) if i not in nograd]
    row_w = [r.shape[1] // d for r, d in zip(rows, row_dil)]
    ct_w = [c.shape[1] // d for c, d in zip(cts, out_dil)]
    grad_dil = [row_dil[i] for i in grad_rows]

    def body(*refs):
        n_out = len(grad_rows) + npar
        out_refs = refs[nr + npar + nct:nr + npar + nct + n_out]
        scr = refs[nr + npar + nct + n_out:]
        n_in, n_ct = sum(d > 1 for d in row_dil), sum(d > 1 for d in out_dil)
        rc = [_load_rows(r, sc, d) for r, sc, d in zip(refs[:nr], _with_scratch(row_dil, scr[:n_in]), row_dil)]
        pc = [_chunks(p) for p in refs[nr:nr + npar]]
        ct = [_load_rows(c, sc, d) for c, sc, d in
              zip(refs[nr + npar:nr + npar + nct], _with_scratch(out_dil, scr[n_in:n_in + n_ct]), out_dil)]
        _, vjp = jax.vjp(f, rc, pc)
        drc, dpc = vjp(ct)
        for o_ref, sc, d, i in zip(out_refs[:len(grad_rows)], _with_scratch(grad_dil, scr[n_in + n_ct:]), grad_dil, grad_rows):
            _store_rows(o_ref, sc, d, drc[i])
        i0 = pl.program_id(0)
        for o_ref, chs in zip(out_refs[len(grad_rows):], dpc):
            @pl.when(i0 == 0)
            def _(o_ref=o_ref):
                o_ref[...] = jnp.zeros_like(o_ref)
            for c, ch in enumerate(chs):
                o_ref[:, c * LANE:(c + 1) * LANE] += ch

    out_shape = ([jax.ShapeDtypeStruct(rows[i].shape, rows[i].dtype) for i in grad_rows]
                 + [jax.ShapeDtypeStruct(p.shape, F32) for p in params])
    out_specs = [_row_spec(row_w[i], br, row_dil[i]) for i in grad_rows] + [_par_spec(p.shape[1]) for p in params]
    res = pl.pallas_call(
        body, out_shape=out_shape, grid=(s // br,),
        in_specs=([_row_spec(w, br, d) for w, d in zip(row_w, row_dil)] + [_par_spec(p.shape[1]) for p in params]
                  + [_row_spec(w, br, d) for w, d in zip(ct_w, out_dil)]),
        out_specs=out_specs,
        scratch_shapes=(_view_scratch(zip(row_w, row_dil), br) + _view_scratch(zip(ct_w, out_dil), br)
                        + _view_scratch([(row_w[i], row_dil[i]) for i in grad_rows], br)),
        compiler_params=_cparams("arbitrary"), name=name + "_bwd")(*rows, *params, *cts)
    drows = [None] * nr
    for o, i in zip(res[:len(grad_rows)], grad_rows):
        drows[i] = o
    for i in nograd:
        drows[i] = jnp.zeros_like(rows[i])
    return tuple(drows), tuple(res[len(grad_rows):])


def rowwise(f, name, rows, params, outs, nograd=(), br=ROW_BLOCK, row_dil=None, out_dil=None):
    row_dil = tuple(row_dil or [1] * len(rows))
    out_dil = tuple(out_dil or [1] * len(outs))

    @jax.custom_vjp
    def op(rows, params):
        return tuple(_rw_fwd(f, name, rows, params, outs, br, row_dil, out_dil))

    def fwd(rows, params):
        return op(rows, params), (rows, params)

    def bwd(saved, cts):
        rows, params = saved
        return _rw_bwd(f, name, rows, params, list(cts), nograd, br, row_dil, out_dil)

    op.defvjp(fwd, bwd)
    return op(tuple(rows), tuple(params))


def _lane_roll(x, s):
    w = x.shape[-1]

    @jax.custom_vjp
    def r(v):
        return pltpu.roll(v, s, 1)

    r.defvjp(lambda v: (r(v), None), lambda _, g: (pltpu.roll(g, (w - s) % w, 1),))
    return r(x)


def _lanes(x):
    return lax.broadcasted_iota(jnp.int32, x.shape, 1)


def _rms(x, g, n=LANE):
    return x * lax.rsqrt(jnp.sum(x * x, axis=-1, keepdims=True) / n + RMS_EPS) * g


def _rope128(x, cos, sin):
    lane = _lanes(x)
    rot = _lane_roll(x, 64) * jnp.where(lane < 64, -1.0, 1.0)
    return x * cos + rot * sin


def _rope64(x, cos, sin):
    lane = _lanes(x)
    rot = jnp.where(lane < 32, -_lane_roll(x, 96), jnp.where(lane < 64, _lane_roll(x, 32), 0.0))
    return x * cos + rot * sin


def _silu(z):
    return z * jax.nn.sigmoid(z)


def _softplus(x):
    return jnp.maximum(x, 0.0) + jnp.log1p(jnp.exp(-jnp.abs(x)))


def f_rms_full(rc, pc):
    x, g = rc[0], pc[0]
    width = len(x) * LANE
    ms = sum(jnp.sum(c * c, axis=-1, keepdims=True) for c in x) / width
    r = lax.rsqrt(ms + RMS_EPS)
    return [[c * r * gc for c, gc in zip(x, g)]]


def f_mla1(rc, pc):
    x = rc[0]
    qn = f_rms_full([x[0:3]], [pc[0]])[0]
    kvn = f_rms_full([x[3:5]], [pc[1]])[0]
    return [qn, kvn, [x[5]]]


def f_mla2(rc, pc):
    q, kv, kpe, cos, sin = rc[0], rc[1], rc[2][0], rc[3][0], rc[4][0]
    gqn, gqp, gkn, gkp = pc[0][0], pc[1][0], pc[2][0], pc[3][0]
    k_pe = _rope64(_rms(kpe, gkp, MLA_ROPE), cos, sin)
    q_att, k_att, v = [], [], []
    for h in range(MLA_HEADS):
        q_att += [_rms(q[h], gqn), _rope64(_rms(q[MLA_HEADS + h], gqp, MLA_ROPE), cos, sin)]
        k_att += [_rms(kv[2 * h], gkn), k_pe]
        v.append(kv[2 * h + 1])
    return [q_att, k_att, v]


def f_gates(rc, pc):
    x, a_log, dt_bias = rc[0][0], pc[0][0], pc[1][0]
    lane = _lanes(x)
    g = -jnp.exp(a_log) * _softplus(x + dt_bias)
    return [[jnp.where(lane < DN_HEADS, g, jnp.where(lane < 2 * DN_HEADS, jax.nn.sigmoid(x), 0.0))]]


def f_headnorm(rc, pc):
    return [[_rms(c, pc[0][0]) for c in rc[0]]]


def f_dil(rc, pc):
    x, cos, sin = rc[0], rc[1][0], rc[2][0]
    gq, gk = pc[0][0], pc[1][0]
    n = len(x) // 3
    q = [_rope128(_rms(c, gq), cos, sin) for c in x[:n]]
    k = [_rope128(_rms(c, gk), cos, sin) for c in x[n:2 * n]]
    v = list(x[2 * n:])
    per = n // DIL_GROUPS
    return [t[g * per:(g + 1) * per] for g in range(DIL_GROUPS) for t in (q, k, v)]


def f_comb(rc, pc):
    o, l = rc[:DIL_GROUPS], rc[DIL_GROUPS:]
    out = []
    for c in range(len(o[0])):
        m = functools.reduce(jnp.maximum, [lg[c] for lg in l])
        e = [jnp.exp(lg[c] - m) for lg in l]
        den = sum(e)
        out.append(sum(eg * og[c] for eg, og in zip(e, o)) / den)
    return [out]


def f_merge1(rc, pc):
    z = rc[3]
    n = len(rc[0])
    return [[y * _silu(z[b * n + c]) for c, y in enumerate(rc[b])] for b in range(3)]


def f_merge2(rc, pc):
    gate = rc[0]
    n = len(rc[1])
    return [[sum(jax.nn.sigmoid(gate[b * n + c]) * rc[1 + b][c] for b in range(3)) for c in range(n)]]


MLA_DQK = 2 * LANE
MLA_SCALE = MLA_QK ** -0.5


def _mla_attn_fwd(q, k, v):
    s = q.shape[0]
    h = q.shape[1] // MLA_DQK
    t = min(512, s)
    n = s // t

    def body(q_ref, k_ref, v_ref, o_ref, lse_ref, m_sc, l_sc, acc_sc):
        qi, kj = pl.program_id(1), pl.program_id(2)

        @pl.when(kj == 0)
        def _():
            m_sc[...] = jnp.full_like(m_sc, -jnp.inf)
            l_sc[...] = jnp.zeros_like(l_sc)
            acc_sc[...] = jnp.zeros_like(acc_sc)

        nsub = 2 if t % 256 == 0 else 1
        ts = t // nsub
        rows = [slice(r * ts, (r + 1) * ts) for r in range(nsub)]

        def step(on_diagonal):
            sc = [lax.dot_general(q_ref[rw, :], k_ref[...], (((1,), (1,)), ((), ())), preferred_element_type=F32) * MLA_SCALE
                  for rw in rows]
            if on_diagonal:
                keep = [(lax.broadcasted_iota(jnp.int32, (ts, t), 1)
                         <= r * ts + lax.broadcasted_iota(jnp.int32, (ts, t), 0)) for r in range(nsub)]
                sc = [jnp.where(kp, x, -jnp.inf) for kp, x in zip(keep, sc)]
            m_old = [m_sc[rw, :] for rw in rows]
            m_new = [jnp.maximum(mo, jnp.max(x, axis=-1, keepdims=True)) for mo, x in zip(m_old, sc)]
            alpha = [jnp.exp(mo - mn) for mo, mn in zip(m_old, m_new)]
            p = [jnp.exp(x - mn) for x, mn in zip(sc, m_new)]
            pv = [jnp.dot(x.astype(BF16), v_ref[...], preferred_element_type=F32) for x in p]
            for r, rw in enumerate(rows):
                l_sc[rw, :] = alpha[r] * l_sc[rw, :] + jnp.sum(p[r], axis=-1, keepdims=True)
                acc_sc[rw, :] = alpha[r] * acc_sc[rw, :] + pv[r]
                m_sc[rw, :] = m_new[r]

        @pl.when(kj < qi)
        def _():
            step(False)

        @pl.when(kj == qi)
        def _():
            step(True)

        @pl.when(kj == n - 1)
        def _():
            o_ref[...] = acc_sc[...] / l_sc[...]
            lse_ref[...] = jnp.broadcast_to(m_sc[...] + jnp.log(l_sc[...]), lse_ref.shape)

    return pl.pallas_call(
        body, out_shape=[jax.ShapeDtypeStruct((s, h * HEAD), F32)] * 2, grid=(h, n, n),
        in_specs=[pl.BlockSpec((t, MLA_DQK), lambda hh, i, j: (i, hh)),
                  pl.BlockSpec((t, MLA_DQK), lambda hh, i, j: (jnp.minimum(j, i), hh)),
                  pl.BlockSpec((t, HEAD), lambda hh, i, j: (jnp.minimum(j, i), hh))],
        out_specs=[pl.BlockSpec((t, HEAD), lambda hh, i, j: (i, hh))] * 2,
        scratch_shapes=[pltpu.VMEM((t, 1), F32), pltpu.VMEM((t, 1), F32), pltpu.VMEM((t, HEAD), F32)],
        compiler_params=_cparams("parallel", "parallel", "arbitrary"), name="mla_attn_fwd")(q, k, v)


def _mla_attn_bwd(q, k, v, o, lse, do):
    s = q.shape[0]
    h = q.shape[1] // MLA_DQK
    t = min(512, s)
    n = s // t
    nt, tn = (((1,), (1,)), ((), ())), (((0,), (0,)), ((), ()))

    def body(q_ref, k_ref, v_ref, o_ref, lse_ref, do_ref, dq_ref, dk_ref, dv_ref, dq_sc, dk_sc, dv_sc):
        kj, qi = pl.program_id(1), pl.program_id(2)

        @pl.when(jnp.logical_and(kj == 0, qi == 0))
        def _():
            dq_sc[...] = jnp.zeros_like(dq_sc)

        @pl.when(qi == 0)
        def _():
            dk_sc[...] = jnp.zeros_like(dk_sc)
            dv_sc[...] = jnp.zeros_like(dv_sc)

        def pair(on_diagonal):
            sc = lax.dot_general(q_ref[...], k_ref[...], nt, preferred_element_type=F32) * MLA_SCALE
            p = jnp.exp(sc - lse_ref[:, 0:1])
            if on_diagonal:
                p = jnp.where(lax.broadcasted_iota(jnp.int32, (t, t), 1) <= lax.broadcasted_iota(jnp.int32, (t, t), 0), p, 0.0)
            do_v = do_ref[...]
            do_b = do_v.astype(BF16)
            dp = lax.dot_general(do_b, v_ref[...], nt, preferred_element_type=F32)
            delta = jnp.sum(do_v * o_ref[...], axis=-1, keepdims=True)
            ds = (p * (dp - delta) * MLA_SCALE).astype(BF16)
            dv_sc[...] += lax.dot_general(p.astype(BF16), do_b, tn, preferred_element_type=F32)
            dk_sc[...] += lax.dot_general(ds, q_ref[...], tn, preferred_element_type=F32)
            rows = pl.ds(pl.multiple_of(qi * t, t), t)
            dq_sc[rows, :] += jnp.dot(ds, k_ref[...], preferred_element_type=F32)

        @pl.when(qi > kj)
        def _():
            pair(False)

        @pl.when(qi == kj)
        def _():
            pair(True)

        @pl.when(qi == n - 1)
        def _():
            dk_ref[...] = dk_sc[...].astype(dk_ref.dtype)
            dv_ref[...] = dv_sc[...].astype(dv_ref.dtype)

        @pl.when(jnp.logical_and(kj == n - 1, qi == n - 1))
        def _():
            dq_ref[...] = dq_sc[...].astype(dq_ref.dtype)

    qmap = lambda hh, j, i: (jnp.maximum(i, j), hh)
    kmap = lambda hh, j, i: (j, hh)
    return pl.pallas_call(
        body, out_shape=[jax.ShapeDtypeStruct(q.shape, BF16), jax.ShapeDtypeStruct(k.shape, BF16), jax.ShapeDtypeStruct(v.shape, BF16)],
        grid=(h, n, n),
        in_specs=[pl.BlockSpec((t, MLA_DQK), qmap), pl.BlockSpec((t, MLA_DQK), kmap), pl.BlockSpec((t, HEAD), kmap),
                  pl.BlockSpec((t, HEAD), qmap), pl.BlockSpec((t, HEAD), qmap), pl.BlockSpec((t, HEAD), qmap)],
        out_specs=[pl.BlockSpec((s, MLA_DQK), lambda hh, j, i: (0, hh)), pl.BlockSpec((t, MLA_DQK), kmap),
                   pl.BlockSpec((t, HEAD), kmap)],
        scratch_shapes=[pltpu.VMEM((s, MLA_DQK), F32), pltpu.VMEM((t, MLA_DQK), F32), pltpu.VMEM((t, HEAD), F32)],
        compiler_params=_cparams("parallel", "arbitrary", "arbitrary"), name="mla_attn_bwd")(q, k, v, o, lse, do)


@jax.custom_vjp
def mla_attention(q, k, v):
    return _mla_attn_fwd(q, k, v)[0]


def _mla_attention_fwd(q, k, v):
    o, lse = _mla_attn_fwd(q, k, v)
    return o, (q, k, v, o, lse)


def _mla_attention_bwd(saved, do):
    return tuple(_mla_attn_bwd(*saved, do))


mla_attention.defvjp(_mla_attention_fwd, _mla_attention_bwd)


DIL_SCALE = HEAD ** -0.5
GROUP_W = 4 * HEAD


def _dil_scores(q, kp, kc, n):
    dn = (((1,), (1,)), ((), ()))
    sp = lax.dot_general(q, kp, dn, preferred_element_type=F32) * DIL_SCALE
    sc = lax.dot_general(q, kc, dn, preferred_element_type=F32) * DIL_SCALE
    qi = lax.broadcasted_iota(jnp.int32, sp.shape, 0)
    kc_i = lax.broadcasted_iota(jnp.int32, sp.shape, 1)
    vp = jnp.logical_and(kc_i >= qi, n > 0)
    vc = kc_i <= qi
    return sp, sc, vp, vc


def _dil_specs(d):
    cur = pl.BlockSpec((DIL_BLOCK, GROUP_W), lambda r, n: (n, r))
    prev = pl.BlockSpec((DIL_BLOCK, GROUP_W), lambda r, n: (jnp.maximum(n - 1, 0), r))
    return cur, prev


def _dil_fwd(q, k, v, d):
    l = q.shape[0]
    nb = l // DIL_BLOCK
    cur, prev = _dil_specs(d)

    def body(q_ref, kp_ref, kc_ref, vp_ref, vc_ref, o_ref, lse_ref):
        n = pl.program_id(1)
        heads = range(4)
        sl = [slice(h * HEAD, (h + 1) * HEAD) for h in heads]
        scores = [_dil_scores(q_ref[:, sl[h]], kp_ref[:, sl[h]], kc_ref[:, sl[h]], n) for h in heads]
        sp = [jnp.where(vp, s_p, -jnp.inf) for s_p, _, vp, _ in scores]
        sc = [jnp.where(vc, s_c, -jnp.inf) for _, s_c, _, vc in scores]
        m = [jnp.maximum(jnp.max(sp[h], axis=-1, keepdims=True), jnp.max(sc[h], axis=-1, keepdims=True)) for h in heads]
        ep = [jnp.exp(sp[h] - m[h]) for h in heads]
        ec = [jnp.exp(sc[h] - m[h]) for h in heads]
        den = [jnp.sum(ep[h], axis=-1, keepdims=True) + jnp.sum(ec[h], axis=-1, keepdims=True) for h in heads]
        acc = [jnp.dot(ep[h].astype(BF16), vp_ref[:, sl[h]], preferred_element_type=F32)
               + jnp.dot(ec[h].astype(BF16), vc_ref[:, sl[h]], preferred_element_type=F32) for h in heads]
        for h in heads:
            o_ref[:, sl[h]] = acc[h] / den[h]
            lse_ref[:, sl[h]] = jnp.broadcast_to(m[h] + jnp.log(den[h]), (DIL_BLOCK, HEAD))

    return pl.pallas_call(
        body, out_shape=[jax.ShapeDtypeStruct(q.shape, F32)] * 2, grid=(d, nb),
        in_specs=[cur, prev, cur, prev, cur], out_specs=[cur, cur],
        compiler_params=_cparams("parallel", "parallel"), name=f"dil_fwd_d{d}")(q, k, k, v, v)


def _dil_bwd(q, k, v, o, lse, do, dlse, d):
    l = q.shape[0]
    nb = l // DIL_BLOCK
    tn = (((0,), (0,)), ((), ()))
    nt = (((1,), (1,)), ((), ()))
    cur = pl.BlockSpec((DIL_BLOCK, GROUP_W), lambda r, n: (jnp.minimum(n, nb - 1), r))
    prev = pl.BlockSpec((DIL_BLOCK, GROUP_W), lambda r, n: (jnp.maximum(jnp.minimum(n, nb - 1) - 1, 0), r))
    lag = pl.BlockSpec((DIL_BLOCK, GROUP_W), lambda r, n: (jnp.maximum(n - 1, 0), r))

    def body(q_ref, kp_ref, kc_ref, vp_ref, vc_ref, o_ref, lse_ref, do_ref, dl_ref, dq_ref, dk_ref, dv_ref, ck_sc, cv_sc):
        n = pl.program_id(1)

        @pl.when(n < nb)
        def _():
            heads = range(4)
            sl = [slice(h * HEAD, (h + 1) * HEAD) for h in heads]
            scores = [_dil_scores(q_ref[:, sl[h]], kp_ref[:, sl[h]], kc_ref[:, sl[h]], n) for h in heads]
            lse = [lse_ref[:, h * HEAD:h * HEAD + 1] for h in heads]
            pp = [jnp.where(scores[h][2], jnp.exp(scores[h][0] - lse[h]), 0.0) for h in heads]
            pc = [jnp.where(scores[h][3], jnp.exp(scores[h][1] - lse[h]), 0.0) for h in heads]
            do_b = [do_ref[:, sl[h]].astype(BF16) for h in heads]
            corr = [jnp.sum(dl_ref[:, sl[h]], axis=-1, keepdims=True)
                    - jnp.sum(do_ref[:, sl[h]] * o_ref[:, sl[h]], axis=-1, keepdims=True) for h in heads]
            dsp = [(pp[h] * (lax.dot_general(do_b[h], vp_ref[:, sl[h]], nt, preferred_element_type=F32) + corr[h])
                    * DIL_SCALE).astype(BF16) for h in heads]
            dsc = [(pc[h] * (lax.dot_general(do_b[h], vc_ref[:, sl[h]], nt, preferred_element_type=F32) + corr[h])
                    * DIL_SCALE).astype(BF16) for h in heads]
            dkp = [lax.dot_general(dsp[h], q_ref[:, sl[h]], tn, preferred_element_type=F32) for h in heads]
            dvp = [lax.dot_general(pp[h].astype(BF16), do_b[h], tn, preferred_element_type=F32) for h in heads]
            for h in heads:
                dq_ref[:, sl[h]] = (jnp.dot(dsp[h], kp_ref[:, sl[h]], preferred_element_type=F32)
                                    + jnp.dot(dsc[h], kc_ref[:, sl[h]], preferred_element_type=F32)).astype(dq_ref.dtype)

            @pl.when(n > 0)
            def _():
                for h in heads:
                    dk_ref[:, sl[h]] = (ck_sc[:, sl[h]] + dkp[h]).astype(dk_ref.dtype)
                    dv_ref[:, sl[h]] = (cv_sc[:, sl[h]] + dvp[h]).astype(dv_ref.dtype)

            for h in heads:
                ck_sc[:, sl[h]] = lax.dot_general(dsc[h], q_ref[:, sl[h]], tn, preferred_element_type=F32)
                cv_sc[:, sl[h]] = lax.dot_general(pc[h].astype(BF16), do_b[h], tn, preferred_element_type=F32)

        @pl.when(n == nb)
        def _():
            dk_ref[...] = ck_sc[...].astype(dk_ref.dtype)
            dv_ref[...] = cv_sc[...].astype(dv_ref.dtype)

    return pl.pallas_call(
        body, out_shape=[jax.ShapeDtypeStruct(q.shape, BF16)] * 3, grid=(d, nb + 1),
        in_specs=[cur, prev, cur, prev, cur, cur, cur, cur, cur], out_specs=[cur, lag, lag],
        scratch_shapes=[pltpu.VMEM((DIL_BLOCK, GROUP_W), F32)] * 2,
        compiler_params=_cparams("parallel", "arbitrary"), name=f"dil_bwd_d{d}")(q, k, k, v, v, o, lse, do, dlse)


def _make_dil(d):
    @jax.custom_vjp
    def att(q, k, v):
        return tuple(_dil_fwd(q, k, v, d))

    def fwd(q, k, v):
        o, lse = _dil_fwd(q, k, v, d)
        return (o, lse), (q, k, v, o, lse)

    def bwd(saved, cts):
        return tuple(_dil_bwd(*saved, cts[0], cts[1], d))

    att.defvjp(fwd, bwd)
    return att


def dilated_group(q, k, v, d):
    return _make_dil(d)(q, k, v)


def _dn_post(c, kind, scale):
    m = _silu(c)
    nrm = m * lax.rsqrt(jnp.sum(m * m, axis=-1, keepdims=True) + 1e-6) * scale
    return kind * nrm + (1.0 - kind) * m


def _dn_kind_scale(j):
    kind = jnp.where(j < 2 * DN_HEADS, 1.0, 0.0).astype(F32)
    scale = jnp.where(j < DN_HEADS, DN_DIM ** -0.5, 1.0).astype(F32)
    return kind, scale


_CONV_RB = 512
_CONV_PAD = 8


def _conv_windows(pad_ref, w_ref, r0, rb, sign):
    acc = None
    for sh in range(DN_CONV):
        win = pad_ref[pl.ds(r0 + _CONV_PAD * (sign < 0) + sign * sh, rb), :]
        term = w_ref[DN_CONV - 1 - sh:DN_CONV - sh, :] * win
        acc = term if acc is None else acc + term
    return acc


def _dn_conv_fwd(x, w):
    s, width = x.shape
    rb = min(_CONV_RB, s)

    def body(x_ref, w_ref, o_ref, pad_ref):
        kind, scale = _dn_kind_scale(pl.program_id(0))
        pad_ref[0:_CONV_PAD, :] = jnp.zeros((_CONV_PAD, LANE), F32)
        pad_ref[_CONV_PAD:, :] = x_ref[...]
        for r0 in range(0, s, rb):
            c = _conv_windows(pad_ref, w_ref, r0, rb, -1)
            o_ref[r0:r0 + rb, :] = _dn_post(c, kind, scale)

    return pl.pallas_call(
        body, out_shape=jax.ShapeDtypeStruct(x.shape, F32), grid=(width // LANE,),
        in_specs=[pl.BlockSpec((s, LANE), lambda j: (0, j)), pl.BlockSpec((DN_CONV, LANE), lambda j: (0, j))],
        out_specs=pl.BlockSpec((s, LANE), lambda j: (0, j)), scratch_shapes=[pltpu.VMEM((s + _CONV_PAD, LANE), F32)],
        compiler_params=_cparams("parallel"), name="dn_conv_fwd")(x, w)


def _dn_conv_bwd(x, w, dy):
    s, width = x.shape
    rb = min(_CONV_RB, s)

    def body(x_ref, w_ref, dy_ref, dx_ref, dw_ref, pad_ref, dpad_ref):
        kind, scale = _dn_kind_scale(pl.program_id(0))
        pad_ref[0:_CONV_PAD, :] = jnp.zeros((_CONV_PAD, LANE), F32)
        pad_ref[_CONV_PAD:, :] = x_ref[...]
        dpad_ref[s:, :] = jnp.zeros((_CONV_PAD, LANE), F32)
        dws = [jnp.zeros((1, LANE), F32) for _ in range(DN_CONV)]
        for r0 in range(0, s, rb):
            c = _conv_windows(pad_ref, w_ref, r0, rb, -1)
            _, vjp = jax.vjp(lambda cc: _dn_post(cc, kind, scale), c)
            dc = vjp(dy_ref[r0:r0 + rb, :])[0]
            dpad_ref[r0:r0 + rb, :] = dc
            for sh in range(DN_CONV):
                win = pad_ref[pl.ds(r0 + _CONV_PAD - sh, rb), :]
                dws[DN_CONV - 1 - sh] = dws[DN_CONV - 1 - sh] + jnp.sum(dc * win, axis=0, keepdims=True)
        for j in range(DN_CONV):
            dw_ref[j:j + 1, :] = dws[j]
        for r0 in range(0, s, rb):
            dx_ref[r0:r0 + rb, :] = _conv_windows(dpad_ref, w_ref, r0, rb, 1)

    return pl.pallas_call(
        body, out_shape=[jax.ShapeDtypeStruct(x.shape, F32), jax.ShapeDtypeStruct(w.shape, F32)], grid=(width // LANE,),
        in_specs=[pl.BlockSpec((s, LANE), lambda j: (0, j)), pl.BlockSpec((DN_CONV, LANE), lambda j: (0, j)),
                  pl.BlockSpec((s, LANE), lambda j: (0, j))],
        out_specs=[pl.BlockSpec((s, LANE), lambda j: (0, j)), pl.BlockSpec((DN_CONV, LANE), lambda j: (0, j))],
        scratch_shapes=[pltpu.VMEM((s + _CONV_PAD, LANE), F32), pltpu.VMEM((s + _CONV_PAD, LANE), F32)],
        compiler_params=_cparams("parallel"), name="dn_conv_bwd")(x, w, dy)


@jax.custom_vjp
def dn_conv(x, w):
    return _dn_conv_fwd(x, w)


dn_conv.defvjp(lambda x, w: (_dn_conv_fwd(x, w), (x, w)), lambda saved, dy: tuple(_dn_conv_bwd(*saved, dy)))


_NN = (((1,), (0,)), ((), ()))
_NT = (((1,), (1,)), ((), ()))
_TN = (((0,), (0,)), ((), ()))


def _bd(a, b, dims, passes=1):
    d = lambda x, y: lax.dot_general(x, y, dims, preferred_element_type=F32)
    if passes == 0:
        return lax.dot_general(a, b, dims, precision=lax.Precision.HIGHEST, preferred_element_type=F32)
    ah, bh = a.astype(BF16), b.astype(BF16)
    if passes == 1:
        return d(ah, bh)
    al, bl = (a - ah.astype(F32)).astype(BF16), (b - bh.astype(F32)).astype(BF16)
    return d(ah, bh) + d(ah, bl) + d(al, bh)


@functools.partial(jax.custom_vjp, nondiff_argnums=(2, 3))
def _pdot(a, b, dims, passes):
    return _bd(a, b, dims, passes)


def _pdot_bwd(dims, passes, saved, g):
    a, b = saved
    if dims == _NN:
        return _bd(g, b, _NT, passes), _bd(a, g, _TN, passes)
    if dims == _NT:
        return _bd(g, b, _NN, passes), _bd(g, a, _TN, passes)
    return _bd(b, g, _NT, passes), _bd(a, g, _NN, passes)


_pdot.defvjp(lambda a, b, dims, passes: (_bd(a, b, dims, passes), (a, b)), _pdot_bwd)


GDN_DOT_PASSES = 1
GDN_SOLVE_PASSES = 3


def _hdot(a, b, dims=_NN):
    return _pdot(a, b, dims, GDN_DOT_PASSES)


def _xdot(a, b, dims=_NN):
    return _pdot(a, b, dims, GDN_SOLVE_PASSES)


def _split3(x):
    hi = x.astype(BF16)
    r1 = x - hi.astype(F32)
    mid = r1.astype(BF16)
    lo = (r1 - mid.astype(F32)).astype(BF16)
    return hi, mid, lo


def _tri_dot(tri, x, dims):
    t = tri.astype(BF16)
    return sum(lax.dot_general(t, p, dims, preferred_element_type=F32) for p in _split3(x))


@jax.custom_vjp
def _cumsum_rows(x):
    c = x.shape[0]
    tri = lax.broadcasted_iota(jnp.int32, (c, c), 0) >= lax.broadcasted_iota(jnp.int32, (c, c), 1)
    return _tri_dot(tri, x, _NN)


def _cumsum_rows_bwd(_, g):
    c = g.shape[0]
    tri = lax.broadcasted_iota(jnp.int32, (c, c), 0) >= lax.broadcasted_iota(jnp.int32, (c, c), 1)
    return (_tri_dot(tri, g, _TN),)


_cumsum_rows.defvjp(lambda x: (_cumsum_rows(x), None), _cumsum_rows_bwd)


@jax.custom_vjp
def _unit_lower_inverses(a):
    c = a[0].shape[0]
    eye = (lax.broadcasted_iota(jnp.int32, (c, c), 0) == lax.broadcasted_iota(jnp.int32, (c, c), 1)).astype(F32)
    pw = [-x for x in a]
    t = [eye + p for p in pw]
    for _ in range(int(math.log2(c)) - 1):
        pw = [_bd(p, p, _NN, GDN_SOLVE_PASSES) for p in pw]
        t = [x + _bd(x, p, _NN, GDN_SOLVE_PASSES) for x, p in zip(t, pw)]
    return t


def _unit_lower_inverses_bwd(t, g):
    left = [_bd(x, y, _TN, GDN_SOLVE_PASSES) for x, y in zip(t, g)]
    return ([-_bd(x, y, _NT, GDN_SOLVE_PASSES) for x, y in zip(left, t)],)


_unit_lower_inverses.defvjp(lambda a: (lambda t: (t, t))(_unit_lower_inverses(a)), _unit_lower_inverses_bwd)


@jax.custom_vjp
def _kept_inverses(a, t):
    return t


_kept_inverses.defvjp(lambda a, t: (t, t),
                      lambda t, g: (_unit_lower_inverses_bwd(t, g)[0], [jnp.zeros_like(x) for x in t]))


def _gdn_prep_fn(qkv, gb, kept=None, keep=None):
    c = GDN_CHUNK
    row = lax.broadcasted_iota(jnp.int32, (c, c), 0)
    col = lax.broadcasted_iota(jnp.int32, (c, c), 1)
    incl, strict = row >= col, row > col
    lane = _lanes(gb)
    heads = range(DN_HEADS)
    q, k, v = qkv[:DN_HEADS], qkv[DN_HEADS:2 * DN_HEADS], qkv[2 * DN_HEADS:]
    g = [jnp.sum(jnp.where(lane == h, gb, 0.0), axis=-1, keepdims=True) for h in heads]
    beta = [jnp.sum(jnp.where(lane == DN_HEADS + h, gb, 0.0), axis=-1, keepdims=True) for h in heads]
    gcb = [_cumsum_rows(jnp.broadcast_to(g[h], (c, c))) for h in heads]
    decay = [jnp.where(incl, jnp.exp(jnp.where(incl, gcb[h] - gcb[h].T, 0.0)), 0.0) for h in heads]
    kb = [k[h] * beta[h] for h in heads]
    a = [jnp.where(strict, _hdot(kb[h], k[h], _NT) * decay[h], 0.0) for h in heads]
    t = _unit_lower_inverses(a) if kept is None else _kept_inverses(a, kept)
    if keep is not None:
        keep.extend(t)
    eg = [jnp.exp(gcb[h]) for h in heads]
    u = [_xdot(t[h], v[h] * beta[h]) for h in heads]
    w = [_xdot(t[h], kb[h] * eg[h]) for h in heads]
    qk = [jnp.where(incl, _hdot(q[h], k[h], _NT) * decay[h], 0.0) for h in heads]
    g_last = [jnp.sum(jnp.where(row == c - 1, gcb[h], 0.0), axis=0, keepdims=True) for h in heads]
    kg = [k[h] * jnp.exp(g_last[h] - gcb[h]) for h in heads]
    qg = [q[h] * eg[h] for h in heads]
    el = [jnp.broadcast_to(jnp.exp(g_last[h]), (8, LANE)) for h in heads]
    return [u, w, qg, kg, qk, el]


def _gdn_prep_specs(nq):
    c = GDN_CHUNK
    big = pl.BlockSpec((c, nq * LANE), lambda n: (n, 0))
    return big


def _gdn_prep_fwd(qkv, gb):
    s = qkv.shape[0]
    c = GDN_CHUNK
    n = s // c
    hw = DN_HEADS * LANE

    def body(qkv_ref, gb_ref, u_ref, w_ref, qg_ref, kg_ref, qk_ref, el_ref, t_ref):
        kept = []
        res = _gdn_prep_fn(_chunks(qkv_ref), gb_ref[...], keep=kept)
        for ref, chs in zip((u_ref, w_ref, qg_ref, kg_ref, qk_ref, el_ref, t_ref), res + [kept]):
            _store(ref, chs)

    row = pl.BlockSpec((c, hw), lambda i: (i, 0))
    return pl.pallas_call(
        body, out_shape=[jax.ShapeDtypeStruct((s, hw), F32)] * 5 + [jax.ShapeDtypeStruct((n * 8, hw), F32),
                                                                   jax.ShapeDtypeStruct((s, hw), F32)], grid=(n,),
        in_specs=[pl.BlockSpec((c, 3 * hw), lambda i: (i, 0)), pl.BlockSpec((c, LANE), lambda i: (i, 0))],
        out_specs=[row] * 5 + [pl.BlockSpec((8, hw), lambda i: (i, 0)), row],
        compiler_params=_cparams("parallel"), name="gdn_prep_fwd")(qkv, gb)


def _gdn_prep_bwd(qkv, gb, t, cts):
    s = qkv.shape[0]
    c = GDN_CHUNK
    n = s // c
    hw = DN_HEADS * LANE

    def body(qkv_ref, gb_ref, t_ref, du_ref, dw_ref, dqg_ref, dkg_ref, dqk_ref, del_ref, dqkv_ref, dgb_ref):
        kept = _chunks(t_ref)
        _, vjp = jax.vjp(lambda x, y: _gdn_prep_fn(x, y, kept=kept), _chunks(qkv_ref), gb_ref[...])
        ct = [_chunks(r) for r in (du_ref, dw_ref, dqg_ref, dkg_ref, dqk_ref, del_ref)]
        dqkv, dgb = vjp(ct)
        _store(dqkv_ref, dqkv)
        dgb_ref[...] = dgb

    row = pl.BlockSpec((c, hw), lambda i: (i, 0))
    return pl.pallas_call(
        body, out_shape=[jax.ShapeDtypeStruct(qkv.shape, F32), jax.ShapeDtypeStruct(gb.shape, F32)], grid=(n,),
        in_specs=[pl.BlockSpec((c, 3 * hw), lambda i: (i, 0)), pl.BlockSpec((c, LANE), lambda i: (i, 0)), row]
        + [row] * 5 + [pl.BlockSpec((8, hw), lambda i: (i, 0))],
        out_specs=[pl.BlockSpec((c, 3 * hw), lambda i: (i, 0)), pl.BlockSpec((c, LANE), lambda i: (i, 0))],
        compiler_params=_cparams("parallel"), name="gdn_prep_bwd")(qkv, gb, t, *cts)


@jax.custom_vjp
def gdn_prep(qkv, gb):
    return tuple(_gdn_prep_fwd(qkv, gb)[:6])


def _gdn_prep_vfwd(qkv, gb):
    res = _gdn_prep_fwd(qkv, gb)
    return tuple(res[:6]), (qkv, gb, res[6])


gdn_prep.defvjp(_gdn_prep_vfwd, lambda saved, cts: tuple(_gdn_prep_bwd(*saved, cts)))


def _gdn_scan_fwd(u, w, qg, kg, qk, el):
    s = u.shape[0]
    c = GDN_CHUNK
    n = s // c
    hw = DN_HEADS * LANE

    def body(u_ref, w_ref, qg_ref, kg_ref, qk_ref, el_ref, o_ref, st_ref, s_sc):
        @pl.when(pl.program_id(0) == 0)
        def _():
            s_sc[...] = jnp.zeros_like(s_sc)

        heads = range(DN_HEADS)
        sl = [slice(h * LANE, (h + 1) * LANE) for h in heads]
        st = [s_sc[h] for h in heads]
        for h in heads:
            st_ref[sl[h], :] = st[h]
        v_new = [u_ref[:, sl[h]] - _hdot(w_ref[:, sl[h]], st[h]) for h in heads]
        o_st = [_hdot(qg_ref[:, sl[h]], st[h]) for h in heads]
        o_in = [_hdot(qk_ref[:, sl[h]], v_new[h]) for h in heads]
        s_up = [_hdot(kg_ref[:, sl[h]], v_new[h], _TN) for h in heads]
        for h in heads:
            o_ref[:, sl[h]] = o_st[h] + o_in[h]
            s_sc[h] = st[h] * el_ref[0:1, sl[h]] + s_up[h]

    row = pl.BlockSpec((c, hw), lambda i: (i, 0))
    return pl.pallas_call(
        body, out_shape=[jax.ShapeDtypeStruct((s, hw), F32), jax.ShapeDtypeStruct((n, hw, LANE), F32)], grid=(n,),
        in_specs=[row] * 5 + [pl.BlockSpec((8, hw), lambda i: (i, 0))],
        out_specs=[row, pl.BlockSpec((None, hw, LANE), lambda i: (i, 0, 0))],
        scratch_shapes=[pltpu.VMEM((DN_HEADS, LANE, LANE), F32)],
        compiler_params=_cparams("arbitrary"), name="gdn_scan_fwd")(u, w, qg, kg, qk, el)


def _gdn_scan_bwd(u, w, qg, kg, qk, el, states, do):
    s = u.shape[0]
    c = GDN_CHUNK
    n = s // c
    hw = DN_HEADS * LANE

    def body(u_ref, w_ref, qg_ref, kg_ref, qk_ref, el_ref, st_ref, do_ref,
             du_ref, dw_ref, dqg_ref, dkg_ref, dqk_ref, del_ref, ds_sc):
        @pl.when(pl.program_id(0) == 0)
        def _():
            ds_sc[...] = jnp.zeros_like(ds_sc)

        heads = range(DN_HEADS)
        sl = [slice(h * LANE, (h + 1) * LANE) for h in heads]
        st = [st_ref[sl[h], :] for h in heads]
        ds = [ds_sc[h] for h in heads]
        do = [do_ref[:, sl[h]] for h in heads]
        v_new = [u_ref[:, sl[h]] - _hdot(w_ref[:, sl[h]], st[h]) for h in heads]
        dv_new = [_hdot(qk_ref[:, sl[h]], do[h], _TN) + _hdot(kg_ref[:, sl[h]], ds[h]) for h in heads]
        first_row = lax.broadcasted_iota(jnp.int32, (8, LANE), 0) == 0
        for h in heads:
            du_ref[:, sl[h]] = dv_new[h]
            dw_ref[:, sl[h]] = -_hdot(dv_new[h], st[h], _NT)
            dqg_ref[:, sl[h]] = _hdot(do[h], st[h], _NT)
            dqk_ref[:, sl[h]] = _hdot(do[h], v_new[h], _NT)
            dkg_ref[:, sl[h]] = _hdot(v_new[h], ds[h], _NT)
            del_ref[:, sl[h]] = jnp.where(first_row, jnp.sum(st[h] * ds[h], axis=0, keepdims=True), 0.0)
        ds_new = [_hdot(qg_ref[:, sl[h]], do[h], _TN) + ds[h] * el_ref[0:1, sl[h]] - _hdot(w_ref[:, sl[h]], dv_new[h], _TN)
                  for h in heads]
        for h in heads:
            ds_sc[h] = ds_new[h]

    row = pl.BlockSpec((c, hw), lambda i: (n - 1 - i, 0))
    small = pl.BlockSpec((8, hw), lambda i: (n - 1 - i, 0))
    return pl.pallas_call(
        body, out_shape=[jax.ShapeDtypeStruct((s, hw), F32)] * 5 + [jax.ShapeDtypeStruct((n * 8, hw), F32)], grid=(n,),
        in_specs=[row] * 5 + [small, pl.BlockSpec((None, hw, LANE), lambda i: (n - 1 - i, 0, 0)), row],
        out_specs=[row] * 5 + [small], scratch_shapes=[pltpu.VMEM((DN_HEADS, LANE, LANE), F32)],
        compiler_params=_cparams("arbitrary"), name="gdn_scan_bwd")(u, w, qg, kg, qk, el, states, do)


@jax.custom_vjp
def gdn_scan(u, w, qg, kg, qk, el):
    return _gdn_scan_fwd(u, w, qg, kg, qk, el)[0]


def _gdn_scan_vfwd(*args):
    o, states = _gdn_scan_fwd(*args)
    return o, args + (states,)


gdn_scan.defvjp(_gdn_scan_vfwd, lambda saved, do: tuple(_gdn_scan_bwd(*saved, do)))


def _loss_call(y, t):
    s, d = y.shape
    br = min(ROW_BLOCK, s)
    n = s // br

    def body(y_ref, t_ref, loss_ref, dy_ref, acc):
        i = pl.program_id(0)

        @pl.when(i == 0)
        def _():
            acc[...] = jnp.zeros_like(acc)

        e = y_ref[...] - t_ref[...]
        dy_ref[...] = e / d
        acc[...] += jnp.sum(e * e, axis=0, keepdims=True)

        @pl.when(i == n - 1)
        def _():
            loss_ref[...] = jnp.broadcast_to(jnp.sum(acc[...], axis=1, keepdims=True) * (0.5 / d), loss_ref.shape)

    row = pl.BlockSpec((br, d), lambda i: (i, 0))
    return pl.pallas_call(
        body, out_shape=[jax.ShapeDtypeStruct((1, LANE), F32), jax.ShapeDtypeStruct((s, d), F32)], grid=(n,),
        in_specs=[row, row], out_specs=[pl.BlockSpec((1, LANE), lambda i: (0, 0)), row],
        scratch_shapes=[pltpu.VMEM((1, d), F32)], compiler_params=_cparams("arbitrary"), name="loss_head")(y, t)


@jax.custom_vjp
def loss_head(y, t):
    return _loss_call(y, t)[0][0, 0]


def _loss_head_fwd(y, t):
    loss, dy = _loss_call(y, t)
    return loss[0, 0], (dy,)


loss_head.defvjp(_loss_head_fwd, lambda saved, g: (saved[0] * g, -saved[0] * g))


IN_OFF = {}
_o = 0
for _name, _size in (('q_lat', 384), ('kv_lat', 320), ('z_a', 512), ('dn_qkv', 1536), ('dn_ab', 8), ('z_b', 512),
                     ('dil_qkv', 4608), ('z_c', 512), ('gate', 3072)):
    IN_OFF[_name] = (_o, _o + _size)
    _o += _size
IN_WIDTH = _o
N_CHIPS = 4


def _cols(w, name):
    a, b = IN_OFF[name]
    return w[:, a:b]


def _pad_cols(w, to):
    return jnp.concatenate([w, jnp.zeros((w.shape[0], to - w.shape[1]), w.dtype)], axis=1)


def _pad_row(v, to=None):
    v = v.reshape(1, -1)
    return v if to is None or v.shape[1] == to else _pad_cols(v, to)


def _shard_cols(pieces, a, b):
    wsh = pieces[0].shape[1]
    parts = [pieces[j][:, max(a, j * wsh) - j * wsh:min(b, (j + 1) * wsh) - j * wsh]
             for j in range(len(pieces)) if max(a, j * wsh) < min(b, (j + 1) * wsh)]
    return parts[0] if len(parts) == 1 else jnp.concatenate(parts, axis=1)


def _win_groups_impl(w_in4):
    out = []
    for l in range(w_in4.shape[1]):
        pieces = [w_in4[j, l] for j in range(w_in4.shape[0])]
        cols = lambda name: _shard_cols(pieces, *IN_OFF[name])
        out.append((_pad_cols(jnp.concatenate([cols('q_lat'), cols('kv_lat')], axis=1), 768),
                    jnp.concatenate([cols('z_a'), cols('z_b'), cols('z_c')], axis=1),
                    cols('dn_qkv'), _pad_cols(cols('dn_ab'), LANE), cols('dil_qkv'), cols('gate')))
    return tuple(out)


@jax.custom_vjp
def win_groups(w_in4):
    return _win_groups_impl(w_in4)


def _win_groups_bwd(_, cts):
    n_chip, depth = N_CHIPS, len(cts)
    wsh = IN_WIDTH // n_chip
    per_layer = []
    for l in range(depth):
        dmla, dz, dgdn, dab, ddil, dgate = cts[l]
        full = jnp.concatenate([dmla[:, :704], dz[:, :512], dgdn, dab[:, :8], dz[:, 512:1024], ddil, dz[:, 1024:], dgate], axis=1)
        per_layer.append([full[:, j * wsh:(j + 1) * wsh] for j in range(n_chip)])
    return (jnp.stack([jnp.stack([per_layer[l][j] for l in range(depth)]) for j in range(n_chip)]),)


win_groups.defvjp(lambda w: (_win_groups_impl(w), None), _win_groups_bwd)


def _layer(x, p, tabs):
    w_mla, w_z, w_gdn, w_ab, w_dil, w_gate = p['w_in_groups']
    h, = rowwise(f_rms_full, "rms_in", [x], [_pad_row(p['norm_g'])], [(D_MODEL, BF16)])
    mla_in, z, dn_qkv, dn_ab, dil_qkv, gate = matmul_shared(h, (w_mla, w_z, w_gdn, w_ab, w_dil, w_gate))

    qn, kvn, kpe = rowwise(f_mla1, "mla_norm", [mla_in], [_pad_row(p['mla_q_a_norm_g']), _pad_row(p['mla_kv_a_norm_g'])],
                           [(MLA_Q_RANK, BF16), (MLA_KV_RANK, BF16), (LANE, F32)])
    wq = p['mla_w_q_b']
    wq_pad = jnp.concatenate(
        [wq[:, hh * MLA_QK:hh * MLA_QK + MLA_NOPE] for hh in range(MLA_HEADS)]
        + [_pad_cols(wq[:, hh * MLA_QK + MLA_NOPE:(hh + 1) * MLA_QK], LANE) for hh in range(MLA_HEADS)], axis=1)
    q = matmul(qn, wq_pad)
    kv = matmul(kvn, p['mla_w_kv_b'])
    gq, gk = p['mla_q_norm_g'], p['mla_k_norm_g']
    q_att, k_att, v_att = rowwise(
        f_mla2, "mla_qk", [q, kv, kpe, tabs['cos_r'], tabs['sin_r']],
        [_pad_row(gq[:MLA_NOPE]), _pad_row(gq[MLA_NOPE:], LANE), _pad_row(gk[:MLA_NOPE]), _pad_row(gk[MLA_NOPE:], LANE)],
        [(MLA_HEADS * MLA_DQK, BF16), (MLA_HEADS * MLA_DQK, BF16), (MLA_HEADS * HEAD, BF16)], nograd=(3, 4))
    y_a = mla_attention(q_att, k_att, v_att)

    qkv_n = dn_conv(dn_qkv, p['dn_conv_w'])
    gb, = rowwise(f_gates, "dn_gates", [dn_ab], [_pad_row(p['dn_a_log'], LANE), _pad_row(p['dn_dt_bias'], LANE)], [(LANE, F32)])
    o_b = gdn_scan(*gdn_prep(qkv_n, gb))
    y_b, = rowwise(f_headnorm, "dn_out_norm", [o_b], [_pad_row(p['dn_out_norm_g'])], [(DN_HEADS * DN_DIM, F32)])

    qkv_d = rowwise(f_dil, "dil_qk", [dil_qkv, tabs['cos_h'], tabs['sin_h']],
                    [_pad_row(p['dil_q_norm_g']), _pad_row(p['dil_k_norm_g'])],
                    [(GROUP_W, BF16)] * (3 * DIL_GROUPS), nograd=(1, 2), out_dil=[d for d in DIL_DILATIONS for _ in range(3)])
    outs, lses = [], []
    for gi, d in enumerate(DIL_DILATIONS):
        o_g, l_g = dilated_group(*qkv_d[3 * gi:3 * gi + 3], d)
        outs.append(o_g)
        lses.append(l_g)
    y_c, = rowwise(f_comb, "dil_comb", outs + lses, [], [(GROUP_W, F32)], row_dil=list(DIL_DILATIONS) * 2)

    ys = rowwise(f_merge1, "merge_silu", [y_a, y_b, y_c, z], [], [(BRANCH_W, BF16)] * 3)
    bo = [matmul(ys[b], p['w_branch'][b]) for b in range(3)]
    mixed, = rowwise(f_merge2, "merge_gate", [gate] + bo, [], [(D_MODEL, BF16)])
    return matmul(mixed, p['w_out'], res=x)


def _rope_tables(pos, dim):
    inv_freq = 1.0 / (ROPE_THETA ** (jnp.arange(0, dim, 2, dtype=F32) / dim))
    ang = pos.astype(F32)[:, None] * inv_freq
    return jnp.cos(ang), jnp.sin(ang)


def _tables(pos):
    cr, sr = _rope_tables(pos, MLA_ROPE)
    ch, sh = _rope_tables(pos, HEAD)
    zero = jnp.zeros((pos.shape[0], LANE - MLA_ROPE), F32)
    return {'cos_r': jnp.concatenate([cr, cr, zero], axis=1), 'sin_r': jnp.concatenate([sr, sr, zero], axis=1),
            'cos_h': jnp.concatenate([ch, ch], axis=1), 'sin_h': jnp.concatenate([sh, sh], axis=1)}


def _local_loss(w_in4, mats, conv_w, small, x, target, tabs):
    groups = win_groups(w_in4)
    for l in range(DEPTH):
        p = {k: v[l] for k, v in mats.items()}
        p.update({k: v[l] for k, v in small.items()})
        p['dn_conv_w'] = conv_w[l]
        p['w_in_groups'] = groups[l]
        x = _layer(x, p, tabs)
    return loss_head(x, target)


def _pack(arrays, dtype, row_tile):
    flat = jnp.concatenate([a.astype(dtype).reshape(-1) for a in arrays])
    rows = -(-flat.shape[0] // (LANE * row_tile)) * row_tile
    flat = jnp.concatenate([flat, jnp.zeros((rows * LANE - flat.shape[0],), dtype)])
    return flat.reshape(rows, LANE)


def _unpack_impl(buf, shapes):
    flat = buf.reshape(-1)
    out, off = [], 0
    for shp in shapes:
        n = math.prod(shp)
        out.append(flat[off:off + n].reshape(shp))
        off += n
    return tuple(out)


@functools.partial(jax.custom_vjp, nondiff_argnums=(1, 2, 3))
def _unpack_p(buf, shapes, dtype_name, rows):
    return _unpack_impl(buf, shapes)


_unpack_p.defvjp(lambda buf, shapes, dtype_name, rows: (_unpack_impl(buf, shapes), None),
                 lambda shapes, dtype_name, rows, _, cts: (_pack(cts, jnp.dtype(dtype_name), rows),))


def _unpack(buf, shapes):
    return _unpack_p(buf, tuple(shapes), jnp.dtype(buf.dtype).name, buf.shape[0])


def _full_from_chips(buf4, shard_shapes):
    per_chip = [_unpack(buf4[j], tuple(shard_shapes)) for j in range(4)]
    return {name: jnp.concatenate([per_chip[j][i] for j in range(4)], axis=axis) for i, (name, axis) in enumerate(MATS)}


_HBM = pl.BlockSpec(memory_space=pltpu.HBM)
_VMEM = pl.BlockSpec(memory_space=pltpu.VMEM)


def _chip_peers(x, y):
    return [(1 - x, y), (x, 1 - y), (1 - x, 1 - y)]


def chip_all_to_all(arrays):
    n = len(arrays)

    def body(*refs):
        in_refs, out_refs = refs[:n], refs[n:2 * n]
        send_sems, recv_sems, local_sems = refs[2 * n:]
        x, y, c = lax.axis_index("x"), lax.axis_index("y"), lax.axis_index("c")
        me = 2 * x + y
        peers = _chip_peers(x, y)
        local, sends = [], []
        for a, (i_ref, o_ref) in enumerate(zip(in_refs, out_refs)):
            local.append(pltpu.make_async_copy(i_ref.at[me], o_ref.at[me], local_sems.at[a]))
            local[-1].start()
            for k, (px, py) in enumerate(peers):
                sends.append(pltpu.make_async_remote_copy(
                    src_ref=i_ref.at[2 * px + py], dst_ref=o_ref.at[me], send_sem=send_sems.at[3 * a + k],
                    recv_sem=recv_sems.at[3 * a + k], device_id=(px, py, c), device_id_type=MESH))
                sends[-1].start()
        for a, (i_ref, o_ref) in enumerate(zip(in_refs, out_refs)):
            for k, (px, py) in enumerate(peers):
                pltpu.make_async_remote_copy(
                    src_ref=i_ref.at[me], dst_ref=o_ref.at[2 * px + py], send_sem=send_sems.at[3 * a + k],
                    recv_sem=recv_sems.at[3 * a + k], device_id=(px, py, c), device_id_type=MESH).wait_recv()
        for cp in sends:
            cp.wait_send()
        for cp in local:
            cp.wait()

    return pl.pallas_call(
        body, out_shape=[jax.ShapeDtypeStruct(a.shape, a.dtype) for a in arrays], in_specs=[_HBM] * n, out_specs=[_HBM] * n,
        scratch_shapes=[pltpu.SemaphoreType.DMA((3 * n,)), pltpu.SemaphoreType.DMA((3 * n,)), pltpu.SemaphoreType.DMA((n,))],
        name="chip_all_to_all")(*arrays)


def chip_all_gather(shards):
    n = len(shards)

    def body(*refs):
        in_refs, out_refs = refs[:n], refs[n:2 * n]
        send_sems, recv_sems = refs[2 * n:]
        x, y, c = lax.axis_index("x"), lax.axis_index("y"), lax.axis_index("c")
        me = 2 * x + y
        peers = _chip_peers(x, y)
        sends = []
        for a, (i_ref, o_ref) in enumerate(zip(in_refs, out_refs)):
            for k, (px, py) in enumerate(peers):
                sends.append(pltpu.make_async_remote_copy(
                    src_ref=i_ref.at[c], dst_ref=o_ref.at[me, c], send_sem=send_sems.at[6 * a + k],
                    recv_sem=recv_sems.at[6 * a + k], device_id=(px, py, c), device_id_type=MESH))
                sends[-1].start()
        for a, (i_ref, o_ref) in enumerate(zip(in_refs, out_refs)):
            for k, (px, py) in enumerate(peers):
                landed = o_ref.at[2 * px + py, c]
                pltpu.make_async_remote_copy(src_ref=i_ref.at[c], dst_ref=landed, send_sem=send_sems.at[6 * a + k],
                                             recv_sem=recv_sems.at[6 * a + k], device_id=(px, py, c), device_id_type=MESH).wait_recv()
                sends.append(pltpu.make_async_remote_copy(
                    src_ref=landed, dst_ref=landed, send_sem=send_sems.at[6 * a + 3 + k], recv_sem=recv_sems.at[6 * a + 3 + k],
                    device_id=(x, y, 1 - c), device_id_type=MESH))
                sends[-1].start()
        for a, (i_ref, o_ref) in enumerate(zip(in_refs, out_refs)):
            for k, (px, py) in enumerate(peers):
                other = o_ref.at[2 * px + py, 1 - c]
                pltpu.make_async_remote_copy(src_ref=other, dst_ref=other, send_sem=send_sems.at[6 * a + 3 + k],
                                             recv_sem=recv_sems.at[6 * a + 3 + k], device_id=(x, y, 1 - c),
                                             device_id_type=MESH).wait_recv()
        for cp in sends:
            cp.wait_send()

    me = 2 * lax.axis_index("x") + lax.axis_index("y")
    outs = pl.pallas_call(
        body, out_shape=[jax.ShapeDtypeStruct((4,) + a.shape, a.dtype) for a in shards], in_specs=[_HBM] * n, out_specs=[_HBM] * n,
        scratch_shapes=[pltpu.SemaphoreType.DMA((6 * n,)), pltpu.SemaphoreType.DMA((6 * n,))],
        name="chip_all_gather")(*shards)
    return [lax.dynamic_update_index_in_dim(o, a, me, 0) for o, a in zip(outs, shards)]


def sibling_swap(arrays):
    n = len(arrays)

    def body(*refs):
        in_refs, out_refs = refs[:n], refs[n:2 * n]
        send_sems, recv_sems = refs[2 * n:]
        x, y, c = lax.axis_index("x"), lax.axis_index("y"), lax.axis_index("c")
        cps = [pltpu.make_async_remote_copy(src_ref=i_ref, dst_ref=o_ref, send_sem=send_sems.at[a], recv_sem=recv_sems.at[a],
                                            device_id=(x, y, 1 - c), device_id_type=MESH)
               for a, (i_ref, o_ref) in enumerate(zip(in_refs, out_refs))]
        for cp in cps:
            cp.start()
        for cp in cps:
            cp.wait()

    return pl.pallas_call(
        body, out_shape=[jax.ShapeDtypeStruct(a.shape, a.dtype) for a in arrays], in_specs=[_HBM] * n, out_specs=[_HBM] * n,
        scratch_shapes=[pltpu.SemaphoreType.DMA((n,)), pltpu.SemaphoreType.DMA((n,))], name="sibling_swap")(*arrays)


def sibling_all_gather(halves):
    n = len(halves)

    def body(*refs):
        in_refs, out_refs = refs[:n], refs[n:2 * n]
        send_sems, recv_sems = refs[2 * n:]
        x, y, c = lax.axis_index("x"), lax.axis_index("y"), lax.axis_index("c")
        sends = [pltpu.make_async_remote_copy(src_ref=i_ref, dst_ref=o_ref.at[c], send_sem=send_sems.at[a], recv_sem=recv_sems.at[a],
                                              device_id=(x, y, 1 - c), device_id_type=MESH)
                 for a, (i_ref, o_ref) in enumerate(zip(in_refs, out_refs))]
        for cp in sends:
            cp.start()
        for a, (i_ref, o_ref) in enumerate(zip(in_refs, out_refs)):
            pltpu.make_async_remote_copy(src_ref=i_ref, dst_ref=o_ref.at[1 - c], send_sem=send_sems.at[a], recv_sem=recv_sems.at[a],
                                         device_id=(x, y, 1 - c), device_id_type=MESH).wait_recv()
        for cp in sends:
            cp.wait_send()

    outs = pl.pallas_call(
        body, out_shape=[jax.ShapeDtypeStruct((2,) + a.shape, a.dtype) for a in halves], in_specs=[_HBM] * n, out_specs=[_HBM] * n,
        scratch_shapes=[pltpu.SemaphoreType.DMA((n,)), pltpu.SemaphoreType.DMA((n,))], name="sibling_all_gather")(*halves)
    return [lax.dynamic_update_index_in_dim(o, a, lax.axis_index("c"), 0) for o, a in zip(outs, halves)]


def all_gather8(v, name):
    def body(v_ref, out_ref, send_sems, recv_sems):
        x, y, c = lax.axis_index("x"), lax.axis_index("y"), lax.axis_index("c")
        out_ref[4 * x + 2 * y + c] = v_ref[...]

        def peer(k):
            return (x ^ (k >> 2), y ^ ((k >> 1) & 1), c ^ (k & 1))

        sends = [pltpu.make_async_remote_copy(src_ref=v_ref, dst_ref=out_ref.at[4 * x + 2 * y + c], send_sem=send_sems.at[k - 1],
                                              recv_sem=recv_sems.at[k - 1], device_id=peer(k), device_id_type=MESH)
                 for k in range(1, 8)]
        for cp in sends:
            cp.start()
        for k in range(1, 8):
            px, py, pc = peer(k)
            pltpu.make_async_remote_copy(src_ref=v_ref, dst_ref=out_ref.at[4 * px + 2 * py + pc], send_sem=send_sems.at[k - 1],
                                         recv_sem=recv_sems.at[k - 1], device_id=peer(k), device_id_type=MESH).wait_recv()
        for cp in sends:
            cp.wait_send()

    return pl.pallas_call(
        body, out_shape=jax.ShapeDtypeStruct((8,) + v.shape, v.dtype), in_specs=[_VMEM], out_specs=_VMEM,
        scratch_shapes=[pltpu.SemaphoreType.DMA((7,)), pltpu.SemaphoreType.DMA((7,))], name=name)(v)


def pair_add(a, b, row_tile, name):
    rows, width = a.shape

    def body(a_ref, b_ref, o_ref):
        o_ref[...] = (a_ref[...].astype(F32) + b_ref[...].astype(F32)).astype(o_ref.dtype)

    spec = pl.BlockSpec((row_tile, width), lambda i: (i, 0))
    return pl.pallas_call(body, out_shape=jax.ShapeDtypeStruct(a.shape, a.dtype), grid=(rows // row_tile,),
                          in_specs=[spec, spec], out_specs=spec, compiler_params=_cparams("parallel"), name=name)(a, b)


def sum_blocks(blocks, row_tile, name):
    n, rows, width = blocks.shape

    def body(b_ref, o_ref):
        acc = b_ref[0].astype(F32)
        for j in range(1, n):
            acc = acc + b_ref[j].astype(F32)
        o_ref[...] = acc

    return pl.pallas_call(
        body, out_shape=jax.ShapeDtypeStruct((rows, width), F32), grid=(rows // row_tile,),
        in_specs=[pl.BlockSpec((n, row_tile, width), lambda i: (0, i, 0))],
        out_specs=pl.BlockSpec((row_tile, width), lambda i: (i, 0)), compiler_params=_cparams("parallel"), name=name)(blocks)


def adamw(g_parts, w, m, v, row_tile, name):
    npart = len(g_parts)

    def body(*refs):
        g = refs[0][...]
        for r in refs[1:npart]:
            g = g + r[...]
        w_ref, m_ref, v_ref, g_out, d_out, m_out, v_out = refs[npart:]
        m_new = ADAM_B1 * m_ref[...] + (1.0 - ADAM_B1) * g
        v_new = ADAM_B2 * v_ref[...] + (1.0 - ADAM_B2) * (g * g)
        m_hat = m_new / (1.0 - ADAM_B1 ** ADAM_STEP)
        v_hat = v_new / (1.0 - ADAM_B2 ** ADAM_STEP)
        g_out[...] = g
        d_out[...] = -ADAM_LR * (m_hat / (jnp.sqrt(v_hat) + ADAM_EPS) + ADAM_WD * w_ref[...])
        m_out[...] = m_new
        v_out[...] = v_new

    rows, width = w.shape
    spec = pl.BlockSpec((row_tile, width), lambda i: (i, 0))
    return pl.pallas_call(
        body, out_shape=[jax.ShapeDtypeStruct(w.shape, F32)] * 4, grid=(rows // row_tile,),
        in_specs=[spec] * (npart + 3), out_specs=[spec] * 4, compiler_params=_cparams("parallel"), name=name)(*g_parts, w, m, v)


def kernel(x, positions, norm_g, w_in, mla_q_a_norm_g, mla_w_q_b, mla_kv_a_norm_g, mla_w_kv_b, mla_q_norm_g, mla_k_norm_g, dn_conv_w, dn_a_log, dn_dt_bias, dn_out_norm_g, dil_q_norm_g, dil_k_norm_g, w_branch, w_out, loss_target, m_norm_g, m_w_in, m_mla_q_a_norm_g, m_mla_w_q_b, m_mla_kv_a_norm_g, m_mla_w_kv_b, m_mla_q_norm_g, m_mla_k_norm_g, m_dn_conv_w, m_dn_a_log, m_dn_dt_bias, m_dn_out_norm_g, m_dil_q_norm_g, m_dil_k_norm_g, m_w_branch, m_w_out, v_norm_g, v_w_in, v_mla_q_a_norm_g, v_mla_w_q_b, v_mla_kv_a_norm_g, v_mla_w_kv_b, v_mla_q_norm_g, v_mla_k_norm_g, v_dn_conv_w, v_dn_a_log, v_dn_dt_bias, v_dn_out_norm_g, v_dil_q_norm_g, v_dil_k_norm_g, v_w_branch, v_w_out):
    w = dict(norm_g=norm_g, w_in=w_in, mla_q_a_norm_g=mla_q_a_norm_g, mla_w_q_b=mla_w_q_b, mla_kv_a_norm_g=mla_kv_a_norm_g,
             mla_w_kv_b=mla_w_kv_b, mla_q_norm_g=mla_q_norm_g, mla_k_norm_g=mla_k_norm_g, dn_conv_w=dn_conv_w, dn_a_log=dn_a_log,
             dn_dt_bias=dn_dt_bias, dn_out_norm_g=dn_out_norm_g, dil_q_norm_g=dil_q_norm_g, dil_k_norm_g=dil_k_norm_g,
             w_branch=w_branch, w_out=w_out)
    m = dict(norm_g=m_norm_g, w_in=m_w_in, mla_q_a_norm_g=m_mla_q_a_norm_g, mla_w_q_b=m_mla_w_q_b, mla_kv_a_norm_g=m_mla_kv_a_norm_g,
             mla_w_kv_b=m_mla_w_kv_b, mla_q_norm_g=m_mla_q_norm_g, mla_k_norm_g=m_mla_k_norm_g, dn_conv_w=m_dn_conv_w,
             dn_a_log=m_dn_a_log, dn_dt_bias=m_dn_dt_bias, dn_out_norm_g=m_dn_out_norm_g, dil_q_norm_g=m_dil_q_norm_g,
             dil_k_norm_g=m_dil_k_norm_g, w_branch=m_w_branch, w_out=m_w_out)
    v = dict(norm_g=v_norm_g, w_in=v_w_in, mla_q_a_norm_g=v_mla_q_a_norm_g, mla_w_q_b=v_mla_w_q_b, mla_kv_a_norm_g=v_mla_kv_a_norm_g,
             mla_w_kv_b=v_mla_w_kv_b, mla_q_norm_g=v_mla_q_norm_g, mla_k_norm_g=v_mla_k_norm_g, dn_conv_w=v_dn_conv_w,
             dn_a_log=v_dn_a_log, dn_dt_bias=v_dn_dt_bias, dn_out_norm_g=v_dn_out_norm_g, dil_q_norm_g=v_dil_q_norm_g,
             dil_k_norm_g=v_dil_k_norm_g, w_branch=v_w_branch, w_out=v_w_out)
    chip = 2 * lax.axis_index("x") + lax.axis_index("y")
    mat_names = [n for n, _ in MATS]
    mat_shapes = tuple(w[n].shape for n in mat_names)
    conv_shard = dn_conv_w.shape
    win_shape = w_in.shape
    win_rows = (win_shape[0] * win_shape[1], win_shape[2])

    mats_sh = _pack([w[n] for n in mat_names], BF16, 2 * MAT_ROWS)
    mat_rows = mats_sh.shape[0]
    w_in4, mats4 = chip_all_gather([w_in.astype(BF16), mats_sh.reshape(2, mat_rows // 2, LANE)])
    mats4 = mats4.reshape(N_CHIPS, mat_rows, LANE)
    conv8 = all_gather8(_pack([dn_conv_w], F32, 8), "gather_conv_w")
    conv_full = jnp.concatenate([_unpack(conv8[2 * j], (conv_shard,))[0] for j in range(4)], axis=2)
    small = {n: w[n] for n in SMALL}
    tabs = _tables(positions[0])

    def loss_fn(w_in4, mats4, conv_full, small, xs):
        return _local_loss(w_in4, _full_from_chips(mats4, mat_shapes), conv_full, small, xs, loss_target[0], tabs)

    loss, (g_win4, g_mats4, g_conv, g_small, g_x) = jax.value_and_grad(loss_fn, argnums=(0, 1, 2, 3, 4))(
        w_in4, mats4, conv_full, small, x[0])
    loss = lax.psum(loss, ("x", "y", "c"))

    c_idx = lax.axis_index("c")
    half_win = (N_CHIPS * win_shape[1], win_shape[2])
    half_mats = (N_CHIPS * (mat_rows // 2), LANE)
    g_mats4 = g_mats4.reshape(N_CHIPS, 2, mat_rows // 2, LANE)
    pick = lambda g, i, shape: lax.dynamic_index_in_dim(g, i, axis=1, keepdims=False).reshape(shape)
    from_sib = sibling_swap([pick(g_win4, 1 - c_idx, half_win), pick(g_mats4, 1 - c_idx, half_mats)])
    s_win = pair_add(pick(g_win4, c_idx, half_win), from_sib[0], WIN_ROWS, "pair_add_w_in")
    s_mats = pair_add(pick(g_mats4, c_idx, half_mats), from_sib[1], MAT_ROWS, "pair_add_mats")
    r_win, r_mats = chip_all_to_all([s_win.reshape(N_CHIPS, win_shape[1], win_shape[2]),
                                     s_mats.reshape(N_CHIPS, mat_rows // 2, LANE)])
    g_win, g_mats = sibling_all_gather([sum_blocks(r_win, WIN_ROWS, "sum_chip_pieces_w_in"),
                                        sum_blocks(r_mats, MAT_ROWS, "sum_chip_pieces_mats")])
    res_win = adamw([g_win.reshape(win_rows)], *[d['w_in'].reshape(win_rows) for d in (w, m, v)], WIN_ROWS, "adamw_w_in")
    packed = [_pack([d[n] for n in mat_names], F32, 2 * MAT_ROWS) for d in (w, m, v)]
    res_big = [dict(zip(mat_names, _unpack(r, mat_shapes)))
               for r in adamw([g_mats.reshape(mat_rows, LANE)], *packed, MAT_ROWS, "adamw_mats")]
    for k in range(4):
        res_big[k]['w_in'] = res_win[k].reshape(win_shape)

    small_shapes = tuple(w[n].shape for n in SMALL) + (g_conv.shape,)
    g_all = sum_blocks(all_gather8(_pack([g_small[n] for n in SMALL] + [g_conv], F32, 8), "gather_small_grads"), 8, "sum_small")
    g_list = list(_unpack(g_all, small_shapes))
    g_list[-1] = lax.dynamic_slice_in_dim(g_list[-1], chip * conv_shard[2], conv_shard[2], axis=2)
    small_names = list(SMALL) + ['dn_conv_w']
    packed_s = [_pack([d[n] for n in small_names], F32, 8) for d in (w, m, v)]
    shapes_s = tuple(w[n].shape for n in small_names)
    res_small = [dict(zip(small_names, _unpack(r, shapes_s))) for r in adamw([_pack(g_list, F32, 8)], *packed_s, 8, "adamw_small")]

    outs = [loss, g_x[None]]
    for k in range(4):
        outs += [res_big[k][n] if n in res_big[k] else res_small[k][n] for n in WEIGHTS]
    return tuple(outs)
```

```python
import functools
import math

import jax
import jax.numpy as jnp
from jax import lax
from jax.experimental import pallas as pl
from jax.experimental.pallas import tpu as pltpu

F32 = jnp.float32
BF16 = jnp.bfloat16
HI = lax.Precision.HIGHEST
MESH = pl.DeviceIdType.MESH

LANE = 128
VMEM_LIMIT = 48 * 1024 * 1024
ROW_BLOCK = 256
ROW_BLOCK_NARROW = 512
MM_TM, MM_TN, MM_TK = 1024, 1024, 1024
MAT_ROWS = 1664
WIN_ROWS = 128

RMS_EPS = 1e-6
ROPE_THETA = 10000.0
D_MODEL = 1024
DEPTH = 2
MLA_HEADS = 4
MLA_NOPE = 128
MLA_ROPE = 64
MLA_QK = MLA_NOPE + MLA_ROPE
MLA_Q_RANK = 384
MLA_KV_RANK = 256
DN_HEADS = 4
DN_DIM = 128
DN_CONV = 4
GDN_CHUNK = 128
DIL_WINDOWS = (128, 512, 2048)
DIL_DILATIONS = (1, 4, 16)
DIL_GROUPS = 3
DIL_BLOCK = 128
HEAD = 128
BRANCH_W = 512

ADAM_LR = 0.001
ADAM_B1 = 0.9
ADAM_B2 = 0.999
ADAM_EPS = 1e-08
ADAM_WD = 0.01
ADAM_STEP = 10

WEIGHTS = ['norm_g', 'w_in', 'mla_q_a_norm_g', 'mla_w_q_b', 'mla_kv_a_norm_g', 'mla_w_kv_b', 'mla_q_norm_g',
           'mla_k_norm_g', 'dn_conv_w', 'dn_a_log', 'dn_dt_bias', 'dn_out_norm_g', 'dil_q_norm_g', 'dil_k_norm_g',
           'w_branch', 'w_out']
MATS = (('mla_w_q_b', 2), ('mla_w_kv_b', 2), ('w_branch', 3), ('w_out', 1))
SMALL = ('norm_g', 'mla_q_a_norm_g', 'mla_kv_a_norm_g', 'mla_q_norm_g', 'mla_k_norm_g', 'dn_a_log', 'dn_dt_bias',
         'dn_out_norm_g', 'dil_q_norm_g', 'dil_k_norm_g')


def _cparams(*sem):
    return pltpu.CompilerParams(dimension_semantics=sem or None, vmem_limit_bytes=VMEM_LIMIT)


def _tile(dim, target):
    best = 0
    for t in range(LANE, min(dim, target) + 1, LANE):
        if dim % t == 0:
            best = t
    assert best, (dim, target)
    return best


def _mm(a, b, mode, out_dtype, res=None):
    if mode == 'nn':
        (m, k), (k2, n) = a.shape, b.shape
    elif mode == 'nt':
        (m, k), (n, k2) = a.shape, b.shape
    else:
        (k, m), (k2, n) = a.shape, b.shape
    assert k == k2, (a.shape, b.shape, mode)
    tm, tn, tk = _tile(m, MM_TM), _tile(n, MM_TN), _tile(k, MM_TK)
    nk = k // tk
    dims = {'nn': (((1,), (0,)), ((), ())), 'nt': (((1,), (1,)), ((), ())), 'tn': (((0,), (0,)), ((), ()))}[mode]

    def body(*refs):
        if res is None:
            a_ref, b_ref, o_ref = refs[:3]
        else:
            a_ref, b_ref, r_ref, o_ref = refs[:4]
        part = lax.dot_general(a_ref[...].astype(BF16), b_ref[...].astype(BF16), dims, preferred_element_type=F32)

        def finish(r):
            if res is not None:
                r = r + r_ref[...].astype(F32)
            o_ref[...] = r.astype(o_ref.dtype)

        if nk == 1:
            finish(part)
            return
        acc = refs[-1]
        kk = pl.program_id(2)

        @pl.when(kk == 0)
        def _():
            acc[...] = part

        @pl.when(jnp.logical_and(kk > 0, kk < nk - 1))
        def _():
            acc[...] += part

        @pl.when(kk == nk - 1)
        def _():
            finish(acc[...] + part)

    a_spec = pl.BlockSpec((tk, tm), lambda i, j, kk: (kk, i)) if mode == 'tn' else pl.BlockSpec((tm, tk), lambda i, j, kk: (i, kk))
    b_spec = pl.BlockSpec((tn, tk), lambda i, j, kk: (j, kk)) if mode == 'nt' else pl.BlockSpec((tk, tn), lambda i, j, kk: (kk, j))
    o_spec = pl.BlockSpec((tm, tn), lambda i, j, kk: (i, j))
    in_specs = [a_spec, b_spec] + ([o_spec] if res is not None else [])
    args = (a, b) + ((res,) if res is not None else ())
    return pl.pallas_call(
        body, out_shape=jax.ShapeDtypeStruct((m, n), out_dtype), grid=(m // tm, n // tn, nk),
        in_specs=in_specs, out_specs=o_spec, scratch_shapes=[pltpu.VMEM((tm, tn), F32)] if nk > 1 else [],
        compiler_params=_cparams("parallel", "parallel", "arbitrary"),
        name=f"mm_{mode}_{m}x{k}x{n}" + ("_res" if res is not None else ""))(*args)


def _make_matmul(out_dtype, with_res):
    @jax.custom_vjp
    def mm(a, b, *r):
        return _mm(a, b, 'nn', out_dtype, *r)

    def fwd(a, b, *r):
        return mm(a, b, *r), (a, b)

    def bwd(saved, g):
        a, b = saved
        da = _mm(g, b, 'nt', a.dtype)
        db = _mm(a, g, 'tn', b.dtype)
        return (da, db) + ((g,) if with_res else ())

    mm.defvjp(fwd, bwd)
    return mm


@jax.custom_vjp
def matmul_shared(a, bs):
    return tuple(_mm(a, b, 'nn', F32) for b in bs)


def _matmul_shared_bwd(saved, gs):
    a, bs = saved
    da = None
    for i, (g, b) in enumerate(zip(gs, bs)):
        da = _mm(g, b, 'nt', a.dtype if i == len(bs) - 1 else F32, da)
    return da, tuple(_mm(a, g, 'tn', b.dtype) for g, b in zip(gs, bs))


matmul_shared.defvjp(lambda a, bs: (matmul_shared(a, bs), (a, bs)), _matmul_shared_bwd)


def matmul(a, b, out_dtype=F32, res=None):
    if res is None:
        return _make_matmul(out_dtype, False)(a, b)
    return _make_matmul(out_dtype, True)(a, b, res)


def _chunks(ref):
    return [ref[:, c * LANE:(c + 1) * LANE].astype(F32) for c in range(ref.shape[-1] // LANE)]


def _store(ref, chunks):
    for c, ch in enumerate(chunks):
        ref[:, c * LANE:(c + 1) * LANE] = ch.astype(ref.dtype)


def _row_spec(width, br, d=1):
    return pl.BlockSpec((br // d, d * width), lambda i: (i, 0))


def _par_spec(width):
    return pl.BlockSpec((1, width), lambda i: (0, 0))


def _load_rows(ref, scratch, d):
    if d == 1:
        return _chunks(ref)
    n, width = ref.shape[0], ref.shape[1] // d
    for c in range(width // LANE):
        for r in range(d):
            lanes = slice(r * width + c * LANE, r * width + (c + 1) * LANE)
            scratch[c, pl.ds(r, n, stride=d), :] = ref[:, lanes].astype(F32)
    return [scratch[c] for c in range(width // LANE)]


def _store_rows(ref, scratch, d, chunks):
    if d == 1:
        return _store(ref, chunks)
    n, width = ref.shape[0], ref.shape[1] // d
    for c, ch in enumerate(chunks):
        scratch[c] = ch
        for r in range(d):
            lanes = slice(r * width + c * LANE, r * width + (c + 1) * LANE)
            ref[:, lanes] = scratch[c, pl.ds(r, n, stride=d), :].astype(ref.dtype)


def _view_scratch(widths_dils, br):
    return [pltpu.VMEM((w // LANE, br, LANE), F32) for w, d in widths_dils if d > 1]


def _with_scratch(dils, scratch_refs):
    it = iter(scratch_refs)
    return [next(it) if d > 1 else None for d in dils]


def _rw_fwd(f, name, rows, params, outs, br, row_dil, out_dil):
    s = rows[0].shape[0] * row_dil[0]
    br = min(br, s)
    nr, npar, nout = len(rows), len(params), len(outs)
    row_w = [r.shape[1] // d for r, d in zip(rows, row_dil)]

    def body(*refs):
        scr = refs[nr + npar + nout:]
        n_in = sum(d > 1 for d in row_dil)
        rc = [_load_rows(r, sc, d) for r, sc, d in zip(refs[:nr], _with_scratch(row_dil, scr[:n_in]), row_dil)]
        pc = [_chunks(p) for p in refs[nr:nr + npar]]
        res = f(rc, pc)
        for o_ref, sc, d, chs in zip(refs[nr + npar:nr + npar + nout], _with_scratch(out_dil, scr[n_in:]), out_dil, res):
            _store_rows(o_ref, sc, d, chs)

    return pl.pallas_call(
        body, out_shape=[jax.ShapeDtypeStruct((s // d, d * w), dt) for (w, dt), d in zip(outs, out_dil)], grid=(s // br,),
        in_specs=[_row_spec(w, br, d) for w, d in zip(row_w, row_dil)] + [_par_spec(p.shape[1]) for p in params],
        out_specs=[_row_spec(w, br, d) for (w, _), d in zip(outs, out_dil)],
        scratch_shapes=_view_scratch(zip(row_w, row_dil), br) + _view_scratch([(w, d) for (w, _), d in zip(outs, out_dil)], br),
        compiler_params=_cparams("parallel"), name=name + "_fwd")(*rows, *params)


def _rw_bwd(f, name, rows, params, cts, nograd, br, row_dil, out_dil):
    s = rows[0].shape[0] * row_dil[0]
    br = min(br, s)
    nr, npar, nct = len(rows), len(params), len(cts)
    grad_rows = [i for i in range(nr) if i not in nograd]
    row_w = [r.shape[1] // d for r, d in zip(rows, row_dil)]
    ct_w = [c.shape[1] // d for c, d in zip(cts, out_dil)]
    grad_dil = [row_dil[i] for i in grad_rows]

    def body(*refs):
        n_out = len(grad_rows) + npar
        out_refs = refs[nr + npar + nct:nr + npar + nct + n_out]
        scr = refs[nr + npar + nct + n_out:]
        n_in, n_ct = sum(d > 1 for d in row_dil), sum(d > 1 for d in out_dil)
        rc = [_load_rows(r, sc, d) for r, sc, d in zip(refs[:nr], _with_scratch(row_dil, scr[:n_in]), row_dil)]
        pc = [_chunks(p) for p in refs[nr:nr + npar]]
        ct = [_load_rows(c, sc, d) for c, sc, d in
              zip(refs[nr + npar:nr + npar + nct], _with_scratch(out_dil, scr[n_in:n_in + n_ct]), out_dil)]
        _, vjp = jax.vjp(f, rc, pc)
        drc, dpc = vjp(ct)
        for o_ref, sc, d, i in zip(out_refs[:len(grad_rows)], _with_scratch(grad_dil, scr[n_in + n_ct:]), grad_dil, grad_rows):
            _store_rows(o_ref, sc, d, drc[i])
        i0 = pl.program_id(0)
        for o_ref, chs in zip(out_refs[len(grad_rows):], dpc):
            @pl.when(i0 == 0)
            def _(o_ref=o_ref):
                o_ref[...] = jnp.zeros_like(o_ref)
            for c, ch in enumerate(chs):
                o_ref[:, c * LANE:(c + 1) * LANE] += ch

    out_shape = ([jax.ShapeDtypeStruct(rows[i].shape, rows[i].dtype) for i in grad_rows]
                 + [jax.ShapeDtypeStruct(p.shape, F32) for p in params])
    out_specs = [_row_spec(row_w[i], br, row_dil[i]) for i in grad_rows] + [_par_spec(p.shape[1]) for p in params]
    res = pl.pallas_call(
        body, out_shape=out_shape, grid=(s // br,),
        in_specs=([_row_spec(w, br, d) for w, d in zip(row_w, row_dil)] + [_par_spec(p.shape[1]) for p in params]
                  + [_row_spec(w, br, d) for w, d in zip(ct_w, out_dil)]),
        out_specs=out_specs,
        scratch_shapes=(_view_scratch(zip(row_w, row_dil), br) + _view_scratch(zip(ct_w, out_dil), br)
                        + _view_scratch([(row_w[i], row_dil[i]) for i in grad_rows], br)),
        compiler_params=_cparams("arbitrary"), name=name + "_bwd")(*rows, *params, *cts)
    drows = [None] * nr
    for o, i in zip(res[:len(grad_rows)], grad_rows):
        drows[i] = o
    for i in nograd:
        drows[i] = jnp.zeros_like(rows[i])
    return tuple(drows), tuple(res[len(grad_rows):])


def rowwise(f, name, rows, params, outs, nograd=(), br=ROW_BLOCK, row_dil=None, out_dil=None):
    row_dil = tuple(row_dil or [1] * len(rows))
    out_dil = tuple(out_dil or [1] * len(outs))

    @jax.custom_vjp
    def op(rows, params):
        return tuple(_rw_fwd(f, name, rows, params, outs, br, row_dil, out_dil))

    def fwd(rows, params):
        return op(rows, params), (rows, params)

    def bwd(saved, cts):
        rows, params = saved
        return _rw_bwd(f, name, rows, params, list(cts), nograd, br, row_dil, out_dil)

    op.defvjp(fwd, bwd)
    return op(tuple(rows), tuple(params))


def _lane_roll(x, s):
    w = x.shape[-1]

    @jax.custom_vjp
    def r(v):
        return pltpu.roll(v, s, 1)

    r.defvjp(lambda v: (r(v), None), lambda _, g: (pltpu.roll(g, (w - s) % w, 1),))
    return r(x)


def _lanes(x):
    return lax.broadcasted_iota(jnp.int32, x.shape, 1)


def _rms(x, g, n=LANE):
    return x * lax.rsqrt(jnp.sum(x * x, axis=-1, keepdims=True) / n + RMS_EPS) * g


def _rope128(x, cos, sin_signed):
    return x * cos + _lane_roll(x, 64) * sin_signed


def _rope64(x, cos, sin):
    lane = _lanes(x)
    rot = jnp.where(lane < 32, -_lane_roll(x, 96), jnp.where(lane < 64, _lane_roll(x, 32), 0.0))
    return x * cos + rot * sin


def _silu(z):
    return z * jax.nn.sigmoid(z)


def _softplus(x):
    return jnp.maximum(x, 0.0) + jnp.log1p(jnp.exp(-jnp.abs(x)))


def f_rms_full(rc, pc):
    x, g = rc[0], pc[0]
    width = len(x) * LANE
    ms = sum(jnp.sum(c * c, axis=-1, keepdims=True) for c in x) / width
    r = lax.rsqrt(ms + RMS_EPS)
    return [[c * r * gc for c, gc in zip(x, g)]]


def f_mla1(rc, pc):
    x = rc[0]
    qn = f_rms_full([x[0:3]], [pc[0]])[0]
    kvn = f_rms_full([x[3:5]], [pc[1]])[0]
    return [qn, kvn, [x[5]]]


def f_mla2(rc, pc):
    q, kv, kpe, cos, sin = rc[0], rc[1], rc[2][0], rc[3][0], rc[4][0]
    gqn, gqp, gkn, gkp = pc[0][0], pc[1][0], pc[2][0], pc[3][0]
    k_pe = _rope64(_rms(kpe, gkp, MLA_ROPE), cos, sin)
    q_att, k_att, v = [], [], []
    for h in range(MLA_HEADS):
        q_att += [_rms(q[h], gqn), _rope64(_rms(q[MLA_HEADS + h], gqp, MLA_ROPE), cos, sin)]
        k_att += [_rms(kv[2 * h], gkn), k_pe]
        v.append(kv[2 * h + 1])
    return [q_att, k_att, v]


def f_gates(rc, pc):
    x, a_log, dt_bias = rc[0][0], pc[0][0], pc[1][0]
    lane = _lanes(x)
    g = -jnp.exp(a_log) * _softplus(x + dt_bias)
    return [[jnp.where(lane < DN_HEADS, g, jnp.where(lane < 2 * DN_HEADS, jax.nn.sigmoid(x), 0.0))]]


def f_headnorm(rc, pc):
    return [[_rms(c, pc[0][0]) for c in rc[0]]]


def f_dil(rc, pc):
    x, cos, sin = rc[0], rc[1][0], rc[2][0]
    gq, gk = pc[0][0], pc[1][0]
    n = len(x) // 3
    q = [_rope128(_rms(c, gq), cos, sin) for c in x[:n]]
    k = [_rope128(_rms(c, gk), cos, sin) for c in x[n:2 * n]]
    v = list(x[2 * n:])
    per = n // DIL_GROUPS
    return [t[g * per:(g + 1) * per] for g in range(DIL_GROUPS) for t in (q, k, v)]


def f_comb(rc, pc):
    o, l = rc[:DIL_GROUPS], rc[DIL_GROUPS:]
    out = []
    for c in range(len(o[0])):
        m = functools.reduce(jnp.maximum, [lg[c] for lg in l])
        e = [jnp.exp(lg[c] - m) for lg in l]
        den = sum(e)
        out.append(sum(eg * og[c] for eg, og in zip(e, o)) / den)
    return [out]


def f_merge1(rc, pc):
    z = rc[3]
    n = len(rc[0])
    return [[y * _silu(z[b * n + c]) for c, y in enumerate(rc[b])] for b in range(3)]


def f_merge2(rc, pc):
    gate = rc[0]
    n = len(rc[1])
    return [[sum(jax.nn.sigmoid(gate[b * n + c]) * rc[1 + b][c] for b in range(3)) for c in range(n)]]


MLA_DQK = 2 * LANE
MLA_SCALE = MLA_QK ** -0.5


def _mla_attn_fwd(q, k, v):
    s = q.shape[0]
    h = q.shape[1] // MLA_DQK
    t = min(512, s)
    n = s // t

    def body(q_ref, k_ref, v_ref, o_ref, lse_ref, m_sc, l_sc, acc_sc):
        qi, kj = pl.program_id(1), pl.program_id(2)

        @pl.when(kj == 0)
        def _():
            m_sc[...] = jnp.full_like(m_sc, -jnp.inf)
            l_sc[...] = jnp.zeros_like(l_sc)
            acc_sc[...] = jnp.zeros_like(acc_sc)

        nsub = 2 if t % 256 == 0 else 1
        ts = t // nsub
        rows = [slice(r * ts, (r + 1) * ts) for r in range(nsub)]

        def step(on_diagonal):
            sc = [lax.dot_general(q_ref[rw, :], k_ref[...], (((1,), (1,)), ((), ())), preferred_element_type=F32) * MLA_SCALE
                  for rw in rows]
            if on_diagonal:
                keep = [(lax.broadcasted_iota(jnp.int32, (ts, t), 1)
                         <= r * ts + lax.broadcasted_iota(jnp.int32, (ts, t), 0)) for r in range(nsub)]
                sc = [jnp.where(kp, x, -jnp.inf) for kp, x in zip(keep, sc)]
            m_old = [m_sc[rw, :] for rw in rows]
            m_new = [jnp.maximum(mo, jnp.max(x, axis=-1, keepdims=True)) for mo, x in zip(m_old, sc)]
            alpha = [jnp.exp(mo - mn) for mo, mn in zip(m_old, m_new)]
            p = [jnp.exp(x - mn) for x, mn in zip(sc, m_new)]
            pv = [jnp.dot(x.astype(BF16), v_ref[...], preferred_element_type=F32) for x in p]
            for r, rw in enumerate(rows):
                l_sc[rw, :] = alpha[r] * l_sc[rw, :] + jnp.sum(p[r], axis=-1, keepdims=True)
                acc_sc[rw, :] = alpha[r] * acc_sc[rw, :] + pv[r]
                m_sc[rw, :] = m_new[r]

        @pl.when(kj < qi)
        def _():
            step(False)

        @pl.when(kj == qi)
        def _():
            step(True)

        @pl.when(kj == n - 1)
        def _():
            o_ref[...] = acc_sc[...] / l_sc[...]
            lse_ref[...] = jnp.broadcast_to(m_sc[...] + jnp.log(l_sc[...]), lse_ref.shape)

    return pl.pallas_call(
        body, out_shape=[jax.ShapeDtypeStruct((s, h * HEAD), F32)] * 2, grid=(h, n, n),
        in_specs=[pl.BlockSpec((t, MLA_DQK), lambda hh, i, j: (i, hh)),
                  pl.BlockSpec((t, MLA_DQK), lambda hh, i, j: (jnp.minimum(j, i), hh)),
                  pl.BlockSpec((t, HEAD), lambda hh, i, j: (jnp.minimum(j, i), hh))],
        out_specs=[pl.BlockSpec((t, HEAD), lambda hh, i, j: (i, hh))] * 2,
        scratch_shapes=[pltpu.VMEM((t, 1), F32), pltpu.VMEM((t, 1), F32), pltpu.VMEM((t, HEAD), F32)],
        compiler_params=_cparams("parallel", "parallel", "arbitrary"), name="mla_attn_fwd")(q, k, v)


def _mla_attn_bwd(q, k, v, o, lse, do):
    s = q.shape[0]
    h = q.shape[1] // MLA_DQK
    t = min(512, s)
    n = s // t
    nt, tn = (((1,), (1,)), ((), ())), (((0,), (0,)), ((), ()))

    def body(q_ref, k_ref, v_ref, o_ref, lse_ref, do_ref, dq_ref, dk_ref, dv_ref, dq_sc, dk_sc, dv_sc):
        kj, qi = pl.program_id(1), pl.program_id(2)

        @pl.when(jnp.logical_and(kj == 0, qi == 0))
        def _():
            dq_sc[...] = jnp.zeros_like(dq_sc)

        @pl.when(qi == 0)
        def _():
            dk_sc[...] = jnp.zeros_like(dk_sc)
            dv_sc[...] = jnp.zeros_like(dv_sc)

        def pair(on_diagonal):
            sc = lax.dot_general(q_ref[...], k_ref[...], nt, preferred_element_type=F32) * MLA_SCALE
            p = jnp.exp(sc - lse_ref[:, 0:1])
            if on_diagonal:
                p = jnp.where(lax.broadcasted_iota(jnp.int32, (t, t), 1) <= lax.broadcasted_iota(jnp.int32, (t, t), 0), p, 0.0)
            do_v = do_ref[...]
            do_b = do_v.astype(BF16)
            dp = lax.dot_general(do_b, v_ref[...], nt, preferred_element_type=F32)
            delta = jnp.sum(do_v * o_ref[...], axis=-1, keepdims=True)
            ds = (p * (dp - delta) * MLA_SCALE).astype(BF16)
            dv_sc[...] += lax.dot_general(p.astype(BF16), do_b, tn, preferred_element_type=F32)
            dk_sc[...] += lax.dot_general(ds, q_ref[...], tn, preferred_element_type=F32)
            rows = pl.ds(pl.multiple_of(qi * t, t), t)
            dq_sc[rows, :] += jnp.dot(ds, k_ref[...], preferred_element_type=F32)

        @pl.when(qi > kj)
        def _():
            pair(False)

        @pl.when(qi == kj)
        def _():
            pair(True)

        @pl.when(qi == n - 1)
        def _():
            dk_ref[...] = dk_sc[...].astype(dk_ref.dtype)
            dv_ref[...] = dv_sc[...].astype(dv_ref.dtype)

        @pl.when(jnp.logical_and(kj == n - 1, qi == n - 1))
        def _():
            dq_ref[...] = dq_sc[...].astype(dq_ref.dtype)

    qmap = lambda hh, j, i: (jnp.maximum(i, j), hh)
    kmap = lambda hh, j, i: (j, hh)
    return pl.pallas_call(
        body, out_shape=[jax.ShapeDtypeStruct(q.shape, BF16), jax.ShapeDtypeStruct(k.shape, BF16), jax.ShapeDtypeStruct(v.shape, BF16)],
        grid=(h, n, n),
        in_specs=[pl.BlockSpec((t, MLA_DQK), qmap), pl.BlockSpec((t, MLA_DQK), kmap), pl.BlockSpec((t, HEAD), kmap),
                  pl.BlockSpec((t, HEAD), qmap), pl.BlockSpec((t, HEAD), qmap), pl.BlockSpec((t, HEAD), qmap)],
        out_specs=[pl.BlockSpec((s, MLA_DQK), lambda hh, j, i: (0, hh)), pl.BlockSpec((t, MLA_DQK), kmap),
                   pl.BlockSpec((t, HEAD), kmap)],
        scratch_shapes=[pltpu.VMEM((s, MLA_DQK), F32), pltpu.VMEM((t, MLA_DQK), F32), pltpu.VMEM((t, HEAD), F32)],
        compiler_params=_cparams("parallel", "arbitrary", "arbitrary"), name="mla_attn_bwd")(q, k, v, o, lse, do)


@jax.custom_vjp
def mla_attention(q, k, v):
    return _mla_attn_fwd(q, k, v)[0]


def _mla_attention_fwd(q, k, v):
    o, lse = _mla_attn_fwd(q, k, v)
    return o, (q, k, v, o, lse)


def _mla_attention_bwd(saved, do):
    return tuple(_mla_attn_bwd(*saved, do))


mla_attention.defvjp(_mla_attention_fwd, _mla_attention_bwd)


DIL_SCALE = HEAD ** -0.5
GROUP_W = 4 * HEAD


def _dil_scores(q, kp, kc, n):
    dn = (((1,), (1,)), ((), ()))
    sp = lax.dot_general(q, kp, dn, preferred_element_type=F32) * DIL_SCALE
    sc = lax.dot_general(q, kc, dn, preferred_element_type=F32) * DIL_SCALE
    qi = lax.broadcasted_iota(jnp.int32, sp.shape, 0)
    kc_i = lax.broadcasted_iota(jnp.int32, sp.shape, 1)
    vp = jnp.logical_and(kc_i >= qi, n > 0)
    vc = kc_i <= qi
    return sp, sc, vp, vc


def _dil_specs(d):
    cur = pl.BlockSpec((DIL_BLOCK, GROUP_W), lambda r, n: (n, r))
    prev = pl.BlockSpec((DIL_BLOCK, GROUP_W), lambda r, n: (jnp.maximum(n - 1, 0), r))
    return cur, prev


def _dil_fwd(q, k, v, d):
    l = q.shape[0]
    nb = l // DIL_BLOCK
    cur, prev = _dil_specs(d)

    def body(q_ref, kp_ref, kc_ref, vp_ref, vc_ref, o_ref, lse_ref):
        n = pl.program_id(1)
        heads = range(4)
        sl = [slice(h * HEAD, (h + 1) * HEAD) for h in heads]
        scores = [_dil_scores(q_ref[:, sl[h]], kp_ref[:, sl[h]], kc_ref[:, sl[h]], n) for h in heads]
        sp = [jnp.where(vp, s_p, -jnp.inf) for s_p, _, vp, _ in scores]
        sc = [jnp.where(vc, s_c, -jnp.inf) for _, s_c, _, vc in scores]
        m = [jnp.maximum(jnp.max(sp[h], axis=-1, keepdims=True), jnp.max(sc[h], axis=-1, keepdims=True)) for h in heads]
        ep = [jnp.exp(sp[h] - m[h]) for h in heads]
        ec = [jnp.exp(sc[h] - m[h]) for h in heads]
        den = [jnp.sum(ep[h], axis=-1, keepdims=True) + jnp.sum(ec[h], axis=-1, keepdims=True) for h in heads]
        acc = [jnp.dot(ep[h].astype(BF16), vp_ref[:, sl[h]], preferred_element_type=F32)
               + jnp.dot(ec[h].astype(BF16), vc_ref[:, sl[h]], preferred_element_type=F32) for h in heads]
        for h in heads:
            o_ref[:, sl[h]] = acc[h] / den[h]
            lse_ref[:, sl[h]] = jnp.broadcast_to(m[h] + jnp.log(den[h]), (DIL_BLOCK, HEAD))

    return pl.pallas_call(
        body, out_shape=[jax.ShapeDtypeStruct(q.shape, F32)] * 2, grid=(d, nb),
        in_specs=[cur, prev, cur, prev, cur], out_specs=[cur, cur],
        compiler_params=_cparams("parallel", "parallel"), name=f"dil_fwd_d{d}")(q, k, k, v, v)


def _dil_bwd(q, k, v, o, lse, do, dlse, d):
    l = q.shape[0]
    nb = l // DIL_BLOCK
    tn = (((0,), (0,)), ((), ()))
    nt = (((1,), (1,)), ((), ()))
    cur = pl.BlockSpec((DIL_BLOCK, GROUP_W), lambda r, n: (jnp.minimum(n, nb - 1), r))
    prev = pl.BlockSpec((DIL_BLOCK, GROUP_W), lambda r, n: (jnp.maximum(jnp.minimum(n, nb - 1) - 1, 0), r))
    lag = pl.BlockSpec((DIL_BLOCK, GROUP_W), lambda r, n: (jnp.maximum(n - 1, 0), r))

    def body(q_ref, kp_ref, kc_ref, vp_ref, vc_ref, o_ref, lse_ref, do_ref, dl_ref, dq_ref, dk_ref, dv_ref, ck_sc, cv_sc):
        n = pl.program_id(1)

        @pl.when(n < nb)
        def _():
            heads = range(4)
            sl = [slice(h * HEAD, (h + 1) * HEAD) for h in heads]
            scores = [_dil_scores(q_ref[:, sl[h]], kp_ref[:, sl[h]], kc_ref[:, sl[h]], n) for h in heads]
            lse = [lse_ref[:, h * HEAD:h * HEAD + 1] for h in heads]
            pp = [jnp.where(scores[h][2], jnp.exp(scores[h][0] - lse[h]), 0.0) for h in heads]
            pc = [jnp.where(scores[h][3], jnp.exp(scores[h][1] - lse[h]), 0.0) for h in heads]
            do_b = [do_ref[:, sl[h]].astype(BF16) for h in heads]
            corr = [jnp.sum(dl_ref[:, sl[h]], axis=-1, keepdims=True)
                    - jnp.sum(do_ref[:, sl[h]] * o_ref[:, sl[h]], axis=-1, keepdims=True) for h in heads]
            dsp = [(pp[h] * (lax.dot_general(do_b[h], vp_ref[:, sl[h]], nt, preferred_element_type=F32) + corr[h])
                    * DIL_SCALE).astype(BF16) for h in heads]
            dsc = [(pc[h] * (lax.dot_general(do_b[h], vc_ref[:, sl[h]], nt, preferred_element_type=F32) + corr[h])
                    * DIL_SCALE).astype(BF16) for h in heads]
            dkp = [lax.dot_general(dsp[h], q_ref[:, sl[h]], tn, preferred_element_type=F32) for h in heads]
            dvp = [lax.dot_general(pp[h].astype(BF16), do_b[h], tn, preferred_element_type=F32) for h in heads]
            for h in heads:
                dq_ref[:, sl[h]] = (jnp.dot(dsp[h], kp_ref[:, sl[h]], preferred_element_type=F32)
                                    + jnp.dot(dsc[h], kc_ref[:, sl[h]], preferred_element_type=F32)).astype(dq_ref.dtype)

            @pl.when(n > 0)
            def _():
                for h in heads:
                    dk_ref[:, sl[h]] = (ck_sc[:, sl[h]] + dkp[h]).astype(dk_ref.dtype)
                    dv_ref[:, sl[h]] = (cv_sc[:, sl[h]] + dvp[h]).astype(dv_ref.dtype)

            for h in heads:
                ck_sc[:, sl[h]] = lax.dot_general(dsc[h], q_ref[:, sl[h]], tn, preferred_element_type=F32)
                cv_sc[:, sl[h]] = lax.dot_general(pc[h].astype(BF16), do_b[h], tn, preferred_element_type=F32)

        @pl.when(n == nb)
        def _():
            dk_ref[...] = ck_sc[...].astype(dk_ref.dtype)
            dv_ref[...] = cv_sc[...].astype(dv_ref.dtype)

    return pl.pallas_call(
        body, out_shape=[jax.ShapeDtypeStruct(q.shape, BF16)] * 3, grid=(d, nb + 1),
        in_specs=[cur, prev, cur, prev, cur, cur, cur, cur, cur], out_specs=[cur, lag, lag],
        scratch_shapes=[pltpu.VMEM((DIL_BLOCK, GROUP_W), F32)] * 2,
        compiler_params=_cparams("parallel", "arbitrary"), name=f"dil_bwd_d{d}")(q, k, k, v, v, o, lse, do, dlse)


def _make_dil(d):
    @jax.custom_vjp
    def att(q, k, v):
        return tuple(_dil_fwd(q, k, v, d))

    def fwd(q, k, v):
        o, lse = _dil_fwd(q, k, v, d)
        return (o, lse), (q, k, v, o, lse)

    def bwd(saved, cts):
        return tuple(_dil_bwd(*saved, cts[0], cts[1], d))

    att.defvjp(fwd, bwd)
    return att


def dilated_group(q, k, v, d):
    return _make_dil(d)(q, k, v)


def _dn_post(c, kind, scale):
    m = _silu(c)
    nrm = m * lax.rsqrt(jnp.sum(m * m, axis=-1, keepdims=True) + 1e-6) * scale
    return kind * nrm + (1.0 - kind) * m


def _dn_kind_scale(j):
    kind = jnp.where(j < 2 * DN_HEADS, 1.0, 0.0).astype(F32)
    scale = jnp.where(j < DN_HEADS, DN_DIM ** -0.5, 1.0).astype(F32)
    return kind, scale


_CONV_RB = 512
_CONV_PAD = 8


def _conv_windows(pad_ref, w_ref, r0, rb, sign):
    acc = None
    for sh in range(DN_CONV):
        win = pad_ref[pl.ds(r0 + _CONV_PAD * (sign < 0) + sign * sh, rb), :]
        term = w_ref[DN_CONV - 1 - sh:DN_CONV - sh, :] * win
        acc = term if acc is None else acc + term
    return acc


def _dn_conv_fwd(x, w):
    s, width = x.shape
    rb = min(_CONV_RB, s)

    def body(x_ref, w_ref, o_ref, pad_ref):
        kind, scale = _dn_kind_scale(pl.program_id(0))
        pad_ref[0:_CONV_PAD, :] = jnp.zeros((_CONV_PAD, LANE), F32)
        pad_ref[_CONV_PAD:, :] = x_ref[...]
        for r0 in range(0, s, rb):
            c = _conv_windows(pad_ref, w_ref, r0, rb, -1)
            o_ref[r0:r0 + rb, :] = _dn_post(c, kind, scale)

    return pl.pallas_call(
        body, out_shape=jax.ShapeDtypeStruct(x.shape, F32), grid=(width // LANE,),
        in_specs=[pl.BlockSpec((s, LANE), lambda j: (0, j)), pl.BlockSpec((DN_CONV, LANE), lambda j: (0, j))],
        out_specs=pl.BlockSpec((s, LANE), lambda j: (0, j)), scratch_shapes=[pltpu.VMEM((s + _CONV_PAD, LANE), F32)],
        compiler_params=_cparams("parallel"), name="dn_conv_fwd")(x, w)


def _dn_conv_bwd(x, w, dy):
    s, width = x.shape
    rb = min(_CONV_RB, s)

    def body(x_ref, w_ref, dy_ref, dx_ref, dw_ref, pad_ref, dpad_ref):
        kind, scale = _dn_kind_scale(pl.program_id(0))
        pad_ref[0:_CONV_PAD, :] = jnp.zeros((_CONV_PAD, LANE), F32)
        pad_ref[_CONV_PAD:, :] = x_ref[...]
        dpad_ref[s:, :] = jnp.zeros((_CONV_PAD, LANE), F32)
        dws = [jnp.zeros((1, LANE), F32) for _ in range(DN_CONV)]
        for r0 in range(0, s, rb):
            c = _conv_windows(pad_ref, w_ref, r0, rb, -1)
            _, vjp = jax.vjp(lambda cc: _dn_post(cc, kind, scale), c)
            dc = vjp(dy_ref[r0:r0 + rb, :])[0]
            dpad_ref[r0:r0 + rb, :] = dc
            for sh in range(DN_CONV):
                win = pad_ref[pl.ds(r0 + _CONV_PAD - sh, rb), :]
                dws[DN_CONV - 1 - sh] = dws[DN_CONV - 1 - sh] + jnp.sum(dc * win, axis=0, keepdims=True)
        for j in range(DN_CONV):
            dw_ref[j:j + 1, :] = dws[j]
        for r0 in range(0, s, rb):
            dx_ref[r0:r0 + rb, :] = _conv_windows(dpad_ref, w_ref, r0, rb, 1)

    return pl.pallas_call(
        body, out_shape=[jax.ShapeDtypeStruct(x.shape, F32), jax.ShapeDtypeStruct(w.shape, F32)], grid=(width // LANE,),
        in_specs=[pl.BlockSpec((s, LANE), lambda j: (0, j)), pl.BlockSpec((DN_CONV, LANE), lambda j: (0, j)),
                  pl.BlockSpec((s, LANE), lambda j: (0, j))],
        out_specs=[pl.BlockSpec((s, LANE), lambda j: (0, j)), pl.BlockSpec((DN_CONV, LANE), lambda j: (0, j))],
        scratch_shapes=[pltpu.VMEM((s + _CONV_PAD, LANE), F32), pltpu.VMEM((s + _CONV_PAD, LANE), F32)],
        compiler_params=_cparams("parallel"), name="dn_conv_bwd")(x, w, dy)


@jax.custom_vjp
def dn_conv(x, w):
    return _dn_conv_fwd(x, w)


dn_conv.defvjp(lambda x, w: (_dn_conv_fwd(x, w), (x, w)), lambda saved, dy: tuple(_dn_conv_bwd(*saved, dy)))


_NN = (((1,), (0,)), ((), ()))
_NT = (((1,), (1,)), ((), ()))
_TN = (((0,), (0,)), ((), ()))


def _bd(a, b, dims, passes=1):
    d = lambda x, y: lax.dot_general(x, y, dims, preferred_element_type=F32)
    if passes == 0:
        return lax.dot_general(a, b, dims, precision=lax.Precision.HIGHEST, preferred_element_type=F32)
    ah, bh = a.astype(BF16), b.astype(BF16)
    if passes == 1:
        return d(ah, bh)
    al, bl = (a - ah.astype(F32)).astype(BF16), (b - bh.astype(F32)).astype(BF16)
    return d(ah, bh) + d(ah, bl) + d(al, bh)


@functools.partial(jax.custom_vjp, nondiff_argnums=(2, 3))
def _pdot(a, b, dims, passes):
    return _bd(a, b, dims, passes)


def _pdot_bwd(dims, passes, saved, g):
    a, b = saved
    if dims == _NN:
        return _bd(g, b, _NT, passes), _bd(a, g, _TN, passes)
    if dims == _NT:
        return _bd(g, b, _NN, passes), _bd(g, a, _TN, passes)
    return _bd(b, g, _NT, passes), _bd(a, g, _NN, passes)


_pdot.defvjp(lambda a, b, dims, passes: (_bd(a, b, dims, passes), (a, b)), _pdot_bwd)


GDN_DOT_PASSES = 1
GDN_SOLVE_PASSES = 3


def _hdot(a, b, dims=_NN):
    return _pdot(a, b, dims, GDN_DOT_PASSES)


def _xdot(a, b, dims=_NN):
    return _pdot(a, b, dims, GDN_SOLVE_PASSES)


def _split3(x):
    hi = x.astype(BF16)
    r1 = x - hi.astype(F32)
    mid = r1.astype(BF16)
    lo = (r1 - mid.astype(F32)).astype(BF16)
    return hi, mid, lo


def _tri_dot(tri, x, dims):
    t = tri.astype(BF16)
    return sum(lax.dot_general(t, p, dims, preferred_element_type=F32) for p in _split3(x))


@jax.custom_vjp
def _cumsum_rows(x):
    c = x.shape[0]
    tri = lax.broadcasted_iota(jnp.int32, (c, c), 0) >= lax.broadcasted_iota(jnp.int32, (c, c), 1)
    return _tri_dot(tri, x, _NN)


def _cumsum_rows_bwd(_, g):
    c = g.shape[0]
    tri = lax.broadcasted_iota(jnp.int32, (c, c), 0) >= lax.broadcasted_iota(jnp.int32, (c, c), 1)
    return (_tri_dot(tri, g, _TN),)


_cumsum_rows.defvjp(lambda x: (_cumsum_rows(x), None), _cumsum_rows_bwd)


@jax.custom_vjp
def _unit_lower_inverses(a):
    c = a[0].shape[0]
    eye = (lax.broadcasted_iota(jnp.int32, (c, c), 0) == lax.broadcasted_iota(jnp.int32, (c, c), 1)).astype(F32)
    pw = [-x for x in a]
    t = [eye + p for p in pw]
    for _ in range(int(math.log2(c)) - 1):
        pw = [_bd(p, p, _NN, GDN_SOLVE_PASSES) for p in pw]
        t = [x + _bd(x, p, _NN, GDN_SOLVE_PASSES) for x, p in zip(t, pw)]
    return t


def _unit_lower_inverses_bwd(t, g):
    left = [_bd(x, y, _TN, GDN_SOLVE_PASSES) for x, y in zip(t, g)]
    return ([-_bd(x, y, _NT, GDN_SOLVE_PASSES) for x, y in zip(left, t)],)


_unit_lower_inverses.defvjp(lambda a: (lambda t: (t, t))(_unit_lower_inverses(a)), _unit_lower_inverses_bwd)


@jax.custom_vjp
def _kept_inverses(a, t):
    return t


_kept_inverses.defvjp(lambda a, t: (t, t),
                      lambda t, g: (_unit_lower_inverses_bwd(t, g)[0], [jnp.zeros_like(x) for x in t]))


def _gdn_prep_fn(qkv, gb, kept=None, keep=None):
    c = GDN_CHUNK
    row = lax.broadcasted_iota(jnp.int32, (c, c), 0)
    col = lax.broadcasted_iota(jnp.int32, (c, c), 1)
    incl, strict = row >= col, row > col
    lane = _lanes(gb)
    heads = range(DN_HEADS)
    q, k, v = qkv[:DN_HEADS], qkv[DN_HEADS:2 * DN_HEADS], qkv[2 * DN_HEADS:]
    g = [jnp.sum(jnp.where(lane == h, gb, 0.0), axis=-1, keepdims=True) for h in heads]
    beta = [jnp.sum(jnp.where(lane == DN_HEADS + h, gb, 0.0), axis=-1, keepdims=True) for h in heads]
    gcb = [_cumsum_rows(jnp.broadcast_to(g[h], (c, c))) for h in heads]
    decay = [jnp.where(incl, jnp.exp(jnp.where(incl, gcb[h] - gcb[h].T, 0.0)), 0.0) for h in heads]
    kb = [k[h] * beta[h] for h in heads]
    a = [jnp.where(strict, _hdot(kb[h], k[h], _NT) * decay[h], 0.0) for h in heads]
    t = _unit_lower_inverses(a) if kept is None else _kept_inverses(a, kept)
    if keep is not None:
        keep.extend(t)
    eg = [jnp.exp(gcb[h]) for h in heads]
    u = [_xdot(t[h], v[h] * beta[h]) for h in heads]
    w = [_xdot(t[h], kb[h] * eg[h]) for h in heads]
    qk = [jnp.where(incl, _hdot(q[h], k[h], _NT) * decay[h], 0.0) for h in heads]
    g_last = [jnp.sum(jnp.where(row == c - 1, gcb[h], 0.0), axis=0, keepdims=True) for h in heads]
    kg = [k[h] * jnp.exp(g_last[h] - gcb[h]) for h in heads]
    qg = [q[h] * eg[h] for h in heads]
    el = [jnp.broadcast_to(jnp.exp(g_last[h]), (8, LANE)) for h in heads]
    return [u, w, qg, kg, qk, el]


def _gdn_prep_specs(nq):
    c = GDN_CHUNK
    big = pl.BlockSpec((c, nq * LANE), lambda n: (n, 0))
    return big


def _gdn_prep_fwd(qkv, gb):
    s = qkv.shape[0]
    c = GDN_CHUNK
    n = s // c
    hw = DN_HEADS * LANE

    def body(qkv_ref, gb_ref, u_ref, w_ref, qg_ref, kg_ref, qk_ref, el_ref, t_ref):
        kept = []
        res = _gdn_prep_fn(_chunks(qkv_ref), gb_ref[...], keep=kept)
        for ref, chs in zip((u_ref, w_ref, qg_ref, kg_ref, qk_ref, el_ref, t_ref), res + [kept]):
            _store(ref, chs)

    row = pl.BlockSpec((c, hw), lambda i: (i, 0))
    return pl.pallas_call(
        body, out_shape=[jax.ShapeDtypeStruct((s, hw), F32)] * 5 + [jax.ShapeDtypeStruct((n * 8, hw), F32),
                                                                   jax.ShapeDtypeStruct((s, hw), F32)], grid=(n,),
        in_specs=[pl.BlockSpec((c, 3 * hw), lambda i: (i, 0)), pl.BlockSpec((c, LANE), lambda i: (i, 0))],
        out_specs=[row] * 5 + [pl.BlockSpec((8, hw), lambda i: (i, 0)), row],
        compiler_params=_cparams("parallel"), name="gdn_prep_fwd")(qkv, gb)


def _gdn_prep_bwd(qkv, gb, t, cts):
    s = qkv.shape[0]
    c = GDN_CHUNK
    n = s // c
    hw = DN_HEADS * LANE

    def body(qkv_ref, gb_ref, t_ref, du_ref, dw_ref, dqg_ref, dkg_ref, dqk_ref, del_ref, dqkv_ref, dgb_ref):
        kept = _chunks(t_ref)
        _, vjp = jax.vjp(lambda x, y: _gdn_prep_fn(x, y, kept=kept), _chunks(qkv_ref), gb_ref[...])
        ct = [_chunks(r) for r in (du_ref, dw_ref, dqg_ref, dkg_ref, dqk_ref, del_ref)]
        dqkv, dgb = vjp(ct)
        _store(dqkv_ref, dqkv)
        dgb_ref[...] = dgb

    row = pl.BlockSpec((c, hw), lambda i: (i, 0))
    return pl.pallas_call(
        body, out_shape=[jax.ShapeDtypeStruct(qkv.shape, F32), jax.ShapeDtypeStruct(gb.shape, F32)], grid=(n,),
        in_specs=[pl.BlockSpec((c, 3 * hw), lambda i: (i, 0)), pl.BlockSpec((c, LANE), lambda i: (i, 0)), row]
        + [row] * 5 + [pl.BlockSpec((8, hw), lambda i: (i, 0))],
        out_specs=[pl.BlockSpec((c, 3 * hw), lambda i: (i, 0)), pl.BlockSpec((c, LANE), lambda i: (i, 0))],
        compiler_params=_cparams("parallel"), name="gdn_prep_bwd")(qkv, gb, t, *cts)


@jax.custom_vjp
def gdn_prep(qkv, gb):
    return tuple(_gdn_prep_fwd(qkv, gb)[:6])


def _gdn_prep_vfwd(qkv, gb):
    res = _gdn_prep_fwd(qkv, gb)
    return tuple(res[:6]), (qkv, gb, res[6])


gdn_prep.defvjp(_gdn_prep_vfwd, lambda saved, cts: tuple(_gdn_prep_bwd(*saved, cts)))


def _gdn_scan_fwd(u, w, qg, kg, qk, el):
    s = u.shape[0]
    c = GDN_CHUNK
    n = s // c
    hw = DN_HEADS * LANE

    def body(u_ref, w_ref, qg_ref, kg_ref, qk_ref, el_ref, o_ref, st_ref, s_sc):
        @pl.when(pl.program_id(0) == 0)
        def _():
            s_sc[...] = jnp.zeros_like(s_sc)

        heads = range(DN_HEADS)
        sl = [slice(h * LANE, (h + 1) * LANE) for h in heads]
        st = [s_sc[h] for h in heads]
        for h in heads:
            st_ref[sl[h], :] = st[h]
        v_new = [u_ref[:, sl[h]] - _hdot(w_ref[:, sl[h]], st[h]) for h in heads]
        o_st = [_hdot(qg_ref[:, sl[h]], st[h]) for h in heads]
        o_in = [_hdot(qk_ref[:, sl[h]], v_new[h]) for h in heads]
        s_up = [_hdot(kg_ref[:, sl[h]], v_new[h], _TN) for h in heads]
        for h in heads:
            o_ref[:, sl[h]] = o_st[h] + o_in[h]
            s_sc[h] = st[h] * el_ref[0:1, sl[h]] + s_up[h]

    row = pl.BlockSpec((c, hw), lambda i: (i, 0))
    return pl.pallas_call(
        body, out_shape=[jax.ShapeDtypeStruct((s, hw), F32), jax.ShapeDtypeStruct((n, hw, LANE), F32)], grid=(n,),
        in_specs=[row] * 5 + [pl.BlockSpec((8, hw), lambda i: (i, 0))],
        out_specs=[row, pl.BlockSpec((None, hw, LANE), lambda i: (i, 0, 0))],
        scratch_shapes=[pltpu.VMEM((DN_HEADS, LANE, LANE), F32)],
        compiler_params=_cparams("arbitrary"), name="gdn_scan_fwd")(u, w, qg, kg, qk, el)


def _gdn_scan_bwd(u, w, qg, kg, qk, el, states, do):
    s = u.shape[0]
    c = GDN_CHUNK
    n = s // c
    hw = DN_HEADS * LANE

    def body(u_ref, w_ref, qg_ref, kg_ref, qk_ref, el_ref, st_ref, do_ref,
             du_ref, dw_ref, dqg_ref, dkg_ref, dqk_ref, del_ref, ds_sc):
        @pl.when(pl.program_id(0) == 0)
        def _():
            ds_sc[...] = jnp.zeros_like(ds_sc)

        heads = range(DN_HEADS)
        sl = [slice(h * LANE, (h + 1) * LANE) for h in heads]
        st = [st_ref[sl[h], :] for h in heads]
        ds = [ds_sc[h] for h in heads]
        do = [do_ref[:, sl[h]] for h in heads]
        v_new = [u_ref[:, sl[h]] - _hdot(w_ref[:, sl[h]], st[h]) for h in heads]
        dv_new = [_hdot(qk_ref[:, sl[h]], do[h], _TN) + _hdot(kg_ref[:, sl[h]], ds[h]) for h in heads]
        first_row = lax.broadcasted_iota(jnp.int32, (8, LANE), 0) == 0
        for h in heads:
            du_ref[:, sl[h]] = dv_new[h]
            dw_ref[:, sl[h]] = -_hdot(dv_new[h], st[h], _NT)
            dqg_ref[:, sl[h]] = _hdot(do[h], st[h], _NT)
            dqk_ref[:, sl[h]] = _hdot(do[h], v_new[h], _NT)
            dkg_ref[:, sl[h]] = _hdot(v_new[h], ds[h], _NT)
            del_ref[:, sl[h]] = jnp.where(first_row, jnp.sum(st[h] * ds[h], axis=0, keepdims=True), 0.0)
        ds_new = [_hdot(qg_ref[:, sl[h]], do[h], _TN) + ds[h] * el_ref[0:1, sl[h]] - _hdot(w_ref[:, sl[h]], dv_new[h], _TN)
                  for h in heads]
        for h in heads:
            ds_sc[h] = ds_new[h]

    row = pl.BlockSpec((c, hw), lambda i: (n - 1 - i, 0))
    small = pl.BlockSpec((8, hw), lambda i: (n - 1 - i, 0))
    return pl.pallas_call(
        body, out_shape=[jax.ShapeDtypeStruct((s, hw), F32)] * 5 + [jax.ShapeDtypeStruct((n * 8, hw), F32)], grid=(n,),
        in_specs=[row] * 5 + [small, pl.BlockSpec((None, hw, LANE), lambda i: (n - 1 - i, 0, 0)), row],
        out_specs=[row] * 5 + [small], scratch_shapes=[pltpu.VMEM((DN_HEADS, LANE, LANE), F32)],
        compiler_params=_cparams("arbitrary"), name="gdn_scan_bwd")(u, w, qg, kg, qk, el, states, do)


@jax.custom_vjp
def gdn_scan(u, w, qg, kg, qk, el):
    return _gdn_scan_fwd(u, w, qg, kg, qk, el)[0]


def _gdn_scan_vfwd(*args):
    o, states = _gdn_scan_fwd(*args)
    return o, args + (states,)


gdn_scan.defvjp(_gdn_scan_vfwd, lambda saved, do: tuple(_gdn_scan_bwd(*saved, do)))


def _loss_call(y, t):
    s, d = y.shape
    br = min(ROW_BLOCK, s)
    n = s // br

    def body(y_ref, t_ref, loss_ref, dy_ref, acc):
        i = pl.program_id(0)

        @pl.when(i == 0)
        def _():
            acc[...] = jnp.zeros_like(acc)

        e = y_ref[...] - t_ref[...]
        dy_ref[...] = e / d
        acc[...] += jnp.sum(e * e, axis=0, keepdims=True)

        @pl.when(i == n - 1)
        def _():
            loss_ref[...] = jnp.broadcast_to(jnp.sum(acc[...], axis=1, keepdims=True) * (0.5 / d), loss_ref.shape)

    row = pl.BlockSpec((br, d), lambda i: (i, 0))
    return pl.pallas_call(
        body, out_shape=[jax.ShapeDtypeStruct((1, LANE), F32), jax.ShapeDtypeStruct((s, d), F32)], grid=(n,),
        in_specs=[row, row], out_specs=[pl.BlockSpec((1, LANE), lambda i: (0, 0)), row],
        scratch_shapes=[pltpu.VMEM((1, d), F32)], compiler_params=_cparams("arbitrary"), name="loss_head")(y, t)


@jax.custom_vjp
def loss_head(y, t):
    return _loss_call(y, t)[0][0, 0]


def _loss_head_fwd(y, t):
    loss, dy = _loss_call(y, t)
    return loss[0, 0], (dy,)


loss_head.defvjp(_loss_head_fwd, lambda saved, g: (saved[0] * g, -saved[0] * g))


IN_OFF = {}
_o = 0
for _name, _size in (('q_lat', 384), ('kv_lat', 320), ('z_a', 512), ('dn_qkv', 1536), ('dn_ab', 8), ('z_b', 512),
                     ('dil_qkv', 4608), ('z_c', 512), ('gate', 3072)):
    IN_OFF[_name] = (_o, _o + _size)
    _o += _size
IN_WIDTH = _o
N_CHIPS = 4


def _cols(w, name):
    a, b = IN_OFF[name]
    return w[:, a:b]


def _pad_cols(w, to):
    return jnp.concatenate([w, jnp.zeros((w.shape[0], to - w.shape[1]), w.dtype)], axis=1)


def _pad_row(v, to=None):
    v = v.reshape(1, -1)
    return v if to is None or v.shape[1] == to else _pad_cols(v, to)


def _shard_cols(pieces, a, b):
    wsh = pieces[0].shape[1]
    parts = [pieces[j][:, max(a, j * wsh) - j * wsh:min(b, (j + 1) * wsh) - j * wsh]
             for j in range(len(pieces)) if max(a, j * wsh) < min(b, (j + 1) * wsh)]
    return parts[0] if len(parts) == 1 else jnp.concatenate(parts, axis=1)


def _win_groups_impl(w_in4):
    out = []
    for l in range(w_in4.shape[1]):
        pieces = [w_in4[j, l] for j in range(w_in4.shape[0])]
        cols = lambda name: _shard_cols(pieces, *IN_OFF[name])
        out.append((_pad_cols(jnp.concatenate([cols('q_lat'), cols('kv_lat')], axis=1), 768),
                    jnp.concatenate([cols('z_a'), cols('z_b'), cols('z_c')], axis=1),
                    cols('dn_qkv'), _pad_cols(cols('dn_ab'), LANE), cols('dil_qkv'), cols('gate')))
    return tuple(out)


@jax.custom_vjp
def win_groups(w_in4):
    return _win_groups_impl(w_in4)


def _win_groups_bwd(_, cts):
    n_chip, depth = N_CHIPS, len(cts)
    wsh = IN_WIDTH // n_chip
    per_layer = []
    for l in range(depth):
        dmla, dz, dgdn, dab, ddil, dgate = cts[l]
        full = jnp.concatenate([dmla[:, :704], dz[:, :512], dgdn, dab[:, :8], dz[:, 512:1024], ddil, dz[:, 1024:], dgate], axis=1)
        per_layer.append([full[:, j * wsh:(j + 1) * wsh] for j in range(n_chip)])
    return (jnp.stack([jnp.stack([per_layer[l][j] for l in range(depth)]) for j in range(n_chip)]),)


win_groups.defvjp(lambda w: (_win_groups_impl(w), None), _win_groups_bwd)


def _layer(x, p, tabs):
    w_mla, w_z, w_gdn, w_ab, w_dil, w_gate = p['w_in_groups']
    h, = rowwise(f_rms_full, "rms_in", [x], [_pad_row(p['norm_g'])], [(D_MODEL, BF16)], br=ROW_BLOCK_NARROW)
    mla_in, z, dn_qkv, dn_ab, dil_qkv, gate = matmul_shared(h, (w_mla, w_z, w_gdn, w_ab, w_dil, w_gate))

    qn, kvn, kpe = rowwise(f_mla1, "mla_norm", [mla_in], [_pad_row(p['mla_q_a_norm_g']), _pad_row(p['mla_kv_a_norm_g'])],
                           [(MLA_Q_RANK, BF16), (MLA_KV_RANK, BF16), (LANE, F32)], br=ROW_BLOCK_NARROW)
    wq = p['mla_w_q_b']
    wq_pad = jnp.concatenate(
        [wq[:, hh * MLA_QK:hh * MLA_QK + MLA_NOPE] for hh in range(MLA_HEADS)]
        + [_pad_cols(wq[:, hh * MLA_QK + MLA_NOPE:(hh + 1) * MLA_QK], LANE) for hh in range(MLA_HEADS)], axis=1)
    q = matmul(qn, wq_pad)
    kv = matmul(kvn, p['mla_w_kv_b'])
    gq, gk = p['mla_q_norm_g'], p['mla_k_norm_g']
    q_att, k_att, v_att = rowwise(
        f_mla2, "mla_qk", [q, kv, kpe, tabs['cos_r'], tabs['sin_r']],
        [_pad_row(gq[:MLA_NOPE]), _pad_row(gq[MLA_NOPE:], LANE), _pad_row(gk[:MLA_NOPE]), _pad_row(gk[MLA_NOPE:], LANE)],
        [(MLA_HEADS * MLA_DQK, BF16), (MLA_HEADS * MLA_DQK, BF16), (MLA_HEADS * HEAD, BF16)], nograd=(3, 4), br=ROW_BLOCK_NARROW)
    y_a = mla_attention(q_att, k_att, v_att)

    qkv_n = dn_conv(dn_qkv, p['dn_conv_w'])
    gb, = rowwise(f_gates, "dn_gates", [dn_ab], [_pad_row(p['dn_a_log'], LANE), _pad_row(p['dn_dt_bias'], LANE)], [(LANE, F32)],
                  br=ROW_BLOCK_NARROW)
    o_b = gdn_scan(*gdn_prep(qkv_n, gb))
    y_b, = rowwise(f_headnorm, "dn_out_norm", [o_b], [_pad_row(p['dn_out_norm_g'])], [(DN_HEADS * DN_DIM, F32)],
                   br=ROW_BLOCK_NARROW)

    qkv_d = rowwise(f_dil, "dil_qk", [dil_qkv, tabs['cos_h'], tabs['sin_h']],
                    [_pad_row(p['dil_q_norm_g']), _pad_row(p['dil_k_norm_g'])],
                    [(GROUP_W, BF16)] * (3 * DIL_GROUPS), nograd=(1, 2), out_dil=[d for d in DIL_DILATIONS for _ in range(3)])
    outs, lses = [], []
    for gi, d in enumerate(DIL_DILATIONS):
        o_g, l_g = dilated_group(*qkv_d[3 * gi:3 * gi + 3], d)
        outs.append(o_g)
        lses.append(l_g)
    y_c, = rowwise(f_comb, "dil_comb", outs + lses, [], [(GROUP_W, F32)], row_dil=list(DIL_DILATIONS) * 2)

    ys = rowwise(f_merge1, "merge_silu", [y_a, y_b, y_c, z], [], [(BRANCH_W, BF16)] * 3, br=ROW_BLOCK_NARROW)
    bo = [matmul(ys[b], p['w_branch'][b]) for b in range(3)]
    mixed, = rowwise(f_merge2, "merge_gate", [gate] + bo, [], [(D_MODEL, BF16)])
    return matmul(mixed, p['w_out'], res=x)


def _rope_tables(pos, dim):
    inv_freq = 1.0 / (ROPE_THETA ** (jnp.arange(0, dim, 2, dtype=F32) / dim))
    ang = pos.astype(F32)[:, None] * inv_freq
    return jnp.cos(ang), jnp.sin(ang)


def _tables(pos):
    cr, sr = _rope_tables(pos, MLA_ROPE)
    ch, sh = _rope_tables(pos, HEAD)
    zero = jnp.zeros((pos.shape[0], LANE - MLA_ROPE), F32)
    return {'cos_r': jnp.concatenate([cr, cr, zero], axis=1), 'sin_r': jnp.concatenate([sr, sr, zero], axis=1),
            'cos_h': jnp.concatenate([ch, ch], axis=1), 'sin_h': jnp.concatenate([-sh, sh], axis=1)}


def _local_loss(w_in4, mats, conv_w, small, x, target, tabs):
    groups = win_groups(w_in4)
    for l in range(DEPTH):
        p = {k: v[l] for k, v in mats.items()}
        p.update({k: v[l] for k, v in small.items()})
        p['dn_conv_w'] = conv_w[l]
        p['w_in_groups'] = groups[l]
        x = _layer(x, p, tabs)
    return loss_head(x, target)


def _pack(arrays, dtype, row_tile):
    flat = jnp.concatenate([a.astype(dtype).reshape(-1) for a in arrays])
    rows = -(-flat.shape[0] // (LANE * row_tile)) * row_tile
    flat = jnp.concatenate([flat, jnp.zeros((rows * LANE - flat.shape[0],), dtype)])
    return flat.reshape(rows, LANE)


def _unpack_impl(buf, shapes):
    flat = buf.reshape(-1)
    out, off = [], 0
    for shp in shapes:
        n = math.prod(shp)
        out.append(flat[off:off + n].reshape(shp))
        off += n
    return tuple(out)


@functools.partial(jax.custom_vjp, nondiff_argnums=(1, 2, 3))
def _unpack_p(buf, shapes, dtype_name, rows):
    return _unpack_impl(buf, shapes)


_unpack_p.defvjp(lambda buf, shapes, dtype_name, rows: (_unpack_impl(buf, shapes), None),
                 lambda shapes, dtype_name, rows, _, cts: (_pack(cts, jnp.dtype(dtype_name), rows),))


def _unpack(buf, shapes):
    return _unpack_p(buf, tuple(shapes), jnp.dtype(buf.dtype).name, buf.shape[0])


def _full_from_chips(buf4, shard_shapes):
    per_chip = [_unpack(buf4[j], tuple(shard_shapes)) for j in range(4)]
    return {name: jnp.concatenate([per_chip[j][i] for j in range(4)], axis=axis) for i, (name, axis) in enumerate(MATS)}


_HBM = pl.BlockSpec(memory_space=pltpu.HBM)
_VMEM = pl.BlockSpec(memory_space=pltpu.VMEM)


def _chip_peers(x, y):
    return [(1 - x, y), (x, 1 - y), (1 - x, 1 - y)]


def chip_all_to_all(arrays):
    n = len(arrays)

    def body(*refs):
        in_refs, out_refs = refs[:n], refs[n:2 * n]
        send_sems, recv_sems, local_sems = refs[2 * n:]
        x, y, c = lax.axis_index("x"), lax.axis_index("y"), lax.axis_index("c")
        me = 2 * x + y
        peers = _chip_peers(x, y)
        local, sends = [], []
        for a, (i_ref, o_ref) in enumerate(zip(in_refs, out_refs)):
            local.append(pltpu.make_async_copy(i_ref.at[me], o_ref.at[me], local_sems.at[a]))
            local[-1].start()
            for k, (px, py) in enumerate(peers):
                sends.append(pltpu.make_async_remote_copy(
                    src_ref=i_ref.at[2 * px + py], dst_ref=o_ref.at[me], send_sem=send_sems.at[3 * a + k],
                    recv_sem=recv_sems.at[3 * a + k], device_id=(px, py, c), device_id_type=MESH))
                sends[-1].start()
        for a, (i_ref, o_ref) in enumerate(zip(in_refs, out_refs)):
            for k, (px, py) in enumerate(peers):
                pltpu.make_async_remote_copy(
                    src_ref=i_ref.at[me], dst_ref=o_ref.at[2 * px + py], send_sem=send_sems.at[3 * a + k],
                    recv_sem=recv_sems.at[3 * a + k], device_id=(px, py, c), device_id_type=MESH).wait_recv()
        for cp in sends:
            cp.wait_send()
        for cp in local:
            cp.wait()

    return pl.pallas_call(
        body, out_shape=[jax.ShapeDtypeStruct(a.shape, a.dtype) for a in arrays], in_specs=[_HBM] * n, out_specs=[_HBM] * n,
        scratch_shapes=[pltpu.SemaphoreType.DMA((3 * n,)), pltpu.SemaphoreType.DMA((3 * n,)), pltpu.SemaphoreType.DMA((n,))],
        name="chip_all_to_all")(*arrays)


def chip_all_gather(shards):
    n = len(shards)

    def body(*refs):
        in_refs, out_refs = refs[:n], refs[n:2 * n]
        send_sems, recv_sems = refs[2 * n:]
        x, y, c = lax.axis_index("x"), lax.axis_index("y"), lax.axis_index("c")
        me = 2 * x + y
        peers = _chip_peers(x, y)
        sends = []
        for a, (i_ref, o_ref) in enumerate(zip(in_refs, out_refs)):
            for k, (px, py) in enumerate(peers):
                sends.append(pltpu.make_async_remote_copy(
                    src_ref=i_ref.at[c], dst_ref=o_ref.at[me, c], send_sem=send_sems.at[6 * a + k],
                    recv_sem=recv_sems.at[6 * a + k], device_id=(px, py, c), device_id_type=MESH))
                sends[-1].start()
        for a, (i_ref, o_ref) in enumerate(zip(in_refs, out_refs)):
            for k, (px, py) in enumerate(peers):
                landed = o_ref.at[2 * px + py, c]
                pltpu.make_async_remote_copy(src_ref=i_ref.at[c], dst_ref=landed, send_sem=send_sems.at[6 * a + k],
                                             recv_sem=recv_sems.at[6 * a + k], device_id=(px, py, c), device_id_type=MESH).wait_recv()
                sends.append(pltpu.make_async_remote_copy(
                    src_ref=landed, dst_ref=landed, send_sem=send_sems.at[6 * a + 3 + k], recv_sem=recv_sems.at[6 * a + 3 + k],
                    device_id=(x, y, 1 - c), device_id_type=MESH))
                sends[-1].start()
        for a, (i_ref, o_ref) in enumerate(zip(in_refs, out_refs)):
            for k, (px, py) in enumerate(peers):
                other = o_ref.at[2 * px + py, 1 - c]
                pltpu.make_async_remote_copy(src_ref=other, dst_ref=other, send_sem=send_sems.at[6 * a + 3 + k],
                                             recv_sem=recv_sems.at[6 * a + 3 + k], device_id=(x, y, 1 - c),
                                             device_id_type=MESH).wait_recv()
        for cp in sends:
            cp.wait_send()

    me = 2 * lax.axis_index("x") + lax.axis_index("y")
    outs = pl.pallas_call(
        body, out_shape=[jax.ShapeDtypeStruct((4,) + a.shape, a.dtype) for a in shards], in_specs=[_HBM] * n, out_specs=[_HBM] * n,
        scratch_shapes=[pltpu.SemaphoreType.DMA((6 * n,)), pltpu.SemaphoreType.DMA((6 * n,))],
        name="chip_all_gather")(*shards)
    return [lax.dynamic_update_index_in_dim(o, a, me, 0) for o, a in zip(outs, shards)]


def sibling_swap(arrays):
    n = len(arrays)

    def body(*refs):
        in_refs, out_refs = refs[:n], refs[n:2 * n]
        send_sems, recv_sems = refs[2 * n:]
        x, y, c = lax.axis_index("x"), lax.axis_index("y"), lax.axis_index("c")
        cps = [pltpu.make_async_remote_copy(src_ref=i_ref, dst_ref=o_ref, send_sem=send_sems.at[a], recv_sem=recv_sems.at[a],
                                            device_id=(x, y, 1 - c), device_id_type=MESH)
               for a, (i_ref, o_ref) in enumerate(zip(in_refs, out_refs))]
        for cp in cps:
            cp.start()
        for cp in cps:
            cp.wait()

    return pl.pallas_call(
        body, out_shape=[jax.ShapeDtypeStruct(a.shape, a.dtype) for a in arrays], in_specs=[_HBM] * n, out_specs=[_HBM] * n,
        scratch_shapes=[pltpu.SemaphoreType.DMA((n,)), pltpu.SemaphoreType.DMA((n,))], name="sibling_swap")(*arrays)


def sibling_all_gather(halves):
    n = len(halves)

    def body(*refs):
        in_refs, out_refs = refs[:n], refs[n:2 * n]
        send_sems, recv_sems = refs[2 * n:]
        x, y, c = lax.axis_index("x"), lax.axis_index("y"), lax.axis_index("c")
        sends = [pltpu.make_async_remote_copy(src_ref=i_ref, dst_ref=o_ref.at[c], send_sem=send_sems.at[a], recv_sem=recv_sems.at[a],
                                              device_id=(x, y, 1 - c), device_id_type=MESH)
                 for a, (i_ref, o_ref) in enumerate(zip(in_refs, out_refs))]
        for cp in sends:
            cp.start()
        for a, (i_ref, o_ref) in enumerate(zip(in_refs, out_refs)):
            pltpu.make_async_remote_copy(src_ref=i_ref, dst_ref=o_ref.at[1 - c], send_sem=send_sems.at[a], recv_sem=recv_sems.at[a],
                                         device_id=(x, y, 1 - c), device_id_type=MESH).wait_recv()
        for cp in sends:
            cp.wait_send()

    outs = pl.pallas_call(
        body, out_shape=[jax.ShapeDtypeStruct((2,) + a.shape, a.dtype) for a in halves], in_specs=[_HBM] * n, out_specs=[_HBM] * n,
        scratch_shapes=[pltpu.SemaphoreType.DMA((n,)), pltpu.SemaphoreType.DMA((n,))], name="sibling_all_gather")(*halves)
    return [lax.dynamic_update_index_in_dim(o, a, lax.axis_index("c"), 0) for o, a in zip(outs, halves)]


def all_gather8(v, name):
    def body(v_ref, out_ref, send_sems, recv_sems):
        x, y, c = lax.axis_index("x"), lax.axis_index("y"), lax.axis_index("c")
        out_ref[4 * x + 2 * y + c] = v_ref[...]

        def peer(k):
            return (x ^ (k >> 2), y ^ ((k >> 1) & 1), c ^ (k & 1))

        sends = [pltpu.make_async_remote_copy(src_ref=v_ref, dst_ref=out_ref.at[4 * x + 2 * y + c], send_sem=send_sems.at[k - 1],
                                              recv_sem=recv_sems.at[k - 1], device_id=peer(k), device_id_type=MESH)
                 for k in range(1, 8)]
        for cp in sends:
            cp.start()
        for k in range(1, 8):
            px, py, pc = peer(k)
            pltpu.make_async_remote_copy(src_ref=v_ref, dst_ref=out_ref.at[4 * px + 2 * py + pc], send_sem=send_sems.at[k - 1],
                                         recv_sem=recv_sems.at[k - 1], device_id=peer(k), device_id_type=MESH).wait_recv()
        for cp in sends:
            cp.wait_send()

    return pl.pallas_call(
        body, out_shape=jax.ShapeDtypeStruct((8,) + v.shape, v.dtype), in_specs=[_VMEM], out_specs=_VMEM,
        scratch_shapes=[pltpu.SemaphoreType.DMA((7,)), pltpu.SemaphoreType.DMA((7,))], name=name)(v)


def pair_add(a, b, row_tile, name):
    rows, width = a.shape

    def body(a_ref, b_ref, o_ref):
        o_ref[...] = (a_ref[...].astype(F32) + b_ref[...].astype(F32)).astype(o_ref.dtype)

    spec = pl.BlockSpec((row_tile, width), lambda i: (i, 0))
    return pl.pallas_call(body, out_shape=jax.ShapeDtypeStruct(a.shape, a.dtype), grid=(rows // row_tile,),
                          in_specs=[spec, spec], out_specs=spec, compiler_params=_cparams("parallel"), name=name)(a, b)


def sum_blocks(blocks, row_tile, name):
    n, rows, width = blocks.shape

    def body(b_ref, o_ref):
        acc = b_ref[0].astype(F32)
        for j in range(1, n):
            acc = acc + b_ref[j].astype(F32)
        o_ref[...] = acc

    return pl.pallas_call(
        body, out_shape=jax.ShapeDtypeStruct((rows, width), F32), grid=(rows // row_tile,),
        in_specs=[pl.BlockSpec((n, row_tile, width), lambda i: (0, i, 0))],
        out_specs=pl.BlockSpec((row_tile, width), lambda i: (i, 0)), compiler_params=_cparams("parallel"), name=name)(blocks)


def adamw(g_parts, w, m, v, row_tile, name):
    npart = len(g_parts)

    def body(*refs):
        g = refs[0][...]
        for r in refs[1:npart]:
            g = g + r[...]
        w_ref, m_ref, v_ref, g_out, d_out, m_out, v_out = refs[npart:]
        m_new = ADAM_B1 * m_ref[...] + (1.0 - ADAM_B1) * g
        v_new = ADAM_B2 * v_ref[...] + (1.0 - ADAM_B2) * (g * g)
        m_hat = m_new / (1.0 - ADAM_B1 ** ADAM_STEP)
        v_hat = v_new / (1.0 - ADAM_B2 ** ADAM_STEP)
        g_out[...] = g
        d_out[...] = -ADAM_LR * (m_hat / (jnp.sqrt(v_hat) + ADAM_EPS) + ADAM_WD * w_ref[...])
        m_out[...] = m_new
        v_out[...] = v_new

    rows, width = w.shape
    spec = pl.BlockSpec((row_tile, width), lambda i: (i, 0))
    return pl.pallas_call(
        body, out_shape=[jax.ShapeDtypeStruct(w.shape, F32)] * 4, grid=(rows // row_tile,),
        in_specs=[spec] * (npart + 3), out_specs=[spec] * 4, compiler_params=_cparams("parallel"), name=name)(*g_parts, w, m, v)


def kernel(x, positions, norm_g, w_in, mla_q_a_norm_g, mla_w_q_b, mla_kv_a_norm_g, mla_w_kv_b, mla_q_norm_g, mla_k_norm_g, dn_conv_w, dn_a_log, dn_dt_bias, dn_out_norm_g, dil_q_norm_g, dil_k_norm_g, w_branch, w_out, loss_target, m_norm_g, m_w_in, m_mla_q_a_norm_g, m_mla_w_q_b, m_mla_kv_a_norm_g, m_mla_w_kv_b, m_mla_q_norm_g, m_mla_k_norm_g, m_dn_conv_w, m_dn_a_log, m_dn_dt_bias, m_dn_out_norm_g, m_dil_q_norm_g, m_dil_k_norm_g, m_w_branch, m_w_out, v_norm_g, v_w_in, v_mla_q_a_norm_g, v_mla_w_q_b, v_mla_kv_a_norm_g, v_mla_w_kv_b, v_mla_q_norm_g, v_mla_k_norm_g, v_dn_conv_w, v_dn_a_log, v_dn_dt_bias, v_dn_out_norm_g, v_dil_q_norm_g, v_dil_k_norm_g, v_w_branch, v_w_out):
    w = dict(norm_g=norm_g, w_in=w_in, mla_q_a_norm_g=mla_q_a_norm_g, mla_w_q_b=mla_w_q_b, mla_kv_a_norm_g=mla_kv_a_norm_g,
             mla_w_kv_b=mla_w_kv_b, mla_q_norm_g=mla_q_norm_g, mla_k_norm_g=mla_k_norm_g, dn_conv_w=dn_conv_w, dn_a_log=dn_a_log,
             dn_dt_bias=dn_dt_bias, dn_out_norm_g=dn_out_norm_g, dil_q_norm_g=dil_q_norm_g, dil_k_norm_g=dil_k_norm_g,
             w_branch=w_branch, w_out=w_out)
    m = dict(norm_g=m_norm_g, w_in=m_w_in, mla_q_a_norm_g=m_mla_q_a_norm_g, mla_w_q_b=m_mla_w_q_b, mla_kv_a_norm_g=m_mla_kv_a_norm_g,
             mla_w_kv_b=m_mla_w_kv_b, mla_q_norm_g=m_mla_q_norm_g, mla_k_norm_g=m_mla_k_norm_g, dn_conv_w=m_dn_conv_w,
             dn_a_log=m_dn_a_log, dn_dt_bias=m_dn_dt_bias, dn_out_norm_g=m_dn_out_norm_g, dil_q_norm_g=m_dil_q_norm_g,
             dil_k_norm_g=m_dil_k_norm_g, w_branch=m_w_branch, w_out=m_w_out)
    v = dict(norm_g=v_norm_g, w_in=v_w_in, mla_q_a_norm_g=v_mla_q_a_norm_g, mla_w_q_b=v_mla_w_q_b, mla_kv_a_norm_g=v_mla_kv_a_norm_g,
             mla_w_kv_b=v_mla_w_kv_b, mla_q_norm_g=v_mla_q_norm_g, mla_k_norm_g=v_mla_k_norm_g, dn_conv_w=v_dn_conv_w,
             dn_a_log=v_dn_a_log, dn_dt_bias=v_dn_dt_bias, dn_out_norm_g=v_dn_out_norm_g, dil_q_norm_g=v_dil_q_norm_g,
             dil_k_norm_g=v_dil_k_norm_g, w_branch=v_w_branch, w_out=v_w_out)
    chip = 2 * lax.axis_index("x") + lax.axis_index("y")
    mat_names = [n for n, _ in MATS]
    mat_shapes = tuple(w[n].shape for n in mat_names)
    conv_shard = dn_conv_w.shape
    win_shape = w_in.shape
    win_rows = (win_shape[0] * win_shape[1], win_shape[2])

    mats_sh = _pack([w[n] for n in mat_names], BF16, 2 * MAT_ROWS)
    mat_rows = mats_sh.shape[0]
    w_in4, mats4 = chip_all_gather([w_in.astype(BF16), mats_sh.reshape(2, mat_rows // 2, LANE)])
    mats4 = mats4.reshape(N_CHIPS, mat_rows, LANE)
    conv8 = all_gather8(_pack([dn_conv_w], F32, 8), "gather_conv_w")
    conv_full = jnp.concatenate([_unpack(conv8[2 * j], (conv_shard,))[0] for j in range(4)], axis=2)
    small = {n: w[n] for n in SMALL}
    tabs = _tables(positions[0])

    def loss_fn(w_in4, mats4, conv_full, small, xs):
        return _local_loss(w_in4, _full_from_chips(mats4, mat_shapes), conv_full, small, xs, loss_target[0], tabs)

    loss, (g_win4, g_mats4, g_conv, g_small, g_x) = jax.value_and_grad(loss_fn, argnums=(0, 1, 2, 3, 4))(
        w_in4, mats4, conv_full, small, x[0])
    loss = lax.psum(loss, ("x", "y", "c"))

    c_idx = lax.axis_index("c")
    half_win = (N_CHIPS * win_shape[1], win_shape[2])
    half_mats = (N_CHIPS * (mat_rows // 2), LANE)
    g_mats4 = g_mats4.reshape(N_CHIPS, 2, mat_rows // 2, LANE)
    pick = lambda g, i, shape: lax.dynamic_index_in_dim(g, i, axis=1, keepdims=False).reshape(shape)
    from_sib = sibling_swap([pick(g_win4, 1 - c_idx, half_win), pick(g_mats4, 1 - c_idx, half_mats)])
    s_win = pair_add(pick(g_win4, c_idx, half_win), from_sib[0], WIN_ROWS, "pair_add_w_in")
    s_mats = pair_add(pick(g_mats4, c_idx, half_mats), from_sib[1], MAT_ROWS, "pair_add_mats")
    r_win, r_mats = chip_all_to_all([s_win.reshape(N_CHIPS, win_shape[1], win_shape[2]),
                                     s_mats.reshape(N_CHIPS, mat_rows // 2, LANE)])
    g_win, g_mats = sibling_all_gather([sum_blocks(r_win, WIN_ROWS, "sum_chip_pieces_w_in"),
                                        sum_blocks(r_mats, MAT_ROWS, "sum_chip_pieces_mats")])
    res_win = adamw([g_win.reshape(win_rows)], *[d['w_in'].reshape(win_rows) for d in (w, m, v)], WIN_ROWS, "adamw_w_in")
    packed = [_pack([d[n] for n in mat_names], F32, 2 * MAT_ROWS) for d in (w, m, v)]
    res_big = [dict(zip(mat_names, _unpack(r, mat_shapes)))
               for r in adamw([g_mats.reshape(mat_rows, LANE)], *packed, MAT_ROWS, "adamw_mats")]
    for k in range(4):
        res_big[k]['w_in'] = res_win[k].reshape(win_shape)

    small_shapes = tuple(w[n].shape for n in SMALL) + (g_conv.shape,)
    g_all = sum_blocks(all_gather8(_pack([g_small[n] for n in SMALL] + [g_conv], F32, 8), "gather_small_grads"), 8, "sum_small")
    g_list = list(_unpack(g_all, small_shapes))
    g_list[-1] = lax.dynamic_slice_in_dim(g_list[-1], chip * conv_shard[2], conv_shard[2], axis=2)
    small_names = list(SMALL) + ['dn_conv_w']
    packed_s = [_pack([d[n] for n in small_names], F32, 8) for d in (w, m, v)]
    shapes_s = tuple(w[n].shape for n in small_names)
    res_small = [dict(zip(small_names, _unpack(r, shapes_s))) for r in adamw([_pack(g_list, F32, 8)], *packed_s, 8, "adamw_small")]

    outs = [loss, g_x[None]]
    for k in range(4):
        outs += [res_big[k][n] if n in res_big[k] else res_small[k][n] for n in WEIGHTS]
    return tuple(outs)
```

```python
import functools
import math

import jax
import jax.numpy as jnp
from jax import lax
from jax.experimental import pallas as pl
from jax.experimental.pallas import tpu as pltpu

F32 = jnp.float32
BF16 = jnp.bfloat16
HI = lax.Precision.HIGHEST
MESH = pl.DeviceIdType.MESH

LANE = 128
VMEM_LIMIT = 48 * 1024 * 1024
ROW_BLOCK = 256
ROW_BLOCK_NARROW = 512
MM_TM, MM_TN, MM_TK = 1024, 1024, 1024
MAT_ROWS = 1664
WIN_ROWS = 128

RMS_EPS = 1e-6
ROPE_THETA = 10000.0
D_MODEL = 1024
DEPTH = 2
MLA_HEADS = 4
MLA_NOPE = 128
MLA_ROPE = 64
MLA_QK = MLA_NOPE + MLA_ROPE
MLA_Q_RANK = 384
MLA_KV_RANK = 256
DN_HEADS = 4
DN_DIM = 128
DN_CONV = 4
GDN_CHUNK = 128
DIL_WINDOWS = (128, 512, 2048)
DIL_DILATIONS = (1, 4, 16)
DIL_GROUPS = 3
DIL_BLOCK = 128
HEAD = 128
BRANCH_W = 512

ADAM_LR = 0.001
ADAM_B1 = 0.9
ADAM_B2 = 0.999
ADAM_EPS = 1e-08
ADAM_WD = 0.01
ADAM_STEP = 10

WEIGHTS = ['norm_g', 'w_in', 'mla_q_a_norm_g', 'mla_w_q_b', 'mla_kv_a_norm_g', 'mla_w_kv_b', 'mla_q_norm_g',
           'mla_k_norm_g', 'dn_conv_w', 'dn_a_log', 'dn_dt_bias', 'dn_out_norm_g', 'dil_q_norm_g', 'dil_k_norm_g',
           'w_branch', 'w_out']
MATS = (('mla_w_q_b', 2), ('mla_w_kv_b', 2), ('w_branch', 3), ('w_out', 1))
SMALL = ('norm_g', 'mla_q_a_norm_g', 'mla_kv_a_norm_g', 'mla_q_norm_g', 'mla_k_norm_g', 'dn_a_log', 'dn_dt_bias',
         'dn_out_norm_g', 'dil_q_norm_g', 'dil_k_norm_g')


def _cparams(*sem):
    return pltpu.CompilerParams(dimension_semantics=sem or None, vmem_limit_bytes=VMEM_LIMIT)


def _tile(dim, target):
    best = 0
    for t in range(LANE, min(dim, target) + 1, LANE):
        if dim % t == 0:
            best = t
    assert best, (dim, target)
    return best


def _mm(a, b, mode, out_dtype, res=None):
    if mode == 'nn':
        (m, k), (k2, n) = a.shape, b.shape
    elif mode == 'nt':
        (m, k), (n, k2) = a.shape, b.shape
    else:
        (k, m), (k2, n) = a.shape, b.shape
    assert k == k2, (a.shape, b.shape, mode)
    tm, tn, tk = _tile(m, MM_TM), _tile(n, MM_TN), _tile(k, MM_TK)
    nk = k // tk
    dims = {'nn': (((1,), (0,)), ((), ())), 'nt': (((1,), (1,)), ((), ())), 'tn': (((0,), (0,)), ((), ()))}[mode]

    def body(*refs):
        if res is None:
            a_ref, b_ref, o_ref = refs[:3]
        else:
            a_ref, b_ref, r_ref, o_ref = refs[:4]
        part = lax.dot_general(a_ref[...].astype(BF16), b_ref[...].astype(BF16), dims, preferred_element_type=F32)

        def finish(r):
            if res is not None:
                r = r + r_ref[...].astype(F32)
            o_ref[...] = r.astype(o_ref.dtype)

        if nk == 1:
            finish(part)
            return
        acc = refs[-1]
        kk = pl.program_id(2)

        @pl.when(kk == 0)
        def _():
            acc[...] = part

        @pl.when(jnp.logical_and(kk > 0, kk < nk - 1))
        def _():
            acc[...] += part

        @pl.when(kk == nk - 1)
        def _():
            finish(acc[...] + part)

    a_spec = pl.BlockSpec((tk, tm), lambda i, j, kk: (kk, i)) if mode == 'tn' else pl.BlockSpec((tm, tk), lambda i, j, kk: (i, kk))
    b_spec = pl.BlockSpec((tn, tk), lambda i, j, kk: (j, kk)) if mode == 'nt' else pl.BlockSpec((tk, tn), lambda i, j, kk: (kk, j))
    o_spec = pl.BlockSpec((tm, tn), lambda i, j, kk: (i, j))
    in_specs = [a_spec, b_spec] + ([o_spec] if res is not None else [])
    args = (a, b) + ((res,) if res is not None else ())
    return pl.pallas_call(
        body, out_shape=jax.ShapeDtypeStruct((m, n), out_dtype), grid=(m // tm, n // tn, nk),
        in_specs=in_specs, out_specs=o_spec, scratch_shapes=[pltpu.VMEM((tm, tn), F32)] if nk > 1 else [],
        compiler_params=_cparams("parallel", "parallel", "arbitrary"),
        name=f"mm_{mode}_{m}x{k}x{n}" + ("_res" if res is not None else ""))(*args)


def _make_matmul(out_dtype, with_res):
    @jax.custom_vjp
    def mm(a, b, *r):
        return _mm(a, b, 'nn', out_dtype, *r)

    def fwd(a, b, *r):
        return mm(a, b, *r), (a, b)

    def bwd(saved, g):
        a, b = saved
        da = _mm(g, b, 'nt', a.dtype)
        db = _mm(a, g, 'tn', b.dtype)
        return (da, db) + ((g,) if with_res else ())

    mm.defvjp(fwd, bwd)
    return mm


@jax.custom_vjp
def matmul_shared(a, bs):
    return tuple(_mm(a, b, 'nn', F32) for b in bs)


def _matmul_shared_bwd(saved, gs):
    a, bs = saved
    da = None
    for i, (g, b) in enumerate(zip(gs, bs)):
        da = _mm(g, b, 'nt', a.dtype if i == len(bs) - 1 else F32, da)
    return da, tuple(_mm(a, g, 'tn', b.dtype) for g, b in zip(gs, bs))


matmul_shared.defvjp(lambda a, bs: (matmul_shared(a, bs), (a, bs)), _matmul_shared_bwd)


def matmul(a, b, out_dtype=F32, res=None):
    if res is None:
        return _make_matmul(out_dtype, False)(a, b)
    return _make_matmul(out_dtype, True)(a, b, res)


def _chunks(ref):
    return [ref[:, c * LANE:(c + 1) * LANE].astype(F32) for c in range(ref.shape[-1] // LANE)]


def _store(ref, chunks):
    for c, ch in enumerate(chunks):
        ref[:, c * LANE:(c + 1) * LANE] = ch.astype(ref.dtype)


def _row_spec(width, br, d=1):
    return pl.BlockSpec((br // d, d * width), lambda i: (i, 0))


def _par_spec(width):
    return pl.BlockSpec((1, width), lambda i: (0, 0))


def _load_rows(ref, scratch, d):
    if d == 1:
        return _chunks(ref)
    n, width = ref.shape[0], ref.shape[1] // d
    for c in range(width // LANE):
        for r in range(d):
            lanes = slice(r * width + c * LANE, r * width + (c + 1) * LANE)
            scratch[c, pl.ds(r, n, stride=d), :] = ref[:, lanes].astype(F32)
    return [scratch[c] for c in range(width // LANE)]


def _store_rows(ref, scratch, d, chunks):
    if d == 1:
        return _store(ref, chunks)
    n, width = ref.shape[0], ref.shape[1] // d
    for c, ch in enumerate(chunks):
        scratch[c] = ch
        for r in range(d):
            lanes = slice(r * width + c * LANE, r * width + (c + 1) * LANE)
            ref[:, lanes] = scratch[c, pl.ds(r, n, stride=d), :].astype(ref.dtype)


def _view_scratch(widths_dils, br):
    return [pltpu.VMEM((w // LANE, br, LANE), F32) for w, d in widths_dils if d > 1]


def _with_scratch(dils, scratch_refs):
    it = iter(scratch_refs)
    return [next(it) if d > 1 else None for d in dils]


def _rw_fwd(f, name, rows, params, outs, br, row_dil, out_dil):
    s = rows[0].shape[0] * row_dil[0]
    br = min(br, s)
    nr, npar, nout = len(rows), len(params), len(outs)
    row_w = [r.shape[1] // d for r, d in zip(rows, row_dil)]

    def body(*refs):
        scr = refs[nr + npar + nout:]
        n_in = sum(d > 1 for d in row_dil)
        rc = [_load_rows(r, sc, d) for r, sc, d in zip(refs[:nr], _with_scratch(row_dil, scr[:n_in]), row_dil)]
        pc = [_chunks(p) for p in refs[nr:nr + npar]]
        res = f(rc, pc)
        for o_ref, sc, d, chs in zip(refs[nr + npar:nr + npar + nout], _with_scratch(out_dil, scr[n_in:]), out_dil, res):
            _store_rows(o_ref, sc, d, chs)

    return pl.pallas_call(
        body, out_shape=[jax.ShapeDtypeStruct((s // d, d * w), dt) for (w, dt), d in zip(outs, out_dil)], grid=(s // br,),
        in_specs=[_row_spec(w, br, d) for w, d in zip(row_w, row_dil)] + [_par_spec(p.shape[1]) for p in params],
        out_specs=[_row_spec(w, br, d) for (w, _), d in zip(outs, out_dil)],
        scratch_shapes=_view_scratch(zip(row_w, row_dil), br) + _view_scratch([(w, d) for (w, _), d in zip(outs, out_dil)], br),
        compiler_params=_cparams("parallel"), name=name + "_fwd")(*rows, *params)


def _rw_bwd(f, name, rows, params, cts, nograd, br, row_dil, out_dil):
    s = rows[0].shape[0] * row_dil[0]
    br = min(br, s)
    nr, npar, nct = len(rows), len(params), len(cts)
    grad_rows = [i for i in range(nr) if i not in nograd]
    row_w = [r.shape[1] // d for r, d in zip(rows, row_dil)]
    ct_w = [c.shape[1] // d for c, d in zip(cts, out_dil)]
    grad_dil = [row_dil[i] for i in grad_rows]

    def body(*refs):
        n_out = len(grad_rows) + npar
        out_refs = refs[nr + npar + nct:nr + npar + nct + n_out]
        scr = refs[nr + npar + nct + n_out:]
        n_in, n_ct = sum(d > 1 for d in row_dil), sum(d > 1 for d in out_dil)
        rc = [_load_rows(r, sc, d) for r, sc, d in zip(refs[:nr], _with_scratch(row_dil, scr[:n_in]), row_dil)]
        pc = [_chunks(p) for p in refs[nr:nr + npar]]
        ct = [_load_rows(c, sc, d) for c, sc, d in
              zip(refs[nr + npar:nr + npar + nct], _with_scratch(out_dil, scr[n_in:n_in + n_ct]), out_dil)]
        _, vjp = jax.vjp(f, rc, pc)
        drc, dpc = vjp(ct)
        for o_ref, sc, d, i in zip(out_refs[:len(grad_rows)], _with_scratch(grad_dil, scr[n_in + n_ct:]), grad_dil, grad_rows):
            _store_rows(o_ref, sc, d, drc[i])
        i0 = pl.program_id(0)
        for o_ref, chs in zip(out_refs[len(grad_rows):], dpc):
            @pl.when(i0 == 0)
            def _(o_ref=o_ref):
                o_ref[...] = jnp.zeros_like(o_ref)
            for c, ch in enumerate(chs):
                o_ref[:, c * LANE:(c + 1) * LANE] += ch

    out_shape = ([jax.ShapeDtypeStruct(rows[i].shape, rows[i].dtype) for i in grad_rows]
                 + [jax.ShapeDtypeStruct(p.shape, F32) for p in params])
    out_specs = [_row_spec(row_w[i], br, row_dil[i]) for i in grad_rows] + [_par_spec(p.shape[1]) for p in params]
    res = pl.pallas_call(
        body, out_shape=out_shape, grid=(s // br,),
        in_specs=([_row_spec(w, br, d) for w, d in zip(row_w, row_dil)] + [_par_spec(p.shape[1]) for p in params]
                  + [_row_spec(w, br, d) for w, d in zip(ct_w, out_dil)]),
        out_specs=out_specs,
        scratch_shapes=(_view_scratch(zip(row_w, row_dil), br) + _view_scratch(zip(ct_w, out_dil), br)
                        + _view_scratch([(row_w[i], row_dil[i]) for i in grad_rows], br)),
        compiler_params=_cparams("arbitrary"), name=name + "_bwd")(*rows, *params, *cts)
    drows = [None] * nr
    for o, i in zip(res[:len(grad_rows)], grad_rows):
        drows[i] = o
    for i in nograd:
        drows[i] = jnp.zeros_like(rows[i])
    return tuple(drows), tuple(res[len(grad_rows):])


def rowwise(f, name, rows, params, outs, nograd=(), br=ROW_BLOCK, row_dil=None, out_dil=None):
    row_dil = tuple(row_dil or [1] * len(rows))
    out_dil = tuple(out_dil or [1] * len(outs))

    @jax.custom_vjp
    def op(rows, params):
        return tuple(_rw_fwd(f, name, rows, params, outs, br, row_dil, out_dil))

    def fwd(rows, params):
        return op(rows, params), (rows, params)

    def bwd(saved, cts):
        rows, params = saved
        return _rw_bwd(f, name, rows, params, list(cts), nograd, br, row_dil, out_dil)

    op.defvjp(fwd, bwd)
    return op(tuple(rows), tuple(params))


def _lane_roll(x, s):
    w = x.shape[-1]

    @jax.custom_vjp
    def r(v):
        return pltpu.roll(v, s, 1)

    r.defvjp(lambda v: (r(v), None), lambda _, g: (pltpu.roll(g, (w - s) % w, 1),))
    return r(x)


def _lanes(x):
    return lax.broadcasted_iota(jnp.int32, x.shape, 1)


def _rms(x, g, n=LANE):
    return x * lax.rsqrt(jnp.sum(x * x, axis=-1, keepdims=True) / n + RMS_EPS) * g


def _rope128(x, cos, sin_signed):
    return x * cos + _lane_roll(x, 64) * sin_signed


def _rope64(x, cos, sin):
    lane = _lanes(x)
    rot = jnp.where(lane < 32, -_lane_roll(x, 96), jnp.where(lane < 64, _lane_roll(x, 32), 0.0))
    return x * cos + rot * sin


def _silu(z):
    return z * jax.nn.sigmoid(z)


def _softplus(x):
    return jnp.maximum(x, 0.0) + jnp.log1p(jnp.exp(-jnp.abs(x)))


def f_rms_full(rc, pc):
    x, g = rc[0], pc[0]
    width = len(x) * LANE
    ms = sum(jnp.sum(c * c, axis=-1, keepdims=True) for c in x) / width
    r = lax.rsqrt(ms + RMS_EPS)
    return [[c * r * gc for c, gc in zip(x, g)]]


def f_mla1(rc, pc):
    x = rc[0]
    qn = f_rms_full([x[0:3]], [pc[0]])[0]
    kvn = f_rms_full([x[3:5]], [pc[1]])[0]
    return [qn, kvn, [x[5]]]


def f_mla2(rc, pc):
    q, kv, kpe, cos, sin = rc[0], rc[1], rc[2][0], rc[3][0], rc[4][0]
    gqn, gqp, gkn, gkp = pc[0][0], pc[1][0], pc[2][0], pc[3][0]
    k_pe = _rope64(_rms(kpe, gkp, MLA_ROPE), cos, sin)
    q_att, k_att, v = [], [], []
    for h in range(MLA_HEADS):
        q_att += [_rms(q[h], gqn), _rope64(_rms(q[MLA_HEADS + h], gqp, MLA_ROPE), cos, sin)]
        k_att += [_rms(kv[2 * h], gkn), k_pe]
        v.append(kv[2 * h + 1])
    return [q_att, k_att, v]


def f_gates(rc, pc):
    x, a_log, dt_bias = rc[0][0], pc[0][0], pc[1][0]
    lane = _lanes(x)
    g = -jnp.exp(a_log) * _softplus(x + dt_bias)
    return [[jnp.where(lane < DN_HEADS, g, jnp.where(lane < 2 * DN_HEADS, jax.nn.sigmoid(x), 0.0))]]


def f_headnorm(rc, pc):
    return [[_rms(c, pc[0][0]) for c in rc[0]]]


def f_dil(rc, pc):
    x, cos, sin = rc[0], rc[1][0], rc[2][0]
    gq, gk = pc[0][0], pc[1][0]
    n = len(x) // 3
    q = [_rope128(_rms(c, gq), cos, sin) for c in x[:n]]
    k = [_rope128(_rms(c, gk), cos, sin) for c in x[n:2 * n]]
    v = list(x[2 * n:])
    per = n // DIL_GROUPS
    return [t[g * per:(g + 1) * per] for g in range(DIL_GROUPS) for t in (q, k, v)]


def f_comb(rc, pc):
    o, l = rc[:DIL_GROUPS], rc[DIL_GROUPS:]
    out = []
    for c in range(len(o[0])):
        m = functools.reduce(jnp.maximum, [lg[c] for lg in l])
        e = [jnp.exp(lg[c] - m) for lg in l]
        den = sum(e)
        out.append(sum(eg * og[c] for eg, og in zip(e, o)) / den)
    return [out]


def f_merge1(rc, pc):
    z = rc[3]
    n = len(rc[0])
    return [[y * _silu(z[b * n + c]) for c, y in enumerate(rc[b])] for b in range(3)]


def f_merge2(rc, pc):
    gate = rc[0]
    n = len(rc[1])
    return [[sum(jax.nn.sigmoid(gate[b * n + c]) * rc[1 + b][c] for b in range(3)) for c in range(n)]]


MLA_DQK = 2 * LANE
MLA_SCALE = MLA_QK ** -0.5


def _mla_attn_fwd(q, k, v):
    s = q.shape[0]
    h = q.shape[1] // MLA_DQK
    t = min(512, s)
    tk = 2 * t
    n, nkv = s // t, s // tk

    def body(q_ref, k_ref, v_ref, o_ref, lse_ref, m_sc, l_sc, acc_sc):
        qi, kj = pl.program_id(1), pl.program_id(2)
        last = qi // 2

        @pl.when(kj == 0)
        def _():
            m_sc[...] = jnp.full_like(m_sc, -jnp.inf)
            l_sc[...] = jnp.zeros_like(l_sc)
            acc_sc[...] = jnp.zeros_like(acc_sc)

        nsub = 2 if t % 256 == 0 else 1
        ts = t // nsub
        rows = [slice(r * ts, (r + 1) * ts) for r in range(nsub)]

        def step(on_diagonal):
            sc = [lax.dot_general(q_ref[rw, :], k_ref[...], (((1,), (1,)), ((), ())), preferred_element_type=F32) * MLA_SCALE
                  for rw in rows]
            if on_diagonal:
                gap = lax.broadcasted_iota(jnp.int32, (ts, tk), 1) - lax.broadcasted_iota(jnp.int32, (ts, tk), 0)
                sc = [jnp.where(gap <= qi * t + r * ts - kj * tk, x, -jnp.inf) for r, x in enumerate(sc)]
            m_old = [m_sc[rw, :] for rw in rows]
            m_new = [jnp.maximum(mo, jnp.max(x, axis=-1, keepdims=True)) for mo, x in zip(m_old, sc)]
            alpha = [jnp.exp(mo - mn) for mo, mn in zip(m_old, m_new)]
            p = [jnp.exp(x - mn) for x, mn in zip(sc, m_new)]
            pv = [jnp.dot(x.astype(BF16), v_ref[...], preferred_element_type=F32) for x in p]
            for r, rw in enumerate(rows):
                l_sc[rw, :] = alpha[r] * l_sc[rw, :] + jnp.sum(p[r], axis=-1, keepdims=True)
                acc_sc[rw, :] = alpha[r] * acc_sc[rw, :] + pv[r]
                m_sc[rw, :] = m_new[r]

        @pl.when(kj < last)
        def _():
            step(False)

        @pl.when(kj == last)
        def _():
            step(True)

        @pl.when(kj == nkv - 1)
        def _():
            o_ref[...] = acc_sc[...] / l_sc[...]
            lse_ref[...] = jnp.broadcast_to(m_sc[...] + jnp.log(l_sc[...]), lse_ref.shape)

    return pl.pallas_call(
        body, out_shape=[jax.ShapeDtypeStruct((s, h * HEAD), F32)] * 2, grid=(h, n, nkv),
        in_specs=[pl.BlockSpec((t, MLA_DQK), lambda hh, i, j: (i, hh)),
                  pl.BlockSpec((tk, MLA_DQK), lambda hh, i, j: (jnp.minimum(j, i // 2), hh)),
                  pl.BlockSpec((tk, HEAD), lambda hh, i, j: (jnp.minimum(j, i // 2), hh))],
        out_specs=[pl.BlockSpec((t, HEAD), lambda hh, i, j: (i, hh))] * 2,
        scratch_shapes=[pltpu.VMEM((t, 1), F32), pltpu.VMEM((t, 1), F32), pltpu.VMEM((t, HEAD), F32)],
        compiler_params=_cparams("parallel", "parallel", "arbitrary"), name="mla_attn_fwd")(q, k, v)


def _mla_attn_bwd(q, k, v, o, lse, do):
    s = q.shape[0]
    h = q.shape[1] // MLA_DQK
    t = min(512, s)
    n = s // t
    nt, tn = (((1,), (1,)), ((), ())), (((0,), (0,)), ((), ()))

    def body(q_ref, k_ref, v_ref, o_ref, lse_ref, do_ref, dq_ref, dk_ref, dv_ref, dq_sc, dk_sc, dv_sc):
        kj, qi = pl.program_id(1), pl.program_id(2)

        @pl.when(jnp.logical_and(kj == 0, qi == 0))
        def _():
            dq_sc[...] = jnp.zeros_like(dq_sc)

        @pl.when(qi == 0)
        def _():
            dk_sc[...] = jnp.zeros_like(dk_sc)
            dv_sc[...] = jnp.zeros_like(dv_sc)

        def pair(on_diagonal):
            sc = lax.dot_general(q_ref[...], k_ref[...], nt, preferred_element_type=F32) * MLA_SCALE
            p = jnp.exp(sc - lse_ref[:, 0:1])
            if on_diagonal:
                p = jnp.where(lax.broadcasted_iota(jnp.int32, (t, t), 1) <= lax.broadcasted_iota(jnp.int32, (t, t), 0), p, 0.0)
            do_v = do_ref[...]
            do_b = do_v.astype(BF16)
            dp = lax.dot_general(do_b, v_ref[...], nt, preferred_element_type=F32)
            delta = jnp.sum(do_v * o_ref[...], axis=-1, keepdims=True)
            ds = (p * (dp - delta) * MLA_SCALE).astype(BF16)
            dv_sc[...] += lax.dot_general(p.astype(BF16), do_b, tn, preferred_element_type=F32)
            dk_sc[...] += lax.dot_general(ds, q_ref[...], tn, preferred_element_type=F32)
            rows = pl.ds(pl.multiple_of(qi * t, t), t)
            dq_sc[rows, :] += jnp.dot(ds, k_ref[...], preferred_element_type=F32)

        @pl.when(qi > kj)
        def _():
            pair(False)

        @pl.when(qi == kj)
        def _():
            pair(True)

        @pl.when(qi == n - 1)
        def _():
            dk_ref[...] = dk_sc[...].astype(dk_ref.dtype)
            dv_ref[...] = dv_sc[...].astype(dv_ref.dtype)

        @pl.when(jnp.logical_and(kj == n - 1, qi == n - 1))
        def _():
            dq_ref[...] = dq_sc[...].astype(dq_ref.dtype)

    qmap = lambda hh, j, i: (jnp.maximum(i, j), hh)
    kmap = lambda hh, j, i: (j, hh)
    return pl.pallas_call(
        body, out_shape=[jax.ShapeDtypeStruct(q.shape, BF16), jax.ShapeDtypeStruct(k.shape, BF16), jax.ShapeDtypeStruct(v.shape, BF16)],
        grid=(h, n, n),
        in_specs=[pl.BlockSpec((t, MLA_DQK), qmap), pl.BlockSpec((t, MLA_DQK), kmap), pl.BlockSpec((t, HEAD), kmap),
                  pl.BlockSpec((t, HEAD), qmap), pl.BlockSpec((t, HEAD), qmap), pl.BlockSpec((t, HEAD), qmap)],
        out_specs=[pl.BlockSpec((s, MLA_DQK), lambda hh, j, i: (0, hh)), pl.BlockSpec((t, MLA_DQK), kmap),
                   pl.BlockSpec((t, HEAD), kmap)],
        scratch_shapes=[pltpu.VMEM((s, MLA_DQK), F32), pltpu.VMEM((t, MLA_DQK), F32), pltpu.VMEM((t, HEAD), F32)],
        compiler_params=_cparams("parallel", "arbitrary", "arbitrary"), name="mla_attn_bwd")(q, k, v, o, lse, do)


@jax.custom_vjp
def mla_attention(q, k, v):
    return _mla_attn_fwd(q, k, v)[0]


def _mla_attention_fwd(q, k, v):
    o, lse = _mla_attn_fwd(q, k, v)
    return o, (q, k, v, o, lse)


def _mla_attention_bwd(saved, do):
    return tuple(_mla_attn_bwd(*saved, do))


mla_attention.defvjp(_mla_attention_fwd, _mla_attention_bwd)


DIL_SCALE = HEAD ** -0.5
GROUP_W = 4 * HEAD


def _dil_scores(q, kp, kc, n):
    dn = (((1,), (1,)), ((), ()))
    sp = lax.dot_general(q, kp, dn, preferred_element_type=F32) * DIL_SCALE
    sc = lax.dot_general(q, kc, dn, preferred_element_type=F32) * DIL_SCALE
    qi = lax.broadcasted_iota(jnp.int32, sp.shape, 0)
    kc_i = lax.broadcasted_iota(jnp.int32, sp.shape, 1)
    vp = jnp.logical_and(kc_i >= qi, n > 0)
    vc = kc_i <= qi
    return sp, sc, vp, vc


def _dil_specs(d):
    cur = pl.BlockSpec((DIL_BLOCK, GROUP_W), lambda r, n: (n, r))
    prev = pl.BlockSpec((DIL_BLOCK, GROUP_W), lambda r, n: (jnp.maximum(n - 1, 0), r))
    return cur, prev


def _dil_fwd(q, k, v, d):
    l = q.shape[0]
    nb = l // DIL_BLOCK
    cur, prev = _dil_specs(d)

    def body(q_ref, kp_ref, kc_ref, vp_ref, vc_ref, o_ref, lse_ref):
        n = pl.program_id(1)
        heads = range(4)
        sl = [slice(h * HEAD, (h + 1) * HEAD) for h in heads]
        scores = [_dil_scores(q_ref[:, sl[h]], kp_ref[:, sl[h]], kc_ref[:, sl[h]], n) for h in heads]
        sp = [jnp.where(vp, s_p, -jnp.inf) for s_p, _, vp, _ in scores]
        sc = [jnp.where(vc, s_c, -jnp.inf) for _, s_c, _, vc in scores]
        m = [jnp.maximum(jnp.max(sp[h], axis=-1, keepdims=True), jnp.max(sc[h], axis=-1, keepdims=True)) for h in heads]
        ep = [jnp.exp(sp[h] - m[h]) for h in heads]
        ec = [jnp.exp(sc[h] - m[h]) for h in heads]
        den = [jnp.sum(ep[h], axis=-1, keepdims=True) + jnp.sum(ec[h], axis=-1, keepdims=True) for h in heads]
        acc = [jnp.dot(ep[h].astype(BF16), vp_ref[:, sl[h]], preferred_element_type=F32)
               + jnp.dot(ec[h].astype(BF16), vc_ref[:, sl[h]], preferred_element_type=F32) for h in heads]
        for h in heads:
            o_ref[:, sl[h]] = acc[h] / den[h]
            lse_ref[:, sl[h]] = jnp.broadcast_to(m[h] + jnp.log(den[h]), (DIL_BLOCK, HEAD))

    return pl.pallas_call(
        body, out_shape=[jax.ShapeDtypeStruct(q.shape, F32)] * 2, grid=(d, nb),
        in_specs=[cur, prev, cur, prev, cur], out_specs=[cur, cur],
        compiler_params=_cparams("parallel", "parallel"), name=f"dil_fwd_d{d}")(q, k, k, v, v)


def _dil_bwd(q, k, v, o, lse, do, dlse, d):
    l = q.shape[0]
    nb = l // DIL_BLOCK
    tn = (((0,), (0,)), ((), ()))
    nt = (((1,), (1,)), ((), ()))
    cur = pl.BlockSpec((DIL_BLOCK, GROUP_W), lambda r, n: (jnp.minimum(n, nb - 1), r))
    prev = pl.BlockSpec((DIL_BLOCK, GROUP_W), lambda r, n: (jnp.maximum(jnp.minimum(n, nb - 1) - 1, 0), r))
    lag = pl.BlockSpec((DIL_BLOCK, GROUP_W), lambda r, n: (jnp.maximum(n - 1, 0), r))

    def body(q_ref, kp_ref, kc_ref, vp_ref, vc_ref, o_ref, lse_ref, do_ref, dl_ref, dq_ref, dk_ref, dv_ref, ck_sc, cv_sc):
        n = pl.program_id(1)

        @pl.when(n < nb)
        def _():
            heads = range(4)
            sl = [slice(h * HEAD, (h + 1) * HEAD) for h in heads]
            scores = [_dil_scores(q_ref[:, sl[h]], kp_ref[:, sl[h]], kc_ref[:, sl[h]], n) for h in heads]
            lse = [lse_ref[:, h * HEAD:h * HEAD + 1] for h in heads]
            pp = [jnp.where(scores[h][2], jnp.exp(scores[h][0] - lse[h]), 0.0) for h in heads]
            pc = [jnp.where(scores[h][3], jnp.exp(scores[h][1] - lse[h]), 0.0) for h in heads]
            do_b = [do_ref[:, sl[h]].astype(BF16) for h in heads]
            corr = [jnp.sum(dl_ref[:, sl[h]], axis=-1, keepdims=True)
                    - jnp.sum(do_ref[:, sl[h]] * o_ref[:, sl[h]], axis=-1, keepdims=True) for h in heads]
            dsp = [(pp[h] * (lax.dot_general(do_b[h], vp_ref[:, sl[h]], nt, preferred_element_type=F32) + corr[h])
                    * DIL_SCALE).astype(BF16) for h in heads]
            dsc = [(pc[h] * (lax.dot_general(do_b[h], vc_ref[:, sl[h]], nt, preferred_element_type=F32) + corr[h])
                    * DIL_SCALE).astype(BF16) for h in heads]
            dkp = [lax.dot_general(dsp[h], q_ref[:, sl[h]], tn, preferred_element_type=F32) for h in heads]
            dvp = [lax.dot_general(pp[h].astype(BF16), do_b[h], tn, preferred_element_type=F32) for h in heads]
            for h in heads:
                dq_ref[:, sl[h]] = (jnp.dot(dsp[h], kp_ref[:, sl[h]], preferred_element_type=F32)
                                    + jnp.dot(dsc[h], kc_ref[:, sl[h]], preferred_element_type=F32)).astype(dq_ref.dtype)

            @pl.when(n > 0)
            def _():
                for h in heads:
                    dk_ref[:, sl[h]] = (ck_sc[:, sl[h]] + dkp[h]).astype(dk_ref.dtype)
                    dv_ref[:, sl[h]] = (cv_sc[:, sl[h]] + dvp[h]).astype(dv_ref.dtype)

            for h in heads:
                ck_sc[:, sl[h]] = lax.dot_general(dsc[h], q_ref[:, sl[h]], tn, preferred_element_type=F32)
                cv_sc[:, sl[h]] = lax.dot_general(pc[h].astype(BF16), do_b[h], tn, preferred_element_type=F32)

        @pl.when(n == nb)
        def _():
            dk_ref[...] = ck_sc[...].astype(dk_ref.dtype)
            dv_ref[...] = cv_sc[...].astype(dv_ref.dtype)

    return pl.pallas_call(
        body, out_shape=[jax.ShapeDtypeStruct(q.shape, BF16)] * 3, grid=(d, nb + 1),
        in_specs=[cur, prev, cur, prev, cur, cur, cur, cur, cur], out_specs=[cur, lag, lag],
        scratch_shapes=[pltpu.VMEM((DIL_BLOCK, GROUP_W), F32)] * 2,
        compiler_params=_cparams("parallel", "arbitrary"), name=f"dil_bwd_d{d}")(q, k, k, v, v, o, lse, do, dlse)


def _make_dil(d):
    @jax.custom_vjp
    def att(q, k, v):
        return tuple(_dil_fwd(q, k, v, d))

    def fwd(q, k, v):
        o, lse = _dil_fwd(q, k, v, d)
        return (o, lse), (q, k, v, o, lse)

    def bwd(saved, cts):
        return tuple(_dil_bwd(*saved, cts[0], cts[1], d))

    att.defvjp(fwd, bwd)
    return att


def dilated_group(q, k, v, d):
    return _make_dil(d)(q, k, v)


def _dn_post(c, kind, scale):
    m = _silu(c)
    nrm = m * lax.rsqrt(jnp.sum(m * m, axis=-1, keepdims=True) + 1e-6) * scale
    return kind * nrm + (1.0 - kind) * m


def _dn_kind_scale(j):
    kind = jnp.where(j < 2 * DN_HEADS, 1.0, 0.0).astype(F32)
    scale = jnp.where(j < DN_HEADS, DN_DIM ** -0.5, 1.0).astype(F32)
    return kind, scale


_CONV_RB = 512
_CONV_PAD = 8


def _conv_windows(pad_ref, w_ref, r0, rb, sign):
    acc = None
    for sh in range(DN_CONV):
        win = pad_ref[pl.ds(r0 + _CONV_PAD * (sign < 0) + sign * sh, rb), :]
        term = w_ref[DN_CONV - 1 - sh:DN_CONV - sh, :] * win
        acc = term if acc is None else acc + term
    return acc


def _dn_conv_fwd(x, w):
    s, width = x.shape
    rb = min(_CONV_RB, s)

    def body(x_ref, w_ref, o_ref, pad_ref):
        kind, scale = _dn_kind_scale(pl.program_id(0))
        pad_ref[0:_CONV_PAD, :] = jnp.zeros((_CONV_PAD, LANE), F32)
        pad_ref[_CONV_PAD:, :] = x_ref[...]
        for r0 in range(0, s, rb):
            c = _conv_windows(pad_ref, w_ref, r0, rb, -1)
            o_ref[r0:r0 + rb, :] = _dn_post(c, kind, scale)

    return pl.pallas_call(
        body, out_shape=jax.ShapeDtypeStruct(x.shape, F32), grid=(width // LANE,),
        in_specs=[pl.BlockSpec((s, LANE), lambda j: (0, j)), pl.BlockSpec((DN_CONV, LANE), lambda j: (0, j))],
        out_specs=pl.BlockSpec((s, LANE), lambda j: (0, j)), scratch_shapes=[pltpu.VMEM((s + _CONV_PAD, LANE), F32)],
        compiler_params=_cparams("parallel"), name="dn_conv_fwd")(x, w)


def _dn_conv_bwd(x, w, dy):
    s, width = x.shape
    rb = min(_CONV_RB, s)

    def body(x_ref, w_ref, dy_ref, dx_ref, dw_ref, pad_ref, dpad_ref):
        kind, scale = _dn_kind_scale(pl.program_id(0))
        pad_ref[0:_CONV_PAD, :] = jnp.zeros((_CONV_PAD, LANE), F32)
        pad_ref[_CONV_PAD:, :] = x_ref[...]
        dpad_ref[s:, :] = jnp.zeros((_CONV_PAD, LANE), F32)
        dws = [jnp.zeros((1, LANE), F32) for _ in range(DN_CONV)]
        for r0 in range(0, s, rb):
            c = _conv_windows(pad_ref, w_ref, r0, rb, -1)
            _, vjp = jax.vjp(lambda cc: _dn_post(cc, kind, scale), c)
            dc = vjp(dy_ref[r0:r0 + rb, :])[0]
            dpad_ref[r0:r0 + rb, :] = dc
            for sh in range(DN_CONV):
                win = pad_ref[pl.ds(r0 + _CONV_PAD - sh, rb), :]
                dws[DN_CONV - 1 - sh] = dws[DN_CONV - 1 - sh] + jnp.sum(dc * win, axis=0, keepdims=True)
        for j in range(DN_CONV):
            dw_ref[j:j + 1, :] = dws[j]
        for r0 in range(0, s, rb):
            dx_ref[r0:r0 + rb, :] = _conv_windows(dpad_ref, w_ref, r0, rb, 1)

    return pl.pallas_call(
        body, out_shape=[jax.ShapeDtypeStruct(x.shape, F32), jax.ShapeDtypeStruct(w.shape, F32)], grid=(width // LANE,),
        in_specs=[pl.BlockSpec((s, LANE), lambda j: (0, j)), pl.BlockSpec((DN_CONV, LANE), lambda j: (0, j)),
                  pl.BlockSpec((s, LANE), lambda j: (0, j))],
        out_specs=[pl.BlockSpec((s, LANE), lambda j: (0, j)), pl.BlockSpec((DN_CONV, LANE), lambda j: (0, j))],
        scratch_shapes=[pltpu.VMEM((s + _CONV_PAD, LANE), F32), pltpu.VMEM((s + _CONV_PAD, LANE), F32)],
        compiler_params=_cparams("parallel"), name="dn_conv_bwd")(x, w, dy)


@jax.custom_vjp
def dn_conv(x, w):
    return _dn_conv_fwd(x, w)


dn_conv.defvjp(lambda x, w: (_dn_conv_fwd(x, w), (x, w)), lambda saved, dy: tuple(_dn_conv_bwd(*saved, dy)))


_NN = (((1,), (0,)), ((), ()))
_NT = (((1,), (1,)), ((), ()))
_TN = (((0,), (0,)), ((), ()))


def _bd(a, b, dims, passes=1):
    d = lambda x, y: lax.dot_general(x, y, dims, preferred_element_type=F32)
    if passes == 0:
        return lax.dot_general(a, b, dims, precision=lax.Precision.HIGHEST, preferred_element_type=F32)
    ah, bh = a.astype(BF16), b.astype(BF16)
    if passes == 1:
        return d(ah, bh)
    al, bl = (a - ah.astype(F32)).astype(BF16), (b - bh.astype(F32)).astype(BF16)
    return d(ah, bh) + d(ah, bl) + d(al, bh)


@functools.partial(jax.custom_vjp, nondiff_argnums=(2, 3))
def _pdot(a, b, dims, passes):
    return _bd(a, b, dims, passes)


def _pdot_bwd(dims, passes, saved, g):
    a, b = saved
    if dims == _NN:
        return _bd(g, b, _NT, passes), _bd(a, g, _TN, passes)
    if dims == _NT:
        return _bd(g, b, _NN, passes), _bd(g, a, _TN, passes)
    return _bd(b, g, _NT, passes), _bd(a, g, _NN, passes)


_pdot.defvjp(lambda a, b, dims, passes: (_bd(a, b, dims, passes), (a, b)), _pdot_bwd)


GDN_DOT_PASSES = 1
GDN_SOLVE_PASSES = 3


def _hdot(a, b, dims=_NN):
    return _pdot(a, b, dims, GDN_DOT_PASSES)


def _xdot(a, b, dims=_NN):
    return _pdot(a, b, dims, GDN_SOLVE_PASSES)


def _split3(x):
    hi = x.astype(BF16)
    r1 = x - hi.astype(F32)
    mid = r1.astype(BF16)
    lo = (r1 - mid.astype(F32)).astype(BF16)
    return hi, mid, lo


def _tri_dot(tri, x, dims):
    t = tri.astype(BF16)
    return sum(lax.dot_general(t, p, dims, preferred_element_type=F32) for p in _split3(x))


@jax.custom_vjp
def _cumsum_rows(x):
    c = x.shape[0]
    tri = lax.broadcasted_iota(jnp.int32, (c, c), 0) >= lax.broadcasted_iota(jnp.int32, (c, c), 1)
    return _tri_dot(tri, x, _NN)


def _cumsum_rows_bwd(_, g):
    c = g.shape[0]
    tri = lax.broadcasted_iota(jnp.int32, (c, c), 0) >= lax.broadcasted_iota(jnp.int32, (c, c), 1)
    return (_tri_dot(tri, g, _TN),)


_cumsum_rows.defvjp(lambda x: (_cumsum_rows(x), None), _cumsum_rows_bwd)


@jax.custom_vjp
def _unit_lower_inverses(a):
    c = a[0].shape[0]
    eye = (lax.broadcasted_iota(jnp.int32, (c, c), 0) == lax.broadcasted_iota(jnp.int32, (c, c), 1)).astype(F32)
    pw = [-x for x in a]
    t = [eye + p for p in pw]
    for _ in range(int(math.log2(c)) - 1):
        pw = [_bd(p, p, _NN, GDN_SOLVE_PASSES) for p in pw]
        t = [x + _bd(x, p, _NN, GDN_SOLVE_PASSES) for x, p in zip(t, pw)]
    return t


def _unit_lower_inverses_bwd(t, g):
    left = [_bd(x, y, _TN, GDN_SOLVE_PASSES) for x, y in zip(t, g)]
    return ([-_bd(x, y, _NT, GDN_SOLVE_PASSES) for x, y in zip(left, t)],)


_unit_lower_inverses.defvjp(lambda a: (lambda t: (t, t))(_unit_lower_inverses(a)), _unit_lower_inverses_bwd)


@jax.custom_vjp
def _kept_inverses(a, t):
    return t


_kept_inverses.defvjp(lambda a, t: (t, t),
                      lambda t, g: (_unit_lower_inverses_bwd(t, g)[0], [jnp.zeros_like(x) for x in t]))


def _gdn_prep_fn(qkv, gb, kept=None, keep=None):
    c = GDN_CHUNK
    row = lax.broadcasted_iota(jnp.int32, (c, c), 0)
    col = lax.broadcasted_iota(jnp.int32, (c, c), 1)
    incl, strict = row >= col, row > col
    lane = _lanes(gb)
    heads = range(DN_HEADS)
    q, k, v = qkv[:DN_HEADS], qkv[DN_HEADS:2 * DN_HEADS], qkv[2 * DN_HEADS:]
    g = [jnp.sum(jnp.where(lane == h, gb, 0.0), axis=-1, keepdims=True) for h in heads]
    beta = [jnp.sum(jnp.where(lane == DN_HEADS + h, gb, 0.0), axis=-1, keepdims=True) for h in heads]
    gcb = [_cumsum_rows(jnp.broadcast_to(g[h], (c, c))) for h in heads]
    decay = [jnp.where(incl, jnp.exp(jnp.where(incl, gcb[h] - gcb[h].T, 0.0)), 0.0) for h in heads]
    kb = [k[h] * beta[h] for h in heads]
    a = [jnp.where(strict, _hdot(kb[h], k[h], _NT) * decay[h], 0.0) for h in heads]
    t = _unit_lower_inverses(a) if kept is None else _kept_inverses(a, kept)
    if keep is not None:
        keep.extend(t)
    eg = [jnp.exp(gcb[h]) for h in heads]
    u = [_xdot(t[h], v[h] * beta[h]) for h in heads]
    w = [_xdot(t[h], kb[h] * eg[h]) for h in heads]
    qk = [jnp.where(incl, _hdot(q[h], k[h], _NT) * decay[h], 0.0) for h in heads]
    g_last = [jnp.sum(jnp.where(row == c - 1, gcb[h], 0.0), axis=0, keepdims=True) for h in heads]
    kg = [k[h] * jnp.exp(g_last[h] - gcb[h]) for h in heads]
    qg = [q[h] * eg[h] for h in heads]
    el = [jnp.broadcast_to(jnp.exp(g_last[h]), (8, LANE)) for h in heads]
    return [u, w, qg, kg, qk, el]


def _gdn_prep_specs(nq):
    c = GDN_CHUNK
    big = pl.BlockSpec((c, nq * LANE), lambda n: (n, 0))
    return big


def _gdn_prep_fwd(qkv, gb):
    s = qkv.shape[0]
    c = GDN_CHUNK
    n = s // c
    hw = DN_HEADS * LANE

    def body(qkv_ref, gb_ref, u_ref, w_ref, qg_ref, kg_ref, qk_ref, el_ref, t_ref):
        kept = []
        res = _gdn_prep_fn(_chunks(qkv_ref), gb_ref[...], keep=kept)
        for ref, chs in zip((u_ref, w_ref, qg_ref, kg_ref, qk_ref, el_ref, t_ref), res + [kept]):
            _store(ref, chs)

    row = pl.BlockSpec((c, hw), lambda i: (i, 0))
    return pl.pallas_call(
        body, out_shape=[jax.ShapeDtypeStruct((s, hw), F32)] * 5 + [jax.ShapeDtypeStruct((n * 8, hw), F32),
                                                                   jax.ShapeDtypeStruct((s, hw), F32)], grid=(n,),
        in_specs=[pl.BlockSpec((c, 3 * hw), lambda i: (i, 0)), pl.BlockSpec((c, LANE), lambda i: (i, 0))],
        out_specs=[row] * 5 + [pl.BlockSpec((8, hw), lambda i: (i, 0)), row],
        compiler_params=_cparams("parallel"), name="gdn_prep_fwd")(qkv, gb)


def _gdn_prep_bwd(qkv, gb, t, cts):
    s = qkv.shape[0]
    c = GDN_CHUNK
    n = s // c
    hw = DN_HEADS * LANE

    def body(qkv_ref, gb_ref, t_ref, du_ref, dw_ref, dqg_ref, dkg_ref, dqk_ref, del_ref, dqkv_ref, dgb_ref):
        kept = _chunks(t_ref)
        _, vjp = jax.vjp(lambda x, y: _gdn_prep_fn(x, y, kept=kept), _chunks(qkv_ref), gb_ref[...])
        ct = [_chunks(r) for r in (du_ref, dw_ref, dqg_ref, dkg_ref, dqk_ref, del_ref)]
        dqkv, dgb = vjp(ct)
        _store(dqkv_ref, dqkv)
        dgb_ref[...] = dgb

    row = pl.BlockSpec((c, hw), lambda i: (i, 0))
    return pl.pallas_call(
        body, out_shape=[jax.ShapeDtypeStruct(qkv.shape, F32), jax.ShapeDtypeStruct(gb.shape, F32)], grid=(n,),
        in_specs=[pl.BlockSpec((c, 3 * hw), lambda i: (i, 0)), pl.BlockSpec((c, LANE), lambda i: (i, 0)), row]
        + [row] * 5 + [pl.BlockSpec((8, hw), lambda i: (i, 0))],
        out_specs=[pl.BlockSpec((c, 3 * hw), lambda i: (i, 0)), pl.BlockSpec((c, LANE), lambda i: (i, 0))],
        compiler_params=_cparams("parallel"), name="gdn_prep_bwd")(qkv, gb, t, *cts)


@jax.custom_vjp
def gdn_prep(qkv, gb):
    return tuple(_gdn_prep_fwd(qkv, gb)[:6])


def _gdn_prep_vfwd(qkv, gb):
    res = _gdn_prep_fwd(qkv, gb)
    return tuple(res[:6]), (qkv, gb, res[6])


gdn_prep.defvjp(_gdn_prep_vfwd, lambda saved, cts: tuple(_gdn_prep_bwd(*saved, cts)))


def _gdn_scan_fwd(u, w, qg, kg, qk, el):
    s = u.shape[0]
    c = GDN_CHUNK
    n = s // c
    hw = DN_HEADS * LANE

    def body(u_ref, w_ref, qg_ref, kg_ref, qk_ref, el_ref, o_ref, st_ref, s_sc):
        @pl.when(pl.program_id(0) == 0)
        def _():
            s_sc[...] = jnp.zeros_like(s_sc)

        heads = range(DN_HEADS)
        sl = [slice(h * LANE, (h + 1) * LANE) for h in heads]
        st = [s_sc[h] for h in heads]
        for h in heads:
            st_ref[sl[h], :] = st[h]
        v_new = [u_ref[:, sl[h]] - _hdot(w_ref[:, sl[h]], st[h]) for h in heads]
        o_st = [_hdot(qg_ref[:, sl[h]], st[h]) for h in heads]
        o_in = [_hdot(qk_ref[:, sl[h]], v_new[h]) for h in heads]
        s_up = [_hdot(kg_ref[:, sl[h]], v_new[h], _TN) for h in heads]
        for h in heads:
            o_ref[:, sl[h]] = o_st[h] + o_in[h]
            s_sc[h] = st[h] * el_ref[0:1, sl[h]] + s_up[h]

    row = pl.BlockSpec((c, hw), lambda i: (i, 0))
    return pl.pallas_call(
        body, out_shape=[jax.ShapeDtypeStruct((s, hw), F32), jax.ShapeDtypeStruct((n, hw, LANE), F32)], grid=(n,),
        in_specs=[row] * 5 + [pl.BlockSpec((8, hw), lambda i: (i, 0))],
        out_specs=[row, pl.BlockSpec((None, hw, LANE), lambda i: (i, 0, 0))],
        scratch_shapes=[pltpu.VMEM((DN_HEADS, LANE, LANE), F32)],
        compiler_params=_cparams("arbitrary"), name="gdn_scan_fwd")(u, w, qg, kg, qk, el)


def _gdn_scan_bwd(u, w, qg, kg, qk, el, states, do):
    s = u.shape[0]
    c = GDN_CHUNK
    n = s // c
    hw = DN_HEADS * LANE

    def body(u_ref, w_ref, qg_ref, kg_ref, qk_ref, el_ref, st_ref, do_ref,
             du_ref, dw_ref, dqg_ref, dkg_ref, dqk_ref, del_ref, ds_sc):
        @pl.when(pl.program_id(0) == 0)
        def _():
            ds_sc[...] = jnp.zeros_like(ds_sc)

        heads = range(DN_HEADS)
        sl = [slice(h * LANE, (h + 1) * LANE) for h in heads]
        st = [st_ref[sl[h], :] for h in heads]
        ds = [ds_sc[h] for h in heads]
        do = [do_ref[:, sl[h]] for h in heads]
        v_new = [u_ref[:, sl[h]] - _hdot(w_ref[:, sl[h]], st[h]) for h in heads]
        dv_new = [_hdot(qk_ref[:, sl[h]], do[h], _TN) + _hdot(kg_ref[:, sl[h]], ds[h]) for h in heads]
        first_row = lax.broadcasted_iota(jnp.int32, (8, LANE), 0) == 0
        for h in heads:
            du_ref[:, sl[h]] = dv_new[h]
            dw_ref[:, sl[h]] = -_hdot(dv_new[h], st[h], _NT)
            dqg_ref[:, sl[h]] = _hdot(do[h], st[h], _NT)
            dqk_ref[:, sl[h]] = _hdot(do[h], v_new[h], _NT)
            dkg_ref[:, sl[h]] = _hdot(v_new[h], ds[h], _NT)
            del_ref[:, sl[h]] = jnp.where(first_row, jnp.sum(st[h] * ds[h], axis=0, keepdims=True), 0.0)
        ds_new = [_hdot(qg_ref[:, sl[h]], do[h], _TN) + ds[h] * el_ref[0:1, sl[h]] - _hdot(w_ref[:, sl[h]], dv_new[h], _TN)
                  for h in heads]
        for h in heads:
            ds_sc[h] = ds_new[h]

    row = pl.BlockSpec((c, hw), lambda i: (n - 1 - i, 0))
    small = pl.BlockSpec((8, hw), lambda i: (n - 1 - i, 0))
    return pl.pallas_call(
        body, out_shape=[jax.ShapeDtypeStruct((s, hw), F32)] * 5 + [jax.ShapeDtypeStruct((n * 8, hw), F32)], grid=(n,),
        in_specs=[row] * 5 + [small, pl.BlockSpec((None, hw, LANE), lambda i: (n - 1 - i, 0, 0)), row],
        out_specs=[row] * 5 + [small], scratch_shapes=[pltpu.VMEM((DN_HEADS, LANE, LANE), F32)],
        compiler_params=_cparams("arbitrary"), name="gdn_scan_bwd")(u, w, qg, kg, qk, el, states, do)


@jax.custom_vjp
def gdn_scan(u, w, qg, kg, qk, el):
    return _gdn_scan_fwd(u, w, qg, kg, qk, el)[0]


def _gdn_scan_vfwd(*args):
    o, states = _gdn_scan_fwd(*args)
    return o, args + (states,)


gdn_scan.defvjp(_gdn_scan_vfwd, lambda saved, do: tuple(_gdn_scan_bwd(*saved, do)))


def _loss_call(y, t):
    s, d = y.shape
    br = min(ROW_BLOCK_NARROW, s)
    n = s // br

    def body(y_ref, t_ref, loss_ref, dy_ref, acc):
        i = pl.program_id(0)

        @pl.when(i == 0)
        def _():
            acc[...] = jnp.zeros_like(acc)

        e = y_ref[...] - t_ref[...]
        dy_ref[...] = e / d
        acc[...] += jnp.sum(e * e, axis=0, keepdims=True)

        @pl.when(i == n - 1)
        def _():
            loss_ref[...] = jnp.broadcast_to(jnp.sum(acc[...], axis=1, keepdims=True) * (0.5 / d), loss_ref.shape)

    row = pl.BlockSpec((br, d), lambda i: (i, 0))
    return pl.pallas_call(
        body, out_shape=[jax.ShapeDtypeStruct((1, LANE), F32), jax.ShapeDtypeStruct((s, d), F32)], grid=(n,),
        in_specs=[row, row], out_specs=[pl.BlockSpec((1, LANE), lambda i: (0, 0)), row],
        scratch_shapes=[pltpu.VMEM((1, d), F32)], compiler_params=_cparams("arbitrary"), name="loss_head")(y, t)


@jax.custom_vjp
def loss_head(y, t):
    return _loss_call(y, t)[0][0, 0]


def _loss_head_fwd(y, t):
    loss, dy = _loss_call(y, t)
    return loss[0, 0], (dy,)


loss_head.defvjp(_loss_head_fwd, lambda saved, g: (saved[0] * g, -saved[0] * g))


IN_OFF = {}
_o = 0
for _name, _size in (('q_lat', 384), ('kv_lat', 320), ('z_a', 512), ('dn_qkv', 1536), ('dn_ab', 8), ('z_b', 512),
                     ('dil_qkv', 4608), ('z_c', 512), ('gate', 3072)):
    IN_OFF[_name] = (_o, _o + _size)
    _o += _size
IN_WIDTH = _o
N_CHIPS = 4


def _cols(w, name):
    a, b = IN_OFF[name]
    return w[:, a:b]


def _pad_cols(w, to):
    return jnp.concatenate([w, jnp.zeros((w.shape[0], to - w.shape[1]), w.dtype)], axis=1)


def _pad_row(v, to=None):
    v = v.reshape(1, -1)
    return v if to is None or v.shape[1] == to else _pad_cols(v, to)


def _shard_cols(pieces, a, b):
    wsh = pieces[0].shape[1]
    parts = [pieces[j][:, max(a, j * wsh) - j * wsh:min(b, (j + 1) * wsh) - j * wsh]
             for j in range(len(pieces)) if max(a, j * wsh) < min(b, (j + 1) * wsh)]
    return parts[0] if len(parts) == 1 else jnp.concatenate(parts, axis=1)


def _win_groups_impl(w_in4):
    out = []
    for l in range(w_in4.shape[1]):
        pieces = [w_in4[j, l] for j in range(w_in4.shape[0])]
        cols = lambda name: _shard_cols(pieces, *IN_OFF[name])
        out.append((_pad_cols(jnp.concatenate([cols('q_lat'), cols('kv_lat')], axis=1), 768),
                    jnp.concatenate([cols('z_a'), cols('z_b'), cols('z_c')], axis=1),
                    cols('dn_qkv'), _pad_cols(cols('dn_ab'), LANE), cols('dil_qkv'), cols('gate')))
    return tuple(out)


@jax.custom_vjp
def win_groups(w_in4):
    return _win_groups_impl(w_in4)


def _win_groups_bwd(_, cts):
    n_chip, depth = N_CHIPS, len(cts)
    wsh = IN_WIDTH // n_chip
    per_layer = []
    for l in range(depth):
        dmla, dz, dgdn, dab, ddil, dgate = cts[l]
        full = jnp.concatenate([dmla[:, :704], dz[:, :512], dgdn, dab[:, :8], dz[:, 512:1024], ddil, dz[:, 1024:], dgate], axis=1)
        per_layer.append([full[:, j * wsh:(j + 1) * wsh] for j in range(n_chip)])
    return (jnp.stack([jnp.stack([per_layer[l][j] for l in range(depth)]) for j in range(n_chip)]),)


win_groups.defvjp(lambda w: (_win_groups_impl(w), None), _win_groups_bwd)


def _layer(x, p, tabs):
    w_mla, w_z, w_gdn, w_ab, w_dil, w_gate = p['w_in_groups']
    h, = rowwise(f_rms_full, "rms_in", [x], [_pad_row(p['norm_g'])], [(D_MODEL, BF16)], br=ROW_BLOCK_NARROW)
    mla_in, z, dn_qkv, dn_ab, dil_qkv, gate = matmul_shared(h, (w_mla, w_z, w_gdn, w_ab, w_dil, w_gate))

    qn, kvn, kpe = rowwise(f_mla1, "mla_norm", [mla_in], [_pad_row(p['mla_q_a_norm_g']), _pad_row(p['mla_kv_a_norm_g'])],
                           [(MLA_Q_RANK, BF16), (MLA_KV_RANK, BF16), (LANE, F32)], br=ROW_BLOCK_NARROW)
    wq = p['mla_w_q_b']
    wq_pad = jnp.concatenate(
        [wq[:, hh * MLA_QK:hh * MLA_QK + MLA_NOPE] for hh in range(MLA_HEADS)]
        + [_pad_cols(wq[:, hh * MLA_QK + MLA_NOPE:(hh + 1) * MLA_QK], LANE) for hh in range(MLA_HEADS)], axis=1)
    q = matmul(qn, wq_pad)
    kv = matmul(kvn, p['mla_w_kv_b'])
    gq, gk = p['mla_q_norm_g'], p['mla_k_norm_g']
    q_att, k_att, v_att = rowwise(
        f_mla2, "mla_qk", [q, kv, kpe, tabs['cos_r'], tabs['sin_r']],
        [_pad_row(gq[:MLA_NOPE]), _pad_row(gq[MLA_NOPE:], LANE), _pad_row(gk[:MLA_NOPE]), _pad_row(gk[MLA_NOPE:], LANE)],
        [(MLA_HEADS * MLA_DQK, BF16), (MLA_HEADS * MLA_DQK, BF16), (MLA_HEADS * HEAD, BF16)], nograd=(3, 4), br=ROW_BLOCK_NARROW)
    y_a = mla_attention(q_att, k_att, v_att)

    qkv_n = dn_conv(dn_qkv, p['dn_conv_w'])
    gb, = rowwise(f_gates, "dn_gates", [dn_ab], [_pad_row(p['dn_a_log'], LANE), _pad_row(p['dn_dt_bias'], LANE)], [(LANE, F32)],
                  br=ROW_BLOCK_NARROW)
    o_b = gdn_scan(*gdn_prep(qkv_n, gb))
    y_b, = rowwise(f_headnorm, "dn_out_norm", [o_b], [_pad_row(p['dn_out_norm_g'])], [(DN_HEADS * DN_DIM, F32)],
                   br=ROW_BLOCK_NARROW)

    qkv_d = rowwise(f_dil, "dil_qk", [dil_qkv, tabs['cos_h'], tabs['sin_h']],
                    [_pad_row(p['dil_q_norm_g']), _pad_row(p['dil_k_norm_g'])],
                    [(GROUP_W, BF16)] * (3 * DIL_GROUPS), nograd=(1, 2), out_dil=[d for d in DIL_DILATIONS for _ in range(3)])
    outs, lses = [], []
    for gi, d in enumerate(DIL_DILATIONS):
        o_g, l_g = dilated_group(*qkv_d[3 * gi:3 * gi + 3], d)
        outs.append(o_g)
        lses.append(l_g)
    y_c, = rowwise(f_comb, "dil_comb", outs + lses, [], [(GROUP_W, F32)], row_dil=list(DIL_DILATIONS) * 2, br=ROW_BLOCK_NARROW)

    ys = rowwise(f_merge1, "merge_silu", [y_a, y_b, y_c, z], [], [(BRANCH_W, BF16)] * 3, br=ROW_BLOCK_NARROW)
    bo = [matmul(ys[b], p['w_branch'][b]) for b in range(3)]
    mixed, = rowwise(f_merge2, "merge_gate", [gate] + bo, [], [(D_MODEL, BF16)])
    return matmul(mixed, p['w_out'], res=x)


def _rope_tables(pos, dim):
    inv_freq = 1.0 / (ROPE_THETA ** (jnp.arange(0, dim, 2, dtype=F32) / dim))
    ang = pos.astype(F32)[:, None] * inv_freq
    return jnp.cos(ang), jnp.sin(ang)


def _tables(pos):
    cr, sr = _rope_tables(pos, MLA_ROPE)
    ch, sh = _rope_tables(pos, HEAD)
    zero = jnp.zeros((pos.shape[0], LANE - MLA_ROPE), F32)
    return {'cos_r': jnp.concatenate([cr, cr, zero], axis=1), 'sin_r': jnp.concatenate([sr, sr, zero], axis=1),
            'cos_h': jnp.concatenate([ch, ch], axis=1), 'sin_h': jnp.concatenate([-sh, sh], axis=1)}


def _local_loss(w_in4, mats, conv_w, small, x, target, tabs):
    groups = win_groups(w_in4)
    for l in range(DEPTH):
        p = {k: v[l] for k, v in mats.items()}
        p.update({k: v[l] for k, v in small.items()})
        p['dn_conv_w'] = conv_w[l]
        p['w_in_groups'] = groups[l]
        x = _layer(x, p, tabs)
    return loss_head(x, target)


def _pack(arrays, dtype, row_tile):
    flat = jnp.concatenate([a.astype(dtype).reshape(-1) for a in arrays])
    rows = -(-flat.shape[0] // (LANE * row_tile)) * row_tile
    flat = jnp.concatenate([flat, jnp.zeros((rows * LANE - flat.shape[0],), dtype)])
    return flat.reshape(rows, LANE)


def _unpack_impl(buf, shapes):
    flat = buf.reshape(-1)
    out, off = [], 0
    for shp in shapes:
        n = math.prod(shp)
        out.append(flat[off:off + n].reshape(shp))
        off += n
    return tuple(out)


@functools.partial(jax.custom_vjp, nondiff_argnums=(1, 2, 3))
def _unpack_p(buf, shapes, dtype_name, rows):
    return _unpack_impl(buf, shapes)


_unpack_p.defvjp(lambda buf, shapes, dtype_name, rows: (_unpack_impl(buf, shapes), None),
                 lambda shapes, dtype_name, rows, _, cts: (_pack(cts, jnp.dtype(dtype_name), rows),))


def _unpack(buf, shapes):
    return _unpack_p(buf, tuple(shapes), jnp.dtype(buf.dtype).name, buf.shape[0])


def _full_from_chips(buf4, shard_shapes):
    per_chip = [_unpack(buf4[j], tuple(shard_shapes)) for j in range(4)]
    return {name: jnp.concatenate([per_chip[j][i] for j in range(4)], axis=axis) for i, (name, axis) in enumerate(MATS)}


_HBM = pl.BlockSpec(memory_space=pltpu.HBM)
_VMEM = pl.BlockSpec(memory_space=pltpu.VMEM)


def _chip_peers(x, y):
    return [(1 - x, y), (x, 1 - y), (1 - x, 1 - y)]


def chip_all_to_all(arrays):
    n = len(arrays)

    def body(*refs):
        in_refs, out_refs = refs[:n], refs[n:2 * n]
        send_sems, recv_sems, local_sems = refs[2 * n:]
        x, y, c = lax.axis_index("x"), lax.axis_index("y"), lax.axis_index("c")
        me = 2 * x + y
        peers = _chip_peers(x, y)
        local, sends = [], []
        for a, (i_ref, o_ref) in enumerate(zip(in_refs, out_refs)):
            local.append(pltpu.make_async_copy(i_ref.at[me], o_ref.at[me], local_sems.at[a]))
            local[-1].start()
            for k, (px, py) in enumerate(peers):
                sends.append(pltpu.make_async_remote_copy(
                    src_ref=i_ref.at[2 * px + py], dst_ref=o_ref.at[me], send_sem=send_sems.at[3 * a + k],
                    recv_sem=recv_sems.at[3 * a + k], device_id=(px, py, c), device_id_type=MESH))
                sends[-1].start()
        for a, (i_ref, o_ref) in enumerate(zip(in_refs, out_refs)):
            for k, (px, py) in enumerate(peers):
                pltpu.make_async_remote_copy(
                    src_ref=i_ref.at[me], dst_ref=o_ref.at[2 * px + py], send_sem=send_sems.at[3 * a + k],
                    recv_sem=recv_sems.at[3 * a + k], device_id=(px, py, c), device_id_type=MESH).wait_recv()
        for cp in sends:
            cp.wait_send()
        for cp in local:
            cp.wait()

    return pl.pallas_call(
        body, out_shape=[jax.ShapeDtypeStruct(a.shape, a.dtype) for a in arrays], in_specs=[_HBM] * n, out_specs=[_HBM] * n,
        scratch_shapes=[pltpu.SemaphoreType.DMA((3 * n,)), pltpu.SemaphoreType.DMA((3 * n,)), pltpu.SemaphoreType.DMA((n,))],
        name="chip_all_to_all")(*arrays)


def chip_all_gather(shards):
    n = len(shards)

    def body(*refs):
        in_refs, out_refs = refs[:n], refs[n:2 * n]
        send_sems, recv_sems = refs[2 * n:]
        x, y, c = lax.axis_index("x"), lax.axis_index("y"), lax.axis_index("c")
        me = 2 * x + y
        peers = _chip_peers(x, y)
        sends = []
        for a, (i_ref, o_ref) in enumerate(zip(in_refs, out_refs)):
            for k, (px, py) in enumerate(peers):
                sends.append(pltpu.make_async_remote_copy(
                    src_ref=i_ref.at[c], dst_ref=o_ref.at[me, c], send_sem=send_sems.at[6 * a + k],
                    recv_sem=recv_sems.at[6 * a + k], device_id=(px, py, c), device_id_type=MESH))
                sends[-1].start()
        for a, (i_ref, o_ref) in enumerate(zip(in_refs, out_refs)):
            for k, (px, py) in enumerate(peers):
                landed = o_ref.at[2 * px + py, c]
                pltpu.make_async_remote_copy(src_ref=i_ref.at[c], dst_ref=landed, send_sem=send_sems.at[6 * a + k],
                                             recv_sem=recv_sems.at[6 * a + k], device_id=(px, py, c), device_id_type=MESH).wait_recv()
                sends.append(pltpu.make_async_remote_copy(
                    src_ref=landed, dst_ref=landed, send_sem=send_sems.at[6 * a + 3 + k], recv_sem=recv_sems.at[6 * a + 3 + k],
                    device_id=(x, y, 1 - c), device_id_type=MESH))
                sends[-1].start()
        for a, (i_ref, o_ref) in enumerate(zip(in_refs, out_refs)):
            for k, (px, py) in enumerate(peers):
                other = o_ref.at[2 * px + py, 1 - c]
                pltpu.make_async_remote_copy(src_ref=other, dst_ref=other, send_sem=send_sems.at[6 * a + 3 + k],
                                             recv_sem=recv_sems.at[6 * a + 3 + k], device_id=(x, y, 1 - c),
                                             device_id_type=MESH).wait_recv()
        for cp in sends:
            cp.wait_send()

    me = 2 * lax.axis_index("x") + lax.axis_index("y")
    outs = pl.pallas_call(
        body, out_shape=[jax.ShapeDtypeStruct((4,) + a.shape, a.dtype) for a in shards], in_specs=[_HBM] * n, out_specs=[_HBM] * n,
        scratch_shapes=[pltpu.SemaphoreType.DMA((6 * n,)), pltpu.SemaphoreType.DMA((6 * n,))],
        name="chip_all_gather")(*shards)
    return [lax.dynamic_update_index_in_dim(o, a, me, 0) for o, a in zip(outs, shards)]


def sibling_swap(arrays):
    n = len(arrays)

    def body(*refs):
        in_refs, out_refs = refs[:n], refs[n:2 * n]
        send_sems, recv_sems = refs[2 * n:]
        x, y, c = lax.axis_index("x"), lax.axis_index("y"), lax.axis_index("c")
        cps = [pltpu.make_async_remote_copy(src_ref=i_ref, dst_ref=o_ref, send_sem=send_sems.at[a], recv_sem=recv_sems.at[a],
                                            device_id=(x, y, 1 - c), device_id_type=MESH)
               for a, (i_ref, o_ref) in enumerate(zip(in_refs, out_refs))]
        for cp in cps:
            cp.start()
        for cp in cps:
            cp.wait()

    return pl.pallas_call(
        body, out_shape=[jax.ShapeDtypeStruct(a.shape, a.dtype) for a in arrays], in_specs=[_HBM] * n, out_specs=[_HBM] * n,
        scratch_shapes=[pltpu.SemaphoreType.DMA((n,)), pltpu.SemaphoreType.DMA((n,))], name="sibling_swap")(*arrays)


def sibling_all_gather(halves):
    n = len(halves)

    def body(*refs):
        in_refs, out_refs = refs[:n], refs[n:2 * n]
        send_sems, recv_sems = refs[2 * n:]
        x, y, c = lax.axis_index("x"), lax.axis_index("y"), lax.axis_index("c")
        sends = [pltpu.make_async_remote_copy(src_ref=i_ref, dst_ref=o_ref.at[c], send_sem=send_sems.at[a], recv_sem=recv_sems.at[a],
                                              device_id=(x, y, 1 - c), device_id_type=MESH)
                 for a, (i_ref, o_ref) in enumerate(zip(in_refs, out_refs))]
        for cp in sends:
            cp.start()
        for a, (i_ref, o_ref) in enumerate(zip(in_refs, out_refs)):
            pltpu.make_async_remote_copy(src_ref=i_ref, dst_ref=o_ref.at[1 - c], send_sem=send_sems.at[a], recv_sem=recv_sems.at[a],
                                         device_id=(x, y, 1 - c), device_id_type=MESH).wait_recv()
        for cp in sends:
            cp.wait_send()

    outs = pl.pallas_call(
        body, out_shape=[jax.ShapeDtypeStruct((2,) + a.shape, a.dtype) for a in halves], in_specs=[_HBM] * n, out_specs=[_HBM] * n,
        scratch_shapes=[pltpu.SemaphoreType.DMA((n,)), pltpu.SemaphoreType.DMA((n,))], name="sibling_all_gather")(*halves)
    return [lax.dynamic_update_index_in_dim(o, a, lax.axis_index("c"), 0) for o, a in zip(outs, halves)]


def all_gather8(v, name):
    def body(v_ref, out_ref, send_sems, recv_sems):
        x, y, c = lax.axis_index("x"), lax.axis_index("y"), lax.axis_index("c")
        out_ref[4 * x + 2 * y + c] = v_ref[...]

        def peer(k):
            return (x ^ (k >> 2), y ^ ((k >> 1) & 1), c ^ (k & 1))

        sends = [pltpu.make_async_remote_copy(src_ref=v_ref, dst_ref=out_ref.at[4 * x + 2 * y + c], send_sem=send_sems.at[k - 1],
                                              recv_sem=recv_sems.at[k - 1], device_id=peer(k), device_id_type=MESH)
                 for k in range(1, 8)]
        for cp in sends:
            cp.start()
        for k in range(1, 8):
            px, py, pc = peer(k)
            pltpu.make_async_remote_copy(src_ref=v_ref, dst_ref=out_ref.at[4 * px + 2 * py + pc], send_sem=send_sems.at[k - 1],
                                         recv_sem=recv_sems.at[k - 1], device_id=peer(k), device_id_type=MESH).wait_recv()
        for cp in sends:
            cp.wait_send()

    return pl.pallas_call(
        body, out_shape=jax.ShapeDtypeStruct((8,) + v.shape, v.dtype), in_specs=[_VMEM], out_specs=_VMEM,
        scratch_shapes=[pltpu.SemaphoreType.DMA((7,)), pltpu.SemaphoreType.DMA((7,))], name=name)(v)


def pair_add(a, b, row_tile, name):
    rows, width = a.shape

    def body(a_ref, b_ref, o_ref):
        o_ref[...] = (a_ref[...].astype(F32) + b_ref[...].astype(F32)).astype(o_ref.dtype)

    spec = pl.BlockSpec((row_tile, width), lambda i: (i, 0))
    return pl.pallas_call(body, out_shape=jax.ShapeDtypeStruct(a.shape, a.dtype), grid=(rows // row_tile,),
                          in_specs=[spec, spec], out_specs=spec, compiler_params=_cparams("parallel"), name=name)(a, b)


def sum_blocks(blocks, row_tile, name):
    n, rows, width = blocks.shape

    def body(b_ref, o_ref):
        acc = b_ref[0].astype(F32)
        for j in range(1, n):
            acc = acc + b_ref[j].astype(F32)
        o_ref[...] = acc

    return pl.pallas_call(
        body, out_shape=jax.ShapeDtypeStruct((rows, width), F32), grid=(rows // row_tile,),
        in_specs=[pl.BlockSpec((n, row_tile, width), lambda i: (0, i, 0))],
        out_specs=pl.BlockSpec((row_tile, width), lambda i: (i, 0)), compiler_params=_cparams("parallel"), name=name)(blocks)


def adamw(g_parts, w, m, v, row_tile, name):
    npart = len(g_parts)

    def body(*refs):
        g = refs[0][...]
        for r in refs[1:npart]:
            g = g + r[...]
        w_ref, m_ref, v_ref, g_out, d_out, m_out, v_out = refs[npart:]
        m_new = ADAM_B1 * m_ref[...] + (1.0 - ADAM_B1) * g
        v_new = ADAM_B2 * v_ref[...] + (1.0 - ADAM_B2) * (g * g)
        m_hat = m_new / (1.0 - ADAM_B1 ** ADAM_STEP)
        v_hat = v_new / (1.0 - ADAM_B2 ** ADAM_STEP)
        g_out[...] = g
        d_out[...] = -ADAM_LR * (m_hat / (jnp.sqrt(v_hat) + ADAM_EPS) + ADAM_WD * w_ref[...])
        m_out[...] = m_new
        v_out[...] = v_new

    rows, width = w.shape
    spec = pl.BlockSpec((row_tile, width), lambda i: (i, 0))
    return pl.pallas_call(
        body, out_shape=[jax.ShapeDtypeStruct(w.shape, F32)] * 4, grid=(rows // row_tile,),
        in_specs=[spec] * (npart + 3), out_specs=[spec] * 4, compiler_params=_cparams("parallel"), name=name)(*g_parts, w, m, v)


def kernel(x, positions, norm_g, w_in, mla_q_a_norm_g, mla_w_q_b, mla_kv_a_norm_g, mla_w_kv_b, mla_q_norm_g, mla_k_norm_g, dn_conv_w, dn_a_log, dn_dt_bias, dn_out_norm_g, dil_q_norm_g, dil_k_norm_g, w_branch, w_out, loss_target, m_norm_g, m_w_in, m_mla_q_a_norm_g, m_mla_w_q_b, m_mla_kv_a_norm_g, m_mla_w_kv_b, m_mla_q_norm_g, m_mla_k_norm_g, m_dn_conv_w, m_dn_a_log, m_dn_dt_bias, m_dn_out_norm_g, m_dil_q_norm_g, m_dil_k_norm_g, m_w_branch, m_w_out, v_norm_g, v_w_in, v_mla_q_a_norm_g, v_mla_w_q_b, v_mla_kv_a_norm_g, v_mla_w_kv_b, v_mla_q_norm_g, v_mla_k_norm_g, v_dn_conv_w, v_dn_a_log, v_dn_dt_bias, v_dn_out_norm_g, v_dil_q_norm_g, v_dil_k_norm_g, v_w_branch, v_w_out):
    w = dict(norm_g=norm_g, w_in=w_in, mla_q_a_norm_g=mla_q_a_norm_g, mla_w_q_b=mla_w_q_b, mla_kv_a_norm_g=mla_kv_a_norm_g,
             mla_w_kv_b=mla_w_kv_b, mla_q_norm_g=mla_q_norm_g, mla_k_norm_g=mla_k_norm_g, dn_conv_w=dn_conv_w, dn_a_log=dn_a_log,
             dn_dt_bias=dn_dt_bias, dn_out_norm_g=dn_out_norm_g, dil_q_norm_g=dil_q_norm_g, dil_k_norm_g=dil_k_norm_g,
             w_branch=w_branch, w_out=w_out)
    m = dict(norm_g=m_norm_g, w_in=m_w_in, mla_q_a_norm_g=m_mla_q_a_norm_g, mla_w_q_b=m_mla_w_q_b, mla_kv_a_norm_g=m_mla_kv_a_norm_g,
             mla_w_kv_b=m_mla_w_kv_b, mla_q_norm_g=m_mla_q_norm_g, mla_k_norm_g=m_mla_k_norm_g, dn_conv_w=m_dn_conv_w,
             dn_a_log=m_dn_a_log, dn_dt_bias=m_dn_dt_bias, dn_out_norm_g=m_dn_out_norm_g, dil_q_norm_g=m_dil_q_norm_g,
             dil_k_norm_g=m_dil_k_norm_g, w_branch=m_w_branch, w_out=m_w_out)
    v = dict(norm_g=v_norm_g, w_in=v_w_in, mla_q_a_norm_g=v_mla_q_a_norm_g, mla_w_q_b=v_mla_w_q_b, mla_kv_a_norm_g=v_mla_kv_a_norm_g,
             mla_w_kv_b=v_mla_w_kv_b, mla_q_norm_g=v_mla_q_norm_g, mla_k_norm_g=v_mla_k_norm_g, dn_conv_w=v_dn_conv_w,
             dn_a_log=v_dn_a_log, dn_dt_bias=v_dn_dt_bias, dn_out_norm_g=v_dn_out_norm_g, dil_q_norm_g=v_dil_q_norm_g,
             dil_k_norm_g=v_dil_k_norm_g, w_branch=v_w_branch, w_out=v_w_out)
    chip = 2 * lax.axis_index("x") + lax.axis_index("y")
    mat_names = [n for n, _ in MATS]
    mat_shapes = tuple(w[n].shape for n in mat_names)
    conv_shard = dn_conv_w.shape
    win_shape = w_in.shape
    win_rows = (win_shape[0] * win_shape[1], win_shape[2])

    mats_sh = _pack([w[n] for n in mat_names], BF16, 2 * MAT_ROWS)
    mat_rows = mats_sh.shape[0]
    w_in4, mats4 = chip_all_gather([w_in.astype(BF16), mats_sh.reshape(2, mat_rows // 2, LANE)])
    mats4 = mats4.reshape(N_CHIPS, mat_rows, LANE)
    conv8 = all_gather8(_pack([dn_conv_w], F32, 8), "gather_conv_w")
    conv_full = jnp.concatenate([_unpack(conv8[2 * j], (conv_shard,))[0] for j in range(4)], axis=2)
    small = {n: w[n] for n in SMALL}
    tabs = _tables(positions[0])

    def loss_fn(w_in4, mats4, conv_full, small, xs):
        return _local_loss(w_in4, _full_from_chips(mats4, mat_shapes), conv_full, small, xs, loss_target[0], tabs)

    loss, (g_win4, g_mats4, g_conv, g_small, g_x) = jax.value_and_grad(loss_fn, argnums=(0, 1, 2, 3, 4))(
        w_in4, mats4, conv_full, small, x[0])
    loss = lax.psum(loss, ("x", "y", "c"))

    c_idx = lax.axis_index("c")
    half_win = (N_CHIPS * win_shape[1], win_shape[2])
    half_mats = (N_CHIPS * (mat_rows // 2), LANE)
    g_mats4 = g_mats4.reshape(N_CHIPS, 2, mat_rows // 2, LANE)
    pick = lambda g, i, shape: lax.dynamic_index_in_dim(g, i, axis=1, keepdims=False).reshape(shape)
    from_sib = sibling_swap([pick(g_win4, 1 - c_idx, half_win), pick(g_mats4, 1 - c_idx, half_mats)])
    s_win = pair_add(pick(g_win4, c_idx, half_win), from_sib[0], WIN_ROWS, "pair_add_w_in")
    s_mats = pair_add(pick(g_mats4, c_idx, half_mats), from_sib[1], MAT_ROWS, "pair_add_mats")
    r_win, r_mats = chip_all_to_all([s_win.reshape(N_CHIPS, win_shape[1], win_shape[2]),
                                     s_mats.reshape(N_CHIPS, mat_rows // 2, LANE)])
    g_win, g_mats = sibling_all_gather([sum_blocks(r_win, WIN_ROWS, "sum_chip_pieces_w_in"),
                                        sum_blocks(r_mats, MAT_ROWS, "sum_chip_pieces_mats")])
    res_win = adamw([g_win.reshape(win_rows)], *[d['w_in'].reshape(win_rows) for d in (w, m, v)], WIN_ROWS, "adamw_w_in")
    packed = [_pack([d[n] for n in mat_names], F32, 2 * MAT_ROWS) for d in (w, m, v)]
    res_big = [dict(zip(mat_names, _unpack(r, mat_shapes)))
               for r in adamw([g_mats.reshape(mat_rows, LANE)], *packed, MAT_ROWS, "adamw_mats")]
    for k in range(4):
        res_big[k]['w_in'] = res_win[k].reshape(win_shape)

    small_shapes = tuple(w[n].shape for n in SMALL) + (g_conv.shape,)
    g_all = sum_blocks(all_gather8(_pack([g_small[n] for n in SMALL] + [g_conv], F32, 8), "gather_small_grads"), 8, "sum_small")
    g_list = list(_unpack(g_all, small_shapes))
    g_list[-1] = lax.dynamic_slice_in_dim(g_list[-1], chip * conv_shard[2], conv_shard[2], axis=2)
    small_names = list(SMALL) + ['dn_conv_w']
    packed_s = [_pack([d[n] for n in small_names], F32, 8) for d in (w, m, v)]
    shapes_s = tuple(w[n].shape for n in small_names)
    res_small = [dict(zip(small_names, _unpack(r, shapes_s))) for r in adamw([_pack(g_list, F32, 8)], *packed_s, 8, "adamw_small")]

    outs = [loss, g_x[None]]
    for k in range(4):
        outs += [res_big[k][n] if n in res_big[k] else res_small[k][n] for n in WEIGHTS]
    return tuple(outs)
```

```python
import functools
import math

import jax
import jax.numpy as jnp
from jax import lax
from jax.experimental import pallas as pl
from jax.experimental.pallas import tpu as pltpu

F32 = jnp.float32
BF16 = jnp.bfloat16
HI = lax.Precision.HIGHEST
MESH = pl.DeviceIdType.MESH

LANE = 128
VMEM_LIMIT = 48 * 1024 * 1024
ROW_BLOCK = 256
ROW_BLOCK_NARROW = 512
MM_TM, MM_TN, MM_TK = 1024, 1024, 1024
MAT_ROWS = 1664
WIN_ROWS = 128

RMS_EPS = 1e-6
ROPE_THETA = 10000.0
D_MODEL = 1024
DEPTH = 2
MLA_HEADS = 4
MLA_NOPE = 128
MLA_ROPE = 64
MLA_QK = MLA_NOPE + MLA_ROPE
MLA_Q_RANK = 384
MLA_KV_RANK = 256
DN_HEADS = 4
DN_DIM = 128
DN_CONV = 4
GDN_CHUNK = 128
DIL_WINDOWS = (128, 512, 2048)
DIL_DILATIONS = (1, 4, 16)
DIL_GROUPS = 3
DIL_BLOCK = 128
HEAD = 128
BRANCH_W = 512

ADAM_LR = 0.001
ADAM_B1 = 0.9
ADAM_B2 = 0.999
ADAM_EPS = 1e-08
ADAM_WD = 0.01
ADAM_STEP = 10

WEIGHTS = ['norm_g', 'w_in', 'mla_q_a_norm_g', 'mla_w_q_b', 'mla_kv_a_norm_g', 'mla_w_kv_b', 'mla_q_norm_g',
           'mla_k_norm_g', 'dn_conv_w', 'dn_a_log', 'dn_dt_bias', 'dn_out_norm_g', 'dil_q_norm_g', 'dil_k_norm_g',
           'w_branch', 'w_out']
MATS = (('mla_w_q_b', 2), ('mla_w_kv_b', 2), ('w_branch', 3), ('w_out', 1))
SMALL = ('norm_g', 'mla_q_a_norm_g', 'mla_kv_a_norm_g', 'mla_q_norm_g', 'mla_k_norm_g', 'dn_a_log', 'dn_dt_bias',
         'dn_out_norm_g', 'dil_q_norm_g', 'dil_k_norm_g')


def _cparams(*sem):
    return pltpu.CompilerParams(dimension_semantics=sem or None, vmem_limit_bytes=VMEM_LIMIT)


def _tile(dim, target):
    best = 0
    for t in range(LANE, min(dim, target) + 1, LANE):
        if dim % t == 0:
            best = t
    assert best, (dim, target)
    return best


def _mm(a, b, mode, out_dtype, res=None):
    if mode == 'nn':
        (m, k), (k2, n) = a.shape, b.shape
    elif mode == 'nt':
        (m, k), (n, k2) = a.shape, b.shape
    else:
        (k, m), (k2, n) = a.shape, b.shape
    assert k == k2, (a.shape, b.shape, mode)
    tm, tn, tk = _tile(m, MM_TM), _tile(n, MM_TN), _tile(k, MM_TK)
    nk = k // tk
    dims = {'nn': (((1,), (0,)), ((), ())), 'nt': (((1,), (1,)), ((), ())), 'tn': (((0,), (0,)), ((), ()))}[mode]

    def body(*refs):
        if res is None:
            a_ref, b_ref, o_ref = refs[:3]
        else:
            a_ref, b_ref, r_ref, o_ref = refs[:4]
        part = lax.dot_general(a_ref[...].astype(BF16), b_ref[...].astype(BF16), dims, preferred_element_type=F32)

        def finish(r):
            if res is not None:
                r = r + r_ref[...].astype(F32)
            o_ref[...] = r.astype(o_ref.dtype)

        if nk == 1:
            finish(part)
            return
        acc = refs[-1]
        kk = pl.program_id(2)

        @pl.when(kk == 0)
        def _():
            acc[...] = part

        @pl.when(jnp.logical_and(kk > 0, kk < nk - 1))
        def _():
            acc[...] += part

        @pl.when(kk == nk - 1)
        def _():
            finish(acc[...] + part)

    a_spec = pl.BlockSpec((tk, tm), lambda i, j, kk: (kk, i)) if mode == 'tn' else pl.BlockSpec((tm, tk), lambda i, j, kk: (i, kk))
    b_spec = pl.BlockSpec((tn, tk), lambda i, j, kk: (j, kk)) if mode == 'nt' else pl.BlockSpec((tk, tn), lambda i, j, kk: (kk, j))
    o_spec = pl.BlockSpec((tm, tn), lambda i, j, kk: (i, j))
    in_specs = [a_spec, b_spec] + ([o_spec] if res is not None else [])
    args = (a, b) + ((res,) if res is not None else ())
    return pl.pallas_call(
        body, out_shape=jax.ShapeDtypeStruct((m, n), out_dtype), grid=(m // tm, n // tn, nk),
        in_specs=in_specs, out_specs=o_spec, scratch_shapes=[pltpu.VMEM((tm, tn), F32)] if nk > 1 else [],
        compiler_params=_cparams("parallel", "parallel", "arbitrary"),
        name=f"mm_{mode}_{m}x{k}x{n}" + ("_res" if res is not None else ""))(*args)


def _make_matmul(out_dtype, with_res):
    @jax.custom_vjp
    def mm(a, b, *r):
        return _mm(a, b, 'nn', out_dtype, *r)

    def fwd(a, b, *r):
        return mm(a, b, *r), (a, b)

    def bwd(saved, g):
        a, b = saved
        da = _mm(g, b, 'nt', a.dtype)
        db = _mm(a, g, 'tn', b.dtype)
        return (da, db) + ((g,) if with_res else ())

    mm.defvjp(fwd, bwd)
    return mm


@jax.custom_vjp
def matmul_shared(a, bs):
    return tuple(_mm(a, b, 'nn', F32) for b in bs)


def _matmul_shared_bwd(saved, gs):
    a, bs = saved
    da = None
    for i, (g, b) in enumerate(zip(gs, bs)):
        da = _mm(g, b, 'nt', a.dtype if i == len(bs) - 1 else F32, da)
    return da, tuple(_mm(a, g, 'tn', b.dtype) for g, b in zip(gs, bs))


matmul_shared.defvjp(lambda a, bs: (matmul_shared(a, bs), (a, bs)), _matmul_shared_bwd)


def matmul(a, b, out_dtype=F32, res=None):
    if res is None:
        return _make_matmul(out_dtype, False)(a, b)
    return _make_matmul(out_dtype, True)(a, b, res)


def _chunks(ref):
    return [ref[:, c * LANE:(c + 1) * LANE].astype(F32) for c in range(ref.shape[-1] // LANE)]


def _store(ref, chunks):
    for c, ch in enumerate(chunks):
        ref[:, c * LANE:(c + 1) * LANE] = ch.astype(ref.dtype)


def _row_spec(width, br, d=1):
    return pl.BlockSpec((br // d, d * width), lambda i: (i, 0))


def _par_spec(width):
    return pl.BlockSpec((1, width), lambda i: (0, 0))


def _load_rows(ref, scratch, d):
    if d == 1:
        return _chunks(ref)
    n, width = ref.shape[0], ref.shape[1] // d
    for c in range(width // LANE):
        for r in range(d):
            lanes = slice(r * width + c * LANE, r * width + (c + 1) * LANE)
            scratch[c, pl.ds(r, n, stride=d), :] = ref[:, lanes].astype(F32)
    return [scratch[c] for c in range(width // LANE)]


def _store_rows(ref, scratch, d, chunks):
    if d == 1:
        return _store(ref, chunks)
    n, width = ref.shape[0], ref.shape[1] // d
    for c, ch in enumerate(chunks):
        scratch[c] = ch
        for r in range(d):
            lanes = slice(r * width + c * LANE, r * width + (c + 1) * LANE)
            ref[:, lanes] = scratch[c, pl.ds(r, n, stride=d), :].astype(ref.dtype)


def _view_scratch(widths_dils, br):
    return [pltpu.VMEM((w // LANE, br, LANE), F32) for w, d in widths_dils if d > 1]


def _with_scratch(dils, scratch_refs):
    it = iter(scratch_refs)
    return [next(it) if d > 1 else None for d in dils]


def _rw_fwd(f, name, rows, params, outs, br, row_dil, out_dil):
    s = rows[0].shape[0] * row_dil[0]
    br = min(br, s)
    nr, npar, nout = len(rows), len(params), len(outs)
    row_w = [r.shape[1] // d for r, d in zip(rows, row_dil)]

    def body(*refs):
        scr = refs[nr + npar + nout:]
        n_in = sum(d > 1 for d in row_dil)
        rc = [_load_rows(r, sc, d) for r, sc, d in zip(refs[:nr], _with_scratch(row_dil, scr[:n_in]), row_dil)]
        pc = [_chunks(p) for p in refs[nr:nr + npar]]
        res = f(rc, pc)
        for o_ref, sc, d, chs in zip(refs[nr + npar:nr + npar + nout], _with_scratch(out_dil, scr[n_in:]), out_dil, res):
            _store_rows(o_ref, sc, d, chs)

    return pl.pallas_call(
        body, out_shape=[jax.ShapeDtypeStruct((s // d, d * w), dt) for (w, dt), d in zip(outs, out_dil)], grid=(s // br,),
        in_specs=[_row_spec(w, br, d) for w, d in zip(row_w, row_dil)] + [_par_spec(p.shape[1]) for p in params],
        out_specs=[_row_spec(w, br, d) for (w, _), d in zip(outs, out_dil)],
        scratch_shapes=_view_scratch(zip(row_w, row_dil), br) + _view_scratch([(w, d) for (w, _), d in zip(outs, out_dil)], br),
        compiler_params=_cparams("parallel"), name=name + "_fwd")(*rows, *params)


def _rw_bwd(f, name, rows, params, cts, nograd, br, row_dil, out_dil):
    s = rows[0].shape[0] * row_dil[0]
    br = min(br, s)
    nr, npar, nct = len(rows), len(params), len(cts)
    grad_rows = [i for i in range(nr) if i not in nograd]
    row_w = [r.shape[1] // d for r, d in zip(rows, row_dil)]
    ct_w = [c.shape[1] // d for c, d in zip(cts, out_dil)]
    grad_dil = [row_dil[i] for i in grad_rows]

    def body(*refs):
        n_out = len(grad_rows) + npar
        out_refs = refs[nr + npar + nct:nr + npar + nct + n_out]
        scr = refs[nr + npar + nct + n_out:]
        n_in, n_ct = sum(d > 1 for d in row_dil), sum(d > 1 for d in out_dil)
        rc = [_load_rows(r, sc, d) for r, sc, d in zip(refs[:nr], _with_scratch(row_dil, scr[:n_in]), row_dil)]
        pc = [_chunks(p) for p in refs[nr:nr + npar]]
        ct = [_load_rows(c, sc, d) for c, sc, d in
              zip(refs[nr + npar:nr + npar + nct], _with_scratch(out_dil, scr[n_in:n_in + n_ct]), out_dil)]
        _, vjp = jax.vjp(f, rc, pc)
        drc, dpc = vjp(ct)
        for o_ref, sc, d, i in zip(out_refs[:len(grad_rows)], _with_scratch(grad_dil, scr[n_in + n_ct:]), grad_dil, grad_rows):
            _store_rows(o_ref, sc, d, drc[i])
        i0 = pl.program_id(0)
        for o_ref, chs in zip(out_refs[len(grad_rows):], dpc):
            @pl.when(i0 == 0)
            def _(o_ref=o_ref):
                o_ref[...] = jnp.zeros_like(o_ref)
            for c, ch in enumerate(chs):
                o_ref[:, c * LANE:(c + 1) * LANE] += ch

    out_shape = ([jax.ShapeDtypeStruct(rows[i].shape, rows[i].dtype) for i in grad_rows]
                 + [jax.ShapeDtypeStruct(p.shape, F32) for p in params])
    out_specs = [_row_spec(row_w[i], br, row_dil[i]) for i in grad_rows] + [_par_spec(p.shape[1]) for p in params]
    res = pl.pallas_call(
        body, out_shape=out_shape, grid=(s // br,),
        in_specs=([_row_spec(w, br, d) for w, d in zip(row_w, row_dil)] + [_par_spec(p.shape[1]) for p in params]
                  + [_row_spec(w, br, d) for w, d in zip(ct_w, out_dil)]),
        out_specs=out_specs,
        scratch_shapes=(_view_scratch(zip(row_w, row_dil), br) + _view_scratch(zip(ct_w, out_dil), br)
                        + _view_scratch([(row_w[i], row_dil[i]) for i in grad_rows], br)),
        compiler_params=_cparams("arbitrary"), name=name + "_bwd")(*rows, *params, *cts)
    drows = [None] * nr
    for o, i in zip(res[:len(grad_rows)], grad_rows):
        drows[i] = o
    for i in nograd:
        drows[i] = jnp.zeros_like(rows[i])
    return tuple(drows), tuple(res[len(grad_rows):])


def rowwise(f, name, rows, params, outs, nograd=(), br=ROW_BLOCK, row_dil=None, out_dil=None):
    row_dil = tuple(row_dil or [1] * len(rows))
    out_dil = tuple(out_dil or [1] * len(outs))

    @jax.custom_vjp
    def op(rows, params):
        return tuple(_rw_fwd(f, name, rows, params, outs, br, row_dil, out_dil))

    def fwd(rows, params):
        return op(rows, params), (rows, params)

    def bwd(saved, cts):
        rows, params = saved
        return _rw_bwd(f, name, rows, params, list(cts), nograd, br, row_dil, out_dil)

    op.defvjp(fwd, bwd)
    return op(tuple(rows), tuple(params))


def _lane_roll(x, s):
    w = x.shape[-1]

    @jax.custom_vjp
    def r(v):
        return pltpu.roll(v, s, 1)

    r.defvjp(lambda v: (r(v), None), lambda _, g: (pltpu.roll(g, (w - s) % w, 1),))
    return r(x)


def _lanes(x):
    return lax.broadcasted_iota(jnp.int32, x.shape, 1)


def _rms(x, g, n=LANE):
    return x * lax.rsqrt(jnp.sum(x * x, axis=-1, keepdims=True) / n + RMS_EPS) * g


def _rope128(x, cos, sin_signed):
    return x * cos + _lane_roll(x, 64) * sin_signed


def _rope64(x, cos, sin):
    lane = _lanes(x)
    rot = jnp.where(lane < 32, -_lane_roll(x, 96), jnp.where(lane < 64, _lane_roll(x, 32), 0.0))
    return x * cos + rot * sin


def _silu(z):
    return z * jax.nn.sigmoid(z)


def _softplus(x):
    return jnp.maximum(x, 0.0) + jnp.log1p(jnp.exp(-jnp.abs(x)))


def f_rms_full(rc, pc):
    x, g = rc[0], pc[0]
    width = len(x) * LANE
    ms = sum(jnp.sum(c * c, axis=-1, keepdims=True) for c in x) / width
    r = lax.rsqrt(ms + RMS_EPS)
    return [[c * r * gc for c, gc in zip(x, g)]]


def f_mla1(rc, pc):
    x = rc[0]
    qn = f_rms_full([x[0:3]], [pc[0]])[0]
    kvn = f_rms_full([x[3:5]], [pc[1]])[0]
    return [qn, kvn, [x[5]]]


def f_mla2(rc, pc):
    q, kv, kpe, cos, sin = rc[0], rc[1], rc[2][0], rc[3][0], rc[4][0]
    gqn, gqp, gkn, gkp = pc[0][0], pc[1][0], pc[2][0], pc[3][0]
    k_pe = _rope64(_rms(kpe, gkp, MLA_ROPE), cos, sin)
    q_att, k_att, v = [], [], []
    for h in range(MLA_HEADS):
        q_att += [_rms(q[h], gqn), _rope64(_rms(q[MLA_HEADS + h], gqp, MLA_ROPE), cos, sin)]
        k_att += [_rms(kv[2 * h], gkn), k_pe]
        v.append(kv[2 * h + 1])
    return [q_att, k_att, v]


def f_gates(rc, pc):
    x, a_log, dt_bias = rc[0][0], pc[0][0], pc[1][0]
    lane = _lanes(x)
    g = -jnp.exp(a_log) * _softplus(x + dt_bias)
    return [[jnp.where(lane < DN_HEADS, g, jnp.where(lane < 2 * DN_HEADS, jax.nn.sigmoid(x), 0.0))]]


def f_headnorm(rc, pc):
    return [[_rms(c, pc[0][0]) for c in rc[0]]]


def f_dil(rc, pc):
    x, cos, sin = rc[0], rc[1][0], rc[2][0]
    gq, gk = pc[0][0], pc[1][0]
    n = len(x) // 3
    q = [_rope128(_rms(c, gq), cos, sin) for c in x[:n]]
    k = [_rope128(_rms(c, gk), cos, sin) for c in x[n:2 * n]]
    v = list(x[2 * n:])
    per = n // DIL_GROUPS
    return [t[g * per:(g + 1) * per] for g in range(DIL_GROUPS) for t in (q, k, v)]


def f_comb(rc, pc):
    o, l = rc[:DIL_GROUPS], rc[DIL_GROUPS:]
    out = []
    for c in range(len(o[0])):
        m = functools.reduce(jnp.maximum, [lg[c] for lg in l])
        e = [jnp.exp(lg[c] - m) for lg in l]
        den = sum(e)
        out.append(sum(eg * og[c] for eg, og in zip(e, o)) / den)
    return [out]


def f_merge1(rc, pc):
    z = rc[3]
    n = len(rc[0])
    return [[y * _silu(z[b * n + c]) for c, y in enumerate(rc[b])] for b in range(3)]


def f_merge2(rc, pc):
    gate = rc[0]
    n = len(rc[1])
    return [[sum(jax.nn.sigmoid(gate[b * n + c]) * rc[1 + b][c] for b in range(3)) for c in range(n)]]


MLA_DQK = 2 * LANE
MLA_SCALE = MLA_QK ** -0.5


def _mla_attn_fwd(q, k, v):
    s = q.shape[0]
    h = q.shape[1] // MLA_DQK
    t = min(512, s)
    tk = 2 * t
    n, nkv = s // t, s // tk

    def body(q_ref, k_ref, v_ref, o_ref, lse_ref, m_sc, l_sc, acc_sc):
        qi, kj = pl.program_id(1), pl.program_id(2)
        last = qi // 2

        @pl.when(kj == 0)
        def _():
            m_sc[...] = jnp.full_like(m_sc, -jnp.inf)
            l_sc[...] = jnp.zeros_like(l_sc)
            acc_sc[...] = jnp.zeros_like(acc_sc)

        nsub = 2 if t % 256 == 0 else 1
        ts = t // nsub
        rows = [slice(r * ts, (r + 1) * ts) for r in range(nsub)]

        def step(on_diagonal):
            sc = [lax.dot_general(q_ref[rw, :], k_ref[...], (((1,), (1,)), ((), ())), preferred_element_type=F32) * MLA_SCALE
                  for rw in rows]
            if on_diagonal:
                gap = lax.broadcasted_iota(jnp.int32, (ts, tk), 1) - lax.broadcasted_iota(jnp.int32, (ts, tk), 0)
                sc = [jnp.where(gap <= qi * t + r * ts - kj * tk, x, -jnp.inf) for r, x in enumerate(sc)]
            m_old = [m_sc[rw, :] for rw in rows]
            m_new = [jnp.maximum(mo, jnp.max(x, axis=-1, keepdims=True)) for mo, x in zip(m_old, sc)]
            alpha = [jnp.exp(mo - mn) for mo, mn in zip(m_old, m_new)]
            p = [jnp.exp(x - mn) for x, mn in zip(sc, m_new)]
            pv = [jnp.dot(x.astype(BF16), v_ref[...], preferred_element_type=F32) for x in p]
            for r, rw in enumerate(rows):
                l_sc[rw, :] = alpha[r] * l_sc[rw, :] + jnp.sum(p[r], axis=-1, keepdims=True)
                acc_sc[rw, :] = alpha[r] * acc_sc[rw, :] + pv[r]
                m_sc[rw, :] = m_new[r]

        @pl.when(kj < last)
        def _():
            step(False)

        @pl.when(kj == last)
        def _():
            step(True)

        @pl.when(kj == nkv - 1)
        def _():
            o_ref[...] = acc_sc[...] / l_sc[...]
            lse_ref[...] = jnp.broadcast_to(m_sc[...] + jnp.log(l_sc[...]), lse_ref.shape)

    return pl.pallas_call(
        body, out_shape=[jax.ShapeDtypeStruct((s, h * HEAD), F32)] * 2, grid=(h, n, nkv),
        in_specs=[pl.BlockSpec((t, MLA_DQK), lambda hh, i, j: (i, hh)),
                  pl.BlockSpec((tk, MLA_DQK), lambda hh, i, j: (jnp.minimum(j, i // 2), hh)),
                  pl.BlockSpec((tk, HEAD), lambda hh, i, j: (jnp.minimum(j, i // 2), hh))],
        out_specs=[pl.BlockSpec((t, HEAD), lambda hh, i, j: (i, hh))] * 2,
        scratch_shapes=[pltpu.VMEM((t, 1), F32), pltpu.VMEM((t, 1), F32), pltpu.VMEM((t, HEAD), F32)],
        compiler_params=_cparams("parallel", "parallel", "arbitrary"), name="mla_attn_fwd")(q, k, v)


def _mla_attn_bwd(q, k, v, o, lse, do):
    s = q.shape[0]
    h = q.shape[1] // MLA_DQK
    t = min(512, s)
    tk = 2 * t
    n, nkv = s // t, s // tk
    nt, tn = (((1,), (1,)), ((), ())), (((0,), (0,)), ((), ()))

    def body(q_ref, k_ref, v_ref, o_ref, lse_ref, do_ref, dq_ref, dk_ref, dv_ref, dq_sc, dk_sc, dv_sc):
        kj, qi = pl.program_id(1), pl.program_id(2)
        last = qi // 2

        @pl.when(jnp.logical_and(kj == 0, qi == 0))
        def _():
            dq_sc[...] = jnp.zeros_like(dq_sc)

        @pl.when(qi == 0)
        def _():
            dk_sc[...] = jnp.zeros_like(dk_sc)
            dv_sc[...] = jnp.zeros_like(dv_sc)

        def pair(on_diagonal):
            sc = lax.dot_general(q_ref[...], k_ref[...], nt, preferred_element_type=F32) * MLA_SCALE
            p = jnp.exp(sc - lse_ref[:, 0:1])
            if on_diagonal:
                gap = lax.broadcasted_iota(jnp.int32, (t, tk), 1) - lax.broadcasted_iota(jnp.int32, (t, tk), 0)
                p = jnp.where(gap <= qi * t - kj * tk, p, 0.0)
            do_v = do_ref[...]
            do_b = do_v.astype(BF16)
            dp = lax.dot_general(do_b, v_ref[...], nt, preferred_element_type=F32)
            delta = jnp.sum(do_v * o_ref[...], axis=-1, keepdims=True)
            ds = (p * (dp - delta) * MLA_SCALE).astype(BF16)
            dv_sc[...] += lax.dot_general(p.astype(BF16), do_b, tn, preferred_element_type=F32)
            dk_sc[...] += lax.dot_general(ds, q_ref[...], tn, preferred_element_type=F32)
            rows = pl.ds(pl.multiple_of(qi * t, t), t)
            dq_sc[rows, :] += jnp.dot(ds, k_ref[...], preferred_element_type=F32)

        @pl.when(last > kj)
        def _():
            pair(False)

        @pl.when(last == kj)
        def _():
            pair(True)

        @pl.when(qi == n - 1)
        def _():
            dk_ref[...] = dk_sc[...].astype(dk_ref.dtype)
            dv_ref[...] = dv_sc[...].astype(dv_ref.dtype)

        @pl.when(jnp.logical_and(kj == nkv - 1, qi == n - 1))
        def _():
            dq_ref[...] = dq_sc[...].astype(dq_ref.dtype)

    qmap = lambda hh, j, i: (jnp.maximum(i, 2 * j), hh)
    kmap = lambda hh, j, i: (j, hh)
    return pl.pallas_call(
        body, out_shape=[jax.ShapeDtypeStruct(q.shape, BF16), jax.ShapeDtypeStruct(k.shape, BF16), jax.ShapeDtypeStruct(v.shape, BF16)],
        grid=(h, nkv, n),
        in_specs=[pl.BlockSpec((t, MLA_DQK), qmap), pl.BlockSpec((tk, MLA_DQK), kmap), pl.BlockSpec((tk, HEAD), kmap),
                  pl.BlockSpec((t, HEAD), qmap), pl.BlockSpec((t, HEAD), qmap), pl.BlockSpec((t, HEAD), qmap)],
        out_specs=[pl.BlockSpec((s, MLA_DQK), lambda hh, j, i: (0, hh)), pl.BlockSpec((tk, MLA_DQK), kmap),
                   pl.BlockSpec((tk, HEAD), kmap)],
        scratch_shapes=[pltpu.VMEM((s, MLA_DQK), F32), pltpu.VMEM((tk, MLA_DQK), F32), pltpu.VMEM((tk, HEAD), F32)],
        compiler_params=_cparams("parallel", "arbitrary", "arbitrary"), name="mla_attn_bwd")(q, k, v, o, lse, do)


@jax.custom_vjp
def mla_attention(q, k, v):
    return _mla_attn_fwd(q, k, v)[0]


def _mla_attention_fwd(q, k, v):
    o, lse = _mla_attn_fwd(q, k, v)
    return o, (q, k, v, o, lse)


def _mla_attention_bwd(saved, do):
    return tuple(_mla_attn_bwd(*saved, do))


mla_attention.defvjp(_mla_attention_fwd, _mla_attention_bwd)


DIL_SCALE = HEAD ** -0.5
GROUP_W = 4 * HEAD


def _dil_scores(q, kp, kc, n):
    dn = (((1,), (1,)), ((), ()))
    sp = lax.dot_general(q, kp, dn, preferred_element_type=F32) * DIL_SCALE
    sc = lax.dot_general(q, kc, dn, preferred_element_type=F32) * DIL_SCALE
    qi = lax.broadcasted_iota(jnp.int32, sp.shape, 0)
    kc_i = lax.broadcasted_iota(jnp.int32, sp.shape, 1)
    vp = jnp.logical_and(kc_i >= qi, n > 0)
    vc = kc_i <= qi
    return sp, sc, vp, vc


def _dil_specs(d):
    cur = pl.BlockSpec((DIL_BLOCK, GROUP_W), lambda r, n: (n, r))
    prev = pl.BlockSpec((DIL_BLOCK, GROUP_W), lambda r, n: (jnp.maximum(n - 1, 0), r))
    return cur, prev


def _dil_fwd(q, k, v, d):
    l = q.shape[0]
    nb = l // DIL_BLOCK
    cur, prev = _dil_specs(d)

    def body(q_ref, kp_ref, kc_ref, vp_ref, vc_ref, o_ref, lse_ref):
        n = pl.program_id(1)
        heads = range(4)
        sl = [slice(h * HEAD, (h + 1) * HEAD) for h in heads]
        scores = [_dil_scores(q_ref[:, sl[h]], kp_ref[:, sl[h]], kc_ref[:, sl[h]], n) for h in heads]
        sp = [jnp.where(vp, s_p, -jnp.inf) for s_p, _, vp, _ in scores]
        sc = [jnp.where(vc, s_c, -jnp.inf) for _, s_c, _, vc in scores]
        m = [jnp.maximum(jnp.max(sp[h], axis=-1, keepdims=True), jnp.max(sc[h], axis=-1, keepdims=True)) for h in heads]
        ep = [jnp.exp(sp[h] - m[h]) for h in heads]
        ec = [jnp.exp(sc[h] - m[h]) for h in heads]
        den = [jnp.sum(ep[h], axis=-1, keepdims=True) + jnp.sum(ec[h], axis=-1, keepdims=True) for h in heads]
        acc = [jnp.dot(ep[h].astype(BF16), vp_ref[:, sl[h]], preferred_element_type=F32)
               + jnp.dot(ec[h].astype(BF16), vc_ref[:, sl[h]], preferred_element_type=F32) for h in heads]
        for h in heads:
            o_ref[:, sl[h]] = acc[h] / den[h]
            lse_ref[:, sl[h]] = jnp.broadcast_to(m[h] + jnp.log(den[h]), (DIL_BLOCK, HEAD))

    return pl.pallas_call(
        body, out_shape=[jax.ShapeDtypeStruct(q.shape, F32)] * 2, grid=(d, nb),
        in_specs=[cur, prev, cur, prev, cur], out_specs=[cur, cur],
        compiler_params=_cparams("parallel", "parallel"), name=f"dil_fwd_d{d}")(q, k, k, v, v)


def _dil_bwd(q, k, v, o, lse, do, dlse, d):
    l = q.shape[0]
    nb = l // DIL_BLOCK
    tn = (((0,), (0,)), ((), ()))
    nt = (((1,), (1,)), ((), ()))
    cur = pl.BlockSpec((DIL_BLOCK, GROUP_W), lambda r, n: (jnp.minimum(n, nb - 1), r))
    prev = pl.BlockSpec((DIL_BLOCK, GROUP_W), lambda r, n: (jnp.maximum(jnp.minimum(n, nb - 1) - 1, 0), r))
    lag = pl.BlockSpec((DIL_BLOCK, GROUP_W), lambda r, n: (jnp.maximum(n - 1, 0), r))

    def body(q_ref, kp_ref, kc_ref, vp_ref, vc_ref, o_ref, lse_ref, do_ref, dl_ref, dq_ref, dk_ref, dv_ref, ck_sc, cv_sc):
        n = pl.program_id(1)

        @pl.when(n < nb)
        def _():
            heads = range(4)
            sl = [slice(h * HEAD, (h + 1) * HEAD) for h in heads]
            scores = [_dil_scores(q_ref[:, sl[h]], kp_ref[:, sl[h]], kc_ref[:, sl[h]], n) for h in heads]
            lse = [lse_ref[:, h * HEAD:h * HEAD + 1] for h in heads]
            pp = [jnp.where(scores[h][2], jnp.exp(scores[h][0] - lse[h]), 0.0) for h in heads]
            pc = [jnp.where(scores[h][3], jnp.exp(scores[h][1] - lse[h]), 0.0) for h in heads]
            do_b = [do_ref[:, sl[h]].astype(BF16) for h in heads]
            corr = [jnp.sum(dl_ref[:, sl[h]], axis=-1, keepdims=True)
                    - jnp.sum(do_ref[:, sl[h]] * o_ref[:, sl[h]], axis=-1, keepdims=True) for h in heads]
            dsp = [(pp[h] * (lax.dot_general(do_b[h], vp_ref[:, sl[h]], nt, preferred_element_type=F32) + corr[h])
                    * DIL_SCALE).astype(BF16) for h in heads]
            dsc = [(pc[h] * (lax.dot_general(do_b[h], vc_ref[:, sl[h]], nt, preferred_element_type=F32) + corr[h])
                    * DIL_SCALE).astype(BF16) for h in heads]
            dkp = [lax.dot_general(dsp[h], q_ref[:, sl[h]], tn, preferred_element_type=F32) for h in heads]
            dvp = [lax.dot_general(pp[h].astype(BF16), do_b[h], tn, preferred_element_type=F32) for h in heads]
            for h in heads:
                dq_ref[:, sl[h]] = (jnp.dot(dsp[h], kp_ref[:, sl[h]], preferred_element_type=F32)
                                    + jnp.dot(dsc[h], kc_ref[:, sl[h]], preferred_element_type=F32)).astype(dq_ref.dtype)

            @pl.when(n > 0)
            def _():
                for h in heads:
                    dk_ref[:, sl[h]] = (ck_sc[:, sl[h]] + dkp[h]).astype(dk_ref.dtype)
                    dv_ref[:, sl[h]] = (cv_sc[:, sl[h]] + dvp[h]).astype(dv_ref.dtype)

            for h in heads:
                ck_sc[:, sl[h]] = lax.dot_general(dsc[h], q_ref[:, sl[h]], tn, preferred_element_type=F32)
                cv_sc[:, sl[h]] = lax.dot_general(pc[h].astype(BF16), do_b[h], tn, preferred_element_type=F32)

        @pl.when(n == nb)
        def _():
            dk_ref[...] = ck_sc[...].astype(dk_ref.dtype)
            dv_ref[...] = cv_sc[...].astype(dv_ref.dtype)

    return pl.pallas_call(
        body, out_shape=[jax.ShapeDtypeStruct(q.shape, BF16)] * 3, grid=(d, nb + 1),
        in_specs=[cur, prev, cur, prev, cur, cur, cur, cur, cur], out_specs=[cur, lag, lag],
        scratch_shapes=[pltpu.VMEM((DIL_BLOCK, GROUP_W), F32)] * 2,
        compiler_params=_cparams("parallel", "arbitrary"), name=f"dil_bwd_d{d}")(q, k, k, v, v, o, lse, do, dlse)


def _make_dil(d):
    @jax.custom_vjp
    def att(q, k, v):
        return tuple(_dil_fwd(q, k, v, d))

    def fwd(q, k, v):
        o, lse = _dil_fwd(q, k, v, d)
        return (o, lse), (q, k, v, o, lse)

    def bwd(saved, cts):
        return tuple(_dil_bwd(*saved, cts[0], cts[1], d))

    att.defvjp(fwd, bwd)
    return att


def dilated_group(q, k, v, d):
    return _make_dil(d)(q, k, v)


def _dn_post(c, kind, scale):
    m = _silu(c)
    nrm = m * lax.rsqrt(jnp.sum(m * m, axis=-1, keepdims=True) + 1e-6) * scale
    return kind * nrm + (1.0 - kind) * m


def _dn_kind_scale(j):
    kind = jnp.where(j < 2 * DN_HEADS, 1.0, 0.0).astype(F32)
    scale = jnp.where(j < DN_HEADS, DN_DIM ** -0.5, 1.0).astype(F32)
    return kind, scale


_CONV_RB = 512
_CONV_PAD = 8


def _conv_windows(pad_ref, w_ref, r0, rb, sign):
    acc = None
    for sh in range(DN_CONV):
        win = pad_ref[pl.ds(r0 + _CONV_PAD * (sign < 0) + sign * sh, rb), :]
        term = w_ref[DN_CONV - 1 - sh:DN_CONV - sh, :] * win
        acc = term if acc is None else acc + term
    return acc


def _dn_conv_fwd(x, w):
    s, width = x.shape
    rb = min(_CONV_RB, s)

    def body(x_ref, w_ref, o_ref, pad_ref):
        kind, scale = _dn_kind_scale(pl.program_id(0))
        pad_ref[0:_CONV_PAD, :] = jnp.zeros((_CONV_PAD, LANE), F32)
        pad_ref[_CONV_PAD:, :] = x_ref[...]
        for r0 in range(0, s, rb):
            c = _conv_windows(pad_ref, w_ref, r0, rb, -1)
            o_ref[r0:r0 + rb, :] = _dn_post(c, kind, scale)

    return pl.pallas_call(
        body, out_shape=jax.ShapeDtypeStruct(x.shape, F32), grid=(width // LANE,),
        in_specs=[pl.BlockSpec((s, LANE), lambda j: (0, j)), pl.BlockSpec((DN_CONV, LANE), lambda j: (0, j))],
        out_specs=pl.BlockSpec((s, LANE), lambda j: (0, j)), scratch_shapes=[pltpu.VMEM((s + _CONV_PAD, LANE), F32)],
        compiler_params=_cparams("parallel"), name="dn_conv_fwd")(x, w)


def _dn_conv_bwd(x, w, dy):
    s, width = x.shape
    rb = min(_CONV_RB, s)

    def body(x_ref, w_ref, dy_ref, dx_ref, dw_ref, pad_ref, dpad_ref):
        kind, scale = _dn_kind_scale(pl.program_id(0))
        pad_ref[0:_CONV_PAD, :] = jnp.zeros((_CONV_PAD, LANE), F32)
        pad_ref[_CONV_PAD:, :] = x_ref[...]
        dpad_ref[s:, :] = jnp.zeros((_CONV_PAD, LANE), F32)
        dws = [jnp.zeros((1, LANE), F32) for _ in range(DN_CONV)]
        for r0 in range(0, s, rb):
            c = _conv_windows(pad_ref, w_ref, r0, rb, -1)
            _, vjp = jax.vjp(lambda cc: _dn_post(cc, kind, scale), c)
            dc = vjp(dy_ref[r0:r0 + rb, :])[0]
            dpad_ref[r0:r0 + rb, :] = dc
            for sh in range(DN_CONV):
                win = pad_ref[pl.ds(r0 + _CONV_PAD - sh, rb), :]
                dws[DN_CONV - 1 - sh] = dws[DN_CONV - 1 - sh] + jnp.sum(dc * win, axis=0, keepdims=True)
        for j in range(DN_CONV):
            dw_ref[j:j + 1, :] = dws[j]
        for r0 in range(0, s, rb):
            dx_ref[r0:r0 + rb, :] = _conv_windows(dpad_ref, w_ref, r0, rb, 1)

    return pl.pallas_call(
        body, out_shape=[jax.ShapeDtypeStruct(x.shape, F32), jax.ShapeDtypeStruct(w.shape, F32)], grid=(width // LANE,),
        in_specs=[pl.BlockSpec((s, LANE), lambda j: (0, j)), pl.BlockSpec((DN_CONV, LANE), lambda j: (0, j)),
                  pl.BlockSpec((s, LANE), lambda j: (0, j))],
        out_specs=[pl.BlockSpec((s, LANE), lambda j: (0, j)), pl.BlockSpec((DN_CONV, LANE), lambda j: (0, j))],
        scratch_shapes=[pltpu.VMEM((s + _CONV_PAD, LANE), F32), pltpu.VMEM((s + _CONV_PAD, LANE), F32)],
        compiler_params=_cparams("parallel"), name="dn_conv_bwd")(x, w, dy)


@jax.custom_vjp
def dn_conv(x, w):
    return _dn_conv_fwd(x, w)


dn_conv.defvjp(lambda x, w: (_dn_conv_fwd(x, w), (x, w)), lambda saved, dy: tuple(_dn_conv_bwd(*saved, dy)))


_NN = (((1,), (0,)), ((), ()))
_NT = (((1,), (1,)), ((), ()))
_TN = (((0,), (0,)), ((), ()))


def _bd(a, b, dims, passes=1):
    d = lambda x, y: lax.dot_general(x, y, dims, preferred_element_type=F32)
    if passes == 0:
        return lax.dot_general(a, b, dims, precision=lax.Precision.HIGHEST, preferred_element_type=F32)
    ah, bh = a.astype(BF16), b.astype(BF16)
    if passes == 1:
        return d(ah, bh)
    al, bl = (a - ah.astype(F32)).astype(BF16), (b - bh.astype(F32)).astype(BF16)
    return d(ah, bh) + d(ah, bl) + d(al, bh)


@functools.partial(jax.custom_vjp, nondiff_argnums=(2, 3))
def _pdot(a, b, dims, passes):
    return _bd(a, b, dims, passes)


def _pdot_bwd(dims, passes, saved, g):
    a, b = saved
    if dims == _NN:
        return _bd(g, b, _NT, passes), _bd(a, g, _TN, passes)
    if dims == _NT:
        return _bd(g, b, _NN, passes), _bd(g, a, _TN, passes)
    return _bd(b, g, _NT, passes), _bd(a, g, _NN, passes)


_pdot.defvjp(lambda a, b, dims, passes: (_bd(a, b, dims, passes), (a, b)), _pdot_bwd)


GDN_DOT_PASSES = 1
GDN_SOLVE_PASSES = 3


def _hdot(a, b, dims=_NN):
    return _pdot(a, b, dims, GDN_DOT_PASSES)


def _xdot(a, b, dims=_NN):
    return _pdot(a, b, dims, GDN_SOLVE_PASSES)


def _split3(x):
    hi = x.astype(BF16)
    r1 = x - hi.astype(F32)
    mid = r1.astype(BF16)
    lo = (r1 - mid.astype(F32)).astype(BF16)
    return hi, mid, lo


def _tri_dot(tri, x, dims):
    t = tri.astype(BF16)
    return sum(lax.dot_general(t, p, dims, preferred_element_type=F32) for p in _split3(x))


@jax.custom_vjp
def _cumsum_rows(x):
    c = x.shape[0]
    tri = lax.broadcasted_iota(jnp.int32, (c, c), 0) >= lax.broadcasted_iota(jnp.int32, (c, c), 1)
    return _tri_dot(tri, x, _NN)


def _cumsum_rows_bwd(_, g):
    c = g.shape[0]
    tri = lax.broadcasted_iota(jnp.int32, (c, c), 0) >= lax.broadcasted_iota(jnp.int32, (c, c), 1)
    return (_tri_dot(tri, g, _TN),)


_cumsum_rows.defvjp(lambda x: (_cumsum_rows(x), None), _cumsum_rows_bwd)


@jax.custom_vjp
def _unit_lower_inverses(a):
    c = a[0].shape[0]
    eye = (lax.broadcasted_iota(jnp.int32, (c, c), 0) == lax.broadcasted_iota(jnp.int32, (c, c), 1)).astype(F32)
    pw = [-x for x in a]
    t = [eye + p for p in pw]
    for _ in range(int(math.log2(c)) - 1):
        pw = [_bd(p, p, _NN, GDN_SOLVE_PASSES) for p in pw]
        t = [x + _bd(x, p, _NN, GDN_SOLVE_PASSES) for x, p in zip(t, pw)]
    return t


def _unit_lower_inverses_bwd(t, g):
    left = [_bd(x, y, _TN, GDN_SOLVE_PASSES) for x, y in zip(t, g)]
    return ([-_bd(x, y, _NT, GDN_SOLVE_PASSES) for x, y in zip(left, t)],)


_unit_lower_inverses.defvjp(lambda a: (lambda t: (t, t))(_unit_lower_inverses(a)), _unit_lower_inverses_bwd)


@jax.custom_vjp
def _kept_inverses(a, t):
    return t


_kept_inverses.defvjp(lambda a, t: (t, t),
                      lambda t, g: (_unit_lower_inverses_bwd(t, g)[0], [jnp.zeros_like(x) for x in t]))


def _gdn_prep_fn(qkv, gb, kept=None, keep=None):
    c = GDN_CHUNK
    row = lax.broadcasted_iota(jnp.int32, (c, c), 0)
    col = lax.broadcasted_iota(jnp.int32, (c, c), 1)
    incl, strict = row >= col, row > col
    lane = _lanes(gb)
    heads = range(DN_HEADS)
    q, k, v = qkv[:DN_HEADS], qkv[DN_HEADS:2 * DN_HEADS], qkv[2 * DN_HEADS:]
    g = [jnp.sum(jnp.where(lane == h, gb, 0.0), axis=-1, keepdims=True) for h in heads]
    beta = [jnp.sum(jnp.where(lane == DN_HEADS + h, gb, 0.0), axis=-1, keepdims=True) for h in heads]
    gcb = [_cumsum_rows(jnp.broadcast_to(g[h], (c, c))) for h in heads]
    decay = [jnp.where(incl, jnp.exp(jnp.where(incl, gcb[h] - gcb[h].T, 0.0)), 0.0) for h in heads]
    kb = [k[h] * beta[h] for h in heads]
    a = [jnp.where(strict, _hdot(kb[h], k[h], _NT) * decay[h], 0.0) for h in heads]
    t = _unit_lower_inverses(a) if kept is None else _kept_inverses(a, kept)
    if keep is not None:
        keep.extend(t)
    eg = [jnp.exp(gcb[h]) for h in heads]
    u = [_xdot(t[h], v[h] * beta[h]) for h in heads]
    w = [_xdot(t[h], kb[h] * eg[h]) for h in heads]
    qk = [jnp.where(incl, _hdot(q[h], k[h], _NT) * decay[h], 0.0) for h in heads]
    g_last = [jnp.sum(jnp.where(row == c - 1, gcb[h], 0.0), axis=0, keepdims=True) for h in heads]
    kg = [k[h] * jnp.exp(g_last[h] - gcb[h]) for h in heads]
    qg = [q[h] * eg[h] for h in heads]
    el = [jnp.broadcast_to(jnp.exp(g_last[h]), (8, LANE)) for h in heads]
    return [u, w, qg, kg, qk, el]


def _gdn_prep_specs(nq):
    c = GDN_CHUNK
    big = pl.BlockSpec((c, nq * LANE), lambda n: (n, 0))
    return big


def _gdn_prep_fwd(qkv, gb):
    s = qkv.shape[0]
    c = GDN_CHUNK
    n = s // c
    hw = DN_HEADS * LANE

    def body(qkv_ref, gb_ref, u_ref, w_ref, qg_ref, kg_ref, qk_ref, el_ref, t_ref):
        kept = []
        res = _gdn_prep_fn(_chunks(qkv_ref), gb_ref[...], keep=kept)
        for ref, chs in zip((u_ref, w_ref, qg_ref, kg_ref, qk_ref, el_ref, t_ref), res + [kept]):
            _store(ref, chs)

    row = pl.BlockSpec((c, hw), lambda i: (i, 0))
    return pl.pallas_call(
        body, out_shape=[jax.ShapeDtypeStruct((s, hw), F32)] * 5 + [jax.ShapeDtypeStruct((n * 8, hw), F32),
                                                                   jax.ShapeDtypeStruct((s, hw), F32)], grid=(n,),
        in_specs=[pl.BlockSpec((c, 3 * hw), lambda i: (i, 0)), pl.BlockSpec((c, LANE), lambda i: (i, 0))],
        out_specs=[row] * 5 + [pl.BlockSpec((8, hw), lambda i: (i, 0)), row],
        compiler_params=_cparams("parallel"), name="gdn_prep_fwd")(qkv, gb)


def _gdn_prep_bwd(qkv, gb, t, cts):
    s = qkv.shape[0]
    c = GDN_CHUNK
    n = s // c
    hw = DN_HEADS * LANE

    def body(qkv_ref, gb_ref, t_ref, du_ref, dw_ref, dqg_ref, dkg_ref, dqk_ref, del_ref, dqkv_ref, dgb_ref):
        kept = _chunks(t_ref)
        _, vjp = jax.vjp(lambda x, y: _gdn_prep_fn(x, y, kept=kept), _chunks(qkv_ref), gb_ref[...])
        ct = [_chunks(r) for r in (du_ref, dw_ref, dqg_ref, dkg_ref, dqk_ref, del_ref)]
        dqkv, dgb = vjp(ct)
        _store(dqkv_ref, dqkv)
        dgb_ref[...] = dgb

    row = pl.BlockSpec((c, hw), lambda i: (i, 0))
    return pl.pallas_call(
        body, out_shape=[jax.ShapeDtypeStruct(qkv.shape, F32), jax.ShapeDtypeStruct(gb.shape, F32)], grid=(n,),
        in_specs=[pl.BlockSpec((c, 3 * hw), lambda i: (i, 0)), pl.BlockSpec((c, LANE), lambda i: (i, 0)), row]
        + [row] * 5 + [pl.BlockSpec((8, hw), lambda i: (i, 0))],
        out_specs=[pl.BlockSpec((c, 3 * hw), lambda i: (i, 0)), pl.BlockSpec((c, LANE), lambda i: (i, 0))],
        compiler_params=_cparams("parallel"), name="gdn_prep_bwd")(qkv, gb, t, *cts)


@jax.custom_vjp
def gdn_prep(qkv, gb):
    return tuple(_gdn_prep_fwd(qkv, gb)[:6])


def _gdn_prep_vfwd(qkv, gb):
    res = _gdn_prep_fwd(qkv, gb)
    return tuple(res[:6]), (qkv, gb, res[6])


gdn_prep.defvjp(_gdn_prep_vfwd, lambda saved, cts: tuple(_gdn_prep_bwd(*saved, cts)))


def _gdn_scan_fwd(u, w, qg, kg, qk, el):
    s = u.shape[0]
    c = GDN_CHUNK
    n = s // c
    hw = DN_HEADS * LANE

    def body(u_ref, w_ref, qg_ref, kg_ref, qk_ref, el_ref, o_ref, st_ref, s_sc):
        @pl.when(pl.program_id(0) == 0)
        def _():
            s_sc[...] = jnp.zeros_like(s_sc)

        heads = range(DN_HEADS)
        sl = [slice(h * LANE, (h + 1) * LANE) for h in heads]
        st = [s_sc[h] for h in heads]
        for h in heads:
            st_ref[sl[h], :] = st[h]
        v_new = [u_ref[:, sl[h]] - _hdot(w_ref[:, sl[h]], st[h]) for h in heads]
        o_st = [_hdot(qg_ref[:, sl[h]], st[h]) for h in heads]
        o_in = [_hdot(qk_ref[:, sl[h]], v_new[h]) for h in heads]
        s_up = [_hdot(kg_ref[:, sl[h]], v_new[h], _TN) for h in heads]
        for h in heads:
            o_ref[:, sl[h]] = o_st[h] + o_in[h]
            s_sc[h] = st[h] * el_ref[0:1, sl[h]] + s_up[h]

    row = pl.BlockSpec((c, hw), lambda i: (i, 0))
    return pl.pallas_call(
        body, out_shape=[jax.ShapeDtypeStruct((s, hw), F32), jax.ShapeDtypeStruct((n, hw, LANE), F32)], grid=(n,),
        in_specs=[row] * 5 + [pl.BlockSpec((8, hw), lambda i: (i, 0))],
        out_specs=[row, pl.BlockSpec((None, hw, LANE), lambda i: (i, 0, 0))],
        scratch_shapes=[pltpu.VMEM((DN_HEADS, LANE, LANE), F32)],
        compiler_params=_cparams("arbitrary"), name="gdn_scan_fwd")(u, w, qg, kg, qk, el)


def _gdn_scan_bwd(u, w, qg, kg, qk, el, states, do):
    s = u.shape[0]
    c = GDN_CHUNK
    n = s // c
    hw = DN_HEADS * LANE

    def body(u_ref, w_ref, qg_ref, kg_ref, qk_ref, el_ref, st_ref, do_ref,
             du_ref, dw_ref, dqg_ref, dkg_ref, dqk_ref, del_ref, ds_sc):
        @pl.when(pl.program_id(0) == 0)
        def _():
            ds_sc[...] = jnp.zeros_like(ds_sc)

        heads = range(DN_HEADS)
        sl = [slice(h * LANE, (h + 1) * LANE) for h in heads]
        st = [st_ref[sl[h], :] for h in heads]
        ds = [ds_sc[h] for h in heads]
        do = [do_ref[:, sl[h]] for h in heads]
        v_new = [u_ref[:, sl[h]] - _hdot(w_ref[:, sl[h]], st[h]) for h in heads]
        dv_new = [_hdot(qk_ref[:, sl[h]], do[h], _TN) + _hdot(kg_ref[:, sl[h]], ds[h]) for h in heads]
        first_row = lax.broadcasted_iota(jnp.int32, (8, LANE), 0) == 0
        for h in heads:
            du_ref[:, sl[h]] = dv_new[h]
            dw_ref[:, sl[h]] = -_hdot(dv_new[h], st[h], _NT)
            dqg_ref[:, sl[h]] = _hdot(do[h], st[h], _NT)
            dqk_ref[:, sl[h]] = _hdot(do[h], v_new[h], _NT)
            dkg_ref[:, sl[h]] = _hdot(v_new[h], ds[h], _NT)
            del_ref[:, sl[h]] = jnp.where(first_row, jnp.sum(st[h] * ds[h], axis=0, keepdims=True), 0.0)
        ds_new = [_hdot(qg_ref[:, sl[h]], do[h], _TN) + ds[h] * el_ref[0:1, sl[h]] - _hdot(w_ref[:, sl[h]], dv_new[h], _TN)
                  for h in heads]
        for h in heads:
            ds_sc[h] = ds_new[h]

    row = pl.BlockSpec((c, hw), lambda i: (n - 1 - i, 0))
    small = pl.BlockSpec((8, hw), lambda i: (n - 1 - i, 0))
    return pl.pallas_call(
        body, out_shape=[jax.ShapeDtypeStruct((s, hw), F32)] * 5 + [jax.ShapeDtypeStruct((n * 8, hw), F32)], grid=(n,),
        in_specs=[row] * 5 + [small, pl.BlockSpec((None, hw, LANE), lambda i: (n - 1 - i, 0, 0)), row],
        out_specs=[row] * 5 + [small], scratch_shapes=[pltpu.VMEM((DN_HEADS, LANE, LANE), F32)],
        compiler_params=_cparams("arbitrary"), name="gdn_scan_bwd")(u, w, qg, kg, qk, el, states, do)


@jax.custom_vjp
def gdn_scan(u, w, qg, kg, qk, el):
    return _gdn_scan_fwd(u, w, qg, kg, qk, el)[0]


def _gdn_scan_vfwd(*args):
    o, states = _gdn_scan_fwd(*args)
    return o, args + (states,)


gdn_scan.defvjp(_gdn_scan_vfwd, lambda saved, do: tuple(_gdn_scan_bwd(*saved, do)))


def _loss_call(y, t):
    s, d = y.shape
    br = min(ROW_BLOCK_NARROW, s)
    n = s // br

    def body(y_ref, t_ref, loss_ref, dy_ref, acc):
        i = pl.program_id(0)

        @pl.when(i == 0)
        def _():
            acc[...] = jnp.zeros_like(acc)

        e = y_ref[...] - t_ref[...]
        dy_ref[...] = e / d
        acc[...] += jnp.sum(e * e, axis=0, keepdims=True)

        @pl.when(i == n - 1)
        def _():
            loss_ref[...] = jnp.broadcast_to(jnp.sum(acc[...], axis=1, keepdims=True) * (0.5 / d), loss_ref.shape)

    row = pl.BlockSpec((br, d), lambda i: (i, 0))
    return pl.pallas_call(
        body, out_shape=[jax.ShapeDtypeStruct((1, LANE), F32), jax.ShapeDtypeStruct((s, d), F32)], grid=(n,),
        in_specs=[row, row], out_specs=[pl.BlockSpec((1, LANE), lambda i: (0, 0)), row],
        scratch_shapes=[pltpu.VMEM((1, d), F32)], compiler_params=_cparams("arbitrary"), name="loss_head")(y, t)


@jax.custom_vjp
def loss_head(y, t):
    return _loss_call(y, t)[0][0, 0]


def _loss_head_fwd(y, t):
    loss, dy = _loss_call(y, t)
    return loss[0, 0], (dy,)


loss_head.defvjp(_loss_head_fwd, lambda saved, g: (saved[0] * g, -saved[0] * g))


IN_OFF = {}
_o = 0
for _name, _size in (('q_lat', 384), ('kv_lat', 320), ('z_a', 512), ('dn_qkv', 1536), ('dn_ab', 8), ('z_b', 512),
                     ('dil_qkv', 4608), ('z_c', 512), ('gate', 3072)):
    IN_OFF[_name] = (_o, _o + _size)
    _o += _size
IN_WIDTH = _o
N_CHIPS = 4


def _cols(w, name):
    a, b = IN_OFF[name]
    return w[:, a:b]


def _pad_cols(w, to):
    return jnp.concatenate([w, jnp.zeros((w.shape[0], to - w.shape[1]), w.dtype)], axis=1)


def _pad_row(v, to=None):
    v = v.reshape(1, -1)
    return v if to is None or v.shape[1] == to else _pad_cols(v, to)


def _shard_cols(pieces, a, b):
    wsh = pieces[0].shape[1]
    parts = [pieces[j][:, max(a, j * wsh) - j * wsh:min(b, (j + 1) * wsh) - j * wsh]
             for j in range(len(pieces)) if max(a, j * wsh) < min(b, (j + 1) * wsh)]
    return parts[0] if len(parts) == 1 else jnp.concatenate(parts, axis=1)


def _win_groups_impl(w_in4):
    out = []
    for l in range(w_in4.shape[1]):
        pieces = [w_in4[j, l] for j in range(w_in4.shape[0])]
        cols = lambda name: _shard_cols(pieces, *IN_OFF[name])
        out.append((_pad_cols(jnp.concatenate([cols('q_lat'), cols('kv_lat')], axis=1), 768),
                    jnp.concatenate([cols('z_a'), cols('z_b'), cols('z_c')], axis=1),
                    cols('dn_qkv'), _pad_cols(cols('dn_ab'), LANE), cols('dil_qkv'), cols('gate')))
    return tuple(out)


@jax.custom_vjp
def win_groups(w_in4):
    return _win_groups_impl(w_in4)


def _win_groups_bwd(_, cts):
    n_chip, depth = N_CHIPS, len(cts)
    wsh = IN_WIDTH // n_chip
    per_layer = []
    for l in range(depth):
        dmla, dz, dgdn, dab, ddil, dgate = cts[l]
        full = jnp.concatenate([dmla[:, :704], dz[:, :512], dgdn, dab[:, :8], dz[:, 512:1024], ddil, dz[:, 1024:], dgate], axis=1)
        per_layer.append([full[:, j * wsh:(j + 1) * wsh] for j in range(n_chip)])
    return (jnp.stack([jnp.stack([per_layer[l][j] for l in range(depth)]) for j in range(n_chip)]),)


win_groups.defvjp(lambda w: (_win_groups_impl(w), None), _win_groups_bwd)


def _layer(x, p, tabs):
    w_mla, w_z, w_gdn, w_ab, w_dil, w_gate = p['w_in_groups']
    h, = rowwise(f_rms_full, "rms_in", [x], [_pad_row(p['norm_g'])], [(D_MODEL, BF16)], br=ROW_BLOCK_NARROW)
    mla_in, z, dn_qkv, dn_ab, dil_qkv, gate = matmul_shared(h, (w_mla, w_z, w_gdn, w_ab, w_dil, w_gate))

    qn, kvn, kpe = rowwise(f_mla1, "mla_norm", [mla_in], [_pad_row(p['mla_q_a_norm_g']), _pad_row(p['mla_kv_a_norm_g'])],
                           [(MLA_Q_RANK, BF16), (MLA_KV_RANK, BF16), (LANE, F32)], br=ROW_BLOCK_NARROW)
    wq = p['mla_w_q_b']
    wq_pad = jnp.concatenate(
        [wq[:, hh * MLA_QK:hh * MLA_QK + MLA_NOPE] for hh in range(MLA_HEADS)]
        + [_pad_cols(wq[:, hh * MLA_QK + MLA_NOPE:(hh + 1) * MLA_QK], LANE) for hh in range(MLA_HEADS)], axis=1)
    q = matmul(qn, wq_pad)
    kv = matmul(kvn, p['mla_w_kv_b'])
    gq, gk = p['mla_q_norm_g'], p['mla_k_norm_g']
    q_att, k_att, v_att = rowwise(
        f_mla2, "mla_qk", [q, kv, kpe, tabs['cos_r'], tabs['sin_r']],
        [_pad_row(gq[:MLA_NOPE]), _pad_row(gq[MLA_NOPE:], LANE), _pad_row(gk[:MLA_NOPE]), _pad_row(gk[MLA_NOPE:], LANE)],
        [(MLA_HEADS * MLA_DQK, BF16), (MLA_HEADS * MLA_DQK, BF16), (MLA_HEADS * HEAD, BF16)], nograd=(3, 4), br=ROW_BLOCK_NARROW)
    y_a = mla_attention(q_att, k_att, v_att)

    qkv_n = dn_conv(dn_qkv, p['dn_conv_w'])
    gb, = rowwise(f_gates, "dn_gates", [dn_ab], [_pad_row(p['dn_a_log'], LANE), _pad_row(p['dn_dt_bias'], LANE)], [(LANE, F32)],
                  br=ROW_BLOCK_NARROW)
    o_b = gdn_scan(*gdn_prep(qkv_n, gb))
    y_b, = rowwise(f_headnorm, "dn_out_norm", [o_b], [_pad_row(p['dn_out_norm_g'])], [(DN_HEADS * DN_DIM, F32)],
                   br=ROW_BLOCK_NARROW)

    qkv_d = rowwise(f_dil, "dil_qk", [dil_qkv, tabs['cos_h'], tabs['sin_h']],
                    [_pad_row(p['dil_q_norm_g']), _pad_row(p['dil_k_norm_g'])],
                    [(GROUP_W, BF16)] * (3 * DIL_GROUPS), nograd=(1, 2), out_dil=[d for d in DIL_DILATIONS for _ in range(3)])
    outs, lses = [], []
    for gi, d in enumerate(DIL_DILATIONS):
        o_g, l_g = dilated_group(*qkv_d[3 * gi:3 * gi + 3], d)
        outs.append(o_g)
        lses.append(l_g)
    y_c, = rowwise(f_comb, "dil_comb", outs + lses, [], [(GROUP_W, F32)], row_dil=list(DIL_DILATIONS) * 2, br=ROW_BLOCK_NARROW)

    ys = rowwise(f_merge1, "merge_silu", [y_a, y_b, y_c, z], [], [(BRANCH_W, BF16)] * 3, br=ROW_BLOCK_NARROW)
    bo = [matmul(ys[b], p['w_branch'][b]) for b in range(3)]
    mixed, = rowwise(f_merge2, "merge_gate", [gate] + bo, [], [(D_MODEL, BF16)])
    return matmul(mixed, p['w_out'], res=x)


def _rope_tables(pos, dim):
    inv_freq = 1.0 / (ROPE_THETA ** (jnp.arange(0, dim, 2, dtype=F32) / dim))
    ang = pos.astype(F32)[:, None] * inv_freq
    return jnp.cos(ang), jnp.sin(ang)


def _tables(pos):
    cr, sr = _rope_tables(pos, MLA_ROPE)
    ch, sh = _rope_tables(pos, HEAD)
    zero = jnp.zeros((pos.shape[0], LANE - MLA_ROPE), F32)
    return {'cos_r': jnp.concatenate([cr, cr, zero], axis=1), 'sin_r': jnp.concatenate([sr, sr, zero], axis=1),
            'cos_h': jnp.concatenate([ch, ch], axis=1), 'sin_h': jnp.concatenate([-sh, sh], axis=1)}


def _local_loss(w_in4, mats, conv_w, small, x, target, tabs):
    groups = win_groups(w_in4)
    for l in range(DEPTH):
        p = {k: v[l] for k, v in mats.items()}
        p.update({k: v[l] for k, v in small.items()})
        p['dn_conv_w'] = conv_w[l]
        p['w_in_groups'] = groups[l]
        x = _layer(x, p, tabs)
    return loss_head(x, target)


def _pack(arrays, dtype, row_tile):
    flat = jnp.concatenate([a.astype(dtype).reshape(-1) for a in arrays])
    rows = -(-flat.shape[0] // (LANE * row_tile)) * row_tile
    flat = jnp.concatenate([flat, jnp.zeros((rows * LANE - flat.shape[0],), dtype)])
    return flat.reshape(rows, LANE)


def _unpack_impl(buf, shapes):
    flat = buf.reshape(-1)
    out, off = [], 0
    for shp in shapes:
        n = math.prod(shp)
        out.append(flat[off:off + n].reshape(shp))
        off += n
    return tuple(out)


@functools.partial(jax.custom_vjp, nondiff_argnums=(1, 2, 3))
def _unpack_p(buf, shapes, dtype_name, rows):
    return _unpack_impl(buf, shapes)


_unpack_p.defvjp(lambda buf, shapes, dtype_name, rows: (_unpack_impl(buf, shapes), None),
                 lambda shapes, dtype_name, rows, _, cts: (_pack(cts, jnp.dtype(dtype_name), rows),))


def _unpack(buf, shapes):
    return _unpack_p(buf, tuple(shapes), jnp.dtype(buf.dtype).name, buf.shape[0])


def _full_from_chips(buf4, shard_shapes):
    per_chip = [_unpack(buf4[j], tuple(shard_shapes)) for j in range(4)]
    return {name: jnp.concatenate([per_chip[j][i] for j in range(4)], axis=axis) for i, (name, axis) in enumerate(MATS)}


_HBM = pl.BlockSpec(memory_space=pltpu.HBM)
_VMEM = pl.BlockSpec(memory_space=pltpu.VMEM)


def _chip_peers(x, y):
    return [(1 - x, y), (x, 1 - y), (1 - x, 1 - y)]


def chip_all_to_all(arrays):
    n = len(arrays)

    def body(*refs):
        in_refs, out_refs = refs[:n], refs[n:2 * n]
        send_sems, recv_sems, local_sems = refs[2 * n:]
        x, y, c = lax.axis_index("x"), lax.axis_index("y"), lax.axis_index("c")
        me = 2 * x + y
        peers = _chip_peers(x, y)
        local, sends = [], []
        for a, (i_ref, o_ref) in enumerate(zip(in_refs, out_refs)):
            local.append(pltpu.make_async_copy(i_ref.at[me], o_ref.at[me], local_sems.at[a]))
            local[-1].start()
            for k, (px, py) in enumerate(peers):
                sends.append(pltpu.make_async_remote_copy(
                    src_ref=i_ref.at[2 * px + py], dst_ref=o_ref.at[me], send_sem=send_sems.at[3 * a + k],
                    recv_sem=recv_sems.at[3 * a + k], device_id=(px, py, c), device_id_type=MESH))
                sends[-1].start()
        for a, (i_ref, o_ref) in enumerate(zip(in_refs, out_refs)):
            for k, (px, py) in enumerate(peers):
                pltpu.make_async_remote_copy(
                    src_ref=i_ref.at[me], dst_ref=o_ref.at[2 * px + py], send_sem=send_sems.at[3 * a + k],
                    recv_sem=recv_sems.at[3 * a + k], device_id=(px, py, c), device_id_type=MESH).wait_recv()
        for cp in sends:
            cp.wait_send()
        for cp in local:
            cp.wait()

    return pl.pallas_call(
        body, out_shape=[jax.ShapeDtypeStruct(a.shape, a.dtype) for a in arrays], in_specs=[_HBM] * n, out_specs=[_HBM] * n,
        scratch_shapes=[pltpu.SemaphoreType.DMA((3 * n,)), pltpu.SemaphoreType.DMA((3 * n,)), pltpu.SemaphoreType.DMA((n,))],
        name="chip_all_to_all")(*arrays)


def chip_all_gather(shards):
    n = len(shards)

    def body(*refs):
        in_refs, out_refs = refs[:n], refs[n:2 * n]
        send_sems, recv_sems = refs[2 * n:]
        x, y, c = lax.axis_index("x"), lax.axis_index("y"), lax.axis_index("c")
        me = 2 * x + y
        peers = _chip_peers(x, y)
        sends = []
        for a, (i_ref, o_ref) in enumerate(zip(in_refs, out_refs)):
            for k, (px, py) in enumerate(peers):
                sends.append(pltpu.make_async_remote_copy(
                    src_ref=i_ref.at[c], dst_ref=o_ref.at[me, c], send_sem=send_sems.at[6 * a + k],
                    recv_sem=recv_sems.at[6 * a + k], device_id=(px, py, c), device_id_type=MESH))
                sends[-1].start()
        for a, (i_ref, o_ref) in enumerate(zip(in_refs, out_refs)):
            for k, (px, py) in enumerate(peers):
                landed = o_ref.at[2 * px + py, c]
                pltpu.make_async_remote_copy(src_ref=i_ref.at[c], dst_ref=landed, send_sem=send_sems.at[6 * a + k],
                                             recv_sem=recv_sems.at[6 * a + k], device_id=(px, py, c), device_id_type=MESH).wait_recv()
                sends.append(pltpu.make_async_remote_copy(
                    src_ref=landed, dst_ref=landed, send_sem=send_sems.at[6 * a + 3 + k], recv_sem=recv_sems.at[6 * a + 3 + k],
                    device_id=(x, y, 1 - c), device_id_type=MESH))
                sends[-1].start()
        for a, (i_ref, o_ref) in enumerate(zip(in_refs, out_refs)):
            for k, (px, py) in enumerate(peers):
                other = o_ref.at[2 * px + py, 1 - c]
                pltpu.make_async_remote_copy(src_ref=other, dst_ref=other, send_sem=send_sems.at[6 * a + 3 + k],
                                             recv_sem=recv_sems.at[6 * a + 3 + k], device_id=(x, y, 1 - c),
                                             device_id_type=MESH).wait_recv()
        for cp in sends:
            cp.wait_send()

    me = 2 * lax.axis_index("x") + lax.axis_index("y")
    outs = pl.pallas_call(
        body, out_shape=[jax.ShapeDtypeStruct((4,) + a.shape, a.dtype) for a in shards], in_specs=[_HBM] * n, out_specs=[_HBM] * n,
        scratch_shapes=[pltpu.SemaphoreType.DMA((6 * n,)), pltpu.SemaphoreType.DMA((6 * n,))],
        name="chip_all_gather")(*shards)
    return [lax.dynamic_update_index_in_dim(o, a, me, 0) for o, a in zip(outs, shards)]


def sibling_swap(arrays):
    n = len(arrays)

    def body(*refs):
        in_refs, out_refs = refs[:n], refs[n:2 * n]
        send_sems, recv_sems = refs[2 * n:]
        x, y, c = lax.axis_index("x"), lax.axis_index("y"), lax.axis_index("c")
        cps = [pltpu.make_async_remote_copy(src_ref=i_ref, dst_ref=o_ref, send_sem=send_sems.at[a], recv_sem=recv_sems.at[a],
                                            device_id=(x, y, 1 - c), device_id_type=MESH)
               for a, (i_ref, o_ref) in enumerate(zip(in_refs, out_refs))]
        for cp in cps:
            cp.start()
        for cp in cps:
            cp.wait()

    return pl.pallas_call(
        body, out_shape=[jax.ShapeDtypeStruct(a.shape, a.dtype) for a in arrays], in_specs=[_HBM] * n, out_specs=[_HBM] * n,
        scratch_shapes=[pltpu.SemaphoreType.DMA((n,)), pltpu.SemaphoreType.DMA((n,))], name="sibling_swap")(*arrays)


def sibling_all_gather(halves):
    n = len(halves)

    def body(*refs):
        in_refs, out_refs = refs[:n], refs[n:2 * n]
        send_sems, recv_sems = refs[2 * n:]
        x, y, c = lax.axis_index("x"), lax.axis_index("y"), lax.axis_index("c")
        sends = [pltpu.make_async_remote_copy(src_ref=i_ref, dst_ref=o_ref.at[c], send_sem=send_sems.at[a], recv_sem=recv_sems.at[a],
                                              device_id=(x, y, 1 - c), device_id_type=MESH)
                 for a, (i_ref, o_ref) in enumerate(zip(in_refs, out_refs))]
        for cp in sends:
            cp.start()
        for a, (i_ref, o_ref) in enumerate(zip(in_refs, out_refs)):
            pltpu.make_async_remote_copy(src_ref=i_ref, dst_ref=o_ref.at[1 - c], send_sem=send_sems.at[a], recv_sem=recv_sems.at[a],
                                         device_id=(x, y, 1 - c), device_id_type=MESH).wait_recv()
        for cp in sends:
            cp.wait_send()

    outs = pl.pallas_call(
        body, out_shape=[jax.ShapeDtypeStruct((2,) + a.shape, a.dtype) for a in halves], in_specs=[_HBM] * n, out_specs=[_HBM] * n,
        scratch_shapes=[pltpu.SemaphoreType.DMA((n,)), pltpu.SemaphoreType.DMA((n,))], name="sibling_all_gather")(*halves)
    return [lax.dynamic_update_index_in_dim(o, a, lax.axis_index("c"), 0) for o, a in zip(outs, halves)]


def all_gather8(v, name):
    def body(v_ref, out_ref, send_sems, recv_sems):
        x, y, c = lax.axis_index("x"), lax.axis_index("y"), lax.axis_index("c")
        out_ref[4 * x + 2 * y + c] = v_ref[...]

        def peer(k):
            return (x ^ (k >> 2), y ^ ((k >> 1) & 1), c ^ (k & 1))

        sends = [pltpu.make_async_remote_copy(src_ref=v_ref, dst_ref=out_ref.at[4 * x + 2 * y + c], send_sem=send_sems.at[k - 1],
                                              recv_sem=recv_sems.at[k - 1], device_id=peer(k), device_id_type=MESH)
                 for k in range(1, 8)]
        for cp in sends:
            cp.start()
        for k in range(1, 8):
            px, py, pc = peer(k)
            pltpu.make_async_remote_copy(src_ref=v_ref, dst_ref=out_ref.at[4 * px + 2 * py + pc], send_sem=send_sems.at[k - 1],
                                         recv_sem=recv_sems.at[k - 1], device_id=peer(k), device_id_type=MESH).wait_recv()
        for cp in sends:
            cp.wait_send()

    return pl.pallas_call(
        body, out_shape=jax.ShapeDtypeStruct((8,) + v.shape, v.dtype), in_specs=[_VMEM], out_specs=_VMEM,
        scratch_shapes=[pltpu.SemaphoreType.DMA((7,)), pltpu.SemaphoreType.DMA((7,))], name=name)(v)


def pair_add(a, b, row_tile, name):
    rows, width = a.shape

    def body(a_ref, b_ref, o_ref):
        o_ref[...] = (a_ref[...].astype(F32) + b_ref[...].astype(F32)).astype(o_ref.dtype)

    spec = pl.BlockSpec((row_tile, width), lambda i: (i, 0))
    return pl.pallas_call(body, out_shape=jax.ShapeDtypeStruct(a.shape, a.dtype), grid=(rows // row_tile,),
                          in_specs=[spec, spec], out_specs=spec, compiler_params=_cparams("parallel"), name=name)(a, b)


def sum_blocks(blocks, row_tile, name):
    n, rows, width = blocks.shape

    def body(b_ref, o_ref):
        acc = b_ref[0].astype(F32)
        for j in range(1, n):
            acc = acc + b_ref[j].astype(F32)
        o_ref[...] = acc

    return pl.pallas_call(
        body, out_shape=jax.ShapeDtypeStruct((rows, width), F32), grid=(rows // row_tile,),
        in_specs=[pl.BlockSpec((n, row_tile, width), lambda i: (0, i, 0))],
        out_specs=pl.BlockSpec((row_tile, width), lambda i: (i, 0)), compiler_params=_cparams("parallel"), name=name)(blocks)


def adamw(g_parts, w, m, v, row_tile, name):
    npart = len(g_parts)

    def body(*refs):
        g = refs[0][...]
        for r in refs[1:npart]:
            g = g + r[...]
        w_ref, m_ref, v_ref, g_out, d_out, m_out, v_out = refs[npart:]
        m_new = ADAM_B1 * m_ref[...] + (1.0 - ADAM_B1) * g
        v_new = ADAM_B2 * v_ref[...] + (1.0 - ADAM_B2) * (g * g)
        m_hat = m_new / (1.0 - ADAM_B1 ** ADAM_STEP)
        v_hat = v_new / (1.0 - ADAM_B2 ** ADAM_STEP)
        g_out[...] = g
        d_out[...] = -ADAM_LR * (m_hat / (jnp.sqrt(v_hat) + ADAM_EPS) + ADAM_WD * w_ref[...])
        m_out[...] = m_new
        v_out[...] = v_new

    rows, width = w.shape
    spec = pl.BlockSpec((row_tile, width), lambda i: (i, 0))
    return pl.pallas_call(
        body, out_shape=[jax.ShapeDtypeStruct(w.shape, F32)] * 4, grid=(rows // row_tile,),
        in_specs=[spec] * (npart + 3), out_specs=[spec] * 4, compiler_params=_cparams("parallel"), name=name)(*g_parts, w, m, v)


def kernel(x, positions, norm_g, w_in, mla_q_a_norm_g, mla_w_q_b, mla_kv_a_norm_g, mla_w_kv_b, mla_q_norm_g, mla_k_norm_g, dn_conv_w, dn_a_log, dn_dt_bias, dn_out_norm_g, dil_q_norm_g, dil_k_norm_g, w_branch, w_out, loss_target, m_norm_g, m_w_in, m_mla_q_a_norm_g, m_mla_w_q_b, m_mla_kv_a_norm_g, m_mla_w_kv_b, m_mla_q_norm_g, m_mla_k_norm_g, m_dn_conv_w, m_dn_a_log, m_dn_dt_bias, m_dn_out_norm_g, m_dil_q_norm_g, m_dil_k_norm_g, m_w_branch, m_w_out, v_norm_g, v_w_in, v_mla_q_a_norm_g, v_mla_w_q_b, v_mla_kv_a_norm_g, v_mla_w_kv_b, v_mla_q_norm_g, v_mla_k_norm_g, v_dn_conv_w, v_dn_a_log, v_dn_dt_bias, v_dn_out_norm_g, v_dil_q_norm_g, v_dil_k_norm_g, v_w_branch, v_w_out):
    w = dict(norm_g=norm_g, w_in=w_in, mla_q_a_norm_g=mla_q_a_norm_g, mla_w_q_b=mla_w_q_b, mla_kv_a_norm_g=mla_kv_a_norm_g,
             mla_w_kv_b=mla_w_kv_b, mla_q_norm_g=mla_q_norm_g, mla_k_norm_g=mla_k_norm_g, dn_conv_w=dn_conv_w, dn_a_log=dn_a_log,
             dn_dt_bias=dn_dt_bias, dn_out_norm_g=dn_out_norm_g, dil_q_norm_g=dil_q_norm_g, dil_k_norm_g=dil_k_norm_g,
             w_branch=w_branch, w_out=w_out)
    m = dict(norm_g=m_norm_g, w_in=m_w_in, mla_q_a_norm_g=m_mla_q_a_norm_g, mla_w_q_b=m_mla_w_q_b, mla_kv_a_norm_g=m_mla_kv_a_norm_g,
             mla_w_kv_b=m_mla_w_kv_b, mla_q_norm_g=m_mla_q_norm_g, mla_k_norm_g=m_mla_k_norm_g, dn_conv_w=m_dn_conv_w,
             dn_a_log=m_dn_a_log, dn_dt_bias=m_dn_dt_bias, dn_out_norm_g=m_dn_out_norm_g, dil_q_norm_g=m_dil_q_norm_g,
             dil_k_norm_g=m_dil_k_norm_g, w_branch=m_w_branch, w_out=m_w_out)
    v = dict(norm_g=v_norm_g, w_in=v_w_in, mla_q_a_norm_g=v_mla_q_a_norm_g, mla_w_q_b=v_mla_w_q_b, mla_kv_a_norm_g=v_mla_kv_a_norm_g,
             mla_w_kv_b=v_mla_w_kv_b, mla_q_norm_g=v_mla_q_norm_g, mla_k_norm_g=v_mla_k_norm_g, dn_conv_w=v_dn_conv_w,
             dn_a_log=v_dn_a_log, dn_dt_bias=v_dn_dt_bias, dn_out_norm_g=v_dn_out_norm_g, dil_q_norm_g=v_dil_q_norm_g,
             dil_k_norm_g=v_dil_k_norm_g, w_branch=v_w_branch, w_out=v_w_out)
    chip = 2 * lax.axis_index("x") + lax.axis_index("y")
    mat_names = [n for n, _ in MATS]
    mat_shapes = tuple(w[n].shape for n in mat_names)
    conv_shard = dn_conv_w.shape
    win_shape = w_in.shape
    win_rows = (win_shape[0] * win_shape[1], win_shape[2])

    mats_sh = _pack([w[n] for n in mat_names], BF16, 2 * MAT_ROWS)
    mat_rows = mats_sh.shape[0]
    w_in4, mats4 = chip_all_gather([w_in.astype(BF16), mats_sh.reshape(2, mat_rows // 2, LANE)])
    mats4 = mats4.reshape(N_CHIPS, mat_rows, LANE)
    conv8 = all_gather8(_pack([dn_conv_w], F32, 8), "gather_conv_w")
    conv_full = jnp.concatenate([_unpack(conv8[2 * j], (conv_shard,))[0] for j in range(4)], axis=2)
    small = {n: w[n] for n in SMALL}
    tabs = _tables(positions[0])

    def loss_fn(w_in4, mats4, conv_full, small, xs):
        return _local_loss(w_in4, _full_from_chips(mats4, mat_shapes), conv_full, small, xs, loss_target[0], tabs)

    loss, (g_win4, g_mats4, g_conv, g_small, g_x) = jax.value_and_grad(loss_fn, argnums=(0, 1, 2, 3, 4))(
        w_in4, mats4, conv_full, small, x[0])
    loss = lax.psum(loss, ("x", "y", "c"))

    c_idx = lax.axis_index("c")
    half_win = (N_CHIPS * win_shape[1], win_shape[2])
    half_mats = (N_CHIPS * (mat_rows // 2), LANE)
    g_mats4 = g_mats4.reshape(N_CHIPS, 2, mat_rows // 2, LANE)
    pick = lambda g, i, shape: lax.dynamic_index_in_dim(g, i, axis=1, keepdims=False).reshape(shape)
    from_sib = sibling_swap([pick(g_win4, 1 - c_idx, half_win), pick(g_mats4, 1 - c_idx, half_mats)])
    s_win = pair_add(pick(g_win4, c_idx, half_win), from_sib[0], WIN_ROWS, "pair_add_w_in")
    s_mats = pair_add(pick(g_mats4, c_idx, half_mats), from_sib[1], MAT_ROWS, "pair_add_mats")
    r_win, r_mats = chip_all_to_all([s_win.reshape(N_CHIPS, win_shape[1], win_shape[2]),
                                     s_mats.reshape(N_CHIPS, mat_rows // 2, LANE)])
    g_win, g_mats = sibling_all_gather([sum_blocks(r_win, WIN_ROWS, "sum_chip_pieces_w_in"),
                                        sum_blocks(r_mats, MAT_ROWS, "sum_chip_pieces_mats")])
    res_win = adamw([g_win.reshape(win_rows)], *[d['w_in'].reshape(win_rows) for d in (w, m, v)], WIN_ROWS, "adamw_w_in")
    packed = [_pack([d[n] for n in mat_names], F32, 2 * MAT_ROWS) for d in (w, m, v)]
    res_big = [dict(zip(mat_names, _unpack(r, mat_shapes)))
               for r in adamw([g_mats.reshape(mat_rows, LANE)], *packed, MAT_ROWS, "adamw_mats")]
    for k in range(4):
        res_big[k]['w_in'] = res_win[k].reshape(win_shape)

    small_shapes = tuple(w[n].shape for n in SMALL) + (g_conv.shape,)
    g_all = sum_blocks(all_gather8(_pack([g_small[n] for n in SMALL] + [g_conv], F32, 8), "gather_small_grads"), 8, "sum_small")
    g_list = list(_unpack(g_all, small_shapes))
    g_list[-1] = lax.dynamic_slice_in_dim(g_list[-1], chip * conv_shard[2], conv_shard[2], axis=2)
    small_names = list(SMALL) + ['dn_conv_w']
    packed_s = [_pack([d[n] for n in small_names], F32, 8) for d in (w, m, v)]
    shapes_s = tuple(w[n].shape for n in small_names)
    res_small = [dict(zip(small_names, _unpack(r, shapes_s))) for r in adamw([_pack(g_list, F32, 8)], *packed_s, 8, "adamw_small")]

    outs = [loss, g_x[None]]
    for k in range(4):
        outs += [res_big[k][n] if n in res_big[k] else res_small[k][n] for n in WEIGHTS]
    return tuple(outs)
```

```python
import functools
import math

import jax
import jax.numpy as jnp
from jax import lax
from jax.experimental import pallas as pl
from jax.experimental.pallas import tpu as pltpu

F32 = jnp.float32
BF16 = jnp.bfloat16
MESH = pl.DeviceIdType.MESH

LANE = 128
VMEM_LIMIT = 48 * 1024 * 1024
ROW_BLOCK = 256
ROW_BLOCK_NARROW = 512
MM_TM, MM_TN, MM_TK = 1024, 1024, 1024
MAT_ROWS = 1664
WIN_ROWS = 128

RMS_EPS = 1e-6
ROPE_THETA = 10000.0
D_MODEL = 1024
DEPTH = 2
MLA_HEADS = 4
MLA_NOPE = 128
MLA_ROPE = 64
MLA_QK = MLA_NOPE + MLA_ROPE
MLA_Q_RANK = 384
MLA_KV_RANK = 256
DN_HEADS = 4
DN_DIM = 128
DN_CONV = 4
GDN_CHUNK = 128
DIL_WINDOWS = (128, 512, 2048)
DIL_DILATIONS = (1, 4, 16)
DIL_GROUPS = 3
DIL_BLOCK = 128
HEAD = 128
BRANCH_W = 512

ADAM_LR = 0.001
ADAM_B1 = 0.9
ADAM_B2 = 0.999
ADAM_EPS = 1e-08
ADAM_WD = 0.01
ADAM_STEP = 10

WEIGHTS = ['norm_g', 'w_in', 'mla_q_a_norm_g', 'mla_w_q_b', 'mla_kv_a_norm_g', 'mla_w_kv_b', 'mla_q_norm_g',
           'mla_k_norm_g', 'dn_conv_w', 'dn_a_log', 'dn_dt_bias', 'dn_out_norm_g', 'dil_q_norm_g', 'dil_k_norm_g',
           'w_branch', 'w_out']
MATS = (('mla_w_q_b', 2), ('mla_w_kv_b', 2), ('w_branch', 3), ('w_out', 1))
SMALL = ('norm_g', 'mla_q_a_norm_g', 'mla_kv_a_norm_g', 'mla_q_norm_g', 'mla_k_norm_g', 'dn_a_log', 'dn_dt_bias',
         'dn_out_norm_g', 'dil_q_norm_g', 'dil_k_norm_g')


def _cparams(*sem):
    return pltpu.CompilerParams(dimension_semantics=sem or None, vmem_limit_bytes=VMEM_LIMIT)


def _tile(dim, target):
    best = 0
    for t in range(LANE, min(dim, target) + 1, LANE):
        if dim % t == 0:
            best = t
    assert best, (dim, target)
    return best


def _mm(a, b, mode, out_dtype, res=None):
    if mode == 'nn':
        (m, k), (k2, n) = a.shape, b.shape
    elif mode == 'nt':
        (m, k), (n, k2) = a.shape, b.shape
    else:
        (k, m), (k2, n) = a.shape, b.shape
    assert k == k2, (a.shape, b.shape, mode)
    tm, tn, tk = _tile(m, MM_TM), _tile(n, MM_TN), _tile(k, MM_TK)
    nk = k // tk
    dims = {'nn': (((1,), (0,)), ((), ())), 'nt': (((1,), (1,)), ((), ())), 'tn': (((0,), (0,)), ((), ()))}[mode]

    def body(*refs):
        if res is None:
            a_ref, b_ref, o_ref = refs[:3]
        else:
            a_ref, b_ref, r_ref, o_ref = refs[:4]
        part = lax.dot_general(a_ref[...].astype(BF16), b_ref[...].astype(BF16), dims, preferred_element_type=F32)

        def finish(r):
            if res is not None:
                r = r + r_ref[...].astype(F32)
            o_ref[...] = r.astype(o_ref.dtype)

        if nk == 1:
            finish(part)
            return
        acc = refs[-1]
        kk = pl.program_id(2)

        @pl.when(kk == 0)
        def _():
            acc[...] = part

        @pl.when(jnp.logical_and(kk > 0, kk < nk - 1))
        def _():
            acc[...] += part

        @pl.when(kk == nk - 1)
        def _():
            finish(acc[...] + part)

    a_spec = pl.BlockSpec((tk, tm), lambda i, j, kk: (kk, i)) if mode == 'tn' else pl.BlockSpec((tm, tk), lambda i, j, kk: (i, kk))
    b_spec = pl.BlockSpec((tn, tk), lambda i, j, kk: (j, kk)) if mode == 'nt' else pl.BlockSpec((tk, tn), lambda i, j, kk: (kk, j))
    o_spec = pl.BlockSpec((tm, tn), lambda i, j, kk: (i, j))
    in_specs = [a_spec, b_spec] + ([o_spec] if res is not None else [])
    args = (a, b) + ((res,) if res is not None else ())
    return pl.pallas_call(
        body, out_shape=jax.ShapeDtypeStruct((m, n), out_dtype), grid=(m // tm, n // tn, nk),
        in_specs=in_specs, out_specs=o_spec, scratch_shapes=[pltpu.VMEM((tm, tn), F32)] if nk > 1 else [],
        compiler_params=_cparams("parallel", "parallel", "arbitrary"),
        name=f"mm_{mode}_{m}x{k}x{n}" + ("_res" if res is not None else ""))(*args)


def _make_matmul(out_dtype, with_res):
    @jax.custom_vjp
    def mm(a, b, *r):
        return _mm(a, b, 'nn', out_dtype, *r)

    def fwd(a, b, *r):
        return mm(a, b, *r), (a, b)

    def bwd(saved, g):
        a, b = saved
        da = _mm(g, b, 'nt', a.dtype)
        db = _mm(a, g, 'tn', b.dtype)
        return (da, db) + ((g,) if with_res else ())

    mm.defvjp(fwd, bwd)
    return mm


@jax.custom_vjp
def matmul_shared(a, bs):
    return tuple(_mm(a, b, 'nn', F32) for b in bs)


def _matmul_shared_bwd(saved, gs):
    a, bs = saved
    da = None
    for i, (g, b) in enumerate(zip(gs, bs)):
        da = _mm(g, b, 'nt', a.dtype if i == len(bs) - 1 else F32, da)
    return da, tuple(_mm(a, g, 'tn', b.dtype) for g, b in zip(gs, bs))


matmul_shared.defvjp(lambda a, bs: (matmul_shared(a, bs), (a, bs)), _matmul_shared_bwd)


def matmul(a, b, out_dtype=F32, res=None):
    if res is None:
        return _make_matmul(out_dtype, False)(a, b)
    return _make_matmul(out_dtype, True)(a, b, res)


def _chunks(ref):
    return [ref[:, c * LANE:(c + 1) * LANE].astype(F32) for c in range(ref.shape[-1] // LANE)]


def _store(ref, chunks):
    for c, ch in enumerate(chunks):
        ref[:, c * LANE:(c + 1) * LANE] = ch.astype(ref.dtype)


def _row_spec(width, br, d=1):
    return pl.BlockSpec((br // d, d * width), lambda i: (i, 0))


def _par_spec(width):
    return pl.BlockSpec((1, width), lambda i: (0, 0))


def _load_rows(ref, scratch, d):
    if d == 1:
        return _chunks(ref)
    n, width = ref.shape[0], ref.shape[1] // d
    for c in range(width // LANE):
        for r in range(d):
            lanes = slice(r * width + c * LANE, r * width + (c + 1) * LANE)
            scratch[c, pl.ds(r, n, stride=d), :] = ref[:, lanes].astype(F32)
    return [scratch[c] for c in range(width // LANE)]


def _store_rows(ref, scratch, d, chunks):
    if d == 1:
        return _store(ref, chunks)
    n, width = ref.shape[0], ref.shape[1] // d
    for c, ch in enumerate(chunks):
        scratch[c] = ch
        for r in range(d):
            lanes = slice(r * width + c * LANE, r * width + (c + 1) * LANE)
            ref[:, lanes] = scratch[c, pl.ds(r, n, stride=d), :].astype(ref.dtype)


def _view_scratch(widths_dils, br):
    return [pltpu.VMEM((w // LANE, br, LANE), F32) for w, d in widths_dils if d > 1]


def _with_scratch(dils, scratch_refs):
    it = iter(scratch_refs)
    return [next(it) if d > 1 else None for d in dils]


def _rw_fwd(f, name, rows, params, outs, br, row_dil, out_dil):
    s = rows[0].shape[0] * row_dil[0]
    br = min(br, s)
    nr, npar, nout = len(rows), len(params), len(outs)
    row_w = [r.shape[1] // d for r, d in zip(rows, row_dil)]

    def body(*refs):
        scr = refs[nr + npar + nout:]
        n_in = sum(d > 1 for d in row_dil)
        rc = [_load_rows(r, sc, d) for r, sc, d in zip(refs[:nr], _with_scratch(row_dil, scr[:n_in]), row_dil)]
        pc = [_chunks(p) for p in refs[nr:nr + npar]]
        res = f(rc, pc)
        for o_ref, sc, d, chs in zip(refs[nr + npar:nr + npar + nout], _with_scratch(out_dil, scr[n_in:]), out_dil, res):
            _store_rows(o_ref, sc, d, chs)

    return pl.pallas_call(
        body, out_shape=[jax.ShapeDtypeStruct((s // d, d * w), dt) for (w, dt), d in zip(outs, out_dil)], grid=(s // br,),
        in_specs=[_row_spec(w, br, d) for w, d in zip(row_w, row_dil)] + [_par_spec(p.shape[1]) for p in params],
        out_specs=[_row_spec(w, br, d) for (w, _), d in zip(outs, out_dil)],
        scratch_shapes=_view_scratch(zip(row_w, row_dil), br) + _view_scratch([(w, d) for (w, _), d in zip(outs, out_dil)], br),
        compiler_params=_cparams("parallel"), name=name + "_fwd")(*rows, *params)


def _rw_bwd(f, name, rows, params, cts, nograd, br, row_dil, out_dil):
    s = rows[0].shape[0] * row_dil[0]
    br = min(br, s)
    nr, npar, nct = len(rows), len(params), len(cts)
    grad_rows = [i for i in range(nr) if i not in nograd]
    row_w = [r.shape[1] // d for r, d in zip(rows, row_dil)]
    ct_w = [c.shape[1] // d for c, d in zip(cts, out_dil)]
    grad_dil = [row_dil[i] for i in grad_rows]

    def body(*refs):
        n_out = len(grad_rows) + npar
        out_refs = refs[nr + npar + nct:nr + npar + nct + n_out]
        scr = refs[nr + npar + nct + n_out:]
        n_in, n_ct = sum(d > 1 for d in row_dil), sum(d > 1 for d in out_dil)
        rc = [_load_rows(r, sc, d) for r, sc, d in zip(refs[:nr], _with_scratch(row_dil, scr[:n_in]), row_dil)]
        pc = [_chunks(p) for p in refs[nr:nr + npar]]
        ct = [_load_rows(c, sc, d) for c, sc, d in
              zip(refs[nr + npar:nr + npar + nct], _with_scratch(out_dil, scr[n_in:n_in + n_ct]), out_dil)]
        _, vjp = jax.vjp(f, rc, pc)
        drc, dpc = vjp(ct)
        for o_ref, sc, d, i in zip(out_refs[:len(grad_rows)], _with_scratch(grad_dil, scr[n_in + n_ct:]), grad_dil, grad_rows):
            _store_rows(o_ref, sc, d, drc[i])
        i0 = pl.program_id(0)
        for o_ref, chs in zip(out_refs[len(grad_rows):], dpc):
            @pl.when(i0 == 0)
            def _(o_ref=o_ref):
                o_ref[...] = jnp.zeros_like(o_ref)
            for c, ch in enumerate(chs):
                o_ref[:, c * LANE:(c + 1) * LANE] += ch

    out_shape = ([jax.ShapeDtypeStruct(rows[i].shape, rows[i].dtype) for i in grad_rows]
                 + [jax.ShapeDtypeStruct(p.shape, F32) for p in params])
    out_specs = [_row_spec(row_w[i], br, row_dil[i]) for i in grad_rows] + [_par_spec(p.shape[1]) for p in params]
    res = pl.pallas_call(
        body, out_shape=out_shape, grid=(s // br,),
        in_specs=([_row_spec(w, br, d) for w, d in zip(row_w, row_dil)] + [_par_spec(p.shape[1]) for p in params]
                  + [_row_spec(w, br, d) for w, d in zip(ct_w, out_dil)]),
        out_specs=out_specs,
        scratch_shapes=(_view_scratch(zip(row_w, row_dil), br) + _view_scratch(zip(ct_w, out_dil), br)
                        + _view_scratch([(row_w[i], row_dil[i]) for i in grad_rows], br)),
        compiler_params=_cparams("arbitrary"), name=name + "_bwd")(*rows, *params, *cts)
    drows = [None] * nr
    for o, i in zip(res[:len(grad_rows)], grad_rows):
        drows[i] = o
    for i in nograd:
        drows[i] = jnp.zeros_like(rows[i])
    return tuple(drows), tuple(res[len(grad_rows):])


def rowwise(f, name, rows, params, outs, nograd=(), br=ROW_BLOCK, row_dil=None, out_dil=None):
    row_dil = tuple(row_dil or [1] * len(rows))
    out_dil = tuple(out_dil or [1] * len(outs))

    @jax.custom_vjp
    def op(rows, params):
        return tuple(_rw_fwd(f, name, rows, params, outs, br, row_dil, out_dil))

    def fwd(rows, params):
        return op(rows, params), (rows, params)

    def bwd(saved, cts):
        rows, params = saved
        return _rw_bwd(f, name, rows, params, list(cts), nograd, br, row_dil, out_dil)

    op.defvjp(fwd, bwd)
    return op(tuple(rows), tuple(params))


def _lane_roll(x, s):
    w = x.shape[-1]

    @jax.custom_vjp
    def r(v):
        return pltpu.roll(v, s, 1)

    r.defvjp(lambda v: (r(v), None), lambda _, g: (pltpu.roll(g, (w - s) % w, 1),))
    return r(x)


def _lanes(x):
    return lax.broadcasted_iota(jnp.int32, x.shape, 1)


def _rms(x, g, n=LANE):
    return x * lax.rsqrt(jnp.sum(x * x, axis=-1, keepdims=True) / n + RMS_EPS) * g


def _rope128(x, cos, sin_signed):
    return x * cos + _lane_roll(x, 64) * sin_signed


def _rope64(x, cos, sin):
    lane = _lanes(x)
    rot = jnp.where(lane < 32, -_lane_roll(x, 96), jnp.where(lane < 64, _lane_roll(x, 32), 0.0))
    return x * cos + rot * sin


def _silu(z):
    return z * jax.nn.sigmoid(z)


def _softplus(x):
    return jnp.maximum(x, 0.0) + jnp.log1p(jnp.exp(-jnp.abs(x)))


def f_rms_full(rc, pc):
    x, g = rc[0], pc[0]
    width = len(x) * LANE
    ms = sum(jnp.sum(c * c, axis=-1, keepdims=True) for c in x) / width
    r = lax.rsqrt(ms + RMS_EPS)
    return [[c * r * gc for c, gc in zip(x, g)]]


def f_mla1(rc, pc):
    x = rc[0]
    qn = f_rms_full([x[0:3]], [pc[0]])[0]
    kvn = f_rms_full([x[3:5]], [pc[1]])[0]
    return [qn, kvn, [x[5]]]


def f_mla2(rc, pc):
    q, kv, kpe, cos, sin = rc[0], rc[1], rc[2][0], rc[3][0], rc[4][0]
    gqn, gqp, gkn, gkp = pc[0][0], pc[1][0], pc[2][0], pc[3][0]
    k_pe = _rope64(_rms(kpe, gkp, MLA_ROPE), cos, sin)
    q_att, k_att, v = [], [], []
    for h in range(MLA_HEADS):
        q_att += [_rms(q[h], gqn), _rope64(_rms(q[MLA_HEADS + h], gqp, MLA_ROPE), cos, sin)]
        k_att += [_rms(kv[2 * h], gkn), k_pe]
        v.append(kv[2 * h + 1])
    return [q_att, k_att, v]


def f_gates(rc, pc):
    x, a_log, dt_bias = rc[0][0], pc[0][0], pc[1][0]
    lane = _lanes(x)
    g = -jnp.exp(a_log) * _softplus(x + dt_bias)
    return [[jnp.where(lane < DN_HEADS, g, jnp.where(lane < 2 * DN_HEADS, jax.nn.sigmoid(x), 0.0))]]


def f_headnorm(rc, pc):
    return [[_rms(c, pc[0][0]) for c in rc[0]]]


def f_dil(rc, pc):
    x, cos, sin = rc[0], rc[1][0], rc[2][0]
    gq, gk = pc[0][0], pc[1][0]
    n = len(x) // 3
    q = [_rope128(_rms(c, gq), cos, sin) for c in x[:n]]
    k = [_rope128(_rms(c, gk), cos, sin) for c in x[n:2 * n]]
    v = list(x[2 * n:])
    per = n // DIL_GROUPS
    return [t[g * per:(g + 1) * per] for g in range(DIL_GROUPS) for t in (q, k, v)]


def f_comb(rc, pc):
    o, l = rc[:DIL_GROUPS], rc[DIL_GROUPS:]
    out = []
    for c in range(len(o[0])):
        m = functools.reduce(jnp.maximum, [lg[c] for lg in l])
        e = [jnp.exp(lg[c] - m) for lg in l]
        den = sum(e)
        out.append(sum(eg * og[c] for eg, og in zip(e, o)) / den)
    return [out]


def f_merge1(rc, pc):
    z = rc[3]
    n = len(rc[0])
    return [[y * _silu(z[b * n + c]) for c, y in enumerate(rc[b])] for b in range(3)]


def f_merge2(rc, pc):
    gate = rc[0]
    n = len(rc[1])
    return [[sum(jax.nn.sigmoid(gate[b * n + c]) * rc[1 + b][c] for b in range(3)) for c in range(n)]]


MLA_DQK = 2 * LANE
MLA_SCALE = MLA_QK ** -0.5


def _mla_attn_fwd(q, k, v):
    s = q.shape[0]
    h = q.shape[1] // MLA_DQK
    t = min(512, s)
    tk = 2 * t
    n, nkv = s // t, s // tk

    def body(q_ref, k_ref, v_ref, o_ref, lse_ref, m_sc, l_sc, acc_sc):
        qi, kj = pl.program_id(1), pl.program_id(2)
        last = qi // 2

        @pl.when(kj == 0)
        def _():
            m_sc[...] = jnp.full_like(m_sc, -jnp.inf)
            l_sc[...] = jnp.zeros_like(l_sc)
            acc_sc[...] = jnp.zeros_like(acc_sc)

        nsub = 2 if t % 256 == 0 else 1
        ts = t // nsub
        rows = [slice(r * ts, (r + 1) * ts) for r in range(nsub)]

        def step(on_diagonal):
            sc = [lax.dot_general(q_ref[rw, :], k_ref[...], (((1,), (1,)), ((), ())), preferred_element_type=F32) * MLA_SCALE
                  for rw in rows]
            if on_diagonal:
                gap = lax.broadcasted_iota(jnp.int32, (ts, tk), 1) - lax.broadcasted_iota(jnp.int32, (ts, tk), 0)
                sc = [jnp.where(gap <= qi * t + r * ts - kj * tk, x, -jnp.inf) for r, x in enumerate(sc)]
            m_old = [m_sc[rw, :] for rw in rows]
            m_new = [jnp.maximum(mo, jnp.max(x, axis=-1, keepdims=True)) for mo, x in zip(m_old, sc)]
            alpha = [jnp.exp(mo - mn) for mo, mn in zip(m_old, m_new)]
            p = [jnp.exp(x - mn) for x, mn in zip(sc, m_new)]
            pv = [jnp.dot(x.astype(BF16), v_ref[...], preferred_element_type=F32) for x in p]
            for r, rw in enumerate(rows):
                l_sc[rw, :] = alpha[r] * l_sc[rw, :] + jnp.sum(p[r], axis=-1, keepdims=True)
                acc_sc[rw, :] = alpha[r] * acc_sc[rw, :] + pv[r]
                m_sc[rw, :] = m_new[r]

        @pl.when(kj < last)
        def _():
            step(False)

        @pl.when(kj == last)
        def _():
            step(True)

        @pl.when(kj == nkv - 1)
        def _():
            o_ref[...] = acc_sc[...] / l_sc[...]
            lse_ref[...] = jnp.broadcast_to(m_sc[...] + jnp.log(l_sc[...]), lse_ref.shape)

    return pl.pallas_call(
        body, out_shape=[jax.ShapeDtypeStruct((s, h * HEAD), F32)] * 2, grid=(h, n, nkv),
        in_specs=[pl.BlockSpec((t, MLA_DQK), lambda hh, i, j: (i, hh)),
                  pl.BlockSpec((tk, MLA_DQK), lambda hh, i, j: (jnp.minimum(j, i // 2), hh)),
                  pl.BlockSpec((tk, HEAD), lambda hh, i, j: (jnp.minimum(j, i // 2), hh))],
        out_specs=[pl.BlockSpec((t, HEAD), lambda hh, i, j: (i, hh))] * 2,
        scratch_shapes=[pltpu.VMEM((t, 1), F32), pltpu.VMEM((t, 1), F32), pltpu.VMEM((t, HEAD), F32)],
        compiler_params=_cparams("parallel", "parallel", "arbitrary"), name="mla_attn_fwd")(q, k, v)


def _mla_attn_bwd(q, k, v, o, lse, do):
    s = q.shape[0]
    h = q.shape[1] // MLA_DQK
    t = min(512, s)
    tk = 2 * t
    n, nkv = s // t, s // tk
    nt, tn = (((1,), (1,)), ((), ())), (((0,), (0,)), ((), ()))

    def body(q_ref, k_ref, v_ref, o_ref, lse_ref, do_ref, dq_ref, dk_ref, dv_ref, dq_sc, dk_sc, dv_sc):
        kj, qi = pl.program_id(1), pl.program_id(2)
        last = qi // 2

        @pl.when(jnp.logical_and(kj == 0, qi == 0))
        def _():
            dq_sc[...] = jnp.zeros_like(dq_sc)

        @pl.when(qi == 0)
        def _():
            dk_sc[...] = jnp.zeros_like(dk_sc)
            dv_sc[...] = jnp.zeros_like(dv_sc)

        def pair(on_diagonal):
            sc = lax.dot_general(q_ref[...], k_ref[...], nt, preferred_element_type=F32) * MLA_SCALE
            p = jnp.exp(sc - lse_ref[:, 0:1])
            if on_diagonal:
                gap = lax.broadcasted_iota(jnp.int32, (t, tk), 1) - lax.broadcasted_iota(jnp.int32, (t, tk), 0)
                p = jnp.where(gap <= qi * t - kj * tk, p, 0.0)
            do_v = do_ref[...]
            do_b = do_v.astype(BF16)
            dp = lax.dot_general(do_b, v_ref[...], nt, preferred_element_type=F32)
            delta = jnp.sum(do_v * o_ref[...], axis=-1, keepdims=True)
            ds = (p * (dp - delta) * MLA_SCALE).astype(BF16)
            dv_sc[...] += lax.dot_general(p.astype(BF16), do_b, tn, preferred_element_type=F32)
            dk_sc[...] += lax.dot_general(ds, q_ref[...], tn, preferred_element_type=F32)
            rows = pl.ds(pl.multiple_of(qi * t, t), t)
            dq_sc[rows, :] += jnp.dot(ds, k_ref[...], preferred_element_type=F32)

        @pl.when(last > kj)
        def _():
            pair(False)

        @pl.when(last == kj)
        def _():
            pair(True)

        @pl.when(qi == n - 1)
        def _():
            dk_ref[...] = dk_sc[...].astype(dk_ref.dtype)
            dv_ref[...] = dv_sc[...].astype(dv_ref.dtype)

        @pl.when(jnp.logical_and(kj == nkv - 1, qi == n - 1))
        def _():
            dq_ref[...] = dq_sc[...].astype(dq_ref.dtype)

    qmap = lambda hh, j, i: (jnp.maximum(i, 2 * j), hh)
    kmap = lambda hh, j, i: (j, hh)
    return pl.pallas_call(
        body, out_shape=[jax.ShapeDtypeStruct(q.shape, BF16), jax.ShapeDtypeStruct(k.shape, BF16), jax.ShapeDtypeStruct(v.shape, BF16)],
        grid=(h, nkv, n),
        in_specs=[pl.BlockSpec((t, MLA_DQK), qmap), pl.BlockSpec((tk, MLA_DQK), kmap), pl.BlockSpec((tk, HEAD), kmap),
                  pl.BlockSpec((t, HEAD), qmap), pl.BlockSpec((t, HEAD), qmap), pl.BlockSpec((t, HEAD), qmap)],
        out_specs=[pl.BlockSpec((s, MLA_DQK), lambda hh, j, i: (0, hh)), pl.BlockSpec((tk, MLA_DQK), kmap),
                   pl.BlockSpec((tk, HEAD), kmap)],
        scratch_shapes=[pltpu.VMEM((s, MLA_DQK), F32), pltpu.VMEM((tk, MLA_DQK), F32), pltpu.VMEM((tk, HEAD), F32)],
        compiler_params=_cparams("parallel", "arbitrary", "arbitrary"), name="mla_attn_bwd")(q, k, v, o, lse, do)


@jax.custom_vjp
def mla_attention(q, k, v):
    return _mla_attn_fwd(q, k, v)[0]


def _mla_attention_fwd(q, k, v):
    o, lse = _mla_attn_fwd(q, k, v)
    return o, (q, k, v, o, lse)


def _mla_attention_bwd(saved, do):
    return tuple(_mla_attn_bwd(*saved, do))


mla_attention.defvjp(_mla_attention_fwd, _mla_attention_bwd)


DIL_SCALE = HEAD ** -0.5
GROUP_W = 4 * HEAD


def _dil_scores(q, kp, kc, n):
    dn = (((1,), (1,)), ((), ()))
    sp = lax.dot_general(q, kp, dn, preferred_element_type=F32) * DIL_SCALE
    sc = lax.dot_general(q, kc, dn, preferred_element_type=F32) * DIL_SCALE
    qi = lax.broadcasted_iota(jnp.int32, sp.shape, 0)
    kc_i = lax.broadcasted_iota(jnp.int32, sp.shape, 1)
    vp = jnp.logical_and(kc_i >= qi, n > 0)
    vc = kc_i <= qi
    return sp, sc, vp, vc


def _dil_specs(d):
    cur = pl.BlockSpec((DIL_BLOCK, GROUP_W), lambda r, n: (n, r))
    prev = pl.BlockSpec((DIL_BLOCK, GROUP_W), lambda r, n: (jnp.maximum(n - 1, 0), r))
    return cur, prev


def _dil_fwd(q, k, v, d):
    l = q.shape[0]
    nb = l // DIL_BLOCK
    cur, prev = _dil_specs(d)

    def body(q_ref, kp_ref, kc_ref, vp_ref, vc_ref, o_ref, lse_ref):
        n = pl.program_id(1)
        heads = range(4)
        sl = [slice(h * HEAD, (h + 1) * HEAD) for h in heads]
        scores = [_dil_scores(q_ref[:, sl[h]], kp_ref[:, sl[h]], kc_ref[:, sl[h]], n) for h in heads]
        sp = [jnp.where(vp, s_p, -jnp.inf) for s_p, _, vp, _ in scores]
        sc = [jnp.where(vc, s_c, -jnp.inf) for _, s_c, _, vc in scores]
        m = [jnp.maximum(jnp.max(sp[h], axis=-1, keepdims=True), jnp.max(sc[h], axis=-1, keepdims=True)) for h in heads]
        ep = [jnp.exp(sp[h] - m[h]) for h in heads]
        ec = [jnp.exp(sc[h] - m[h]) for h in heads]
        den = [jnp.sum(ep[h], axis=-1, keepdims=True) + jnp.sum(ec[h], axis=-1, keepdims=True) for h in heads]
        acc = [jnp.dot(ep[h].astype(BF16), vp_ref[:, sl[h]], preferred_element_type=F32)
               + jnp.dot(ec[h].astype(BF16), vc_ref[:, sl[h]], preferred_element_type=F32) for h in heads]
        for h in heads:
            o_ref[:, sl[h]] = acc[h] / den[h]
            lse_ref[:, sl[h]] = jnp.broadcast_to(m[h] + jnp.log(den[h]), (DIL_BLOCK, HEAD))

    return pl.pallas_call(
        body, out_shape=[jax.ShapeDtypeStruct(q.shape, F32)] * 2, grid=(d, nb),
        in_specs=[cur, prev, cur, prev, cur], out_specs=[cur, cur],
        compiler_params=_cparams("parallel", "parallel"), name=f"dil_fwd_d{d}")(q, k, k, v, v)


def _dil_bwd(q, k, v, o, lse, do, dlse, d):
    l = q.shape[0]
    nb = l // DIL_BLOCK
    tn = (((0,), (0,)), ((), ()))
    nt = (((1,), (1,)), ((), ()))
    cur = pl.BlockSpec((DIL_BLOCK, GROUP_W), lambda r, n: (jnp.minimum(n, nb - 1), r))
    prev = pl.BlockSpec((DIL_BLOCK, GROUP_W), lambda r, n: (jnp.maximum(jnp.minimum(n, nb - 1) - 1, 0), r))
    lag = pl.BlockSpec((DIL_BLOCK, GROUP_W), lambda r, n: (jnp.maximum(n - 1, 0), r))

    def body(q_ref, kp_ref, kc_ref, vp_ref, vc_ref, o_ref, lse_ref, do_ref, dl_ref, dq_ref, dk_ref, dv_ref, ck_sc, cv_sc):
        n = pl.program_id(1)

        @pl.when(n < nb)
        def _():
            heads = range(4)
            sl = [slice(h * HEAD, (h + 1) * HEAD) for h in heads]
            scores = [_dil_scores(q_ref[:, sl[h]], kp_ref[:, sl[h]], kc_ref[:, sl[h]], n) for h in heads]
            lse = [lse_ref[:, h * HEAD:h * HEAD + 1] for h in heads]
            pp = [jnp.where(scores[h][2], jnp.exp(scores[h][0] - lse[h]), 0.0) for h in heads]
            pc = [jnp.where(scores[h][3], jnp.exp(scores[h][1] - lse[h]), 0.0) for h in heads]
            do_b = [do_ref[:, sl[h]].astype(BF16) for h in heads]
            corr = [jnp.sum(dl_ref[:, sl[h]], axis=-1, keepdims=True)
                    - jnp.sum(do_ref[:, sl[h]] * o_ref[:, sl[h]], axis=-1, keepdims=True) for h in heads]
            dsp = [(pp[h] * (lax.dot_general(do_b[h], vp_ref[:, sl[h]], nt, preferred_element_type=F32) + corr[h])
                    * DIL_SCALE).astype(BF16) for h in heads]
            dsc = [(pc[h] * (lax.dot_general(do_b[h], vc_ref[:, sl[h]], nt, preferred_element_type=F32) + corr[h])
                    * DIL_SCALE).astype(BF16) for h in heads]
            dkp = [lax.dot_general(dsp[h], q_ref[:, sl[h]], tn, preferred_element_type=F32) for h in heads]
            dvp = [lax.dot_general(pp[h].astype(BF16), do_b[h], tn, preferred_element_type=F32) for h in heads]
            for h in heads:
                dq_ref[:, sl[h]] = (jnp.dot(dsp[h], kp_ref[:, sl[h]], preferred_element_type=F32)
                                    + jnp.dot(dsc[h], kc_ref[:, sl[h]], preferred_element_type=F32)).astype(dq_ref.dtype)

            @pl.when(n > 0)
            def _():
                for h in heads:
                    dk_ref[:, sl[h]] = (ck_sc[:, sl[h]] + dkp[h]).astype(dk_ref.dtype)
                    dv_ref[:, sl[h]] = (cv_sc[:, sl[h]] + dvp[h]).astype(dv_ref.dtype)

            for h in heads:
                ck_sc[:, sl[h]] = lax.dot_general(dsc[h], q_ref[:, sl[h]], tn, preferred_element_type=F32)
                cv_sc[:, sl[h]] = lax.dot_general(pc[h].astype(BF16), do_b[h], tn, preferred_element_type=F32)

        @pl.when(n == nb)
        def _():
            dk_ref[...] = ck_sc[...].astype(dk_ref.dtype)
            dv_ref[...] = cv_sc[...].astype(dv_ref.dtype)

    return pl.pallas_call(
        body, out_shape=[jax.ShapeDtypeStruct(q.shape, BF16)] * 3, grid=(d, nb + 1),
        in_specs=[cur, prev, cur, prev, cur, cur, cur, cur, cur], out_specs=[cur, lag, lag],
        scratch_shapes=[pltpu.VMEM((DIL_BLOCK, GROUP_W), F32)] * 2,
        compiler_params=_cparams("parallel", "arbitrary"), name=f"dil_bwd_d{d}")(q, k, k, v, v, o, lse, do, dlse)


def _make_dil(d):
    @jax.custom_vjp
    def att(q, k, v):
        return tuple(_dil_fwd(q, k, v, d))

    def fwd(q, k, v):
        o, lse = _dil_fwd(q, k, v, d)
        return (o, lse), (q, k, v, o, lse)

    def bwd(saved, cts):
        return tuple(_dil_bwd(*saved, cts[0], cts[1], d))

    att.defvjp(fwd, bwd)
    return att


def dilated_group(q, k, v, d):
    return _make_dil(d)(q, k, v)


def _dn_post(c, kind, scale):
    m = _silu(c)
    nrm = m * lax.rsqrt(jnp.sum(m * m, axis=-1, keepdims=True) + 1e-6) * scale
    return kind * nrm + (1.0 - kind) * m


def _dn_kind_scale(j):
    kind = jnp.where(j < 2 * DN_HEADS, 1.0, 0.0).astype(F32)
    scale = jnp.where(j < DN_HEADS, DN_DIM ** -0.5, 1.0).astype(F32)
    return kind, scale


_CONV_RB = 512
_CONV_PAD = 8


def _conv_windows(pad_ref, w_ref, r0, rb, sign):
    acc = None
    for sh in range(DN_CONV):
        win = pad_ref[pl.ds(r0 + _CONV_PAD * (sign < 0) + sign * sh, rb), :]
        term = w_ref[DN_CONV - 1 - sh:DN_CONV - sh, :] * win
        acc = term if acc is None else acc + term
    return acc


def _dn_conv_fwd(x, w):
    s, width = x.shape
    rb = min(_CONV_RB, s)

    def body(x_ref, w_ref, o_ref, pad_ref):
        kind, scale = _dn_kind_scale(pl.program_id(0))
        pad_ref[0:_CONV_PAD, :] = jnp.zeros((_CONV_PAD, LANE), F32)
        pad_ref[_CONV_PAD:, :] = x_ref[...]
        for r0 in range(0, s, rb):
            c = _conv_windows(pad_ref, w_ref, r0, rb, -1)
            o_ref[r0:r0 + rb, :] = _dn_post(c, kind, scale)

    return pl.pallas_call(
        body, out_shape=jax.ShapeDtypeStruct(x.shape, F32), grid=(width // LANE,),
        in_specs=[pl.BlockSpec((s, LANE), lambda j: (0, j)), pl.BlockSpec((DN_CONV, LANE), lambda j: (0, j))],
        out_specs=pl.BlockSpec((s, LANE), lambda j: (0, j)), scratch_shapes=[pltpu.VMEM((s + _CONV_PAD, LANE), F32)],
        compiler_params=_cparams("parallel"), name="dn_conv_fwd")(x, w)


def _dn_conv_bwd(x, w, dy):
    s, width = x.shape
    rb = min(_CONV_RB, s)

    def body(x_ref, w_ref, dy_ref, dx_ref, dw_ref, pad_ref, dpad_ref):
        kind, scale = _dn_kind_scale(pl.program_id(0))
        pad_ref[0:_CONV_PAD, :] = jnp.zeros((_CONV_PAD, LANE), F32)
        pad_ref[_CONV_PAD:, :] = x_ref[...]
        dpad_ref[s:, :] = jnp.zeros((_CONV_PAD, LANE), F32)
        dws = [jnp.zeros((1, LANE), F32) for _ in range(DN_CONV)]
        for r0 in range(0, s, rb):
            c = _conv_windows(pad_ref, w_ref, r0, rb, -1)
            _, vjp = jax.vjp(lambda cc: _dn_post(cc, kind, scale), c)
            dc = vjp(dy_ref[r0:r0 + rb, :])[0]
            dpad_ref[r0:r0 + rb, :] = dc
            for sh in range(DN_CONV):
                win = pad_ref[pl.ds(r0 + _CONV_PAD - sh, rb), :]
                dws[DN_CONV - 1 - sh] = dws[DN_CONV - 1 - sh] + jnp.sum(dc * win, axis=0, keepdims=True)
        for j in range(DN_CONV):
            dw_ref[j:j + 1, :] = dws[j]
        for r0 in range(0, s, rb):
            dx_ref[r0:r0 + rb, :] = _conv_windows(dpad_ref, w_ref, r0, rb, 1)

    return pl.pallas_call(
        body, out_shape=[jax.ShapeDtypeStruct(x.shape, F32), jax.ShapeDtypeStruct(w.shape, F32)], grid=(width // LANE,),
        in_specs=[pl.BlockSpec((s, LANE), lambda j: (0, j)), pl.BlockSpec((DN_CONV, LANE), lambda j: (0, j)),
                  pl.BlockSpec((s, LANE), lambda j: (0, j))],
        out_specs=[pl.BlockSpec((s, LANE), lambda j: (0, j)), pl.BlockSpec((DN_CONV, LANE), lambda j: (0, j))],
        scratch_shapes=[pltpu.VMEM((s + _CONV_PAD, LANE), F32), pltpu.VMEM((s + _CONV_PAD, LANE), F32)],
        compiler_params=_cparams("parallel"), name="dn_conv_bwd")(x, w, dy)


@jax.custom_vjp
def dn_conv(x, w):
    return _dn_conv_fwd(x, w)


dn_conv.defvjp(lambda x, w: (_dn_conv_fwd(x, w), (x, w)), lambda saved, dy: tuple(_dn_conv_bwd(*saved, dy)))


_NN = (((1,), (0,)), ((), ()))
_NT = (((1,), (1,)), ((), ()))
_TN = (((0,), (0,)), ((), ()))


def _bd(a, b, dims, passes=1):
    d = lambda x, y: lax.dot_general(x, y, dims, preferred_element_type=F32)
    ah, bh = a.astype(BF16), b.astype(BF16)
    if passes == 1:
        return d(ah, bh)
    al, bl = (a - ah.astype(F32)).astype(BF16), (b - bh.astype(F32)).astype(BF16)
    return d(ah, bh) + d(ah, bl) + d(al, bh)


@functools.partial(jax.custom_vjp, nondiff_argnums=(2, 3))
def _pdot(a, b, dims, passes):
    return _bd(a, b, dims, passes)


def _pdot_bwd(dims, passes, saved, g):
    a, b = saved
    if dims == _NN:
        return _bd(g, b, _NT, passes), _bd(a, g, _TN, passes)
    if dims == _NT:
        return _bd(g, b, _NN, passes), _bd(g, a, _TN, passes)
    return _bd(b, g, _NT, passes), _bd(a, g, _NN, passes)


_pdot.defvjp(lambda a, b, dims, passes: (_bd(a, b, dims, passes), (a, b)), _pdot_bwd)


GDN_DOT_PASSES = 1
GDN_SOLVE_PASSES = 3


def _hdot(a, b, dims=_NN):
    return _pdot(a, b, dims, GDN_DOT_PASSES)


def _xdot(a, b, dims=_NN):
    return _pdot(a, b, dims, GDN_SOLVE_PASSES)


def _split3(x):
    hi = x.astype(BF16)
    r1 = x - hi.astype(F32)
    mid = r1.astype(BF16)
    lo = (r1 - mid.astype(F32)).astype(BF16)
    return hi, mid, lo


def _tri_dot(tri, x, dims):
    t = tri.astype(BF16)
    return sum(lax.dot_general(t, p, dims, preferred_element_type=F32) for p in _split3(x))


@jax.custom_vjp
def _cumsum_rows(x):
    c = x.shape[0]
    tri = lax.broadcasted_iota(jnp.int32, (c, c), 0) >= lax.broadcasted_iota(jnp.int32, (c, c), 1)
    return _tri_dot(tri, x, _NN)


def _cumsum_rows_bwd(_, g):
    c = g.shape[0]
    tri = lax.broadcasted_iota(jnp.int32, (c, c), 0) >= lax.broadcasted_iota(jnp.int32, (c, c), 1)
    return (_tri_dot(tri, g, _TN),)


_cumsum_rows.defvjp(lambda x: (_cumsum_rows(x), None), _cumsum_rows_bwd)


@jax.custom_vjp
def _unit_lower_inverses(a):
    c = a[0].shape[0]
    eye = (lax.broadcasted_iota(jnp.int32, (c, c), 0) == lax.broadcasted_iota(jnp.int32, (c, c), 1)).astype(F32)
    pw = [-x for x in a]
    t = [eye + p for p in pw]
    for _ in range(int(math.log2(c)) - 1):
        pw = [_bd(p, p, _NN, GDN_SOLVE_PASSES) for p in pw]
        t = [x + _bd(x, p, _NN, GDN_SOLVE_PASSES) for x, p in zip(t, pw)]
    return t


def _unit_lower_inverses_bwd(t, g):
    left = [_bd(x, y, _TN, GDN_SOLVE_PASSES) for x, y in zip(t, g)]
    return ([-_bd(x, y, _NT, GDN_SOLVE_PASSES) for x, y in zip(left, t)],)


_unit_lower_inverses.defvjp(lambda a: (lambda t: (t, t))(_unit_lower_inverses(a)), _unit_lower_inverses_bwd)


@jax.custom_vjp
def _kept_inverses(a, t):
    return t


_kept_inverses.defvjp(lambda a, t: (t, t),
                      lambda t, g: (_unit_lower_inverses_bwd(t, g)[0], [jnp.zeros_like(x) for x in t]))


def _gdn_prep_fn(qkv, gb, kept=None, keep=None):
    c = GDN_CHUNK
    row = lax.broadcasted_iota(jnp.int32, (c, c), 0)
    col = lax.broadcasted_iota(jnp.int32, (c, c), 1)
    incl, strict = row >= col, row > col
    lane = _lanes(gb)
    heads = range(DN_HEADS)
    q, k, v = qkv[:DN_HEADS], qkv[DN_HEADS:2 * DN_HEADS], qkv[2 * DN_HEADS:]
    g = [jnp.sum(jnp.where(lane == h, gb, 0.0), axis=-1, keepdims=True) for h in heads]
    beta = [jnp.sum(jnp.where(lane == DN_HEADS + h, gb, 0.0), axis=-1, keepdims=True) for h in heads]
    gcb = [_cumsum_rows(jnp.broadcast_to(g[h], (c, c))) for h in heads]
    decay = [jnp.where(incl, jnp.exp(jnp.where(incl, gcb[h] - gcb[h].T, 0.0)), 0.0) for h in heads]
    kb = [k[h] * beta[h] for h in heads]
    a = [jnp.where(strict, _hdot(kb[h], k[h], _NT) * decay[h], 0.0) for h in heads]
    t = _unit_lower_inverses(a) if kept is None else _kept_inverses(a, kept)
    if keep is not None:
        keep.extend(t)
    eg = [jnp.exp(gcb[h]) for h in heads]
    u = [_xdot(t[h], v[h] * beta[h]) for h in heads]
    w = [_xdot(t[h], kb[h] * eg[h]) for h in heads]
    qk = [jnp.where(incl, _hdot(q[h], k[h], _NT) * decay[h], 0.0) for h in heads]
    g_last = [jnp.sum(jnp.where(row == c - 1, gcb[h], 0.0), axis=0, keepdims=True) for h in heads]
    kg = [k[h] * jnp.exp(g_last[h] - gcb[h]) for h in heads]
    qg = [q[h] * eg[h] for h in heads]
    el = [jnp.broadcast_to(jnp.exp(g_last[h]), (8, LANE)) for h in heads]
    return [u, w, qg, kg, qk, el]


def _gdn_prep_fwd(qkv, gb):
    s = qkv.shape[0]
    c = GDN_CHUNK
    n = s // c
    hw = DN_HEADS * LANE

    def body(qkv_ref, gb_ref, u_ref, w_ref, qg_ref, kg_ref, qk_ref, el_ref, t_ref):
        kept = []
        res = _gdn_prep_fn(_chunks(qkv_ref), gb_ref[...], keep=kept)
        for ref, chs in zip((u_ref, w_ref, qg_ref, kg_ref, qk_ref, el_ref, t_ref), res + [kept]):
            _store(ref, chs)

    row = pl.BlockSpec((c, hw), lambda i: (i, 0))
    return pl.pallas_call(
        body, out_shape=[jax.ShapeDtypeStruct((s, hw), F32)] * 5 + [jax.ShapeDtypeStruct((n * 8, hw), F32),
                                                                   jax.ShapeDtypeStruct((s, hw), F32)], grid=(n,),
        in_specs=[pl.BlockSpec((c, 3 * hw), lambda i: (i, 0)), pl.BlockSpec((c, LANE), lambda i: (i, 0))],
        out_specs=[row] * 5 + [pl.BlockSpec((8, hw), lambda i: (i, 0)), row],
        compiler_params=_cparams("parallel"), name="gdn_prep_fwd")(qkv, gb)


def _gdn_prep_bwd(qkv, gb, t, cts):
    s = qkv.shape[0]
    c = GDN_CHUNK
    n = s // c
    hw = DN_HEADS * LANE

    def body(qkv_ref, gb_ref, t_ref, du_ref, dw_ref, dqg_ref, dkg_ref, dqk_ref, del_ref, dqkv_ref, dgb_ref):
        kept = _chunks(t_ref)
        _, vjp = jax.vjp(lambda x, y: _gdn_prep_fn(x, y, kept=kept), _chunks(qkv_ref), gb_ref[...])
        ct = [_chunks(r) for r in (du_ref, dw_ref, dqg_ref, dkg_ref, dqk_ref, del_ref)]
        dqkv, dgb = vjp(ct)
        _store(dqkv_ref, dqkv)
        dgb_ref[...] = dgb

    row = pl.BlockSpec((c, hw), lambda i: (i, 0))
    return pl.pallas_call(
        body, out_shape=[jax.ShapeDtypeStruct(qkv.shape, F32), jax.ShapeDtypeStruct(gb.shape, F32)], grid=(n,),
        in_specs=[pl.BlockSpec((c, 3 * hw), lambda i: (i, 0)), pl.BlockSpec((c, LANE), lambda i: (i, 0)), row]
        + [row] * 5 + [pl.BlockSpec((8, hw), lambda i: (i, 0))],
        out_specs=[pl.BlockSpec((c, 3 * hw), lambda i: (i, 0)), pl.BlockSpec((c, LANE), lambda i: (i, 0))],
        compiler_params=_cparams("parallel"), name="gdn_prep_bwd")(qkv, gb, t, *cts)


@jax.custom_vjp
def gdn_prep(qkv, gb):
    return tuple(_gdn_prep_fwd(qkv, gb)[:6])


def _gdn_prep_vfwd(qkv, gb):
    res = _gdn_prep_fwd(qkv, gb)
    return tuple(res[:6]), (qkv, gb, res[6])


gdn_prep.defvjp(_gdn_prep_vfwd, lambda saved, cts: tuple(_gdn_prep_bwd(*saved, cts)))


def _gdn_scan_fwd(u, w, qg, kg, qk, el):
    s = u.shape[0]
    c = GDN_CHUNK
    n = s // c
    hw = DN_HEADS * LANE

    def body(u_ref, w_ref, qg_ref, kg_ref, qk_ref, el_ref, o_ref, st_ref, s_sc):
        @pl.when(pl.program_id(0) == 0)
        def _():
            s_sc[...] = jnp.zeros_like(s_sc)

        heads = range(DN_HEADS)
        sl = [slice(h * LANE, (h + 1) * LANE) for h in heads]
        st = [s_sc[h] for h in heads]
        for h in heads:
            st_ref[sl[h], :] = st[h]
        v_new = [u_ref[:, sl[h]] - _hdot(w_ref[:, sl[h]], st[h]) for h in heads]
        o_st = [_hdot(qg_ref[:, sl[h]], st[h]) for h in heads]
        o_in = [_hdot(qk_ref[:, sl[h]], v_new[h]) for h in heads]
        s_up = [_hdot(kg_ref[:, sl[h]], v_new[h], _TN) for h in heads]
        for h in heads:
            o_ref[:, sl[h]] = o_st[h] + o_in[h]
            s_sc[h] = st[h] * el_ref[0:1, sl[h]] + s_up[h]

    row = pl.BlockSpec((c, hw), lambda i: (i, 0))
    return pl.pallas_call(
        body, out_shape=[jax.ShapeDtypeStruct((s, hw), F32), jax.ShapeDtypeStruct((n, hw, LANE), F32)], grid=(n,),
        in_specs=[row] * 5 + [pl.BlockSpec((8, hw), lambda i: (i, 0))],
        out_specs=[row, pl.BlockSpec((None, hw, LANE), lambda i: (i, 0, 0))],
        scratch_shapes=[pltpu.VMEM((DN_HEADS, LANE, LANE), F32)],
        compiler_params=_cparams("arbitrary"), name="gdn_scan_fwd")(u, w, qg, kg, qk, el)


def _gdn_scan_bwd(u, w, qg, kg, qk, el, states, do):
    s = u.shape[0]
    c = GDN_CHUNK
    n = s // c
    hw = DN_HEADS * LANE

    def body(u_ref, w_ref, qg_ref, kg_ref, qk_ref, el_ref, st_ref, do_ref,
             du_ref, dw_ref, dqg_ref, dkg_ref, dqk_ref, del_ref, ds_sc):
        @pl.when(pl.program_id(0) == 0)
        def _():
            ds_sc[...] = jnp.zeros_like(ds_sc)

        heads = range(DN_HEADS)
        sl = [slice(h * LANE, (h + 1) * LANE) for h in heads]
        st = [st_ref[sl[h], :] for h in heads]
        ds = [ds_sc[h] for h in heads]
        do = [do_ref[:, sl[h]] for h in heads]
        v_new = [u_ref[:, sl[h]] - _hdot(w_ref[:, sl[h]], st[h]) for h in heads]
        dv_new = [_hdot(qk_ref[:, sl[h]], do[h], _TN) + _hdot(kg_ref[:, sl[h]], ds[h]) for h in heads]
        first_row = lax.broadcasted_iota(jnp.int32, (8, LANE), 0) == 0
        for h in heads:
            du_ref[:, sl[h]] = dv_new[h]
            dw_ref[:, sl[h]] = -_hdot(dv_new[h], st[h], _NT)
            dqg_ref[:, sl[h]] = _hdot(do[h], st[h], _NT)
            dqk_ref[:, sl[h]] = _hdot(do[h], v_new[h], _NT)
            dkg_ref[:, sl[h]] = _hdot(v_new[h], ds[h], _NT)
            del_ref[:, sl[h]] = jnp.where(first_row, jnp.sum(st[h] * ds[h], axis=0, keepdims=True), 0.0)
        ds_new = [_hdot(qg_ref[:, sl[h]], do[h], _TN) + ds[h] * el_ref[0:1, sl[h]] - _hdot(w_ref[:, sl[h]], dv_new[h], _TN)
                  for h in heads]
        for h in heads:
            ds_sc[h] = ds_new[h]

    row = pl.BlockSpec((c, hw), lambda i: (n - 1 - i, 0))
    small = pl.BlockSpec((8, hw), lambda i: (n - 1 - i, 0))
    return pl.pallas_call(
        body, out_shape=[jax.ShapeDtypeStruct((s, hw), F32)] * 5 + [jax.ShapeDtypeStruct((n * 8, hw), F32)], grid=(n,),
        in_specs=[row] * 5 + [small, pl.BlockSpec((None, hw, LANE), lambda i: (n - 1 - i, 0, 0)), row],
        out_specs=[row] * 5 + [small], scratch_shapes=[pltpu.VMEM((DN_HEADS, LANE, LANE), F32)],
        compiler_params=_cparams("arbitrary"), name="gdn_scan_bwd")(u, w, qg, kg, qk, el, states, do)


@jax.custom_vjp
def gdn_scan(u, w, qg, kg, qk, el):
    return _gdn_scan_fwd(u, w, qg, kg, qk, el)[0]


def _gdn_scan_vfwd(*args):
    o, states = _gdn_scan_fwd(*args)
    return o, args + (states,)


gdn_scan.defvjp(_gdn_scan_vfwd, lambda saved, do: tuple(_gdn_scan_bwd(*saved, do)))


def _loss_call(y, t):
    s, d = y.shape
    br = min(ROW_BLOCK_NARROW, s)
    n = s // br

    def body(y_ref, t_ref, loss_ref, dy_ref, acc):
        i = pl.program_id(0)

        @pl.when(i == 0)
        def _():
            acc[...] = jnp.zeros_like(acc)

        e = y_ref[...] - t_ref[...]
        dy_ref[...] = e / d
        acc[...] += jnp.sum(e * e, axis=0, keepdims=True)

        @pl.when(i == n - 1)
        def _():
            loss_ref[...] = jnp.broadcast_to(jnp.sum(acc[...], axis=1, keepdims=True) * (0.5 / d), loss_ref.shape)

    row = pl.BlockSpec((br, d), lambda i: (i, 0))
    return pl.pallas_call(
        body, out_shape=[jax.ShapeDtypeStruct((1, LANE), F32), jax.ShapeDtypeStruct((s, d), F32)], grid=(n,),
        in_specs=[row, row], out_specs=[pl.BlockSpec((1, LANE), lambda i: (0, 0)), row],
        scratch_shapes=[pltpu.VMEM((1, d), F32)], compiler_params=_cparams("arbitrary"), name="loss_head")(y, t)


@jax.custom_vjp
def loss_head(y, t):
    return _loss_call(y, t)[0][0, 0]


def _loss_head_fwd(y, t):
    loss, dy = _loss_call(y, t)
    return loss[0, 0], (dy,)


loss_head.defvjp(_loss_head_fwd, lambda saved, g: (saved[0] * g, -saved[0] * g))


IN_OFF = {}
_o = 0
for _name, _size in (('q_lat', 384), ('kv_lat', 320), ('z_a', 512), ('dn_qkv', 1536), ('dn_ab', 8), ('z_b', 512),
                     ('dil_qkv', 4608), ('z_c', 512), ('gate', 3072)):
    IN_OFF[_name] = (_o, _o + _size)
    _o += _size
IN_WIDTH = _o
N_CHIPS = 4


def _pad_cols(w, to):
    return jnp.concatenate([w, jnp.zeros((w.shape[0], to - w.shape[1]), w.dtype)], axis=1)


def _pad_row(v, to=None):
    v = v.reshape(1, -1)
    return v if to is None or v.shape[1] == to else _pad_cols(v, to)


def _shard_cols(pieces, a, b):
    wsh = pieces[0].shape[1]
    parts = [pieces[j][:, max(a, j * wsh) - j * wsh:min(b, (j + 1) * wsh) - j * wsh]
             for j in range(len(pieces)) if max(a, j * wsh) < min(b, (j + 1) * wsh)]
    return parts[0] if len(parts) == 1 else jnp.concatenate(parts, axis=1)


def _win_groups_impl(w_in4):
    out = []
    for l in range(w_in4.shape[1]):
        pieces = [w_in4[j, l] for j in range(w_in4.shape[0])]
        cols = lambda name: _shard_cols(pieces, *IN_OFF[name])
        out.append((_pad_cols(jnp.concatenate([cols('q_lat'), cols('kv_lat')], axis=1), 6 * LANE),
                    jnp.concatenate([cols('z_a'), cols('z_b'), cols('z_c')], axis=1),
                    cols('dn_qkv'), _pad_cols(cols('dn_ab'), LANE), cols('dil_qkv'), cols('gate')))
    return tuple(out)


@jax.custom_vjp
def win_groups(w_in4):
    return _win_groups_impl(w_in4)


def _win_groups_bwd(_, cts):
    n_chip, depth = N_CHIPS, len(cts)
    wsh = IN_WIDTH // n_chip
    per_layer = []
    for l in range(depth):
        dmla, dz, dgdn, dab, ddil, dgate = cts[l]
        width = lambda name: IN_OFF[name][1] - IN_OFF[name][0]
        n_lat, n_z, n_ab = width('q_lat') + width('kv_lat'), width('z_a'), width('dn_ab')
        full = jnp.concatenate([dmla[:, :n_lat], dz[:, :n_z], dgdn, dab[:, :n_ab], dz[:, n_z:2 * n_z], ddil, dz[:, 2 * n_z:],
                                dgate], axis=1)
        per_layer.append([full[:, j * wsh:(j + 1) * wsh] for j in range(n_chip)])
    return (jnp.stack([jnp.stack([per_layer[l][j] for l in range(depth)]) for j in range(n_chip)]),)


win_groups.defvjp(lambda w: (_win_groups_impl(w), None), _win_groups_bwd)


def _layer(x, p, tabs):
    w_mla, w_z, w_gdn, w_ab, w_dil, w_gate = p['w_in_groups']
    h, = rowwise(f_rms_full, "rms_in", [x], [_pad_row(p['norm_g'])], [(D_MODEL, BF16)], br=ROW_BLOCK_NARROW)
    mla_in, z, dn_qkv, dn_ab, dil_qkv, gate = matmul_shared(h, (w_mla, w_z, w_gdn, w_ab, w_dil, w_gate))

    qn, kvn, kpe = rowwise(f_mla1, "mla_norm", [mla_in], [_pad_row(p['mla_q_a_norm_g']), _pad_row(p['mla_kv_a_norm_g'])],
                           [(MLA_Q_RANK, BF16), (MLA_KV_RANK, BF16), (LANE, F32)], br=ROW_BLOCK_NARROW)
    wq = p['mla_w_q_b']
    wq_pad = jnp.concatenate(
        [wq[:, hh * MLA_QK:hh * MLA_QK + MLA_NOPE] for hh in range(MLA_HEADS)]
        + [_pad_cols(wq[:, hh * MLA_QK + MLA_NOPE:(hh + 1) * MLA_QK], LANE) for hh in range(MLA_HEADS)], axis=1)
    q = matmul(qn, wq_pad)
    kv = matmul(kvn, p['mla_w_kv_b'])
    gq, gk = p['mla_q_norm_g'], p['mla_k_norm_g']
    q_att, k_att, v_att = rowwise(
        f_mla2, "mla_qk", [q, kv, kpe, tabs['cos_r'], tabs['sin_r']],
        [_pad_row(gq[:MLA_NOPE]), _pad_row(gq[MLA_NOPE:], LANE), _pad_row(gk[:MLA_NOPE]), _pad_row(gk[MLA_NOPE:], LANE)],
        [(MLA_HEADS * MLA_DQK, BF16), (MLA_HEADS * MLA_DQK, BF16), (MLA_HEADS * HEAD, BF16)], nograd=(3, 4), br=ROW_BLOCK_NARROW)
    y_a = mla_attention(q_att, k_att, v_att)

    qkv_n = dn_conv(dn_qkv, p['dn_conv_w'])
    gb, = rowwise(f_gates, "dn_gates", [dn_ab], [_pad_row(p['dn_a_log'], LANE), _pad_row(p['dn_dt_bias'], LANE)], [(LANE, F32)],
                  br=ROW_BLOCK_NARROW)
    o_b = gdn_scan(*gdn_prep(qkv_n, gb))
    y_b, = rowwise(f_headnorm, "dn_out_norm", [o_b], [_pad_row(p['dn_out_norm_g'])], [(DN_HEADS * DN_DIM, F32)],
                   br=ROW_BLOCK_NARROW)

    qkv_d = rowwise(f_dil, "dil_qk", [dil_qkv, tabs['cos_h'], tabs['sin_h']],
                    [_pad_row(p['dil_q_norm_g']), _pad_row(p['dil_k_norm_g'])],
                    [(GROUP_W, BF16)] * (3 * DIL_GROUPS), nograd=(1, 2), out_dil=[d for d in DIL_DILATIONS for _ in range(3)])
    outs, lses = [], []
    for gi, d in enumerate(DIL_DILATIONS):
        o_g, l_g = dilated_group(*qkv_d[3 * gi:3 * gi + 3], d)
        outs.append(o_g)
        lses.append(l_g)
    y_c, = rowwise(f_comb, "dil_comb", outs + lses, [], [(GROUP_W, F32)], row_dil=list(DIL_DILATIONS) * 2)

    ys = rowwise(f_merge1, "merge_silu", [y_a, y_b, y_c, z], [], [(BRANCH_W, BF16)] * 3, br=ROW_BLOCK_NARROW)
    bo = [matmul(ys[b], p['w_branch'][b]) for b in range(3)]
    mixed, = rowwise(f_merge2, "merge_gate", [gate] + bo, [], [(D_MODEL, BF16)])
    return matmul(mixed, p['w_out'], res=x)


def _rope_tables(pos, dim):
    inv_freq = 1.0 / (ROPE_THETA ** (jnp.arange(0, dim, 2, dtype=F32) / dim))
    ang = pos.astype(F32)[:, None] * inv_freq
    return jnp.cos(ang), jnp.sin(ang)


def _tables(pos):
    cr, sr = _rope_tables(pos, MLA_ROPE)
    ch, sh = _rope_tables(pos, HEAD)
    zero = jnp.zeros((pos.shape[0], LANE - MLA_ROPE), F32)
    return {'cos_r': jnp.concatenate([cr, cr, zero], axis=1), 'sin_r': jnp.concatenate([sr, sr, zero], axis=1),
            'cos_h': jnp.concatenate([ch, ch], axis=1), 'sin_h': jnp.concatenate([-sh, sh], axis=1)}


def _local_loss(w_in4, mats, conv_w, small, x, target, tabs):
    groups = win_groups(w_in4)
    for l in range(DEPTH):
        p = {k: v[l] for k, v in mats.items()}
        p.update({k: v[l] for k, v in small.items()})
        p['dn_conv_w'] = conv_w[l]
        p['w_in_groups'] = groups[l]
        x = _layer(x, p, tabs)
    return loss_head(x, target)


def _pack(arrays, dtype, row_tile):
    flat = jnp.concatenate([a.astype(dtype).reshape(-1) for a in arrays])
    rows = -(-flat.shape[0] // (LANE * row_tile)) * row_tile
    flat = jnp.concatenate([flat, jnp.zeros((rows * LANE - flat.shape[0],), dtype)])
    return flat.reshape(rows, LANE)


def _unpack_impl(buf, shapes):
    flat = buf.reshape(-1)
    out, off = [], 0
    for shp in shapes:
        n = math.prod(shp)
        out.append(flat[off:off + n].reshape(shp))
        off += n
    return tuple(out)


@functools.partial(jax.custom_vjp, nondiff_argnums=(1, 2, 3))
def _unpack_p(buf, shapes, dtype_name, rows):
    return _unpack_impl(buf, shapes)


_unpack_p.defvjp(lambda buf, shapes, dtype_name, rows: (_unpack_impl(buf, shapes), None),
                 lambda shapes, dtype_name, rows, _, cts: (_pack(cts, jnp.dtype(dtype_name), rows),))


def _unpack(buf, shapes):
    return _unpack_p(buf, tuple(shapes), jnp.dtype(buf.dtype).name, buf.shape[0])


def _full_from_chips(buf4, shard_shapes):
    per_chip = [_unpack(buf4[j], tuple(shard_shapes)) for j in range(4)]
    return {name: jnp.concatenate([per_chip[j][i] for j in range(4)], axis=axis) for i, (name, axis) in enumerate(MATS)}


_HBM = pl.BlockSpec(memory_space=pltpu.HBM)
_VMEM = pl.BlockSpec(memory_space=pltpu.VMEM)


def _chip_peers(x, y):
    return [(1 - x, y), (x, 1 - y), (1 - x, 1 - y)]


def chip_all_to_all(arrays):
    n = len(arrays)

    def body(*refs):
        in_refs, out_refs = refs[:n], refs[n:2 * n]
        send_sems, recv_sems, local_sems = refs[2 * n:]
        x, y, c = lax.axis_index("x"), lax.axis_index("y"), lax.axis_index("c")
        me = 2 * x + y
        peers = _chip_peers(x, y)
        local, sends = [], []
        for a, (i_ref, o_ref) in enumerate(zip(in_refs, out_refs)):
            local.append(pltpu.make_async_copy(i_ref.at[me], o_ref.at[me], local_sems.at[a]))
            local[-1].start()
            for k, (px, py) in enumerate(peers):
                sends.append(pltpu.make_async_remote_copy(
                    src_ref=i_ref.at[2 * px + py], dst_ref=o_ref.at[me], send_sem=send_sems.at[3 * a + k],
                    recv_sem=recv_sems.at[3 * a + k], device_id=(px, py, c), device_id_type=MESH))
                sends[-1].start()
        for a, (i_ref, o_ref) in enumerate(zip(in_refs, out_refs)):
            for k, (px, py) in enumerate(peers):
                pltpu.make_async_remote_copy(
                    src_ref=i_ref.at[me], dst_ref=o_ref.at[2 * px + py], send_sem=send_sems.at[3 * a + k],
                    recv_sem=recv_sems.at[3 * a + k], device_id=(px, py, c), device_id_type=MESH).wait_recv()
        for cp in sends:
            cp.wait_send()
        for cp in local:
            cp.wait()

    return pl.pallas_call(
        body, out_shape=[jax.ShapeDtypeStruct(a.shape, a.dtype) for a in arrays], in_specs=[_HBM] * n, out_specs=[_HBM] * n,
        scratch_shapes=[pltpu.SemaphoreType.DMA((3 * n,)), pltpu.SemaphoreType.DMA((3 * n,)), pltpu.SemaphoreType.DMA((n,))],
        name="chip_all_to_all")(*arrays)


def chip_all_gather(shards):
    n = len(shards)

    def body(*refs):
        in_refs, out_refs = refs[:n], refs[n:2 * n]
        send_sems, recv_sems = refs[2 * n:]
        x, y, c = lax.axis_index("x"), lax.axis_index("y"), lax.axis_index("c")
        me = 2 * x + y
        peers = _chip_peers(x, y)
        sends = []
        for a, (i_ref, o_ref) in enumerate(zip(in_refs, out_refs)):
            for k, (px, py) in enumerate(peers):
                sends.append(pltpu.make_async_remote_copy(
                    src_ref=i_ref.at[c], dst_ref=o_ref.at[me, c], send_sem=send_sems.at[6 * a + k],
                    recv_sem=recv_sems.at[6 * a + k], device_id=(px, py, c), device_id_type=MESH))
                sends[-1].start()
        for a, (i_ref, o_ref) in enumerate(zip(in_refs, out_refs)):
            for k, (px, py) in enumerate(peers):
                landed = o_ref.at[2 * px + py, c]
                pltpu.make_async_remote_copy(src_ref=i_ref.at[c], dst_ref=landed, send_sem=send_sems.at[6 * a + k],
                                             recv_sem=recv_sems.at[6 * a + k], device_id=(px, py, c), device_id_type=MESH).wait_recv()
                sends.append(pltpu.make_async_remote_copy(
                    src_ref=landed, dst_ref=landed, send_sem=send_sems.at[6 * a + 3 + k], recv_sem=recv_sems.at[6 * a + 3 + k],
                    device_id=(x, y, 1 - c), device_id_type=MESH))
                sends[-1].start()
        for a, (i_ref, o_ref) in enumerate(zip(in_refs, out_refs)):
            for k, (px, py) in enumerate(peers):
                other = o_ref.at[2 * px + py, 1 - c]
                pltpu.make_async_remote_copy(src_ref=other, dst_ref=other, send_sem=send_sems.at[6 * a + 3 + k],
                                             recv_sem=recv_sems.at[6 * a + 3 + k], device_id=(x, y, 1 - c),
                                             device_id_type=MESH).wait_recv()
        for cp in sends:
            cp.wait_send()

    me = 2 * lax.axis_index("x") + lax.axis_index("y")
    outs = pl.pallas_call(
        body, out_shape=[jax.ShapeDtypeStruct((4,) + a.shape, a.dtype) for a in shards], in_specs=[_HBM] * n, out_specs=[_HBM] * n,
        scratch_shapes=[pltpu.SemaphoreType.DMA((6 * n,)), pltpu.SemaphoreType.DMA((6 * n,))],
        name="chip_all_gather")(*shards)
    return [lax.dynamic_update_index_in_dim(o, a, me, 0) for o, a in zip(outs, shards)]


def sibling_swap(arrays):
    n = len(arrays)

    def body(*refs):
        in_refs, out_refs = refs[:n], refs[n:2 * n]
        send_sems, recv_sems = refs[2 * n:]
        x, y, c = lax.axis_index("x"), lax.axis_index("y"), lax.axis_index("c")
        cps = [pltpu.make_async_remote_copy(src_ref=i_ref, dst_ref=o_ref, send_sem=send_sems.at[a], recv_sem=recv_sems.at[a],
                                            device_id=(x, y, 1 - c), device_id_type=MESH)
               for a, (i_ref, o_ref) in enumerate(zip(in_refs, out_refs))]
        for cp in cps:
            cp.start()
        for cp in cps:
            cp.wait()

    return pl.pallas_call(
        body, out_shape=[jax.ShapeDtypeStruct(a.shape, a.dtype) for a in arrays], in_specs=[_HBM] * n, out_specs=[_HBM] * n,
        scratch_shapes=[pltpu.SemaphoreType.DMA((n,)), pltpu.SemaphoreType.DMA((n,))], name="sibling_swap")(*arrays)


def sibling_all_gather(halves):
    n = len(halves)

    def body(*refs):
        in_refs, out_refs = refs[:n], refs[n:2 * n]
        send_sems, recv_sems = refs[2 * n:]
        x, y, c = lax.axis_index("x"), lax.axis_index("y"), lax.axis_index("c")
        sends = [pltpu.make_async_remote_copy(src_ref=i_ref, dst_ref=o_ref.at[c], send_sem=send_sems.at[a], recv_sem=recv_sems.at[a],
                                              device_id=(x, y, 1 - c), device_id_type=MESH)
                 for a, (i_ref, o_ref) in enumerate(zip(in_refs, out_refs))]
        for cp in sends:
            cp.start()
        for a, (i_ref, o_ref) in enumerate(zip(in_refs, out_refs)):
            pltpu.make_async_remote_copy(src_ref=i_ref, dst_ref=o_ref.at[1 - c], send_sem=send_sems.at[a], recv_sem=recv_sems.at[a],
                                         device_id=(x, y, 1 - c), device_id_type=MESH).wait_recv()
        for cp in sends:
            cp.wait_send()

    outs = pl.pallas_call(
        body, out_shape=[jax.ShapeDtypeStruct((2,) + a.shape, a.dtype) for a in halves], in_specs=[_HBM] * n, out_specs=[_HBM] * n,
        scratch_shapes=[pltpu.SemaphoreType.DMA((n,)), pltpu.SemaphoreType.DMA((n,))], name="sibling_all_gather")(*halves)
    return [lax.dynamic_update_index_in_dim(o, a, lax.axis_index("c"), 0) for o, a in zip(outs, halves)]


def all_gather8(v, name):
    def body(v_ref, out_ref, send_sems, recv_sems):
        x, y, c = lax.axis_index("x"), lax.axis_index("y"), lax.axis_index("c")
        out_ref[4 * x + 2 * y + c] = v_ref[...]

        def peer(k):
            return (x ^ (k >> 2), y ^ ((k >> 1) & 1), c ^ (k & 1))

        sends = [pltpu.make_async_remote_copy(src_ref=v_ref, dst_ref=out_ref.at[4 * x + 2 * y + c], send_sem=send_sems.at[k - 1],
                                              recv_sem=recv_sems.at[k - 1], device_id=peer(k), device_id_type=MESH)
                 for k in range(1, 8)]
        for cp in sends:
            cp.start()
        for k in range(1, 8):
            px, py, pc = peer(k)
            pltpu.make_async_remote_copy(src_ref=v_ref, dst_ref=out_ref.at[4 * px + 2 * py + pc], send_sem=send_sems.at[k - 1],
                                         recv_sem=recv_sems.at[k - 1], device_id=peer(k), device_id_type=MESH).wait_recv()
        for cp in sends:
            cp.wait_send()

    return pl.pallas_call(
        body, out_shape=jax.ShapeDtypeStruct((8,) + v.shape, v.dtype), in_specs=[_VMEM], out_specs=_VMEM,
        scratch_shapes=[pltpu.SemaphoreType.DMA((7,)), pltpu.SemaphoreType.DMA((7,))], name=name)(v)


def pair_add(a, b, row_tile, name):
    rows, width = a.shape

    def body(a_ref, b_ref, o_ref):
        o_ref[...] = (a_ref[...].astype(F32) + b_ref[...].astype(F32)).astype(o_ref.dtype)

    spec = pl.BlockSpec((row_tile, width), lambda i: (i, 0))
    return pl.pallas_call(body, out_shape=jax.ShapeDtypeStruct(a.shape, a.dtype), grid=(rows // row_tile,),
                          in_specs=[spec, spec], out_specs=spec, compiler_params=_cparams("parallel"), name=name)(a, b)


def sum_blocks(blocks, row_tile, name):
    n, rows, width = blocks.shape

    def body(b_ref, o_ref):
        acc = b_ref[0].astype(F32)
        for j in range(1, n):
            acc = acc + b_ref[j].astype(F32)
        o_ref[...] = acc

    return pl.pallas_call(
        body, out_shape=jax.ShapeDtypeStruct((rows, width), F32), grid=(rows // row_tile,),
        in_specs=[pl.BlockSpec((n, row_tile, width), lambda i: (0, i, 0))],
        out_specs=pl.BlockSpec((row_tile, width), lambda i: (i, 0)), compiler_params=_cparams("parallel"), name=name)(blocks)


def adamw(g_parts, w, m, v, row_tile, name):
    npart = len(g_parts)

    def body(*refs):
        g = refs[0][...]
        for r in refs[1:npart]:
            g = g + r[...]
        w_ref, m_ref, v_ref, g_out, d_out, m_out, v_out = refs[npart:]
        m_new = ADAM_B1 * m_ref[...] + (1.0 - ADAM_B1) * g
        v_new = ADAM_B2 * v_ref[...] + (1.0 - ADAM_B2) * (g * g)
        m_hat = m_new / (1.0 - ADAM_B1 ** ADAM_STEP)
        v_hat = v_new / (1.0 - ADAM_B2 ** ADAM_STEP)
        g_out[...] = g
        d_out[...] = -ADAM_LR * (m_hat / (jnp.sqrt(v_hat) + ADAM_EPS) + ADAM_WD * w_ref[...])
        m_out[...] = m_new
        v_out[...] = v_new

    rows, width = w.shape
    spec = pl.BlockSpec((row_tile, width), lambda i: (i, 0))
    return pl.pallas_call(
        body, out_shape=[jax.ShapeDtypeStruct(w.shape, F32)] * 4, grid=(rows // row_tile,),
        in_specs=[spec] * (npart + 3), out_specs=[spec] * 4, compiler_params=_cparams("parallel"), name=name)(*g_parts, w, m, v)


def kernel(x, positions, norm_g, w_in, mla_q_a_norm_g, mla_w_q_b, mla_kv_a_norm_g, mla_w_kv_b, mla_q_norm_g, mla_k_norm_g, dn_conv_w, dn_a_log, dn_dt_bias, dn_out_norm_g, dil_q_norm_g, dil_k_norm_g, w_branch, w_out, loss_target, m_norm_g, m_w_in, m_mla_q_a_norm_g, m_mla_w_q_b, m_mla_kv_a_norm_g, m_mla_w_kv_b, m_mla_q_norm_g, m_mla_k_norm_g, m_dn_conv_w, m_dn_a_log, m_dn_dt_bias, m_dn_out_norm_g, m_dil_q_norm_g, m_dil_k_norm_g, m_w_branch, m_w_out, v_norm_g, v_w_in, v_mla_q_a_norm_g, v_mla_w_q_b, v_mla_kv_a_norm_g, v_mla_w_kv_b, v_mla_q_norm_g, v_mla_k_norm_g, v_dn_conv_w, v_dn_a_log, v_dn_dt_bias, v_dn_out_norm_g, v_dil_q_norm_g, v_dil_k_norm_g, v_w_branch, v_w_out):
    w = dict(norm_g=norm_g, w_in=w_in, mla_q_a_norm_g=mla_q_a_norm_g, mla_w_q_b=mla_w_q_b, mla_kv_a_norm_g=mla_kv_a_norm_g,
             mla_w_kv_b=mla_w_kv_b, mla_q_norm_g=mla_q_norm_g, mla_k_norm_g=mla_k_norm_g, dn_conv_w=dn_conv_w, dn_a_log=dn_a_log,
             dn_dt_bias=dn_dt_bias, dn_out_norm_g=dn_out_norm_g, dil_q_norm_g=dil_q_norm_g, dil_k_norm_g=dil_k_norm_g,
             w_branch=w_branch, w_out=w_out)
    m = dict(norm_g=m_norm_g, w_in=m_w_in, mla_q_a_norm_g=m_mla_q_a_norm_g, mla_w_q_b=m_mla_w_q_b, mla_kv_a_norm_g=m_mla_kv_a_norm_g,
             mla_w_kv_b=m_mla_w_kv_b, mla_q_norm_g=m_mla_q_norm_g, mla_k_norm_g=m_mla_k_norm_g, dn_conv_w=m_dn_conv_w,
             dn_a_log=m_dn_a_log, dn_dt_bias=m_dn_dt_bias, dn_out_norm_g=m_dn_out_norm_g, dil_q_norm_g=m_dil_q_norm_g,
             dil_k_norm_g=m_dil_k_norm_g, w_branch=m_w_branch, w_out=m_w_out)
    v = dict(norm_g=v_norm_g, w_in=v_w_in, mla_q_a_norm_g=v_mla_q_a_norm_g, mla_w_q_b=v_mla_w_q_b, mla_kv_a_norm_g=v_mla_kv_a_norm_g,
             mla_w_kv_b=v_mla_w_kv_b, mla_q_norm_g=v_mla_q_norm_g, mla_k_norm_g=v_mla_k_norm_g, dn_conv_w=v_dn_conv_w,
             dn_a_log=v_dn_a_log, dn_dt_bias=v_dn_dt_bias, dn_out_norm_g=v_dn_out_norm_g, dil_q_norm_g=v_dil_q_norm_g,
             dil_k_norm_g=v_dil_k_norm_g, w_branch=v_w_branch, w_out=v_w_out)
    chip = 2 * lax.axis_index("x") + lax.axis_index("y")
    mat_names = [n for n, _ in MATS]
    mat_shapes = tuple(w[n].shape for n in mat_names)
    conv_shard = dn_conv_w.shape
    win_shape = w_in.shape
    win_rows = (win_shape[0] * win_shape[1], win_shape[2])

    mats_sh = _pack([w[n] for n in mat_names], BF16, 2 * MAT_ROWS)
    mat_rows = mats_sh.shape[0]
    w_in4, mats4 = chip_all_gather([w_in.astype(BF16), mats_sh.reshape(2, mat_rows // 2, LANE)])
    mats4 = mats4.reshape(N_CHIPS, mat_rows, LANE)
    conv8 = all_gather8(_pack([dn_conv_w], F32, 8), "gather_conv_w")
    conv_full = jnp.concatenate([_unpack(conv8[2 * j], (conv_shard,))[0] for j in range(4)], axis=2)
    small = {n: w[n] for n in SMALL}
    tabs = _tables(positions[0])

    def loss_fn(w_in4, mats4, conv_full, small, xs):
        return _local_loss(w_in4, _full_from_chips(mats4, mat_shapes), conv_full, small, xs, loss_target[0], tabs)

    loss, (g_win4, g_mats4, g_conv, g_small, g_x) = jax.value_and_grad(loss_fn, argnums=(0, 1, 2, 3, 4))(
        w_in4, mats4, conv_full, small, x[0])
    loss = lax.psum(loss, ("x", "y", "c"))

    c_idx = lax.axis_index("c")
    half_win = (N_CHIPS * win_shape[1], win_shape[2])
    half_mats = (N_CHIPS * (mat_rows // 2), LANE)
    g_mats4 = g_mats4.reshape(N_CHIPS, 2, mat_rows // 2, LANE)
    pick = lambda g, i, shape: lax.dynamic_index_in_dim(g, i, axis=1, keepdims=False).reshape(shape)
    from_sib = sibling_swap([pick(g_win4, 1 - c_idx, half_win), pick(g_mats4, 1 - c_idx, half_mats)])
    s_win = pair_add(pick(g_win4, c_idx, half_win), from_sib[0], WIN_ROWS, "pair_add_w_in")
    s_mats = pair_add(pick(g_mats4, c_idx, half_mats), from_sib[1], MAT_ROWS, "pair_add_mats")
    r_win, r_mats = chip_all_to_all([s_win.reshape(N_CHIPS, win_shape[1], win_shape[2]),
                                     s_mats.reshape(N_CHIPS, mat_rows // 2, LANE)])
    g_win, g_mats = sibling_all_gather([sum_blocks(r_win, WIN_ROWS, "sum_chip_pieces_w_in"),
                                        sum_blocks(r_mats, MAT_ROWS, "sum_chip_pieces_mats")])
    res_win = adamw([g_win.reshape(win_rows)], *[d['w_in'].reshape(win_rows) for d in (w, m, v)], WIN_ROWS, "adamw_w_in")
    packed = [_pack([d[n] for n in mat_names], F32, 2 * MAT_ROWS) for d in (w, m, v)]
    res_big = [dict(zip(mat_names, _unpack(r, mat_shapes)))
               for r in adamw([g_mats.reshape(mat_rows, LANE)], *packed, MAT_ROWS, "adamw_mats")]
    for k in range(4):
        res_big[k]['w_in'] = res_win[k].reshape(win_shape)

    small_shapes = tuple(w[n].shape for n in SMALL) + (g_conv.shape,)
    g_all = sum_blocks(all_gather8(_pack([g_small[n] for n in SMALL] + [g_conv], F32, 8), "gather_small_grads"), 8, "sum_small")
    g_list = list(_unpack(g_all, small_shapes))
    g_list[-1] = lax.dynamic_slice_in_dim(g_list[-1], chip * conv_shard[2], conv_shard[2], axis=2)
    small_names = list(SMALL) + ['dn_conv_w']
    packed_s = [_pack([d[n] for n in small_names], F32, 8) for d in (w, m, v)]
    shapes_s = tuple(w[n].shape for n in small_names)
    res_small = [dict(zip(small_names, _unpack(r, shapes_s))) for r in adamw([_pack(g_list, F32, 8)], *packed_s, 8, "adamw_small")]

    outs = [loss, g_x[None]]
    for k in range(4):
        outs += [res_big[k][n] if n in res_big[k] else res_small[k][n] for n in WEIGHTS]
    return tuple(outs)
```

```python
import functools
import math

import jax
import jax.numpy as jnp
from jax import lax
from jax.experimental import pallas as pl
from jax.experimental.pallas import tpu as pltpu

F32 = jnp.float32
BF16 = jnp.bfloat16
MESH = pl.DeviceIdType.MESH

LANE = 128
VMEM_LIMIT = 48 * 1024 * 1024
ROW_BLOCK = 256
ROW_BLOCK_NARROW = 512
MM_TM, MM_TN, MM_TK = 1024, 1024, 1024
MAT_ROWS = 1664
WIN_ROWS = 128

RMS_EPS = 1e-6
ROPE_THETA = 10000.0
D_MODEL = 1024
DEPTH = 2
MLA_HEADS = 4
MLA_NOPE = 128
MLA_ROPE = 64
MLA_QK = MLA_NOPE + MLA_ROPE
MLA_Q_RANK = 384
MLA_KV_RANK = 256
DN_HEADS = 4
DN_DIM = 128
DN_CONV = 4
GDN_CHUNK = 128
DIL_WINDOWS = (128, 512, 2048)
DIL_DILATIONS = (1, 4, 16)
DIL_GROUPS = 3
DIL_BLOCK = 128
HEAD = 128
BRANCH_W = 512

ADAM_LR = 0.001
ADAM_B1 = 0.9
ADAM_B2 = 0.999
ADAM_EPS = 1e-08
ADAM_WD = 0.01
ADAM_STEP = 10

WEIGHTS = ['norm_g', 'w_in', 'mla_q_a_norm_g', 'mla_w_q_b', 'mla_kv_a_norm_g', 'mla_w_kv_b', 'mla_q_norm_g',
           'mla_k_norm_g', 'dn_conv_w', 'dn_a_log', 'dn_dt_bias', 'dn_out_norm_g', 'dil_q_norm_g', 'dil_k_norm_g',
           'w_branch', 'w_out']
MATS = (('mla_w_q_b', 2), ('mla_w_kv_b', 2), ('w_branch', 3), ('w_out', 1))
SMALL = ('norm_g', 'mla_q_a_norm_g', 'mla_kv_a_norm_g', 'mla_q_norm_g', 'mla_k_norm_g', 'dn_a_log', 'dn_dt_bias',
         'dn_out_norm_g', 'dil_q_norm_g', 'dil_k_norm_g')


def _cparams(*sem):
    return pltpu.CompilerParams(dimension_semantics=sem or None, vmem_limit_bytes=VMEM_LIMIT)


def _tile(dim, target):
    best = 0
    for t in range(LANE, min(dim, target) + 1, LANE):
        if dim % t == 0:
            best = t
    assert best, (dim, target)
    return best


def _mm(a, b, mode, out_dtype, res=None):
    if mode == 'nn':
        (m, k), (k2, n) = a.shape, b.shape
    elif mode == 'nt':
        (m, k), (n, k2) = a.shape, b.shape
    else:
        (k, m), (k2, n) = a.shape, b.shape
    assert k == k2, (a.shape, b.shape, mode)
    tm, tn, tk = _tile(m, MM_TM), _tile(n, MM_TN), _tile(k, MM_TK)
    nk = k // tk
    dims = {'nn': (((1,), (0,)), ((), ())), 'nt': (((1,), (1,)), ((), ())), 'tn': (((0,), (0,)), ((), ()))}[mode]

    def body(*refs):
        if res is None:
            a_ref, b_ref, o_ref = refs[:3]
        else:
            a_ref, b_ref, r_ref, o_ref = refs[:4]
        part = lax.dot_general(a_ref[...].astype(BF16), b_ref[...].astype(BF16), dims, preferred_element_type=F32)

        def finish(r):
            if res is not None:
                r = r + r_ref[...].astype(F32)
            o_ref[...] = r.astype(o_ref.dtype)

        if nk == 1:
            finish(part)
            return
        acc = refs[-1]
        kk = pl.program_id(2)

        @pl.when(kk == 0)
        def _():
            acc[...] = part

        @pl.when(jnp.logical_and(kk > 0, kk < nk - 1))
        def _():
            acc[...] += part

        @pl.when(kk == nk - 1)
        def _():
            finish(acc[...] + part)

    a_spec = pl.BlockSpec((tk, tm), lambda i, j, kk: (kk, i)) if mode == 'tn' else pl.BlockSpec((tm, tk), lambda i, j, kk: (i, kk))
    b_spec = pl.BlockSpec((tn, tk), lambda i, j, kk: (j, kk)) if mode == 'nt' else pl.BlockSpec((tk, tn), lambda i, j, kk: (kk, j))
    o_spec = pl.BlockSpec((tm, tn), lambda i, j, kk: (i, j))
    in_specs = [a_spec, b_spec] + ([o_spec] if res is not None else [])
    args = (a, b) + ((res,) if res is not None else ())
    return pl.pallas_call(
        body, out_shape=jax.ShapeDtypeStruct((m, n), out_dtype), grid=(m // tm, n // tn, nk),
        in_specs=in_specs, out_specs=o_spec, scratch_shapes=[pltpu.VMEM((tm, tn), F32)] if nk > 1 else [],
        compiler_params=_cparams("parallel", "parallel", "arbitrary"),
        name=f"mm_{mode}_{m}x{k}x{n}" + ("_res" if res is not None else ""))(*args)


def _make_matmul(out_dtype, with_res):
    @jax.custom_vjp
    def mm(a, b, *r):
        return _mm(a, b, 'nn', out_dtype, *r)

    def fwd(a, b, *r):
        return mm(a, b, *r), (a, b)

    def bwd(saved, g):
        a, b = saved
        da = _mm(g, b, 'nt', a.dtype)
        db = _mm(a, g, 'tn', b.dtype)
        return (da, db) + ((g,) if with_res else ())

    mm.defvjp(fwd, bwd)
    return mm


@jax.custom_vjp
def matmul_shared(a, bs):
    return tuple(_mm(a, b, 'nn', F32) for b in bs)


def _matmul_shared_bwd(saved, gs):
    a, bs = saved
    da = None
    for i, (g, b) in enumerate(zip(gs, bs)):
        da = _mm(g, b, 'nt', a.dtype if i == len(bs) - 1 else F32, da)
    return da, tuple(_mm(a, g, 'tn', b.dtype) for g, b in zip(gs, bs))


matmul_shared.defvjp(lambda a, bs: (matmul_shared(a, bs), (a, bs)), _matmul_shared_bwd)


def matmul(a, b, out_dtype=F32, res=None):
    if res is None:
        return _make_matmul(out_dtype, False)(a, b)
    return _make_matmul(out_dtype, True)(a, b, res)


def _chunks(ref):
    return [ref[:, c * LANE:(c + 1) * LANE].astype(F32) for c in range(ref.shape[-1] // LANE)]


def _store(ref, chunks):
    for c, ch in enumerate(chunks):
        ref[:, c * LANE:(c + 1) * LANE] = ch.astype(ref.dtype)


def _row_spec(width, br, d=1):
    return pl.BlockSpec((br // d, d * width), lambda i: (i, 0))


def _par_spec(width):
    return pl.BlockSpec((1, width), lambda i: (0, 0))


def _load_rows(ref, scratch, d):
    if d == 1:
        return _chunks(ref)
    n, width = ref.shape[0], ref.shape[1] // d
    for c in range(width // LANE):
        for r in range(d):
            lanes = slice(r * width + c * LANE, r * width + (c + 1) * LANE)
            scratch[c, pl.ds(r, n, stride=d), :] = ref[:, lanes].astype(F32)
    return [scratch[c] for c in range(width // LANE)]


def _store_rows(ref, scratch, d, chunks):
    if d == 1:
        return _store(ref, chunks)
    n, width = ref.shape[0], ref.shape[1] // d
    for c, ch in enumerate(chunks):
        scratch[c] = ch
        for r in range(d):
            lanes = slice(r * width + c * LANE, r * width + (c + 1) * LANE)
            ref[:, lanes] = scratch[c, pl.ds(r, n, stride=d), :].astype(ref.dtype)


def _view_scratch(widths_dils, br):
    return [pltpu.VMEM((w // LANE, br, LANE), F32) for w, d in widths_dils if d > 1]


def _with_scratch(dils, scratch_refs):
    it = iter(scratch_refs)
    return [next(it) if d > 1 else None for d in dils]


def _rw_fwd(f, name, rows, params, outs, br, row_dil, out_dil):
    s = rows[0].shape[0] * row_dil[0]
    br = min(br, s)
    nr, npar, nout = len(rows), len(params), len(outs)
    row_w = [r.shape[1] // d for r, d in zip(rows, row_dil)]

    def body(*refs):
        scr = refs[nr + npar + nout:]
        n_in = sum(d > 1 for d in row_dil)
        rc = [_load_rows(r, sc, d) for r, sc, d in zip(refs[:nr], _with_scratch(row_dil, scr[:n_in]), row_dil)]
        pc = [_chunks(p) for p in refs[nr:nr + npar]]
        res = f(rc, pc)
        for o_ref, sc, d, chs in zip(refs[nr + npar:nr + npar + nout], _with_scratch(out_dil, scr[n_in:]), out_dil, res):
            _store_rows(o_ref, sc, d, chs)

    return pl.pallas_call(
        body, out_shape=[jax.ShapeDtypeStruct((s // d, d * w), dt) for (w, dt), d in zip(outs, out_dil)], grid=(s // br,),
        in_specs=[_row_spec(w, br, d) for w, d in zip(row_w, row_dil)] + [_par_spec(p.shape[1]) for p in params],
        out_specs=[_row_spec(w, br, d) for (w, _), d in zip(outs, out_dil)],
        scratch_shapes=_view_scratch(zip(row_w, row_dil), br) + _view_scratch([(w, d) for (w, _), d in zip(outs, out_dil)], br),
        compiler_params=_cparams("parallel"), name=name + "_fwd")(*rows, *params)


def _rw_bwd(f, name, rows, params, cts, nograd, br, row_dil, out_dil):
    s = rows[0].shape[0] * row_dil[0]
    br = min(br, s)
    nr, npar, nct = len(rows), len(params), len(cts)
    grad_rows = [i for i in range(nr) if i not in nograd]
    row_w = [r.shape[1] // d for r, d in zip(rows, row_dil)]
    ct_w = [c.shape[1] // d for c, d in zip(cts, out_dil)]
    grad_dil = [row_dil[i] for i in grad_rows]

    def body(*refs):
        n_out = len(grad_rows) + npar
        out_refs = refs[nr + npar + nct:nr + npar + nct + n_out]
        scr = refs[nr + npar + nct + n_out:]
        n_in, n_ct = sum(d > 1 for d in row_dil), sum(d > 1 for d in out_dil)
        rc = [_load_rows(r, sc, d) for r, sc, d in zip(refs[:nr], _with_scratch(row_dil, scr[:n_in]), row_dil)]
        pc = [_chunks(p) for p in refs[nr:nr + npar]]
        ct = [_load_rows(c, sc, d) for c, sc, d in
              zip(refs[nr + npar:nr + npar + nct], _with_scratch(out_dil, scr[n_in:n_in + n_ct]), out_dil)]
        _, vjp = jax.vjp(f, rc, pc)
        drc, dpc = vjp(ct)
        for o_ref, sc, d, i in zip(out_refs[:len(grad_rows)], _with_scratch(grad_dil, scr[n_in + n_ct:]), grad_dil, grad_rows):
            _store_rows(o_ref, sc, d, drc[i])
        i0 = pl.program_id(0)
        for o_ref, chs in zip(out_refs[len(grad_rows):], dpc):
            @pl.when(i0 == 0)
            def _(o_ref=o_ref):
                o_ref[...] = jnp.zeros_like(o_ref)
            for c, ch in enumerate(chs):
                o_ref[:, c * LANE:(c + 1) * LANE] += ch

    out_shape = ([jax.ShapeDtypeStruct(rows[i].shape, rows[i].dtype) for i in grad_rows]
                 + [jax.ShapeDtypeStruct(p.shape, F32) for p in params])
    out_specs = [_row_spec(row_w[i], br, row_dil[i]) for i in grad_rows] + [_par_spec(p.shape[1]) for p in params]
    res = pl.pallas_call(
        body, out_shape=out_shape, grid=(s // br,),
        in_specs=([_row_spec(w, br, d) for w, d in zip(row_w, row_dil)] + [_par_spec(p.shape[1]) for p in params]
                  + [_row_spec(w, br, d) for w, d in zip(ct_w, out_dil)]),
        out_specs=out_specs,
        scratch_shapes=(_view_scratch(zip(row_w, row_dil), br) + _view_scratch(zip(ct_w, out_dil), br)
                        + _view_scratch([(row_w[i], row_dil[i]) for i in grad_rows], br)),
        compiler_params=_cparams("arbitrary"), name=name + "_bwd")(*rows, *params, *cts)
    drows = [None] * nr
    for o, i in zip(res[:len(grad_rows)], grad_rows):
        drows[i] = o
    for i in nograd:
        drows[i] = jnp.zeros_like(rows[i])
    return tuple(drows), tuple(res[len(grad_rows):])


def rowwise(f, name, rows, params, outs, nograd=(), br=ROW_BLOCK, row_dil=None, out_dil=None):
    row_dil = tuple(row_dil or [1] * len(rows))
    out_dil = tuple(out_dil or [1] * len(outs))

    @jax.custom_vjp
    def op(rows, params):
        return tuple(_rw_fwd(f, name, rows, params, outs, br, row_dil, out_dil))

    def fwd(rows, params):
        return op(rows, params), (rows, params)

    def bwd(saved, cts):
        rows, params = saved
        return _rw_bwd(f, name, rows, params, list(cts), nograd, br, row_dil, out_dil)

    op.defvjp(fwd, bwd)
    return op(tuple(rows), tuple(params))


def _lane_roll(x, s):
    w = x.shape[-1]

    @jax.custom_vjp
    def r(v):
        return pltpu.roll(v, s, 1)

    r.defvjp(lambda v: (r(v), None), lambda _, g: (pltpu.roll(g, (w - s) % w, 1),))
    return r(x)


def _lanes(x):
    return lax.broadcasted_iota(jnp.int32, x.shape, 1)


def _rms(x, g, n=LANE):
    return x * lax.rsqrt(jnp.sum(x * x, axis=-1, keepdims=True) / n + RMS_EPS) * g


def _rope128(x, cos, sin_signed):
    return x * cos + _lane_roll(x, 64) * sin_signed


def _rope64(x, cos, sin):
    lane = _lanes(x)
    rot = jnp.where(lane < 32, -_lane_roll(x, 96), jnp.where(lane < 64, _lane_roll(x, 32), 0.0))
    return x * cos + rot * sin


def _silu(z):
    return z * jax.nn.sigmoid(z)


def _softplus(x):
    return jnp.maximum(x, 0.0) + jnp.log1p(jnp.exp(-jnp.abs(x)))


def f_rms_full(rc, pc):
    x, g = rc[0], pc[0]
    width = len(x) * LANE
    ms = sum(jnp.sum(c * c, axis=-1, keepdims=True) for c in x) / width
    r = lax.rsqrt(ms + RMS_EPS)
    return [[c * r * gc for c, gc in zip(x, g)]]


def f_mla1(rc, pc):
    x = rc[0]
    qn = f_rms_full([x[0:3]], [pc[0]])[0]
    kvn = f_rms_full([x[3:5]], [pc[1]])[0]
    return [qn, kvn, [x[5]]]


def f_mla2(rc, pc):
    q, kv, kpe, cos, sin = rc[0], rc[1], rc[2][0], rc[3][0], rc[4][0]
    gqn, gqp, gkn, gkp = pc[0][0], pc[1][0], pc[2][0], pc[3][0]
    k_pe = _rope64(_rms(kpe, gkp, MLA_ROPE), cos, sin)
    q_att, k_att, v = [], [], []
    for h in range(MLA_HEADS):
        q_att += [_rms(q[h], gqn), _rope64(_rms(q[MLA_HEADS + h], gqp, MLA_ROPE), cos, sin)]
        k_att += [_rms(kv[2 * h], gkn), k_pe]
        v.append(kv[2 * h + 1])
    return [q_att, k_att, v]


def f_gates(rc, pc):
    x, a_log, dt_bias = rc[0][0], pc[0][0], pc[1][0]
    lane = _lanes(x)
    g = -jnp.exp(a_log) * _softplus(x + dt_bias)
    return [[jnp.where(lane < DN_HEADS, g, jnp.where(lane < 2 * DN_HEADS, jax.nn.sigmoid(x), 0.0))]]


def f_headnorm(rc, pc):
    return [[_rms(c, pc[0][0]) for c in rc[0]]]


def f_dil(rc, pc):
    x, cos, sin = rc[0], rc[1][0], rc[2][0]
    gq, gk = pc[0][0], pc[1][0]
    n = len(x) // 3
    q = [_rope128(_rms(c, gq), cos, sin) for c in x[:n]]
    k = [_rope128(_rms(c, gk), cos, sin) for c in x[n:2 * n]]
    v = list(x[2 * n:])
    per = n // DIL_GROUPS
    return [t[g * per:(g + 1) * per] for g in range(DIL_GROUPS) for t in (q, k, v)]


def f_comb(rc, pc):
    o, l = rc[:DIL_GROUPS], rc[DIL_GROUPS:]
    out = []
    for c in range(len(o[0])):
        m = functools.reduce(jnp.maximum, [lg[c] for lg in l])
        e = [jnp.exp(lg[c] - m) for lg in l]
        den = sum(e)
        out.append(sum(eg * og[c] for eg, og in zip(e, o)) / den)
    return [out]


def f_merge1(rc, pc):
    z = rc[3]
    n = len(rc[0])
    return [[y * _silu(z[b * n + c]) for c, y in enumerate(rc[b])] for b in range(3)]


def f_merge2(rc, pc):
    gate = rc[0]
    n = len(rc[1])
    return [[sum(jax.nn.sigmoid(gate[b * n + c]) * rc[1 + b][c] for b in range(3)) for c in range(n)]]


MLA_DQK = 2 * LANE
MLA_SCALE = MLA_QK ** -0.5


def _mla_attn_fwd(q, k, v):
    s = q.shape[0]
    h = q.shape[1] // MLA_DQK
    t = min(512, s)
    kw = 4 if s % (4 * t) == 0 else 2
    tk = kw * t
    n, nkv = s // t, s // tk

    def body(q_ref, k_ref, v_ref, o_ref, lse_ref, m_sc, l_sc, acc_sc):
        qi, kj = pl.program_id(1), pl.program_id(2)
        last = qi // kw

        @pl.when(kj == 0)
        def _():
            m_sc[...] = jnp.full_like(m_sc, -jnp.inf)
            l_sc[...] = jnp.zeros_like(l_sc)
            acc_sc[...] = jnp.zeros_like(acc_sc)

        nsub = 2 if t % 256 == 0 else 1
        ts = t // nsub
        rows = [slice(r * ts, (r + 1) * ts) for r in range(nsub)]

        def step(on_diagonal):
            sc = [lax.dot_general(q_ref[rw, :], k_ref[...], (((1,), (1,)), ((), ())), preferred_element_type=F32) * MLA_SCALE
                  for rw in rows]
            if on_diagonal:
                gap = lax.broadcasted_iota(jnp.int32, (ts, tk), 1) - lax.broadcasted_iota(jnp.int32, (ts, tk), 0)
                sc = [jnp.where(gap <= qi * t + r * ts - kj * tk, x, -jnp.inf) for r, x in enumerate(sc)]
            m_old = [m_sc[rw, :] for rw in rows]
            m_new = [jnp.maximum(mo, jnp.max(x, axis=-1, keepdims=True)) for mo, x in zip(m_old, sc)]
            alpha = [jnp.exp(mo - mn) for mo, mn in zip(m_old, m_new)]
            p = [jnp.exp(x - mn) for x, mn in zip(sc, m_new)]
            pv = [jnp.dot(x.astype(BF16), v_ref[...], preferred_element_type=F32) for x in p]
            for r, rw in enumerate(rows):
                l_sc[rw, :] = alpha[r] * l_sc[rw, :] + jnp.sum(p[r], axis=-1, keepdims=True)
                acc_sc[rw, :] = alpha[r] * acc_sc[rw, :] + pv[r]
                m_sc[rw, :] = m_new[r]

        @pl.when(kj < last)
        def _():
            step(False)

        @pl.when(kj == last)
        def _():
            step(True)

        @pl.when(kj == nkv - 1)
        def _():
            o_ref[...] = acc_sc[...] / l_sc[...]
            lse_ref[...] = jnp.broadcast_to(m_sc[...] + jnp.log(l_sc[...]), lse_ref.shape)

    return pl.pallas_call(
        body, out_shape=[jax.ShapeDtypeStruct((s, h * HEAD), F32)] * 2, grid=(h, n, nkv),
        in_specs=[pl.BlockSpec((t, MLA_DQK), lambda hh, i, j: (i, hh)),
                  pl.BlockSpec((tk, MLA_DQK), lambda hh, i, j: (jnp.minimum(j, i // kw), hh)),
                  pl.BlockSpec((tk, HEAD), lambda hh, i, j: (jnp.minimum(j, i // kw), hh))],
        out_specs=[pl.BlockSpec((t, HEAD), lambda hh, i, j: (i, hh))] * 2,
        scratch_shapes=[pltpu.VMEM((t, 1), F32), pltpu.VMEM((t, 1), F32), pltpu.VMEM((t, HEAD), F32)],
        compiler_params=_cparams("parallel", "parallel", "arbitrary"), name="mla_attn_fwd")(q, k, v)


def _mla_attn_bwd(q, k, v, o, lse, do):
    s = q.shape[0]
    h = q.shape[1] // MLA_DQK
    t = min(512, s)
    tk = 2 * t
    n, nkv = s // t, s // tk
    nt, tn = (((1,), (1,)), ((), ())), (((0,), (0,)), ((), ()))

    def body(q_ref, k_ref, v_ref, o_ref, lse_ref, do_ref, dq_ref, dk_ref, dv_ref, dq_sc, dk_sc, dv_sc):
        kj, qi = pl.program_id(1), pl.program_id(2)
        last = qi // 2

        @pl.when(jnp.logical_and(kj == 0, qi == 0))
        def _():
            dq_sc[...] = jnp.zeros_like(dq_sc)

        @pl.when(qi == 0)
        def _():
            dk_sc[...] = jnp.zeros_like(dk_sc)
            dv_sc[...] = jnp.zeros_like(dv_sc)

        def pair(on_diagonal):
            sc = lax.dot_general(q_ref[...], k_ref[...], nt, preferred_element_type=F32) * MLA_SCALE
            p = jnp.exp(sc - lse_ref[:, 0:1])
            if on_diagonal:
                gap = lax.broadcasted_iota(jnp.int32, (t, tk), 1) - lax.broadcasted_iota(jnp.int32, (t, tk), 0)
                p = jnp.where(gap <= qi * t - kj * tk, p, 0.0)
            do_v = do_ref[...]
            do_b = do_v.astype(BF16)
            dp = lax.dot_general(do_b, v_ref[...], nt, preferred_element_type=F32)
            delta = jnp.sum(do_v * o_ref[...], axis=-1, keepdims=True)
            ds = (p * (dp - delta) * MLA_SCALE).astype(BF16)
            dv_sc[...] += lax.dot_general(p.astype(BF16), do_b, tn, preferred_element_type=F32)
            dk_sc[...] += lax.dot_general(ds, q_ref[...], tn, preferred_element_type=F32)
            rows = pl.ds(pl.multiple_of(qi * t, t), t)
            dq_sc[rows, :] += jnp.dot(ds, k_ref[...], preferred_element_type=F32)

        @pl.when(last > kj)
        def _():
            pair(False)

        @pl.when(last == kj)
        def _():
            pair(True)

        @pl.when(qi == n - 1)
        def _():
            dk_ref[...] = dk_sc[...].astype(dk_ref.dtype)
            dv_ref[...] = dv_sc[...].astype(dv_ref.dtype)

        @pl.when(jnp.logical_and(kj == nkv - 1, qi == n - 1))
        def _():
            dq_ref[...] = dq_sc[...].astype(dq_ref.dtype)

    qmap = lambda hh, j, i: (jnp.maximum(i, 2 * j), hh)
    kmap = lambda hh, j, i: (j, hh)
    return pl.pallas_call(
        body, out_shape=[jax.ShapeDtypeStruct(q.shape, BF16), jax.ShapeDtypeStruct(k.shape, BF16), jax.ShapeDtypeStruct(v.shape, BF16)],
        grid=(h, nkv, n),
        in_specs=[pl.BlockSpec((t, MLA_DQK), qmap), pl.BlockSpec((tk, MLA_DQK), kmap), pl.BlockSpec((tk, HEAD), kmap),
                  pl.BlockSpec((t, HEAD), qmap), pl.BlockSpec((t, HEAD), qmap), pl.BlockSpec((t, HEAD), qmap)],
        out_specs=[pl.BlockSpec((s, MLA_DQK), lambda hh, j, i: (0, hh)), pl.BlockSpec((tk, MLA_DQK), kmap),
                   pl.BlockSpec((tk, HEAD), kmap)],
        scratch_shapes=[pltpu.VMEM((s, MLA_DQK), F32), pltpu.VMEM((tk, MLA_DQK), F32), pltpu.VMEM((tk, HEAD), F32)],
        compiler_params=_cparams("parallel", "arbitrary", "arbitrary"), name="mla_attn_bwd")(q, k, v, o, lse, do)


@jax.custom_vjp
def mla_attention(q, k, v):
    return _mla_attn_fwd(q, k, v)[0]


def _mla_attention_fwd(q, k, v):
    o, lse = _mla_attn_fwd(q, k, v)
    return o, (q, k, v, o, lse)


def _mla_attention_bwd(saved, do):
    return tuple(_mla_attn_bwd(*saved, do))


mla_attention.defvjp(_mla_attention_fwd, _mla_attention_bwd)


DIL_SCALE = HEAD ** -0.5
GROUP_W = 4 * HEAD


def _dil_scores(q, kp, kc, n):
    dn = (((1,), (1,)), ((), ()))
    sp = lax.dot_general(q, kp, dn, preferred_element_type=F32) * DIL_SCALE
    sc = lax.dot_general(q, kc, dn, preferred_element_type=F32) * DIL_SCALE
    qi = lax.broadcasted_iota(jnp.int32, sp.shape, 0)
    kc_i = lax.broadcasted_iota(jnp.int32, sp.shape, 1)
    vp = jnp.logical_and(kc_i >= qi, n > 0)
    vc = kc_i <= qi
    return sp, sc, vp, vc


def _dil_specs(d):
    cur = pl.BlockSpec((DIL_BLOCK, GROUP_W), lambda r, n: (n, r))
    prev = pl.BlockSpec((DIL_BLOCK, GROUP_W), lambda r, n: (jnp.maximum(n - 1, 0), r))
    return cur, prev


def _dil_fwd(q, k, v, d):
    l = q.shape[0]
    nb = l // DIL_BLOCK
    cur, prev = _dil_specs(d)

    def body(q_ref, kp_ref, kc_ref, vp_ref, vc_ref, o_ref, lse_ref):
        n = pl.program_id(1)
        heads = range(4)
        sl = [slice(h * HEAD, (h + 1) * HEAD) for h in heads]
        scores = [_dil_scores(q_ref[:, sl[h]], kp_ref[:, sl[h]], kc_ref[:, sl[h]], n) for h in heads]
        sp = [jnp.where(vp, s_p, -jnp.inf) for s_p, _, vp, _ in scores]
        sc = [jnp.where(vc, s_c, -jnp.inf) for _, s_c, _, vc in scores]
        m = [jnp.maximum(jnp.max(sp[h], axis=-1, keepdims=True), jnp.max(sc[h], axis=-1, keepdims=True)) for h in heads]
        ep = [jnp.exp(sp[h] - m[h]) for h in heads]
        ec = [jnp.exp(sc[h] - m[h]) for h in heads]
        den = [jnp.sum(ep[h], axis=-1, keepdims=True) + jnp.sum(ec[h], axis=-1, keepdims=True) for h in heads]
        acc = [jnp.dot(ep[h].astype(BF16), vp_ref[:, sl[h]], preferred_element_type=F32)
               + jnp.dot(ec[h].astype(BF16), vc_ref[:, sl[h]], preferred_element_type=F32) for h in heads]
        for h in heads:
            o_ref[:, sl[h]] = acc[h] / den[h]
            lse_ref[:, sl[h]] = jnp.broadcast_to(m[h] + jnp.log(den[h]), (DIL_BLOCK, HEAD))

    return pl.pallas_call(
        body, out_shape=[jax.ShapeDtypeStruct(q.shape, F32)] * 2, grid=(d, nb),
        in_specs=[cur, prev, cur, prev, cur], out_specs=[cur, cur],
        compiler_params=_cparams("parallel", "parallel"), name=f"dil_fwd_d{d}")(q, k, k, v, v)


def _dil_bwd(q, k, v, o, lse, do, dlse, d):
    l = q.shape[0]
    nb = l // DIL_BLOCK
    tn = (((0,), (0,)), ((), ()))
    nt = (((1,), (1,)), ((), ()))
    cur = pl.BlockSpec((DIL_BLOCK, GROUP_W), lambda r, n: (jnp.minimum(n, nb - 1), r))
    prev = pl.BlockSpec((DIL_BLOCK, GROUP_W), lambda r, n: (jnp.maximum(jnp.minimum(n, nb - 1) - 1, 0), r))
    lag = pl.BlockSpec((DIL_BLOCK, GROUP_W), lambda r, n: (jnp.maximum(n - 1, 0), r))

    def body(q_ref, kp_ref, kc_ref, vp_ref, vc_ref, o_ref, lse_ref, do_ref, dl_ref, dq_ref, dk_ref, dv_ref, ck_sc, cv_sc):
        n = pl.program_id(1)

        @pl.when(n < nb)
        def _():
            heads = range(4)
            sl = [slice(h * HEAD, (h + 1) * HEAD) for h in heads]
            scores = [_dil_scores(q_ref[:, sl[h]], kp_ref[:, sl[h]], kc_ref[:, sl[h]], n) for h in heads]
            lse = [lse_ref[:, h * HEAD:h * HEAD + 1] for h in heads]
            pp = [jnp.where(scores[h][2], jnp.exp(scores[h][0] - lse[h]), 0.0) for h in heads]
            pc = [jnp.where(scores[h][3], jnp.exp(scores[h][1] - lse[h]), 0.0) for h in heads]
            do_b = [do_ref[:, sl[h]].astype(BF16) for h in heads]
            corr = [jnp.sum(dl_ref[:, sl[h]], axis=-1, keepdims=True)
                    - jnp.sum(do_ref[:, sl[h]] * o_ref[:, sl[h]], axis=-1, keepdims=True) for h in heads]
            dsp = [(pp[h] * (lax.dot_general(do_b[h], vp_ref[:, sl[h]], nt, preferred_element_type=F32) + corr[h])
                    * DIL_SCALE).astype(BF16) for h in heads]
            dsc = [(pc[h] * (lax.dot_general(do_b[h], vc_ref[:, sl[h]], nt, preferred_element_type=F32) + corr[h])
                    * DIL_SCALE).astype(BF16) for h in heads]
            dkp = [lax.dot_general(dsp[h], q_ref[:, sl[h]], tn, preferred_element_type=F32) for h in heads]
            dvp = [lax.dot_general(pp[h].astype(BF16), do_b[h], tn, preferred_element_type=F32) for h in heads]
            for h in heads:
                dq_ref[:, sl[h]] = (jnp.dot(dsp[h], kp_ref[:, sl[h]], preferred_element_type=F32)
                                    + jnp.dot(dsc[h], kc_ref[:, sl[h]], preferred_element_type=F32)).astype(dq_ref.dtype)

            @pl.when(n > 0)
            def _():
                for h in heads:
                    dk_ref[:, sl[h]] = (ck_sc[:, sl[h]] + dkp[h]).astype(dk_ref.dtype)
                    dv_ref[:, sl[h]] = (cv_sc[:, sl[h]] + dvp[h]).astype(dv_ref.dtype)

            for h in heads:
                ck_sc[:, sl[h]] = lax.dot_general(dsc[h], q_ref[:, sl[h]], tn, preferred_element_type=F32)
                cv_sc[:, sl[h]] = lax.dot_general(pc[h].astype(BF16), do_b[h], tn, preferred_element_type=F32)

        @pl.when(n == nb)
        def _():
            dk_ref[...] = ck_sc[...].astype(dk_ref.dtype)
            dv_ref[...] = cv_sc[...].astype(dv_ref.dtype)

    return pl.pallas_call(
        body, out_shape=[jax.ShapeDtypeStruct(q.shape, BF16)] * 3, grid=(d, nb + 1),
        in_specs=[cur, prev, cur, prev, cur, cur, cur, cur, cur], out_specs=[cur, lag, lag],
        scratch_shapes=[pltpu.VMEM((DIL_BLOCK, GROUP_W), F32)] * 2,
        compiler_params=_cparams("parallel", "arbitrary"), name=f"dil_bwd_d{d}")(q, k, k, v, v, o, lse, do, dlse)


def _make_dil(d):
    @jax.custom_vjp
    def att(q, k, v):
        return tuple(_dil_fwd(q, k, v, d))

    def fwd(q, k, v):
        o, lse = _dil_fwd(q, k, v, d)
        return (o, lse), (q, k, v, o, lse)

    def bwd(saved, cts):
        return tuple(_dil_bwd(*saved, cts[0], cts[1], d))

    att.defvjp(fwd, bwd)
    return att


def dilated_group(q, k, v, d):
    return _make_dil(d)(q, k, v)


def _dn_post(c, kind, scale):
    m = _silu(c)
    nrm = m * lax.rsqrt(jnp.sum(m * m, axis=-1, keepdims=True) + 1e-6) * scale
    return kind * nrm + (1.0 - kind) * m


def _dn_kind_scale(j):
    kind = jnp.where(j < 2 * DN_HEADS, 1.0, 0.0).astype(F32)
    scale = jnp.where(j < DN_HEADS, DN_DIM ** -0.5, 1.0).astype(F32)
    return kind, scale


_CONV_RB = 512
_CONV_PAD = 8


def _conv_windows(pad_ref, w_ref, r0, rb, sign):
    acc = None
    for sh in range(DN_CONV):
        win = pad_ref[pl.ds(r0 + _CONV_PAD * (sign < 0) + sign * sh, rb), :]
        term = w_ref[DN_CONV - 1 - sh:DN_CONV - sh, :] * win
        acc = term if acc is None else acc + term
    return acc


def _dn_conv_fwd(x, w):
    s, width = x.shape
    rb = min(_CONV_RB, s)

    def body(x_ref, w_ref, o_ref, pad_ref):
        kind, scale = _dn_kind_scale(pl.program_id(0))
        pad_ref[0:_CONV_PAD, :] = jnp.zeros((_CONV_PAD, LANE), F32)
        pad_ref[_CONV_PAD:, :] = x_ref[...]
        for r0 in range(0, s, rb):
            c = _conv_windows(pad_ref, w_ref, r0, rb, -1)
            o_ref[r0:r0 + rb, :] = _dn_post(c, kind, scale)

    return pl.pallas_call(
        body, out_shape=jax.ShapeDtypeStruct(x.shape, F32), grid=(width // LANE,),
        in_specs=[pl.BlockSpec((s, LANE), lambda j: (0, j)), pl.BlockSpec((DN_CONV, LANE), lambda j: (0, j))],
        out_specs=pl.BlockSpec((s, LANE), lambda j: (0, j)), scratch_shapes=[pltpu.VMEM((s + _CONV_PAD, LANE), F32)],
        compiler_params=_cparams("parallel"), name="dn_conv_fwd")(x, w)


def _dn_conv_bwd(x, w, dy):
    s, width = x.shape
    rb = min(_CONV_RB, s)

    def body(x_ref, w_ref, dy_ref, dx_ref, dw_ref, pad_ref, dpad_ref):
        kind, scale = _dn_kind_scale(pl.program_id(0))
        pad_ref[0:_CONV_PAD, :] = jnp.zeros((_CONV_PAD, LANE), F32)
        pad_ref[_CONV_PAD:, :] = x_ref[...]
        dpad_ref[s:, :] = jnp.zeros((_CONV_PAD, LANE), F32)
        dws = [jnp.zeros((1, LANE), F32) for _ in range(DN_CONV)]
        for r0 in range(0, s, rb):
            c = _conv_windows(pad_ref, w_ref, r0, rb, -1)
            _, vjp = jax.vjp(lambda cc: _dn_post(cc, kind, scale), c)
            dc = vjp(dy_ref[r0:r0 + rb, :])[0]
            dpad_ref[r0:r0 + rb, :] = dc
            for sh in range(DN_CONV):
                win = pad_ref[pl.ds(r0 + _CONV_PAD - sh, rb), :]
                dws[DN_CONV - 1 - sh] = dws[DN_CONV - 1 - sh] + jnp.sum(dc * win, axis=0, keepdims=True)
        for j in range(DN_CONV):
            dw_ref[j:j + 1, :] = dws[j]
        for r0 in range(0, s, rb):
            dx_ref[r0:r0 + rb, :] = _conv_windows(dpad_ref, w_ref, r0, rb, 1)

    return pl.pallas_call(
        body, out_shape=[jax.ShapeDtypeStruct(x.shape, F32), jax.ShapeDtypeStruct(w.shape, F32)], grid=(width // LANE,),
        in_specs=[pl.BlockSpec((s, LANE), lambda j: (0, j)), pl.BlockSpec((DN_CONV, LANE), lambda j: (0, j)),
                  pl.BlockSpec((s, LANE), lambda j: (0, j))],
        out_specs=[pl.BlockSpec((s, LANE), lambda j: (0, j)), pl.BlockSpec((DN_CONV, LANE), lambda j: (0, j))],
        scratch_shapes=[pltpu.VMEM((s + _CONV_PAD, LANE), F32), pltpu.VMEM((s + _CONV_PAD, LANE), F32)],
        compiler_params=_cparams("parallel"), name="dn_conv_bwd")(x, w, dy)


@jax.custom_vjp
def dn_conv(x, w):
    return _dn_conv_fwd(x, w)


dn_conv.defvjp(lambda x, w: (_dn_conv_fwd(x, w), (x, w)), lambda saved, dy: tuple(_dn_conv_bwd(*saved, dy)))


_NN = (((1,), (0,)), ((), ()))
_NT = (((1,), (1,)), ((), ()))
_TN = (((0,), (0,)), ((), ()))


def _bd(a, b, dims, passes=1):
    d = lambda x, y: lax.dot_general(x, y, dims, preferred_element_type=F32)
    ah, bh = a.astype(BF16), b.astype(BF16)
    if passes == 1:
        return d(ah, bh)
    al, bl = (a - ah.astype(F32)).astype(BF16), (b - bh.astype(F32)).astype(BF16)
    return d(ah, bh) + d(ah, bl) + d(al, bh)


@functools.partial(jax.custom_vjp, nondiff_argnums=(2, 3))
def _pdot(a, b, dims, passes):
    return _bd(a, b, dims, passes)


def _pdot_bwd(dims, passes, saved, g):
    a, b = saved
    if dims == _NN:
        return _bd(g, b, _NT, passes), _bd(a, g, _TN, passes)
    if dims == _NT:
        return _bd(g, b, _NN, passes), _bd(g, a, _TN, passes)
    return _bd(b, g, _NT, passes), _bd(a, g, _NN, passes)


_pdot.defvjp(lambda a, b, dims, passes: (_bd(a, b, dims, passes), (a, b)), _pdot_bwd)


GDN_DOT_PASSES = 1
GDN_SOLVE_PASSES = 3


def _hdot(a, b, dims=_NN):
    return _pdot(a, b, dims, GDN_DOT_PASSES)


def _xdot(a, b, dims=_NN):
    return _pdot(a, b, dims, GDN_SOLVE_PASSES)


def _split3(x):
    hi = x.astype(BF16)
    r1 = x - hi.astype(F32)
    mid = r1.astype(BF16)
    lo = (r1 - mid.astype(F32)).astype(BF16)
    return hi, mid, lo


def _tri_dot(tri, x, dims):
    t = tri.astype(BF16)
    return sum(lax.dot_general(t, p, dims, preferred_element_type=F32) for p in _split3(x))


@jax.custom_vjp
def _cumsum_rows(x):
    c = x.shape[0]
    tri = lax.broadcasted_iota(jnp.int32, (c, c), 0) >= lax.broadcasted_iota(jnp.int32, (c, c), 1)
    return _tri_dot(tri, x, _NN)


def _cumsum_rows_bwd(_, g):
    c = g.shape[0]
    tri = lax.broadcasted_iota(jnp.int32, (c, c), 0) >= lax.broadcasted_iota(jnp.int32, (c, c), 1)
    return (_tri_dot(tri, g, _TN),)


_cumsum_rows.defvjp(lambda x: (_cumsum_rows(x), None), _cumsum_rows_bwd)


@jax.custom_vjp
def _unit_lower_inverses(a):
    c = a[0].shape[0]
    eye = (lax.broadcasted_iota(jnp.int32, (c, c), 0) == lax.broadcasted_iota(jnp.int32, (c, c), 1)).astype(F32)
    pw = [-x for x in a]
    t = [eye + p for p in pw]
    for _ in range(int(math.log2(c)) - 1):
        pw = [_bd(p, p, _NN, GDN_SOLVE_PASSES) for p in pw]
        t = [x + _bd(x, p, _NN, GDN_SOLVE_PASSES) for x, p in zip(t, pw)]
    return t


def _unit_lower_inverses_bwd(t, g):
    left = [_bd(x, y, _TN, GDN_SOLVE_PASSES) for x, y in zip(t, g)]
    return ([-_bd(x, y, _NT, GDN_SOLVE_PASSES) for x, y in zip(left, t)],)


_unit_lower_inverses.defvjp(lambda a: (lambda t: (t, t))(_unit_lower_inverses(a)), _unit_lower_inverses_bwd)


@jax.custom_vjp
def _kept_inverses(a, t):
    return t


_kept_inverses.defvjp(lambda a, t: (t, t),
                      lambda t, g: (_unit_lower_inverses_bwd(t, g)[0], [jnp.zeros_like(x) for x in t]))


def _gdn_prep_fn(qkv, gb, kept=None, keep=None):
    c = GDN_CHUNK
    row = lax.broadcasted_iota(jnp.int32, (c, c), 0)
    col = lax.broadcasted_iota(jnp.int32, (c, c), 1)
    incl, strict = row >= col, row > col
    lane = _lanes(gb)
    heads = range(DN_HEADS)
    q, k, v = qkv[:DN_HEADS], qkv[DN_HEADS:2 * DN_HEADS], qkv[2 * DN_HEADS:]
    g = [jnp.sum(jnp.where(lane == h, gb, 0.0), axis=-1, keepdims=True) for h in heads]
    beta = [jnp.sum(jnp.where(lane == DN_HEADS + h, gb, 0.0), axis=-1, keepdims=True) for h in heads]
    gcb = [_cumsum_rows(jnp.broadcast_to(g[h], (c, c))) for h in heads]
    decay = [jnp.where(incl, jnp.exp(jnp.where(incl, gcb[h] - gcb[h].T, 0.0)), 0.0) for h in heads]
    kb = [k[h] * beta[h] for h in heads]
    a = [jnp.where(strict, _hdot(kb[h], k[h], _NT) * decay[h], 0.0) for h in heads]
    t = _unit_lower_inverses(a) if kept is None else _kept_inverses(a, kept)
    if keep is not None:
        keep.extend(t)
    eg = [jnp.exp(gcb[h]) for h in heads]
    u = [_xdot(t[h], v[h] * beta[h]) for h in heads]
    w = [_xdot(t[h], kb[h] * eg[h]) for h in heads]
    qk = [jnp.where(incl, _hdot(q[h], k[h], _NT) * decay[h], 0.0) for h in heads]
    g_last = [jnp.sum(jnp.where(row == c - 1, gcb[h], 0.0), axis=0, keepdims=True) for h in heads]
    kg = [k[h] * jnp.exp(g_last[h] - gcb[h]) for h in heads]
    qg = [q[h] * eg[h] for h in heads]
    el = [jnp.broadcast_to(jnp.exp(g_last[h]), (8, LANE)) for h in heads]
    return [u, w, qg, kg, qk, el]


def _gdn_prep_fwd(qkv, gb):
    s = qkv.shape[0]
    c = GDN_CHUNK
    n = s // c
    hw = DN_HEADS * LANE

    def body(qkv_ref, gb_ref, u_ref, w_ref, qg_ref, kg_ref, qk_ref, el_ref, t_ref):
        kept = []
        res = _gdn_prep_fn(_chunks(qkv_ref), gb_ref[...], keep=kept)
        for ref, chs in zip((u_ref, w_ref, qg_ref, kg_ref, qk_ref, el_ref, t_ref), res + [kept]):
            _store(ref, chs)

    row = pl.BlockSpec((c, hw), lambda i: (i, 0))
    return pl.pallas_call(
        body, out_shape=[jax.ShapeDtypeStruct((s, hw), F32)] * 5 + [jax.ShapeDtypeStruct((n * 8, hw), F32),
                                                                   jax.ShapeDtypeStruct((s, hw), F32)], grid=(n,),
        in_specs=[pl.BlockSpec((c, 3 * hw), lambda i: (i, 0)), pl.BlockSpec((c, LANE), lambda i: (i, 0))],
        out_specs=[row] * 5 + [pl.BlockSpec((8, hw), lambda i: (i, 0)), row],
        compiler_params=_cparams("parallel"), name="gdn_prep_fwd")(qkv, gb)


def _gdn_prep_bwd(qkv, gb, t, cts):
    s = qkv.shape[0]
    c = GDN_CHUNK
    n = s // c
    hw = DN_HEADS * LANE

    def body(qkv_ref, gb_ref, t_ref, du_ref, dw_ref, dqg_ref, dkg_ref, dqk_ref, del_ref, dqkv_ref, dgb_ref):
        kept = _chunks(t_ref)
        _, vjp = jax.vjp(lambda x, y: _gdn_prep_fn(x, y, kept=kept), _chunks(qkv_ref), gb_ref[...])
        ct = [_chunks(r) for r in (du_ref, dw_ref, dqg_ref, dkg_ref, dqk_ref, del_ref)]
        dqkv, dgb = vjp(ct)
        _store(dqkv_ref, dqkv)
        dgb_ref[...] = dgb

    row = pl.BlockSpec((c, hw), lambda i: (i, 0))
    return pl.pallas_call(
        body, out_shape=[jax.ShapeDtypeStruct(qkv.shape, F32), jax.ShapeDtypeStruct(gb.shape, F32)], grid=(n,),
        in_specs=[pl.BlockSpec((c, 3 * hw), lambda i: (i, 0)), pl.BlockSpec((c, LANE), lambda i: (i, 0)), row]
        + [row] * 5 + [pl.BlockSpec((8, hw), lambda i: (i, 0))],
        out_specs=[pl.BlockSpec((c, 3 * hw), lambda i: (i, 0)), pl.BlockSpec((c, LANE), lambda i: (i, 0))],
        compiler_params=_cparams("parallel"), name="gdn_prep_bwd")(qkv, gb, t, *cts)


@jax.custom_vjp
def gdn_prep(qkv, gb):
    return tuple(_gdn_prep_fwd(qkv, gb)[:6])


def _gdn_prep_vfwd(qkv, gb):
    res = _gdn_prep_fwd(qkv, gb)
    return tuple(res[:6]), (qkv, gb, res[6])


gdn_prep.defvjp(_gdn_prep_vfwd, lambda saved, cts: tuple(_gdn_prep_bwd(*saved, cts)))


def _gdn_scan_fwd(u, w, qg, kg, qk, el):
    s = u.shape[0]
    c = GDN_CHUNK
    n = s // c
    hw = DN_HEADS * LANE

    def body(u_ref, w_ref, qg_ref, kg_ref, qk_ref, el_ref, o_ref, st_ref, s_sc):
        @pl.when(pl.program_id(0) == 0)
        def _():
            s_sc[...] = jnp.zeros_like(s_sc)

        heads = range(DN_HEADS)
        sl = [slice(h * LANE, (h + 1) * LANE) for h in heads]
        st = [s_sc[h] for h in heads]
        for h in heads:
            st_ref[sl[h], :] = st[h]
        v_new = [u_ref[:, sl[h]] - _hdot(w_ref[:, sl[h]], st[h]) for h in heads]
        o_st = [_hdot(qg_ref[:, sl[h]], st[h]) for h in heads]
        o_in = [_hdot(qk_ref[:, sl[h]], v_new[h]) for h in heads]
        s_up = [_hdot(kg_ref[:, sl[h]], v_new[h], _TN) for h in heads]
        for h in heads:
            o_ref[:, sl[h]] = o_st[h] + o_in[h]
            s_sc[h] = st[h] * el_ref[0:1, sl[h]] + s_up[h]

    row = pl.BlockSpec((c, hw), lambda i: (i, 0))
    return pl.pallas_call(
        body, out_shape=[jax.ShapeDtypeStruct((s, hw), F32), jax.ShapeDtypeStruct((n, hw, LANE), F32)], grid=(n,),
        in_specs=[row] * 5 + [pl.BlockSpec((8, hw), lambda i: (i, 0))],
        out_specs=[row, pl.BlockSpec((None, hw, LANE), lambda i: (i, 0, 0))],
        scratch_shapes=[pltpu.VMEM((DN_HEADS, LANE, LANE), F32)],
        compiler_params=_cparams("arbitrary"), name="gdn_scan_fwd")(u, w, qg, kg, qk, el)


def _gdn_scan_bwd(u, w, qg, kg, qk, el, states, do):
    s = u.shape[0]
    c = GDN_CHUNK
    n = s // c
    hw = DN_HEADS * LANE

    def body(u_ref, w_ref, qg_ref, kg_ref, qk_ref, el_ref, st_ref, do_ref,
             du_ref, dw_ref, dqg_ref, dkg_ref, dqk_ref, del_ref, ds_sc):
        @pl.when(pl.program_id(0) == 0)
        def _():
            ds_sc[...] = jnp.zeros_like(ds_sc)

        heads = range(DN_HEADS)
        sl = [slice(h * LANE, (h + 1) * LANE) for h in heads]
        st = [st_ref[sl[h], :] for h in heads]
        ds = [ds_sc[h] for h in heads]
        do = [do_ref[:, sl[h]] for h in heads]
        v_new = [u_ref[:, sl[h]] - _hdot(w_ref[:, sl[h]], st[h]) for h in heads]
        dv_new = [_hdot(qk_ref[:, sl[h]], do[h], _TN) + _hdot(kg_ref[:, sl[h]], ds[h]) for h in heads]
        first_row = lax.broadcasted_iota(jnp.int32, (8, LANE), 0) == 0
        for h in heads:
            du_ref[:, sl[h]] = dv_new[h]
            dw_ref[:, sl[h]] = -_hdot(dv_new[h], st[h], _NT)
            dqg_ref[:, sl[h]] = _hdot(do[h], st[h], _NT)
            dqk_ref[:, sl[h]] = _hdot(do[h], v_new[h], _NT)
            dkg_ref[:, sl[h]] = _hdot(v_new[h], ds[h], _NT)
            del_ref[:, sl[h]] = jnp.where(first_row, jnp.sum(st[h] * ds[h], axis=0, keepdims=True), 0.0)
        ds_new = [_hdot(qg_ref[:, sl[h]], do[h], _TN) + ds[h] * el_ref[0:1, sl[h]] - _hdot(w_ref[:, sl[h]], dv_new[h], _TN)
                  for h in heads]
        for h in heads:
            ds_sc[h] = ds_new[h]

    row = pl.BlockSpec((c, hw), lambda i: (n - 1 - i, 0))
    small = pl.BlockSpec((8, hw), lambda i: (n - 1 - i, 0))
    return pl.pallas_call(
        body, out_shape=[jax.ShapeDtypeStruct((s, hw), F32)] * 5 + [jax.ShapeDtypeStruct((n * 8, hw), F32)], grid=(n,),
        in_specs=[row] * 5 + [small, pl.BlockSpec((None, hw, LANE), lambda i: (n - 1 - i, 0, 0)), row],
        out_specs=[row] * 5 + [small], scratch_shapes=[pltpu.VMEM((DN_HEADS, LANE, LANE), F32)],
        compiler_params=_cparams("arbitrary"), name="gdn_scan_bwd")(u, w, qg, kg, qk, el, states, do)


@jax.custom_vjp
def gdn_scan(u, w, qg, kg, qk, el):
    return _gdn_scan_fwd(u, w, qg, kg, qk, el)[0]


def _gdn_scan_vfwd(*args):
    o, states = _gdn_scan_fwd(*args)
    return o, args + (states,)


gdn_scan.defvjp(_gdn_scan_vfwd, lambda saved, do: tuple(_gdn_scan_bwd(*saved, do)))


def _loss_call(y, t):
    s, d = y.shape
    br = min(ROW_BLOCK_NARROW, s)
    n = s // br

    def body(y_ref, t_ref, loss_ref, dy_ref, acc):
        i = pl.program_id(0)

        @pl.when(i == 0)
        def _():
            acc[...] = jnp.zeros_like(acc)

        e = y_ref[...] - t_ref[...]
        dy_ref[...] = e / d
        acc[...] += jnp.sum(e * e, axis=0, keepdims=True)

        @pl.when(i == n - 1)
        def _():
            loss_ref[...] = jnp.broadcast_to(jnp.sum(acc[...], axis=1, keepdims=True) * (0.5 / d), loss_ref.shape)

    row = pl.BlockSpec((br, d), lambda i: (i, 0))
    return pl.pallas_call(
        body, out_shape=[jax.ShapeDtypeStruct((1, LANE), F32), jax.ShapeDtypeStruct((s, d), F32)], grid=(n,),
        in_specs=[row, row], out_specs=[pl.BlockSpec((1, LANE), lambda i: (0, 0)), row],
        scratch_shapes=[pltpu.VMEM((1, d), F32)], compiler_params=_cparams("arbitrary"), name="loss_head")(y, t)


@jax.custom_vjp
def loss_head(y, t):
    return _loss_call(y, t)[0][0, 0]


def _loss_head_fwd(y, t):
    loss, dy = _loss_call(y, t)
    return loss[0, 0], (dy,)


loss_head.defvjp(_loss_head_fwd, lambda saved, g: (saved[0] * g, -saved[0] * g))


IN_OFF = {}
_o = 0
for _name, _size in (('q_lat', 384), ('kv_lat', 320), ('z_a', 512), ('dn_qkv', 1536), ('dn_ab', 8), ('z_b', 512),
                     ('dil_qkv', 4608), ('z_c', 512), ('gate', 3072)):
    IN_OFF[_name] = (_o, _o + _size)
    _o += _size
IN_WIDTH = _o
N_CHIPS = 4


def _pad_cols(w, to):
    return jnp.concatenate([w, jnp.zeros((w.shape[0], to - w.shape[1]), w.dtype)], axis=1)


def _pad_row(v, to=None):
    v = v.reshape(1, -1)
    return v if to is None or v.shape[1] == to else _pad_cols(v, to)


def _shard_cols(pieces, a, b):
    wsh = pieces[0].shape[1]
    parts = [pieces[j][:, max(a, j * wsh) - j * wsh:min(b, (j + 1) * wsh) - j * wsh]
             for j in range(len(pieces)) if max(a, j * wsh) < min(b, (j + 1) * wsh)]
    return parts[0] if len(parts) == 1 else jnp.concatenate(parts, axis=1)


def _win_groups_impl(w_in4):
    out = []
    for l in range(w_in4.shape[1]):
        pieces = [w_in4[j, l] for j in range(w_in4.shape[0])]
        cols = lambda name: _shard_cols(pieces, *IN_OFF[name])
        out.append((_pad_cols(jnp.concatenate([cols('q_lat'), cols('kv_lat')], axis=1), 6 * LANE),
                    jnp.concatenate([cols('z_a'), cols('z_b'), cols('z_c')], axis=1),
                    cols('dn_qkv'), _pad_cols(cols('dn_ab'), LANE), cols('dil_qkv'), cols('gate')))
    return tuple(out)


@jax.custom_vjp
def win_groups(w_in4):
    return _win_groups_impl(w_in4)


def _win_groups_bwd(_, cts):
    n_chip, depth = N_CHIPS, len(cts)
    wsh = IN_WIDTH // n_chip
    per_layer = []
    for l in range(depth):
        dmla, dz, dgdn, dab, ddil, dgate = cts[l]
        width = lambda name: IN_OFF[name][1] - IN_OFF[name][0]
        n_lat, n_z, n_ab = width('q_lat') + width('kv_lat'), width('z_a'), width('dn_ab')
        full = jnp.concatenate([dmla[:, :n_lat], dz[:, :n_z], dgdn, dab[:, :n_ab], dz[:, n_z:2 * n_z], ddil, dz[:, 2 * n_z:],
                                dgate], axis=1)
        per_layer.append([full[:, j * wsh:(j + 1) * wsh] for j in range(n_chip)])
    return (jnp.stack([jnp.stack([per_layer[l][j] for l in range(depth)]) for j in range(n_chip)]),)


win_groups.defvjp(lambda w: (_win_groups_impl(w), None), _win_groups_bwd)


def _layer(x, p, tabs):
    w_mla, w_z, w_gdn, w_ab, w_dil, w_gate = p['w_in_groups']
    h, = rowwise(f_rms_full, "rms_in", [x], [_pad_row(p['norm_g'])], [(D_MODEL, BF16)], br=ROW_BLOCK_NARROW)
    mla_in, z, dn_qkv, dn_ab, dil_qkv, gate = matmul_shared(h, (w_mla, w_z, w_gdn, w_ab, w_dil, w_gate))

    qn, kvn, kpe = rowwise(f_mla1, "mla_norm", [mla_in], [_pad_row(p['mla_q_a_norm_g']), _pad_row(p['mla_kv_a_norm_g'])],
                           [(MLA_Q_RANK, BF16), (MLA_KV_RANK, BF16), (LANE, F32)], br=ROW_BLOCK_NARROW)
    wq = p['mla_w_q_b']
    wq_pad = jnp.concatenate(
        [wq[:, hh * MLA_QK:hh * MLA_QK + MLA_NOPE] for hh in range(MLA_HEADS)]
        + [_pad_cols(wq[:, hh * MLA_QK + MLA_NOPE:(hh + 1) * MLA_QK], LANE) for hh in range(MLA_HEADS)], axis=1)
    q = matmul(qn, wq_pad)
    kv = matmul(kvn, p['mla_w_kv_b'])
    gq, gk = p['mla_q_norm_g'], p['mla_k_norm_g']
    q_att, k_att, v_att = rowwise(
        f_mla2, "mla_qk", [q, kv, kpe, tabs['cos_r'], tabs['sin_r']],
        [_pad_row(gq[:MLA_NOPE]), _pad_row(gq[MLA_NOPE:], LANE), _pad_row(gk[:MLA_NOPE]), _pad_row(gk[MLA_NOPE:], LANE)],
        [(MLA_HEADS * MLA_DQK, BF16), (MLA_HEADS * MLA_DQK, BF16), (MLA_HEADS * HEAD, BF16)], nograd=(3, 4), br=ROW_BLOCK_NARROW)
    y_a = mla_attention(q_att, k_att, v_att)

    qkv_n = dn_conv(dn_qkv, p['dn_conv_w'])
    gb, = rowwise(f_gates, "dn_gates", [dn_ab], [_pad_row(p['dn_a_log'], LANE), _pad_row(p['dn_dt_bias'], LANE)], [(LANE, F32)],
                  br=ROW_BLOCK_NARROW)
    o_b = gdn_scan(*gdn_prep(qkv_n, gb))
    y_b, = rowwise(f_headnorm, "dn_out_norm", [o_b], [_pad_row(p['dn_out_norm_g'])], [(DN_HEADS * DN_DIM, F32)],
                   br=ROW_BLOCK_NARROW)

    qkv_d = rowwise(f_dil, "dil_qk", [dil_qkv, tabs['cos_h'], tabs['sin_h']],
                    [_pad_row(p['dil_q_norm_g']), _pad_row(p['dil_k_norm_g'])],
                    [(GROUP_W, BF16)] * (3 * DIL_GROUPS), nograd=(1, 2), out_dil=[d for d in DIL_DILATIONS for _ in range(3)])
    outs, lses = [], []
    for gi, d in enumerate(DIL_DILATIONS):
        o_g, l_g = dilated_group(*qkv_d[3 * gi:3 * gi + 3], d)
        outs.append(o_g)
        lses.append(l_g)
    y_c, = rowwise(f_comb, "dil_comb", outs + lses, [], [(GROUP_W, F32)], row_dil=list(DIL_DILATIONS) * 2)

    ys = rowwise(f_merge1, "merge_silu", [y_a, y_b, y_c, z], [], [(BRANCH_W, BF16)] * 3, br=ROW_BLOCK_NARROW)
    bo = [matmul(ys[b], p['w_branch'][b]) for b in range(3)]
    mixed, = rowwise(f_merge2, "merge_gate", [gate] + bo, [], [(D_MODEL, BF16)])
    return matmul(mixed, p['w_out'], res=x)


def _rope_tables(pos, dim):
    inv_freq = 1.0 / (ROPE_THETA ** (jnp.arange(0, dim, 2, dtype=F32) / dim))
    ang = pos.astype(F32)[:, None] * inv_freq
    return jnp.cos(ang), jnp.sin(ang)


def _tables(pos):
    cr, sr = _rope_tables(pos, MLA_ROPE)
    ch, sh = _rope_tables(pos, HEAD)
    zero = jnp.zeros((pos.shape[0], LANE - MLA_ROPE), F32)
    return {'cos_r': jnp.concatenate([cr, cr, zero], axis=1), 'sin_r': jnp.concatenate([sr, sr, zero], axis=1),
            'cos_h': jnp.concatenate([ch, ch], axis=1), 'sin_h': jnp.concatenate([-sh, sh], axis=1)}


def _local_loss(w_in4, mats, conv_w, small, x, target, tabs):
    groups = win_groups(w_in4)
    for l in range(DEPTH):
        p = {k: v[l] for k, v in mats.items()}
        p.update({k: v[l] for k, v in small.items()})
        p['dn_conv_w'] = conv_w[l]
        p['w_in_groups'] = groups[l]
        x = _layer(x, p, tabs)
    return loss_head(x, target)


def _pack(arrays, dtype, row_tile):
    flat = jnp.concatenate([a.astype(dtype).reshape(-1) for a in arrays])
    rows = -(-flat.shape[0] // (LANE * row_tile)) * row_tile
    flat = jnp.concatenate([flat, jnp.zeros((rows * LANE - flat.shape[0],), dtype)])
    return flat.reshape(rows, LANE)


def _unpack_impl(buf, shapes):
    flat = buf.reshape(-1)
    out, off = [], 0
    for shp in shapes:
        n = math.prod(shp)
        out.append(flat[off:off + n].reshape(shp))
        off += n
    return tuple(out)


@functools.partial(jax.custom_vjp, nondiff_argnums=(1, 2, 3))
def _unpack_p(buf, shapes, dtype_name, rows):
    return _unpack_impl(buf, shapes)


_unpack_p.defvjp(lambda buf, shapes, dtype_name, rows: (_unpack_impl(buf, shapes), None),
                 lambda shapes, dtype_name, rows, _, cts: (_pack(cts, jnp.dtype(dtype_name), rows),))


def _unpack(buf, shapes):
    return _unpack_p(buf, tuple(shapes), jnp.dtype(buf.dtype).name, buf.shape[0])


def _full_from_chips(buf4, shard_shapes):
    per_chip = [_unpack(buf4[j], tuple(shard_shapes)) for j in range(4)]
    return {name: jnp.concatenate([per_chip[j][i] for j in range(4)], axis=axis) for i, (name, axis) in enumerate(MATS)}


_HBM = pl.BlockSpec(memory_space=pltpu.HBM)
_VMEM = pl.BlockSpec(memory_space=pltpu.VMEM)


def _chip_peers(x, y):
    return [(1 - x, y), (x, 1 - y), (1 - x, 1 - y)]


def chip_all_to_all(arrays):
    n = len(arrays)

    def body(*refs):
        in_refs, out_refs = refs[:n], refs[n:2 * n]
        send_sems, recv_sems, local_sems = refs[2 * n:]
        x, y, c = lax.axis_index("x"), lax.axis_index("y"), lax.axis_index("c")
        me = 2 * x + y
        peers = _chip_peers(x, y)
        local, sends = [], []
        for a, (i_ref, o_ref) in enumerate(zip(in_refs, out_refs)):
            local.append(pltpu.make_async_copy(i_ref.at[me], o_ref.at[me], local_sems.at[a]))
            local[-1].start()
            for k, (px, py) in enumerate(peers):
                sends.append(pltpu.make_async_remote_copy(
                    src_ref=i_ref.at[2 * px + py], dst_ref=o_ref.at[me], send_sem=send_sems.at[3 * a + k],
                    recv_sem=recv_sems.at[3 * a + k], device_id=(px, py, c), device_id_type=MESH))
                sends[-1].start()
        for a, (i_ref, o_ref) in enumerate(zip(in_refs, out_refs)):
            for k, (px, py) in enumerate(peers):
                pltpu.make_async_remote_copy(
                    src_ref=i_ref.at[me], dst_ref=o_ref.at[2 * px + py], send_sem=send_sems.at[3 * a + k],
                    recv_sem=recv_sems.at[3 * a + k], device_id=(px, py, c), device_id_type=MESH).wait_recv()
        for cp in sends:
            cp.wait_send()
        for cp in local:
            cp.wait()

    return pl.pallas_call(
        body, out_shape=[jax.ShapeDtypeStruct(a.shape, a.dtype) for a in arrays], in_specs=[_HBM] * n, out_specs=[_HBM] * n,
        scratch_shapes=[pltpu.SemaphoreType.DMA((3 * n,)), pltpu.SemaphoreType.DMA((3 * n,)), pltpu.SemaphoreType.DMA((n,))],
        name="chip_all_to_all")(*arrays)


def chip_all_gather(shards):
    n = len(shards)

    def body(*refs):
        in_refs, out_refs = refs[:n], refs[n:2 * n]
        send_sems, recv_sems = refs[2 * n:]
        x, y, c = lax.axis_index("x"), lax.axis_index("y"), lax.axis_index("c")
        me = 2 * x + y
        peers = _chip_peers(x, y)
        sends = []
        for a, (i_ref, o_ref) in enumerate(zip(in_refs, out_refs)):
            for k, (px, py) in enumerate(peers):
                sends.append(pltpu.make_async_remote_copy(
                    src_ref=i_ref.at[c], dst_ref=o_ref.at[me, c], send_sem=send_sems.at[6 * a + k],
                    recv_sem=recv_sems.at[6 * a + k], device_id=(px, py, c), device_id_type=MESH))
                sends[-1].start()
        for a, (i_ref, o_ref) in enumerate(zip(in_refs, out_refs)):
            for k, (px, py) in enumerate(peers):
                landed = o_ref.at[2 * px + py, c]
                pltpu.make_async_remote_copy(src_ref=i_ref.at[c], dst_ref=landed, send_sem=send_sems.at[6 * a + k],
                                             recv_sem=recv_sems.at[6 * a + k], device_id=(px, py, c), device_id_type=MESH).wait_recv()
                sends.append(pltpu.make_async_remote_copy(
                    src_ref=landed, dst_ref=landed, send_sem=send_sems.at[6 * a + 3 + k], recv_sem=recv_sems.at[6 * a + 3 + k],
                    device_id=(x, y, 1 - c), device_id_type=MESH))
                sends[-1].start()
        for a, (i_ref, o_ref) in enumerate(zip(in_refs, out_refs)):
            for k, (px, py) in enumerate(peers):
                other = o_ref.at[2 * px + py, 1 - c]
                pltpu.make_async_remote_copy(src_ref=other, dst_ref=other, send_sem=send_sems.at[6 * a + 3 + k],
                                             recv_sem=recv_sems.at[6 * a + 3 + k], device_id=(x, y, 1 - c),
                                             device_id_type=MESH).wait_recv()
        for cp in sends:
            cp.wait_send()

    me = 2 * lax.axis_index("x") + lax.axis_index("y")
    outs = pl.pallas_call(
        body, out_shape=[jax.ShapeDtypeStruct((4,) + a.shape, a.dtype) for a in shards], in_specs=[_HBM] * n, out_specs=[_HBM] * n,
        scratch_shapes=[pltpu.SemaphoreType.DMA((6 * n,)), pltpu.SemaphoreType.DMA((6 * n,))],
        name="chip_all_gather")(*shards)
    return [lax.dynamic_update_index_in_dim(o, a, me, 0) for o, a in zip(outs, shards)]


def sibling_swap(arrays):
    n = len(arrays)

    def body(*refs):
        in_refs, out_refs = refs[:n], refs[n:2 * n]
        send_sems, recv_sems = refs[2 * n:]
        x, y, c = lax.axis_index("x"), lax.axis_index("y"), lax.axis_index("c")
        cps = [pltpu.make_async_remote_copy(src_ref=i_ref, dst_ref=o_ref, send_sem=send_sems.at[a], recv_sem=recv_sems.at[a],
                                            device_id=(x, y, 1 - c), device_id_type=MESH)
               for a, (i_ref, o_ref) in enumerate(zip(in_refs, out_refs))]
        for cp in cps:
            cp.start()
        for cp in cps:
            cp.wait()

    return pl.pallas_call(
        body, out_shape=[jax.ShapeDtypeStruct(a.shape, a.dtype) for a in arrays], in_specs=[_HBM] * n, out_specs=[_HBM] * n,
        scratch_shapes=[pltpu.SemaphoreType.DMA((n,)), pltpu.SemaphoreType.DMA((n,))], name="sibling_swap")(*arrays)


def sibling_all_gather(halves):
    n = len(halves)

    def body(*refs):
        in_refs, out_refs = refs[:n], refs[n:2 * n]
        send_sems, recv_sems = refs[2 * n:]
        x, y, c = lax.axis_index("x"), lax.axis_index("y"), lax.axis_index("c")
        sends = [pltpu.make_async_remote_copy(src_ref=i_ref, dst_ref=o_ref.at[c], send_sem=send_sems.at[a], recv_sem=recv_sems.at[a],
                                              device_id=(x, y, 1 - c), device_id_type=MESH)
                 for a, (i_ref, o_ref) in enumerate(zip(in_refs, out_refs))]
        for cp in sends:
            cp.start()
        for a, (i_ref, o_ref) in enumerate(zip(in_refs, out_refs)):
            pltpu.make_async_remote_copy(src_ref=i_ref, dst_ref=o_ref.at[1 - c], send_sem=send_sems.at[a], recv_sem=recv_sems.at[a],
                                         device_id=(x, y, 1 - c), device_id_type=MESH).wait_recv()
        for cp in sends:
            cp.wait_send()

    outs = pl.pallas_call(
        body, out_shape=[jax.ShapeDtypeStruct((2,) + a.shape, a.dtype) for a in halves], in_specs=[_HBM] * n, out_specs=[_HBM] * n,
        scratch_shapes=[pltpu.SemaphoreType.DMA((n,)), pltpu.SemaphoreType.DMA((n,))], name="sibling_all_gather")(*halves)
    return [lax.dynamic_update_index_in_dim(o, a, lax.axis_index("c"), 0) for o, a in zip(outs, halves)]


def all_gather8(v, name):
    def body(v_ref, out_ref, send_sems, recv_sems):
        x, y, c = lax.axis_index("x"), lax.axis_index("y"), lax.axis_index("c")
        out_ref[4 * x + 2 * y + c] = v_ref[...]

        def peer(k):
            return (x ^ (k >> 2), y ^ ((k >> 1) & 1), c ^ (k & 1))

        sends = [pltpu.make_async_remote_copy(src_ref=v_ref, dst_ref=out_ref.at[4 * x + 2 * y + c], send_sem=send_sems.at[k - 1],
                                              recv_sem=recv_sems.at[k - 1], device_id=peer(k), device_id_type=MESH)
                 for k in range(1, 8)]
        for cp in sends:
            cp.start()
        for k in range(1, 8):
            px, py, pc = peer(k)
            pltpu.make_async_remote_copy(src_ref=v_ref, dst_ref=out_ref.at[4 * px + 2 * py + pc], send_sem=send_sems.at[k - 1],
                                         recv_sem=recv_sems.at[k - 1], device_id=peer(k), device_id_type=MESH).wait_recv()
        for cp in sends:
            cp.wait_send()

    return pl.pallas_call(
        body, out_shape=jax.ShapeDtypeStruct((8,) + v.shape, v.dtype), in_specs=[_VMEM], out_specs=_VMEM,
        scratch_shapes=[pltpu.SemaphoreType.DMA((7,)), pltpu.SemaphoreType.DMA((7,))], name=name)(v)


def pair_add(a, b, row_tile, name):
    rows, width = a.shape

    def body(a_ref, b_ref, o_ref):
        o_ref[...] = (a_ref[...].astype(F32) + b_ref[...].astype(F32)).astype(o_ref.dtype)

    spec = pl.BlockSpec((row_tile, width), lambda i: (i, 0))
    return pl.pallas_call(body, out_shape=jax.ShapeDtypeStruct(a.shape, a.dtype), grid=(rows // row_tile,),
                          in_specs=[spec, spec], out_specs=spec, compiler_params=_cparams("parallel"), name=name)(a, b)


def sum_blocks(blocks, row_tile, name):
    n, rows, width = blocks.shape

    def body(b_ref, o_ref):
        acc = b_ref[0].astype(F32)
        for j in range(1, n):
            acc = acc + b_ref[j].astype(F32)
        o_ref[...] = acc

    return pl.pallas_call(
        body, out_shape=jax.ShapeDtypeStruct((rows, width), F32), grid=(rows // row_tile,),
        in_specs=[pl.BlockSpec((n, row_tile, width), lambda i: (0, i, 0))],
        out_specs=pl.BlockSpec((row_tile, width), lambda i: (i, 0)), compiler_params=_cparams("parallel"), name=name)(blocks)


def adamw(g_parts, w, m, v, row_tile, name):
    npart = len(g_parts)

    def body(*refs):
        g = refs[0][...]
        for r in refs[1:npart]:
            g = g + r[...]
        w_ref, m_ref, v_ref, g_out, d_out, m_out, v_out = refs[npart:]
        m_new = ADAM_B1 * m_ref[...] + (1.0 - ADAM_B1) * g
        v_new = ADAM_B2 * v_ref[...] + (1.0 - ADAM_B2) * (g * g)
        m_hat = m_new / (1.0 - ADAM_B1 ** ADAM_STEP)
        v_hat = v_new / (1.0 - ADAM_B2 ** ADAM_STEP)
        g_out[...] = g
        d_out[...] = -ADAM_LR * (m_hat / (jnp.sqrt(v_hat) + ADAM_EPS) + ADAM_WD * w_ref[...])
        m_out[...] = m_new
        v_out[...] = v_new

    rows, width = w.shape
    spec = pl.BlockSpec((row_tile, width), lambda i: (i, 0))
    return pl.pallas_call(
        body, out_shape=[jax.ShapeDtypeStruct(w.shape, F32)] * 4, grid=(rows // row_tile,),
        in_specs=[spec] * (npart + 3), out_specs=[spec] * 4, compiler_params=_cparams("parallel"), name=name)(*g_parts, w, m, v)


def kernel(x, positions, norm_g, w_in, mla_q_a_norm_g, mla_w_q_b, mla_kv_a_norm_g, mla_w_kv_b, mla_q_norm_g, mla_k_norm_g, dn_conv_w, dn_a_log, dn_dt_bias, dn_out_norm_g, dil_q_norm_g, dil_k_norm_g, w_branch, w_out, loss_target, m_norm_g, m_w_in, m_mla_q_a_norm_g, m_mla_w_q_b, m_mla_kv_a_norm_g, m_mla_w_kv_b, m_mla_q_norm_g, m_mla_k_norm_g, m_dn_conv_w, m_dn_a_log, m_dn_dt_bias, m_dn_out_norm_g, m_dil_q_norm_g, m_dil_k_norm_g, m_w_branch, m_w_out, v_norm_g, v_w_in, v_mla_q_a_norm_g, v_mla_w_q_b, v_mla_kv_a_norm_g, v_mla_w_kv_b, v_mla_q_norm_g, v_mla_k_norm_g, v_dn_conv_w, v_dn_a_log, v_dn_dt_bias, v_dn_out_norm_g, v_dil_q_norm_g, v_dil_k_norm_g, v_w_branch, v_w_out):
    w = dict(norm_g=norm_g, w_in=w_in, mla_q_a_norm_g=mla_q_a_norm_g, mla_w_q_b=mla_w_q_b, mla_kv_a_norm_g=mla_kv_a_norm_g,
             mla_w_kv_b=mla_w_kv_b, mla_q_norm_g=mla_q_norm_g, mla_k_norm_g=mla_k_norm_g, dn_conv_w=dn_conv_w, dn_a_log=dn_a_log,
             dn_dt_bias=dn_dt_bias, dn_out_norm_g=dn_out_norm_g, dil_q_norm_g=dil_q_norm_g, dil_k_norm_g=dil_k_norm_g,
             w_branch=w_branch, w_out=w_out)
    m = dict(norm_g=m_norm_g, w_in=m_w_in, mla_q_a_norm_g=m_mla_q_a_norm_g, mla_w_q_b=m_mla_w_q_b, mla_kv_a_norm_g=m_mla_kv_a_norm_g,
             mla_w_kv_b=m_mla_w_kv_b, mla_q_norm_g=m_mla_q_norm_g, mla_k_norm_g=m_mla_k_norm_g, dn_conv_w=m_dn_conv_w,
             dn_a_log=m_dn_a_log, dn_dt_bias=m_dn_dt_bias, dn_out_norm_g=m_dn_out_norm_g, dil_q_norm_g=m_dil_q_norm_g,
             dil_k_norm_g=m_dil_k_norm_g, w_branch=m_w_branch, w_out=m_w_out)
    v = dict(norm_g=v_norm_g, w_in=v_w_in, mla_q_a_norm_g=v_mla_q_a_norm_g, mla_w_q_b=v_mla_w_q_b, mla_kv_a_norm_g=v_mla_kv_a_norm_g,
             mla_w_kv_b=v_mla_w_kv_b, mla_q_norm_g=v_mla_q_norm_g, mla_k_norm_g=v_mla_k_norm_g, dn_conv_w=v_dn_conv_w,
             dn_a_log=v_dn_a_log, dn_dt_bias=v_dn_dt_bias, dn_out_norm_g=v_dn_out_norm_g, dil_q_norm_g=v_dil_q_norm_g,
             dil_k_norm_g=v_dil_k_norm_g, w_branch=v_w_branch, w_out=v_w_out)
    chip = 2 * lax.axis_index("x") + lax.axis_index("y")
    mat_names = [n for n, _ in MATS]
    mat_shapes = tuple(w[n].shape for n in mat_names)
    conv_shard = dn_conv_w.shape
    win_shape = w_in.shape
    win_rows = (win_shape[0] * win_shape[1], win_shape[2])

    mats_sh = _pack([w[n] for n in mat_names], BF16, 2 * MAT_ROWS)
    mat_rows = mats_sh.shape[0]
    w_in4, mats4 = chip_all_gather([w_in.astype(BF16), mats_sh.reshape(2, mat_rows // 2, LANE)])
    mats4 = mats4.reshape(N_CHIPS, mat_rows, LANE)
    conv8 = all_gather8(_pack([dn_conv_w], F32, 8), "gather_conv_w")
    conv_full = jnp.concatenate([_unpack(conv8[2 * j], (conv_shard,))[0] for j in range(4)], axis=2)
    small = {n: w[n] for n in SMALL}
    tabs = _tables(positions[0])

    def loss_fn(w_in4, mats4, conv_full, small, xs):
        return _local_loss(w_in4, _full_from_chips(mats4, mat_shapes), conv_full, small, xs, loss_target[0], tabs)

    loss, (g_win4, g_mats4, g_conv, g_small, g_x) = jax.value_and_grad(loss_fn, argnums=(0, 1, 2, 3, 4))(
        w_in4, mats4, conv_full, small, x[0])
    loss = lax.psum(loss, ("x", "y", "c"))

    c_idx = lax.axis_index("c")
    half_win = (N_CHIPS * win_shape[1], win_shape[2])
    half_mats = (N_CHIPS * (mat_rows // 2), LANE)
    g_mats4 = g_mats4.reshape(N_CHIPS, 2, mat_rows // 2, LANE)
    pick = lambda g, i, shape: lax.dynamic_index_in_dim(g, i, axis=1, keepdims=False).reshape(shape)
    from_sib = sibling_swap([pick(g_win4, 1 - c_idx, half_win), pick(g_mats4, 1 - c_idx, half_mats)])
    s_win = pair_add(pick(g_win4, c_idx, half_win), from_sib[0], WIN_ROWS, "pair_add_w_in")
    s_mats = pair_add(pick(g_mats4, c_idx, half_mats), from_sib[1], MAT_ROWS, "pair_add_mats")
    r_win, r_mats = chip_all_to_all([s_win.reshape(N_CHIPS, win_shape[1], win_shape[2]),
                                     s_mats.reshape(N_CHIPS, mat_rows // 2, LANE)])
    g_win, g_mats = sibling_all_gather([sum_blocks(r_win, WIN_ROWS, "sum_chip_pieces_w_in"),
                                        sum_blocks(r_mats, MAT_ROWS, "sum_chip_pieces_mats")])
    res_win = adamw([g_win.reshape(win_rows)], *[d['w_in'].reshape(win_rows) for d in (w, m, v)], WIN_ROWS, "adamw_w_in")
    packed = [_pack([d[n] for n in mat_names], F32, 2 * MAT_ROWS) for d in (w, m, v)]
    res_big = [dict(zip(mat_names, _unpack(r, mat_shapes)))
               for r in adamw([g_mats.reshape(mat_rows, LANE)], *packed, MAT_ROWS, "adamw_mats")]
    for k in range(4):
        res_big[k]['w_in'] = res_win[k].reshape(win_shape)

    small_shapes = tuple(w[n].shape for n in SMALL) + (g_conv.shape,)
    g_all = sum_blocks(all_gather8(_pack([g_small[n] for n in SMALL] + [g_conv], F32, 8), "gather_small_grads"), 8, "sum_small")
    g_list = list(_unpack(g_all, small_shapes))
    g_list[-1] = lax.dynamic_slice_in_dim(g_list[-1], chip * conv_shard[2], conv_shard[2], axis=2)
    small_names = list(SMALL) + ['dn_conv_w']
    packed_s = [_pack([d[n] for n in small_names], F32, 8) for d in (w, m, v)]
    shapes_s = tuple(w[n].shape for n in small_names)
    res_small = [dict(zip(small_names, _unpack(r, shapes_s))) for r in adamw([_pack(g_list, F32, 8)], *packed_s, 8, "adamw_small")]

    outs = [loss, g_x[None]]
    for k in range(4):
        outs += [res_big[k][n] if n in res_big[k] else res_small[k][n] for n in WEIGHTS]
    return tuple(outs)
```

```python
import functools
import math

import jax
import jax.numpy as jnp
from jax import lax
from jax.experimental import pallas as pl
from jax.experimental.pallas import tpu as pltpu

F32 = jnp.float32
BF16 = jnp.bfloat16
MESH = pl.DeviceIdType.MESH

LANE = 128
VMEM_LIMIT = 48 * 1024 * 1024
ROW_BLOCK = 256
ROW_BLOCK_NARROW = 512
MM_TM, MM_TN, MM_TK = 1024, 1024, 1024
MAT_ROWS = 1664
WIN_ROWS = 128

RMS_EPS = 1e-6
ROPE_THETA = 10000.0
D_MODEL = 1024
DEPTH = 2
MLA_HEADS = 4
MLA_NOPE = 128
MLA_ROPE = 64
MLA_QK = MLA_NOPE + MLA_ROPE
MLA_Q_RANK = 384
MLA_KV_RANK = 256
DN_HEADS = 4
DN_DIM = 128
DN_CONV = 4
GDN_CHUNK = 128
DIL_WINDOWS = (128, 512, 2048)
DIL_DILATIONS = (1, 4, 16)
DIL_GROUPS = 3
DIL_BLOCK = 128
HEAD = 128
BRANCH_W = 512

ADAM_LR = 0.001
ADAM_B1 = 0.9
ADAM_B2 = 0.999
ADAM_EPS = 1e-08
ADAM_WD = 0.01
ADAM_STEP = 10

WEIGHTS = ['norm_g', 'w_in', 'mla_q_a_norm_g', 'mla_w_q_b', 'mla_kv_a_norm_g', 'mla_w_kv_b', 'mla_q_norm_g',
           'mla_k_norm_g', 'dn_conv_w', 'dn_a_log', 'dn_dt_bias', 'dn_out_norm_g', 'dil_q_norm_g', 'dil_k_norm_g',
           'w_branch', 'w_out']
MATS = (('mla_w_q_b', 2), ('mla_w_kv_b', 2), ('w_branch', 3), ('w_out', 1))
SMALL = ('norm_g', 'mla_q_a_norm_g', 'mla_kv_a_norm_g', 'mla_q_norm_g', 'mla_k_norm_g', 'dn_a_log', 'dn_dt_bias',
         'dn_out_norm_g', 'dil_q_norm_g', 'dil_k_norm_g')


def _cparams(*sem):
    return pltpu.CompilerParams(dimension_semantics=sem or None, vmem_limit_bytes=VMEM_LIMIT)


def _tile(dim, target):
    best = 0
    for t in range(LANE, min(dim, target) + 1, LANE):
        if dim % t == 0:
            best = t
    assert best, (dim, target)
    return best


def _mm(a, b, mode, out_dtype, res=None):
    if mode == 'nn':
        (m, k), (k2, n) = a.shape, b.shape
    elif mode == 'nt':
        (m, k), (n, k2) = a.shape, b.shape
    else:
        (k, m), (k2, n) = a.shape, b.shape
    assert k == k2, (a.shape, b.shape, mode)
    tm, tn, tk = _tile(m, MM_TM), _tile(n, MM_TN), _tile(k, MM_TK)
    nk = k // tk
    dims = {'nn': (((1,), (0,)), ((), ())), 'nt': (((1,), (1,)), ((), ())), 'tn': (((0,), (0,)), ((), ()))}[mode]

    def body(*refs):
        if res is None:
            a_ref, b_ref, o_ref = refs[:3]
        else:
            a_ref, b_ref, r_ref, o_ref = refs[:4]
        part = lax.dot_general(a_ref[...].astype(BF16), b_ref[...].astype(BF16), dims, preferred_element_type=F32)

        def finish(r):
            if res is not None:
                r = r + r_ref[...].astype(F32)
            o_ref[...] = r.astype(o_ref.dtype)

        if nk == 1:
            finish(part)
            return
        acc = refs[-1]
        kk = pl.program_id(2)

        @pl.when(kk == 0)
        def _():
            acc[...] = part

        @pl.when(jnp.logical_and(kk > 0, kk < nk - 1))
        def _():
            acc[...] += part

        @pl.when(kk == nk - 1)
        def _():
            finish(acc[...] + part)

    a_spec = pl.BlockSpec((tk, tm), lambda i, j, kk: (kk, i)) if mode == 'tn' else pl.BlockSpec((tm, tk), lambda i, j, kk: (i, kk))
    b_spec = pl.BlockSpec((tn, tk), lambda i, j, kk: (j, kk)) if mode == 'nt' else pl.BlockSpec((tk, tn), lambda i, j, kk: (kk, j))
    o_spec = pl.BlockSpec((tm, tn), lambda i, j, kk: (i, j))
    in_specs = [a_spec, b_spec] + ([o_spec] if res is not None else [])
    args = (a, b) + ((res,) if res is not None else ())
    return pl.pallas_call(
        body, out_shape=jax.ShapeDtypeStruct((m, n), out_dtype), grid=(m // tm, n // tn, nk),
        in_specs=in_specs, out_specs=o_spec, scratch_shapes=[pltpu.VMEM((tm, tn), F32)] if nk > 1 else [],
        compiler_params=_cparams("parallel", "parallel", "arbitrary"),
        name=f"mm_{mode}_{m}x{k}x{n}" + ("_res" if res is not None else ""))(*args)


def _make_matmul(out_dtype, with_res):
    @jax.custom_vjp
    def mm(a, b, *r):
        return _mm(a, b, 'nn', out_dtype, *r)

    def fwd(a, b, *r):
        return mm(a, b, *r), (a, b)

    def bwd(saved, g):
        a, b = saved
        da = _mm(g, b, 'nt', a.dtype)
        db = _mm(a, g, 'tn', b.dtype)
        return (da, db) + ((g,) if with_res else ())

    mm.defvjp(fwd, bwd)
    return mm


@jax.custom_vjp
def matmul_shared(a, bs):
    return tuple(_mm(a, b, 'nn', F32) for b in bs)


def _matmul_shared_bwd(saved, gs):
    a, bs = saved
    da = None
    for i, (g, b) in enumerate(zip(gs, bs)):
        da = _mm(g, b, 'nt', a.dtype if i == len(bs) - 1 else F32, da)
    return da, tuple(_mm(a, g, 'tn', b.dtype) for g, b in zip(gs, bs))


matmul_shared.defvjp(lambda a, bs: (matmul_shared(a, bs), (a, bs)), _matmul_shared_bwd)


def matmul(a, b, out_dtype=F32, res=None):
    if res is None:
        return _make_matmul(out_dtype, False)(a, b)
    return _make_matmul(out_dtype, True)(a, b, res)


def _chunks(ref):
    return [ref[:, c * LANE:(c + 1) * LANE].astype(F32) for c in range(ref.shape[-1] // LANE)]


def _store(ref, chunks):
    for c, ch in enumerate(chunks):
        ref[:, c * LANE:(c + 1) * LANE] = ch.astype(ref.dtype)


def _row_spec(width, br, d=1):
    return pl.BlockSpec((br // d, d * width), lambda i: (i, 0))


def _par_spec(width):
    return pl.BlockSpec((1, width), lambda i: (0, 0))


def _load_rows(ref, scratch, d):
    if d == 1:
        return _chunks(ref)
    n, width = ref.shape[0], ref.shape[1] // d
    for c in range(width // LANE):
        for r in range(d):
            lanes = slice(r * width + c * LANE, r * width + (c + 1) * LANE)
            scratch[c, pl.ds(r, n, stride=d), :] = ref[:, lanes].astype(F32)
    return [scratch[c] for c in range(width // LANE)]


def _store_rows(ref, scratch, d, chunks):
    if d == 1:
        return _store(ref, chunks)
    n, width = ref.shape[0], ref.shape[1] // d
    for c, ch in enumerate(chunks):
        scratch[c] = ch
        for r in range(d):
            lanes = slice(r * width + c * LANE, r * width + (c + 1) * LANE)
            ref[:, lanes] = scratch[c, pl.ds(r, n, stride=d), :].astype(ref.dtype)


def _view_scratch(widths_dils, br):
    return [pltpu.VMEM((w // LANE, br, LANE), F32) for w, d in widths_dils if d > 1]


def _with_scratch(dils, scratch_refs):
    it = iter(scratch_refs)
    return [next(it) if d > 1 else None for d in dils]


def _rw_fwd(f, name, rows, params, outs, br, row_dil, out_dil):
    s = rows[0].shape[0] * row_dil[0]
    br = min(br, s)
    nr, npar, nout = len(rows), len(params), len(outs)
    row_w = [r.shape[1] // d for r, d in zip(rows, row_dil)]

    def body(*refs):
        scr = refs[nr + npar + nout:]
        n_in = sum(d > 1 for d in row_dil)
        rc = [_load_rows(r, sc, d) for r, sc, d in zip(refs[:nr], _with_scratch(row_dil, scr[:n_in]), row_dil)]
        pc = [_chunks(p) for p in refs[nr:nr + npar]]
        res = f(rc, pc)
        for o_ref, sc, d, chs in zip(refs[nr + npar:nr + npar + nout], _with_scratch(out_dil, scr[n_in:]), out_dil, res):
            _store_rows(o_ref, sc, d, chs)

    return pl.pallas_call(
        body, out_shape=[jax.ShapeDtypeStruct((s // d, d * w), dt) for (w, dt), d in zip(outs, out_dil)], grid=(s // br,),
        in_specs=[_row_spec(w, br, d) for w, d in zip(row_w, row_dil)] + [_par_spec(p.shape[1]) for p in params],
        out_specs=[_row_spec(w, br, d) for (w, _), d in zip(outs, out_dil)],
        scratch_shapes=_view_scratch(zip(row_w, row_dil), br) + _view_scratch([(w, d) for (w, _), d in zip(outs, out_dil)], br),
        compiler_params=_cparams("parallel"), name=name + "_fwd")(*rows, *params)


def _rw_bwd(f, name, rows, params, cts, nograd, br, row_dil, out_dil):
    s = rows[0].shape[0] * row_dil[0]
    br = min(br, s)
    nr, npar, nct = len(rows), len(params), len(cts)
    grad_rows = [i for i in range(nr) if i not in nograd]
    row_w = [r.shape[1] // d for r, d in zip(rows, row_dil)]
    ct_w = [c.shape[1] // d for c, d in zip(cts, out_dil)]
    grad_dil = [row_dil[i] for i in grad_rows]

    def body(*refs):
        n_out = len(grad_rows) + npar
        out_refs = refs[nr + npar + nct:nr + npar + nct + n_out]
        scr = refs[nr + npar + nct + n_out:]
        n_in, n_ct = sum(d > 1 for d in row_dil), sum(d > 1 for d in out_dil)
        rc = [_load_rows(r, sc, d) for r, sc, d in zip(refs[:nr], _with_scratch(row_dil, scr[:n_in]), row_dil)]
        pc = [_chunks(p) for p in refs[nr:nr + npar]]
        ct = [_load_rows(c, sc, d) for c, sc, d in
              zip(refs[nr + npar:nr + npar + nct], _with_scratch(out_dil, scr[n_in:n_in + n_ct]), out_dil)]
        _, vjp = jax.vjp(f, rc, pc)
        drc, dpc = vjp(ct)
        for o_ref, sc, d, i in zip(out_refs[:len(grad_rows)], _with_scratch(grad_dil, scr[n_in + n_ct:]), grad_dil, grad_rows):
            _store_rows(o_ref, sc, d, drc[i])
        i0 = pl.program_id(0)
        for o_ref, chs in zip(out_refs[len(grad_rows):], dpc):
            @pl.when(i0 == 0)
            def _(o_ref=o_ref):
                o_ref[...] = jnp.zeros_like(o_ref)
            for c, ch in enumerate(chs):
                o_ref[:, c * LANE:(c + 1) * LANE] += ch

    out_shape = ([jax.ShapeDtypeStruct(rows[i].shape, rows[i].dtype) for i in grad_rows]
                 + [jax.ShapeDtypeStruct(p.shape, F32) for p in params])
    out_specs = [_row_spec(row_w[i], br, row_dil[i]) for i in grad_rows] + [_par_spec(p.shape[1]) for p in params]
    res = pl.pallas_call(
        body, out_shape=out_shape, grid=(s // br,),
        in_specs=([_row_spec(w, br, d) for w, d in zip(row_w, row_dil)] + [_par_spec(p.shape[1]) for p in params]
                  + [_row_spec(w, br, d) for w, d in zip(ct_w, out_dil)]),
        out_specs=out_specs,
        scratch_shapes=(_view_scratch(zip(row_w, row_dil), br) + _view_scratch(zip(ct_w, out_dil), br)
                        + _view_scratch([(row_w[i], row_dil[i]) for i in grad_rows], br)),
        compiler_params=_cparams("arbitrary"), name=name + "_bwd")(*rows, *params, *cts)
    drows = [None] * nr
    for o, i in zip(res[:len(grad_rows)], grad_rows):
        drows[i] = o
    for i in nograd:
        drows[i] = jnp.zeros_like(rows[i])
    return tuple(drows), tuple(res[len(grad_rows):])


def rowwise(f, name, rows, params, outs, nograd=(), br=ROW_BLOCK, row_dil=None, out_dil=None):
    row_dil = tuple(row_dil or [1] * len(rows))
    out_dil = tuple(out_dil or [1] * len(outs))

    @jax.custom_vjp
    def op(rows, params):
        return tuple(_rw_fwd(f, name, rows, params, outs, br, row_dil, out_dil))

    def fwd(rows, params):
        return op(rows, params), (rows, params)

    def bwd(saved, cts):
        rows, params = saved
        return _rw_bwd(f, name, rows, params, list(cts), nograd, br, row_dil, out_dil)

    op.defvjp(fwd, bwd)
    return op(tuple(rows), tuple(params))


def _lane_roll(x, s):
    w = x.shape[-1]

    @jax.custom_vjp
    def r(v):
        return pltpu.roll(v, s, 1)

    r.defvjp(lambda v: (r(v), None), lambda _, g: (pltpu.roll(g, (w - s) % w, 1),))
    return r(x)


def _lanes(x):
    return lax.broadcasted_iota(jnp.int32, x.shape, 1)


def _rms(x, g, n=LANE):
    return x * lax.rsqrt(jnp.sum(x * x, axis=-1, keepdims=True) / n + RMS_EPS) * g


def _rope128(x, cos, sin_signed):
    return x * cos + _lane_roll(x, 64) * sin_signed


def _rope64(x, cos, sin):
    lane = _lanes(x)
    rot = jnp.where(lane < 32, -_lane_roll(x, 96), jnp.where(lane < 64, _lane_roll(x, 32), 0.0))
    return x * cos + rot * sin


def _silu(z):
    return z * jax.nn.sigmoid(z)


def _softplus(x):
    return jnp.maximum(x, 0.0) + jnp.log1p(jnp.exp(-jnp.abs(x)))


def f_rms_full(rc, pc):
    x, g = rc[0], pc[0]
    width = len(x) * LANE
    ms = sum(jnp.sum(c * c, axis=-1, keepdims=True) for c in x) / width
    r = lax.rsqrt(ms + RMS_EPS)
    return [[c * r * gc for c, gc in zip(x, g)]]


def f_mla1(rc, pc):
    x = rc[0]
    qn = f_rms_full([x[0:3]], [pc[0]])[0]
    kvn = f_rms_full([x[3:5]], [pc[1]])[0]
    return [qn, kvn, [x[5]]]


def f_mla2(rc, pc):
    q, kv, kpe, cos, sin = rc[0], rc[1], rc[2][0], rc[3][0], rc[4][0]
    gqn, gqp, gkn, gkp = pc[0][0], pc[1][0], pc[2][0], pc[3][0]
    k_pe = _rope64(_rms(kpe, gkp, MLA_ROPE), cos, sin)
    q_att, k_att, v = [], [], []
    for h in range(MLA_HEADS):
        q_att += [_rms(q[h], gqn), _rope64(_rms(q[MLA_HEADS + h], gqp, MLA_ROPE), cos, sin)]
        k_att += [_rms(kv[2 * h], gkn), k_pe]
        v.append(kv[2 * h + 1])
    return [q_att, k_att, v]


def f_gates(rc, pc):
    x, a_log, dt_bias = rc[0][0], pc[0][0], pc[1][0]
    lane = _lanes(x)
    g = -jnp.exp(a_log) * _softplus(x + dt_bias)
    return [[jnp.where(lane < DN_HEADS, g, jnp.where(lane < 2 * DN_HEADS, jax.nn.sigmoid(x), 0.0))]]


def f_headnorm(rc, pc):
    return [[_rms(c, pc[0][0]) for c in rc[0]]]


def f_dil(rc, pc):
    x, cos, sin = rc[0], rc[1][0], rc[2][0]
    gq, gk = pc[0][0], pc[1][0]
    n = len(x) // 3
    q = [_rope128(_rms(c, gq), cos, sin) for c in x[:n]]
    k = [_rope128(_rms(c, gk), cos, sin) for c in x[n:2 * n]]
    v = list(x[2 * n:])
    per = n // DIL_GROUPS
    return [t[g * per:(g + 1) * per] for g in range(DIL_GROUPS) for t in (q, k, v)]


def f_comb(rc, pc):
    o, l = rc[:DIL_GROUPS], rc[DIL_GROUPS:]
    out = []
    for c in range(len(o[0])):
        m = functools.reduce(jnp.maximum, [lg[c] for lg in l])
        e = [jnp.exp(lg[c] - m) for lg in l]
        den = sum(e)
        out.append(sum(eg * og[c] for eg, og in zip(e, o)) / den)
    return [out]


def f_merge1(rc, pc):
    z = rc[3]
    n = len(rc[0])
    return [[y * _silu(z[b * n + c]) for c, y in enumerate(rc[b])] for b in range(3)]


def f_merge2(rc, pc):
    gate = rc[0]
    n = len(rc[1])
    return [[sum(jax.nn.sigmoid(gate[b * n + c]) * rc[1 + b][c] for b in range(3)) for c in range(n)]]


MLA_DQK = 2 * LANE
MLA_SCALE = MLA_QK ** -0.5


def _mla_attn_fwd(q, k, v):
    s = q.shape[0]
    h = q.shape[1] // MLA_DQK
    t = min(512, s)
    kw = 4 if s % (4 * t) == 0 else 2
    tk = kw * t
    n, nkv = s // t, s // tk

    def body(q_ref, k_ref, v_ref, o_ref, lse_ref, m_sc, l_sc, acc_sc):
        qi, kj = pl.program_id(1), pl.program_id(2)
        last = qi // kw

        @pl.when(kj == 0)
        def _():
            m_sc[...] = jnp.full_like(m_sc, -jnp.inf)
            l_sc[...] = jnp.zeros_like(l_sc)
            acc_sc[...] = jnp.zeros_like(acc_sc)

        nsub = 2 if t % 256 == 0 else 1
        ts = t // nsub
        rows = [slice(r * ts, (r + 1) * ts) for r in range(nsub)]

        def step(on_diagonal):
            sc = [lax.dot_general(q_ref[rw, :], k_ref[...], (((1,), (1,)), ((), ())), preferred_element_type=F32) * MLA_SCALE
                  for rw in rows]
            if on_diagonal:
                gap = lax.broadcasted_iota(jnp.int32, (ts, tk), 1) - lax.broadcasted_iota(jnp.int32, (ts, tk), 0)
                sc = [jnp.where(gap <= qi * t + r * ts - kj * tk, x, -jnp.inf) for r, x in enumerate(sc)]
            m_old = [m_sc[rw, :] for rw in rows]
            m_new = [jnp.maximum(mo, jnp.max(x, axis=-1, keepdims=True)) for mo, x in zip(m_old, sc)]
            alpha = [jnp.exp(mo - mn) for mo, mn in zip(m_old, m_new)]
            p = [jnp.exp(x - mn) for x, mn in zip(sc, m_new)]
            pv = [jnp.dot(x.astype(BF16), v_ref[...], preferred_element_type=F32) for x in p]
            for r, rw in enumerate(rows):
                l_sc[rw, :] = alpha[r] * l_sc[rw, :] + jnp.sum(p[r], axis=-1, keepdims=True)
                acc_sc[rw, :] = alpha[r] * acc_sc[rw, :] + pv[r]
                m_sc[rw, :] = m_new[r]

        @pl.when(kj < last)
        def _():
            step(False)

        @pl.when(kj == last)
        def _():
            step(True)

        @pl.when(kj == nkv - 1)
        def _():
            o_ref[...] = acc_sc[...] / l_sc[...]
            lse_ref[...] = jnp.broadcast_to(m_sc[...] + jnp.log(l_sc[...]), lse_ref.shape)

    return pl.pallas_call(
        body, out_shape=[jax.ShapeDtypeStruct((s, h * HEAD), F32)] * 2, grid=(h, n, nkv),
        in_specs=[pl.BlockSpec((t, MLA_DQK), lambda hh, i, j: (i, hh)),
                  pl.BlockSpec((tk, MLA_DQK), lambda hh, i, j: (jnp.minimum(j, i // kw), hh)),
                  pl.BlockSpec((tk, HEAD), lambda hh, i, j: (jnp.minimum(j, i // kw), hh))],
        out_specs=[pl.BlockSpec((t, HEAD), lambda hh, i, j: (i, hh))] * 2,
        scratch_shapes=[pltpu.VMEM((t, 1), F32), pltpu.VMEM((t, 1), F32), pltpu.VMEM((t, HEAD), F32)],
        compiler_params=_cparams("parallel", "parallel", "arbitrary"), name="mla_attn_fwd")(q, k, v)


def _mla_attn_bwd(q, k, v, o, lse, do):
    s = q.shape[0]
    h = q.shape[1] // MLA_DQK
    t = min(512, s)
    tk = 2 * t
    n, nkv = s // t, s // tk
    nt, tn = (((1,), (1,)), ((), ())), (((0,), (0,)), ((), ()))

    def body(q_ref, k_ref, v_ref, o_ref, lse_ref, do_ref, dq_ref, dk_ref, dv_ref, dq_sc, dk_sc, dv_sc):
        kj, qi = pl.program_id(1), pl.program_id(2)
        last = qi // 2

        @pl.when(jnp.logical_and(kj == 0, qi == 0))
        def _():
            dq_sc[...] = jnp.zeros_like(dq_sc)

        @pl.when(qi == 0)
        def _():
            dk_sc[...] = jnp.zeros_like(dk_sc)
            dv_sc[...] = jnp.zeros_like(dv_sc)

        def pair(on_diagonal):
            sc = lax.dot_general(q_ref[...], k_ref[...], nt, preferred_element_type=F32) * MLA_SCALE
            p = jnp.exp(sc - lse_ref[:, 0:1])
            if on_diagonal:
                gap = lax.broadcasted_iota(jnp.int32, (t, tk), 1) - lax.broadcasted_iota(jnp.int32, (t, tk), 0)
                p = jnp.where(gap <= qi * t - kj * tk, p, 0.0)
            do_v = do_ref[...]
            do_b = do_v.astype(BF16)
            dp = lax.dot_general(do_b, v_ref[...], nt, preferred_element_type=F32)
            delta = jnp.sum(do_v * o_ref[...], axis=-1, keepdims=True)
            ds = (p * (dp - delta) * MLA_SCALE).astype(BF16)
            dv_sc[...] += lax.dot_general(p.astype(BF16), do_b, tn, preferred_element_type=F32)
            dk_sc[...] += lax.dot_general(ds, q_ref[...], tn, preferred_element_type=F32)
            rows = pl.ds(pl.multiple_of(qi * t, t), t)
            dq_sc[rows, :] += jnp.dot(ds, k_ref[...], preferred_element_type=F32)

        @pl.when(last > kj)
        def _():
            pair(False)

        @pl.when(last == kj)
        def _():
            pair(True)

        @pl.when(qi == n - 1)
        def _():
            dk_ref[...] = dk_sc[...].astype(dk_ref.dtype)
            dv_ref[...] = dv_sc[...].astype(dv_ref.dtype)

        @pl.when(jnp.logical_and(kj == nkv - 1, qi == n - 1))
        def _():
            dq_ref[...] = dq_sc[...].astype(dq_ref.dtype)

    qmap = lambda hh, j, i: (jnp.maximum(i, 2 * j), hh)
    kmap = lambda hh, j, i: (j, hh)
    return pl.pallas_call(
        body, out_shape=[jax.ShapeDtypeStruct(q.shape, BF16), jax.ShapeDtypeStruct(k.shape, BF16), jax.ShapeDtypeStruct(v.shape, BF16)],
        grid=(h, nkv, n),
        in_specs=[pl.BlockSpec((t, MLA_DQK), qmap), pl.BlockSpec((tk, MLA_DQK), kmap), pl.BlockSpec((tk, HEAD), kmap),
                  pl.BlockSpec((t, HEAD), qmap), pl.BlockSpec((t, HEAD), qmap), pl.BlockSpec((t, HEAD), qmap)],
        out_specs=[pl.BlockSpec((s, MLA_DQK), lambda hh, j, i: (0, hh)), pl.BlockSpec((tk, MLA_DQK), kmap),
                   pl.BlockSpec((tk, HEAD), kmap)],
        scratch_shapes=[pltpu.VMEM((s, MLA_DQK), F32), pltpu.VMEM((tk, MLA_DQK), F32), pltpu.VMEM((tk, HEAD), F32)],
        compiler_params=_cparams("parallel", "arbitrary", "arbitrary"), name="mla_attn_bwd")(q, k, v, o, lse, do)


@jax.custom_vjp
def mla_attention(q, k, v):
    return _mla_attn_fwd(q, k, v)[0]


def _mla_attention_fwd(q, k, v):
    o, lse = _mla_attn_fwd(q, k, v)
    return o, (q, k, v, o, lse)


def _mla_attention_bwd(saved, do):
    return tuple(_mla_attn_bwd(*saved, do))


mla_attention.defvjp(_mla_attention_fwd, _mla_attention_bwd)


DIL_SCALE = HEAD ** -0.5
GROUP_W = 4 * HEAD


def _dil_scores(q, kp, kc, n):
    dn = (((1,), (1,)), ((), ()))
    sp = lax.dot_general(q, kp, dn, preferred_element_type=F32) * DIL_SCALE
    sc = lax.dot_general(q, kc, dn, preferred_element_type=F32) * DIL_SCALE
    qi = lax.broadcasted_iota(jnp.int32, sp.shape, 0)
    kc_i = lax.broadcasted_iota(jnp.int32, sp.shape, 1)
    vp = jnp.logical_and(kc_i >= qi, n > 0)
    vc = kc_i <= qi
    return sp, sc, vp, vc


def _dil_specs(d):
    cur = pl.BlockSpec((DIL_BLOCK, GROUP_W), lambda r, n: (n, r))
    prev = pl.BlockSpec((DIL_BLOCK, GROUP_W), lambda r, n: (jnp.maximum(n - 1, 0), r))
    return cur, prev


def _dil_fwd(q, k, v, d):
    l = q.shape[0]
    nb = l // DIL_BLOCK
    cur, prev = _dil_specs(d)

    def body(q_ref, kp_ref, kc_ref, vp_ref, vc_ref, o_ref, lse_ref):
        n = pl.program_id(1)
        heads = range(4)
        sl = [slice(h * HEAD, (h + 1) * HEAD) for h in heads]
        scores = [_dil_scores(q_ref[:, sl[h]], kp_ref[:, sl[h]], kc_ref[:, sl[h]], n) for h in heads]
        sp = [jnp.where(vp, s_p, -jnp.inf) for s_p, _, vp, _ in scores]
        sc = [jnp.where(vc, s_c, -jnp.inf) for _, s_c, _, vc in scores]
        m = [jnp.maximum(jnp.max(sp[h], axis=-1, keepdims=True), jnp.max(sc[h], axis=-1, keepdims=True)) for h in heads]
        ep = [jnp.exp(sp[h] - m[h]) for h in heads]
        ec = [jnp.exp(sc[h] - m[h]) for h in heads]
        den = [jnp.sum(ep[h], axis=-1, keepdims=True) + jnp.sum(ec[h], axis=-1, keepdims=True) for h in heads]
        acc = [jnp.dot(ep[h].astype(BF16), vp_ref[:, sl[h]], preferred_element_type=F32)
               + jnp.dot(ec[h].astype(BF16), vc_ref[:, sl[h]], preferred_element_type=F32) for h in heads]
        for h in heads:
            o_ref[:, sl[h]] = acc[h] / den[h]
            lse_ref[:, sl[h]] = jnp.broadcast_to(m[h] + jnp.log(den[h]), (DIL_BLOCK, HEAD))

    return pl.pallas_call(
        body, out_shape=[jax.ShapeDtypeStruct(q.shape, F32)] * 2, grid=(d, nb),
        in_specs=[cur, prev, cur, prev, cur], out_specs=[cur, cur],
        compiler_params=_cparams("parallel", "parallel"), name=f"dil_fwd_d{d}")(q, k, k, v, v)


def _dil_bwd(q, k, v, o, lse, do, dlse, d):
    l = q.shape[0]
    nb = l // DIL_BLOCK
    tn = (((0,), (0,)), ((), ()))
    nt = (((1,), (1,)), ((), ()))
    cur = pl.BlockSpec((DIL_BLOCK, GROUP_W), lambda r, n: (jnp.minimum(n, nb - 1), r))
    prev = pl.BlockSpec((DIL_BLOCK, GROUP_W), lambda r, n: (jnp.maximum(jnp.minimum(n, nb - 1) - 1, 0), r))
    lag = pl.BlockSpec((DIL_BLOCK, GROUP_W), lambda r, n: (jnp.maximum(n - 1, 0), r))

    def body(q_ref, kp_ref, kc_ref, vp_ref, vc_ref, o_ref, lse_ref, do_ref, dl_ref, dq_ref, dk_ref, dv_ref, ck_sc, cv_sc):
        n = pl.program_id(1)

        @pl.when(n < nb)
        def _():
            heads = range(4)
            sl = [slice(h * HEAD, (h + 1) * HEAD) for h in heads]
            scores = [_dil_scores(q_ref[:, sl[h]], kp_ref[:, sl[h]], kc_ref[:, sl[h]], n) for h in heads]
            lse = [lse_ref[:, h * HEAD:h * HEAD + 1] for h in heads]
            pp = [jnp.where(scores[h][2], jnp.exp(scores[h][0] - lse[h]), 0.0) for h in heads]
            pc = [jnp.where(scores[h][3], jnp.exp(scores[h][1] - lse[h]), 0.0) for h in heads]
            do_b = [do_ref[:, sl[h]].astype(BF16) for h in heads]
            corr = [jnp.sum(dl_ref[:, sl[h]], axis=-1, keepdims=True)
                    - jnp.sum(do_ref[:, sl[h]] * o_ref[:, sl[h]], axis=-1, keepdims=True) for h in heads]
            dsp = [(pp[h] * (lax.dot_general(do_b[h], vp_ref[:, sl[h]], nt, preferred_element_type=F32) + corr[h])
                    * DIL_SCALE).astype(BF16) for h in heads]
            dsc = [(pc[h] * (lax.dot_general(do_b[h], vc_ref[:, sl[h]], nt, preferred_element_type=F32) + corr[h])
                    * DIL_SCALE).astype(BF16) for h in heads]
            dkp = [lax.dot_general(dsp[h], q_ref[:, sl[h]], tn, preferred_element_type=F32) for h in heads]
            dvp = [lax.dot_general(pp[h].astype(BF16), do_b[h], tn, preferred_element_type=F32) for h in heads]
            for h in heads:
                dq_ref[:, sl[h]] = (jnp.dot(dsp[h], kp_ref[:, sl[h]], preferred_element_type=F32)
                                    + jnp.dot(dsc[h], kc_ref[:, sl[h]], preferred_element_type=F32)).astype(dq_ref.dtype)

            @pl.when(n > 0)
            def _():
                for h in heads:
                    dk_ref[:, sl[h]] = (ck_sc[:, sl[h]] + dkp[h]).astype(dk_ref.dtype)
                    dv_ref[:, sl[h]] = (cv_sc[:, sl[h]] + dvp[h]).astype(dv_ref.dtype)

            for h in heads:
                ck_sc[:, sl[h]] = lax.dot_general(dsc[h], q_ref[:, sl[h]], tn, preferred_element_type=F32)
                cv_sc[:, sl[h]] = lax.dot_general(pc[h].astype(BF16), do_b[h], tn, preferred_element_type=F32)

        @pl.when(n == nb)
        def _():
            dk_ref[...] = ck_sc[...].astype(dk_ref.dtype)
            dv_ref[...] = cv_sc[...].astype(dv_ref.dtype)

    return pl.pallas_call(
        body, out_shape=[jax.ShapeDtypeStruct(q.shape, BF16)] * 3, grid=(d, nb + 1),
        in_specs=[cur, prev, cur, prev, cur, cur, cur, cur, cur], out_specs=[cur, lag, lag],
        scratch_shapes=[pltpu.VMEM((DIL_BLOCK, GROUP_W), F32)] * 2,
        compiler_params=_cparams("parallel", "arbitrary"), name=f"dil_bwd_d{d}")(q, k, k, v, v, o, lse, do, dlse)


def _make_dil(d):
    @jax.custom_vjp
    def att(q, k, v):
        return tuple(_dil_fwd(q, k, v, d))

    def fwd(q, k, v):
        o, lse = _dil_fwd(q, k, v, d)
        return (o, lse), (q, k, v, o, lse)

    def bwd(saved, cts):
        return tuple(_dil_bwd(*saved, cts[0], cts[1], d))

    att.defvjp(fwd, bwd)
    return att


def dilated_group(q, k, v, d):
    return _make_dil(d)(q, k, v)


def _dn_post(c, kind, scale):
    m = _silu(c)
    nrm = m * lax.rsqrt(jnp.sum(m * m, axis=-1, keepdims=True) + 1e-6) * scale
    return kind * nrm + (1.0 - kind) * m


def _dn_kind_scale(j):
    kind = jnp.where(j < 2 * DN_HEADS, 1.0, 0.0).astype(F32)
    scale = jnp.where(j < DN_HEADS, DN_DIM ** -0.5, 1.0).astype(F32)
    return kind, scale


_CONV_RB = 512
_CONV_PAD = 8


def _conv_windows(pad_ref, w_ref, r0, rb, sign):
    acc = None
    for sh in range(DN_CONV):
        win = pad_ref[pl.ds(r0 + _CONV_PAD * (sign < 0) + sign * sh, rb), :]
        term = w_ref[DN_CONV - 1 - sh:DN_CONV - sh, :] * win
        acc = term if acc is None else acc + term
    return acc


def _dn_conv_fwd(x, w):
    s, width = x.shape
    rb = min(_CONV_RB, s)

    def body(x_ref, w_ref, o_ref, pad_ref):
        kind, scale = _dn_kind_scale(pl.program_id(0))
        pad_ref[0:_CONV_PAD, :] = jnp.zeros((_CONV_PAD, LANE), F32)
        pad_ref[_CONV_PAD:, :] = x_ref[...]
        for r0 in range(0, s, rb):
            c = _conv_windows(pad_ref, w_ref, r0, rb, -1)
            o_ref[r0:r0 + rb, :] = _dn_post(c, kind, scale)

    return pl.pallas_call(
        body, out_shape=jax.ShapeDtypeStruct(x.shape, F32), grid=(width // LANE,),
        in_specs=[pl.BlockSpec((s, LANE), lambda j: (0, j)), pl.BlockSpec((DN_CONV, LANE), lambda j: (0, j))],
        out_specs=pl.BlockSpec((s, LANE), lambda j: (0, j)), scratch_shapes=[pltpu.VMEM((s + _CONV_PAD, LANE), F32)],
        compiler_params=_cparams("parallel"), name="dn_conv_fwd")(x, w)


def _dn_conv_bwd(x, w, dy):
    s, width = x.shape
    rb = min(_CONV_RB, s)

    def body(x_ref, w_ref, dy_ref, dx_ref, dw_ref, pad_ref, dpad_ref):
        kind, scale = _dn_kind_scale(pl.program_id(0))
        pad_ref[0:_CONV_PAD, :] = jnp.zeros((_CONV_PAD, LANE), F32)
        pad_ref[_CONV_PAD:, :] = x_ref[...]
        dpad_ref[s:, :] = jnp.zeros((_CONV_PAD, LANE), F32)
        dws = [jnp.zeros((1, LANE), F32) for _ in range(DN_CONV)]
        for r0 in range(0, s, rb):
            c = _conv_windows(pad_ref, w_ref, r0, rb, -1)
            _, vjp = jax.vjp(lambda cc: _dn_post(cc, kind, scale), c)
            dc = vjp(dy_ref[r0:r0 + rb, :])[0]
            dpad_ref[r0:r0 + rb, :] = dc
            for sh in range(DN_CONV):
                win = pad_ref[pl.ds(r0 + _CONV_PAD - sh, rb), :]
                dws[DN_CONV - 1 - sh] = dws[DN_CONV - 1 - sh] + jnp.sum(dc * win, axis=0, keepdims=True)
        for j in range(DN_CONV):
            dw_ref[j:j + 1, :] = dws[j]
        for r0 in range(0, s, rb):
            dx_ref[r0:r0 + rb, :] = _conv_windows(dpad_ref, w_ref, r0, rb, 1)

    return pl.pallas_call(
        body, out_shape=[jax.ShapeDtypeStruct(x.shape, F32), jax.ShapeDtypeStruct(w.shape, F32)], grid=(width // LANE,),
        in_specs=[pl.BlockSpec((s, LANE), lambda j: (0, j)), pl.BlockSpec((DN_CONV, LANE), lambda j: (0, j)),
                  pl.BlockSpec((s, LANE), lambda j: (0, j))],
        out_specs=[pl.BlockSpec((s, LANE), lambda j: (0, j)), pl.BlockSpec((DN_CONV, LANE), lambda j: (0, j))],
        scratch_shapes=[pltpu.VMEM((s + _CONV_PAD, LANE), F32), pltpu.VMEM((s + _CONV_PAD, LANE), F32)],
        compiler_params=_cparams("parallel"), name="dn_conv_bwd")(x, w, dy)


@jax.custom_vjp
def dn_conv(x, w):
    return _dn_conv_fwd(x, w)


dn_conv.defvjp(lambda x, w: (_dn_conv_fwd(x, w), (x, w)), lambda saved, dy: tuple(_dn_conv_bwd(*saved, dy)))


_NN = (((1,), (0,)), ((), ()))
_NT = (((1,), (1,)), ((), ()))
_TN = (((0,), (0,)), ((), ()))


def _bd(a, b, dims, passes=1):
    d = lambda x, y: lax.dot_general(x, y, dims, preferred_element_type=F32)
    ah, bh = a.astype(BF16), b.astype(BF16)
    if passes == 1:
        return d(ah, bh)
    al, bl = (a - ah.astype(F32)).astype(BF16), (b - bh.astype(F32)).astype(BF16)
    return d(ah, bh) + d(ah, bl) + d(al, bh)


@functools.partial(jax.custom_vjp, nondiff_argnums=(2, 3))
def _pdot(a, b, dims, passes):
    return _bd(a, b, dims, passes)


def _pdot_bwd(dims, passes, saved, g):
    a, b = saved
    if dims == _NN:
        return _bd(g, b, _NT, passes), _bd(a, g, _TN, passes)
    if dims == _NT:
        return _bd(g, b, _NN, passes), _bd(g, a, _TN, passes)
    return _bd(b, g, _NT, passes), _bd(a, g, _NN, passes)


_pdot.defvjp(lambda a, b, dims, passes: (_bd(a, b, dims, passes), (a, b)), _pdot_bwd)


GDN_DOT_PASSES = 1
GDN_SOLVE_PASSES = 3


def _hdot(a, b, dims=_NN):
    return _pdot(a, b, dims, GDN_DOT_PASSES)


def _xdot(a, b, dims=_NN):
    return _pdot(a, b, dims, GDN_SOLVE_PASSES)


def _split3(x):
    hi = x.astype(BF16)
    r1 = x - hi.astype(F32)
    mid = r1.astype(BF16)
    lo = (r1 - mid.astype(F32)).astype(BF16)
    return hi, mid, lo


def _tri_dot(tri, x, dims):
    t = tri.astype(BF16)
    return sum(lax.dot_general(t, p, dims, preferred_element_type=F32) for p in _split3(x))


@jax.custom_vjp
def _cumsum_rows(x):
    c = x.shape[0]
    tri = lax.broadcasted_iota(jnp.int32, (c, c), 0) >= lax.broadcasted_iota(jnp.int32, (c, c), 1)
    return _tri_dot(tri, x, _NN)


def _cumsum_rows_bwd(_, g):
    c = g.shape[0]
    tri = lax.broadcasted_iota(jnp.int32, (c, c), 0) >= lax.broadcasted_iota(jnp.int32, (c, c), 1)
    return (_tri_dot(tri, g, _TN),)


_cumsum_rows.defvjp(lambda x: (_cumsum_rows(x), None), _cumsum_rows_bwd)


@jax.custom_vjp
def _unit_lower_inverses(a):
    c = a[0].shape[0]
    eye = (lax.broadcasted_iota(jnp.int32, (c, c), 0) == lax.broadcasted_iota(jnp.int32, (c, c), 1)).astype(F32)
    pw = [-x for x in a]
    t = [eye + p for p in pw]
    for _ in range(int(math.log2(c)) - 1):
        pw = [_bd(p, p, _NN, GDN_SOLVE_PASSES) for p in pw]
        t = [x + _bd(x, p, _NN, GDN_SOLVE_PASSES) for x, p in zip(t, pw)]
    return t


def _unit_lower_inverses_bwd(t, g):
    left = [_bd(x, y, _TN, GDN_SOLVE_PASSES) for x, y in zip(t, g)]
    return ([-_bd(x, y, _NT, GDN_SOLVE_PASSES) for x, y in zip(left, t)],)


_unit_lower_inverses.defvjp(lambda a: (lambda t: (t, t))(_unit_lower_inverses(a)), _unit_lower_inverses_bwd)


@jax.custom_vjp
def _kept_inverses(a, t):
    return t


_kept_inverses.defvjp(lambda a, t: (t, t),
                      lambda t, g: (_unit_lower_inverses_bwd(t, g)[0], [jnp.zeros_like(x) for x in t]))


def _gdn_prep_fn(qkv, gb, kept=None, keep=None):
    c = GDN_CHUNK
    row = lax.broadcasted_iota(jnp.int32, (c, c), 0)
    col = lax.broadcasted_iota(jnp.int32, (c, c), 1)
    incl, strict = row >= col, row > col
    lane = _lanes(gb)
    heads = range(DN_HEADS)
    q, k, v = qkv[:DN_HEADS], qkv[DN_HEADS:2 * DN_HEADS], qkv[2 * DN_HEADS:]
    g = [jnp.sum(jnp.where(lane == h, gb, 0.0), axis=-1, keepdims=True) for h in heads]
    beta = [jnp.sum(jnp.where(lane == DN_HEADS + h, gb, 0.0), axis=-1, keepdims=True) for h in heads]
    gcb = [_cumsum_rows(jnp.broadcast_to(g[h], (c, c))) for h in heads]
    decay = [jnp.where(incl, jnp.exp(jnp.where(incl, gcb[h] - gcb[h].T, 0.0)), 0.0) for h in heads]
    kb = [k[h] * beta[h] for h in heads]
    a = [jnp.where(strict, _hdot(kb[h], k[h], _NT) * decay[h], 0.0) for h in heads]
    t = _unit_lower_inverses(a) if kept is None else _kept_inverses(a, kept)
    if keep is not None:
        keep.extend(t)
    eg = [jnp.exp(gcb[h]) for h in heads]
    u = [_xdot(t[h], v[h] * beta[h]) for h in heads]
    w = [_xdot(t[h], kb[h] * eg[h]) for h in heads]
    qk = [jnp.where(incl, _hdot(q[h], k[h], _NT) * decay[h], 0.0) for h in heads]
    g_last = [jnp.sum(jnp.where(row == c - 1, gcb[h], 0.0), axis=0, keepdims=True) for h in heads]
    kg = [k[h] * jnp.exp(g_last[h] - gcb[h]) for h in heads]
    qg = [q[h] * eg[h] for h in heads]
    el = [jnp.broadcast_to(jnp.exp(g_last[h]), (8, LANE)) for h in heads]
    return [u, w, qg, kg, qk, el]


def _gdn_prep_fwd(qkv, gb):
    s = qkv.shape[0]
    c = GDN_CHUNK
    n = s // c
    hw = DN_HEADS * LANE

    def body(qkv_ref, gb_ref, u_ref, w_ref, qg_ref, kg_ref, qk_ref, el_ref, t_ref):
        kept = []
        res = _gdn_prep_fn(_chunks(qkv_ref), gb_ref[...], keep=kept)
        for ref, chs in zip((u_ref, w_ref, qg_ref, kg_ref, qk_ref, el_ref, t_ref), res + [kept]):
            _store(ref, chs)

    row = pl.BlockSpec((c, hw), lambda i: (i, 0))
    return pl.pallas_call(
        body, out_shape=[jax.ShapeDtypeStruct((s, hw), F32)] * 5 + [jax.ShapeDtypeStruct((n * 8, hw), F32),
                                                                   jax.ShapeDtypeStruct((s, hw), F32)], grid=(n,),
        in_specs=[pl.BlockSpec((c, 3 * hw), lambda i: (i, 0)), pl.BlockSpec((c, LANE), lambda i: (i, 0))],
        out_specs=[row] * 5 + [pl.BlockSpec((8, hw), lambda i: (i, 0)), row],
        compiler_params=_cparams("parallel"), name="gdn_prep_fwd")(qkv, gb)


def _gdn_prep_bwd(qkv, gb, t, cts):
    s = qkv.shape[0]
    c = GDN_CHUNK
    n = s // c
    hw = DN_HEADS * LANE

    def body(qkv_ref, gb_ref, t_ref, du_ref, dw_ref, dqg_ref, dkg_ref, dqk_ref, del_ref, dqkv_ref, dgb_ref):
        kept = _chunks(t_ref)
        _, vjp = jax.vjp(lambda x, y: _gdn_prep_fn(x, y, kept=kept), _chunks(qkv_ref), gb_ref[...])
        ct = [_chunks(r) for r in (du_ref, dw_ref, dqg_ref, dkg_ref, dqk_ref, del_ref)]
        dqkv, dgb = vjp(ct)
        _store(dqkv_ref, dqkv)
        dgb_ref[...] = dgb

    row = pl.BlockSpec((c, hw), lambda i: (i, 0))
    return pl.pallas_call(
        body, out_shape=[jax.ShapeDtypeStruct(qkv.shape, F32), jax.ShapeDtypeStruct(gb.shape, F32)], grid=(n,),
        in_specs=[pl.BlockSpec((c, 3 * hw), lambda i: (i, 0)), pl.BlockSpec((c, LANE), lambda i: (i, 0)), row]
        + [row] * 5 + [pl.BlockSpec((8, hw), lambda i: (i, 0))],
        out_specs=[pl.BlockSpec((c, 3 * hw), lambda i: (i, 0)), pl.BlockSpec((c, LANE), lambda i: (i, 0))],
        compiler_params=_cparams("parallel"), name="gdn_prep_bwd")(qkv, gb, t, *cts)


@jax.custom_vjp
def gdn_prep(qkv, gb):
    return tuple(_gdn_prep_fwd(qkv, gb)[:6])


def _gdn_prep_vfwd(qkv, gb):
    res = _gdn_prep_fwd(qkv, gb)
    return tuple(res[:6]), (qkv, gb, res[6])


gdn_prep.defvjp(_gdn_prep_vfwd, lambda saved, cts: tuple(_gdn_prep_bwd(*saved, cts)))


def _gdn_scan_fwd(u, w, qg, kg, qk, el):
    s = u.shape[0]
    c = GDN_CHUNK
    n = s // c
    hw = DN_HEADS * LANE

    def body(u_ref, w_ref, qg_ref, kg_ref, qk_ref, el_ref, o_ref, st_ref, s_sc):
        @pl.when(pl.program_id(0) == 0)
        def _():
            s_sc[...] = jnp.zeros_like(s_sc)

        heads = range(DN_HEADS)
        sl = [slice(h * LANE, (h + 1) * LANE) for h in heads]
        st = [s_sc[h] for h in heads]
        for h in heads:
            st_ref[sl[h], :] = st[h]
        v_new = [u_ref[:, sl[h]] - _hdot(w_ref[:, sl[h]], st[h]) for h in heads]
        o_st = [_hdot(qg_ref[:, sl[h]], st[h]) for h in heads]
        o_in = [_hdot(qk_ref[:, sl[h]], v_new[h]) for h in heads]
        s_up = [_hdot(kg_ref[:, sl[h]], v_new[h], _TN) for h in heads]
        for h in heads:
            o_ref[:, sl[h]] = o_st[h] + o_in[h]
            s_sc[h] = st[h] * el_ref[0:1, sl[h]] + s_up[h]

    row = pl.BlockSpec((c, hw), lambda i: (i, 0))
    return pl.pallas_call(
        body, out_shape=[jax.ShapeDtypeStruct((s, hw), F32), jax.ShapeDtypeStruct((n, hw, LANE), F32)], grid=(n,),
        in_specs=[row] * 5 + [pl.BlockSpec((8, hw), lambda i: (i, 0))],
        out_specs=[row, pl.BlockSpec((None, hw, LANE), lambda i: (i, 0, 0))],
        scratch_shapes=[pltpu.VMEM((DN_HEADS, LANE, LANE), F32)],
        compiler_params=_cparams("arbitrary"), name="gdn_scan_fwd")(u, w, qg, kg, qk, el)


def _gdn_scan_bwd(u, w, qg, kg, qk, el, states, do):
    s = u.shape[0]
    c = GDN_CHUNK
    n = s // c
    hw = DN_HEADS * LANE

    def body(u_ref, w_ref, qg_ref, kg_ref, qk_ref, el_ref, st_ref, do_ref,
             du_ref, dw_ref, dqg_ref, dkg_ref, dqk_ref, del_ref, ds_sc):
        @pl.when(pl.program_id(0) == 0)
        def _():
            ds_sc[...] = jnp.zeros_like(ds_sc)

        heads = range(DN_HEADS)
        sl = [slice(h * LANE, (h + 1) * LANE) for h in heads]
        st = [st_ref[sl[h], :] for h in heads]
        ds = [ds_sc[h] for h in heads]
        do = [do_ref[:, sl[h]] for h in heads]
        v_new = [u_ref[:, sl[h]] - _hdot(w_ref[:, sl[h]], st[h]) for h in heads]
        dv_new = [_hdot(qk_ref[:, sl[h]], do[h], _TN) + _hdot(kg_ref[:, sl[h]], ds[h]) for h in heads]
        first_row = lax.broadcasted_iota(jnp.int32, (8, LANE), 0) == 0
        for h in heads:
            du_ref[:, sl[h]] = dv_new[h]
            dw_ref[:, sl[h]] = -_hdot(dv_new[h], st[h], _NT)
            dqg_ref[:, sl[h]] = _hdot(do[h], st[h], _NT)
            dqk_ref[:, sl[h]] = _hdot(do[h], v_new[h], _NT)
            dkg_ref[:, sl[h]] = _hdot(v_new[h], ds[h], _NT)
            del_ref[:, sl[h]] = jnp.where(first_row, jnp.sum(st[h] * ds[h], axis=0, keepdims=True), 0.0)
        ds_new = [_hdot(qg_ref[:, sl[h]], do[h], _TN) + ds[h] * el_ref[0:1, sl[h]] - _hdot(w_ref[:, sl[h]], dv_new[h], _TN)
                  for h in heads]
        for h in heads:
            ds_sc[h] = ds_new[h]

    row = pl.BlockSpec((c, hw), lambda i: (n - 1 - i, 0))
    small = pl.BlockSpec((8, hw), lambda i: (n - 1 - i, 0))
    return pl.pallas_call(
        body, out_shape=[jax.ShapeDtypeStruct((s, hw), F32)] * 5 + [jax.ShapeDtypeStruct((n * 8, hw), F32)], grid=(n,),
        in_specs=[row] * 5 + [small, pl.BlockSpec((None, hw, LANE), lambda i: (n - 1 - i, 0, 0)), row],
        out_specs=[row] * 5 + [small], scratch_shapes=[pltpu.VMEM((DN_HEADS, LANE, LANE), F32)],
        compiler_params=_cparams("arbitrary"), name="gdn_scan_bwd")(u, w, qg, kg, qk, el, states, do)


@jax.custom_vjp
def gdn_scan(u, w, qg, kg, qk, el):
    return _gdn_scan_fwd(u, w, qg, kg, qk, el)[0]


def _gdn_scan_vfwd(*args):
    o, states = _gdn_scan_fwd(*args)
    return o, args + (states,)


gdn_scan.defvjp(_gdn_scan_vfwd, lambda saved, do: tuple(_gdn_scan_bwd(*saved, do)))


def _loss_call(y, t):
    s, d = y.shape
    br = min(ROW_BLOCK_NARROW, s)
    n = s // br

    def body(y_ref, t_ref, loss_ref, dy_ref, acc):
        i = pl.program_id(0)

        @pl.when(i == 0)
        def _():
            acc[...] = jnp.zeros_like(acc)

        e = y_ref[...] - t_ref[...]
        dy_ref[...] = e / d
        acc[...] += jnp.sum(e * e, axis=0, keepdims=True)

        @pl.when(i == n - 1)
        def _():
            loss_ref[...] = jnp.broadcast_to(jnp.sum(acc[...], axis=1, keepdims=True) * (0.5 / d), loss_ref.shape)

    row = pl.BlockSpec((br, d), lambda i: (i, 0))
    return pl.pallas_call(
        body, out_shape=[jax.ShapeDtypeStruct((1, LANE), F32), jax.ShapeDtypeStruct((s, d), F32)], grid=(n,),
        in_specs=[row, row], out_specs=[pl.BlockSpec((1, LANE), lambda i: (0, 0)), row],
        scratch_shapes=[pltpu.VMEM((1, d), F32)], compiler_params=_cparams("arbitrary"), name="loss_head")(y, t)


@jax.custom_vjp
def loss_head(y, t):
    return _loss_call(y, t)[0][0, 0]


def _loss_head_fwd(y, t):
    loss, dy = _loss_call(y, t)
    return loss[0, 0], (dy,)


loss_head.defvjp(_loss_head_fwd, lambda saved, g: (saved[0] * g, -saved[0] * g))


IN_OFF = {}
_o = 0
for _name, _size in (('q_lat', 384), ('kv_lat', 320), ('z_a', 512), ('dn_qkv', 1536), ('dn_ab', 8), ('z_b', 512),
                     ('dil_qkv', 4608), ('z_c', 512), ('gate', 3072)):
    IN_OFF[_name] = (_o, _o + _size)
    _o += _size
IN_WIDTH = _o
N_CHIPS = 4


def _pad_cols(w, to):
    return jnp.concatenate([w, jnp.zeros((w.shape[0], to - w.shape[1]), w.dtype)], axis=1)


def _pad_row(v, to=None):
    v = v.reshape(1, -1)
    return v if to is None or v.shape[1] == to else _pad_cols(v, to)


def _shard_cols(pieces, a, b):
    wsh = pieces[0].shape[1]
    parts = [pieces[j][:, max(a, j * wsh) - j * wsh:min(b, (j + 1) * wsh) - j * wsh]
             for j in range(len(pieces)) if max(a, j * wsh) < min(b, (j + 1) * wsh)]
    return parts[0] if len(parts) == 1 else jnp.concatenate(parts, axis=1)


def _win_groups_impl(w_in4):
    out = []
    for l in range(w_in4.shape[1]):
        pieces = [w_in4[j, l] for j in range(w_in4.shape[0])]
        cols = lambda name: _shard_cols(pieces, *IN_OFF[name])
        out.append((_pad_cols(jnp.concatenate([cols('q_lat'), cols('kv_lat')], axis=1), 6 * LANE),
                    jnp.concatenate([cols('z_a'), cols('z_b'), cols('z_c')], axis=1),
                    cols('dn_qkv'), _pad_cols(cols('dn_ab'), LANE), cols('dil_qkv'), cols('gate')))
    return tuple(out)


@jax.custom_vjp
def win_groups(w_in4):
    return _win_groups_impl(w_in4)


def _win_groups_bwd(_, cts):
    n_chip, depth = N_CHIPS, len(cts)
    wsh = IN_WIDTH // n_chip
    per_layer = []
    for l in range(depth):
        dmla, dz, dgdn, dab, ddil, dgate = cts[l]
        width = lambda name: IN_OFF[name][1] - IN_OFF[name][0]
        n_lat, n_z, n_ab = width('q_lat') + width('kv_lat'), width('z_a'), width('dn_ab')
        full = jnp.concatenate([dmla[:, :n_lat], dz[:, :n_z], dgdn, dab[:, :n_ab], dz[:, n_z:2 * n_z], ddil, dz[:, 2 * n_z:],
                                dgate], axis=1)
        per_layer.append([full[:, j * wsh:(j + 1) * wsh] for j in range(n_chip)])
    return (jnp.stack([jnp.stack([per_layer[l][j] for l in range(depth)]) for j in range(n_chip)]),)


win_groups.defvjp(lambda w: (_win_groups_impl(w), None), _win_groups_bwd)


def _layer(x, p, tabs):
    w_mla, w_z, w_gdn, w_ab, w_dil, w_gate = p['w_in_groups']
    h, = rowwise(f_rms_full, "rms_in", [x], [_pad_row(p['norm_g'])], [(D_MODEL, BF16)], br=ROW_BLOCK_NARROW)
    mla_in, z, dn_qkv, dn_ab, dil_qkv, gate = matmul_shared(h, (w_mla, w_z, w_gdn, w_ab, w_dil, w_gate))

    qn, kvn, kpe = rowwise(f_mla1, "mla_norm", [mla_in], [_pad_row(p['mla_q_a_norm_g']), _pad_row(p['mla_kv_a_norm_g'])],
                           [(MLA_Q_RANK, BF16), (MLA_KV_RANK, BF16), (LANE, F32)], br=ROW_BLOCK_NARROW)
    wq = p['mla_w_q_b']
    wq_pad = jnp.concatenate(
        [wq[:, hh * MLA_QK:hh * MLA_QK + MLA_NOPE] for hh in range(MLA_HEADS)]
        + [_pad_cols(wq[:, hh * MLA_QK + MLA_NOPE:(hh + 1) * MLA_QK], LANE) for hh in range(MLA_HEADS)], axis=1)
    q = matmul(qn, wq_pad)
    kv = matmul(kvn, p['mla_w_kv_b'])
    gq, gk = p['mla_q_norm_g'], p['mla_k_norm_g']
    q_att, k_att, v_att = rowwise(
        f_mla2, "mla_qk", [q, kv, kpe, tabs['cos_r'], tabs['sin_r']],
        [_pad_row(gq[:MLA_NOPE]), _pad_row(gq[MLA_NOPE:], LANE), _pad_row(gk[:MLA_NOPE]), _pad_row(gk[MLA_NOPE:], LANE)],
        [(MLA_HEADS * MLA_DQK, BF16), (MLA_HEADS * MLA_DQK, BF16), (MLA_HEADS * HEAD, BF16)], nograd=(3, 4), br=ROW_BLOCK_NARROW)
    y_a = mla_attention(q_att, k_att, v_att)

    qkv_n = dn_conv(dn_qkv, p['dn_conv_w'])
    gb, = rowwise(f_gates, "dn_gates", [dn_ab], [_pad_row(p['dn_a_log'], LANE), _pad_row(p['dn_dt_bias'], LANE)], [(LANE, F32)],
                  br=ROW_BLOCK_NARROW)
    o_b = gdn_scan(*gdn_prep(qkv_n, gb))
    y_b, = rowwise(f_headnorm, "dn_out_norm", [o_b], [_pad_row(p['dn_out_norm_g'])], [(DN_HEADS * DN_DIM, F32)],
                   br=ROW_BLOCK_NARROW)

    qkv_d = rowwise(f_dil, "dil_qk", [dil_qkv, tabs['cos_h'], tabs['sin_h']],
                    [_pad_row(p['dil_q_norm_g']), _pad_row(p['dil_k_norm_g'])],
                    [(GROUP_W, BF16)] * (3 * DIL_GROUPS), nograd=(1, 2), out_dil=[d for d in DIL_DILATIONS for _ in range(3)])
    outs, lses = [], []
    for gi, d in enumerate(DIL_DILATIONS):
        o_g, l_g = dilated_group(*qkv_d[3 * gi:3 * gi + 3], d)
        outs.append(o_g)
        lses.append(l_g)
    y_c, = rowwise(f_comb, "dil_comb", outs + lses, [], [(GROUP_W, F32)], row_dil=list(DIL_DILATIONS) * 2)

    ys = rowwise(f_merge1, "merge_silu", [y_a, y_b, y_c, z], [], [(BRANCH_W, BF16)] * 3, br=ROW_BLOCK_NARROW)
    bo = [matmul(ys[b], p['w_branch'][b]) for b in range(3)]
    mixed, = rowwise(f_merge2, "merge_gate", [gate] + bo, [], [(D_MODEL, BF16)])
    return matmul(mixed, p['w_out'], res=x)


def _rope_tables(pos, dim):
    inv_freq = 1.0 / (ROPE_THETA ** (jnp.arange(0, dim, 2, dtype=F32) / dim))
    ang = pos.astype(F32)[:, None] * inv_freq
    return jnp.cos(ang), jnp.sin(ang)


def _tables(pos):
    cr, sr = _rope_tables(pos, MLA_ROPE)
    ch, sh = _rope_tables(pos, HEAD)
    zero = jnp.zeros((pos.shape[0], LANE - MLA_ROPE), F32)
    return {'cos_r': jnp.concatenate([cr, cr, zero], axis=1), 'sin_r': jnp.concatenate([sr, sr, zero], axis=1),
            'cos_h': jnp.concatenate([ch, ch], axis=1), 'sin_h': jnp.concatenate([-sh, sh], axis=1)}


def _local_loss(w_in4, mats, conv_w, small, x, target, tabs):
    groups = win_groups(w_in4)
    for l in range(DEPTH):
        p = {k: v[l] for k, v in mats.items()}
        p.update({k: v[l] for k, v in small.items()})
        p['dn_conv_w'] = conv_w[l]
        p['w_in_groups'] = groups[l]
        x = _layer(x, p, tabs)
    return loss_head(x, target)


def _pack(arrays, dtype, row_tile):
    flat = jnp.concatenate([a.astype(dtype).reshape(-1) for a in arrays])
    rows = -(-flat.shape[0] // (LANE * row_tile)) * row_tile
    flat = jnp.concatenate([flat, jnp.zeros((rows * LANE - flat.shape[0],), dtype)])
    return flat.reshape(rows, LANE)


def _unpack_impl(buf, shapes):
    flat = buf.reshape(-1)
    out, off = [], 0
    for shp in shapes:
        n = math.prod(shp)
        out.append(flat[off:off + n].reshape(shp))
        off += n
    return tuple(out)


@functools.partial(jax.custom_vjp, nondiff_argnums=(1, 2, 3))
def _unpack_p(buf, shapes, dtype_name, rows):
    return _unpack_impl(buf, shapes)


_unpack_p.defvjp(lambda buf, shapes, dtype_name, rows: (_unpack_impl(buf, shapes), None),
                 lambda shapes, dtype_name, rows, _, cts: (_pack(cts, jnp.dtype(dtype_name), rows),))


def _unpack(buf, shapes):
    return _unpack_p(buf, tuple(shapes), jnp.dtype(buf.dtype).name, buf.shape[0])


def _full_from_chips(buf4, shard_shapes):
    per_chip = [_unpack(buf4[j], tuple(shard_shapes)) for j in range(4)]
    return {name: jnp.concatenate([per_chip[j][i] for j in range(4)], axis=axis) for i, (name, axis) in enumerate(MATS)}


_HBM = pl.BlockSpec(memory_space=pltpu.HBM)
_VMEM = pl.BlockSpec(memory_space=pltpu.VMEM)


def _chip_peers(x, y):
    return [(1 - x, y), (x, 1 - y), (1 - x, 1 - y)]


def chip_all_to_all(arrays):
    n = len(arrays)

    def body(*refs):
        in_refs, out_refs = refs[:n], refs[n:2 * n]
        send_sems, recv_sems, local_sems = refs[2 * n:]
        x, y, c = lax.axis_index("x"), lax.axis_index("y"), lax.axis_index("c")
        me = 2 * x + y
        peers = _chip_peers(x, y)
        local, sends = [], []
        for a, (i_ref, o_ref) in enumerate(zip(in_refs, out_refs)):
            local.append(pltpu.make_async_copy(i_ref.at[me], o_ref.at[me], local_sems.at[a]))
            local[-1].start()
            for k, (px, py) in enumerate(peers):
                sends.append(pltpu.make_async_remote_copy(
                    src_ref=i_ref.at[2 * px + py], dst_ref=o_ref.at[me], send_sem=send_sems.at[3 * a + k],
                    recv_sem=recv_sems.at[3 * a + k], device_id=(px, py, c), device_id_type=MESH))
                sends[-1].start()
        for a, (i_ref, o_ref) in enumerate(zip(in_refs, out_refs)):
            for k, (px, py) in enumerate(peers):
                pltpu.make_async_remote_copy(
                    src_ref=i_ref.at[me], dst_ref=o_ref.at[2 * px + py], send_sem=send_sems.at[3 * a + k],
                    recv_sem=recv_sems.at[3 * a + k], device_id=(px, py, c), device_id_type=MESH).wait_recv()
        for cp in sends:
            cp.wait_send()
        for cp in local:
            cp.wait()

    return pl.pallas_call(
        body, out_shape=[jax.ShapeDtypeStruct(a.shape, a.dtype) for a in arrays], in_specs=[_HBM] * n, out_specs=[_HBM] * n,
        scratch_shapes=[pltpu.SemaphoreType.DMA((3 * n,)), pltpu.SemaphoreType.DMA((3 * n,)), pltpu.SemaphoreType.DMA((n,))],
        name="chip_all_to_all")(*arrays)


def chip_all_gather(shards):
    n = len(shards)

    def body(*refs):
        in_refs, out_refs = refs[:n], refs[n:2 * n]
        send_sems, recv_sems = refs[2 * n:]
        x, y, c = lax.axis_index("x"), lax.axis_index("y"), lax.axis_index("c")
        me = 2 * x + y
        peers = _chip_peers(x, y)
        sends = []
        for a, (i_ref, o_ref) in enumerate(zip(in_refs, out_refs)):
            for k, (px, py) in enumerate(peers):
                sends.append(pltpu.make_async_remote_copy(
                    src_ref=i_ref.at[c], dst_ref=o_ref.at[me, c], send_sem=send_sems.at[6 * a + k],
                    recv_sem=recv_sems.at[6 * a + k], device_id=(px, py, c), device_id_type=MESH))
                sends[-1].start()
        for a, (i_ref, o_ref) in enumerate(zip(in_refs, out_refs)):
            for k, (px, py) in enumerate(peers):
                landed = o_ref.at[2 * px + py, c]
                pltpu.make_async_remote_copy(src_ref=i_ref.at[c], dst_ref=landed, send_sem=send_sems.at[6 * a + k],
                                             recv_sem=recv_sems.at[6 * a + k], device_id=(px, py, c), device_id_type=MESH).wait_recv()
                sends.append(pltpu.make_async_remote_copy(
                    src_ref=landed, dst_ref=landed, send_sem=send_sems.at[6 * a + 3 + k], recv_sem=recv_sems.at[6 * a + 3 + k],
                    device_id=(x, y, 1 - c), device_id_type=MESH))
                sends[-1].start()
        for a, (i_ref, o_ref) in enumerate(zip(in_refs, out_refs)):
            for k, (px, py) in enumerate(peers):
                other = o_ref.at[2 * px + py, 1 - c]
                pltpu.make_async_remote_copy(src_ref=other, dst_ref=other, send_sem=send_sems.at[6 * a + 3 + k],
                                             recv_sem=recv_sems.at[6 * a + 3 + k], device_id=(x, y, 1 - c),
                                             device_id_type=MESH).wait_recv()
        for cp in sends:
            cp.wait_send()

    me = 2 * lax.axis_index("x") + lax.axis_index("y")
    outs = pl.pallas_call(
        body, out_shape=[jax.ShapeDtypeStruct((4,) + a.shape, a.dtype) for a in shards], in_specs=[_HBM] * n, out_specs=[_HBM] * n,
        scratch_shapes=[pltpu.SemaphoreType.DMA((6 * n,)), pltpu.SemaphoreType.DMA((6 * n,))],
        name="chip_all_gather")(*shards)
    return [lax.dynamic_update_index_in_dim(o, a, me, 0) for o, a in zip(outs, shards)]


def sibling_swap(arrays):
    n = len(arrays)

    def body(*refs):
        in_refs, out_refs = refs[:n], refs[n:2 * n]
        send_sems, recv_sems = refs[2 * n:]
        x, y, c = lax.axis_index("x"), lax.axis_index("y"), lax.axis_index("c")
        cps = [pltpu.make_async_remote_copy(src_ref=i_ref, dst_ref=o_ref, send_sem=send_sems.at[a], recv_sem=recv_sems.at[a],
                                            device_id=(x, y, 1 - c), device_id_type=MESH)
               for a, (i_ref, o_ref) in enumerate(zip(in_refs, out_refs))]
        for cp in cps:
            cp.start()
        for cp in cps:
            cp.wait()

    return pl.pallas_call(
        body, out_shape=[jax.ShapeDtypeStruct(a.shape, a.dtype) for a in arrays], in_specs=[_HBM] * n, out_specs=[_HBM] * n,
        scratch_shapes=[pltpu.SemaphoreType.DMA((n,)), pltpu.SemaphoreType.DMA((n,))], name="sibling_swap")(*arrays)


def sibling_all_gather(halves):
    n = len(halves)

    def body(*refs):
        in_refs, out_refs = refs[:n], refs[n:2 * n]
        send_sems, recv_sems = refs[2 * n:]
        x, y, c = lax.axis_index("x"), lax.axis_index("y"), lax.axis_index("c")
        sends = [pltpu.make_async_remote_copy(src_ref=i_ref, dst_ref=o_ref.at[c], send_sem=send_sems.at[a], recv_sem=recv_sems.at[a],
                                              device_id=(x, y, 1 - c), device_id_type=MESH)
                 for a, (i_ref, o_ref) in enumerate(zip(in_refs, out_refs))]
        for cp in sends:
            cp.start()
        for a, (i_ref, o_ref) in enumerate(zip(in_refs, out_refs)):
            pltpu.make_async_remote_copy(src_ref=i_ref, dst_ref=o_ref.at[1 - c], send_sem=send_sems.at[a], recv_sem=recv_sems.at[a],
                                         device_id=(x, y, 1 - c), device_id_type=MESH).wait_recv()
        for cp in sends:
            cp.wait_send()

    outs = pl.pallas_call(
        body, out_shape=[jax.ShapeDtypeStruct((2,) + a.shape, a.dtype) for a in halves], in_specs=[_HBM] * n, out_specs=[_HBM] * n,
        scratch_shapes=[pltpu.SemaphoreType.DMA((n,)), pltpu.SemaphoreType.DMA((n,))], name="sibling_all_gather")(*halves)
    return [lax.dynamic_update_index_in_dim(o, a, lax.axis_index("c"), 0) for o, a in zip(outs, halves)]


def all_gather8(v, name):
    def body(v_ref, out_ref, send_sems, recv_sems):
        x, y, c = lax.axis_index("x"), lax.axis_index("y"), lax.axis_index("c")
        out_ref[4 * x + 2 * y + c] = v_ref[...]

        def peer(k):
            return (x ^ (k >> 2), y ^ ((k >> 1) & 1), c ^ (k & 1))

        sends = [pltpu.make_async_remote_copy(src_ref=v_ref, dst_ref=out_ref.at[4 * x + 2 * y + c], send_sem=send_sems.at[k - 1],
                                              recv_sem=recv_sems.at[k - 1], device_id=peer(k), device_id_type=MESH)
                 for k in range(1, 8)]
        for cp in sends:
            cp.start()
        for k in range(1, 8):
            px, py, pc = peer(k)
            pltpu.make_async_remote_copy(src_ref=v_ref, dst_ref=out_ref.at[4 * px + 2 * py + pc], send_sem=send_sems.at[k - 1],
                                         recv_sem=recv_sems.at[k - 1], device_id=peer(k), device_id_type=MESH).wait_recv()
        for cp in sends:
            cp.wait_send()

    return pl.pallas_call(
        body, out_shape=jax.ShapeDtypeStruct((8,) + v.shape, v.dtype), in_specs=[_VMEM], out_specs=_VMEM,
        scratch_shapes=[pltpu.SemaphoreType.DMA((7,)), pltpu.SemaphoreType.DMA((7,))], name=name)(v)


def pair_add(a, b, row_tile, name):
    rows, width = a.shape

    def body(a_ref, b_ref, o_ref):
        o_ref[...] = (a_ref[...].astype(F32) + b_ref[...].astype(F32)).astype(o_ref.dtype)

    spec = pl.BlockSpec((row_tile, width), lambda i: (i, 0))
    return pl.pallas_call(body, out_shape=jax.ShapeDtypeStruct(a.shape, a.dtype), grid=(rows // row_tile,),
                          in_specs=[spec, spec], out_specs=spec, compiler_params=_cparams("parallel"), name=name)(a, b)


def sum_blocks(blocks, row_tile, name):
    n, rows, width = blocks.shape

    def body(b_ref, o_ref):
        acc = b_ref[0].astype(F32)
        for j in range(1, n):
            acc = acc + b_ref[j].astype(F32)
        o_ref[...] = acc

    return pl.pallas_call(
        body, out_shape=jax.ShapeDtypeStruct((rows, width), F32), grid=(rows // row_tile,),
        in_specs=[pl.BlockSpec((n, row_tile, width), lambda i: (0, i, 0))],
        out_specs=pl.BlockSpec((row_tile, width), lambda i: (i, 0)), compiler_params=_cparams("parallel"), name=name)(blocks)


def adamw(g_parts, w, m, v, row_tile, name):
    npart = len(g_parts)

    def body(*refs):
        g = refs[0][...]
        for r in refs[1:npart]:
            g = g + r[...]
        w_ref, m_ref, v_ref, g_out, d_out, m_out, v_out = refs[npart:]
        m_new = ADAM_B1 * m_ref[...] + (1.0 - ADAM_B1) * g
        v_new = ADAM_B2 * v_ref[...] + (1.0 - ADAM_B2) * (g * g)
        m_hat = m_new / (1.0 - ADAM_B1 ** ADAM_STEP)
        v_hat = v_new / (1.0 - ADAM_B2 ** ADAM_STEP)
        g_out[...] = g
        d_out[...] = -ADAM_LR * (m_hat / (jnp.sqrt(v_hat) + ADAM_EPS) + ADAM_WD * w_ref[...])
        m_out[...] = m_new
        v_out[...] = v_new

    rows, width = w.shape
    spec = pl.BlockSpec((row_tile, width), lambda i: (i, 0))
    return pl.pallas_call(
        body, out_shape=[jax.ShapeDtypeStruct(w.shape, F32)] * 4, grid=(rows // row_tile,),
        in_specs=[spec] * (npart + 3), out_specs=[spec] * 4, compiler_params=_cparams("parallel"), name=name)(*g_parts, w, m, v)


def kernel(x, positions, norm_g, w_in, mla_q_a_norm_g, mla_w_q_b, mla_kv_a_norm_g, mla_w_kv_b, mla_q_norm_g, mla_k_norm_g, dn_conv_w, dn_a_log, dn_dt_bias, dn_out_norm_g, dil_q_norm_g, dil_k_norm_g, w_branch, w_out, loss_target, m_norm_g, m_w_in, m_mla_q_a_norm_g, m_mla_w_q_b, m_mla_kv_a_norm_g, m_mla_w_kv_b, m_mla_q_norm_g, m_mla_k_norm_g, m_dn_conv_w, m_dn_a_log, m_dn_dt_bias, m_dn_out_norm_g, m_dil_q_norm_g, m_dil_k_norm_g, m_w_branch, m_w_out, v_norm_g, v_w_in, v_mla_q_a_norm_g, v_mla_w_q_b, v_mla_kv_a_norm_g, v_mla_w_kv_b, v_mla_q_norm_g, v_mla_k_norm_g, v_dn_conv_w, v_dn_a_log, v_dn_dt_bias, v_dn_out_norm_g, v_dil_q_norm_g, v_dil_k_norm_g, v_w_branch, v_w_out):
    w = dict(norm_g=norm_g, w_in=w_in, mla_q_a_norm_g=mla_q_a_norm_g, mla_w_q_b=mla_w_q_b, mla_kv_a_norm_g=mla_kv_a_norm_g,
             mla_w_kv_b=mla_w_kv_b, mla_q_norm_g=mla_q_norm_g, mla_k_norm_g=mla_k_norm_g, dn_conv_w=dn_conv_w, dn_a_log=dn_a_log,
             dn_dt_bias=dn_dt_bias, dn_out_norm_g=dn_out_norm_g, dil_q_norm_g=dil_q_norm_g, dil_k_norm_g=dil_k_norm_g,
             w_branch=w_branch, w_out=w_out)
    m = dict(norm_g=m_norm_g, w_in=m_w_in, mla_q_a_norm_g=m_mla_q_a_norm_g, mla_w_q_b=m_mla_w_q_b, mla_kv_a_norm_g=m_mla_kv_a_norm_g,
             mla_w_kv_b=m_mla_w_kv_b, mla_q_norm_g=m_mla_q_norm_g, mla_k_norm_g=m_mla_k_norm_g, dn_conv_w=m_dn_conv_w,
             dn_a_log=m_dn_a_log, dn_dt_bias=m_dn_dt_bias, dn_out_norm_g=m_dn_out_norm_g, dil_q_norm_g=m_dil_q_norm_g,
             dil_k_norm_g=m_dil_k_norm_g, w_branch=m_w_branch, w_out=m_w_out)
    v = dict(norm_g=v_norm_g, w_in=v_w_in, mla_q_a_norm_g=v_mla_q_a_norm_g, mla_w_q_b=v_mla_w_q_b, mla_kv_a_norm_g=v_mla_kv_a_norm_g,
             mla_w_kv_b=v_mla_w_kv_b, mla_q_norm_g=v_mla_q_norm_g, mla_k_norm_g=v_mla_k_norm_g, dn_conv_w=v_dn_conv_w,
             dn_a_log=v_dn_a_log, dn_dt_bias=v_dn_dt_bias, dn_out_norm_g=v_dn_out_norm_g, dil_q_norm_g=v_dil_q_norm_g,
             dil_k_norm_g=v_dil_k_norm_g, w_branch=v_w_branch, w_out=v_w_out)
    chip = 2 * lax.axis_index("x") + lax.axis_index("y")
    mat_names = [n for n, _ in MATS]
    mat_shapes = tuple(w[n].shape for n in mat_names)
    conv_shard = dn_conv_w.shape
    win_shape = w_in.shape
    win_rows = (win_shape[0] * win_shape[1], win_shape[2])

    mats_sh = _pack([w[n] for n in mat_names], BF16, 2 * MAT_ROWS)
    mat_rows = mats_sh.shape[0]
    w_in4, mats4, conv4 = chip_all_gather([w_in.astype(BF16), mats_sh.reshape(2, mat_rows // 2, LANE), dn_conv_w])
    mats4 = mats4.reshape(N_CHIPS, mat_rows, LANE)
    conv_full = jnp.concatenate([conv4[j] for j in range(N_CHIPS)], axis=2)
    small = {n: w[n] for n in SMALL}
    tabs = _tables(positions[0])

    def loss_fn(w_in4, mats4, conv_full, small, xs):
        return _local_loss(w_in4, _full_from_chips(mats4, mat_shapes), conv_full, small, xs, loss_target[0], tabs)

    loss, (g_win4, g_mats4, g_conv, g_small, g_x) = jax.value_and_grad(loss_fn, argnums=(0, 1, 2, 3, 4))(
        w_in4, mats4, conv_full, small, x[0])
    loss = lax.psum(loss, ("x", "y", "c"))

    c_idx = lax.axis_index("c")
    half_win = (N_CHIPS * win_shape[1], win_shape[2])
    half_mats = (N_CHIPS * (mat_rows // 2), LANE)
    g_mats4 = g_mats4.reshape(N_CHIPS, 2, mat_rows // 2, LANE)
    pick = lambda g, i, shape: lax.dynamic_index_in_dim(g, i, axis=1, keepdims=False).reshape(shape)
    from_sib = sibling_swap([pick(g_win4, 1 - c_idx, half_win), pick(g_mats4, 1 - c_idx, half_mats)])
    s_win = pair_add(pick(g_win4, c_idx, half_win), from_sib[0], WIN_ROWS, "pair_add_w_in")
    s_mats = pair_add(pick(g_mats4, c_idx, half_mats), from_sib[1], MAT_ROWS, "pair_add_mats")
    r_win, r_mats = chip_all_to_all([s_win.reshape(N_CHIPS, win_shape[1], win_shape[2]),
                                     s_mats.reshape(N_CHIPS, mat_rows // 2, LANE)])
    g_win, g_mats = sibling_all_gather([sum_blocks(r_win, WIN_ROWS, "sum_chip_pieces_w_in"),
                                        sum_blocks(r_mats, MAT_ROWS, "sum_chip_pieces_mats")])
    res_win = adamw([g_win.reshape(win_rows)], *[d['w_in'].reshape(win_rows) for d in (w, m, v)], WIN_ROWS, "adamw_w_in")
    packed = [_pack([d[n] for n in mat_names], F32, 2 * MAT_ROWS) for d in (w, m, v)]
    res_big = [dict(zip(mat_names, _unpack(r, mat_shapes)))
               for r in adamw([g_mats.reshape(mat_rows, LANE)], *packed, MAT_ROWS, "adamw_mats")]
    for k in range(4):
        res_big[k]['w_in'] = res_win[k].reshape(win_shape)

    small_shapes = tuple(w[n].shape for n in SMALL) + (g_conv.shape,)
    g_all = sum_blocks(all_gather8(_pack([g_small[n] for n in SMALL] + [g_conv], F32, 8), "gather_small_grads"), 8, "sum_small")
    g_list = list(_unpack(g_all, small_shapes))
    g_list[-1] = lax.dynamic_slice_in_dim(g_list[-1], chip * conv_shard[2], conv_shard[2], axis=2)
    small_names = list(SMALL) + ['dn_conv_w']
    packed_s = [_pack([d[n] for n in small_names], F32, 8) for d in (w, m, v)]
    shapes_s = tuple(w[n].shape for n in small_names)
    res_small = [dict(zip(small_names, _unpack(r, shapes_s))) for r in adamw([_pack(g_list, F32, 8)], *packed_s, 8, "adamw_small")]

    outs = [loss, g_x[None]]
    for k in range(4):
        outs += [res_big[k][n] if n in res_big[k] else res_small[k][n] for n in WEIGHTS]
    return tuple(outs)
```
